```python
import jax, jax.numpy as jnp
from jax import lax
import numpy as np

D_MODEL = 1024
BATCH = 16
SEQ = 2048
DEPTH = 1

CTX_LEN = 256
GRID_W = 64
D_MIX = D_MODEL
LRU_W = D_MIX // 2
LRU_BLOCKS = 8
LRU_BD = LRU_W // LRU_BLOCKS
CONV_W = 4
CONV_PAD_L = 1
LRU_C = 8.0
HG_W = D_MIX - LRU_W
HG_HEADS = 4
HG_HD = HG_W // HG_HEADS
HG_CHUNK = 32
N_EXPERTS = 256
TOP_K = 8
N_GROUPS = 8
TOPK_GROUPS = 4
D_EXPERT = 256
D_SHARED = 256
ROUTE_SCALE = 2.5
MOE_BLOCK = 128
EPS = 1e-6
IN_COLS = 2 * LRU_W + 5 * HG_W
IN_SPLITS = [LRU_W, 2 * LRU_W, 2 * LRU_W + HG_W, 2 * LRU_W + 2 * HG_W, 2 * LRU_W + 3 * HG_W, 2 * LRU_W + 4 * HG_W]

kernel_name = 'hybrid_rglru_hgrn2_moe_dit_block'


def _rmsnorm(x, gain):
    xf = x.astype(jnp.float32)
    y = xf * lax.rsqrt(jnp.mean(xf * xf, axis=-1, keepdims=True) + EPS)
    return (y * gain.astype(jnp.float32)).astype(x.dtype)


def _to_colmajor(t):
    b, n, ch = t.shape
    rows = n // GRID_W
    return t.reshape(b, rows, GRID_W, ch).transpose(0, 2, 1, 3).reshape(b, n, ch)


def _from_colmajor(t):
    b, n, ch = t.shape
    rows = n // GRID_W
    return t.reshape(b, GRID_W, rows, ch).transpose(0, 2, 1, 3).reshape(b, n, ch)


def _dwconv_centred(t, w, bias):
    n = t.shape[1]
    tp = jnp.pad(t, ((0, 0), (CONV_PAD_L, CONV_W - 1 - CONV_PAD_L), (0, 0)))
    out = bias
    for k in range(CONV_W):
        out = out + tp[:, k:k + n] * w[k]
    return out


def _rglru_coeffs(u, wa, ba, wx, bx, lam):
    blocks = u.reshape(u.shape[:-1] + (LRU_BLOCKS, LRU_BD))
    r = jax.nn.sigmoid(jnp.einsum('blhi,hij->blhj', blocks, wa).reshape(u.shape) + ba)
    i = jax.nn.sigmoid(jnp.einsum('blhi,hij->blhj', blocks, wx).reshape(u.shape) + bx)
    log_a = -LRU_C * r * jax.nn.softplus(-lam)
    a = jnp.exp(log_a)
    b = jnp.sqrt(-jnp.expm1(2.0 * log_a)) * (i * u)
    return a, b


def _linear_scan(a, b, h0, reverse):
    if h0 is not None:
        first = b.shape[1] - 1 if reverse else 0
        b = b.at[:, first].add(a[:, first] * h0)

    def combine(e1, e2):
        a1, b1 = e1
        a2, b2 = e2
        return a1 * a2, a2 * b1 + b2

    _, h = lax.associative_scan(combine, (a, b), reverse=reverse, axis=1)
    return h


def _hgrn2_chunkwise(q, k, v, logf, s0):
    b, n, h, d = q.shape
    nc = n // HG_CHUNK

    def split(t):
        return t.reshape(b, nc, HG_CHUNK, h, d).transpose(1, 0, 3, 2, 4)

    q, k, v, logf = split(q), split(k), split(v), split(logf)
    g = jnp.cumsum(logf, axis=3)
    g_last = g[:, :, :, -1:, :]
    q_dec = q * jnp.exp(g)
    tri = jnp.tril(jnp.ones((HG_CHUNK, HG_CHUNK), dtype=bool))
    scores = jnp.einsum('nbhid,nbhjd->nbhij', q_dec, k * jnp.exp(-g))
    scores = jnp.where(tri, scores, 0.0)
    o_intra = jnp.einsum('nbhij,nbhje->nbhie', scores, v)
    kv = jnp.einsum('nbhjd,nbhje->nbhde', k * jnp.exp(g_last - g), v)
    decay = jnp.exp(g_last[:, :, :, 0, :])

    def step(s, inp):
        dec, kv_c = inp
        return dec[..., None] * s + kv_c, s

    s_final, s_prev = lax.scan(step, s0, (decay, kv))
    o = o_intra + jnp.einsum('nbhid,nbhde->nbhie', q_dec, s_prev)
    o = o.transpose(1, 0, 3, 2, 4).reshape(b, n, h, d)
    return o, s_final


def _hgrn2_direction(q, v, zf, lb, s0, reverse):
    b, n, _ = q.shape
    f32 = jnp.float32
    zf = zf.astype(f32)
    logf = jnp.log(lb + (1.0 - lb) * jax.nn.sigmoid(zf))
    k = (1.0 - lb) * jax.nn.sigmoid(-zf)

    def heads(t):
        return t.astype(f32).reshape(b, n, HG_HEADS, HG_HD)

    q, k, v, logf = heads(q), heads(k), heads(v), heads(logf)
    if reverse:
        q, k, v, logf = (jnp.flip(t, axis=1) for t in (q, k, v, logf))
    if s0 is None:
        s0 = jnp.zeros((b, HG_HEADS, HG_HD, HG_HD), f32)
    o, s = _hgrn2_chunkwise(q, k, v, logf, s0)
    if reverse:
        o = jnp.flip(o, axis=1)
    return o.reshape(b, n, HG_W), s


def _head_rmsnorm(o, gain):
    b, n, _ = o.shape
    oh = o.reshape(b, n, HG_HEADS, HG_HD)
    oh = oh * lax.rsqrt(jnp.mean(oh * oh, axis=-1, keepdims=True) + EPS)
    return oh.reshape(b, n, HG_W) * gain.astype(jnp.float32)


def _hybrid_mixer(hx, hc, w_in, w_out, conv_w, conv_b, wa, ba, wx, bx, lam, lb, head_gain, need_ctx):
    f32 = jnp.float32
    rx, rgx, qx, vx, fx_f, fx_b, gx = jnp.split(hx @ w_in, IN_SPLITS, axis=-1)
    rc, rgc, qc, vc, fc_f, fc_b, gc = jnp.split(hc @ w_in, IN_SPLITS, axis=-1)

    ul = _dwconv_centred(rx, conv_w, conv_b).astype(f32)
    uc = _dwconv_centred(rc, conv_w, conv_b).astype(f32)
    lru_l = jnp.zeros_like(ul)
    lru_c = jnp.zeros_like(uc)
    for d, rev in enumerate((False, True)):
        a, bb = _rglru_coeffs(uc, wa[d], ba[d], wx[d], bx[d], lam[d])
        h_c = _linear_scan(a, bb, None, rev)
        h0 = h_c[:, 0] if rev else h_c[:, -1]
        a, bb = _rglru_coeffs(ul, wa[d], ba[d], wx[d], bx[d], lam[d])
        lru_l = lru_l + _linear_scan(a, bb, h0, rev)
        lru_c = lru_c + h_c

    ql, vl, zl_f, zl_b = (_to_colmajor(t) for t in (qx, vx, fx_f, fx_b))
    hg_l = 0.0
    hg_c = 0.0
    for d, (zl, zc, rev) in enumerate(((zl_f, fc_f, False), (zl_b, fc_b, True))):
        o_c, s_c = _hgrn2_direction(qc, vc, zc, lb[d], None, rev)
        o_l, _ = _hgrn2_direction(ql, vl, zl, lb[d], s_c, rev)
        hg_l = hg_l + o_l
        hg_c = hg_c + o_c
    hg_l = _from_colmajor(hg_l)

    y_l = jnp.concatenate([jax.nn.gelu(rgx.astype(f32)) * lru_l,
                           _head_rmsnorm(hg_l, head_gain) * jax.nn.silu(gx.astype(f32))], axis=-1)
    y_l = y_l.astype(hx.dtype) @ w_out
    if not need_ctx:
        return y_l, None
    y_c = jnp.concatenate([jax.nn.gelu(rgc.astype(f32)) * lru_c,
                           _head_rmsnorm(hg_c, head_gain) * jax.nn.silu(gc.astype(f32))], axis=-1)
    y_c = y_c.astype(hc.dtype) @ w_out
    return y_l, y_c


def _moe(h, router_w, router_b, w13, w2, sw13, sw2):
    n, d = h.shape
    f32 = jnp.float32
    scores = jax.nn.sigmoid((h @ router_w).astype(f32))
    biased = scores + router_b.astype(f32)
    grouped = biased.reshape(n, N_GROUPS, N_EXPERTS // N_GROUPS)
    group_score = jnp.sum(lax.top_k(grouped, 2)[0], axis=-1)
    _, gidx = lax.top_k(group_score, TOPK_GROUPS)
    gmask = jnp.sum(jax.nn.one_hot(gidx, N_GROUPS, dtype=f32), axis=1) > 0
    emask = jnp.repeat(gmask, N_EXPERTS // N_GROUPS, axis=-1)
    _, eidx = lax.top_k(jnp.where(emask, biased, -jnp.inf), TOP_K)
    gate = jnp.take_along_axis(scores, eidx, axis=-1)
    gate = gate / jnp.sum(gate, axis=-1, keepdims=True) * ROUTE_SCALE

    nk = n * TOP_K
    e_flat = eidx.reshape(nk)
    order = jnp.argsort(e_flat)
    se = e_flat[order]
    st = (order // TOP_K).astype(jnp.int32)
    sw = gate.reshape(nk)[order]
    counts = jnp.bincount(e_flat, length=N_EXPERTS)
    starts = jnp.cumsum(counts) - counts
    padded = (counts + MOE_BLOCK - 1) // MOE_BLOCK * MOE_BLOCK
    pends = jnp.cumsum(padded)
    pstarts = pends - padded
    pos = pstarts[se] + jnp.arange(nk) - starts[se]
    n_blocks = -(-nk // MOE_BLOCK) + N_EXPERTS
    cap = n_blocks * MOE_BLOCK
    tok_buf = jnp.full((cap,), n, jnp.int32).at[pos].set(st)
    w_buf = jnp.zeros((cap,), f32).at[pos].set(sw)
    blk_e = jnp.minimum(jnp.searchsorted(pends, jnp.arange(n_blocks) * MOE_BLOCK, side='right'), N_EXPERTS - 1)
    h_pad = jnp.concatenate([h, jnp.zeros((1, d), h.dtype)], axis=0)

    def body(acc, inp):
        tb, wb, e = inp
        xb = h_pad[tb]
        u = xb @ w13[e]
        y = (jax.nn.silu(u[:, :D_EXPERT]) * u[:, D_EXPERT:]) @ w2[e]
        return acc.at[tb].add(y * wb[:, None].astype(y.dtype)), None

    acc, _ = lax.scan(body, jnp.zeros((n + 1, d), h.dtype),
                      (tok_buf.reshape(n_blocks, MOE_BLOCK), w_buf.reshape(n_blocks, MOE_BLOCK), blk_e))
    u = h @ sw13
    shared = (jax.nn.silu(u[:, :D_SHARED]) * u[:, D_SHARED:]) @ sw2
    return acc[:n] + shared


def setup_inputs(seed: int = 0) -> dict:
    key = jax.random.key(seed)
    ks = jax.random.split(key, 32)
    f32 = jnp.float32
    D = D_MODEL

    def nrm(k, shape, s):
        return jax.random.normal(k, shape, f32) * s

    lam_u = jax.random.uniform(ks[16], (DEPTH, 2, LRU_W), f32, 0.9, 0.999)
    sig = lam_u ** (1.0 / LRU_C)
    return {
        'x': nrm(ks[0], (BATCH, SEQ, D), 1.0),
        'c': nrm(ks[1], (BATCH, D), 1.0),
        'ctx': nrm(ks[2], (BATCH, CTX_LEN, D), 1.0),
        'c_ctx': nrm(ks[3], (D,), 1.0),
        'ada_w': nrm(ks[4], (DEPTH, D, 6 * D), 0.5 * D ** -0.5),
        'ada_b': nrm(ks[5], (DEPTH, 6 * D), 0.02),
        'norm_mix': 1.0 + nrm(ks[6], (DEPTH, D), 0.02),
        'norm_ffn': 1.0 + nrm(ks[7], (DEPTH, D), 0.02),
        'norm_final': 1.0 + nrm(ks[8], (D,), 0.02),
        'w_in': nrm(ks[9], (DEPTH, D, IN_COLS), D ** -0.5),
        'w_out': nrm(ks[10], (DEPTH, D_MIX, D), D_MIX ** -0.5),
        'lru_conv_w': nrm(ks[11], (DEPTH, CONV_W, LRU_W), CONV_W ** -0.5),
        'lru_conv_b': nrm(ks[12], (DEPTH, LRU_W), 0.01),
        'lru_wa': nrm(ks[13], (DEPTH, 2, LRU_BLOCKS, LRU_BD, LRU_BD), LRU_BD ** -0.5),
        'lru_ba': nrm(ks[14], (DEPTH, 2, LRU_W), 0.01),
        'lru_wx': nrm(ks[15], (DEPTH, 2, LRU_BLOCKS, LRU_BD, LRU_BD), LRU_BD ** -0.5),
        'lru_bx': nrm(ks[17], (DEPTH, 2, LRU_W), 0.01),
        'lru_lambda': jnp.log(sig) - jnp.log1p(-sig),
        'hgrn_lb_logits': nrm(ks[18], (2, DEPTH + 1, HG_W), 0.1),
        'hgrn_norm': 1.0 + nrm(ks[19], (DEPTH, HG_W), 0.02),
        'router_w': nrm(ks[20], (DEPTH, D, N_EXPERTS), D ** -0.5),
        'router_b': nrm(ks[21], (DEPTH, N_EXPERTS), 0.01),
        'exp_w13': nrm(ks[22], (DEPTH, N_EXPERTS, D, 2 * D_EXPERT), D ** -0.5),
        'exp_w2': nrm(ks[23], (DEPTH, N_EXPERTS, D_EXPERT, D), D_EXPERT ** -0.5),
        'shared_w13': nrm(ks[24], (DEPTH, D, 2 * D_SHARED), D ** -0.5),
        'shared_w2': nrm(ks[25], (DEPTH, D_SHARED, D), D_SHARED ** -0.5),
    }


def reference(x, c, ctx, c_ctx, ada_w, ada_b, norm_mix, norm_ffn, norm_final, w_in, w_out,
              lru_conv_w, lru_conv_b, lru_wa, lru_ba, lru_wx, lru_bx, lru_lambda,
              hgrn_lb_logits, hgrn_norm, router_w, router_b, exp_w13, exp_w2, shared_w13, shared_w2):
    b, n, d = x.shape
    lb_all = jnp.cumsum(jax.nn.softmax(hgrn_lb_logits.astype(jnp.float32), axis=1), axis=1)
    for l in range(DEPTH):
        last = l == DEPTH - 1
        mod_x = (jax.nn.silu(c) @ ada_w[l] + ada_b[l])[:, None, :]
        mod_c = jax.nn.silu(c_ctx) @ ada_w[l] + ada_b[l]
        sh1x, sc1x, g1x, sh2x, sc2x, g2x = jnp.split(mod_x, 6, axis=-1)
        sh1c, sc1c, g1c, sh2c, sc2c, g2c = jnp.split(mod_c, 6, axis=-1)

        hx = _rmsnorm(x, norm_mix[l]) * (1.0 + sc1x) + sh1x
        hc = _rmsnorm(ctx, norm_mix[l]) * (1.0 + sc1c) + sh1c
        yx, yc = _hybrid_mixer(hx, hc, w_in[l], w_out[l], lru_conv_w[l], lru_conv_b[l],
                               lru_wa[l], lru_ba[l], lru_wx[l], lru_bx[l], lru_lambda[l],
                               lb_all[:, l], hgrn_norm[l], not last)
        x = x + g1x * yx
        hx2 = _rmsnorm(x, norm_ffn[l]) * (1.0 + sc2x) + sh2x
        if last:
            y = _moe(hx2.reshape(b * n, d), router_w[l], router_b[l], exp_w13[l], exp_w2[l],
                     shared_w13[l], shared_w2[l])
            x = x + g2x * y.reshape(b, n, d)
        else:
            ctx = ctx + g1c * yc
            hc2 = _rmsnorm(ctx, norm_ffn[l]) * (1.0 + sc2c) + sh2c
            tokens = jnp.concatenate([hx2.reshape(b * n, d), hc2.reshape(-1, d)], axis=0)
            y = _moe(tokens, router_w[l], router_b[l], exp_w13[l], exp_w2[l],
                     shared_w13[l], shared_w2[l])
            x = x + g2x * y[:b * n].reshape(b, n, d)
            ctx = ctx + g2c * y[b * n:].reshape(ctx.shape)
    return _rmsnorm(x, norm_final)
```

```python
import functools

import jax
import jax.numpy as jnp
from jax import lax
from jax.experimental import pallas as pl
from jax.experimental.pallas import tpu as pltpu

F32 = jnp.float32
BF16 = jnp.bfloat16
I32 = jnp.int32

EPS = 1e-6
LRU_C = 8.0
ROUTE_SCALE = 2.5
GRID_W = 64
HG_CHUNK = 32
N_GROUPS = 8
TOPK_GROUPS = 4
TOP_K = 8
LRU_BLOCKS = 8
HG_HEADS = 4
CONV_W = 4
CONV_PAD_L = 1

SUBLANES = 8
LANES = 128
N_SEG = SUBLANES
VMEM_LIMIT = 56 * 1024 * 1024


def _cparams(sem, vmem=VMEM_LIMIT):
    return pltpu.CompilerParams(dimension_semantics=sem, vmem_limit_bytes=vmem)


def _sigmoid(x):
    return jax.nn.sigmoid(x)


def _silu(x):
    return x * _sigmoid(x)


def _gelu_tanh(x):
    c = 0.7978845608028654
    return 0.5 * x * (1.0 + jnp.tanh(c * (x + 0.044715 * (x * x * x))))


def _ada_kernel(c_ref, w_ref, b_ref, o_ref):
    s = _silu(c_ref[...])
    o_ref[...] = jnp.dot(s, w_ref[...], preferred_element_type=F32,
                         precision=lax.Precision.HIGHEST) + b_ref[...]


def _ada(cs, w, b):
    rows, d = cs.shape
    n = w.shape[1]
    bn = 1024
    return pl.pallas_call(
        _ada_kernel,
        out_shape=jax.ShapeDtypeStruct((rows, n), F32),
        grid=(n // bn,),
        in_specs=[pl.BlockSpec((rows, d), lambda j: (0, 0)),
                  pl.BlockSpec((d, bn), lambda j: (0, j)),
                  pl.BlockSpec((1, bn), lambda j: (0, j))],
        out_specs=pl.BlockSpec((rows, bn), lambda j: (0, j)),
        compiler_params=_cparams(("arbitrary",)),
        name="ada",
    )(cs, w, b)


def _inproj_kernel(x_ref, mod_ref, gain_ref, w_ref, o_ref):
    x = x_ref[0]
    ms = jnp.mean(x * x, axis=-1, keepdims=True)
    y = x * lax.rsqrt(ms + EPS) * gain_ref[...]
    h = y * (1.0 + mod_ref[0, 1:2, :]) + mod_ref[0, 0:1, :]
    o_ref[0] = jnp.dot(h.astype(BF16), w_ref[...], preferred_element_type=F32)


def _inproj(x, mods, gain, w_bf, tm, shared_mod):
    b, s, d = x.shape
    n = w_bf.shape[1]
    mod_map = (lambda bi, i: (0, 0, 0)) if shared_mod else (lambda bi, i: (bi, 0, 0))
    return pl.pallas_call(
        _inproj_kernel,
        out_shape=jax.ShapeDtypeStruct((b, s, n), F32),
        grid=(b, s // tm),
        in_specs=[pl.BlockSpec((1, tm, d), lambda bi, i: (bi, i, 0)),
                  pl.BlockSpec((1, 8, d), mod_map),
                  pl.BlockSpec((1, d), lambda bi, i: (0, 0)),
                  pl.BlockSpec((d, n), lambda bi, i: (0, 0))],
        out_specs=pl.BlockSpec((1, tm, n), lambda bi, i: (bi, i, 0)),
        compiler_params=_cparams(("arbitrary", "arbitrary")),
        name="inproj",
    )(x, mods, gain, w_bf)


LRU_G = 256
LRU_CHUNK = 256


def _seg_pitch(seg_len):
    return seg_len + SUBLANES


def _lane_store(ref, d, rows, val):
    nl = val.shape[-1] // LANES
    for l in range(nl):
        ref[d * nl + l, rows, :] = val[:, l * LANES:(l + 1) * LANES]


def _lane_load(ref, d, rows, nl):
    return jnp.concatenate([ref[d * nl + l, rows, :] for l in range(nl)], axis=-1)


def _lru_coeffs(pad_ref, t0, rows, cw, cb, wg_ref, bg, sp, a_ref, b_ref, dst0):
    win = pad_ref[pl.ds(t0, rows + 2 * SUBLANES), :]
    u = cb
    for k in range(CONV_W):
        off = SUBLANES - CONV_PAD_L + k
        u = u + win[off:off + rows, :] * cw[k:k + 1, :]
    ub = u.astype(BF16)
    for d in range(2):
        r = _sigmoid(jnp.dot(ub, wg_ref[2 * d, 0], preferred_element_type=F32)
                     + bg[2 * d:2 * d + 1, :])
        ig = _sigmoid(jnp.dot(ub, wg_ref[2 * d + 1, 0], preferred_element_type=F32)
                      + bg[2 * d + 1:2 * d + 2, :])
        log_a = (-LRU_C) * r * sp[d:d + 1, :]
        a = jnp.exp(log_a)
        th = jnp.tanh(log_a)
        one_minus_a2 = (-2.0 * th) / (1.0 - th)
        bb = jnp.sqrt(one_minus_a2) * (ig * u)
        _lane_store(a_ref, d, pl.ds(dst0, rows), a)
        _lane_store(b_ref, d, pl.ds(dst0, rows), bb)


def _seg_scan(a_ref, b_ref, seg_len, pitch, unroll):
    n_lead = a_ref.shape[0]
    nl = n_lead // 2
    zero = jnp.zeros((N_SEG, LANES), F32)
    one = jnp.ones((N_SEG, LANES), F32)
    init = tuple((zero, one) for _ in range(n_lead))

    def step(t, carry):
        out = []
        for i in range(n_lead):
            h, p = carry[i]
            tt = t if i < nl else seg_len - 1 - t
            rows = pl.ds(tt, N_SEG, stride=pitch)
            a = a_ref[i, rows, :]
            h = a * h + b_ref[i, rows, :]
            p = p * a
            b_ref[i, rows, :] = h
            a_ref[i, rows, :] = p
            out.append((h, p))
        return tuple(out)

    def body(i, carry):
        for j in range(unroll):
            carry = step(i * unroll + j, carry)
        return carry

    ends = lax.fori_loop(0, seg_len // unroll, body, init)
    h_end = [jnp.concatenate([ends[d * nl + l][0] for l in range(nl)], axis=-1) for d in range(2)]
    p_end = [jnp.concatenate([ends[d * nl + l][1] for l in range(nl)], axis=-1) for d in range(2)]
    return h_end, p_end


def _seg_carries(h_end, p_end, h0, reverse):
    order = range(N_SEG - 1, -1, -1) if reverse else range(N_SEG)
    cin = [None] * N_SEG
    c = h0
    for s in order:
        cin[s] = c
        c = p_end[s:s + 1, :] * c + h_end[s:s + 1, :]
    return cin, c


def _lru_kernel(rx_ref, rg_ref, rc_ref, cw_ref, cb_ref, wg_ref, bg_ref, lam_ref, o_ref,
                padl, padc, a_l, b_l, a_c, b_c):
    s_len = rx_ref.shape[1]
    c_len = rc_ref.shape[1]
    g = rx_ref.shape[2]
    nl = g // LANES
    seg_l = s_len // N_SEG
    seg_c = c_len // N_SEG
    pitch_l = _seg_pitch(seg_l)
    pitch_c = _seg_pitch(seg_c)

    zeros = jnp.zeros((SUBLANES, g), F32)
    padl[0:SUBLANES, :] = zeros
    padl[SUBLANES + s_len:2 * SUBLANES + s_len, :] = zeros
    padc[0:SUBLANES, :] = zeros
    padc[SUBLANES + c_len:2 * SUBLANES + c_len, :] = zeros
    for i in range(s_len // LRU_CHUNK):
        padl[SUBLANES + i * LRU_CHUNK:SUBLANES + (i + 1) * LRU_CHUNK, :] = (
            rx_ref[0, i * LRU_CHUNK:(i + 1) * LRU_CHUNK, :])
    padc[SUBLANES:SUBLANES + c_len, :] = rc_ref[0]

    cw = cw_ref[...]
    cb = cb_ref[...]
    bg = bg_ref[...]
    x = -lam_ref[...]
    sp = jnp.maximum(x, 0.0) + jnp.log(1.0 + jnp.exp(-jnp.abs(x)))

    _lru_coeffs(padc, 0, c_len, cw, cb, wg_ref, bg, sp, a_l, b_l, 0)
    for i in range(2 * nl):
        for s in range(N_SEG):
            a_c[i, s * pitch_c:s * pitch_c + seg_c, :] = a_l[i, s * seg_c:(s + 1) * seg_c, :]
            b_c[i, s * pitch_c:s * pitch_c + seg_c, :] = b_l[i, s * seg_c:(s + 1) * seg_c, :]
    h_end, p_end = _seg_scan(a_c, b_c, seg_c, pitch_c, unroll=4)
    zero_row = jnp.zeros((1, g), F32)
    _, h0_f = _seg_carries(h_end[0], p_end[0], zero_row, reverse=False)
    _, h0_b = _seg_carries(h_end[1], p_end[1], zero_row, reverse=True)

    def coeff_body(s, carry):
        t0 = pl.multiple_of(s * seg_l, SUBLANES)
        dst = pl.multiple_of(s * pitch_l, SUBLANES)
        _lru_coeffs(padl, t0, seg_l, cw, cb, wg_ref, bg, sp, a_l, b_l, dst)
        return carry

    lax.fori_loop(0, N_SEG, coeff_body, 0)
    h_end, p_end = _seg_scan(a_l, b_l, seg_l, pitch_l, unroll=4)
    cin_f, _ = _seg_carries(h_end[0], p_end[0], h0_f, reverse=False)
    cin_b, _ = _seg_carries(h_end[1], p_end[1], h0_b, reverse=True)

    for s in range(N_SEG):
        rows = slice(s * pitch_l, s * pitch_l + seg_l)
        h = (_lane_load(b_l, 0, rows, nl) + _lane_load(a_l, 0, rows, nl) * cin_f[s]) + (
            _lane_load(b_l, 1, rows, nl) + _lane_load(a_l, 1, rows, nl) * cin_b[s])
        o_ref[0, s * seg_l:(s + 1) * seg_l, :] = _gelu_tanh(rg_ref[0, s * seg_l:(s + 1) * seg_l, :]) * h


def _lru(projx, projc, cw, cb, wg, bg, lam):
    b, s_len, _ = projx.shape
    c_len = projc.shape[1]
    w = cw.shape[1]
    g = LRU_G
    ng = w // g
    seg_l = s_len // N_SEG
    seg_c = c_len // N_SEG
    assert seg_l == LRU_CHUNK and c_len % (N_SEG * 1) == 0
    return pl.pallas_call(
        _lru_kernel,
        out_shape=jax.ShapeDtypeStruct((b, s_len, w), F32),
        grid=(b, ng),
        in_specs=[pl.BlockSpec((1, s_len, g), lambda bi, j: (bi, 0, j)),
                  pl.BlockSpec((1, s_len, g), lambda bi, j: (bi, 0, ng + j)),
                  pl.BlockSpec((1, c_len, g), lambda bi, j: (bi, 0, j)),
                  pl.BlockSpec((CONV_W, g), lambda bi, j: (0, j)),
                  pl.BlockSpec((1, g), lambda bi, j: (0, j)),
                  pl.BlockSpec((4, 1, g, g), lambda bi, j: (0, j, 0, 0)),
                  pl.BlockSpec((4, g), lambda bi, j: (0, j)),
                  pl.BlockSpec((2, g), lambda bi, j: (0, j))],
        out_specs=pl.BlockSpec((1, s_len, g), lambda bi, j: (bi, 0, j)),
        scratch_shapes=[pltpu.VMEM((s_len + 2 * SUBLANES, g), F32),
                        pltpu.VMEM((c_len + 2 * SUBLANES, g), F32),
                        pltpu.VMEM((2 * g // LANES, N_SEG * _seg_pitch(seg_l), LANES), F32),
                        pltpu.VMEM((2 * g // LANES, N_SEG * _seg_pitch(seg_l), LANES), F32),
                        pltpu.VMEM((2 * g // LANES, N_SEG * _seg_pitch(seg_c), LANES), F32),
                        pltpu.VMEM((2 * g // LANES, N_SEG * _seg_pitch(seg_c), LANES), F32)],
        compiler_params=_cparams(("arbitrary", "arbitrary")),
        name="lru",
    )(projx, projx, projc, cw, cb, wg, bg, lam)


HG_PITCH_L = GRID_W + SUBLANES
HG_PITCH_C = SUBLANES


def _hg_gates(z, lb):
    e = jnp.exp(-jnp.abs(z))
    s = 1.0 / (1.0 + e)
    es = e * s
    pos = z >= 0.0
    sig = jnp.where(pos, s, es)
    nsig = jnp.where(pos, es, s)
    logf = jnp.log(lb + (1.0 - lb) * sig)
    k = (1.0 - lb) * nsig
    return logf, k


def _hg_prepass(load, width, pitch, lb, qd, kd, kl, vs, dec):
    n_pos = HG_CHUNK
    for d, zname in enumerate(("zf", "zb")):
        order = range(n_pos) if d == 0 else range(n_pos - 1, -1, -1)
        g = jnp.zeros((width, LANES), F32)
        lbd = lb[d:d + 1, :]
        for p in order:
            logf, k = _hg_gates(load(zname, p), lbd)
            g = g + logf
            qd[d, p * pitch:p * pitch + width, :] = g
            kl[d, p * pitch:p * pitch + width, :] = k
        g_last = g
        dec[d, 0:width, :] = jnp.exp(g_last)
        for p in range(n_pos):
            rows = slice(p * pitch, p * pitch + width)
            gp = qd[d, rows, :]
            k = kl[d, rows, :]
            qd[d, rows, :] = load("q", p) * jnp.exp(gp)
            kd[d, rows, :] = k * jnp.exp(-gp)
            kl[d, rows, :] = k * jnp.exp(g_last - gp)
    for p in range(n_pos):
        vs[p * pitch:p * pitch + width, :] = load("v", p)


def _hg_chunk(d, w, pitch, qd, kd, kl, vs, dec, st, masks, o_ref):
    rows = pl.ds(w, HG_CHUNK, stride=pitch)
    q = qd[d, rows, :].astype(BF16)
    k = kd[d, rows, :].astype(BF16)
    kls = kl[d, rows, :].astype(BF16)
    v = vs[rows, :].astype(BF16)
    dc = dec[d, pl.ds(w, 1), :]
    sc = lax.dot_general(q, k, (((1,), (1,)), ((), ())), preferred_element_type=F32)
    sc = jnp.where(masks[d], sc, 0.0)
    s_t = st[d]
    if o_ref is not None:
        o = jnp.dot(sc.astype(BF16), v, preferred_element_type=F32)
        o = o + lax.dot_general(q, s_t.astype(BF16), (((1,), (1,)), ((), ())),
                                preferred_element_type=F32)
        o_ref[d, rows, :] = o
    kv_t = lax.dot_general(v, kls, (((0,), (0,)), ((), ())), preferred_element_type=F32)
    st[d] = s_t * dc + kv_t


def _hgrn_kernel(q_ref, v_ref, zf_ref, zb_ref, g_ref, qc_ref, vc_ref, zfc_ref, zbc_ref,
                 lbl_ref, gain_ref, o_ref, qd, kd, kl, vs, dec, obuf, st):
    s_len = q_ref.shape[1]
    c_len = qc_ref.shape[1]
    n_col = s_len // HG_CHUNK
    n_cc = c_len // HG_CHUNK
    assert n_col == GRID_W and n_cc == HG_PITCH_C

    lg = lbl_ref[...]
    m = jnp.max(lg, axis=1, keepdims=True)
    ex = jnp.exp(lg - m)
    lb = ex[:, 0, :] / jnp.sum(ex, axis=1)

    ii = lax.broadcasted_iota(I32, (HG_CHUNK, HG_CHUNK), 0)
    jj = lax.broadcasted_iota(I32, (HG_CHUNK, HG_CHUNK), 1)
    masks = (jj <= ii, jj >= ii)

    st[...] = jnp.zeros(st.shape, F32)

    c_refs = {"q": qc_ref, "v": vc_ref, "zf": zfc_ref, "zb": zbc_ref}

    def load_c(name, p):
        return c_refs[name][0, pl.ds(p, n_cc, stride=HG_CHUNK), :]

    _hg_prepass(load_c, n_cc, HG_PITCH_C, lb, qd, kd, kl, vs, dec)

    def ctx_body(n, carry):
        _hg_chunk(0, n, HG_PITCH_C, qd, kd, kl, vs, dec, st, masks, None)
        _hg_chunk(1, n_cc - 1 - n, HG_PITCH_C, qd, kd, kl, vs, dec, st, masks, None)
        return carry

    lax.fori_loop(0, n_cc, ctx_body, 0)

    l_refs = {"q": q_ref, "v": v_ref, "zf": zf_ref, "zb": zb_ref}

    def load_l(name, p):
        return l_refs[name][0, p * n_col:(p + 1) * n_col, :]

    _hg_prepass(load_l, n_col, HG_PITCH_L, lb, qd, kd, kl, vs, dec)

    def lat_body(i, carry):
        _hg_chunk(0, i, HG_PITCH_L, qd, kd, kl, vs, dec, st, masks, obuf)
        _hg_chunk(1, n_col - 1 - i, HG_PITCH_L, qd, kd, kl, vs, dec, st, masks, obuf)
        return carry

    lax.fori_loop(0, n_col, lat_body, 0)

    gain = gain_ref[...]
    for r in range(HG_CHUNK):
        rows = slice(r * HG_PITCH_L, r * HG_PITCH_L + n_col)
        o = obuf[0, rows, :] + obuf[1, rows, :]
        y = o * lax.rsqrt(jnp.mean(o * o, axis=-1, keepdims=True) + EPS) * gain
        o_ref[0, r * n_col:(r + 1) * n_col, :] = y * _silu(g_ref[0, r * n_col:(r + 1) * n_col, :])


def _hgrn(projx, projc, lbl, gain, col0):
    b, s_len, _ = projx.shape
    c_len = projc.shape[1]
    hw = gain.shape[1]
    nh = hw // LANES
    n_slots = lbl.shape[1]

    def xs(k):
        return pl.BlockSpec((1, s_len, LANES), lambda bi, h, k=k: (bi, 0, col0 + k * nh + h))

    def cs(k):
        return pl.BlockSpec((1, c_len, LANES), lambda bi, h, k=k: (bi, 0, col0 + k * nh + h))

    sc_rows = HG_CHUNK * HG_PITCH_L
    return pl.pallas_call(
        _hgrn_kernel,
        out_shape=jax.ShapeDtypeStruct((b, s_len, hw), F32),
        grid=(b, nh),
        in_specs=[xs(0), xs(1), xs(2), xs(3), xs(4), cs(0), cs(1), cs(2), cs(3),
                  pl.BlockSpec((2, n_slots, LANES), lambda bi, h: (0, 0, h)),
                  pl.BlockSpec((1, LANES), lambda bi, h: (0, h))],
        out_specs=pl.BlockSpec((1, s_len, LANES), lambda bi, h: (bi, 0, h)),
        scratch_shapes=[pltpu.VMEM((2, sc_rows, LANES), F32),
                        pltpu.VMEM((2, sc_rows, LANES), F32),
                        pltpu.VMEM((2, sc_rows, LANES), F32),
                        pltpu.VMEM((sc_rows, LANES), F32),
                        pltpu.VMEM((2, HG_PITCH_L, LANES), F32),
                        pltpu.VMEM((2, sc_rows, LANES), F32),
                        pltpu.VMEM((2, LANES, LANES), F32)],
        compiler_params=_cparams(("arbitrary", "arbitrary")),
        name="hgrn",
    )(projx, projx, projx, projx, projx, projc, projc, projc, projc, lbl, gain)


def _first_index_of_max(vals, iota, big):
    m = jnp.max(vals, axis=0, keepdims=True)
    idx = jnp.min(jnp.where(vals == m, iota, big), axis=0, keepdims=True)
    return m, idx


def _route(logits_t, rb, n_experts):
    t = logits_t.shape[1]
    gsz = n_experts // N_GROUPS
    neg = -jnp.inf
    scores = _sigmoid(logits_t)
    biased = scores + rb
    iota_g = lax.broadcasted_iota(I32, (gsz, t), 0)
    gscore = []
    for gi in range(N_GROUPS):
        blk = biased[gi * gsz:(gi + 1) * gsz, :]
        m1, i1 = _first_index_of_max(blk, iota_g, gsz)
        m2 = jnp.max(jnp.where(iota_g == i1, neg, blk), axis=0, keepdims=True)
        gscore.append(m1 + m2)
    gs = jnp.concatenate(gscore, axis=0)
    iota_n = lax.broadcasted_iota(I32, (N_GROUPS, t), 0)
    gsel = jnp.zeros((N_GROUPS, t), jnp.bool_)
    for _ in range(TOPK_GROUPS):
        _, gi1 = _first_index_of_max(gs, iota_n, N_GROUPS)
        hit = iota_n == gi1
        gsel = jnp.logical_or(gsel, hit)
        gs = jnp.where(hit, neg, gs)
    emask = jnp.concatenate(
        [jnp.broadcast_to(gsel[gi:gi + 1, :], (gsz, t)) for gi in range(N_GROUPS)], axis=0)
    masked = jnp.where(emask, biased, neg)
    iota_e = lax.broadcasted_iota(I32, (n_experts, t), 0)
    ids, gates = [], []
    sel = jnp.zeros((n_experts, t), F32)
    for _ in range(TOP_K):
        _, ei = _first_index_of_max(masked, iota_e, n_experts)
        hit = iota_e == ei
        ids.append(ei)
        gates.append(jnp.sum(jnp.where(hit, scores, 0.0), axis=0, keepdims=True))
        sel = jnp.where(hit, 1.0, sel)
        masked = jnp.where(hit, neg, masked)
    ids = jnp.concatenate(ids, axis=0)
    gates = jnp.concatenate(gates, axis=0)
    gates = gates / jnp.sum(gates, axis=0, keepdims=True) * ROUTE_SCALE
    return ids, gates, sel


def _outproj_kernel(yl_ref, yh_ref, x_ref, mod_ref, gain_ref, wo_ref, rwt_ref, rb_ref,
                    x1_ref, h2_ref, eid_ref, gate_ref, rank_ref, cnt_ref, carry):
    first = jnp.logical_and(pl.program_id(0) == 0, pl.program_id(1) == 0)

    @pl.when(first)
    def _():
        carry[...] = jnp.zeros(carry.shape, F32)

    y = jnp.dot(yl_ref[0].astype(BF16), wo_ref[0], preferred_element_type=F32)
    y = y + jnp.dot(yh_ref[0].astype(BF16), wo_ref[1], preferred_element_type=F32)
    x1 = x_ref[0] + mod_ref[0, 2:3, :] * y
    x1_ref[0] = x1
    ms = jnp.mean(x1 * x1, axis=-1, keepdims=True)
    h2 = x1 * lax.rsqrt(ms + EPS) * gain_ref[...]
    h2 = h2 * (1.0 + mod_ref[0, 4:5, :]) + mod_ref[0, 3:4, :]
    h2_ref[...] = h2

    n_experts = rwt_ref.shape[0]
    t = h2.shape[0]
    logits_t = lax.dot_general(rwt_ref[...], h2, (((1,), (1,)), ((), ())),
                               preferred_element_type=F32,
                               precision=lax.Precision.HIGHEST)
    ids, gates, sel = _route(logits_t, rb_ref[...], n_experts)
    eid_ref[...] = ids
    gate_ref[...] = gates

    ti = lax.broadcasted_iota(I32, (t, t), 0)
    tj = lax.broadcasted_iota(I32, (t, t), 1)
    upper = (ti < tj).astype(BF16)
    selb = sel.astype(BF16)
    before = jnp.dot(selb, upper, preferred_element_type=F32) + carry[:, 0:1]
    iota_e = lax.broadcasted_iota(I32, (n_experts, t), 0)
    ranks = [jnp.sum(jnp.where(iota_e == ids[k:k + 1, :], before, 0.0), axis=0, keepdims=True)
             for k in range(TOP_K)]
    rank_ref[...] = jnp.concatenate(ranks, axis=0).astype(I32)
    total = carry[...] + jnp.dot(selb, jnp.ones((t, LANES), BF16), preferred_element_type=F32)
    carry[...] = total
    cnt_ref[...] = total.astype(I32)


def _outproj(ylru, yhg, x, mods, gain, wo, rwt, rb, tm):
    b, s, d = x.shape
    hw = ylru.shape[2]
    e = rwt.shape[0]
    n = b * s
    nt = s // tm
    tok = lambda bi, i: (0, bi * nt + i)
    return pl.pallas_call(
        _outproj_kernel,
        out_shape=(jax.ShapeDtypeStruct((b, s, d), F32),
                   jax.ShapeDtypeStruct((n, d), F32),
                   jax.ShapeDtypeStruct((TOP_K, n), I32),
                   jax.ShapeDtypeStruct((TOP_K, n), F32),
                   jax.ShapeDtypeStruct((TOP_K, n), I32),
                   jax.ShapeDtypeStruct((e, LANES), I32)),
        grid=(b, nt),
        in_specs=[pl.BlockSpec((1, tm, hw), lambda bi, i: (bi, i, 0)),
                  pl.BlockSpec((1, tm, hw), lambda bi, i: (bi, i, 0)),
                  pl.BlockSpec((1, tm, d), lambda bi, i: (bi, i, 0)),
                  pl.BlockSpec((1, 8, d), lambda bi, i: (bi, 0, 0)),
                  pl.BlockSpec((1, d), lambda bi, i: (0, 0)),
                  pl.BlockSpec((2, hw, d), lambda bi, i: (0, 0, 0)),
                  pl.BlockSpec((e, d), lambda bi, i: (0, 0)),
                  pl.BlockSpec((e, 1), lambda bi, i: (0, 0))],
        out_specs=(pl.BlockSpec((1, tm, d), lambda bi, i: (bi, i, 0)),
                   pl.BlockSpec((tm, d), lambda bi, i: (bi * nt + i, 0)),
                   pl.BlockSpec((TOP_K, tm), tok),
                   pl.BlockSpec((TOP_K, tm), tok),
                   pl.BlockSpec((TOP_K, tm), tok),
                   pl.BlockSpec((e, LANES), lambda bi, i: (0, 0))),
        scratch_shapes=[pltpu.VMEM((e, LANES), F32)],
        compiler_params=_cparams(("arbitrary", "arbitrary")),
        name="outproj",
    )(ylru, yhg, x, mods, gain, wo, rwt, rb)


def _row_copy(src, dst, sem):
    return pltpu.make_async_copy(src, dst, sem)


def _dispatch_kernel(pos_ref, h_ref, xs_in, xs_ref, sem):
    del xs_in
    td = h_ref.shape[0]

    def body(j, carry):
        src = h_ref.at[pl.ds(j, 1)]
        for k in range(TOP_K):
            _row_copy(src, xs_ref.at[pl.ds(pos_ref[0, 0, k * td + j], 1)], sem).start()
        return carry

    lax.fori_loop(0, td, body, 0)
    def wait_body(j, carry):
        for k in range(TOP_K):
            _row_copy(h_ref.at[pl.ds(0, 1)], xs_ref.at[pl.ds(0, 1)], sem).wait()
        return carry

    lax.fori_loop(0, td, wait_body, 0)


def _dispatch(h2, pos_tiles, xs_init, td):
    n, d = h2.shape
    nt = n // td
    return pl.pallas_call(
        _dispatch_kernel,
        out_shape=jax.ShapeDtypeStruct(xs_init.shape, xs_init.dtype),
        grid=(nt,),
        in_specs=[pl.BlockSpec((1, 1, TOP_K * td), lambda i: (i, 0, 0), memory_space=pltpu.SMEM),
                  pl.BlockSpec((td, d), lambda i: (i, 0)),
                  pl.BlockSpec(memory_space=pl.ANY)],
        out_specs=pl.BlockSpec(memory_space=pl.ANY),
        scratch_shapes=[pltpu.SemaphoreType.DMA(())],
        input_output_aliases={2: 0},
        compiler_params=_cparams(("arbitrary",)),
        name="dispatch",
    )(pos_tiles, h2, xs_init)


def _gmm_kernel(blk_e, nused, xs_ref, w13_ref, w2_ref, ys_ref, w13b, w2b):
    i = pl.program_id(0)
    de = w2_ref.shape[1]

    @pl.when(i < nused[0])
    def _():
        prev = blk_e[jnp.maximum(i - 1, 0)]
        new_expert = jnp.logical_or(i == 0, blk_e[i] != prev)

        @pl.when(new_expert)
        def _():
            w13b[...] = w13_ref[0].astype(BF16)
            w2b[...] = w2_ref[0].astype(BF16)

        u = jnp.dot(xs_ref[...].astype(BF16), w13b[...], preferred_element_type=F32)
        hmid = _silu(u[:, :de]) * u[:, de:]
        ys_ref[...] = jnp.dot(hmid.astype(BF16), w2b[...], preferred_element_type=F32)


def _gmm(xs, w13, w2, blk_e, nused, bm):
    cap, d = xs.shape
    e, _, de2 = w13.shape
    de = w2.shape[1]
    nb = cap // bm

    def row_map(i, be, nu):
        return (jnp.minimum(i, nu[0] - 1), 0)

    def w_map(i, be, nu):
        return (be[jnp.minimum(i, nu[0] - 1)], 0, 0)

    return pl.pallas_call(
        _gmm_kernel,
        out_shape=jax.ShapeDtypeStruct((cap, d), F32),
        grid_spec=pltpu.PrefetchScalarGridSpec(
            num_scalar_prefetch=2,
            grid=(nb,),
            in_specs=[pl.BlockSpec((bm, d), row_map),
                      pl.BlockSpec((1, d, de2), w_map),
                      pl.BlockSpec((1, de, d), w_map)],
            out_specs=pl.BlockSpec((bm, d), row_map),
            scratch_shapes=[pltpu.VMEM((d, de2), BF16), pltpu.VMEM((de, d), BF16)]),
        compiler_params=_cparams(("arbitrary",)),
        name="gmm",
    )(blk_e, nused, xs, w13, w2)


def _combine_kernel(pos_ref, gate_ref, x1_ref, h2_ref, mod_ref, gain_ref, sw13_ref, sw2_ref,
                    ys_ref, o_ref, ybuf, sem):
    tc = x1_ref.shape[0]
    ds_ = sw2_ref.shape[0]

    def body(j, carry):
        for k in range(TOP_K):
            _row_copy(ys_ref.at[pl.ds(pos_ref[0, 0, k * tc + j], 1)],
                      ybuf.at[k, pl.ds(j, 1)], sem).start()
        return carry

    lax.fori_loop(0, tc, body, 0)

    u = jnp.dot(h2_ref[...].astype(BF16), sw13_ref[...], preferred_element_type=F32)
    hmid = _silu(u[:, :ds_]) * u[:, ds_:]
    y = jnp.dot(hmid.astype(BF16), sw2_ref[...], preferred_element_type=F32)

    def wait_body(j, carry):
        for k in range(TOP_K):
            _row_copy(ys_ref.at[pl.ds(0, 1)], ybuf.at[0, pl.ds(0, 1)], sem).wait()
        return carry

    lax.fori_loop(0, tc, wait_body, 0)

    gate = gate_ref[...]
    moe = gate[:, 0:1] * ybuf[0]
    for k in range(1, TOP_K):
        moe = moe + gate[:, k:k + 1] * ybuf[k]
    xo = x1_ref[...] + mod_ref[0, 5:6, :] * (moe + y)
    ms = jnp.mean(xo * xo, axis=-1, keepdims=True)
    o_ref[...] = xo * lax.rsqrt(ms + EPS) * gain_ref[...]


def _combine(pos_tiles, gate_tok, x1, h2, mods, gain, sw13, sw2, ys, tc, tiles_per_batch):
    n, d = h2.shape
    nt = n // tc
    ds2 = sw13.shape[1]
    ds_ = sw2.shape[0]
    return pl.pallas_call(
        _combine_kernel,
        out_shape=jax.ShapeDtypeStruct((n, d), F32),
        grid=(nt,),
        in_specs=[pl.BlockSpec((1, 1, TOP_K * tc), lambda i: (i, 0, 0), memory_space=pltpu.SMEM),
                  pl.BlockSpec((tc, TOP_K), lambda i: (i, 0)),
                  pl.BlockSpec((tc, d), lambda i: (i, 0)),
                  pl.BlockSpec((tc, d), lambda i: (i, 0)),
                  pl.BlockSpec((1, 8, d), lambda i: (i // tiles_per_batch, 0, 0)),
                  pl.BlockSpec((1, d), lambda i: (0, 0)),
                  pl.BlockSpec((d, ds2), lambda i: (0, 0)),
                  pl.BlockSpec((ds_, d), lambda i: (0, 0)),
                  pl.BlockSpec(memory_space=pl.ANY)],
        out_specs=pl.BlockSpec((tc, d), lambda i: (i, 0)),
        scratch_shapes=[pltpu.VMEM((TOP_K, tc, d), F32), pltpu.SemaphoreType.DMA(())],
        compiler_params=_cparams(("arbitrary",)),
        name="combine",
    )(pos_tiles, gate_tok, x1, h2, mods, gain, sw13, sw2, ys)


def _block_diag_pairs(w, per):
    nb, bd, _ = w.shape
    w = w.reshape(nb // per, per, bd, bd)
    eye = jnp.eye(per, dtype=w.dtype)
    out = jnp.einsum("gpij,pq->gpiqj", w, eye)
    return out.reshape(nb // per, per * bd, per * bd)


def _pos_tiles(pos, t):
    k, n = pos.shape
    return pos.reshape(k, n // t, t).transpose(1, 0, 2).reshape(n // t, 1, k * t)


def kernel(x, c, ctx, c_ctx, ada_w, ada_b, norm_mix, norm_ffn, norm_final, w_in, w_out,
           lru_conv_w, lru_conv_b, lru_wa, lru_ba, lru_wx, lru_bx, lru_lambda,
           hgrn_lb_logits, hgrn_norm, router_w, router_b, exp_w13, exp_w2, shared_w13, shared_w2):
    assert ada_w.shape[0] == 1, "single-layer block"
    b, s, d = x.shape
    n = b * s
    lru_w = lru_conv_w.shape[2]
    hg_w = hgrn_norm.shape[1]
    n_experts = router_w.shape[2]

    rows = -(-(b + 1) // SUBLANES) * SUBLANES
    cs = jnp.zeros((rows, d), F32).at[:b].set(c).at[b].set(c_ctx)
    mod = _ada(cs, ada_w[0], ada_b[0][None, :]).reshape(rows, 6, d)
    mods = jnp.pad(mod, ((0, 0), (0, 2), (0, 0)))

    w_in_bf = w_in[0].astype(BF16)
    gain1 = norm_mix[0][None, :]
    projx = _inproj(x, mods, gain1, w_in_bf, 512, shared_mod=False)
    projc = _inproj(ctx, mods[b:b + 1], gain1, w_in_bf, ctx.shape[1], shared_mod=True)

    per = LRU_G // (lru_w // LRU_BLOCKS)
    wg = jnp.stack([_block_diag_pairs(lru_wa[0, 0], per), _block_diag_pairs(lru_wx[0, 0], per),
                    _block_diag_pairs(lru_wa[0, 1], per), _block_diag_pairs(lru_wx[0, 1], per)]
                   ).astype(BF16)
    bg = jnp.stack([lru_ba[0, 0], lru_bx[0, 0], lru_ba[0, 1], lru_bx[0, 1]])
    ylru = _lru(projx, projc, lru_conv_w[0], lru_conv_b[0][None, :], wg, bg, lru_lambda[0])

    yhg = _hgrn(projx, projc, hgrn_lb_logits, hgrn_norm[0][None, :], (2 * lru_w) // LANES)

    wo = w_out[0].astype(BF16).reshape(2, lru_w, d)
    x1, h2, eid_t, gate_t, rank_t, cnt = _outproj(
        ylru, yhg, x, mods, norm_ffn[0][None, :], wo,
        router_w[0].T, router_b[0][:, None], 512)

    bm = 256
    counts = cnt[:, 0]
    padded = (counts + bm - 1) // bm * bm
    pends = jnp.cumsum(padded)
    pstarts = pends - padded
    pos = pstarts[eid_t] + rank_t
    nb = (n * TOP_K) // bm + n_experts
    cap = nb * bm
    blk_e = jnp.minimum(jnp.searchsorted(pends, jnp.arange(nb, dtype=I32) * bm, side="right"),
                        n_experts - 1).astype(I32)
    nused = (pends[-1] // bm).astype(I32)[None]

    td = 256
    xs = _dispatch(h2, _pos_tiles(pos, td), jnp.zeros((cap, d), F32), td)
    ys = _gmm(xs, exp_w13[0], exp_w2[0], blk_e, nused, bm)

    tc = 128
    out = _combine(_pos_tiles(pos, tc), gate_t.T, x1.reshape(n, d), h2, mods,
                   norm_final[None, :], shared_w13[0].astype(BF16), shared_w2[0].astype(BF16),
                   ys, tc, s // tc)
    return out.reshape(b, s, d)
```

```python
import functools

import jax
import jax.numpy as jnp
from jax import lax
from jax.experimental import pallas as pl
from jax.experimental.pallas import tpu as pltpu

F32 = jnp.float32
BF16 = jnp.bfloat16
I32 = jnp.int32
U32 = jnp.uint32

EPS = 1e-6
LRU_C = 8.0
ROUTE_SCALE = 2.5
GRID_W = 64
HG_CHUNK = 32
N_GROUPS = 8
TOPK_GROUPS = 4
TOP_K = 8
LRU_BLOCKS = 8
HG_HEADS = 4
CONV_W = 4
CONV_PAD_L = 1

SUBLANES = 8
LANES = 128
N_SEG = SUBLANES
VMEM_LIMIT = 56 * 1024 * 1024


def _cparams(sem, vmem=VMEM_LIMIT):
    return pltpu.CompilerParams(dimension_semantics=sem, vmem_limit_bytes=vmem)


def _sigmoid(x):
    return jax.nn.sigmoid(x)


def _silu(x):
    return x * _sigmoid(x)


def _pack_rows(x):
    w = x.shape[1] // 2
    bits = pltpu.bitcast(x.astype(BF16).astype(F32), U32)
    return (bits[:, :w] >> 16) | (bits[:, w:] & jnp.uint32(0xFFFF0000))


def _unpack_rows(p):
    lo = pltpu.bitcast(p << 16, F32)
    hi = pltpu.bitcast(p & jnp.uint32(0xFFFF0000), F32)
    return jnp.concatenate([lo, hi], axis=1)


def _gelu_tanh(x):
    c = 0.7978845608028654
    return 0.5 * x * (1.0 + jnp.tanh(c * (x + 0.044715 * (x * x * x))))


def _ada_kernel(c_ref, w_ref, b_ref, o_ref):
    s = _silu(c_ref[...])
    o_ref[...] = jnp.dot(s, w_ref[...], preferred_element_type=F32,
                         precision=lax.Precision.HIGHEST) + b_ref[...]


def _ada(cs, w, b):
    rows, d = cs.shape
    n = w.shape[1]
    bn = 1024
    return pl.pallas_call(
        _ada_kernel,
        out_shape=jax.ShapeDtypeStruct((rows, n), F32),
        grid=(n // bn,),
        in_specs=[pl.BlockSpec((rows, d), lambda j: (0, 0)),
                  pl.BlockSpec((d, bn), lambda j: (0, j)),
                  pl.BlockSpec((1, bn), lambda j: (0, j))],
        out_specs=pl.BlockSpec((rows, bn), lambda j: (0, j)),
        compiler_params=_cparams(("arbitrary",)),
        name="ada",
    )(cs, w, b)


def _inproj_kernel(x_ref, mod_ref, gain_ref, w_ref, o_ref):
    x = x_ref[0]
    ms = jnp.mean(x * x, axis=-1, keepdims=True)
    y = x * lax.rsqrt(ms + EPS) * gain_ref[...]
    h = y * (1.0 + mod_ref[0, 1:2, :]) + mod_ref[0, 0:1, :]
    o_ref[0] = jnp.dot(h.astype(BF16), w_ref[...], preferred_element_type=F32)


def _inproj(x, mods, gain, w_bf, tm, shared_mod):
    b, s, d = x.shape
    n = w_bf.shape[1]
    mod_map = (lambda bi, i: (0, 0, 0)) if shared_mod else (lambda bi, i: (bi, 0, 0))
    return pl.pallas_call(
        _inproj_kernel,
        out_shape=jax.ShapeDtypeStruct((b, s, n), F32),
        grid=(b, s // tm),
        in_specs=[pl.BlockSpec((1, tm, d), lambda bi, i: (bi, i, 0)),
                  pl.BlockSpec((1, 8, d), mod_map),
                  pl.BlockSpec((1, d), lambda bi, i: (0, 0)),
                  pl.BlockSpec((d, n), lambda bi, i: (0, 0))],
        out_specs=pl.BlockSpec((1, tm, n), lambda bi, i: (bi, i, 0)),
        compiler_params=_cparams(("arbitrary", "arbitrary")),
        name="inproj",
    )(x, mods, gain, w_bf)


LRU_G = 256
LRU_CHUNK = 256


def _seg_pitch(seg_len):
    return seg_len + SUBLANES


def _lane_store(ref, d, rows, val):
    nl = val.shape[-1] // LANES
    for l in range(nl):
        ref[d * nl + l, rows, :] = val[:, l * LANES:(l + 1) * LANES]


def _lane_load(ref, d, rows, nl):
    return jnp.concatenate([ref[d * nl + l, rows, :] for l in range(nl)], axis=-1)


def _lru_coeffs(pad_ref, t0, rows, cw, cb, wg_ref, bg, sp, a_ref, b_ref, dst0):
    win = pad_ref[pl.ds(t0, rows + 2 * SUBLANES), :]
    u = cb
    for k in range(CONV_W):
        off = SUBLANES - CONV_PAD_L + k
        u = u + win[off:off + rows, :] * cw[k:k + 1, :]
    ub = u.astype(BF16)
    for d in range(2):
        r = _sigmoid(jnp.dot(ub, wg_ref[2 * d, 0], preferred_element_type=F32)
                     + bg[2 * d:2 * d + 1, :])
        ig = _sigmoid(jnp.dot(ub, wg_ref[2 * d + 1, 0], preferred_element_type=F32)
                      + bg[2 * d + 1:2 * d + 2, :])
        log_a = (-LRU_C) * r * sp[d:d + 1, :]
        a = jnp.exp(log_a)
        th = jnp.tanh(log_a)
        one_minus_a2 = (-2.0 * th) / (1.0 - th)
        bb = jnp.sqrt(one_minus_a2) * (ig * u)
        _lane_store(a_ref, d, pl.ds(dst0, rows), a)
        _lane_store(b_ref, d, pl.ds(dst0, rows), bb)


def _seg_scan(a_ref, b_ref, seg_len, pitch, unroll):
    n_lead = a_ref.shape[0]
    nl = n_lead // 2
    zero = jnp.zeros((N_SEG, LANES), F32)
    one = jnp.ones((N_SEG, LANES), F32)
    init = tuple((zero, one) for _ in range(n_lead))

    def step(t, carry):
        out = []
        for i in range(n_lead):
            h, p = carry[i]
            tt = t if i < nl else seg_len - 1 - t
            rows = pl.ds(tt, N_SEG, stride=pitch)
            a = a_ref[i, rows, :]
            h = a * h + b_ref[i, rows, :]
            p = p * a
            b_ref[i, rows, :] = h
            a_ref[i, rows, :] = p
            out.append((h, p))
        return tuple(out)

    def body(i, carry):
        for j in range(unroll):
            carry = step(i * unroll + j, carry)
        return carry

    ends = lax.fori_loop(0, seg_len // unroll, body, init)
    h_end = [jnp.concatenate([ends[d * nl + l][0] for l in range(nl)], axis=-1) for d in range(2)]
    p_end = [jnp.concatenate([ends[d * nl + l][1] for l in range(nl)], axis=-1) for d in range(2)]
    return h_end, p_end


def _seg_carries(h_end, p_end, h0, reverse):
    order = range(N_SEG - 1, -1, -1) if reverse else range(N_SEG)
    cin = [None] * N_SEG
    c = h0
    for s in order:
        cin[s] = c
        c = p_end[s:s + 1, :] * c + h_end[s:s + 1, :]
    return cin, c


def _lru_kernel(rx_ref, rg_ref, rc_ref, cw_ref, cb_ref, wg_ref, bg_ref, lam_ref, o_ref,
                padl, padc, a_l, b_l, a_c, b_c):
    s_len = rx_ref.shape[1]
    c_len = rc_ref.shape[1]
    g = rx_ref.shape[2]
    nl = g // LANES
    seg_l = s_len // N_SEG
    seg_c = c_len // N_SEG
    pitch_l = _seg_pitch(seg_l)
    pitch_c = _seg_pitch(seg_c)

    zeros = jnp.zeros((SUBLANES, g), F32)
    padl[0:SUBLANES, :] = zeros
    padl[SUBLANES + s_len:2 * SUBLANES + s_len, :] = zeros
    padc[0:SUBLANES, :] = zeros
    padc[SUBLANES + c_len:2 * SUBLANES + c_len, :] = zeros
    for i in range(s_len // LRU_CHUNK):
        padl[SUBLANES + i * LRU_CHUNK:SUBLANES + (i + 1) * LRU_CHUNK, :] = (
            rx_ref[0, i * LRU_CHUNK:(i + 1) * LRU_CHUNK, :])
    padc[SUBLANES:SUBLANES + c_len, :] = rc_ref[0]

    cw = cw_ref[...]
    cb = cb_ref[...]
    bg = bg_ref[...]
    x = -lam_ref[...]
    sp = jnp.maximum(x, 0.0) + jnp.log(1.0 + jnp.exp(-jnp.abs(x)))

    _lru_coeffs(padc, 0, c_len, cw, cb, wg_ref, bg, sp, a_l, b_l, 0)
    for i in range(2 * nl):
        for s in range(N_SEG):
            a_c[i, s * pitch_c:s * pitch_c + seg_c, :] = a_l[i, s * seg_c:(s + 1) * seg_c, :]
            b_c[i, s * pitch_c:s * pitch_c + seg_c, :] = b_l[i, s * seg_c:(s + 1) * seg_c, :]
    h_end, p_end = _seg_scan(a_c, b_c, seg_c, pitch_c, unroll=4)
    zero_row = jnp.zeros((1, g), F32)
    _, h0_f = _seg_carries(h_end[0], p_end[0], zero_row, reverse=False)
    _, h0_b = _seg_carries(h_end[1], p_end[1], zero_row, reverse=True)

    def coeff_body(s, carry):
        t0 = pl.multiple_of(s * seg_l, SUBLANES)
        dst = pl.multiple_of(s * pitch_l, SUBLANES)
        _lru_coeffs(padl, t0, seg_l, cw, cb, wg_ref, bg, sp, a_l, b_l, dst)
        return carry

    lax.fori_loop(0, N_SEG, coeff_body, 0)
    h_end, p_end = _seg_scan(a_l, b_l, seg_l, pitch_l, unroll=4)
    cin_f, _ = _seg_carries(h_end[0], p_end[0], h0_f, reverse=False)
    cin_b, _ = _seg_carries(h_end[1], p_end[1], h0_b, reverse=True)

    for s in range(N_SEG):
        rows = slice(s * pitch_l, s * pitch_l + seg_l)
        h = (_lane_load(b_l, 0, rows, nl) + _lane_load(a_l, 0, rows, nl) * cin_f[s]) + (
            _lane_load(b_l, 1, rows, nl) + _lane_load(a_l, 1, rows, nl) * cin_b[s])
        o_ref[0, s * seg_l:(s + 1) * seg_l, :] = _gelu_tanh(rg_ref[0, s * seg_l:(s + 1) * seg_l, :]) * h


def _lru(projx, projc, cw, cb, wg, bg, lam):
    b, s_len, _ = projx.shape
    c_len = projc.shape[1]
    w = cw.shape[1]
    g = LRU_G
    ng = w // g
    seg_l = s_len // N_SEG
    seg_c = c_len // N_SEG
    assert seg_l == LRU_CHUNK and c_len % (N_SEG * 1) == 0
    return pl.pallas_call(
        _lru_kernel,
        out_shape=jax.ShapeDtypeStruct((b, s_len, w), F32),
        grid=(b, ng),
        in_specs=[pl.BlockSpec((1, s_len, g), lambda bi, j: (bi, 0, j)),
                  pl.BlockSpec((1, s_len, g), lambda bi, j: (bi, 0, ng + j)),
                  pl.BlockSpec((1, c_len, g), lambda bi, j: (bi, 0, j)),
                  pl.BlockSpec((CONV_W, g), lambda bi, j: (0, j)),
                  pl.BlockSpec((1, g), lambda bi, j: (0, j)),
                  pl.BlockSpec((4, 1, g, g), lambda bi, j: (0, j, 0, 0)),
                  pl.BlockSpec((4, g), lambda bi, j: (0, j)),
                  pl.BlockSpec((2, g), lambda bi, j: (0, j))],
        out_specs=pl.BlockSpec((1, s_len, g), lambda bi, j: (bi, 0, j)),
        scratch_shapes=[pltpu.VMEM((s_len + 2 * SUBLANES, g), F32),
                        pltpu.VMEM((c_len + 2 * SUBLANES, g), F32),
                        pltpu.VMEM((2 * g // LANES, N_SEG * _seg_pitch(seg_l), LANES), F32),
                        pltpu.VMEM((2 * g // LANES, N_SEG * _seg_pitch(seg_l), LANES), F32),
                        pltpu.VMEM((2 * g // LANES, N_SEG * _seg_pitch(seg_c), LANES), F32),
                        pltpu.VMEM((2 * g // LANES, N_SEG * _seg_pitch(seg_c), LANES), F32)],
        compiler_params=_cparams(("arbitrary", "arbitrary")),
        name="lru",
    )(projx, projx, projc, cw, cb, wg, bg, lam)


HG_PITCH_L = GRID_W + SUBLANES
HG_PITCH_C = SUBLANES


def _hg_gates(z, lb):
    e = jnp.exp(-jnp.abs(z))
    s = 1.0 / (1.0 + e)
    es = e * s
    pos = z >= 0.0
    sig = jnp.where(pos, s, es)
    nsig = jnp.where(pos, es, s)
    logf = jnp.log(lb + (1.0 - lb) * sig)
    k = (1.0 - lb) * nsig
    return logf, k


def _hg_prepass(load, width, pitch, lb, qd, kd, kl, vs, dec):
    n_pos = HG_CHUNK
    for d, zname in enumerate(("zf", "zb")):
        order = range(n_pos) if d == 0 else range(n_pos - 1, -1, -1)
        g = jnp.zeros((width, LANES), F32)
        lbd = lb[d:d + 1, :]
        for p in order:
            logf, k = _hg_gates(load(zname, p), lbd)
            g = g + logf
            qd[d, p * pitch:p * pitch + width, :] = g
            kl[d, p * pitch:p * pitch + width, :] = k
        g_last = g
        dec[d, 0:width, :] = jnp.exp(g_last)
        for p in range(n_pos):
            rows = slice(p * pitch, p * pitch + width)
            gp = qd[d, rows, :]
            k = kl[d, rows, :]
            qd[d, rows, :] = load("q", p) * jnp.exp(gp)
            kd[d, rows, :] = k * jnp.exp(-gp)
            kl[d, rows, :] = k * jnp.exp(g_last - gp)
    for p in range(n_pos):
        vs[p * pitch:p * pitch + width, :] = load("v", p)


def _hg_chunk(d, w, pitch, qd, kd, kl, vs, dec, st, masks, o_ref):
    rows = pl.ds(w, HG_CHUNK, stride=pitch)
    q = qd[d, rows, :].astype(BF16)
    k = kd[d, rows, :].astype(BF16)
    kls = kl[d, rows, :].astype(BF16)
    v = vs[rows, :].astype(BF16)
    dc = dec[d, pl.ds(w, 1), :]
    sc = lax.dot_general(q, k, (((1,), (1,)), ((), ())), preferred_element_type=F32)
    sc = jnp.where(masks[d], sc, 0.0)
    s_t = st[d]
    if o_ref is not None:
        o = jnp.dot(sc.astype(BF16), v, preferred_element_type=F32)
        o = o + lax.dot_general(q, s_t.astype(BF16), (((1,), (1,)), ((), ())),
                                preferred_element_type=F32)
        o_ref[d, rows, :] = o
    kv_t = lax.dot_general(v, kls, (((0,), (0,)), ((), ())), preferred_element_type=F32)
    st[d] = s_t * dc + kv_t


def _hgrn_kernel(q_ref, v_ref, zf_ref, zb_ref, g_ref, qc_ref, vc_ref, zfc_ref, zbc_ref,
                 lbl_ref, gain_ref, o_ref, qd, kd, kl, vs, dec, obuf, st):
    s_len = q_ref.shape[1]
    c_len = qc_ref.shape[1]
    n_col = s_len // HG_CHUNK
    n_cc = c_len // HG_CHUNK
    assert n_col == GRID_W and n_cc == HG_PITCH_C

    lg = lbl_ref[...]
    m = jnp.max(lg, axis=1, keepdims=True)
    ex = jnp.exp(lg - m)
    lb = ex[:, 0, :] / jnp.sum(ex, axis=1)

    ii = lax.broadcasted_iota(I32, (HG_CHUNK, HG_CHUNK), 0)
    jj = lax.broadcasted_iota(I32, (HG_CHUNK, HG_CHUNK), 1)
    masks = (jj <= ii, jj >= ii)

    st[...] = jnp.zeros(st.shape, F32)

    c_refs = {"q": qc_ref, "v": vc_ref, "zf": zfc_ref, "zb": zbc_ref}

    def load_c(name, p):
        return c_refs[name][0, pl.ds(p, n_cc, stride=HG_CHUNK), :]

    _hg_prepass(load_c, n_cc, HG_PITCH_C, lb, qd, kd, kl, vs, dec)

    def ctx_body(n, carry):
        _hg_chunk(0, n, HG_PITCH_C, qd, kd, kl, vs, dec, st, masks, None)
        _hg_chunk(1, n_cc - 1 - n, HG_PITCH_C, qd, kd, kl, vs, dec, st, masks, None)
        return carry

    lax.fori_loop(0, n_cc, ctx_body, 0)

    l_refs = {"q": q_ref, "v": v_ref, "zf": zf_ref, "zb": zb_ref}

    def load_l(name, p):
        return l_refs[name][0, p * n_col:(p + 1) * n_col, :]

    _hg_prepass(load_l, n_col, HG_PITCH_L, lb, qd, kd, kl, vs, dec)

    def lat_body(i, carry):
        _hg_chunk(0, i, HG_PITCH_L, qd, kd, kl, vs, dec, st, masks, obuf)
        _hg_chunk(1, n_col - 1 - i, HG_PITCH_L, qd, kd, kl, vs, dec, st, masks, obuf)
        return carry

    lax.fori_loop(0, n_col, lat_body, 0)

    gain = gain_ref[...]
    for r in range(HG_CHUNK):
        rows = slice(r * HG_PITCH_L, r * HG_PITCH_L + n_col)
        o = obuf[0, rows, :] + obuf[1, rows, :]
        y = o * lax.rsqrt(jnp.mean(o * o, axis=-1, keepdims=True) + EPS) * gain
        o_ref[0, r * n_col:(r + 1) * n_col, :] = y * _silu(g_ref[0, r * n_col:(r + 1) * n_col, :])


def _hgrn(projx, projc, lbl, gain, col0):
    b, s_len, _ = projx.shape
    c_len = projc.shape[1]
    hw = gain.shape[1]
    nh = hw // LANES
    n_slots = lbl.shape[1]

    def xs(k):
        return pl.BlockSpec((1, s_len, LANES), lambda bi, h, k=k: (bi, 0, col0 + k * nh + h))

    def cs(k):
        return pl.BlockSpec((1, c_len, LANES), lambda bi, h, k=k: (bi, 0, col0 + k * nh + h))

    sc_rows = HG_CHUNK * HG_PITCH_L
    return pl.pallas_call(
        _hgrn_kernel,
        out_shape=jax.ShapeDtypeStruct((b, s_len, hw), F32),
        grid=(b, nh),
        in_specs=[xs(0), xs(1), xs(2), xs(3), xs(4), cs(0), cs(1), cs(2), cs(3),
                  pl.BlockSpec((2, n_slots, LANES), lambda bi, h: (0, 0, h)),
                  pl.BlockSpec((1, LANES), lambda bi, h: (0, h))],
        out_specs=pl.BlockSpec((1, s_len, LANES), lambda bi, h: (bi, 0, h)),
        scratch_shapes=[pltpu.VMEM((2, sc_rows, LANES), F32),
                        pltpu.VMEM((2, sc_rows, LANES), F32),
                        pltpu.VMEM((2, sc_rows, LANES), F32),
                        pltpu.VMEM((sc_rows, LANES), F32),
                        pltpu.VMEM((2, HG_PITCH_L, LANES), F32),
                        pltpu.VMEM((2, sc_rows, LANES), F32),
                        pltpu.VMEM((2, LANES, LANES), F32)],
        compiler_params=_cparams(("arbitrary", "arbitrary")),
        name="hgrn",
    )(projx, projx, projx, projx, projx, projc, projc, projc, projc, lbl, gain)


def _first_index_of_max(vals, iota, big):
    m = jnp.max(vals, axis=0, keepdims=True)
    idx = jnp.min(jnp.where(vals == m, iota, big), axis=0, keepdims=True)
    return m, idx


def _route(logits_t, rb, n_experts):
    t = logits_t.shape[1]
    gsz = n_experts // N_GROUPS
    neg = -jnp.inf
    scores = _sigmoid(logits_t)
    biased = scores + rb
    iota_g = lax.broadcasted_iota(I32, (gsz, t), 0)
    gscore = []
    for gi in range(N_GROUPS):
        blk = biased[gi * gsz:(gi + 1) * gsz, :]
        m1, i1 = _first_index_of_max(blk, iota_g, gsz)
        m2 = jnp.max(jnp.where(iota_g == i1, neg, blk), axis=0, keepdims=True)
        gscore.append(m1 + m2)
    gs = jnp.concatenate(gscore, axis=0)
    iota_n = lax.broadcasted_iota(I32, (N_GROUPS, t), 0)
    gsel = jnp.zeros((N_GROUPS, t), jnp.bool_)
    for _ in range(TOPK_GROUPS):
        _, gi1 = _first_index_of_max(gs, iota_n, N_GROUPS)
        hit = iota_n == gi1
        gsel = jnp.logical_or(gsel, hit)
        gs = jnp.where(hit, neg, gs)
    emask = jnp.concatenate(
        [jnp.broadcast_to(gsel[gi:gi + 1, :], (gsz, t)) for gi in range(N_GROUPS)], axis=0)
    masked = jnp.where(emask, biased, neg)
    iota_e = lax.broadcasted_iota(I32, (n_experts, t), 0)
    ids, gates = [], []
    sel = jnp.zeros((n_experts, t), F32)
    for _ in range(TOP_K):
        _, ei = _first_index_of_max(masked, iota_e, n_experts)
        hit = iota_e == ei
        ids.append(ei)
        gates.append(jnp.sum(jnp.where(hit, scores, 0.0), axis=0, keepdims=True))
        sel = jnp.where(hit, 1.0, sel)
        masked = jnp.where(hit, neg, masked)
    ids = jnp.concatenate(ids, axis=0)
    gates = jnp.concatenate(gates, axis=0)
    gates = gates / jnp.sum(gates, axis=0, keepdims=True) * ROUTE_SCALE
    return ids, gates, sel


def _outproj_kernel(yl_ref, yh_ref, x_ref, mod_ref, gain_ref, wo_ref, rwt_ref, rb_ref,
                    x1_ref, h2_ref, eid_ref, gate_ref, rank_ref, cnt_ref, carry):
    first = jnp.logical_and(pl.program_id(0) == 0, pl.program_id(1) == 0)

    @pl.when(first)
    def _():
        carry[...] = jnp.zeros(carry.shape, F32)

    y = jnp.dot(yl_ref[0].astype(BF16), wo_ref[0], preferred_element_type=F32)
    y = y + jnp.dot(yh_ref[0].astype(BF16), wo_ref[1], preferred_element_type=F32)
    x1 = x_ref[0] + mod_ref[0, 2:3, :] * y
    x1_ref[0] = x1
    ms = jnp.mean(x1 * x1, axis=-1, keepdims=True)
    h2 = x1 * lax.rsqrt(ms + EPS) * gain_ref[...]
    h2 = h2 * (1.0 + mod_ref[0, 4:5, :]) + mod_ref[0, 3:4, :]
    h2_ref[...] = _pack_rows(h2)

    n_experts = rwt_ref.shape[0]
    t = h2.shape[0]
    logits_t = lax.dot_general(rwt_ref[...], h2, (((1,), (1,)), ((), ())),
                               preferred_element_type=F32,
                               precision=lax.Precision.HIGHEST)
    ids, gates, sel = _route(logits_t, rb_ref[...], n_experts)
    eid_ref[...] = ids
    gate_ref[...] = gates

    ti = lax.broadcasted_iota(I32, (t, t), 0)
    tj = lax.broadcasted_iota(I32, (t, t), 1)
    upper = (ti < tj).astype(BF16)
    selb = sel.astype(BF16)
    before = jnp.dot(selb, upper, preferred_element_type=F32) + carry[:, 0:1]
    iota_e = lax.broadcasted_iota(I32, (n_experts, t), 0)
    ranks = [jnp.sum(jnp.where(iota_e == ids[k:k + 1, :], before, 0.0), axis=0, keepdims=True)
             for k in range(TOP_K)]
    rank_ref[...] = jnp.concatenate(ranks, axis=0).astype(I32)
    total = carry[...] + jnp.dot(selb, jnp.ones((t, LANES), BF16), preferred_element_type=F32)
    carry[...] = total
    cnt_ref[...] = total.astype(I32)


def _outproj(ylru, yhg, x, mods, gain, wo, rwt, rb, tm):
    b, s, d = x.shape
    hw = ylru.shape[2]
    e = rwt.shape[0]
    n = b * s
    nt = s // tm
    tok = lambda bi, i: (0, bi * nt + i)
    return pl.pallas_call(
        _outproj_kernel,
        out_shape=(jax.ShapeDtypeStruct((b, s, d), F32),
                   jax.ShapeDtypeStruct((n, d // 2), U32),
                   jax.ShapeDtypeStruct((TOP_K, n), I32),
                   jax.ShapeDtypeStruct((TOP_K, n), F32),
                   jax.ShapeDtypeStruct((TOP_K, n), I32),
                   jax.ShapeDtypeStruct((e, LANES), I32)),
        grid=(b, nt),
        in_specs=[pl.BlockSpec((1, tm, hw), lambda bi, i: (bi, i, 0)),
                  pl.BlockSpec((1, tm, hw), lambda bi, i: (bi, i, 0)),
                  pl.BlockSpec((1, tm, d), lambda bi, i: (bi, i, 0)),
                  pl.BlockSpec((1, 8, d), lambda bi, i: (bi, 0, 0)),
                  pl.BlockSpec((1, d), lambda bi, i: (0, 0)),
                  pl.BlockSpec((2, hw, d), lambda bi, i: (0, 0, 0)),
                  pl.BlockSpec((e, d), lambda bi, i: (0, 0)),
                  pl.BlockSpec((e, 1), lambda bi, i: (0, 0))],
        out_specs=(pl.BlockSpec((1, tm, d), lambda bi, i: (bi, i, 0)),
                   pl.BlockSpec((tm, d // 2), lambda bi, i: (bi * nt + i, 0)),
                   pl.BlockSpec((TOP_K, tm), tok),
                   pl.BlockSpec((TOP_K, tm), tok),
                   pl.BlockSpec((TOP_K, tm), tok),
                   pl.BlockSpec((e, LANES), lambda bi, i: (0, 0))),
        scratch_shapes=[pltpu.VMEM((e, LANES), F32)],
        compiler_params=_cparams(("arbitrary", "arbitrary")),
        name="outproj",
    )(ylru, yhg, x, mods, gain, wo, rwt, rb)


def _row_copy(src, dst, sem):
    return pltpu.make_async_copy(src, dst, sem)


def _pos_kernel(pstart, eid_ref, rank_ref, pos_ref):
    eid = eid_ref[...]

    def body(e, acc):
        return jnp.where(eid == e, pstart[e], acc)

    base = lax.fori_loop(0, pstart.shape[0], body, jnp.zeros(eid.shape, I32))
    pos_ref[...] = base + rank_ref[...]


def _pos(pstarts, eid_t, rank_t, tn):
    k, n = eid_t.shape
    spec = pl.BlockSpec((k, tn), lambda i, ps: (0, i))
    return pl.pallas_call(
        _pos_kernel,
        out_shape=jax.ShapeDtypeStruct((k, n), I32),
        grid_spec=pltpu.PrefetchScalarGridSpec(
            num_scalar_prefetch=1, grid=(n // tn,), in_specs=[spec, spec], out_specs=spec),
        compiler_params=_cparams(("arbitrary",)),
        name="pos",
    )(pstarts, eid_t, rank_t)


def _pad_fill(padstart, padlen, zbuf, xs_ref, sem, wait):
    nbits = zbuf.shape[0].bit_length() - 1
    low_bits = SUBLANES.bit_length() - 1

    def go(cp):
        if wait:
            cp.wait()
        else:
            cp.start()

    def body(e, carry):
        st = padstart[e]
        ln = padlen[e]
        end = st + ln
        for bit in range(nbits - 1, low_bits - 1, -1):
            size = 1 << bit
            back = ((ln >> (bit + 1)) << (bit + 1)) + size

            @pl.when((ln & size) != 0)
            def _():
                start = pl.multiple_of(end - back, SUBLANES)
                go(pltpu.make_async_copy(zbuf.at[pl.ds(0, size)],
                                         xs_ref.at[pl.ds(start, size)], sem))
        for j in range(SUBLANES - 1):
            @pl.when(j < (ln & (SUBLANES - 1)))
            def _():
                go(pltpu.make_async_copy(zbuf.at[pl.ds(0, 1)], xs_ref.at[pl.ds(st + j, 1)], sem))
        return carry

    lax.fori_loop(0, padstart.shape[0], body, 0)


def _dispatch_kernel(padstart, padlen, pos_ref, h_ref, xs_ref, zbuf, sem, zsem):
    td = h_ref.shape[0]

    @pl.when(pl.program_id(0) == 0)
    def _():
        zbuf[...] = jnp.zeros(zbuf.shape, zbuf.dtype)
        _pad_fill(padstart, padlen, zbuf, xs_ref, zsem, wait=False)
        _pad_fill(padstart, padlen, zbuf, xs_ref, zsem, wait=True)

    def body(j, carry):
        src = h_ref.at[pl.ds(j, 1)]
        for k in range(TOP_K):
            _row_copy(src, xs_ref.at[pl.ds(pos_ref[0, 0, k * td + j], 1)], sem).start()
        return carry

    lax.fori_loop(0, td, body, 0)
    def wait_body(j, carry):
        for k in range(TOP_K):
            _row_copy(h_ref.at[pl.ds(0, 1)], xs_ref.at[pl.ds(0, 1)], sem).wait()
        return carry

    lax.fori_loop(0, td, wait_body, 0)


def _dispatch(h2, pos_tiles, padstart, padlen, cap, bm, td):
    n, dw = h2.shape
    nt = n // td
    return pl.pallas_call(
        _dispatch_kernel,
        out_shape=jax.ShapeDtypeStruct((cap, dw), h2.dtype),
        grid_spec=pltpu.PrefetchScalarGridSpec(
            num_scalar_prefetch=2,
            grid=(nt,),
            in_specs=[pl.BlockSpec((1, 1, TOP_K * td), lambda i, ps, pn: (i, 0, 0),
                                   memory_space=pltpu.SMEM),
                      pl.BlockSpec((td, dw), lambda i, ps, pn: (i, 0))],
            out_specs=pl.BlockSpec(memory_space=pl.ANY),
            scratch_shapes=[pltpu.VMEM((bm, dw), h2.dtype),
                            pltpu.SemaphoreType.DMA(()), pltpu.SemaphoreType.DMA(())]),
        compiler_params=_cparams(("arbitrary",)),
        name="dispatch",
    )(padstart, padlen, pos_tiles, h2)


def _gmm_kernel(blk_e, nused, xs_ref, w13_ref, w2_ref, ys_ref, w13b, w2b):
    i = pl.program_id(0)
    de = w2_ref.shape[1]

    @pl.when(i < nused[0])
    def _():
        prev = blk_e[jnp.maximum(i - 1, 0)]
        new_expert = jnp.logical_or(i == 0, blk_e[i] != prev)

        @pl.when(new_expert)
        def _():
            w13b[...] = w13_ref[0].astype(BF16)
            w2b[...] = w2_ref[0].astype(BF16)

        xb = _unpack_rows(xs_ref[...]).astype(BF16)
        u = jnp.dot(xb, w13b[...], preferred_element_type=F32)
        hmid = _silu(u[:, :de]) * u[:, de:]
        ys_ref[...] = _pack_rows(jnp.dot(hmid.astype(BF16), w2b[...], preferred_element_type=F32))


def _gmm(xs, w13, w2, blk_e, nused, bm):
    cap, dw = xs.shape
    e, d, de2 = w13.shape
    de = w2.shape[1]
    nb = cap // bm

    def row_map(i, be, nu):
        return (jnp.minimum(i, nu[0] - 1), 0)

    def w_map(i, be, nu):
        return (be[jnp.minimum(i, nu[0] - 1)], 0, 0)

    return pl.pallas_call(
        _gmm_kernel,
        out_shape=jax.ShapeDtypeStruct((cap, dw), U32),
        grid_spec=pltpu.PrefetchScalarGridSpec(
            num_scalar_prefetch=2,
            grid=(nb,),
            in_specs=[pl.BlockSpec((bm, dw), row_map),
                      pl.BlockSpec((1, d, de2), w_map),
                      pl.BlockSpec((1, de, d), w_map)],
            out_specs=pl.BlockSpec((bm, dw), row_map),
            scratch_shapes=[pltpu.VMEM((d, de2), BF16), pltpu.VMEM((de, d), BF16)]),
        compiler_params=_cparams(("arbitrary",)),
        name="gmm",
    )(blk_e, nused, xs, w13, w2)


def _combine_kernel(pos_ref, gate_ref, x1_ref, h2_ref, mod_ref, gain_ref, sw13_ref, sw2_ref,
                    ys_ref, o_ref, ybuf, sem):
    tc = x1_ref.shape[0]
    ds_ = sw2_ref.shape[0]

    def body(j, carry):
        for k in range(TOP_K):
            _row_copy(ys_ref.at[pl.ds(pos_ref[0, 0, k * tc + j], 1)],
                      ybuf.at[k, pl.ds(j, 1)], sem).start()
        return carry

    lax.fori_loop(0, tc, body, 0)

    hb = _unpack_rows(h2_ref[...]).astype(BF16)
    u = jnp.dot(hb, sw13_ref[...], preferred_element_type=F32)
    hmid = _silu(u[:, :ds_]) * u[:, ds_:]
    y = jnp.dot(hmid.astype(BF16), sw2_ref[...], preferred_element_type=F32)

    def wait_body(j, carry):
        for k in range(TOP_K):
            _row_copy(ys_ref.at[pl.ds(0, 1)], ybuf.at[0, pl.ds(0, 1)], sem).wait()
        return carry

    lax.fori_loop(0, tc, wait_body, 0)

    gate = gate_ref[...]
    moe = gate[:, 0:1] * _unpack_rows(ybuf[0])
    for k in range(1, TOP_K):
        moe = moe + gate[:, k:k + 1] * _unpack_rows(ybuf[k])
    xo = x1_ref[...] + mod_ref[0, 5:6, :] * (moe + y)
    ms = jnp.mean(xo * xo, axis=-1, keepdims=True)
    o_ref[...] = xo * lax.rsqrt(ms + EPS) * gain_ref[...]


def _combine(pos_tiles, gate_tok, x1, h2, mods, gain, sw13, sw2, ys, tc, tiles_per_batch):
    n, d = x1.shape
    dw = h2.shape[1]
    nt = n // tc
    ds2 = sw13.shape[1]
    ds_ = sw2.shape[0]
    return pl.pallas_call(
        _combine_kernel,
        out_shape=jax.ShapeDtypeStruct((n, d), F32),
        grid=(nt,),
        in_specs=[pl.BlockSpec((1, 1, TOP_K * tc), lambda i: (i, 0, 0), memory_space=pltpu.SMEM),
                  pl.BlockSpec((tc, TOP_K), lambda i: (i, 0)),
                  pl.BlockSpec((tc, d), lambda i: (i, 0)),
                  pl.BlockSpec((tc, dw), lambda i: (i, 0)),
                  pl.BlockSpec((1, 8, d), lambda i: (i // tiles_per_batch, 0, 0)),
                  pl.BlockSpec((1, d), lambda i: (0, 0)),
                  pl.BlockSpec((d, ds2), lambda i: (0, 0)),
                  pl.BlockSpec((ds_, d), lambda i: (0, 0)),
                  pl.BlockSpec(memory_space=pl.ANY)],
        out_specs=pl.BlockSpec((tc, d), lambda i: (i, 0)),
        scratch_shapes=[pltpu.VMEM((TOP_K, tc, dw), U32), pltpu.SemaphoreType.DMA(())],
        compiler_params=_cparams(("arbitrary",)),
        name="combine",
    )(pos_tiles, gate_tok, x1, h2, mods, gain, sw13, sw2, ys)


def _block_diag_pairs(w, per):
    nb, bd, _ = w.shape
    w = w.reshape(nb // per, per, bd, bd)
    eye = jnp.eye(per, dtype=w.dtype)
    out = jnp.einsum("gpij,pq->gpiqj", w, eye)
    return out.reshape(nb // per, per * bd, per * bd)


def _pos_tiles(pos, t):
    k, n = pos.shape
    return pos.reshape(k, n // t, t).transpose(1, 0, 2).reshape(n // t, 1, k * t)


def kernel(x, c, ctx, c_ctx, ada_w, ada_b, norm_mix, norm_ffn, norm_final, w_in, w_out,
           lru_conv_w, lru_conv_b, lru_wa, lru_ba, lru_wx, lru_bx, lru_lambda,
           hgrn_lb_logits, hgrn_norm, router_w, router_b, exp_w13, exp_w2, shared_w13, shared_w2):
    assert ada_w.shape[0] == 1, "single-layer block"
    b, s, d = x.shape
    n = b * s
    lru_w = lru_conv_w.shape[2]
    hg_w = hgrn_norm.shape[1]
    n_experts = router_w.shape[2]

    rows = -(-(b + 1) // SUBLANES) * SUBLANES
    cs = jnp.zeros((rows, d), F32).at[:b].set(c).at[b].set(c_ctx)
    mod = _ada(cs, ada_w[0], ada_b[0][None, :]).reshape(rows, 6, d)
    mods = jnp.pad(mod, ((0, 0), (0, 2), (0, 0)))

    w_in_bf = w_in[0].astype(BF16)
    gain1 = norm_mix[0][None, :]
    projx = _inproj(x, mods, gain1, w_in_bf, 512, shared_mod=False)
    projc = _inproj(ctx, mods[b:b + 1], gain1, w_in_bf, ctx.shape[1], shared_mod=True)

    per = LRU_G // (lru_w // LRU_BLOCKS)
    wg = jnp.stack([_block_diag_pairs(lru_wa[0, 0], per), _block_diag_pairs(lru_wx[0, 0], per),
                    _block_diag_pairs(lru_wa[0, 1], per), _block_diag_pairs(lru_wx[0, 1], per)]
                   ).astype(BF16)
    bg = jnp.stack([lru_ba[0, 0], lru_bx[0, 0], lru_ba[0, 1], lru_bx[0, 1]])
    ylru = _lru(projx, projc, lru_conv_w[0], lru_conv_b[0][None, :], wg, bg, lru_lambda[0])

    yhg = _hgrn(projx, projc, hgrn_lb_logits, hgrn_norm[0][None, :], (2 * lru_w) // LANES)

    wo = w_out[0].astype(BF16).reshape(2, lru_w, d)
    x1, h2, eid_t, gate_t, rank_t, cnt = _outproj(
        ylru, yhg, x, mods, norm_ffn[0][None, :], wo,
        router_w[0].T, router_b[0][:, None], 512)

    bm = 256
    counts = cnt[:, 0]
    padded = (counts + bm - 1) // bm * bm
    pends = jnp.cumsum(padded)
    pstarts = pends - padded
    pos = _pos(pstarts, eid_t, rank_t, 2048)
    nb = (n * TOP_K) // bm + n_experts
    cap = nb * bm
    blk_first = jnp.arange(nb, dtype=I32) * bm
    blk_e = jnp.minimum(jnp.sum(pends[None, :] <= blk_first[:, None], axis=1),
                        n_experts - 1).astype(I32)
    nused = (pends[-1] // bm).astype(I32)[None]

    td = 256
    xs = _dispatch(h2, _pos_tiles(pos, td), pstarts + counts, padded - counts, cap, bm, td)
    ys = _gmm(xs, exp_w13[0], exp_w2[0], blk_e, nused, bm)

    tc = 128
    out = _combine(_pos_tiles(pos, tc), gate_t.T, x1.reshape(n, d), h2, mods,
                   norm_final[None, :], shared_w13[0].astype(BF16), shared_w2[0].astype(BF16),
                   ys, tc, s // tc)
    return out.reshape(b, s, d)
```

```python
import functools

import jax
import jax.numpy as jnp
from jax import lax
from jax.experimental import pallas as pl
from jax.experimental.pallas import tpu as pltpu
from jax.experimental.pallas import tpu_sc as plsc

F32 = jnp.float32
BF16 = jnp.bfloat16
I32 = jnp.int32
U32 = jnp.uint32

EPS = 1e-6
LRU_C = 8.0
ROUTE_SCALE = 2.5
GRID_W = 64
HG_CHUNK = 32
N_GROUPS = 8
TOPK_GROUPS = 4
TOP_K = 8
LRU_BLOCKS = 8
HG_HEADS = 4
CONV_W = 4
CONV_PAD_L = 1

SUBLANES = 8
LANES = 128
N_SEG = SUBLANES
VMEM_LIMIT = 56 * 1024 * 1024


def _cparams(sem, vmem=VMEM_LIMIT):
    return pltpu.CompilerParams(dimension_semantics=sem, vmem_limit_bytes=vmem)


def _sigmoid(x):
    return jax.nn.sigmoid(x)


def _silu(x):
    return x * _sigmoid(x)


def _pack_rows(x):
    w = x.shape[1] // 2
    bits = pltpu.bitcast(x.astype(BF16).astype(F32), U32)
    return (bits[:, :w] >> 16) | (bits[:, w:] & jnp.uint32(0xFFFF0000))


def _unpack_rows(p):
    lo = pltpu.bitcast(p << 16, F32)
    hi = pltpu.bitcast(p & jnp.uint32(0xFFFF0000), F32)
    return jnp.concatenate([lo, hi], axis=1)


def _gelu_tanh(x):
    c = 0.7978845608028654
    return 0.5 * x * (1.0 + jnp.tanh(c * (x + 0.044715 * (x * x * x))))


def _ada_kernel(c_ref, w_ref, b_ref, o_ref):
    s = _silu(c_ref[...])
    o_ref[...] = jnp.dot(s, w_ref[...], preferred_element_type=F32,
                         precision=lax.Precision.HIGHEST) + b_ref[...]


def _ada(cs, w, b):
    rows, d = cs.shape
    n = w.shape[1]
    bn = 1024
    return pl.pallas_call(
        _ada_kernel,
        out_shape=jax.ShapeDtypeStruct((rows, n), F32),
        grid=(n // bn,),
        in_specs=[pl.BlockSpec((rows, d), lambda j: (0, 0)),
                  pl.BlockSpec((d, bn), lambda j: (0, j)),
                  pl.BlockSpec((1, bn), lambda j: (0, j))],
        out_specs=pl.BlockSpec((rows, bn), lambda j: (0, j)),
        compiler_params=_cparams(("arbitrary",)),
        name="ada",
    )(cs, w, b)


def _inproj_kernel(x_ref, mod_ref, gain_ref, w_ref, o_ref):
    x = x_ref[0]
    ms = jnp.mean(x * x, axis=-1, keepdims=True)
    y = x * lax.rsqrt(ms + EPS) * gain_ref[...]
    h = y * (1.0 + mod_ref[0, 1:2, :]) + mod_ref[0, 0:1, :]
    o_ref[0] = jnp.dot(h.astype(BF16), w_ref[...], preferred_element_type=F32)


def _inproj(x, mods, gain, w_bf, tm, shared_mod):
    b, s, d = x.shape
    n = w_bf.shape[1]
    mod_map = (lambda bi, i: (0, 0, 0)) if shared_mod else (lambda bi, i: (bi, 0, 0))
    return pl.pallas_call(
        _inproj_kernel,
        out_shape=jax.ShapeDtypeStruct((b, s, n), F32),
        grid=(b, s // tm),
        in_specs=[pl.BlockSpec((1, tm, d), lambda bi, i: (bi, i, 0)),
                  pl.BlockSpec((1, 8, d), mod_map),
                  pl.BlockSpec((1, d), lambda bi, i: (0, 0)),
                  pl.BlockSpec((d, n), lambda bi, i: (0, 0))],
        out_specs=pl.BlockSpec((1, tm, n), lambda bi, i: (bi, i, 0)),
        compiler_params=_cparams(("arbitrary", "arbitrary")),
        name="inproj",
    )(x, mods, gain, w_bf)


LRU_G = 256
LRU_CHUNK = 256


def _seg_pitch(seg_len):
    return seg_len + SUBLANES


def _lane_store(ref, d, rows, val):
    nl = val.shape[-1] // LANES
    for l in range(nl):
        ref[d * nl + l, rows, :] = val[:, l * LANES:(l + 1) * LANES]


def _lane_load(ref, d, rows, nl):
    return jnp.concatenate([ref[d * nl + l, rows, :] for l in range(nl)], axis=-1)


def _lru_coeffs(pad_ref, t0, rows, cw, cb, wg_ref, bg, sp, a_ref, b_ref, dst0):
    win = pad_ref[pl.ds(t0, rows + 2 * SUBLANES), :]
    u = cb
    for k in range(CONV_W):
        off = SUBLANES - CONV_PAD_L + k
        u = u + win[off:off + rows, :] * cw[k:k + 1, :]
    ub = u.astype(BF16)
    for d in range(2):
        r = _sigmoid(jnp.dot(ub, wg_ref[2 * d, 0], preferred_element_type=F32)
                     + bg[2 * d:2 * d + 1, :])
        ig = _sigmoid(jnp.dot(ub, wg_ref[2 * d + 1, 0], preferred_element_type=F32)
                      + bg[2 * d + 1:2 * d + 2, :])
        log_a = (-LRU_C) * r * sp[d:d + 1, :]
        a = jnp.exp(log_a)
        th = jnp.tanh(log_a)
        one_minus_a2 = (-2.0 * th) / (1.0 - th)
        bb = jnp.sqrt(one_minus_a2) * (ig * u)
        _lane_store(a_ref, d, pl.ds(dst0, rows), a)
        _lane_store(b_ref, d, pl.ds(dst0, rows), bb)


def _seg_scan(a_ref, b_ref, seg_len, pitch, unroll):
    n_lead = a_ref.shape[0]
    nl = n_lead // 2
    zero = jnp.zeros((N_SEG, LANES), F32)
    one = jnp.ones((N_SEG, LANES), F32)
    init = tuple((zero, one) for _ in range(n_lead))

    def step(t, carry):
        out = []
        for i in range(n_lead):
            h, p = carry[i]
            tt = t if i < nl else seg_len - 1 - t
            rows = pl.ds(tt, N_SEG, stride=pitch)
            a = a_ref[i, rows, :]
            h = a * h + b_ref[i, rows, :]
            p = p * a
            b_ref[i, rows, :] = h
            a_ref[i, rows, :] = p
            out.append((h, p))
        return tuple(out)

    def body(i, carry):
        for j in range(unroll):
            carry = step(i * unroll + j, carry)
        return carry

    ends = lax.fori_loop(0, seg_len // unroll, body, init)
    h_end = [jnp.concatenate([ends[d * nl + l][0] for l in range(nl)], axis=-1) for d in range(2)]
    p_end = [jnp.concatenate([ends[d * nl + l][1] for l in range(nl)], axis=-1) for d in range(2)]
    return h_end, p_end


def _seg_carries(h_end, p_end, h0, reverse):
    order = range(N_SEG - 1, -1, -1) if reverse else range(N_SEG)
    cin = [None] * N_SEG
    c = h0
    for s in order:
        cin[s] = c
        c = p_end[s:s + 1, :] * c + h_end[s:s + 1, :]
    return cin, c


def _lru_kernel(rx_ref, rg_ref, rc_ref, cw_ref, cb_ref, wg_ref, bg_ref, lam_ref, o_ref,
                padl, padc, a_l, b_l, a_c, b_c):
    s_len = rx_ref.shape[1]
    c_len = rc_ref.shape[1]
    g = rx_ref.shape[2]
    nl = g // LANES
    seg_l = s_len // N_SEG
    seg_c = c_len // N_SEG
    pitch_l = _seg_pitch(seg_l)
    pitch_c = _seg_pitch(seg_c)

    zeros = jnp.zeros((SUBLANES, g), F32)
    padl[0:SUBLANES, :] = zeros
    padl[SUBLANES + s_len:2 * SUBLANES + s_len, :] = zeros
    padc[0:SUBLANES, :] = zeros
    padc[SUBLANES + c_len:2 * SUBLANES + c_len, :] = zeros
    for i in range(s_len // LRU_CHUNK):
        padl[SUBLANES + i * LRU_CHUNK:SUBLANES + (i + 1) * LRU_CHUNK, :] = (
            rx_ref[0, i * LRU_CHUNK:(i + 1) * LRU_CHUNK, :])
    padc[SUBLANES:SUBLANES + c_len, :] = rc_ref[0]

    cw = cw_ref[...]
    cb = cb_ref[...]
    bg = bg_ref[...]
    x = -lam_ref[...]
    sp = jnp.maximum(x, 0.0) + jnp.log(1.0 + jnp.exp(-jnp.abs(x)))

    _lru_coeffs(padc, 0, c_len, cw, cb, wg_ref, bg, sp, a_l, b_l, 0)
    for i in range(2 * nl):
        for s in range(N_SEG):
            a_c[i, s * pitch_c:s * pitch_c + seg_c, :] = a_l[i, s * seg_c:(s + 1) * seg_c, :]
            b_c[i, s * pitch_c:s * pitch_c + seg_c, :] = b_l[i, s * seg_c:(s + 1) * seg_c, :]
    h_end, p_end = _seg_scan(a_c, b_c, seg_c, pitch_c, unroll=4)
    zero_row = jnp.zeros((1, g), F32)
    _, h0_f = _seg_carries(h_end[0], p_end[0], zero_row, reverse=False)
    _, h0_b = _seg_carries(h_end[1], p_end[1], zero_row, reverse=True)

    def coeff_body(s, carry):
        t0 = pl.multiple_of(s * seg_l, SUBLANES)
        dst = pl.multiple_of(s * pitch_l, SUBLANES)
        _lru_coeffs(padl, t0, seg_l, cw, cb, wg_ref, bg, sp, a_l, b_l, dst)
        return carry

    lax.fori_loop(0, N_SEG, coeff_body, 0)
    h_end, p_end = _seg_scan(a_l, b_l, seg_l, pitch_l, unroll=4)
    cin_f, _ = _seg_carries(h_end[0], p_end[0], h0_f, reverse=False)
    cin_b, _ = _seg_carries(h_end[1], p_end[1], h0_b, reverse=True)

    for s in range(N_SEG):
        rows = slice(s * pitch_l, s * pitch_l + seg_l)
        h = (_lane_load(b_l, 0, rows, nl) + _lane_load(a_l, 0, rows, nl) * cin_f[s]) + (
            _lane_load(b_l, 1, rows, nl) + _lane_load(a_l, 1, rows, nl) * cin_b[s])
        o_ref[0, s * seg_l:(s + 1) * seg_l, :] = _gelu_tanh(rg_ref[0, s * seg_l:(s + 1) * seg_l, :]) * h


def _lru(projx, projc, cw, cb, wg, bg, lam):
    b, s_len, _ = projx.shape
    c_len = projc.shape[1]
    w = cw.shape[1]
    g = LRU_G
    ng = w // g
    seg_l = s_len // N_SEG
    seg_c = c_len // N_SEG
    assert seg_l == LRU_CHUNK and c_len % (N_SEG * 1) == 0
    return pl.pallas_call(
        _lru_kernel,
        out_shape=jax.ShapeDtypeStruct((b, s_len, w), F32),
        grid=(b, ng),
        in_specs=[pl.BlockSpec((1, s_len, g), lambda bi, j: (bi, 0, j)),
                  pl.BlockSpec((1, s_len, g), lambda bi, j: (bi, 0, ng + j)),
                  pl.BlockSpec((1, c_len, g), lambda bi, j: (bi, 0, j)),
                  pl.BlockSpec((CONV_W, g), lambda bi, j: (0, j)),
                  pl.BlockSpec((1, g), lambda bi, j: (0, j)),
                  pl.BlockSpec((4, 1, g, g), lambda bi, j: (0, j, 0, 0)),
                  pl.BlockSpec((4, g), lambda bi, j: (0, j)),
                  pl.BlockSpec((2, g), lambda bi, j: (0, j))],
        out_specs=pl.BlockSpec((1, s_len, g), lambda bi, j: (bi, 0, j)),
        scratch_shapes=[pltpu.VMEM((s_len + 2 * SUBLANES, g), F32),
                        pltpu.VMEM((c_len + 2 * SUBLANES, g), F32),
                        pltpu.VMEM((2 * g // LANES, N_SEG * _seg_pitch(seg_l), LANES), F32),
                        pltpu.VMEM((2 * g // LANES, N_SEG * _seg_pitch(seg_l), LANES), F32),
                        pltpu.VMEM((2 * g // LANES, N_SEG * _seg_pitch(seg_c), LANES), F32),
                        pltpu.VMEM((2 * g // LANES, N_SEG * _seg_pitch(seg_c), LANES), F32)],
        compiler_params=_cparams(("arbitrary", "arbitrary")),
        name="lru",
    )(projx, projx, projc, cw, cb, wg, bg, lam)


HG_PITCH_L = GRID_W + SUBLANES
HG_PITCH_C = SUBLANES


def _hg_gates(z, lb):
    e = jnp.exp(-jnp.abs(z))
    s = 1.0 / (1.0 + e)
    es = e * s
    pos = z >= 0.0
    sig = jnp.where(pos, s, es)
    nsig = jnp.where(pos, es, s)
    logf = jnp.log(lb + (1.0 - lb) * sig)
    k = (1.0 - lb) * nsig
    return logf, k


def _hg_prepass(load, width, pitch, lb, qd, kd, kl, vs, dec):
    n_pos = HG_CHUNK
    for d, zname in enumerate(("zf", "zb")):
        order = range(n_pos) if d == 0 else range(n_pos - 1, -1, -1)
        g = jnp.zeros((width, LANES), F32)
        lbd = lb[d:d + 1, :]
        for p in order:
            logf, k = _hg_gates(load(zname, p), lbd)
            g = g + logf
            qd[d, p * pitch:p * pitch + width, :] = g
            kl[d, p * pitch:p * pitch + width, :] = k
        g_last = g
        dec[d, 0:width, :] = jnp.exp(g_last)
        for p in range(n_pos):
            rows = slice(p * pitch, p * pitch + width)
            gp = qd[d, rows, :]
            k = kl[d, rows, :]
            qd[d, rows, :] = load("q", p) * jnp.exp(gp)
            kd[d, rows, :] = k * jnp.exp(-gp)
            kl[d, rows, :] = k * jnp.exp(g_last - gp)
    for p in range(n_pos):
        vs[p * pitch:p * pitch + width, :] = load("v", p)


def _hg_chunk(d, w, pitch, qd, kd, kl, vs, dec, st, masks, o_ref):
    rows = pl.ds(w, HG_CHUNK, stride=pitch)
    q = qd[d, rows, :].astype(BF16)
    k = kd[d, rows, :].astype(BF16)
    kls = kl[d, rows, :].astype(BF16)
    v = vs[rows, :].astype(BF16)
    dc = dec[d, pl.ds(w, 1), :]
    sc = lax.dot_general(q, k, (((1,), (1,)), ((), ())), preferred_element_type=F32)
    sc = jnp.where(masks[d], sc, 0.0)
    s_t = st[d]
    if o_ref is not None:
        o = jnp.dot(sc.astype(BF16), v, preferred_element_type=F32)
        o = o + lax.dot_general(q, s_t.astype(BF16), (((1,), (1,)), ((), ())),
                                preferred_element_type=F32)
        o_ref[d, rows, :] = o
    kv_t = lax.dot_general(v, kls, (((0,), (0,)), ((), ())), preferred_element_type=F32)
    st[d] = s_t * dc + kv_t


def _hgrn_kernel(q_ref, v_ref, zf_ref, zb_ref, g_ref, qc_ref, vc_ref, zfc_ref, zbc_ref,
                 lbl_ref, gain_ref, o_ref, qd, kd, kl, vs, dec, obuf, st):
    s_len = q_ref.shape[1]
    c_len = qc_ref.shape[1]
    n_col = s_len // HG_CHUNK
    n_cc = c_len // HG_CHUNK
    assert n_col == GRID_W and n_cc == HG_PITCH_C

    lg = lbl_ref[...]
    m = jnp.max(lg, axis=1, keepdims=True)
    ex = jnp.exp(lg - m)
    lb = ex[:, 0, :] / jnp.sum(ex, axis=1)

    ii = lax.broadcasted_iota(I32, (HG_CHUNK, HG_CHUNK), 0)
    jj = lax.broadcasted_iota(I32, (HG_CHUNK, HG_CHUNK), 1)
    masks = (jj <= ii, jj >= ii)

    st[...] = jnp.zeros(st.shape, F32)

    c_refs = {"q": qc_ref, "v": vc_ref, "zf": zfc_ref, "zb": zbc_ref}

    def load_c(name, p):
        return c_refs[name][0, pl.ds(p, n_cc, stride=HG_CHUNK), :]

    _hg_prepass(load_c, n_cc, HG_PITCH_C, lb, qd, kd, kl, vs, dec)

    def ctx_body(n, carry):
        _hg_chunk(0, n, HG_PITCH_C, qd, kd, kl, vs, dec, st, masks, None)
        _hg_chunk(1, n_cc - 1 - n, HG_PITCH_C, qd, kd, kl, vs, dec, st, masks, None)
        return carry

    lax.fori_loop(0, n_cc, ctx_body, 0)

    l_refs = {"q": q_ref, "v": v_ref, "zf": zf_ref, "zb": zb_ref}

    def load_l(name, p):
        return l_refs[name][0, p * n_col:(p + 1) * n_col, :]

    _hg_prepass(load_l, n_col, HG_PITCH_L, lb, qd, kd, kl, vs, dec)

    def lat_body(i, carry):
        _hg_chunk(0, i, HG_PITCH_L, qd, kd, kl, vs, dec, st, masks, obuf)
        _hg_chunk(1, n_col - 1 - i, HG_PITCH_L, qd, kd, kl, vs, dec, st, masks, obuf)
        return carry

    lax.fori_loop(0, n_col, lat_body, 0)

    gain = gain_ref[...]
    for r in range(HG_CHUNK):
        rows = slice(r * HG_PITCH_L, r * HG_PITCH_L + n_col)
        o = obuf[0, rows, :] + obuf[1, rows, :]
        y = o * lax.rsqrt(jnp.mean(o * o, axis=-1, keepdims=True) + EPS) * gain
        o_ref[0, r * n_col:(r + 1) * n_col, :] = y * _silu(g_ref[0, r * n_col:(r + 1) * n_col, :])


def _hgrn(projx, projc, lbl, gain, col0):
    b, s_len, _ = projx.shape
    c_len = projc.shape[1]
    hw = gain.shape[1]
    nh = hw // LANES
    n_slots = lbl.shape[1]

    def xs(k):
        return pl.BlockSpec((1, s_len, LANES), lambda bi, h, k=k: (bi, 0, col0 + k * nh + h))

    def cs(k):
        return pl.BlockSpec((1, c_len, LANES), lambda bi, h, k=k: (bi, 0, col0 + k * nh + h))

    sc_rows = HG_CHUNK * HG_PITCH_L
    return pl.pallas_call(
        _hgrn_kernel,
        out_shape=jax.ShapeDtypeStruct((b, s_len, hw), F32),
        grid=(b, nh),
        in_specs=[xs(0), xs(1), xs(2), xs(3), xs(4), cs(0), cs(1), cs(2), cs(3),
                  pl.BlockSpec((2, n_slots, LANES), lambda bi, h: (0, 0, h)),
                  pl.BlockSpec((1, LANES), lambda bi, h: (0, h))],
        out_specs=pl.BlockSpec((1, s_len, LANES), lambda bi, h: (bi, 0, h)),
        scratch_shapes=[pltpu.VMEM((2, sc_rows, LANES), F32),
                        pltpu.VMEM((2, sc_rows, LANES), F32),
                        pltpu.VMEM((2, sc_rows, LANES), F32),
                        pltpu.VMEM((sc_rows, LANES), F32),
                        pltpu.VMEM((2, HG_PITCH_L, LANES), F32),
                        pltpu.VMEM((2, sc_rows, LANES), F32),
                        pltpu.VMEM((2, LANES, LANES), F32)],
        compiler_params=_cparams(("arbitrary", "arbitrary")),
        name="hgrn",
    )(projx, projx, projx, projx, projx, projc, projc, projc, projc, lbl, gain)


def _first_index_of_max(vals, iota, big):
    m = jnp.max(vals, axis=0, keepdims=True)
    idx = jnp.min(jnp.where(vals == m, iota, big), axis=0, keepdims=True)
    return m, idx


def _route(logits_t, rb, n_experts):
    t = logits_t.shape[1]
    gsz = n_experts // N_GROUPS
    neg = -jnp.inf
    scores = _sigmoid(logits_t)
    biased = scores + rb
    iota_g = lax.broadcasted_iota(I32, (gsz, t), 0)
    gscore = []
    for gi in range(N_GROUPS):
        blk = biased[gi * gsz:(gi + 1) * gsz, :]
        m1, i1 = _first_index_of_max(blk, iota_g, gsz)
        m2 = jnp.max(jnp.where(iota_g == i1, neg, blk), axis=0, keepdims=True)
        gscore.append(m1 + m2)
    gs = jnp.concatenate(gscore, axis=0)
    iota_n = lax.broadcasted_iota(I32, (N_GROUPS, t), 0)
    gsel = jnp.zeros((N_GROUPS, t), jnp.bool_)
    for _ in range(TOPK_GROUPS):
        _, gi1 = _first_index_of_max(gs, iota_n, N_GROUPS)
        hit = iota_n == gi1
        gsel = jnp.logical_or(gsel, hit)
        gs = jnp.where(hit, neg, gs)
    emask = jnp.concatenate(
        [jnp.broadcast_to(gsel[gi:gi + 1, :], (gsz, t)) for gi in range(N_GROUPS)], axis=0)
    masked = jnp.where(emask, biased, neg)
    iota_e = lax.broadcasted_iota(I32, (n_experts, t), 0)
    ids, gates = [], []
    sel = jnp.zeros((n_experts, t), F32)
    for _ in range(TOP_K):
        _, ei = _first_index_of_max(masked, iota_e, n_experts)
        hit = iota_e == ei
        ids.append(ei)
        gates.append(jnp.sum(jnp.where(hit, scores, 0.0), axis=0, keepdims=True))
        sel = jnp.where(hit, 1.0, sel)
        masked = jnp.where(hit, neg, masked)
    ids = jnp.concatenate(ids, axis=0)
    gates = jnp.concatenate(gates, axis=0)
    gates = gates / jnp.sum(gates, axis=0, keepdims=True) * ROUTE_SCALE
    return ids, gates, sel


def _outproj_kernel(yl_ref, yh_ref, x_ref, mod_ref, gain_ref, wo_ref, rwt_ref, rb_ref,
                    x1_ref, h2_ref, eid_ref, gate_ref, rank_ref, cnt_ref, carry):
    first = jnp.logical_and(pl.program_id(0) == 0, pl.program_id(1) == 0)

    @pl.when(first)
    def _():
        carry[...] = jnp.zeros(carry.shape, F32)

    y = jnp.dot(yl_ref[0].astype(BF16), wo_ref[0], preferred_element_type=F32)
    y = y + jnp.dot(yh_ref[0].astype(BF16), wo_ref[1], preferred_element_type=F32)
    x1 = x_ref[0] + mod_ref[0, 2:3, :] * y
    x1_ref[0] = x1
    ms = jnp.mean(x1 * x1, axis=-1, keepdims=True)
    h2 = x1 * lax.rsqrt(ms + EPS) * gain_ref[...]
    h2 = h2 * (1.0 + mod_ref[0, 4:5, :]) + mod_ref[0, 3:4, :]
    h2_ref[...] = _pack_rows(h2)

    n_experts = rwt_ref.shape[0]
    t = h2.shape[0]
    logits_t = lax.dot_general(rwt_ref[...], h2, (((1,), (1,)), ((), ())),
                               preferred_element_type=F32,
                               precision=lax.Precision.HIGHEST)
    ids, gates, sel = _route(logits_t, rb_ref[...], n_experts)
    eid_ref[...] = ids
    gate_ref[...] = gates

    ti = lax.broadcasted_iota(I32, (t, t), 0)
    tj = lax.broadcasted_iota(I32, (t, t), 1)
    upper = (ti < tj).astype(BF16)
    selb = sel.astype(BF16)
    before = jnp.dot(selb, upper, preferred_element_type=F32) + carry[:, 0:1]
    iota_e = lax.broadcasted_iota(I32, (n_experts, t), 0)
    ranks = [jnp.sum(jnp.where(iota_e == ids[k:k + 1, :], before, 0.0), axis=0, keepdims=True)
             for k in range(TOP_K)]
    rank_ref[...] = jnp.concatenate(ranks, axis=0).astype(I32)
    total = carry[...] + jnp.dot(selb, jnp.ones((t, LANES), BF16), preferred_element_type=F32)
    carry[...] = total
    cnt_ref[...] = total.astype(I32)


def _outproj(ylru, yhg, x, mods, gain, wo, rwt, rb, tm):
    b, s, d = x.shape
    hw = ylru.shape[2]
    e = rwt.shape[0]
    n = b * s
    nt = s // tm
    tok = lambda bi, i: (0, bi * nt + i)
    return pl.pallas_call(
        _outproj_kernel,
        out_shape=(jax.ShapeDtypeStruct((b, s, d), F32),
                   jax.ShapeDtypeStruct((n, d // 2), U32),
                   jax.ShapeDtypeStruct((TOP_K, n), I32),
                   jax.ShapeDtypeStruct((TOP_K, n), F32),
                   jax.ShapeDtypeStruct((TOP_K, n), I32),
                   jax.ShapeDtypeStruct((e, LANES), I32)),
        grid=(b, nt),
        in_specs=[pl.BlockSpec((1, tm, hw), lambda bi, i: (bi, i, 0)),
                  pl.BlockSpec((1, tm, hw), lambda bi, i: (bi, i, 0)),
                  pl.BlockSpec((1, tm, d), lambda bi, i: (bi, i, 0)),
                  pl.BlockSpec((1, 8, d), lambda bi, i: (bi, 0, 0)),
                  pl.BlockSpec((1, d), lambda bi, i: (0, 0)),
                  pl.BlockSpec((2, hw, d), lambda bi, i: (0, 0, 0)),
                  pl.BlockSpec((e, d), lambda bi, i: (0, 0)),
                  pl.BlockSpec((e, 1), lambda bi, i: (0, 0))],
        out_specs=(pl.BlockSpec((1, tm, d), lambda bi, i: (bi, i, 0)),
                   pl.BlockSpec((tm, d // 2), lambda bi, i: (bi * nt + i, 0)),
                   pl.BlockSpec((TOP_K, tm), tok),
                   pl.BlockSpec((TOP_K, tm), tok),
                   pl.BlockSpec((TOP_K, tm), tok),
                   pl.BlockSpec((e, LANES), lambda bi, i: (0, 0))),
        scratch_shapes=[pltpu.VMEM((e, LANES), F32)],
        compiler_params=_cparams(("arbitrary", "arbitrary")),
        name="outproj",
    )(ylru, yhg, x, mods, gain, wo, rwt, rb)


def _row_copy(src, dst, sem):
    return pltpu.make_async_copy(src, dst, sem)


def _pos_kernel(pstart, eid_ref, rank_ref, pos_ref):
    eid = eid_ref[...]

    def body(e, acc):
        return jnp.where(eid == e, pstart[e], acc)

    base = lax.fori_loop(0, pstart.shape[0], body, jnp.zeros(eid.shape, I32))
    pos_ref[...] = base + rank_ref[...]


def _pos(pstarts, eid_t, rank_t, tn):
    k, n = eid_t.shape
    spec = pl.BlockSpec((k, tn), lambda i, ps: (0, i))
    return pl.pallas_call(
        _pos_kernel,
        out_shape=jax.ShapeDtypeStruct((k, n), I32),
        grid_spec=pltpu.PrefetchScalarGridSpec(
            num_scalar_prefetch=1, grid=(n // tn,), in_specs=[spec, spec], out_specs=spec),
        compiler_params=_cparams(("arbitrary",)),
        name="pos",
    )(pstarts, eid_t, rank_t)


def _pad_fill(padstart, padlen, zbuf, xs_ref, sem, wait):
    nbits = zbuf.shape[0].bit_length() - 1
    low_bits = SUBLANES.bit_length() - 1

    def go(cp):
        if wait:
            cp.wait()
        else:
            cp.start()

    def body(e, carry):
        st = padstart[e]
        ln = padlen[e]
        end = st + ln
        for bit in range(nbits - 1, low_bits - 1, -1):
            size = 1 << bit
            back = ((ln >> (bit + 1)) << (bit + 1)) + size

            @pl.when((ln & size) != 0)
            def _():
                start = pl.multiple_of(end - back, SUBLANES)
                go(pltpu.make_async_copy(zbuf.at[pl.ds(0, size)],
                                         xs_ref.at[pl.ds(start, size)], sem))
        for j in range(SUBLANES - 1):
            @pl.when(j < (ln & (SUBLANES - 1)))
            def _():
                go(pltpu.make_async_copy(zbuf.at[pl.ds(0, 1)], xs_ref.at[pl.ds(st + j, 1)], sem))
        return carry

    lax.fori_loop(0, padstart.shape[0], body, 0)


def _dispatch_kernel(padstart, padlen, pos_ref, h_ref, xs_ref, zbuf, sem, zsem):
    td = h_ref.shape[0]

    @pl.when(pl.program_id(0) == 0)
    def _():
        zbuf[...] = jnp.zeros(zbuf.shape, zbuf.dtype)
        _pad_fill(padstart, padlen, zbuf, xs_ref, zsem, wait=False)
        _pad_fill(padstart, padlen, zbuf, xs_ref, zsem, wait=True)

    def body(j, carry):
        src = h_ref.at[pl.ds(j, 1)]
        for k in range(TOP_K):
            _row_copy(src, xs_ref.at[pl.ds(pos_ref[0, 0, k * td + j], 1)], sem).start()
        return carry

    lax.fori_loop(0, td, body, 0)
    def wait_body(j, carry):
        for k in range(TOP_K):
            _row_copy(h_ref.at[pl.ds(0, 1)], xs_ref.at[pl.ds(0, 1)], sem).wait()
        return carry

    lax.fori_loop(0, td, wait_body, 0)


def _dispatch(h2, pos_tiles, padstart, padlen, cap, bm, td):
    n, dw = h2.shape
    nt = n // td
    return pl.pallas_call(
        _dispatch_kernel,
        out_shape=jax.ShapeDtypeStruct((cap, dw), h2.dtype),
        grid_spec=pltpu.PrefetchScalarGridSpec(
            num_scalar_prefetch=2,
            grid=(nt,),
            in_specs=[pl.BlockSpec((1, 1, TOP_K * td), lambda i, ps, pn: (i, 0, 0),
                                   memory_space=pltpu.SMEM),
                      pl.BlockSpec((td, dw), lambda i, ps, pn: (i, 0))],
            out_specs=pl.BlockSpec(memory_space=pl.ANY),
            scratch_shapes=[pltpu.VMEM((bm, dw), h2.dtype),
                            pltpu.SemaphoreType.DMA(()), pltpu.SemaphoreType.DMA(())]),
        compiler_params=_cparams(("arbitrary",)),
        name="dispatch",
    )(padstart, padlen, pos_tiles, h2)


def _gmm_kernel(blk_e, nused, xs_ref, w13_ref, w2_ref, ys_ref, w13b, w2b):
    i = pl.program_id(0)
    de = w2_ref.shape[1]

    @pl.when(i < nused[0])
    def _():
        prev = blk_e[jnp.maximum(i - 1, 0)]
        new_expert = jnp.logical_or(i == 0, blk_e[i] != prev)

        @pl.when(new_expert)
        def _():
            w13b[...] = w13_ref[0].astype(BF16)
            w2b[...] = w2_ref[0].astype(BF16)

        xb = _unpack_rows(xs_ref[...]).astype(BF16)
        u = jnp.dot(xb, w13b[...], preferred_element_type=F32)
        hmid = _silu(u[:, :de]) * u[:, de:]
        ys_ref[...] = _pack_rows(jnp.dot(hmid.astype(BF16), w2b[...], preferred_element_type=F32))


def _gmm(xs, w13, w2, blk_e, nused, bm):
    cap, dw = xs.shape
    e, d, de2 = w13.shape
    de = w2.shape[1]
    nb = cap // bm

    def row_map(i, be, nu):
        return (jnp.minimum(i, nu[0] - 1), 0)

    def w_map(i, be, nu):
        return (be[jnp.minimum(i, nu[0] - 1)], 0, 0)

    return pl.pallas_call(
        _gmm_kernel,
        out_shape=jax.ShapeDtypeStruct((cap, dw), U32),
        grid_spec=pltpu.PrefetchScalarGridSpec(
            num_scalar_prefetch=2,
            grid=(nb,),
            in_specs=[pl.BlockSpec((bm, dw), row_map),
                      pl.BlockSpec((1, d, de2), w_map),
                      pl.BlockSpec((1, de, d), w_map)],
            out_specs=pl.BlockSpec((bm, dw), row_map),
            scratch_shapes=[pltpu.VMEM((d, de2), BF16), pltpu.VMEM((de, d), BF16)]),
        compiler_params=_cparams(("arbitrary",)),
        name="gmm",
    )(blk_e, nused, xs, w13, w2)


def _sc_gather_rows(table, idx, window):
    n_idx = idx.shape[0]
    dw = table.shape[1]
    mesh = plsc.VectorSubcoreMesh(core_axis_name="core", subcore_axis_name="subcore")

    n_workers = mesh.num_cores * mesh.num_subcores
    per_w = n_idx // n_workers
    assert per_w * n_workers == n_idx and per_w % window == 0

    @functools.partial(pl.kernel, mesh=mesh,
                       out_type=jax.ShapeDtypeStruct((n_idx, dw), table.dtype),
                       scratch_types=[pltpu.VMEM((per_w,), I32),
                                      pltpu.VMEM((window, dw), table.dtype),
                                      pltpu.SemaphoreType.DMA])
    def gather(x_hbm, i_hbm, o_hbm, idx_v, rows_v, sem):
        wid = lax.axis_index("subcore") * mesh.num_cores + lax.axis_index("core")
        base = wid * per_w
        pltpu.sync_copy(i_hbm.at[pl.ds(base, per_w)], idx_v)

        @pl.loop(0, per_w // window)
        def _(j):
            off = pl.multiple_of(j * window, window)
            pltpu.async_copy(x_hbm.at[idx_v.at[pl.ds(off, window)]], rows_v, sem).wait()
            pltpu.sync_copy(rows_v, o_hbm.at[pl.ds(base + off, window)])

    return gather(table, idx)


def _combine_kernel(gate_ref, x1_ref, h2_ref, mod_ref, gain_ref, sw13_ref, sw2_ref, yg_ref, o_ref):
    ds_ = sw2_ref.shape[0]
    hb = _unpack_rows(h2_ref[...]).astype(BF16)
    u = jnp.dot(hb, sw13_ref[...], preferred_element_type=F32)
    hmid = _silu(u[:, :ds_]) * u[:, ds_:]
    y = jnp.dot(hmid.astype(BF16), sw2_ref[...], preferred_element_type=F32)

    gate = gate_ref[...]
    moe = gate[:, 0:1] * _unpack_rows(yg_ref[0])
    for k in range(1, TOP_K):
        moe = moe + gate[:, k:k + 1] * _unpack_rows(yg_ref[k])
    xo = x1_ref[...] + mod_ref[0, 5:6, :] * (moe + y)
    ms = jnp.mean(xo * xo, axis=-1, keepdims=True)
    o_ref[...] = xo * lax.rsqrt(ms + EPS) * gain_ref[...]


def _combine(gate_tok, x1, h2, mods, gain, sw13, sw2, yg, tc, tiles_per_batch):
    n, d = x1.shape
    dw = h2.shape[1]
    nt = n // tc
    ds2 = sw13.shape[1]
    ds_ = sw2.shape[0]
    return pl.pallas_call(
        _combine_kernel,
        out_shape=jax.ShapeDtypeStruct((n, d), F32),
        grid=(nt,),
        in_specs=[pl.BlockSpec((tc, TOP_K), lambda i: (i, 0)),
                  pl.BlockSpec((tc, d), lambda i: (i, 0)),
                  pl.BlockSpec((tc, dw), lambda i: (i, 0)),
                  pl.BlockSpec((1, 8, d), lambda i: (i // tiles_per_batch, 0, 0)),
                  pl.BlockSpec((1, d), lambda i: (0, 0)),
                  pl.BlockSpec((d, ds2), lambda i: (0, 0)),
                  pl.BlockSpec((ds_, d), lambda i: (0, 0)),
                  pl.BlockSpec((TOP_K, tc, dw), lambda i: (0, i, 0))],
        out_specs=pl.BlockSpec((tc, d), lambda i: (i, 0)),
        compiler_params=_cparams(("arbitrary",)),
        name="combine",
    )(gate_tok, x1, h2, mods, gain, sw13, sw2, yg)


def _block_diag_pairs(w, per):
    nb, bd, _ = w.shape
    w = w.reshape(nb // per, per, bd, bd)
    eye = jnp.eye(per, dtype=w.dtype)
    out = jnp.einsum("gpij,pq->gpiqj", w, eye)
    return out.reshape(nb // per, per * bd, per * bd)


def _pos_tiles(pos, t):
    k, n = pos.shape
    return pos.reshape(k, n // t, t).transpose(1, 0, 2).reshape(n // t, 1, k * t)


def kernel(x, c, ctx, c_ctx, ada_w, ada_b, norm_mix, norm_ffn, norm_final, w_in, w_out,
           lru_conv_w, lru_conv_b, lru_wa, lru_ba, lru_wx, lru_bx, lru_lambda,
           hgrn_lb_logits, hgrn_norm, router_w, router_b, exp_w13, exp_w2, shared_w13, shared_w2):
    assert ada_w.shape[0] == 1, "single-layer block"
    b, s, d = x.shape
    n = b * s
    lru_w = lru_conv_w.shape[2]
    hg_w = hgrn_norm.shape[1]
    n_experts = router_w.shape[2]

    rows = -(-(b + 1) // SUBLANES) * SUBLANES
    cs = jnp.zeros((rows, d), F32).at[:b].set(c).at[b].set(c_ctx)
    mod = _ada(cs, ada_w[0], ada_b[0][None, :]).reshape(rows, 6, d)
    mods = jnp.pad(mod, ((0, 0), (0, 2), (0, 0)))

    w_in_bf = w_in[0].astype(BF16)
    gain1 = norm_mix[0][None, :]
    projx = _inproj(x, mods, gain1, w_in_bf, 512, shared_mod=False)
    projc = _inproj(ctx, mods[b:b + 1], gain1, w_in_bf, ctx.shape[1], shared_mod=True)

    per = LRU_G // (lru_w // LRU_BLOCKS)
    wg = jnp.stack([_block_diag_pairs(lru_wa[0, 0], per), _block_diag_pairs(lru_wx[0, 0], per),
                    _block_diag_pairs(lru_wa[0, 1], per), _block_diag_pairs(lru_wx[0, 1], per)]
                   ).astype(BF16)
    bg = jnp.stack([lru_ba[0, 0], lru_bx[0, 0], lru_ba[0, 1], lru_bx[0, 1]])
    ylru = _lru(projx, projc, lru_conv_w[0], lru_conv_b[0][None, :], wg, bg, lru_lambda[0])

    yhg = _hgrn(projx, projc, hgrn_lb_logits, hgrn_norm[0][None, :], (2 * lru_w) // LANES)

    wo = w_out[0].astype(BF16).reshape(2, lru_w, d)
    x1, h2, eid_t, gate_t, rank_t, cnt = _outproj(
        ylru, yhg, x, mods, norm_ffn[0][None, :], wo,
        router_w[0].T, router_b[0][:, None], 512)

    bm = 256
    counts = cnt[:, 0]
    padded = (counts + bm - 1) // bm * bm
    pends = jnp.cumsum(padded)
    pstarts = pends - padded
    pos = _pos(pstarts, eid_t, rank_t, 2048)
    nb = (n * TOP_K) // bm + n_experts
    cap = nb * bm
    blk_first = jnp.arange(nb, dtype=I32) * bm
    blk_e = jnp.minimum(jnp.sum(pends[None, :] <= blk_first[:, None], axis=1),
                        n_experts - 1).astype(I32)
    nused = (pends[-1] // bm).astype(I32)[None]

    td = 256
    xs = _dispatch(h2, _pos_tiles(pos, td), pstarts + counts, padded - counts, cap, bm, td)
    ys = _gmm(xs, exp_w13[0], exp_w2[0], blk_e, nused, bm)

    yg = _sc_gather_rows(ys, pos.reshape(-1), 64).reshape(TOP_K, n, d // 2)
    tc = 256
    out = _combine(gate_t.T, x1.reshape(n, d), h2, mods,
                   norm_final[None, :], shared_w13[0].astype(BF16), shared_w2[0].astype(BF16),
                   yg, tc, s // tc)
    return out.reshape(b, s, d)
```

```python
import functools

import jax
import jax.numpy as jnp
from jax import lax
from jax.experimental import pallas as pl
from jax.experimental.pallas import tpu as pltpu
from jax.experimental.pallas import tpu_sc as plsc

F32 = jnp.float32
BF16 = jnp.bfloat16
I32 = jnp.int32
U32 = jnp.uint32

EPS = 1e-6
LRU_C = 8.0
ROUTE_SCALE = 2.5
GRID_W = 64
HG_CHUNK = 32
N_GROUPS = 8
TOPK_GROUPS = 4
TOP_K = 8
LRU_BLOCKS = 8
HG_HEADS = 4
CONV_W = 4
CONV_PAD_L = 1

SUBLANES = 8
LANES = 128
N_SEG = SUBLANES
VMEM_LIMIT = 56 * 1024 * 1024


def _cparams(sem, vmem=VMEM_LIMIT):
    return pltpu.CompilerParams(dimension_semantics=sem, vmem_limit_bytes=vmem)


def _sigmoid(x):
    return jax.nn.sigmoid(x)


def _silu(x):
    return x * _sigmoid(x)


def _pack_rows(x):
    w = x.shape[1] // 2
    bits = pltpu.bitcast(x.astype(BF16).astype(F32), U32)
    return (bits[:, :w] >> 16) | (bits[:, w:] & jnp.uint32(0xFFFF0000))


def _unpack_rows(p):
    lo = pltpu.bitcast(p << 16, F32)
    hi = pltpu.bitcast(p & jnp.uint32(0xFFFF0000), F32)
    return jnp.concatenate([lo, hi], axis=1)


def _gelu_tanh(x):
    c = 0.7978845608028654
    return 0.5 * x * (1.0 + jnp.tanh(c * (x + 0.044715 * (x * x * x))))


def _ada_kernel(c_ref, w_ref, b_ref, o_ref):
    s = _silu(c_ref[...])
    o_ref[...] = jnp.dot(s, w_ref[...], preferred_element_type=F32,
                         precision=lax.Precision.HIGHEST) + b_ref[...]


def _ada(cs, w, b):
    rows, d = cs.shape
    n = w.shape[1]
    bn = 1024
    return pl.pallas_call(
        _ada_kernel,
        out_shape=jax.ShapeDtypeStruct((rows, n), F32),
        grid=(n // bn,),
        in_specs=[pl.BlockSpec((rows, d), lambda j: (0, 0)),
                  pl.BlockSpec((d, bn), lambda j: (0, j)),
                  pl.BlockSpec((1, bn), lambda j: (0, j))],
        out_specs=pl.BlockSpec((rows, bn), lambda j: (0, j)),
        compiler_params=_cparams(("arbitrary",)),
        name="ada",
    )(cs, w, b)


def _inproj_kernel(x_ref, mod_ref, gain_ref, w_ref, o_ref):
    x = x_ref[0]
    ms = jnp.mean(x * x, axis=-1, keepdims=True)
    y = x * lax.rsqrt(ms + EPS) * gain_ref[...]
    h = y * (1.0 + mod_ref[0, 1:2, :]) + mod_ref[0, 0:1, :]
    o_ref[0] = jnp.dot(h.astype(BF16), w_ref[...], preferred_element_type=F32)


def _inproj(x, mods, gain, w_bf, tm, shared_mod):
    b, s, d = x.shape
    n = w_bf.shape[1]
    mod_map = (lambda bi, i: (0, 0, 0)) if shared_mod else (lambda bi, i: (bi, 0, 0))
    return pl.pallas_call(
        _inproj_kernel,
        out_shape=jax.ShapeDtypeStruct((b, s, n), F32),
        grid=(b, s // tm),
        in_specs=[pl.BlockSpec((1, tm, d), lambda bi, i: (bi, i, 0)),
                  pl.BlockSpec((1, 8, d), mod_map),
                  pl.BlockSpec((1, d), lambda bi, i: (0, 0)),
                  pl.BlockSpec((d, n), lambda bi, i: (0, 0))],
        out_specs=pl.BlockSpec((1, tm, n), lambda bi, i: (bi, i, 0)),
        compiler_params=_cparams(("arbitrary", "arbitrary")),
        name="inproj",
    )(x, mods, gain, w_bf)


LRU_G = 256
LRU_CHUNK = 256


def _seg_pitch(seg_len):
    return seg_len + SUBLANES


def _lane_store(ref, d, rows, val):
    nl = val.shape[-1] // LANES
    for l in range(nl):
        ref[d * nl + l, rows, :] = val[:, l * LANES:(l + 1) * LANES]


def _lane_load(ref, d, rows, nl):
    return jnp.concatenate([ref[d * nl + l, rows, :] for l in range(nl)], axis=-1)


def _lru_coeffs(pad_ref, t0, rows, cw, cb, wg_ref, bg, sp, a_ref, b_ref, dst0):
    win = pad_ref[pl.ds(t0, rows + 2 * SUBLANES), :]
    u = cb
    for k in range(CONV_W):
        off = SUBLANES - CONV_PAD_L + k
        u = u + win[off:off + rows, :] * cw[k:k + 1, :]
    ub = u.astype(BF16)
    for d in range(2):
        r = _sigmoid(jnp.dot(ub, wg_ref[2 * d, 0], preferred_element_type=F32)
                     + bg[2 * d:2 * d + 1, :])
        ig = _sigmoid(jnp.dot(ub, wg_ref[2 * d + 1, 0], preferred_element_type=F32)
                      + bg[2 * d + 1:2 * d + 2, :])
        log_a = (-LRU_C) * r * sp[d:d + 1, :]
        a = jnp.exp(log_a)
        th = jnp.tanh(log_a)
        one_minus_a2 = (-2.0 * th) / (1.0 - th)
        bb = jnp.sqrt(one_minus_a2) * (ig * u)
        _lane_store(a_ref, d, pl.ds(dst0, rows), a)
        _lane_store(b_ref, d, pl.ds(dst0, rows), bb)


def _seg_scan(a_ref, b_ref, seg_len, pitch, unroll):
    n_lead = a_ref.shape[0]
    nl = n_lead // 2
    zero = jnp.zeros((N_SEG, LANES), F32)
    one = jnp.ones((N_SEG, LANES), F32)
    init = tuple((zero, one) for _ in range(n_lead))

    def step(t, carry):
        out = []
        for i in range(n_lead):
            h, p = carry[i]
            tt = t if i < nl else seg_len - 1 - t
            rows = pl.ds(tt, N_SEG, stride=pitch)
            a = a_ref[i, rows, :]
            h = a * h + b_ref[i, rows, :]
            p = p * a
            b_ref[i, rows, :] = h
            a_ref[i, rows, :] = p
            out.append((h, p))
        return tuple(out)

    def body(i, carry):
        for j in range(unroll):
            carry = step(i * unroll + j, carry)
        return carry

    ends = lax.fori_loop(0, seg_len // unroll, body, init)
    h_end = [jnp.concatenate([ends[d * nl + l][0] for l in range(nl)], axis=-1) for d in range(2)]
    p_end = [jnp.concatenate([ends[d * nl + l][1] for l in range(nl)], axis=-1) for d in range(2)]
    return h_end, p_end


def _seg_carries(h_end, p_end, h0, reverse):
    order = range(N_SEG - 1, -1, -1) if reverse else range(N_SEG)
    cin = [None] * N_SEG
    c = h0
    for s in order:
        cin[s] = c
        c = p_end[s:s + 1, :] * c + h_end[s:s + 1, :]
    return cin, c


def _lru_kernel(rx_ref, rg_ref, rc_ref, cw_ref, cb_ref, wg_ref, bg_ref, lam_ref, o_ref,
                padl, padc, a_l, b_l, a_c, b_c):
    s_len = rx_ref.shape[1]
    c_len = rc_ref.shape[1]
    g = rx_ref.shape[2]
    nl = g // LANES
    seg_l = s_len // N_SEG
    seg_c = c_len // N_SEG
    pitch_l = _seg_pitch(seg_l)
    pitch_c = _seg_pitch(seg_c)

    zeros = jnp.zeros((SUBLANES, g), F32)
    padl[0:SUBLANES, :] = zeros
    padl[SUBLANES + s_len:2 * SUBLANES + s_len, :] = zeros
    padc[0:SUBLANES, :] = zeros
    padc[SUBLANES + c_len:2 * SUBLANES + c_len, :] = zeros
    for i in range(s_len // LRU_CHUNK):
        padl[SUBLANES + i * LRU_CHUNK:SUBLANES + (i + 1) * LRU_CHUNK, :] = (
            rx_ref[0, i * LRU_CHUNK:(i + 1) * LRU_CHUNK, :])
    padc[SUBLANES:SUBLANES + c_len, :] = rc_ref[0]

    cw = cw_ref[...]
    cb = cb_ref[...]
    bg = bg_ref[...]
    x = -lam_ref[...]
    sp = jnp.maximum(x, 0.0) + jnp.log(1.0 + jnp.exp(-jnp.abs(x)))

    _lru_coeffs(padc, 0, c_len, cw, cb, wg_ref, bg, sp, a_l, b_l, 0)
    for i in range(2 * nl):
        for s in range(N_SEG):
            a_c[i, s * pitch_c:s * pitch_c + seg_c, :] = a_l[i, s * seg_c:(s + 1) * seg_c, :]
            b_c[i, s * pitch_c:s * pitch_c + seg_c, :] = b_l[i, s * seg_c:(s + 1) * seg_c, :]
    h_end, p_end = _seg_scan(a_c, b_c, seg_c, pitch_c, unroll=4)
    zero_row = jnp.zeros((1, g), F32)
    _, h0_f = _seg_carries(h_end[0], p_end[0], zero_row, reverse=False)
    _, h0_b = _seg_carries(h_end[1], p_end[1], zero_row, reverse=True)

    def coeff_body(s, carry):
        t0 = pl.multiple_of(s * seg_l, SUBLANES)
        dst = pl.multiple_of(s * pitch_l, SUBLANES)
        _lru_coeffs(padl, t0, seg_l, cw, cb, wg_ref, bg, sp, a_l, b_l, dst)
        return carry

    lax.fori_loop(0, N_SEG, coeff_body, 0)
    h_end, p_end = _seg_scan(a_l, b_l, seg_l, pitch_l, unroll=4)
    cin_f, _ = _seg_carries(h_end[0], p_end[0], h0_f, reverse=False)
    cin_b, _ = _seg_carries(h_end[1], p_end[1], h0_b, reverse=True)

    for s in range(N_SEG):
        rows = slice(s * pitch_l, s * pitch_l + seg_l)
        h = (_lane_load(b_l, 0, rows, nl) + _lane_load(a_l, 0, rows, nl) * cin_f[s]) + (
            _lane_load(b_l, 1, rows, nl) + _lane_load(a_l, 1, rows, nl) * cin_b[s])
        o_ref[0, s * seg_l:(s + 1) * seg_l, :] = _gelu_tanh(rg_ref[0, s * seg_l:(s + 1) * seg_l, :]) * h


def _lru(projx, projc, cw, cb, wg, bg, lam):
    b, s_len, _ = projx.shape
    c_len = projc.shape[1]
    w = cw.shape[1]
    g = LRU_G
    ng = w // g
    seg_l = s_len // N_SEG
    seg_c = c_len // N_SEG
    assert seg_l == LRU_CHUNK and c_len % (N_SEG * 1) == 0
    return pl.pallas_call(
        _lru_kernel,
        out_shape=jax.ShapeDtypeStruct((b, s_len, w), F32),
        grid=(b, ng),
        in_specs=[pl.BlockSpec((1, s_len, g), lambda bi, j: (bi, 0, j)),
                  pl.BlockSpec((1, s_len, g), lambda bi, j: (bi, 0, ng + j)),
                  pl.BlockSpec((1, c_len, g), lambda bi, j: (bi, 0, j)),
                  pl.BlockSpec((CONV_W, g), lambda bi, j: (0, j)),
                  pl.BlockSpec((1, g), lambda bi, j: (0, j)),
                  pl.BlockSpec((4, 1, g, g), lambda bi, j: (0, j, 0, 0)),
                  pl.BlockSpec((4, g), lambda bi, j: (0, j)),
                  pl.BlockSpec((2, g), lambda bi, j: (0, j))],
        out_specs=pl.BlockSpec((1, s_len, g), lambda bi, j: (bi, 0, j)),
        scratch_shapes=[pltpu.VMEM((s_len + 2 * SUBLANES, g), F32),
                        pltpu.VMEM((c_len + 2 * SUBLANES, g), F32),
                        pltpu.VMEM((2 * g // LANES, N_SEG * _seg_pitch(seg_l), LANES), F32),
                        pltpu.VMEM((2 * g // LANES, N_SEG * _seg_pitch(seg_l), LANES), F32),
                        pltpu.VMEM((2 * g // LANES, N_SEG * _seg_pitch(seg_c), LANES), F32),
                        pltpu.VMEM((2 * g // LANES, N_SEG * _seg_pitch(seg_c), LANES), F32)],
        compiler_params=_cparams(("arbitrary", "arbitrary")),
        name="lru",
    )(projx, projx, projc, cw, cb, wg, bg, lam)


HG_PITCH_L = GRID_W + SUBLANES
HG_PITCH_C = SUBLANES


def _hg_gates(z, lb):
    e = jnp.exp(-jnp.abs(z))
    s = 1.0 / (1.0 + e)
    es = e * s
    pos = z >= 0.0
    sig = jnp.where(pos, s, es)
    nsig = jnp.where(pos, es, s)
    logf = jnp.log(lb + (1.0 - lb) * sig)
    k = (1.0 - lb) * nsig
    return logf, k


def _hg_prepass(load, width, pitch, lb, qd, kd, kl, vs, dec):
    n_pos = HG_CHUNK
    for d, zname in enumerate(("zf", "zb")):
        order = range(n_pos) if d == 0 else range(n_pos - 1, -1, -1)
        g = jnp.zeros((width, LANES), F32)
        lbd = lb[d:d + 1, :]
        for p in order:
            logf, k = _hg_gates(load(zname, p), lbd)
            g = g + logf
            qd[d, p * pitch:p * pitch + width, :] = g
            kl[d, p * pitch:p * pitch + width, :] = k
        g_last = g
        dec[d, 0:width, :] = jnp.exp(g_last)
        for p in range(n_pos):
            rows = slice(p * pitch, p * pitch + width)
            gp = qd[d, rows, :]
            k = kl[d, rows, :]
            qd[d, rows, :] = load("q", p) * jnp.exp(gp)
            kd[d, rows, :] = k * jnp.exp(-gp)
            kl[d, rows, :] = k * jnp.exp(g_last - gp)
    for p in range(n_pos):
        vs[p * pitch:p * pitch + width, :] = load("v", p)


def _hg_chunk(d, w, pitch, qd, kd, kl, vs, dec, st, masks, o_ref):
    rows = pl.ds(w, HG_CHUNK, stride=pitch)
    q = qd[d, rows, :].astype(BF16)
    k = kd[d, rows, :].astype(BF16)
    kls = kl[d, rows, :].astype(BF16)
    v = vs[rows, :].astype(BF16)
    dc = dec[d, pl.ds(w, 1), :]
    sc = lax.dot_general(q, k, (((1,), (1,)), ((), ())), preferred_element_type=F32)
    sc = jnp.where(masks[d], sc, 0.0)
    s_t = st[d]
    if o_ref is not None:
        o = jnp.dot(sc.astype(BF16), v, preferred_element_type=F32)
        o = o + lax.dot_general(q, s_t.astype(BF16), (((1,), (1,)), ((), ())),
                                preferred_element_type=F32)
        o_ref[d, rows, :] = o
    kv_t = lax.dot_general(v, kls, (((0,), (0,)), ((), ())), preferred_element_type=F32)
    st[d] = s_t * dc + kv_t


def _hgrn_kernel(q_ref, v_ref, zf_ref, zb_ref, g_ref, qc_ref, vc_ref, zfc_ref, zbc_ref,
                 lbl_ref, gain_ref, o_ref, qd, kd, kl, vs, dec, obuf, st):
    s_len = q_ref.shape[1]
    c_len = qc_ref.shape[1]
    n_col = s_len // HG_CHUNK
    n_cc = c_len // HG_CHUNK
    assert n_col == GRID_W and n_cc == HG_PITCH_C

    lg = lbl_ref[...]
    m = jnp.max(lg, axis=1, keepdims=True)
    ex = jnp.exp(lg - m)
    lb = ex[:, 0, :] / jnp.sum(ex, axis=1)

    ii = lax.broadcasted_iota(I32, (HG_CHUNK, HG_CHUNK), 0)
    jj = lax.broadcasted_iota(I32, (HG_CHUNK, HG_CHUNK), 1)
    masks = (jj <= ii, jj >= ii)

    st[...] = jnp.zeros(st.shape, F32)

    c_refs = {"q": qc_ref, "v": vc_ref, "zf": zfc_ref, "zb": zbc_ref}

    def load_c(name, p):
        return c_refs[name][0, pl.ds(p, n_cc, stride=HG_CHUNK), :]

    _hg_prepass(load_c, n_cc, HG_PITCH_C, lb, qd, kd, kl, vs, dec)

    def ctx_body(n, carry):
        _hg_chunk(0, n, HG_PITCH_C, qd, kd, kl, vs, dec, st, masks, None)
        _hg_chunk(1, n_cc - 1 - n, HG_PITCH_C, qd, kd, kl, vs, dec, st, masks, None)
        return carry

    lax.fori_loop(0, n_cc, ctx_body, 0)

    l_refs = {"q": q_ref, "v": v_ref, "zf": zf_ref, "zb": zb_ref}

    def load_l(name, p):
        return l_refs[name][0, p * n_col:(p + 1) * n_col, :]

    _hg_prepass(load_l, n_col, HG_PITCH_L, lb, qd, kd, kl, vs, dec)

    def lat_body(i, carry):
        _hg_chunk(0, i, HG_PITCH_L, qd, kd, kl, vs, dec, st, masks, obuf)
        _hg_chunk(1, n_col - 1 - i, HG_PITCH_L, qd, kd, kl, vs, dec, st, masks, obuf)
        return carry

    lax.fori_loop(0, n_col, lat_body, 0)

    gain = gain_ref[...]
    for r in range(HG_CHUNK):
        rows = slice(r * HG_PITCH_L, r * HG_PITCH_L + n_col)
        o = obuf[0, rows, :] + obuf[1, rows, :]
        y = o * lax.rsqrt(jnp.mean(o * o, axis=-1, keepdims=True) + EPS) * gain
        o_ref[0, r * n_col:(r + 1) * n_col, :] = y * _silu(g_ref[0, r * n_col:(r + 1) * n_col, :])


def _hgrn(projx, projc, lbl, gain, col0):
    b, s_len, _ = projx.shape
    c_len = projc.shape[1]
    hw = gain.shape[1]
    nh = hw // LANES
    n_slots = lbl.shape[1]

    def xs(k):
        return pl.BlockSpec((1, s_len, LANES), lambda bi, h, k=k: (bi, 0, col0 + k * nh + h))

    def cs(k):
        return pl.BlockSpec((1, c_len, LANES), lambda bi, h, k=k: (bi, 0, col0 + k * nh + h))

    sc_rows = HG_CHUNK * HG_PITCH_L
    return pl.pallas_call(
        _hgrn_kernel,
        out_shape=jax.ShapeDtypeStruct((b, s_len, hw), F32),
        grid=(b, nh),
        in_specs=[xs(0), xs(1), xs(2), xs(3), xs(4), cs(0), cs(1), cs(2), cs(3),
                  pl.BlockSpec((2, n_slots, LANES), lambda bi, h: (0, 0, h)),
                  pl.BlockSpec((1, LANES), lambda bi, h: (0, h))],
        out_specs=pl.BlockSpec((1, s_len, LANES), lambda bi, h: (bi, 0, h)),
        scratch_shapes=[pltpu.VMEM((2, sc_rows, LANES), F32),
                        pltpu.VMEM((2, sc_rows, LANES), F32),
                        pltpu.VMEM((2, sc_rows, LANES), F32),
                        pltpu.VMEM((sc_rows, LANES), F32),
                        pltpu.VMEM((2, HG_PITCH_L, LANES), F32),
                        pltpu.VMEM((2, sc_rows, LANES), F32),
                        pltpu.VMEM((2, LANES, LANES), F32)],
        compiler_params=_cparams(("arbitrary", "arbitrary")),
        name="hgrn",
    )(projx, projx, projx, projx, projx, projc, projc, projc, projc, lbl, gain)


def _first_index_of_max(vals, iota, big):
    m = jnp.max(vals, axis=0, keepdims=True)
    idx = jnp.min(jnp.where(vals == m, iota, big), axis=0, keepdims=True)
    return m, idx


def _route(logits_t, rb, n_experts):
    t = logits_t.shape[1]
    gsz = n_experts // N_GROUPS
    neg = -jnp.inf
    scores = _sigmoid(logits_t)
    biased = scores + rb
    iota_g = lax.broadcasted_iota(I32, (gsz, t), 0)
    gscore = []
    for gi in range(N_GROUPS):
        blk = biased[gi * gsz:(gi + 1) * gsz, :]
        m1, i1 = _first_index_of_max(blk, iota_g, gsz)
        m2 = jnp.max(jnp.where(iota_g == i1, neg, blk), axis=0, keepdims=True)
        gscore.append(m1 + m2)
    gs = jnp.concatenate(gscore, axis=0)
    iota_n = lax.broadcasted_iota(I32, (N_GROUPS, t), 0)
    gsel = jnp.zeros((N_GROUPS, t), jnp.bool_)
    for _ in range(TOPK_GROUPS):
        _, gi1 = _first_index_of_max(gs, iota_n, N_GROUPS)
        hit = iota_n == gi1
        gsel = jnp.logical_or(gsel, hit)
        gs = jnp.where(hit, neg, gs)
    emask = jnp.concatenate(
        [jnp.broadcast_to(gsel[gi:gi + 1, :], (gsz, t)) for gi in range(N_GROUPS)], axis=0)
    masked = jnp.where(emask, biased, neg)
    iota_e = lax.broadcasted_iota(I32, (n_experts, t), 0)
    ids, gates = [], []
    sel = jnp.zeros((n_experts, t), F32)
    for _ in range(TOP_K):
        _, ei = _first_index_of_max(masked, iota_e, n_experts)
        hit = iota_e == ei
        ids.append(ei)
        gates.append(jnp.sum(jnp.where(hit, scores, 0.0), axis=0, keepdims=True))
        sel = jnp.where(hit, 1.0, sel)
        masked = jnp.where(hit, neg, masked)
    ids = jnp.concatenate(ids, axis=0)
    gates = jnp.concatenate(gates, axis=0)
    gates = gates / jnp.sum(gates, axis=0, keepdims=True) * ROUTE_SCALE
    return ids, gates, sel


def _outproj_kernel(yl_ref, yh_ref, x_ref, mod_ref, gain_ref, wo_ref, rwt_ref, rb_ref,
                    x1_ref, h2_ref, eid_ref, gate_ref, rank_ref, cnt_ref, carry):
    first = jnp.logical_and(pl.program_id(0) == 0, pl.program_id(1) == 0)

    @pl.when(first)
    def _():
        carry[...] = jnp.zeros(carry.shape, F32)

    y = jnp.dot(yl_ref[0].astype(BF16), wo_ref[0], preferred_element_type=F32)
    y = y + jnp.dot(yh_ref[0].astype(BF16), wo_ref[1], preferred_element_type=F32)
    x1 = x_ref[0] + mod_ref[0, 2:3, :] * y
    x1_ref[0] = x1
    ms = jnp.mean(x1 * x1, axis=-1, keepdims=True)
    h2 = x1 * lax.rsqrt(ms + EPS) * gain_ref[...]
    h2 = h2 * (1.0 + mod_ref[0, 4:5, :]) + mod_ref[0, 3:4, :]
    h2_ref[...] = _pack_rows(h2)

    n_experts = rwt_ref.shape[0]
    t = h2.shape[0]
    logits_t = lax.dot_general(rwt_ref[...], h2, (((1,), (1,)), ((), ())),
                               preferred_element_type=F32,
                               precision=lax.Precision.HIGHEST)
    ids, gates, sel = _route(logits_t, rb_ref[...], n_experts)
    eid_ref[...] = ids
    gate_ref[...] = gates

    ti = lax.broadcasted_iota(I32, (t, t), 0)
    tj = lax.broadcasted_iota(I32, (t, t), 1)
    upper = (ti < tj).astype(BF16)
    selb = sel.astype(BF16)
    before = jnp.dot(selb, upper, preferred_element_type=F32) + carry[:, 0:1]
    iota_e = lax.broadcasted_iota(I32, (n_experts, t), 0)
    ranks = [jnp.sum(jnp.where(iota_e == ids[k:k + 1, :], before, 0.0), axis=0, keepdims=True)
             for k in range(TOP_K)]
    rank_ref[...] = jnp.concatenate(ranks, axis=0).astype(I32)
    total = carry[...] + jnp.dot(selb, jnp.ones((t, LANES), BF16), preferred_element_type=F32)
    carry[...] = total
    cnt_ref[...] = total.astype(I32)


def _outproj(ylru, yhg, x, mods, gain, wo, rwt, rb, tm):
    b, s, d = x.shape
    hw = ylru.shape[2]
    e = rwt.shape[0]
    n = b * s
    nt = s // tm
    tok = lambda bi, i: (0, bi * nt + i)
    return pl.pallas_call(
        _outproj_kernel,
        out_shape=(jax.ShapeDtypeStruct((b, s, d), F32),
                   jax.ShapeDtypeStruct((n, d // 2), U32),
                   jax.ShapeDtypeStruct((TOP_K, n), I32),
                   jax.ShapeDtypeStruct((TOP_K, n), F32),
                   jax.ShapeDtypeStruct((TOP_K, n), I32),
                   jax.ShapeDtypeStruct((e, LANES), I32)),
        grid=(b, nt),
        in_specs=[pl.BlockSpec((1, tm, hw), lambda bi, i: (bi, i, 0)),
                  pl.BlockSpec((1, tm, hw), lambda bi, i: (bi, i, 0)),
                  pl.BlockSpec((1, tm, d), lambda bi, i: (bi, i, 0)),
                  pl.BlockSpec((1, 8, d), lambda bi, i: (bi, 0, 0)),
                  pl.BlockSpec((1, d), lambda bi, i: (0, 0)),
                  pl.BlockSpec((2, hw, d), lambda bi, i: (0, 0, 0)),
                  pl.BlockSpec((e, d), lambda bi, i: (0, 0)),
                  pl.BlockSpec((e, 1), lambda bi, i: (0, 0))],
        out_specs=(pl.BlockSpec((1, tm, d), lambda bi, i: (bi, i, 0)),
                   pl.BlockSpec((tm, d // 2), lambda bi, i: (bi * nt + i, 0)),
                   pl.BlockSpec((TOP_K, tm), tok),
                   pl.BlockSpec((TOP_K, tm), tok),
                   pl.BlockSpec((TOP_K, tm), tok),
                   pl.BlockSpec((e, LANES), lambda bi, i: (0, 0))),
        scratch_shapes=[pltpu.VMEM((e, LANES), F32)],
        compiler_params=_cparams(("arbitrary", "arbitrary")),
        name="outproj",
    )(ylru, yhg, x, mods, gain, wo, rwt, rb)


def _pos_kernel(pstart, eid_ref, rank_ref, pos_ref):
    eid = eid_ref[...]

    def body(e, acc):
        return jnp.where(eid == e, pstart[e], acc)

    base = lax.fori_loop(0, pstart.shape[0], body, jnp.zeros(eid.shape, I32))
    pos_ref[...] = base + rank_ref[...]


def _pos(pstarts, eid_t, rank_t, tn):
    k, n = eid_t.shape
    spec = pl.BlockSpec((k, tn), lambda i, ps: (0, i))
    return pl.pallas_call(
        _pos_kernel,
        out_shape=jax.ShapeDtypeStruct((k, n), I32),
        grid_spec=pltpu.PrefetchScalarGridSpec(
            num_scalar_prefetch=1, grid=(n // tn,), in_specs=[spec, spec], out_specs=spec),
        compiler_params=_cparams(("arbitrary",)),
        name="pos",
    )(pstarts, eid_t, rank_t)


def _pad_fill(padstart, padlen, zbuf, xs_ref, sem, wait):
    nbits = zbuf.shape[0].bit_length() - 1
    low_bits = SUBLANES.bit_length() - 1

    def go(cp):
        if wait:
            cp.wait()
        else:
            cp.start()

    def body(e, carry):
        st = padstart[e]
        ln = padlen[e]
        end = st + ln
        for bit in range(nbits - 1, low_bits - 1, -1):
            size = 1 << bit
            back = ((ln >> (bit + 1)) << (bit + 1)) + size

            @pl.when((ln & size) != 0)
            def _():
                start = pl.multiple_of(end - back, SUBLANES)
                go(pltpu.make_async_copy(zbuf.at[pl.ds(0, size)],
                                         xs_ref.at[pl.ds(start, size)], sem))
        for j in range(SUBLANES - 1):
            @pl.when(j < (ln & (SUBLANES - 1)))
            def _():
                go(pltpu.make_async_copy(zbuf.at[pl.ds(0, 1)], xs_ref.at[pl.ds(st + j, 1)], sem))
        return carry

    lax.fori_loop(0, padstart.shape[0], body, 0)


def _padfill_kernel(padstart, padlen, xs_in, xs_ref, zbuf, sem):
    del xs_in
    zbuf[...] = jnp.zeros(zbuf.shape, zbuf.dtype)
    _pad_fill(padstart, padlen, zbuf, xs_ref, sem, wait=False)
    _pad_fill(padstart, padlen, zbuf, xs_ref, sem, wait=True)


def _padfill(xs, padstart, padlen, bm):
    return pl.pallas_call(
        _padfill_kernel,
        out_shape=jax.ShapeDtypeStruct(xs.shape, xs.dtype),
        grid_spec=pltpu.PrefetchScalarGridSpec(
            num_scalar_prefetch=2,
            grid=(1,),
            in_specs=[pl.BlockSpec(memory_space=pl.ANY)],
            out_specs=pl.BlockSpec(memory_space=pl.ANY),
            scratch_shapes=[pltpu.VMEM((bm, xs.shape[1]), xs.dtype), pltpu.SemaphoreType.DMA(())]),
        input_output_aliases={2: 0},
        compiler_params=_cparams(("arbitrary",)),
        name="padfill",
    )(padstart, padlen, xs)


SC_SCATTER_WINDOW = 128


def _sc_mesh():
    return plsc.VectorSubcoreMesh(core_axis_name="core", subcore_axis_name="subcore")


def _sc_scatter_rows(rows, pos, cap):
    n, dw = rows.shape
    top_k = pos.shape[0]
    mesh = _sc_mesh()
    n_workers = mesh.num_cores * mesh.num_subcores
    win = SC_SCATTER_WINDOW
    per_w = n // n_workers
    n_chunks = per_w // win
    assert per_w * n_workers == n and n_chunks * win == per_w
    pos_w = pos.reshape(top_k, n_workers, n_chunks, win).transpose(1, 2, 0, 3)
    pos_w = pos_w.reshape(n_workers, n_chunks * top_k, win)

    @functools.partial(pl.kernel, mesh=mesh,
                       out_type=jax.ShapeDtypeStruct((cap, dw), rows.dtype),
                       scratch_types=[pltpu.VMEM((n_chunks * top_k, win), I32),
                                      pltpu.VMEM((win, dw), rows.dtype),
                                      pltpu.SemaphoreType.DMA])
    def scatter(r_hbm, p_hbm, o_hbm, idx_v, rows_v, sem):
        wid = lax.axis_index("subcore") * mesh.num_cores + lax.axis_index("core")
        pltpu.sync_copy(p_hbm.at[wid], idx_v)

        @pl.loop(0, n_chunks)
        def _(c):
            pltpu.sync_copy(r_hbm.at[pl.ds(wid * per_w + c * win, win)], rows_v)
            copies = [pltpu.async_copy(rows_v, o_hbm.at[idx_v.at[c * top_k + k]], sem)
                      for k in range(top_k)]
            for cp in copies:
                cp.wait()

    return scatter(rows, pos_w)


def _gmm_kernel(blk_e, nused, xs_ref, w13_ref, w2_ref, ys_ref, w13b, w2b):
    i = pl.program_id(0)
    de = w2_ref.shape[1]

    @pl.when(i < nused[0])
    def _():
        prev = blk_e[jnp.maximum(i - 1, 0)]
        new_expert = jnp.logical_or(i == 0, blk_e[i] != prev)

        @pl.when(new_expert)
        def _():
            w13b[...] = w13_ref[0].astype(BF16)
            w2b[...] = w2_ref[0].astype(BF16)

        xb = _unpack_rows(xs_ref[...]).astype(BF16)
        u = jnp.dot(xb, w13b[...], preferred_element_type=F32)
        hmid = _silu(u[:, :de]) * u[:, de:]
        ys_ref[...] = _pack_rows(jnp.dot(hmid.astype(BF16), w2b[...], preferred_element_type=F32))


def _gmm(xs, w13, w2, blk_e, nused, bm):
    cap, dw = xs.shape
    e, d, de2 = w13.shape
    de = w2.shape[1]
    nb = cap // bm

    def row_map(i, be, nu):
        return (jnp.minimum(i, nu[0] - 1), 0)

    def w_map(i, be, nu):
        return (be[jnp.minimum(i, nu[0] - 1)], 0, 0)

    return pl.pallas_call(
        _gmm_kernel,
        out_shape=jax.ShapeDtypeStruct((cap, dw), U32),
        grid_spec=pltpu.PrefetchScalarGridSpec(
            num_scalar_prefetch=2,
            grid=(nb,),
            in_specs=[pl.BlockSpec((bm, dw), row_map),
                      pl.BlockSpec((1, d, de2), w_map),
                      pl.BlockSpec((1, de, d), w_map)],
            out_specs=pl.BlockSpec((bm, dw), row_map),
            scratch_shapes=[pltpu.VMEM((d, de2), BF16), pltpu.VMEM((de, d), BF16)]),
        compiler_params=_cparams(("arbitrary",)),
        name="gmm",
    )(blk_e, nused, xs, w13, w2)


def _sc_gather_rows(table, idx, window):
    n_idx = idx.shape[0]
    dw = table.shape[1]
    mesh = _sc_mesh()
    n_workers = mesh.num_cores * mesh.num_subcores
    per_w = n_idx // n_workers
    n_chunks = per_w // window
    assert per_w * n_workers == n_idx and n_chunks * window == per_w and n_chunks % 2 == 0

    @functools.partial(pl.kernel, mesh=mesh,
                       out_type=jax.ShapeDtypeStruct((n_idx, dw), table.dtype),
                       scratch_types=[pltpu.VMEM((per_w,), I32),
                                      pltpu.VMEM((window, dw), table.dtype),
                                      pltpu.VMEM((window, dw), table.dtype),
                                      pltpu.SemaphoreType.DMA, pltpu.SemaphoreType.DMA,
                                      pltpu.SemaphoreType.DMA, pltpu.SemaphoreType.DMA])
    def gather(x_hbm, i_hbm, o_hbm, idx_v, rows0, rows1, gs0, gs1, os0, os1):
        wid = lax.axis_index("subcore") * mesh.num_cores + lax.axis_index("core")
        base = wid * per_w
        pltpu.sync_copy(i_hbm.at[pl.ds(base, per_w)], idx_v)
        bufs = ((rows0, gs0, os0), (rows1, gs1, os1))

        def fetch(j, slot):
            rows, gs, _ = bufs[slot]
            off = pl.multiple_of(j * window, window)
            return pltpu.async_copy(x_hbm.at[idx_v.at[pl.ds(off, window)]], rows, gs)

        def fetch_wait(slot):
            rows, gs, _ = bufs[slot]
            pltpu.make_async_copy(x_hbm.at[idx_v.at[pl.ds(0, window)]], rows, gs).wait()

        def put(j, slot):
            rows, _, osem = bufs[slot]
            off = pl.multiple_of(j * window, window)
            return pltpu.async_copy(rows, o_hbm.at[pl.ds(base + off, window)], osem)

        def put_wait(slot):
            rows, _, osem = bufs[slot]
            pltpu.make_async_copy(rows, o_hbm.at[pl.ds(base, window)], osem).wait()

        fetch(0, 0)

        @pl.loop(0, n_chunks, step=2)
        def _(j):
            @pl.when(j > 0)
            def _():
                put_wait(1)
            fetch(j + 1, 1)
            fetch_wait(0)
            put(j, 0)
            put_wait(0)

            @pl.when(j + 2 < n_chunks)
            def _():
                fetch(j + 2, 0)
            fetch_wait(1)
            put(j + 1, 1)

        put_wait(1)

    return gather(table, idx)


def _combine_kernel(gate_ref, x1_ref, h2_ref, mod_ref, gain_ref, sw13_ref, sw2_ref, yg_ref, o_ref):
    ds_ = sw2_ref.shape[0]
    hb = _unpack_rows(h2_ref[...]).astype(BF16)
    u = jnp.dot(hb, sw13_ref[...], preferred_element_type=F32)
    hmid = _silu(u[:, :ds_]) * u[:, ds_:]
    y = jnp.dot(hmid.astype(BF16), sw2_ref[...], preferred_element_type=F32)

    gate = gate_ref[...]
    moe = gate[:, 0:1] * _unpack_rows(yg_ref[0])
    for k in range(1, TOP_K):
        moe = moe + gate[:, k:k + 1] * _unpack_rows(yg_ref[k])
    xo = x1_ref[...] + mod_ref[0, 5:6, :] * (moe + y)
    ms = jnp.mean(xo * xo, axis=-1, keepdims=True)
    o_ref[...] = xo * lax.rsqrt(ms + EPS) * gain_ref[...]


def _combine(gate_tok, x1, h2, mods, gain, sw13, sw2, yg, tc, tiles_per_batch):
    n, d = x1.shape
    dw = h2.shape[1]
    nt = n // tc
    ds2 = sw13.shape[1]
    ds_ = sw2.shape[0]
    return pl.pallas_call(
        _combine_kernel,
        out_shape=jax.ShapeDtypeStruct((n, d), F32),
        grid=(nt,),
        in_specs=[pl.BlockSpec((tc, TOP_K), lambda i: (i, 0)),
                  pl.BlockSpec((tc, d), lambda i: (i, 0)),
                  pl.BlockSpec((tc, dw), lambda i: (i, 0)),
                  pl.BlockSpec((1, 8, d), lambda i: (i // tiles_per_batch, 0, 0)),
                  pl.BlockSpec((1, d), lambda i: (0, 0)),
                  pl.BlockSpec((d, ds2), lambda i: (0, 0)),
                  pl.BlockSpec((ds_, d), lambda i: (0, 0)),
                  pl.BlockSpec((TOP_K, tc, dw), lambda i: (0, i, 0))],
        out_specs=pl.BlockSpec((tc, d), lambda i: (i, 0)),
        compiler_params=_cparams(("arbitrary",)),
        name="combine",
    )(gate_tok, x1, h2, mods, gain, sw13, sw2, yg)


def _block_diag_pairs(w, per):
    nb, bd, _ = w.shape
    w = w.reshape(nb // per, per, bd, bd)
    eye = jnp.eye(per, dtype=w.dtype)
    out = jnp.einsum("gpij,pq->gpiqj", w, eye)
    return out.reshape(nb // per, per * bd, per * bd)


def kernel(x, c, ctx, c_ctx, ada_w, ada_b, norm_mix, norm_ffn, norm_final, w_in, w_out,
           lru_conv_w, lru_conv_b, lru_wa, lru_ba, lru_wx, lru_bx, lru_lambda,
           hgrn_lb_logits, hgrn_norm, router_w, router_b, exp_w13, exp_w2, shared_w13, shared_w2):
    assert ada_w.shape[0] == 1, "single-layer block"
    b, s, d = x.shape
    n = b * s
    lru_w = lru_conv_w.shape[2]
    hg_w = hgrn_norm.shape[1]
    n_experts = router_w.shape[2]

    rows = -(-(b + 1) // SUBLANES) * SUBLANES
    cs = jnp.zeros((rows, d), F32).at[:b].set(c).at[b].set(c_ctx)
    mod = _ada(cs, ada_w[0], ada_b[0][None, :]).reshape(rows, 6, d)
    mods = jnp.pad(mod, ((0, 0), (0, 2), (0, 0)))

    w_in_bf = w_in[0].astype(BF16)
    gain1 = norm_mix[0][None, :]
    projx = _inproj(x, mods, gain1, w_in_bf, 512, shared_mod=False)
    projc = _inproj(ctx, mods[b:b + 1], gain1, w_in_bf, ctx.shape[1], shared_mod=True)

    per = LRU_G // (lru_w // LRU_BLOCKS)
    wg = jnp.stack([_block_diag_pairs(lru_wa[0, 0], per), _block_diag_pairs(lru_wx[0, 0], per),
                    _block_diag_pairs(lru_wa[0, 1], per), _block_diag_pairs(lru_wx[0, 1], per)]
                   ).astype(BF16)
    bg = jnp.stack([lru_ba[0, 0], lru_bx[0, 0], lru_ba[0, 1], lru_bx[0, 1]])
    ylru = _lru(projx, projc, lru_conv_w[0], lru_conv_b[0][None, :], wg, bg, lru_lambda[0])

    yhg = _hgrn(projx, projc, hgrn_lb_logits, hgrn_norm[0][None, :], (2 * lru_w) // LANES)

    wo = w_out[0].astype(BF16).reshape(2, lru_w, d)
    x1, h2, eid_t, gate_t, rank_t, cnt = _outproj(
        ylru, yhg, x, mods, norm_ffn[0][None, :], wo,
        router_w[0].T, router_b[0][:, None], 512)

    bm = 256
    counts = cnt[:, 0]
    padded = (counts + bm - 1) // bm * bm
    pends = jnp.cumsum(padded)
    pstarts = pends - padded
    pos = _pos(pstarts, eid_t, rank_t, 2048)
    nb = (n * TOP_K) // bm + n_experts
    cap = nb * bm
    blk_first = jnp.arange(nb, dtype=I32) * bm
    blk_e = jnp.minimum(jnp.sum(pends[None, :] <= blk_first[:, None], axis=1),
                        n_experts - 1).astype(I32)
    nused = (pends[-1] // bm).astype(I32)[None]

    xs = _padfill(_sc_scatter_rows(h2, pos, cap), pstarts + counts, padded - counts, bm)
    ys = _gmm(xs, exp_w13[0], exp_w2[0], blk_e, nused, bm)

    yg = _sc_gather_rows(ys, pos.reshape(-1), 64).reshape(TOP_K, n, d // 2)
    tc = 256
    out = _combine(gate_t.T, x1.reshape(n, d), h2, mods,
                   norm_final[None, :], shared_w13[0].astype(BF16), shared_w2[0].astype(BF16),
                   yg, tc, s // tc)
    return out.reshape(b, s, d)
```

```python
import functools

import jax
import jax.numpy as jnp
from jax import lax
from jax.experimental import pallas as pl
from jax.experimental.pallas import tpu as pltpu
from jax.experimental.pallas import tpu_sc as plsc

F32 = jnp.float32
BF16 = jnp.bfloat16
I32 = jnp.int32
U32 = jnp.uint32

EPS = 1e-6
LRU_C = 8.0
ROUTE_SCALE = 2.5
GRID_W = 64
HG_CHUNK = 32
N_GROUPS = 8
TOPK_GROUPS = 4
TOP_K = 8
LRU_BLOCKS = 8
HG_HEADS = 4
CONV_W = 4
CONV_PAD_L = 1

SUBLANES = 8
LANES = 128
N_SEG = SUBLANES
VMEM_LIMIT = 56 * 1024 * 1024


def _cparams(sem, vmem=VMEM_LIMIT):
    return pltpu.CompilerParams(dimension_semantics=sem, vmem_limit_bytes=vmem)


def _sigmoid(x):
    return jax.nn.sigmoid(x)


def _silu(x):
    return x * _sigmoid(x)


def _pack_rows(x):
    w = x.shape[1] // 2
    bits = pltpu.bitcast(x.astype(BF16).astype(F32), U32)
    return (bits[:, :w] >> 16) | (bits[:, w:] & jnp.uint32(0xFFFF0000))


def _unpack_rows(p):
    lo = pltpu.bitcast(p << 16, F32)
    hi = pltpu.bitcast(p & jnp.uint32(0xFFFF0000), F32)
    return jnp.concatenate([lo, hi], axis=1)


def _gelu_tanh(x):
    c = 0.7978845608028654
    return 0.5 * x * (1.0 + jnp.tanh(c * (x + 0.044715 * (x * x * x))))


def _ada_kernel(c_ref, w_ref, b_ref, o_ref):
    s = _silu(c_ref[...])
    o_ref[...] = jnp.dot(s, w_ref[...], preferred_element_type=F32,
                         precision=lax.Precision.HIGHEST) + b_ref[...]


def _ada(cs, w, b):
    rows, d = cs.shape
    n = w.shape[1]
    bn = 1024
    return pl.pallas_call(
        _ada_kernel,
        out_shape=jax.ShapeDtypeStruct((rows, n), F32),
        grid=(n // bn,),
        in_specs=[pl.BlockSpec((rows, d), lambda j: (0, 0)),
                  pl.BlockSpec((d, bn), lambda j: (0, j)),
                  pl.BlockSpec((1, bn), lambda j: (0, j))],
        out_specs=pl.BlockSpec((rows, bn), lambda j: (0, j)),
        compiler_params=_cparams(("arbitrary",)),
        name="ada",
    )(cs, w, b)


def _inproj_kernel(x_ref, mod_ref, gain_ref, w_ref, o_ref):
    x = x_ref[0]
    ms = jnp.mean(x * x, axis=-1, keepdims=True)
    y = x * lax.rsqrt(ms + EPS) * gain_ref[...]
    h = y * (1.0 + mod_ref[0, 1:2, :]) + mod_ref[0, 0:1, :]
    o_ref[0] = jnp.dot(h.astype(BF16), w_ref[...], preferred_element_type=F32)


def _inproj(x, mods, gain, w_bf, tm, shared_mod):
    b, s, d = x.shape
    n = w_bf.shape[1]
    mod_map = (lambda bi, i: (0, 0, 0)) if shared_mod else (lambda bi, i: (bi, 0, 0))
    return pl.pallas_call(
        _inproj_kernel,
        out_shape=jax.ShapeDtypeStruct((b, s, n), F32),
        grid=(b, s // tm),
        in_specs=[pl.BlockSpec((1, tm, d), lambda bi, i: (bi, i, 0)),
                  pl.BlockSpec((1, 8, d), mod_map),
                  pl.BlockSpec((1, d), lambda bi, i: (0, 0)),
                  pl.BlockSpec((d, n), lambda bi, i: (0, 0))],
        out_specs=pl.BlockSpec((1, tm, n), lambda bi, i: (bi, i, 0)),
        compiler_params=_cparams(("arbitrary", "arbitrary")),
        name="inproj",
    )(x, mods, gain, w_bf)


LRU_G = 256
LRU_CHUNK = 256


def _seg_pitch(seg_len):
    return seg_len + SUBLANES


def _lane_store(ref, d, rows, val):
    nl = val.shape[-1] // LANES
    for l in range(nl):
        ref[d * nl + l, rows, :] = val[:, l * LANES:(l + 1) * LANES]


def _lane_load(ref, d, rows, nl):
    return jnp.concatenate([ref[d * nl + l, rows, :] for l in range(nl)], axis=-1)


def _lru_coeffs(pad_ref, t0, rows, cw, cb, wg_ref, bg, sp, a_ref, b_ref, dst0):
    win = pad_ref[pl.ds(t0, rows + 2 * SUBLANES), :]
    u = cb
    for k in range(CONV_W):
        off = SUBLANES - CONV_PAD_L + k
        u = u + win[off:off + rows, :] * cw[k:k + 1, :]
    ub = u.astype(BF16)
    for d in range(2):
        r = _sigmoid(jnp.dot(ub, wg_ref[2 * d, 0], preferred_element_type=F32)
                     + bg[2 * d:2 * d + 1, :])
        ig = _sigmoid(jnp.dot(ub, wg_ref[2 * d + 1, 0], preferred_element_type=F32)
                      + bg[2 * d + 1:2 * d + 2, :])
        log_a = (-LRU_C) * r * sp[d:d + 1, :]
        a = jnp.exp(log_a)
        th = jnp.tanh(log_a)
        one_minus_a2 = (-2.0 * th) / (1.0 - th)
        bb = jnp.sqrt(one_minus_a2) * (ig * u)
        _lane_store(a_ref, d, pl.ds(dst0, rows), a)
        _lane_store(b_ref, d, pl.ds(dst0, rows), bb)


def _seg_scan(a_ref, b_ref, seg_len, pitch, unroll):
    n_lead = a_ref.shape[0]
    nl = n_lead // 2
    zero = jnp.zeros((N_SEG, LANES), F32)
    one = jnp.ones((N_SEG, LANES), F32)
    init = tuple((zero, one) for _ in range(n_lead))

    def step(t, carry):
        out = []
        for i in range(n_lead):
            h, p = carry[i]
            tt = t if i < nl else seg_len - 1 - t
            rows = pl.ds(tt, N_SEG, stride=pitch)
            a = a_ref[i, rows, :]
            h = a * h + b_ref[i, rows, :]
            p = p * a
            b_ref[i, rows, :] = h
            a_ref[i, rows, :] = p
            out.append((h, p))
        return tuple(out)

    def body(i, carry):
        for j in range(unroll):
            carry = step(i * unroll + j, carry)
        return carry

    ends = lax.fori_loop(0, seg_len // unroll, body, init)
    h_end = [jnp.concatenate([ends[d * nl + l][0] for l in range(nl)], axis=-1) for d in range(2)]
    p_end = [jnp.concatenate([ends[d * nl + l][1] for l in range(nl)], axis=-1) for d in range(2)]
    return h_end, p_end


def _seg_carries(h_end, p_end, h0, reverse):
    order = range(N_SEG - 1, -1, -1) if reverse else range(N_SEG)
    cin = [None] * N_SEG
    c = h0
    for s in order:
        cin[s] = c
        c = p_end[s:s + 1, :] * c + h_end[s:s + 1, :]
    return cin, c


def _lru_kernel(rx_ref, rg_ref, rc_ref, cw_ref, cb_ref, wg_ref, bg_ref, lam_ref, o_ref,
                padl, padc, a_l, b_l, a_c, b_c):
    s_len = rx_ref.shape[1]
    c_len = rc_ref.shape[1]
    g = rx_ref.shape[2]
    nl = g // LANES
    seg_l = s_len // N_SEG
    seg_c = c_len // N_SEG
    pitch_l = _seg_pitch(seg_l)
    pitch_c = _seg_pitch(seg_c)

    zeros = jnp.zeros((SUBLANES, g), F32)
    padl[0:SUBLANES, :] = zeros
    padl[SUBLANES + s_len:2 * SUBLANES + s_len, :] = zeros
    padc[0:SUBLANES, :] = zeros
    padc[SUBLANES + c_len:2 * SUBLANES + c_len, :] = zeros
    for i in range(s_len // LRU_CHUNK):
        padl[SUBLANES + i * LRU_CHUNK:SUBLANES + (i + 1) * LRU_CHUNK, :] = (
            rx_ref[0, i * LRU_CHUNK:(i + 1) * LRU_CHUNK, :])
    padc[SUBLANES:SUBLANES + c_len, :] = rc_ref[0]

    cw = cw_ref[...]
    cb = cb_ref[...]
    bg = bg_ref[...]
    x = -lam_ref[...]
    sp = jnp.maximum(x, 0.0) + jnp.log(1.0 + jnp.exp(-jnp.abs(x)))

    _lru_coeffs(padc, 0, c_len, cw, cb, wg_ref, bg, sp, a_l, b_l, 0)
    for i in range(2 * nl):
        for s in range(N_SEG):
            a_c[i, s * pitch_c:s * pitch_c + seg_c, :] = a_l[i, s * seg_c:(s + 1) * seg_c, :]
            b_c[i, s * pitch_c:s * pitch_c + seg_c, :] = b_l[i, s * seg_c:(s + 1) * seg_c, :]
    h_end, p_end = _seg_scan(a_c, b_c, seg_c, pitch_c, unroll=4)
    zero_row = jnp.zeros((1, g), F32)
    _, h0_f = _seg_carries(h_end[0], p_end[0], zero_row, reverse=False)
    _, h0_b = _seg_carries(h_end[1], p_end[1], zero_row, reverse=True)

    def coeff_body(s, carry):
        t0 = pl.multiple_of(s * seg_l, SUBLANES)
        dst = pl.multiple_of(s * pitch_l, SUBLANES)
        _lru_coeffs(padl, t0, seg_l, cw, cb, wg_ref, bg, sp, a_l, b_l, dst)
        return carry

    lax.fori_loop(0, N_SEG, coeff_body, 0)
    h_end, p_end = _seg_scan(a_l, b_l, seg_l, pitch_l, unroll=4)
    cin_f, _ = _seg_carries(h_end[0], p_end[0], h0_f, reverse=False)
    cin_b, _ = _seg_carries(h_end[1], p_end[1], h0_b, reverse=True)

    for s in range(N_SEG):
        rows = slice(s * pitch_l, s * pitch_l + seg_l)
        h = (_lane_load(b_l, 0, rows, nl) + _lane_load(a_l, 0, rows, nl) * cin_f[s]) + (
            _lane_load(b_l, 1, rows, nl) + _lane_load(a_l, 1, rows, nl) * cin_b[s])
        o_ref[0, s * seg_l:(s + 1) * seg_l, :] = _gelu_tanh(rg_ref[0, s * seg_l:(s + 1) * seg_l, :]) * h


def _lru(projx, projc, cw, cb, wg, bg, lam):
    b, s_len, _ = projx.shape
    c_len = projc.shape[1]
    w = cw.shape[1]
    g = LRU_G
    ng = w // g
    seg_l = s_len // N_SEG
    seg_c = c_len // N_SEG
    assert seg_l == LRU_CHUNK and c_len % (N_SEG * 1) == 0
    return pl.pallas_call(
        _lru_kernel,
        out_shape=jax.ShapeDtypeStruct((b, s_len, w), F32),
        grid=(b, ng),
        in_specs=[pl.BlockSpec((1, s_len, g), lambda bi, j: (bi, 0, j)),
                  pl.BlockSpec((1, s_len, g), lambda bi, j: (bi, 0, ng + j)),
                  pl.BlockSpec((1, c_len, g), lambda bi, j: (bi, 0, j)),
                  pl.BlockSpec((CONV_W, g), lambda bi, j: (0, j)),
                  pl.BlockSpec((1, g), lambda bi, j: (0, j)),
                  pl.BlockSpec((4, 1, g, g), lambda bi, j: (0, j, 0, 0)),
                  pl.BlockSpec((4, g), lambda bi, j: (0, j)),
                  pl.BlockSpec((2, g), lambda bi, j: (0, j))],
        out_specs=pl.BlockSpec((1, s_len, g), lambda bi, j: (bi, 0, j)),
        scratch_shapes=[pltpu.VMEM((s_len + 2 * SUBLANES, g), F32),
                        pltpu.VMEM((c_len + 2 * SUBLANES, g), F32),
                        pltpu.VMEM((2 * g // LANES, N_SEG * _seg_pitch(seg_l), LANES), F32),
                        pltpu.VMEM((2 * g // LANES, N_SEG * _seg_pitch(seg_l), LANES), F32),
                        pltpu.VMEM((2 * g // LANES, N_SEG * _seg_pitch(seg_c), LANES), F32),
                        pltpu.VMEM((2 * g // LANES, N_SEG * _seg_pitch(seg_c), LANES), F32)],
        compiler_params=_cparams(("arbitrary", "arbitrary")),
        name="lru",
    )(projx, projx, projc, cw, cb, wg, bg, lam)


HG_PITCH_L = GRID_W + SUBLANES
HG_PITCH_C = SUBLANES


def _hg_gates(z, lb):
    e = jnp.exp(-jnp.abs(z))
    s = 1.0 / (1.0 + e)
    es = e * s
    pos = z >= 0.0
    sig = jnp.where(pos, s, es)
    nsig = jnp.where(pos, es, s)
    logf = jnp.log(lb + (1.0 - lb) * sig)
    k = (1.0 - lb) * nsig
    return logf, k


def _hg_prepass(load, width, pitch, lb, qd, kd, kl, vs, dec):
    n_pos = HG_CHUNK
    for d, zname in enumerate(("zf", "zb")):
        order = range(n_pos) if d == 0 else range(n_pos - 1, -1, -1)
        g = jnp.zeros((width, LANES), F32)
        lbd = lb[d:d + 1, :]
        for p in order:
            logf, k = _hg_gates(load(zname, p), lbd)
            g = g + logf
            qd[d, p * pitch:p * pitch + width, :] = g
            kl[d, p * pitch:p * pitch + width, :] = k
        g_last = g
        dec[d, 0:width, :] = jnp.exp(g_last)
        for p in range(n_pos):
            rows = slice(p * pitch, p * pitch + width)
            gp = qd[d, rows, :]
            k = kl[d, rows, :]
            qd[d, rows, :] = load("q", p) * jnp.exp(gp)
            kd[d, rows, :] = k * jnp.exp(-gp)
            kl[d, rows, :] = k * jnp.exp(g_last - gp)
    for p in range(n_pos):
        vs[p * pitch:p * pitch + width, :] = load("v", p)


def _hg_chunk(d, w, pitch, qd, kd, kl, vs, dec, st, masks, o_ref):
    rows = pl.ds(w, HG_CHUNK, stride=pitch)
    q = qd[d, rows, :].astype(BF16)
    k = kd[d, rows, :].astype(BF16)
    kls = kl[d, rows, :].astype(BF16)
    v = vs[rows, :].astype(BF16)
    dc = dec[d, pl.ds(w, 1), :]
    sc = lax.dot_general(q, k, (((1,), (1,)), ((), ())), preferred_element_type=F32)
    sc = jnp.where(masks[d], sc, 0.0)
    s_t = st[d]
    if o_ref is not None:
        o = jnp.dot(sc.astype(BF16), v, preferred_element_type=F32)
        o = o + lax.dot_general(q, s_t.astype(BF16), (((1,), (1,)), ((), ())),
                                preferred_element_type=F32)
        o_ref[d, rows, :] = o
    kv_t = lax.dot_general(v, kls, (((0,), (0,)), ((), ())), preferred_element_type=F32)
    st[d] = s_t * dc + kv_t


def _hgrn_kernel(q_ref, v_ref, zf_ref, zb_ref, g_ref, qc_ref, vc_ref, zfc_ref, zbc_ref,
                 lbl_ref, gain_ref, o_ref, qd, kd, kl, vs, dec, obuf, st):
    s_len = q_ref.shape[1]
    c_len = qc_ref.shape[1]
    n_col = s_len // HG_CHUNK
    n_cc = c_len // HG_CHUNK
    assert n_col == GRID_W and n_cc == HG_PITCH_C

    lg = lbl_ref[...]
    m = jnp.max(lg, axis=1, keepdims=True)
    ex = jnp.exp(lg - m)
    lb = ex[:, 0, :] / jnp.sum(ex, axis=1)

    ii = lax.broadcasted_iota(I32, (HG_CHUNK, HG_CHUNK), 0)
    jj = lax.broadcasted_iota(I32, (HG_CHUNK, HG_CHUNK), 1)
    masks = (jj <= ii, jj >= ii)

    st[...] = jnp.zeros(st.shape, F32)

    c_refs = {"q": qc_ref, "v": vc_ref, "zf": zfc_ref, "zb": zbc_ref}

    def load_c(name, p):
        return c_refs[name][0, pl.ds(p, n_cc, stride=HG_CHUNK), :]

    _hg_prepass(load_c, n_cc, HG_PITCH_C, lb, qd, kd, kl, vs, dec)

    def ctx_body(n, carry):
        _hg_chunk(0, n, HG_PITCH_C, qd, kd, kl, vs, dec, st, masks, None)
        _hg_chunk(1, n_cc - 1 - n, HG_PITCH_C, qd, kd, kl, vs, dec, st, masks, None)
        return carry

    lax.fori_loop(0, n_cc, ctx_body, 0)

    l_refs = {"q": q_ref, "v": v_ref, "zf": zf_ref, "zb": zb_ref}

    def load_l(name, p):
        return l_refs[name][0, p * n_col:(p + 1) * n_col, :]

    _hg_prepass(load_l, n_col, HG_PITCH_L, lb, qd, kd, kl, vs, dec)

    def lat_body(i, carry):
        _hg_chunk(0, i, HG_PITCH_L, qd, kd, kl, vs, dec, st, masks, obuf)
        _hg_chunk(1, n_col - 1 - i, HG_PITCH_L, qd, kd, kl, vs, dec, st, masks, obuf)
        return carry

    lax.fori_loop(0, n_col, lat_body, 0)

    gain = gain_ref[...]
    for r in range(HG_CHUNK):
        rows = slice(r * HG_PITCH_L, r * HG_PITCH_L + n_col)
        o = obuf[0, rows, :] + obuf[1, rows, :]
        y = o * lax.rsqrt(jnp.mean(o * o, axis=-1, keepdims=True) + EPS) * gain
        o_ref[0, r * n_col:(r + 1) * n_col, :] = y * _silu(g_ref[0, r * n_col:(r + 1) * n_col, :])


def _hgrn(projx, projc, lbl, gain, col0):
    b, s_len, _ = projx.shape
    c_len = projc.shape[1]
    hw = gain.shape[1]
    nh = hw // LANES
    n_slots = lbl.shape[1]

    def xs(k):
        return pl.BlockSpec((1, s_len, LANES), lambda bi, h, k=k: (bi, 0, col0 + k * nh + h))

    def cs(k):
        return pl.BlockSpec((1, c_len, LANES), lambda bi, h, k=k: (bi, 0, col0 + k * nh + h))

    sc_rows = HG_CHUNK * HG_PITCH_L
    return pl.pallas_call(
        _hgrn_kernel,
        out_shape=jax.ShapeDtypeStruct((b, s_len, hw), F32),
        grid=(b, nh),
        in_specs=[xs(0), xs(1), xs(2), xs(3), xs(4), cs(0), cs(1), cs(2), cs(3),
                  pl.BlockSpec((2, n_slots, LANES), lambda bi, h: (0, 0, h)),
                  pl.BlockSpec((1, LANES), lambda bi, h: (0, h))],
        out_specs=pl.BlockSpec((1, s_len, LANES), lambda bi, h: (bi, 0, h)),
        scratch_shapes=[pltpu.VMEM((2, sc_rows, LANES), F32),
                        pltpu.VMEM((2, sc_rows, LANES), F32),
                        pltpu.VMEM((2, sc_rows, LANES), F32),
                        pltpu.VMEM((sc_rows, LANES), F32),
                        pltpu.VMEM((2, HG_PITCH_L, LANES), F32),
                        pltpu.VMEM((2, sc_rows, LANES), F32),
                        pltpu.VMEM((2, LANES, LANES), F32)],
        compiler_params=_cparams(("arbitrary", "arbitrary")),
        name="hgrn",
    )(projx, projx, projx, projx, projx, projc, projc, projc, projc, lbl, gain)


def _first_index_of_max(vals, iota, big):
    m = jnp.max(vals, axis=0, keepdims=True)
    idx = jnp.min(jnp.where(vals == m, iota, big), axis=0, keepdims=True)
    return m, idx


def _route(logits_t, rb, n_experts):
    t = logits_t.shape[1]
    gsz = n_experts // N_GROUPS
    neg = -jnp.inf
    scores = _sigmoid(logits_t)
    biased = scores + rb
    iota_g = lax.broadcasted_iota(I32, (gsz, t), 0)
    gscore = []
    for gi in range(N_GROUPS):
        blk = biased[gi * gsz:(gi + 1) * gsz, :]
        m1, i1 = _first_index_of_max(blk, iota_g, gsz)
        m2 = jnp.max(jnp.where(iota_g == i1, neg, blk), axis=0, keepdims=True)
        gscore.append(m1 + m2)
    gs = jnp.concatenate(gscore, axis=0)
    iota_n = lax.broadcasted_iota(I32, (N_GROUPS, t), 0)
    gsel = jnp.zeros((N_GROUPS, t), jnp.bool_)
    for _ in range(TOPK_GROUPS):
        _, gi1 = _first_index_of_max(gs, iota_n, N_GROUPS)
        hit = iota_n == gi1
        gsel = jnp.logical_or(gsel, hit)
        gs = jnp.where(hit, neg, gs)
    emask = jnp.concatenate(
        [jnp.broadcast_to(gsel[gi:gi + 1, :], (gsz, t)) for gi in range(N_GROUPS)], axis=0)
    masked = jnp.where(emask, biased, neg)
    iota_e = lax.broadcasted_iota(I32, (n_experts, t), 0)
    ids, gates = [], []
    sel = jnp.zeros((n_experts, t), F32)
    for _ in range(TOP_K):
        _, ei = _first_index_of_max(masked, iota_e, n_experts)
        hit = iota_e == ei
        ids.append(ei)
        gates.append(jnp.sum(jnp.where(hit, scores, 0.0), axis=0, keepdims=True))
        sel = jnp.where(hit, 1.0, sel)
        masked = jnp.where(hit, neg, masked)
    ids = jnp.concatenate(ids, axis=0)
    gates = jnp.concatenate(gates, axis=0)
    gates = gates / jnp.sum(gates, axis=0, keepdims=True) * ROUTE_SCALE
    return ids, gates, sel


def _outproj_kernel(yl_ref, yh_ref, x_ref, mod_ref, gain_ref, wo_ref, rwt_ref, rb_ref,
                    x1_ref, h2_ref, eid_ref, gate_ref, rank_ref, cnt_ref, carry):
    first = jnp.logical_and(pl.program_id(0) == 0, pl.program_id(1) == 0)

    @pl.when(first)
    def _():
        carry[...] = jnp.zeros(carry.shape, F32)

    y = jnp.dot(yl_ref[0].astype(BF16), wo_ref[0], preferred_element_type=F32)
    y = y + jnp.dot(yh_ref[0].astype(BF16), wo_ref[1], preferred_element_type=F32)
    x1 = x_ref[0] + mod_ref[0, 2:3, :] * y
    x1_ref[0] = x1
    ms = jnp.mean(x1 * x1, axis=-1, keepdims=True)
    h2 = x1 * lax.rsqrt(ms + EPS) * gain_ref[...]
    h2 = h2 * (1.0 + mod_ref[0, 4:5, :]) + mod_ref[0, 3:4, :]
    h2_ref[...] = _pack_rows(h2)

    n_experts = rwt_ref.shape[0]
    t = h2.shape[0]
    logits_t = lax.dot_general(rwt_ref[...], h2, (((1,), (1,)), ((), ())),
                               preferred_element_type=F32,
                               precision=lax.Precision.HIGHEST)
    ids, gates, sel = _route(logits_t, rb_ref[...], n_experts)
    eid_ref[...] = ids
    gate_ref[...] = gates

    ti = lax.broadcasted_iota(I32, (t, t), 0)
    tj = lax.broadcasted_iota(I32, (t, t), 1)
    upper = (ti < tj).astype(BF16)
    selb = sel.astype(BF16)
    before = jnp.dot(selb, upper, preferred_element_type=F32) + carry[:, 0:1]
    iota_e = lax.broadcasted_iota(I32, (n_experts, t), 0)
    ranks = [jnp.sum(jnp.where(iota_e == ids[k:k + 1, :], before, 0.0), axis=0, keepdims=True)
             for k in range(TOP_K)]
    rank_ref[...] = jnp.concatenate(ranks, axis=0).astype(I32)
    total = carry[...] + jnp.dot(selb, jnp.ones((t, LANES), BF16), preferred_element_type=F32)
    carry[...] = total
    cnt_ref[...] = total.astype(I32)


def _outproj(ylru, yhg, x, mods, gain, wo, rwt, rb, tm):
    b, s, d = x.shape
    hw = ylru.shape[2]
    e = rwt.shape[0]
    n = b * s
    nt = s // tm
    tok = lambda bi, i: (0, bi * nt + i)
    return pl.pallas_call(
        _outproj_kernel,
        out_shape=(jax.ShapeDtypeStruct((b, s, d), F32),
                   jax.ShapeDtypeStruct((n, d // 2), U32),
                   jax.ShapeDtypeStruct((TOP_K, n), I32),
                   jax.ShapeDtypeStruct((TOP_K, n), F32),
                   jax.ShapeDtypeStruct((TOP_K, n), I32),
                   jax.ShapeDtypeStruct((e, LANES), I32)),
        grid=(b, nt),
        in_specs=[pl.BlockSpec((1, tm, hw), lambda bi, i: (bi, i, 0)),
                  pl.BlockSpec((1, tm, hw), lambda bi, i: (bi, i, 0)),
                  pl.BlockSpec((1, tm, d), lambda bi, i: (bi, i, 0)),
                  pl.BlockSpec((1, 8, d), lambda bi, i: (bi, 0, 0)),
                  pl.BlockSpec((1, d), lambda bi, i: (0, 0)),
                  pl.BlockSpec((2, hw, d), lambda bi, i: (0, 0, 0)),
                  pl.BlockSpec((e, d), lambda bi, i: (0, 0)),
                  pl.BlockSpec((e, 1), lambda bi, i: (0, 0))],
        out_specs=(pl.BlockSpec((1, tm, d), lambda bi, i: (bi, i, 0)),
                   pl.BlockSpec((tm, d // 2), lambda bi, i: (bi * nt + i, 0)),
                   pl.BlockSpec((TOP_K, tm), tok),
                   pl.BlockSpec((TOP_K, tm), tok),
                   pl.BlockSpec((TOP_K, tm), tok),
                   pl.BlockSpec((e, LANES), lambda bi, i: (0, 0))),
        scratch_shapes=[pltpu.VMEM((e, LANES), F32)],
        compiler_params=_cparams(("arbitrary", "arbitrary")),
        name="outproj",
    )(ylru, yhg, x, mods, gain, wo, rwt, rb)


def _pos_kernel(pstart, eid_ref, rank_ref, pos_ref):
    eid = eid_ref[...]

    def body(e, acc):
        return jnp.where(eid == e, pstart[e], acc)

    base = lax.fori_loop(0, pstart.shape[0], body, jnp.zeros(eid.shape, I32))
    pos_ref[...] = base + rank_ref[...]


def _pos(pstarts, eid_t, rank_t, tn):
    k, n = eid_t.shape
    spec = pl.BlockSpec((k, tn), lambda i, ps: (0, i))
    return pl.pallas_call(
        _pos_kernel,
        out_shape=jax.ShapeDtypeStruct((k, n), I32),
        grid_spec=pltpu.PrefetchScalarGridSpec(
            num_scalar_prefetch=1, grid=(n // tn,), in_specs=[spec, spec], out_specs=spec),
        compiler_params=_cparams(("arbitrary",)),
        name="pos",
    )(pstarts, eid_t, rank_t)


def _pad_fill(padstart, padlen, zbuf, xs_ref, sem, wait):
    nbits = zbuf.shape[0].bit_length() - 1
    low_bits = SUBLANES.bit_length() - 1

    def go(cp):
        if wait:
            cp.wait()
        else:
            cp.start()

    def body(e, carry):
        st = padstart[e]
        ln = padlen[e]
        end = st + ln
        for bit in range(nbits - 1, low_bits - 1, -1):
            size = 1 << bit
            back = ((ln >> (bit + 1)) << (bit + 1)) + size

            @pl.when((ln & size) != 0)
            def _():
                start = pl.multiple_of(end - back, SUBLANES)
                go(pltpu.make_async_copy(zbuf.at[pl.ds(0, size)],
                                         xs_ref.at[pl.ds(start, size)], sem))
        for j in range(SUBLANES - 1):
            @pl.when(j < (ln & (SUBLANES - 1)))
            def _():
                go(pltpu.make_async_copy(zbuf.at[pl.ds(0, 1)], xs_ref.at[pl.ds(st + j, 1)], sem))
        return carry

    lax.fori_loop(0, padstart.shape[0], body, 0)


def _padfill_kernel(padstart, padlen, xs_in, xs_ref, zbuf, sem):
    del xs_in
    zbuf[...] = jnp.zeros(zbuf.shape, zbuf.dtype)
    _pad_fill(padstart, padlen, zbuf, xs_ref, sem, wait=False)
    _pad_fill(padstart, padlen, zbuf, xs_ref, sem, wait=True)


def _padfill(xs, padstart, padlen, bm):
    return pl.pallas_call(
        _padfill_kernel,
        out_shape=jax.ShapeDtypeStruct(xs.shape, xs.dtype),
        grid_spec=pltpu.PrefetchScalarGridSpec(
            num_scalar_prefetch=2,
            grid=(1,),
            in_specs=[pl.BlockSpec(memory_space=pl.ANY)],
            out_specs=pl.BlockSpec(memory_space=pl.ANY),
            scratch_shapes=[pltpu.VMEM((bm, xs.shape[1]), xs.dtype), pltpu.SemaphoreType.DMA(())]),
        input_output_aliases={2: 0},
        compiler_params=_cparams(("arbitrary",)),
        name="padfill",
    )(padstart, padlen, xs)


SC_SCATTER_WINDOW = 128


def _sc_mesh():
    return plsc.VectorSubcoreMesh(core_axis_name="core", subcore_axis_name="subcore")


def _sc_scatter_rows(rows, pos, cap):
    n, dw = rows.shape
    top_k = pos.shape[0]
    mesh = _sc_mesh()
    n_workers = mesh.num_cores * mesh.num_subcores
    win = SC_SCATTER_WINDOW
    per_w = n // n_workers
    n_chunks = per_w // win
    assert per_w * n_workers == n and n_chunks * win == per_w
    pos_w = pos.reshape(top_k, n_workers, n_chunks, win).transpose(1, 2, 0, 3)
    pos_w = pos_w.reshape(n_workers, n_chunks * top_k, win)

    @functools.partial(pl.kernel, mesh=mesh,
                       out_type=jax.ShapeDtypeStruct((cap, dw), rows.dtype),
                       scratch_types=[pltpu.VMEM((n_chunks * top_k, win), I32),
                                      pltpu.VMEM((win, dw), rows.dtype),
                                      pltpu.SemaphoreType.DMA])
    def scatter(r_hbm, p_hbm, o_hbm, idx_v, rows_v, sem):
        wid = lax.axis_index("subcore") * mesh.num_cores + lax.axis_index("core")
        pltpu.sync_copy(p_hbm.at[wid], idx_v)

        @pl.loop(0, n_chunks)
        def _(c):
            pltpu.sync_copy(r_hbm.at[pl.ds(wid * per_w + c * win, win)], rows_v)
            copies = [pltpu.async_copy(rows_v, o_hbm.at[idx_v.at[c * top_k + k]], sem)
                      for k in range(top_k)]
            for cp in copies:
                cp.wait()

    return scatter(rows, pos_w)


def _gmm_rows(xw, w13b, w2b, de):
    xb = _unpack_rows(xw).astype(BF16)
    u = jnp.dot(xb, w13b[...], preferred_element_type=F32)
    hmid = _silu(u[:, :de]) * u[:, de:]
    return _pack_rows(jnp.dot(hmid.astype(BF16), w2b[...], preferred_element_type=F32))


def _gmm_kernel(blk0, nblk, w13_ref, w2_ref, xs_ref, ys_ref, w13b, w2b, xbuf, ybuf, isem, osem):
    e = pl.program_id(0)
    n_e = pl.num_programs(0)
    bm = xbuf.shape[1]
    de = w2_ref.shape[1]
    nb = nblk[e]
    b0 = blk0[e]

    def x_copy(blk, slot):
        rows = pl.ds(pl.multiple_of(blk * bm, bm), bm)
        return pltpu.make_async_copy(xs_ref.at[rows], xbuf.at[slot], isem.at[slot])

    def y_copy(blk, slot):
        rows = pl.ds(pl.multiple_of(blk * bm, bm), bm)
        return pltpu.make_async_copy(ybuf.at[slot], ys_ref.at[rows], osem.at[slot])

    @pl.when(jnp.logical_and(e == 0, nb > 0))
    def _():
        x_copy(b0, 0).start()

    w13b[...] = w13_ref[0].astype(BF16)
    w2b[...] = w2_ref[0].astype(BF16)

    def body(b, carry):
        slot = b & 1
        x_copy(b0 + b, slot).wait()

        @pl.when(b + 1 < nb)
        def _():
            x_copy(b0 + b + 1, 1 - slot).start()

        @pl.when(b >= 2)
        def _():
            y_copy(b0 + b - 2, slot).wait()

        ybuf[slot] = _gmm_rows(xbuf[slot], w13b, w2b, de)
        y_copy(b0 + b, slot).start()
        return carry

    lax.fori_loop(0, nb, body, 0)

    @pl.when(nb >= 2)
    def _():
        y_copy(b0 + nb - 2, (nb - 2) & 1).wait()

    @pl.when(nb >= 1)
    def _():
        y_copy(b0 + nb - 1, (nb - 1) & 1).wait()

    nxt = jnp.minimum(e + 1, n_e - 1)

    @pl.when(jnp.logical_and(e + 1 < n_e, nblk[nxt] > 0))
    def _():
        x_copy(blk0[nxt], 0).start()


def _gmm(xs, w13, w2, blk0, nblk, bm):
    cap, dw = xs.shape
    e, d, de2 = w13.shape
    de = w2.shape[1]
    return pl.pallas_call(
        _gmm_kernel,
        out_shape=jax.ShapeDtypeStruct((cap, dw), U32),
        grid_spec=pltpu.PrefetchScalarGridSpec(
            num_scalar_prefetch=2,
            grid=(e,),
            in_specs=[pl.BlockSpec((1, d, de2), lambda i, b0, nb: (i, 0, 0)),
                      pl.BlockSpec((1, de, d), lambda i, b0, nb: (i, 0, 0)),
                      pl.BlockSpec(memory_space=pl.ANY)],
            out_specs=pl.BlockSpec(memory_space=pl.ANY),
            scratch_shapes=[pltpu.VMEM((d, de2), BF16), pltpu.VMEM((de, d), BF16),
                            pltpu.VMEM((2, bm, dw), U32), pltpu.VMEM((2, bm, dw), U32),
                            pltpu.SemaphoreType.DMA((2,)), pltpu.SemaphoreType.DMA((2,))]),
        compiler_params=_cparams(("arbitrary",)),
        name="gmm",
    )(blk0, nblk, w13, w2, xs)


def _sc_gather_rows(table, idx, window):
    n_idx = idx.shape[0]
    dw = table.shape[1]
    mesh = _sc_mesh()
    n_workers = mesh.num_cores * mesh.num_subcores
    per_w = n_idx // n_workers
    n_chunks = per_w // window
    assert per_w * n_workers == n_idx and n_chunks * window == per_w and n_chunks % 2 == 0

    @functools.partial(pl.kernel, mesh=mesh,
                       out_type=jax.ShapeDtypeStruct((n_idx, dw), table.dtype),
                       scratch_types=[pltpu.VMEM((per_w,), I32),
                                      pltpu.VMEM((window, dw), table.dtype),
                                      pltpu.VMEM((window, dw), table.dtype),
                                      pltpu.SemaphoreType.DMA, pltpu.SemaphoreType.DMA,
                                      pltpu.SemaphoreType.DMA, pltpu.SemaphoreType.DMA])
    def gather(x_hbm, i_hbm, o_hbm, idx_v, rows0, rows1, gs0, gs1, os0, os1):
        wid = lax.axis_index("subcore") * mesh.num_cores + lax.axis_index("core")
        base = wid * per_w
        pltpu.sync_copy(i_hbm.at[pl.ds(base, per_w)], idx_v)
        bufs = ((rows0, gs0, os0), (rows1, gs1, os1))

        def fetch(j, slot):
            rows, gs, _ = bufs[slot]
            off = pl.multiple_of(j * window, window)
            return pltpu.async_copy(x_hbm.at[idx_v.at[pl.ds(off, window)]], rows, gs)

        def fetch_wait(slot):
            rows, gs, _ = bufs[slot]
            pltpu.make_async_copy(x_hbm.at[idx_v.at[pl.ds(0, window)]], rows, gs).wait()

        def put(j, slot):
            rows, _, osem = bufs[slot]
            off = pl.multiple_of(j * window, window)
            return pltpu.async_copy(rows, o_hbm.at[pl.ds(base + off, window)], osem)

        def put_wait(slot):
            rows, _, osem = bufs[slot]
            pltpu.make_async_copy(rows, o_hbm.at[pl.ds(base, window)], osem).wait()

        fetch(0, 0)

        @pl.loop(0, n_chunks, step=2)
        def _(j):
            @pl.when(j > 0)
            def _():
                put_wait(1)
            fetch(j + 1, 1)
            fetch_wait(0)
            put(j, 0)
            put_wait(0)

            @pl.when(j + 2 < n_chunks)
            def _():
                fetch(j + 2, 0)
            fetch_wait(1)
            put(j + 1, 1)

        put_wait(1)

    return gather(table, idx)


def _combine_kernel(gate_ref, x1_ref, h2_ref, mod_ref, gain_ref, sw13_ref, sw2_ref, yg_ref, o_ref):
    ds_ = sw2_ref.shape[0]
    hb = _unpack_rows(h2_ref[...]).astype(BF16)
    u = jnp.dot(hb, sw13_ref[...], preferred_element_type=F32)
    hmid = _silu(u[:, :ds_]) * u[:, ds_:]
    y = jnp.dot(hmid.astype(BF16), sw2_ref[...], preferred_element_type=F32)

    gate = gate_ref[...]
    moe = gate[:, 0:1] * _unpack_rows(yg_ref[0])
    for k in range(1, TOP_K):
        moe = moe + gate[:, k:k + 1] * _unpack_rows(yg_ref[k])
    xo = x1_ref[...] + mod_ref[0, 5:6, :] * (moe + y)
    ms = jnp.mean(xo * xo, axis=-1, keepdims=True)
    o_ref[...] = xo * lax.rsqrt(ms + EPS) * gain_ref[...]


def _combine(gate_tok, x1, h2, mods, gain, sw13, sw2, yg, tc, tiles_per_batch):
    n, d = x1.shape
    dw = h2.shape[1]
    nt = n // tc
    ds2 = sw13.shape[1]
    ds_ = sw2.shape[0]
    return pl.pallas_call(
        _combine_kernel,
        out_shape=jax.ShapeDtypeStruct((n, d), F32),
        grid=(nt,),
        in_specs=[pl.BlockSpec((tc, TOP_K), lambda i: (i, 0)),
                  pl.BlockSpec((tc, d), lambda i: (i, 0)),
                  pl.BlockSpec((tc, dw), lambda i: (i, 0)),
                  pl.BlockSpec((1, 8, d), lambda i: (i // tiles_per_batch, 0, 0)),
                  pl.BlockSpec((1, d), lambda i: (0, 0)),
                  pl.BlockSpec((d, ds2), lambda i: (0, 0)),
                  pl.BlockSpec((ds_, d), lambda i: (0, 0)),
                  pl.BlockSpec((TOP_K, tc, dw), lambda i: (0, i, 0))],
        out_specs=pl.BlockSpec((tc, d), lambda i: (i, 0)),
        compiler_params=_cparams(("arbitrary",)),
        name="combine",
    )(gate_tok, x1, h2, mods, gain, sw13, sw2, yg)


def _block_diag_pairs(w, per):
    nb, bd, _ = w.shape
    w = w.reshape(nb // per, per, bd, bd)
    eye = jnp.eye(per, dtype=w.dtype)
    out = jnp.einsum("gpij,pq->gpiqj", w, eye)
    return out.reshape(nb // per, per * bd, per * bd)


def kernel(x, c, ctx, c_ctx, ada_w, ada_b, norm_mix, norm_ffn, norm_final, w_in, w_out,
           lru_conv_w, lru_conv_b, lru_wa, lru_ba, lru_wx, lru_bx, lru_lambda,
           hgrn_lb_logits, hgrn_norm, router_w, router_b, exp_w13, exp_w2, shared_w13, shared_w2):
    assert ada_w.shape[0] == 1, "single-layer block"
    b, s, d = x.shape
    n = b * s
    lru_w = lru_conv_w.shape[2]
    hg_w = hgrn_norm.shape[1]
    n_experts = router_w.shape[2]

    rows = -(-(b + 1) // SUBLANES) * SUBLANES
    cs = jnp.zeros((rows, d), F32).at[:b].set(c).at[b].set(c_ctx)
    mod = _ada(cs, ada_w[0], ada_b[0][None, :]).reshape(rows, 6, d)
    mods = jnp.pad(mod, ((0, 0), (0, 2), (0, 0)))

    w_in_bf = w_in[0].astype(BF16)
    gain1 = norm_mix[0][None, :]
    projx = _inproj(x, mods, gain1, w_in_bf, 512, shared_mod=False)
    projc = _inproj(ctx, mods[b:b + 1], gain1, w_in_bf, ctx.shape[1], shared_mod=True)

    per = LRU_G // (lru_w // LRU_BLOCKS)
    wg = jnp.stack([_block_diag_pairs(lru_wa[0, 0], per), _block_diag_pairs(lru_wx[0, 0], per),
                    _block_diag_pairs(lru_wa[0, 1], per), _block_diag_pairs(lru_wx[0, 1], per)]
                   ).astype(BF16)
    bg = jnp.stack([lru_ba[0, 0], lru_bx[0, 0], lru_ba[0, 1], lru_bx[0, 1]])
    ylru = _lru(projx, projc, lru_conv_w[0], lru_conv_b[0][None, :], wg, bg, lru_lambda[0])

    yhg = _hgrn(projx, projc, hgrn_lb_logits, hgrn_norm[0][None, :], (2 * lru_w) // LANES)

    wo = w_out[0].astype(BF16).reshape(2, lru_w, d)
    x1, h2, eid_t, gate_t, rank_t, cnt = _outproj(
        ylru, yhg, x, mods, norm_ffn[0][None, :], wo,
        router_w[0].T, router_b[0][:, None], 512)

    bm = 256
    counts = cnt[:, 0]
    padded = (counts + bm - 1) // bm * bm
    pends = jnp.cumsum(padded)
    pstarts = pends - padded
    pos = _pos(pstarts, eid_t, rank_t, 2048)
    cap = n * TOP_K + n_experts * bm

    xs = _padfill(_sc_scatter_rows(h2, pos, cap), pstarts + counts, padded - counts, bm)
    ys = _gmm(xs, exp_w13[0], exp_w2[0], pstarts // bm, padded // bm, bm)

    yg = _sc_gather_rows(ys, pos.reshape(-1), 64).reshape(TOP_K, n, d // 2)
    tc = 256
    out = _combine(gate_t.T, x1.reshape(n, d), h2, mods,
                   norm_final[None, :], shared_w13[0].astype(BF16), shared_w2[0].astype(BF16),
                   yg, tc, s // tc)
    return out.reshape(b, s, d)
```

```python
import functools

import jax
import jax.numpy as jnp
from jax import lax
from jax.experimental import pallas as pl
from jax.experimental.pallas import tpu as pltpu
from jax.experimental.pallas import tpu_sc as plsc

F32 = jnp.float32
BF16 = jnp.bfloat16
I32 = jnp.int32
U32 = jnp.uint32

EPS = 1e-6
LRU_C = 8.0
ROUTE_SCALE = 2.5
GRID_W = 64
HG_CHUNK = 32
N_GROUPS = 8
TOPK_GROUPS = 4
TOP_K = 8
LRU_BLOCKS = 8
HG_HEADS = 4
CONV_W = 4
CONV_PAD_L = 1

SUBLANES = 8
LANES = 128
N_SEG = SUBLANES
VMEM_LIMIT = 56 * 1024 * 1024


def _cparams(sem, vmem=VMEM_LIMIT):
    return pltpu.CompilerParams(dimension_semantics=sem, vmem_limit_bytes=vmem)


def _sigmoid(x):
    return jax.nn.sigmoid(x)


def _silu(x):
    return x * _sigmoid(x)


def _pack_rows(x):
    w = x.shape[1] // 2
    bits = pltpu.bitcast(x.astype(BF16).astype(F32), U32)
    return (bits[:, :w] >> 16) | (bits[:, w:] & jnp.uint32(0xFFFF0000))


def _unpack_rows(p):
    lo = pltpu.bitcast(p << 16, F32)
    hi = pltpu.bitcast(p & jnp.uint32(0xFFFF0000), F32)
    return jnp.concatenate([lo, hi], axis=1)


def _gelu_tanh(x):
    c = 0.7978845608028654
    return 0.5 * x * (1.0 + jnp.tanh(c * (x + 0.044715 * (x * x * x))))


def _ada_kernel(c_ref, w_ref, b_ref, o_ref):
    s = _silu(c_ref[...])
    o_ref[...] = jnp.dot(s, w_ref[...], preferred_element_type=F32,
                         precision=lax.Precision.HIGHEST) + b_ref[...]


def _ada(cs, w, b):
    rows, d = cs.shape
    n = w.shape[1]
    bn = 1024
    return pl.pallas_call(
        _ada_kernel,
        out_shape=jax.ShapeDtypeStruct((rows, n), F32),
        grid=(n // bn,),
        in_specs=[pl.BlockSpec((rows, d), lambda j: (0, 0)),
                  pl.BlockSpec((d, bn), lambda j: (0, j)),
                  pl.BlockSpec((1, bn), lambda j: (0, j))],
        out_specs=pl.BlockSpec((rows, bn), lambda j: (0, j)),
        compiler_params=_cparams(("arbitrary",)),
        name="ada",
    )(cs, w, b)


def _inproj_kernel(x_ref, mod_ref, gain_ref, w_ref, o_ref):
    x = x_ref[0]
    ms = jnp.mean(x * x, axis=-1, keepdims=True)
    y = x * lax.rsqrt(ms + EPS) * gain_ref[...]
    h = y * (1.0 + mod_ref[0, 1:2, :]) + mod_ref[0, 0:1, :]
    o_ref[0] = jnp.dot(h.astype(BF16), w_ref[...], preferred_element_type=F32)


def _inproj(x, mods, gain, w_bf, tm, shared_mod):
    b, s, d = x.shape
    n = w_bf.shape[1]
    mod_map = (lambda bi, i: (0, 0, 0)) if shared_mod else (lambda bi, i: (bi, 0, 0))
    return pl.pallas_call(
        _inproj_kernel,
        out_shape=jax.ShapeDtypeStruct((b, s, n), F32),
        grid=(b, s // tm),
        in_specs=[pl.BlockSpec((1, tm, d), lambda bi, i: (bi, i, 0)),
                  pl.BlockSpec((1, 8, d), mod_map),
                  pl.BlockSpec((1, d), lambda bi, i: (0, 0)),
                  pl.BlockSpec((d, n), lambda bi, i: (0, 0))],
        out_specs=pl.BlockSpec((1, tm, n), lambda bi, i: (bi, i, 0)),
        compiler_params=_cparams(("arbitrary", "arbitrary")),
        name="inproj",
    )(x, mods, gain, w_bf)


LRU_G = 256
LRU_CHUNK = 256


def _seg_pitch(seg_len):
    return seg_len + SUBLANES


def _lane_store(ref, d, rows, val):
    nl = val.shape[-1] // LANES
    for l in range(nl):
        ref[d * nl + l, rows, :] = val[:, l * LANES:(l + 1) * LANES]


def _lane_load(ref, d, rows, nl):
    return jnp.concatenate([ref[d * nl + l, rows, :] for l in range(nl)], axis=-1)


def _lru_coeffs(pad_ref, t0, rows, cw, cb, wg_ref, bg, sp, a_ref, b_ref, dst0):
    win = pad_ref[pl.ds(t0, rows + 2 * SUBLANES), :]
    u = cb
    for k in range(CONV_W):
        off = SUBLANES - CONV_PAD_L + k
        u = u + win[off:off + rows, :] * cw[k:k + 1, :]
    ub = u.astype(BF16)
    for d in range(2):
        r = _sigmoid(jnp.dot(ub, wg_ref[2 * d, 0], preferred_element_type=F32)
                     + bg[2 * d:2 * d + 1, :])
        ig = _sigmoid(jnp.dot(ub, wg_ref[2 * d + 1, 0], preferred_element_type=F32)
                      + bg[2 * d + 1:2 * d + 2, :])
        log_a = (-LRU_C) * r * sp[d:d + 1, :]
        a = jnp.exp(log_a)
        th = jnp.tanh(log_a)
        one_minus_a2 = (-2.0 * th) / (1.0 - th)
        bb = jnp.sqrt(one_minus_a2) * (ig * u)
        _lane_store(a_ref, d, pl.ds(dst0, rows), a)
        _lane_store(b_ref, d, pl.ds(dst0, rows), bb)


def _seg_scan(a_ref, b_ref, seg_len, pitch, unroll):
    n_lead = a_ref.shape[0]
    nl = n_lead // 2
    zero = jnp.zeros((N_SEG, LANES), F32)
    one = jnp.ones((N_SEG, LANES), F32)
    init = tuple((zero, one) for _ in range(n_lead))

    def step(t, carry):
        out = []
        for i in range(n_lead):
            h, p = carry[i]
            tt = t if i < nl else seg_len - 1 - t
            rows = pl.ds(tt, N_SEG, stride=pitch)
            a = a_ref[i, rows, :]
            h = a * h + b_ref[i, rows, :]
            p = p * a
            b_ref[i, rows, :] = h
            a_ref[i, rows, :] = p
            out.append((h, p))
        return tuple(out)

    def body(i, carry):
        for j in range(unroll):
            carry = step(i * unroll + j, carry)
        return carry

    ends = lax.fori_loop(0, seg_len // unroll, body, init)
    h_end = [jnp.concatenate([ends[d * nl + l][0] for l in range(nl)], axis=-1) for d in range(2)]
    p_end = [jnp.concatenate([ends[d * nl + l][1] for l in range(nl)], axis=-1) for d in range(2)]
    return h_end, p_end


def _seg_carries(h_end, p_end, h0, reverse):
    order = range(N_SEG - 1, -1, -1) if reverse else range(N_SEG)
    cin = [None] * N_SEG
    c = h0
    for s in order:
        cin[s] = c
        c = p_end[s:s + 1, :] * c + h_end[s:s + 1, :]
    return cin, c


def _lru_kernel(rx_ref, rg_ref, rc_ref, cw_ref, cb_ref, wg_ref, bg_ref, lam_ref, o_ref,
                padl, padc, a_l, b_l, a_c, b_c):
    s_len = rx_ref.shape[1]
    c_len = rc_ref.shape[1]
    g = rx_ref.shape[2]
    nl = g // LANES
    seg_l = s_len // N_SEG
    seg_c = c_len // N_SEG
    pitch_l = _seg_pitch(seg_l)
    pitch_c = _seg_pitch(seg_c)

    zeros = jnp.zeros((SUBLANES, g), F32)
    padl[0:SUBLANES, :] = zeros
    padl[SUBLANES + s_len:2 * SUBLANES + s_len, :] = zeros
    padc[0:SUBLANES, :] = zeros
    padc[SUBLANES + c_len:2 * SUBLANES + c_len, :] = zeros
    for i in range(s_len // LRU_CHUNK):
        padl[SUBLANES + i * LRU_CHUNK:SUBLANES + (i + 1) * LRU_CHUNK, :] = (
            rx_ref[0, i * LRU_CHUNK:(i + 1) * LRU_CHUNK, :])
    padc[SUBLANES:SUBLANES + c_len, :] = rc_ref[0]

    cw = cw_ref[...]
    cb = cb_ref[...]
    bg = bg_ref[...]
    x = -lam_ref[...]
    sp = jnp.maximum(x, 0.0) + jnp.log(1.0 + jnp.exp(-jnp.abs(x)))

    _lru_coeffs(padc, 0, c_len, cw, cb, wg_ref, bg, sp, a_l, b_l, 0)
    for i in range(2 * nl):
        for s in range(N_SEG):
            a_c[i, s * pitch_c:s * pitch_c + seg_c, :] = a_l[i, s * seg_c:(s + 1) * seg_c, :]
            b_c[i, s * pitch_c:s * pitch_c + seg_c, :] = b_l[i, s * seg_c:(s + 1) * seg_c, :]
    h_end, p_end = _seg_scan(a_c, b_c, seg_c, pitch_c, unroll=4)
    zero_row = jnp.zeros((1, g), F32)
    _, h0_f = _seg_carries(h_end[0], p_end[0], zero_row, reverse=False)
    _, h0_b = _seg_carries(h_end[1], p_end[1], zero_row, reverse=True)

    def coeff_body(s, carry):
        t0 = pl.multiple_of(s * seg_l, SUBLANES)
        dst = pl.multiple_of(s * pitch_l, SUBLANES)
        _lru_coeffs(padl, t0, seg_l, cw, cb, wg_ref, bg, sp, a_l, b_l, dst)
        return carry

    lax.fori_loop(0, N_SEG, coeff_body, 0)
    h_end, p_end = _seg_scan(a_l, b_l, seg_l, pitch_l, unroll=4)
    cin_f, _ = _seg_carries(h_end[0], p_end[0], h0_f, reverse=False)
    cin_b, _ = _seg_carries(h_end[1], p_end[1], h0_b, reverse=True)

    for s in range(N_SEG):
        rows = slice(s * pitch_l, s * pitch_l + seg_l)
        h = (_lane_load(b_l, 0, rows, nl) + _lane_load(a_l, 0, rows, nl) * cin_f[s]) + (
            _lane_load(b_l, 1, rows, nl) + _lane_load(a_l, 1, rows, nl) * cin_b[s])
        o_ref[0, s * seg_l:(s + 1) * seg_l, :] = _gelu_tanh(rg_ref[0, s * seg_l:(s + 1) * seg_l, :]) * h


def _lru(projx, projc, cw, cb, wg, bg, lam):
    b, s_len, _ = projx.shape
    c_len = projc.shape[1]
    w = cw.shape[1]
    g = LRU_G
    ng = w // g
    seg_l = s_len // N_SEG
    seg_c = c_len // N_SEG
    assert seg_l == LRU_CHUNK and c_len % (N_SEG * 1) == 0
    return pl.pallas_call(
        _lru_kernel,
        out_shape=jax.ShapeDtypeStruct((b, s_len, w), F32),
        grid=(b, ng),
        in_specs=[pl.BlockSpec((1, s_len, g), lambda bi, j: (bi, 0, j)),
                  pl.BlockSpec((1, s_len, g), lambda bi, j: (bi, 0, ng + j)),
                  pl.BlockSpec((1, c_len, g), lambda bi, j: (bi, 0, j)),
                  pl.BlockSpec((CONV_W, g), lambda bi, j: (0, j)),
                  pl.BlockSpec((1, g), lambda bi, j: (0, j)),
                  pl.BlockSpec((4, 1, g, g), lambda bi, j: (0, j, 0, 0)),
                  pl.BlockSpec((4, g), lambda bi, j: (0, j)),
                  pl.BlockSpec((2, g), lambda bi, j: (0, j))],
        out_specs=pl.BlockSpec((1, s_len, g), lambda bi, j: (bi, 0, j)),
        scratch_shapes=[pltpu.VMEM((s_len + 2 * SUBLANES, g), F32),
                        pltpu.VMEM((c_len + 2 * SUBLANES, g), F32),
                        pltpu.VMEM((2 * g // LANES, N_SEG * _seg_pitch(seg_l), LANES), F32),
                        pltpu.VMEM((2 * g // LANES, N_SEG * _seg_pitch(seg_l), LANES), F32),
                        pltpu.VMEM((2 * g // LANES, N_SEG * _seg_pitch(seg_c), LANES), F32),
                        pltpu.VMEM((2 * g // LANES, N_SEG * _seg_pitch(seg_c), LANES), F32)],
        compiler_params=_cparams(("arbitrary", "arbitrary")),
        name="lru",
    )(projx, projx, projc, cw, cb, wg, bg, lam)


HG_PITCH_L = GRID_W + SUBLANES
HG_PITCH_C = SUBLANES


def _hg_gates(z, lb):
    e = jnp.exp(-jnp.abs(z))
    s = 1.0 / (1.0 + e)
    es = e * s
    pos = z >= 0.0
    sig = jnp.where(pos, s, es)
    nsig = jnp.where(pos, es, s)
    logf = jnp.log(lb + (1.0 - lb) * sig)
    k = (1.0 - lb) * nsig
    return logf, k


def _hg_prepass(load, width, pitch, lb, qd, kd, kl, vs, dec):
    n_pos = HG_CHUNK
    for d, zname in enumerate(("zf", "zb")):
        order = range(n_pos) if d == 0 else range(n_pos - 1, -1, -1)
        g = jnp.zeros((width, LANES), F32)
        lbd = lb[d:d + 1, :]
        for p in order:
            logf, k = _hg_gates(load(zname, p), lbd)
            g = g + logf
            qd[d, p * pitch:p * pitch + width, :] = g
            kl[d, p * pitch:p * pitch + width, :] = k
        g_last = g
        dec[d, 0:width, :] = jnp.exp(g_last)
        for p in range(n_pos):
            rows = slice(p * pitch, p * pitch + width)
            gp = qd[d, rows, :]
            k = kl[d, rows, :]
            qd[d, rows, :] = load("q", p) * jnp.exp(gp)
            kd[d, rows, :] = k * jnp.exp(-gp)
            kl[d, rows, :] = k * jnp.exp(g_last - gp)
    for p in range(n_pos):
        vs[p * pitch:p * pitch + width, :] = load("v", p)


def _hg_chunk(d, w, pitch, qd, kd, kl, vs, dec, st, masks, o_ref):
    rows = pl.ds(w, HG_CHUNK, stride=pitch)
    q = qd[d, rows, :].astype(BF16)
    k = kd[d, rows, :].astype(BF16)
    kls = kl[d, rows, :].astype(BF16)
    v = vs[rows, :].astype(BF16)
    dc = dec[d, pl.ds(w, 1), :]
    sc = lax.dot_general(q, k, (((1,), (1,)), ((), ())), preferred_element_type=F32)
    sc = jnp.where(masks[d], sc, 0.0)
    s_t = st[d]
    if o_ref is not None:
        o = jnp.dot(sc.astype(BF16), v, preferred_element_type=F32)
        o = o + lax.dot_general(q, s_t.astype(BF16), (((1,), (1,)), ((), ())),
                                preferred_element_type=F32)
        o_ref[d, rows, :] = o
    kv_t = lax.dot_general(v, kls, (((0,), (0,)), ((), ())), preferred_element_type=F32)
    st[d] = s_t * dc + kv_t


def _hgrn_kernel(q_ref, v_ref, zf_ref, zb_ref, g_ref, qc_ref, vc_ref, zfc_ref, zbc_ref,
                 lbl_ref, gain_ref, o_ref, qd, kd, kl, vs, dec, obuf, st):
    s_len = q_ref.shape[1]
    c_len = qc_ref.shape[1]
    n_col = s_len // HG_CHUNK
    n_cc = c_len // HG_CHUNK
    assert n_col == GRID_W and n_cc == HG_PITCH_C

    lg = lbl_ref[...]
    m = jnp.max(lg, axis=1, keepdims=True)
    ex = jnp.exp(lg - m)
    lb = ex[:, 0, :] / jnp.sum(ex, axis=1)

    ii = lax.broadcasted_iota(I32, (HG_CHUNK, HG_CHUNK), 0)
    jj = lax.broadcasted_iota(I32, (HG_CHUNK, HG_CHUNK), 1)
    masks = (jj <= ii, jj >= ii)

    st[...] = jnp.zeros(st.shape, F32)

    c_refs = {"q": qc_ref, "v": vc_ref, "zf": zfc_ref, "zb": zbc_ref}

    def load_c(name, p):
        return c_refs[name][0, pl.ds(p, n_cc, stride=HG_CHUNK), :]

    _hg_prepass(load_c, n_cc, HG_PITCH_C, lb, qd, kd, kl, vs, dec)

    def ctx_body(n, carry):
        _hg_chunk(0, n, HG_PITCH_C, qd, kd, kl, vs, dec, st, masks, None)
        _hg_chunk(1, n_cc - 1 - n, HG_PITCH_C, qd, kd, kl, vs, dec, st, masks, None)
        return carry

    lax.fori_loop(0, n_cc, ctx_body, 0)

    l_refs = {"q": q_ref, "v": v_ref, "zf": zf_ref, "zb": zb_ref}

    def load_l(name, p):
        return l_refs[name][0, p * n_col:(p + 1) * n_col, :]

    _hg_prepass(load_l, n_col, HG_PITCH_L, lb, qd, kd, kl, vs, dec)

    def lat_body(i, carry):
        _hg_chunk(0, i, HG_PITCH_L, qd, kd, kl, vs, dec, st, masks, obuf)
        _hg_chunk(1, n_col - 1 - i, HG_PITCH_L, qd, kd, kl, vs, dec, st, masks, obuf)
        return carry

    lax.fori_loop(0, n_col, lat_body, 0)

    gain = gain_ref[...]
    for r in range(HG_CHUNK):
        rows = slice(r * HG_PITCH_L, r * HG_PITCH_L + n_col)
        o = obuf[0, rows, :] + obuf[1, rows, :]
        y = o * lax.rsqrt(jnp.mean(o * o, axis=-1, keepdims=True) + EPS) * gain
        o_ref[0, r * n_col:(r + 1) * n_col, :] = y * _silu(g_ref[0, r * n_col:(r + 1) * n_col, :])


def _hgrn(projx, projc, lbl, gain, col0):
    b, s_len, _ = projx.shape
    c_len = projc.shape[1]
    hw = gain.shape[1]
    nh = hw // LANES
    n_slots = lbl.shape[1]

    def xs(k):
        return pl.BlockSpec((1, s_len, LANES), lambda bi, h, k=k: (bi, 0, col0 + k * nh + h))

    def cs(k):
        return pl.BlockSpec((1, c_len, LANES), lambda bi, h, k=k: (bi, 0, col0 + k * nh + h))

    sc_rows = HG_CHUNK * HG_PITCH_L
    return pl.pallas_call(
        _hgrn_kernel,
        out_shape=jax.ShapeDtypeStruct((b, s_len, hw), F32),
        grid=(b, nh),
        in_specs=[xs(0), xs(1), xs(2), xs(3), xs(4), cs(0), cs(1), cs(2), cs(3),
                  pl.BlockSpec((2, n_slots, LANES), lambda bi, h: (0, 0, h)),
                  pl.BlockSpec((1, LANES), lambda bi, h: (0, h))],
        out_specs=pl.BlockSpec((1, s_len, LANES), lambda bi, h: (bi, 0, h)),
        scratch_shapes=[pltpu.VMEM((2, sc_rows, LANES), F32),
                        pltpu.VMEM((2, sc_rows, LANES), F32),
                        pltpu.VMEM((2, sc_rows, LANES), F32),
                        pltpu.VMEM((sc_rows, LANES), F32),
                        pltpu.VMEM((2, HG_PITCH_L, LANES), F32),
                        pltpu.VMEM((2, sc_rows, LANES), F32),
                        pltpu.VMEM((2, LANES, LANES), F32)],
        compiler_params=_cparams(("arbitrary", "arbitrary")),
        name="hgrn",
    )(projx, projx, projx, projx, projx, projc, projc, projc, projc, lbl, gain)


def _first_index_of_max(vals, iota, big):
    m = jnp.max(vals, axis=0, keepdims=True)
    idx = jnp.min(jnp.where(vals == m, iota, big), axis=0, keepdims=True)
    return m, idx


def _route(logits_t, rb, n_experts):
    t = logits_t.shape[1]
    gsz = n_experts // N_GROUPS
    neg = -jnp.inf
    scores = _sigmoid(logits_t)
    biased = scores + rb
    iota_g = lax.broadcasted_iota(I32, (gsz, t), 0)
    gscore = []
    for gi in range(N_GROUPS):
        blk = biased[gi * gsz:(gi + 1) * gsz, :]
        m1, i1 = _first_index_of_max(blk, iota_g, gsz)
        m2 = jnp.max(jnp.where(iota_g == i1, neg, blk), axis=0, keepdims=True)
        gscore.append(m1 + m2)
    gs = jnp.concatenate(gscore, axis=0)
    iota_n = lax.broadcasted_iota(I32, (N_GROUPS, t), 0)
    gsel = jnp.zeros((N_GROUPS, t), jnp.bool_)
    for _ in range(TOPK_GROUPS):
        _, gi1 = _first_index_of_max(gs, iota_n, N_GROUPS)
        hit = iota_n == gi1
        gsel = jnp.logical_or(gsel, hit)
        gs = jnp.where(hit, neg, gs)
    emask = jnp.concatenate(
        [jnp.broadcast_to(gsel[gi:gi + 1, :], (gsz, t)) for gi in range(N_GROUPS)], axis=0)
    masked = jnp.where(emask, biased, neg)
    iota_e = lax.broadcasted_iota(I32, (n_experts, t), 0)
    ids, gates = [], []
    sel = jnp.zeros((n_experts, t), F32)
    for _ in range(TOP_K):
        _, ei = _first_index_of_max(masked, iota_e, n_experts)
        hit = iota_e == ei
        ids.append(ei)
        gates.append(jnp.sum(jnp.where(hit, scores, 0.0), axis=0, keepdims=True))
        sel = jnp.where(hit, 1.0, sel)
        masked = jnp.where(hit, neg, masked)
    ids = jnp.concatenate(ids, axis=0)
    gates = jnp.concatenate(gates, axis=0)
    gates = gates / jnp.sum(gates, axis=0, keepdims=True) * ROUTE_SCALE
    return ids, gates, sel


def _outproj_kernel(yl_ref, yh_ref, x_ref, mod_ref, gain_ref, wo_ref, rwt_ref, rb_ref,
                    x1_ref, h2_ref, eid_ref, gate_ref, rank_ref, cnt_ref, carry):
    first = jnp.logical_and(pl.program_id(0) == 0, pl.program_id(1) == 0)

    @pl.when(first)
    def _():
        carry[...] = jnp.zeros(carry.shape, F32)

    y = jnp.dot(yl_ref[0].astype(BF16), wo_ref[0], preferred_element_type=F32)
    y = y + jnp.dot(yh_ref[0].astype(BF16), wo_ref[1], preferred_element_type=F32)
    x1 = x_ref[0] + mod_ref[0, 2:3, :] * y
    x1_ref[0] = x1
    ms = jnp.mean(x1 * x1, axis=-1, keepdims=True)
    h2 = x1 * lax.rsqrt(ms + EPS) * gain_ref[...]
    h2 = h2 * (1.0 + mod_ref[0, 4:5, :]) + mod_ref[0, 3:4, :]
    h2_ref[...] = _pack_rows(h2)

    n_experts = rwt_ref.shape[0]
    t = h2.shape[0]
    logits_t = lax.dot_general(rwt_ref[...], h2, (((1,), (1,)), ((), ())),
                               preferred_element_type=F32,
                               precision=lax.Precision.HIGHEST)
    ids, gates, sel = _route(logits_t, rb_ref[...], n_experts)
    eid_ref[...] = ids
    gate_ref[...] = gates

    ti = lax.broadcasted_iota(I32, (t, t), 0)
    tj = lax.broadcasted_iota(I32, (t, t), 1)
    upper = (ti < tj).astype(BF16)
    selb = sel.astype(BF16)
    before = jnp.dot(selb, upper, preferred_element_type=F32) + carry[:, 0:1]
    iota_e = lax.broadcasted_iota(I32, (n_experts, t), 0)
    ranks = [jnp.sum(jnp.where(iota_e == ids[k:k + 1, :], before, 0.0), axis=0, keepdims=True)
             for k in range(TOP_K)]
    rank_ref[...] = jnp.concatenate(ranks, axis=0).astype(I32)
    total = carry[...] + jnp.dot(selb, jnp.ones((t, LANES), BF16), preferred_element_type=F32)
    carry[...] = total
    cnt_ref[...] = total.astype(I32)


def _outproj(ylru, yhg, x, mods, gain, wo, rwt, rb, tm):
    b, s, d = x.shape
    hw = ylru.shape[2]
    e = rwt.shape[0]
    n = b * s
    nt = s // tm
    tok = lambda bi, i: (0, bi * nt + i)
    return pl.pallas_call(
        _outproj_kernel,
        out_shape=(jax.ShapeDtypeStruct((b, s, d), F32),
                   jax.ShapeDtypeStruct((n, d // 2), U32),
                   jax.ShapeDtypeStruct((TOP_K, n), I32),
                   jax.ShapeDtypeStruct((TOP_K, n), F32),
                   jax.ShapeDtypeStruct((TOP_K, n), I32),
                   jax.ShapeDtypeStruct((e, LANES), I32)),
        grid=(b, nt),
        in_specs=[pl.BlockSpec((1, tm, hw), lambda bi, i: (bi, i, 0)),
                  pl.BlockSpec((1, tm, hw), lambda bi, i: (bi, i, 0)),
                  pl.BlockSpec((1, tm, d), lambda bi, i: (bi, i, 0)),
                  pl.BlockSpec((1, 8, d), lambda bi, i: (bi, 0, 0)),
                  pl.BlockSpec((1, d), lambda bi, i: (0, 0)),
                  pl.BlockSpec((2, hw, d), lambda bi, i: (0, 0, 0)),
                  pl.BlockSpec((e, d), lambda bi, i: (0, 0)),
                  pl.BlockSpec((e, 1), lambda bi, i: (0, 0))],
        out_specs=(pl.BlockSpec((1, tm, d), lambda bi, i: (bi, i, 0)),
                   pl.BlockSpec((tm, d // 2), lambda bi, i: (bi * nt + i, 0)),
                   pl.BlockSpec((TOP_K, tm), tok),
                   pl.BlockSpec((TOP_K, tm), tok),
                   pl.BlockSpec((TOP_K, tm), tok),
                   pl.BlockSpec((e, LANES), lambda bi, i: (0, 0))),
        scratch_shapes=[pltpu.VMEM((e, LANES), F32)],
        compiler_params=_cparams(("arbitrary", "arbitrary")),
        name="outproj",
    )(ylru, yhg, x, mods, gain, wo, rwt, rb)


def _pos_kernel(pstart, eid_ref, rank_ref, pos_ref):
    eid = eid_ref[...]

    def body(e, acc):
        return jnp.where(eid == e, pstart[e], acc)

    base = lax.fori_loop(0, pstart.shape[0], body, jnp.zeros(eid.shape, I32))
    pos_ref[...] = base + rank_ref[...]


def _pos(pstarts, eid_t, rank_t, tn):
    k, n = eid_t.shape
    spec = pl.BlockSpec((k, tn), lambda i, ps: (0, i))
    return pl.pallas_call(
        _pos_kernel,
        out_shape=jax.ShapeDtypeStruct((k, n), I32),
        grid_spec=pltpu.PrefetchScalarGridSpec(
            num_scalar_prefetch=1, grid=(n // tn,), in_specs=[spec, spec], out_specs=spec),
        compiler_params=_cparams(("arbitrary",)),
        name="pos",
    )(pstarts, eid_t, rank_t)


def _pad_fill(padstart, padlen, zbuf, xs_ref, sem, wait):
    nbits = zbuf.shape[0].bit_length() - 1
    low_bits = SUBLANES.bit_length() - 1

    def go(cp):
        if wait:
            cp.wait()
        else:
            cp.start()

    def body(e, carry):
        st = padstart[e]
        ln = padlen[e]
        end = st + ln
        for bit in range(nbits - 1, low_bits - 1, -1):
            size = 1 << bit
            back = ((ln >> (bit + 1)) << (bit + 1)) + size

            @pl.when((ln & size) != 0)
            def _():
                start = pl.multiple_of(end - back, SUBLANES)
                go(pltpu.make_async_copy(zbuf.at[pl.ds(0, size)],
                                         xs_ref.at[pl.ds(start, size)], sem))
        for j in range(SUBLANES - 1):
            @pl.when(j < (ln & (SUBLANES - 1)))
            def _():
                go(pltpu.make_async_copy(zbuf.at[pl.ds(0, 1)], xs_ref.at[pl.ds(st + j, 1)], sem))
        return carry

    lax.fori_loop(0, padstart.shape[0], body, 0)


def _padfill_kernel(padstart, padlen, xs_in, xs_ref, zbuf, sem):
    del xs_in
    zbuf[...] = jnp.zeros(zbuf.shape, zbuf.dtype)
    _pad_fill(padstart, padlen, zbuf, xs_ref, sem, wait=False)
    _pad_fill(padstart, padlen, zbuf, xs_ref, sem, wait=True)


def _padfill(xs, padstart, padlen, bm):
    return pl.pallas_call(
        _padfill_kernel,
        out_shape=jax.ShapeDtypeStruct(xs.shape, xs.dtype),
        grid_spec=pltpu.PrefetchScalarGridSpec(
            num_scalar_prefetch=2,
            grid=(1,),
            in_specs=[pl.BlockSpec(memory_space=pl.ANY)],
            out_specs=pl.BlockSpec(memory_space=pl.ANY),
            scratch_shapes=[pltpu.VMEM((bm, xs.shape[1]), xs.dtype), pltpu.SemaphoreType.DMA(())]),
        input_output_aliases={2: 0},
        compiler_params=_cparams(("arbitrary",)),
        name="padfill",
    )(padstart, padlen, xs)


SC_SCATTER_WINDOW = 128


def _sc_mesh():
    return plsc.VectorSubcoreMesh(core_axis_name="core", subcore_axis_name="subcore")


def _sc_scatter_rows(rows, pos, cap):
    n, dw = rows.shape
    top_k = pos.shape[0]
    mesh = _sc_mesh()
    n_workers = mesh.num_cores * mesh.num_subcores
    win = SC_SCATTER_WINDOW
    per_w = n // n_workers
    n_chunks = per_w // win
    assert per_w * n_workers == n and n_chunks * win == per_w
    pos_w = pos.reshape(top_k, n_workers, n_chunks, win).transpose(1, 2, 0, 3)
    pos_w = pos_w.reshape(n_workers, n_chunks * top_k, win)

    @functools.partial(pl.kernel, mesh=mesh,
                       out_type=jax.ShapeDtypeStruct((cap, dw), rows.dtype),
                       scratch_types=[pltpu.VMEM((n_chunks * top_k, win), I32),
                                      pltpu.VMEM((win, dw), rows.dtype),
                                      pltpu.SemaphoreType.DMA])
    def scatter(r_hbm, p_hbm, o_hbm, idx_v, rows_v, sem):
        wid = lax.axis_index("subcore") * mesh.num_cores + lax.axis_index("core")
        pltpu.sync_copy(p_hbm.at[wid], idx_v)

        @pl.loop(0, n_chunks)
        def _(c):
            pltpu.sync_copy(r_hbm.at[pl.ds(wid * per_w + c * win, win)], rows_v)
            copies = [pltpu.async_copy(rows_v, o_hbm.at[idx_v.at[c * top_k + k]], sem)
                      for k in range(top_k)]
            for cp in copies:
                cp.wait()

    return scatter(rows, pos_w)


GMM_SLOTS = 4
GMM_AHEAD = GMM_SLOTS - 1


def _gmm_rows(xw, w13b, w2b, de):
    xb = _unpack_rows(xw).astype(BF16)
    u = jnp.dot(xb, w13b[...], preferred_element_type=F32)
    hmid = _silu(u[:, :de]) * u[:, de:]
    return _pack_rows(jnp.dot(hmid.astype(BF16), w2b[...], preferred_element_type=F32))


def _gmm_kernel(blk0, nblk, w13_ref, w2_ref, xs_ref, ys_ref, w13b, w2b, xbuf, ybuf, isem, osem):
    e = pl.program_id(0)
    n_e = pl.num_programs(0)
    slots, bm = xbuf.shape[0], xbuf.shape[1]
    de = w2_ref.shape[1]
    nb = nblk[e]
    b0 = blk0[e]
    total = blk0[n_e - 1] + nblk[n_e - 1]

    def x_copy(g):
        rows = pl.ds(pl.multiple_of(g * bm, bm), bm)
        slot = g & (slots - 1)
        return pltpu.make_async_copy(xs_ref.at[rows], xbuf.at[slot], isem.at[slot])

    def y_copy(g):
        rows = pl.ds(pl.multiple_of(g * bm, bm), bm)
        slot = g & (slots - 1)
        return pltpu.make_async_copy(ybuf.at[slot], ys_ref.at[rows], osem.at[slot])

    @pl.when(e == 0)
    def _():
        for j in range(GMM_AHEAD):
            @pl.when(j < total)
            def _():
                x_copy(j).start()

    w13b[...] = w13_ref[0].astype(BF16)
    w2b[...] = w2_ref[0].astype(BF16)

    def body(b, carry):
        g = b0 + b
        slot = g & (slots - 1)
        x_copy(g).wait()

        @pl.when(g + GMM_AHEAD < total)
        def _():
            x_copy(g + GMM_AHEAD).start()

        @pl.when(g >= slots)
        def _():
            y_copy(g - slots).wait()

        ybuf[slot] = _gmm_rows(xbuf[slot], w13b, w2b, de)
        y_copy(g).start()
        return carry

    lax.fori_loop(0, nb, body, 0)

    @pl.when(e == n_e - 1)
    def _():
        for j in range(1, slots + 1):
            @pl.when(total - j >= 0)
            def _():
                y_copy(total - j).wait()


def _gmm(xs, w13, w2, blk0, nblk, bm):
    cap, dw = xs.shape
    e, d, de2 = w13.shape
    de = w2.shape[1]
    return pl.pallas_call(
        _gmm_kernel,
        out_shape=jax.ShapeDtypeStruct((cap, dw), U32),
        grid_spec=pltpu.PrefetchScalarGridSpec(
            num_scalar_prefetch=2,
            grid=(e,),
            in_specs=[pl.BlockSpec((1, d, de2), lambda i, b0, nb: (i, 0, 0)),
                      pl.BlockSpec((1, de, d), lambda i, b0, nb: (i, 0, 0)),
                      pl.BlockSpec(memory_space=pl.ANY)],
            out_specs=pl.BlockSpec(memory_space=pl.ANY),
            scratch_shapes=[pltpu.VMEM((d, de2), BF16), pltpu.VMEM((de, d), BF16),
                            pltpu.VMEM((GMM_SLOTS, bm, dw), U32),
                            pltpu.VMEM((GMM_SLOTS, bm, dw), U32),
                            pltpu.SemaphoreType.DMA((GMM_SLOTS,)),
                            pltpu.SemaphoreType.DMA((GMM_SLOTS,))]),
        compiler_params=_cparams(("arbitrary",)),
        name="gmm",
    )(blk0, nblk, w13, w2, xs)


def _sc_gather_rows(table, idx, window):
    n_idx = idx.shape[0]
    dw = table.shape[1]
    mesh = _sc_mesh()
    n_workers = mesh.num_cores * mesh.num_subcores
    per_w = n_idx // n_workers
    n_chunks = per_w // window
    assert per_w * n_workers == n_idx and n_chunks * window == per_w and n_chunks % 2 == 0

    @functools.partial(pl.kernel, mesh=mesh,
                       out_type=jax.ShapeDtypeStruct((n_idx, dw), table.dtype),
                       scratch_types=[pltpu.VMEM((per_w,), I32),
                                      pltpu.VMEM((window, dw), table.dtype),
                                      pltpu.VMEM((window, dw), table.dtype),
                                      pltpu.SemaphoreType.DMA, pltpu.SemaphoreType.DMA,
                                      pltpu.SemaphoreType.DMA, pltpu.SemaphoreType.DMA])
    def gather(x_hbm, i_hbm, o_hbm, idx_v, rows0, rows1, gs0, gs1, os0, os1):
        wid = lax.axis_index("subcore") * mesh.num_cores + lax.axis_index("core")
        base = wid * per_w
        pltpu.sync_copy(i_hbm.at[pl.ds(base, per_w)], idx_v)
        bufs = ((rows0, gs0, os0), (rows1, gs1, os1))

        def fetch(j, slot):
            rows, gs, _ = bufs[slot]
            off = pl.multiple_of(j * window, window)
            return pltpu.async_copy(x_hbm.at[idx_v.at[pl.ds(off, window)]], rows, gs)

        def fetch_wait(slot):
            rows, gs, _ = bufs[slot]
            pltpu.make_async_copy(x_hbm.at[idx_v.at[pl.ds(0, window)]], rows, gs).wait()

        def put(j, slot):
            rows, _, osem = bufs[slot]
            off = pl.multiple_of(j * window, window)
            return pltpu.async_copy(rows, o_hbm.at[pl.ds(base + off, window)], osem)

        def put_wait(slot):
            rows, _, osem = bufs[slot]
            pltpu.make_async_copy(rows, o_hbm.at[pl.ds(base, window)], osem).wait()

        fetch(0, 0)

        @pl.loop(0, n_chunks, step=2)
        def _(j):
            @pl.when(j > 0)
            def _():
                put_wait(1)
            fetch(j + 1, 1)
            fetch_wait(0)
            put(j, 0)
            put_wait(0)

            @pl.when(j + 2 < n_chunks)
            def _():
                fetch(j + 2, 0)
            fetch_wait(1)
            put(j + 1, 1)

        put_wait(1)

    return gather(table, idx)


def _combine_kernel(gate_ref, x1_ref, h2_ref, mod_ref, gain_ref, sw13_ref, sw2_ref, yg_ref, o_ref):
    ds_ = sw2_ref.shape[0]
    hb = _unpack_rows(h2_ref[...]).astype(BF16)
    u = jnp.dot(hb, sw13_ref[...], preferred_element_type=F32)
    hmid = _silu(u[:, :ds_]) * u[:, ds_:]
    y = jnp.dot(hmid.astype(BF16), sw2_ref[...], preferred_element_type=F32)

    gate = gate_ref[...]
    moe = gate[:, 0:1] * _unpack_rows(yg_ref[0])
    for k in range(1, TOP_K):
        moe = moe + gate[:, k:k + 1] * _unpack_rows(yg_ref[k])
    xo = x1_ref[...] + mod_ref[0, 5:6, :] * (moe + y)
    ms = jnp.mean(xo * xo, axis=-1, keepdims=True)
    o_ref[...] = xo * lax.rsqrt(ms + EPS) * gain_ref[...]


def _combine(gate_tok, x1, h2, mods, gain, sw13, sw2, yg, tc, tiles_per_batch):
    n, d = x1.shape
    dw = h2.shape[1]
    nt = n // tc
    ds2 = sw13.shape[1]
    ds_ = sw2.shape[0]
    return pl.pallas_call(
        _combine_kernel,
        out_shape=jax.ShapeDtypeStruct((n, d), F32),
        grid=(nt,),
        in_specs=[pl.BlockSpec((tc, TOP_K), lambda i: (i, 0)),
                  pl.BlockSpec((tc, d), lambda i: (i, 0)),
                  pl.BlockSpec((tc, dw), lambda i: (i, 0)),
                  pl.BlockSpec((1, 8, d), lambda i: (i // tiles_per_batch, 0, 0)),
                  pl.BlockSpec((1, d), lambda i: (0, 0)),
                  pl.BlockSpec((d, ds2), lambda i: (0, 0)),
                  pl.BlockSpec((ds_, d), lambda i: (0, 0)),
                  pl.BlockSpec((TOP_K, tc, dw), lambda i: (0, i, 0))],
        out_specs=pl.BlockSpec((tc, d), lambda i: (i, 0)),
        compiler_params=_cparams(("arbitrary",)),
        name="combine",
    )(gate_tok, x1, h2, mods, gain, sw13, sw2, yg)


def _block_diag_pairs(w, per):
    nb, bd, _ = w.shape
    w = w.reshape(nb // per, per, bd, bd)
    eye = jnp.eye(per, dtype=w.dtype)
    out = jnp.einsum("gpij,pq->gpiqj", w, eye)
    return out.reshape(nb // per, per * bd, per * bd)


def kernel(x, c, ctx, c_ctx, ada_w, ada_b, norm_mix, norm_ffn, norm_final, w_in, w_out,
           lru_conv_w, lru_conv_b, lru_wa, lru_ba, lru_wx, lru_bx, lru_lambda,
           hgrn_lb_logits, hgrn_norm, router_w, router_b, exp_w13, exp_w2, shared_w13, shared_w2):
    assert ada_w.shape[0] == 1, "single-layer block"
    b, s, d = x.shape
    n = b * s
    lru_w = lru_conv_w.shape[2]
    hg_w = hgrn_norm.shape[1]
    n_experts = router_w.shape[2]

    rows = -(-(b + 1) // SUBLANES) * SUBLANES
    cs = jnp.zeros((rows, d), F32).at[:b].set(c).at[b].set(c_ctx)
    mod = _ada(cs, ada_w[0], ada_b[0][None, :]).reshape(rows, 6, d)
    mods = jnp.pad(mod, ((0, 0), (0, 2), (0, 0)))

    w_in_bf = w_in[0].astype(BF16)
    gain1 = norm_mix[0][None, :]
    projx = _inproj(x, mods, gain1, w_in_bf, 512, shared_mod=False)
    projc = _inproj(ctx, mods[b:b + 1], gain1, w_in_bf, ctx.shape[1], shared_mod=True)

    per = LRU_G // (lru_w // LRU_BLOCKS)
    wg = jnp.stack([_block_diag_pairs(lru_wa[0, 0], per), _block_diag_pairs(lru_wx[0, 0], per),
                    _block_diag_pairs(lru_wa[0, 1], per), _block_diag_pairs(lru_wx[0, 1], per)]
                   ).astype(BF16)
    bg = jnp.stack([lru_ba[0, 0], lru_bx[0, 0], lru_ba[0, 1], lru_bx[0, 1]])
    ylru = _lru(projx, projc, lru_conv_w[0], lru_conv_b[0][None, :], wg, bg, lru_lambda[0])

    yhg = _hgrn(projx, projc, hgrn_lb_logits, hgrn_norm[0][None, :], (2 * lru_w) // LANES)

    wo = w_out[0].astype(BF16).reshape(2, lru_w, d)
    x1, h2, eid_t, gate_t, rank_t, cnt = _outproj(
        ylru, yhg, x, mods, norm_ffn[0][None, :], wo,
        router_w[0].T, router_b[0][:, None], 512)

    bm = 256
    counts = cnt[:, 0]
    padded = (counts + bm - 1) // bm * bm
    pends = jnp.cumsum(padded)
    pstarts = pends - padded
    pos = _pos(pstarts, eid_t, rank_t, 2048)
    cap = n * TOP_K + n_experts * bm

    xs = _padfill(_sc_scatter_rows(h2, pos, cap), pstarts + counts, padded - counts, bm)
    ys = _gmm(xs, exp_w13[0], exp_w2[0], pstarts // bm, padded // bm, bm)

    yg = _sc_gather_rows(ys, pos.reshape(-1), 64).reshape(TOP_K, n, d // 2)
    tc = 256
    out = _combine(gate_t.T, x1.reshape(n, d), h2, mods,
                   norm_final[None, :], shared_w13[0].astype(BF16), shared_w2[0].astype(BF16),
                   yg, tc, s // tc)
    return out.reshape(b, s, d)
```

```python
import functools

import jax
import jax.numpy as jnp
from jax import lax
from jax.experimental import pallas as pl
from jax.experimental.pallas import tpu as pltpu
from jax.experimental.pallas import tpu_sc as plsc

F32 = jnp.float32
BF16 = jnp.bfloat16
I32 = jnp.int32
U32 = jnp.uint32

EPS = 1e-6
LRU_C = 8.0
ROUTE_SCALE = 2.5
GRID_W = 64
HG_CHUNK = 32
N_GROUPS = 8
TOPK_GROUPS = 4
TOP_K = 8
LRU_BLOCKS = 8
HG_HEADS = 4
CONV_W = 4
CONV_PAD_L = 1

SUBLANES = 8
LANES = 128
N_SEG = SUBLANES
VMEM_LIMIT = 56 * 1024 * 1024


def _cparams(sem, vmem=VMEM_LIMIT):
    return pltpu.CompilerParams(dimension_semantics=sem, vmem_limit_bytes=vmem)


def _sigmoid(x):
    return jax.nn.sigmoid(x)


def _silu(x):
    return x * _sigmoid(x)


def _pack_rows(x):
    w = x.shape[1] // 2
    bits = pltpu.bitcast(x.astype(BF16).astype(F32), U32)
    return (bits[:, :w] >> 16) | (bits[:, w:] & jnp.uint32(0xFFFF0000))


def _unpack_rows(p):
    lo = pltpu.bitcast(p << 16, F32)
    hi = pltpu.bitcast(p & jnp.uint32(0xFFFF0000), F32)
    return jnp.concatenate([lo, hi], axis=1)


def _gelu_tanh(x):
    c = 0.7978845608028654
    return 0.5 * x * (1.0 + jnp.tanh(c * (x + 0.044715 * (x * x * x))))


def _ada_kernel(c_ref, w_ref, b_ref, o_ref):
    s = _silu(c_ref[...])
    o_ref[...] = jnp.dot(s, w_ref[...], preferred_element_type=F32,
                         precision=lax.Precision.HIGHEST) + b_ref[...]


def _ada(cs, w, b):
    rows, d = cs.shape
    n = w.shape[1]
    bn = 1024
    return pl.pallas_call(
        _ada_kernel,
        out_shape=jax.ShapeDtypeStruct((rows, n), F32),
        grid=(n // bn,),
        in_specs=[pl.BlockSpec((rows, d), lambda j: (0, 0)),
                  pl.BlockSpec((d, bn), lambda j: (0, j)),
                  pl.BlockSpec((1, bn), lambda j: (0, j))],
        out_specs=pl.BlockSpec((rows, bn), lambda j: (0, j)),
        compiler_params=_cparams(("arbitrary",)),
        name="ada",
    )(cs, w, b)


def _inproj_kernel(x_ref, mod_ref, gain_ref, w_ref, o_ref):
    x = x_ref[0]
    ms = jnp.mean(x * x, axis=-1, keepdims=True)
    y = x * lax.rsqrt(ms + EPS) * gain_ref[...]
    h = y * (1.0 + mod_ref[0, 1:2, :]) + mod_ref[0, 0:1, :]
    o_ref[0] = jnp.dot(h.astype(BF16), w_ref[...], preferred_element_type=F32)


def _inproj(x, mods, gain, w_bf, tm, shared_mod):
    b, s, d = x.shape
    n = w_bf.shape[1]
    mod_map = (lambda bi, i: (0, 0, 0)) if shared_mod else (lambda bi, i: (bi, 0, 0))
    return pl.pallas_call(
        _inproj_kernel,
        out_shape=jax.ShapeDtypeStruct((b, s, n), F32),
        grid=(b, s // tm),
        in_specs=[pl.BlockSpec((1, tm, d), lambda bi, i: (bi, i, 0)),
                  pl.BlockSpec((1, 8, d), mod_map),
                  pl.BlockSpec((1, d), lambda bi, i: (0, 0)),
                  pl.BlockSpec((d, n), lambda bi, i: (0, 0))],
        out_specs=pl.BlockSpec((1, tm, n), lambda bi, i: (bi, i, 0)),
        compiler_params=_cparams(("arbitrary", "arbitrary")),
        name="inproj",
    )(x, mods, gain, w_bf)


LRU_G = 256
LRU_CHUNK = 256


def _seg_pitch(seg_len):
    return seg_len + SUBLANES


def _lane_store(ref, d, rows, val):
    nl = val.shape[-1] // LANES
    for l in range(nl):
        ref[d * nl + l, rows, :] = val[:, l * LANES:(l + 1) * LANES]


def _lane_load(ref, d, rows, nl):
    return jnp.concatenate([ref[d * nl + l, rows, :] for l in range(nl)], axis=-1)


def _lru_coeffs(pad_ref, t0, rows, cw, cb, wg_ref, bg, sp, a_ref, b_ref, dst0):
    win = pad_ref[pl.ds(t0, rows + 2 * SUBLANES), :]
    u = cb
    for k in range(CONV_W):
        off = SUBLANES - CONV_PAD_L + k
        u = u + win[off:off + rows, :] * cw[k:k + 1, :]
    ub = u.astype(BF16)
    for d in range(2):
        r = _sigmoid(jnp.dot(ub, wg_ref[2 * d, 0], preferred_element_type=F32)
                     + bg[2 * d:2 * d + 1, :])
        ig = _sigmoid(jnp.dot(ub, wg_ref[2 * d + 1, 0], preferred_element_type=F32)
                      + bg[2 * d + 1:2 * d + 2, :])
        log_a = (-LRU_C) * r * sp[d:d + 1, :]
        a = jnp.exp(log_a)
        th = jnp.tanh(log_a)
        one_minus_a2 = (-2.0 * th) / (1.0 - th)
        bb = jnp.sqrt(one_minus_a2) * (ig * u)
        _lane_store(a_ref, d, pl.ds(dst0, rows), a)
        _lane_store(b_ref, d, pl.ds(dst0, rows), bb)


def _seg_scan(a_ref, b_ref, seg_len, pitch, unroll):
    n_lead = a_ref.shape[0]
    nl = n_lead // 2
    zero = jnp.zeros((N_SEG, LANES), F32)
    one = jnp.ones((N_SEG, LANES), F32)
    init = tuple((zero, one) for _ in range(n_lead))

    def step(t, carry):
        out = []
        for i in range(n_lead):
            h, p = carry[i]
            tt = t if i < nl else seg_len - 1 - t
            rows = pl.ds(tt, N_SEG, stride=pitch)
            a = a_ref[i, rows, :]
            h = a * h + b_ref[i, rows, :]
            p = p * a
            b_ref[i, rows, :] = h
            a_ref[i, rows, :] = p
            out.append((h, p))
        return tuple(out)

    def body(i, carry):
        for j in range(unroll):
            carry = step(i * unroll + j, carry)
        return carry

    ends = lax.fori_loop(0, seg_len // unroll, body, init)
    h_end = [jnp.concatenate([ends[d * nl + l][0] for l in range(nl)], axis=-1) for d in range(2)]
    p_end = [jnp.concatenate([ends[d * nl + l][1] for l in range(nl)], axis=-1) for d in range(2)]
    return h_end, p_end


def _seg_carries(h_end, p_end, h0, reverse):
    order = range(N_SEG - 1, -1, -1) if reverse else range(N_SEG)
    cin = [None] * N_SEG
    c = h0
    for s in order:
        cin[s] = c
        c = p_end[s:s + 1, :] * c + h_end[s:s + 1, :]
    return cin, c


def _lru_kernel(rx_ref, rg_ref, rc_ref, cw_ref, cb_ref, wg_ref, bg_ref, lam_ref, o_ref,
                padl, padc, a_l, b_l, a_c, b_c):
    s_len = rx_ref.shape[1]
    c_len = rc_ref.shape[1]
    g = rx_ref.shape[2]
    nl = g // LANES
    seg_l = s_len // N_SEG
    seg_c = c_len // N_SEG
    pitch_l = _seg_pitch(seg_l)
    pitch_c = _seg_pitch(seg_c)

    zeros = jnp.zeros((SUBLANES, g), F32)
    padl[0:SUBLANES, :] = zeros
    padl[SUBLANES + s_len:2 * SUBLANES + s_len, :] = zeros
    padc[0:SUBLANES, :] = zeros
    padc[SUBLANES + c_len:2 * SUBLANES + c_len, :] = zeros
    for i in range(s_len // LRU_CHUNK):
        padl[SUBLANES + i * LRU_CHUNK:SUBLANES + (i + 1) * LRU_CHUNK, :] = (
            rx_ref[0, i * LRU_CHUNK:(i + 1) * LRU_CHUNK, :])
    padc[SUBLANES:SUBLANES + c_len, :] = rc_ref[0]

    cw = cw_ref[...]
    cb = cb_ref[...]
    bg = bg_ref[...]
    x = -lam_ref[...]
    sp = jnp.maximum(x, 0.0) + jnp.log(1.0 + jnp.exp(-jnp.abs(x)))

    _lru_coeffs(padc, 0, c_len, cw, cb, wg_ref, bg, sp, a_l, b_l, 0)
    for i in range(2 * nl):
        for s in range(N_SEG):
            a_c[i, s * pitch_c:s * pitch_c + seg_c, :] = a_l[i, s * seg_c:(s + 1) * seg_c, :]
            b_c[i, s * pitch_c:s * pitch_c + seg_c, :] = b_l[i, s * seg_c:(s + 1) * seg_c, :]
    h_end, p_end = _seg_scan(a_c, b_c, seg_c, pitch_c, unroll=4)
    zero_row = jnp.zeros((1, g), F32)
    _, h0_f = _seg_carries(h_end[0], p_end[0], zero_row, reverse=False)
    _, h0_b = _seg_carries(h_end[1], p_end[1], zero_row, reverse=True)

    def coeff_body(s, carry):
        t0 = pl.multiple_of(s * seg_l, SUBLANES)
        dst = pl.multiple_of(s * pitch_l, SUBLANES)
        _lru_coeffs(padl, t0, seg_l, cw, cb, wg_ref, bg, sp, a_l, b_l, dst)
        return carry

    lax.fori_loop(0, N_SEG, coeff_body, 0)
    h_end, p_end = _seg_scan(a_l, b_l, seg_l, pitch_l, unroll=4)
    cin_f, _ = _seg_carries(h_end[0], p_end[0], h0_f, reverse=False)
    cin_b, _ = _seg_carries(h_end[1], p_end[1], h0_b, reverse=True)

    for s in range(N_SEG):
        rows = slice(s * pitch_l, s * pitch_l + seg_l)
        h = (_lane_load(b_l, 0, rows, nl) + _lane_load(a_l, 0, rows, nl) * cin_f[s]) + (
            _lane_load(b_l, 1, rows, nl) + _lane_load(a_l, 1, rows, nl) * cin_b[s])
        o_ref[0, s * seg_l:(s + 1) * seg_l, :] = _gelu_tanh(rg_ref[0, s * seg_l:(s + 1) * seg_l, :]) * h


def _lru(projx, projc, cw, cb, wg, bg, lam):
    b, s_len, _ = projx.shape
    c_len = projc.shape[1]
    w = cw.shape[1]
    g = LRU_G
    ng = w // g
    seg_l = s_len // N_SEG
    seg_c = c_len // N_SEG
    assert seg_l == LRU_CHUNK and c_len % (N_SEG * 1) == 0
    return pl.pallas_call(
        _lru_kernel,
        out_shape=jax.ShapeDtypeStruct((b, s_len, w), F32),
        grid=(b, ng),
        in_specs=[pl.BlockSpec((1, s_len, g), lambda bi, j: (bi, 0, j)),
                  pl.BlockSpec((1, s_len, g), lambda bi, j: (bi, 0, ng + j)),
                  pl.BlockSpec((1, c_len, g), lambda bi, j: (bi, 0, j)),
                  pl.BlockSpec((CONV_W, g), lambda bi, j: (0, j)),
                  pl.BlockSpec((1, g), lambda bi, j: (0, j)),
                  pl.BlockSpec((4, 1, g, g), lambda bi, j: (0, j, 0, 0)),
                  pl.BlockSpec((4, g), lambda bi, j: (0, j)),
                  pl.BlockSpec((2, g), lambda bi, j: (0, j))],
        out_specs=pl.BlockSpec((1, s_len, g), lambda bi, j: (bi, 0, j)),
        scratch_shapes=[pltpu.VMEM((s_len + 2 * SUBLANES, g), F32),
                        pltpu.VMEM((c_len + 2 * SUBLANES, g), F32),
                        pltpu.VMEM((2 * g // LANES, N_SEG * _seg_pitch(seg_l), LANES), F32),
                        pltpu.VMEM((2 * g // LANES, N_SEG * _seg_pitch(seg_l), LANES), F32),
                        pltpu.VMEM((2 * g // LANES, N_SEG * _seg_pitch(seg_c), LANES), F32),
                        pltpu.VMEM((2 * g // LANES, N_SEG * _seg_pitch(seg_c), LANES), F32)],
        compiler_params=_cparams(("arbitrary", "arbitrary")),
        name="lru",
    )(projx, projx, projc, cw, cb, wg, bg, lam)


HG_PITCH_L = GRID_W + SUBLANES
HG_PITCH_C = SUBLANES


def _hg_gates(z, lb):
    e = jnp.exp(-jnp.abs(z))
    s = 1.0 / (1.0 + e)
    es = e * s
    pos = z >= 0.0
    sig = jnp.where(pos, s, es)
    nsig = jnp.where(pos, es, s)
    logf = jnp.log(lb + (1.0 - lb) * sig)
    k = (1.0 - lb) * nsig
    return logf, k


def _hg_prepass(load, width, pitch, lb, qd, kd, kl, vs, dec):
    n_pos = HG_CHUNK
    for d, zname in enumerate(("zf", "zb")):
        order = range(n_pos) if d == 0 else range(n_pos - 1, -1, -1)
        g = jnp.zeros((width, LANES), F32)
        lbd = lb[d:d + 1, :]
        for p in order:
            logf, k = _hg_gates(load(zname, p), lbd)
            g = g + logf
            qd[d, p * pitch:p * pitch + width, :] = g
            kl[d, p * pitch:p * pitch + width, :] = k
        g_last = g
        dec[d, 0:width, :] = jnp.exp(g_last)
        for p in range(n_pos):
            rows = slice(p * pitch, p * pitch + width)
            gp = qd[d, rows, :]
            k = kl[d, rows, :]
            qd[d, rows, :] = load("q", p) * jnp.exp(gp)
            kd[d, rows, :] = k * jnp.exp(-gp)
            kl[d, rows, :] = k * jnp.exp(g_last - gp)
    for p in range(n_pos):
        vs[p * pitch:p * pitch + width, :] = load("v", p)


HG_GROUP = 4
HG_UNROLL = 4
HG_INTRA_UNROLL = 4
_NT = (((1,), (1,)), ((), ()))
_TN = (((0,), (0,)), ((), ()))


def _hg_group_rows(ref, lead, c0, pitch):
    parts = []
    for j in range(HG_GROUP):
        rows = pl.ds(c0 + j, HG_CHUNK, stride=pitch)
        parts.append(ref[rows, :] if lead is None else ref[lead, rows, :])
    return jnp.concatenate(parts, axis=0)


def _hg_intra(d, c0, pitch, qd, kd, kl, vs, mask, oi, kv):
    kls = _hg_group_rows(kl, d, c0, pitch).astype(BF16)
    v = _hg_group_rows(vs, None, c0, pitch).astype(BF16)
    if oi is not None:
        q = _hg_group_rows(qd, d, c0, pitch).astype(BF16)
        k = _hg_group_rows(kd, d, c0, pitch).astype(BF16)
        sc = lax.dot_general(q, k, _NT, preferred_element_type=F32)
        sc = jnp.where(mask, sc, 0.0)
        o = jnp.dot(sc.astype(BF16), v, preferred_element_type=F32)
        for j in range(HG_GROUP):
            oi[d, pl.ds(c0 + j, HG_CHUNK, stride=pitch), :] = o[j * HG_CHUNK:(j + 1) * HG_CHUNK, :]
    for j in range(HG_GROUP):
        sl = slice(j * HG_CHUNK, (j + 1) * HG_CHUNK)
        kv[d, c0 + j] = lax.dot_general(v[sl], kls[sl], _TN, preferred_element_type=F32)


def _hg_steps(step0, n_steps, n_chunks, pitch, qd, dec, kv, states, ox):
    states = list(states)
    for u in range(n_steps):
        for d in range(2):
            c = step0 + u if d == 0 else n_chunks - 1 - (step0 + u)
            s_t = states[d]
            if ox is not None:
                rows = pl.ds(c, HG_CHUNK, stride=pitch)
                q = qd[d, rows, :].astype(BF16)
                ox[d, rows, :] = lax.dot_general(q, s_t.astype(BF16), _NT,
                                                 preferred_element_type=F32)
            states[d] = s_t * dec[d, pl.ds(c, 1), :] + kv[d, c]
    return tuple(states)


def _hgrn_kernel(q_ref, v_ref, zf_ref, zb_ref, g_ref, qc_ref, vc_ref, zfc_ref, zbc_ref,
                 lbl_ref, gain_ref, o_ref, qd, kd, kl, vs, dec, oi, ox, kv):
    s_len = q_ref.shape[1]
    c_len = qc_ref.shape[1]
    n_col = s_len // HG_CHUNK
    n_cc = c_len // HG_CHUNK
    assert n_col == GRID_W and n_cc == HG_PITCH_C
    assert n_col % HG_GROUP == 0 and n_cc % HG_GROUP == 0 and n_col % HG_UNROLL == 0

    lg = lbl_ref[...]
    m = jnp.max(lg, axis=1, keepdims=True)
    ex = jnp.exp(lg - m)
    lb = ex[:, 0, :] / jnp.sum(ex, axis=1)

    gr = HG_GROUP * HG_CHUNK
    ii = lax.broadcasted_iota(I32, (gr, gr), 0)
    jj = lax.broadcasted_iota(I32, (gr, gr), 1)
    same = (ii // HG_CHUNK) == (jj // HG_CHUNK)
    masks = (jnp.logical_and(same, jj <= ii), jnp.logical_and(same, jj >= ii))

    c_refs = {"q": qc_ref, "v": vc_ref, "zf": zfc_ref, "zb": zbc_ref}

    def load_c(name, p):
        return c_refs[name][0, pl.ds(p, n_cc, stride=HG_CHUNK), :]

    _hg_prepass(load_c, n_cc, HG_PITCH_C, lb, qd, kd, kl, vs, dec)
    for d in range(2):
        for c0 in range(0, n_cc, HG_GROUP):
            _hg_intra(d, c0, HG_PITCH_C, qd, kd, kl, vs, None, None, kv)
    zero = jnp.zeros((LANES, LANES), F32)
    states = _hg_steps(0, n_cc, n_cc, HG_PITCH_C, qd, dec, kv, (zero, zero), None)

    l_refs = {"q": q_ref, "v": v_ref, "zf": zf_ref, "zb": zb_ref}

    def load_l(name, p):
        return l_refs[name][0, p * n_col:(p + 1) * n_col, :]

    _hg_prepass(load_l, n_col, HG_PITCH_L, lb, qd, kd, kl, vs, dec)

    def intra_body(i, carry):
        for u in range(HG_INTRA_UNROLL):
            c0 = (i * HG_INTRA_UNROLL + u) * HG_GROUP
            _hg_intra(0, c0, HG_PITCH_L, qd, kd, kl, vs, masks[0], oi, kv)
            _hg_intra(1, c0, HG_PITCH_L, qd, kd, kl, vs, masks[1], oi, kv)
        return carry

    lax.fori_loop(0, n_col // (HG_GROUP * HG_INTRA_UNROLL), intra_body, 0)

    def step_body(i, carry):
        return _hg_steps(i * HG_UNROLL, HG_UNROLL, n_col, HG_PITCH_L, qd, dec, kv, carry, ox)

    lax.fori_loop(0, n_col // HG_UNROLL, step_body, states)

    gain = gain_ref[...]
    for r in range(HG_CHUNK):
        rows = slice(r * HG_PITCH_L, r * HG_PITCH_L + n_col)
        o = (oi[0, rows, :] + oi[1, rows, :]) + (ox[0, rows, :] + ox[1, rows, :])
        y = o * lax.rsqrt(jnp.mean(o * o, axis=-1, keepdims=True) + EPS) * gain
        o_ref[0, r * n_col:(r + 1) * n_col, :] = y * _silu(g_ref[0, r * n_col:(r + 1) * n_col, :])


def _hgrn(projx, projc, lbl, gain, col0):
    b, s_len, _ = projx.shape
    c_len = projc.shape[1]
    hw = gain.shape[1]
    nh = hw // LANES
    n_slots = lbl.shape[1]

    def xs(k):
        return pl.BlockSpec((1, s_len, LANES), lambda bi, h, k=k: (bi, 0, col0 + k * nh + h))

    def cs(k):
        return pl.BlockSpec((1, c_len, LANES), lambda bi, h, k=k: (bi, 0, col0 + k * nh + h))

    sc_rows = HG_CHUNK * HG_PITCH_L
    return pl.pallas_call(
        _hgrn_kernel,
        out_shape=jax.ShapeDtypeStruct((b, s_len, hw), F32),
        grid=(b, nh),
        in_specs=[xs(0), xs(1), xs(2), xs(3), xs(4), cs(0), cs(1), cs(2), cs(3),
                  pl.BlockSpec((2, n_slots, LANES), lambda bi, h: (0, 0, h)),
                  pl.BlockSpec((1, LANES), lambda bi, h: (0, h))],
        out_specs=pl.BlockSpec((1, s_len, LANES), lambda bi, h: (bi, 0, h)),
        scratch_shapes=[pltpu.VMEM((2, sc_rows, LANES), F32),
                        pltpu.VMEM((2, sc_rows, LANES), F32),
                        pltpu.VMEM((2, sc_rows, LANES), F32),
                        pltpu.VMEM((sc_rows, LANES), F32),
                        pltpu.VMEM((2, HG_PITCH_L, LANES), F32),
                        pltpu.VMEM((2, sc_rows, LANES), F32),
                        pltpu.VMEM((2, sc_rows, LANES), F32),
                        pltpu.VMEM((2, GRID_W, LANES, LANES), F32)],
        compiler_params=_cparams(("arbitrary", "arbitrary")),
        name="hgrn",
    )(projx, projx, projx, projx, projx, projc, projc, projc, projc, lbl, gain)


def _first_index_of_max(vals, iota, big):
    m = jnp.max(vals, axis=0, keepdims=True)
    idx = jnp.min(jnp.where(vals == m, iota, big), axis=0, keepdims=True)
    return m, idx


def _route(logits_t, rb, n_experts):
    t = logits_t.shape[1]
    gsz = n_experts // N_GROUPS
    neg = -jnp.inf
    scores = _sigmoid(logits_t)
    biased = scores + rb
    iota_g = lax.broadcasted_iota(I32, (gsz, t), 0)
    gscore = []
    for gi in range(N_GROUPS):
        blk = biased[gi * gsz:(gi + 1) * gsz, :]
        m1, i1 = _first_index_of_max(blk, iota_g, gsz)
        m2 = jnp.max(jnp.where(iota_g == i1, neg, blk), axis=0, keepdims=True)
        gscore.append(m1 + m2)
    gs = jnp.concatenate(gscore, axis=0)
    iota_n = lax.broadcasted_iota(I32, (N_GROUPS, t), 0)
    gsel = jnp.zeros((N_GROUPS, t), jnp.bool_)
    for _ in range(TOPK_GROUPS):
        _, gi1 = _first_index_of_max(gs, iota_n, N_GROUPS)
        hit = iota_n == gi1
        gsel = jnp.logical_or(gsel, hit)
        gs = jnp.where(hit, neg, gs)
    emask = jnp.concatenate(
        [jnp.broadcast_to(gsel[gi:gi + 1, :], (gsz, t)) for gi in range(N_GROUPS)], axis=0)
    masked = jnp.where(emask, biased, neg)
    iota_e = lax.broadcasted_iota(I32, (n_experts, t), 0)
    ids, gates = [], []
    sel = jnp.zeros((n_experts, t), F32)
    for _ in range(TOP_K):
        _, ei = _first_index_of_max(masked, iota_e, n_experts)
        hit = iota_e == ei
        ids.append(ei)
        gates.append(jnp.sum(jnp.where(hit, scores, 0.0), axis=0, keepdims=True))
        sel = jnp.where(hit, 1.0, sel)
        masked = jnp.where(hit, neg, masked)
    ids = jnp.concatenate(ids, axis=0)
    gates = jnp.concatenate(gates, axis=0)
    gates = gates / jnp.sum(gates, axis=0, keepdims=True) * ROUTE_SCALE
    return ids, gates, sel


def _outproj_kernel(yl_ref, yh_ref, x_ref, mod_ref, gain_ref, wo_ref, rwt_ref, rb_ref,
                    x1_ref, h2_ref, eid_ref, gate_ref, rank_ref, cnt_ref, carry):
    first = jnp.logical_and(pl.program_id(0) == 0, pl.program_id(1) == 0)

    @pl.when(first)
    def _():
        carry[...] = jnp.zeros(carry.shape, F32)

    y = jnp.dot(yl_ref[0].astype(BF16), wo_ref[0], preferred_element_type=F32)
    y = y + jnp.dot(yh_ref[0].astype(BF16), wo_ref[1], preferred_element_type=F32)
    x1 = x_ref[0] + mod_ref[0, 2:3, :] * y
    x1_ref[0] = x1
    ms = jnp.mean(x1 * x1, axis=-1, keepdims=True)
    h2 = x1 * lax.rsqrt(ms + EPS) * gain_ref[...]
    h2 = h2 * (1.0 + mod_ref[0, 4:5, :]) + mod_ref[0, 3:4, :]
    h2_ref[...] = _pack_rows(h2)

    n_experts = rwt_ref.shape[0]
    t = h2.shape[0]
    logits_t = lax.dot_general(rwt_ref[...], h2, (((1,), (1,)), ((), ())),
                               preferred_element_type=F32,
                               precision=lax.Precision.HIGHEST)
    ids, gates, sel = _route(logits_t, rb_ref[...], n_experts)
    eid_ref[...] = ids
    gate_ref[...] = gates

    ti = lax.broadcasted_iota(I32, (t, t), 0)
    tj = lax.broadcasted_iota(I32, (t, t), 1)
    upper = (ti < tj).astype(BF16)
    selb = sel.astype(BF16)
    before = jnp.dot(selb, upper, preferred_element_type=F32) + carry[:, 0:1]
    iota_e = lax.broadcasted_iota(I32, (n_experts, t), 0)
    ranks = [jnp.sum(jnp.where(iota_e == ids[k:k + 1, :], before, 0.0), axis=0, keepdims=True)
             for k in range(TOP_K)]
    rank_ref[...] = jnp.concatenate(ranks, axis=0).astype(I32)
    total = carry[...] + jnp.dot(selb, jnp.ones((t, LANES), BF16), preferred_element_type=F32)
    carry[...] = total
    cnt_ref[...] = total.astype(I32)


def _outproj(ylru, yhg, x, mods, gain, wo, rwt, rb, tm):
    b, s, d = x.shape
    hw = ylru.shape[2]
    e = rwt.shape[0]
    n = b * s
    nt = s // tm
    tok = lambda bi, i: (0, bi * nt + i)
    return pl.pallas_call(
        _outproj_kernel,
        out_shape=(jax.ShapeDtypeStruct((b, s, d), F32),
                   jax.ShapeDtypeStruct((n, d // 2), U32),
                   jax.ShapeDtypeStruct((TOP_K, n), I32),
                   jax.ShapeDtypeStruct((TOP_K, n), F32),
                   jax.ShapeDtypeStruct((TOP_K, n), I32),
                   jax.ShapeDtypeStruct((e, LANES), I32)),
        grid=(b, nt),
        in_specs=[pl.BlockSpec((1, tm, hw), lambda bi, i: (bi, i, 0)),
                  pl.BlockSpec((1, tm, hw), lambda bi, i: (bi, i, 0)),
                  pl.BlockSpec((1, tm, d), lambda bi, i: (bi, i, 0)),
                  pl.BlockSpec((1, 8, d), lambda bi, i: (bi, 0, 0)),
                  pl.BlockSpec((1, d), lambda bi, i: (0, 0)),
                  pl.BlockSpec((2, hw, d), lambda bi, i: (0, 0, 0)),
                  pl.BlockSpec((e, d), lambda bi, i: (0, 0)),
                  pl.BlockSpec((e, 1), lambda bi, i: (0, 0))],
        out_specs=(pl.BlockSpec((1, tm, d), lambda bi, i: (bi, i, 0)),
                   pl.BlockSpec((tm, d // 2), lambda bi, i: (bi * nt + i, 0)),
                   pl.BlockSpec((TOP_K, tm), tok),
                   pl.BlockSpec((TOP_K, tm), tok),
                   pl.BlockSpec((TOP_K, tm), tok),
                   pl.BlockSpec((e, LANES), lambda bi, i: (0, 0))),
        scratch_shapes=[pltpu.VMEM((e, LANES), F32)],
        compiler_params=_cparams(("arbitrary", "arbitrary")),
        name="outproj",
    )(ylru, yhg, x, mods, gain, wo, rwt, rb)


def _pos_kernel(pstart, eid_ref, rank_ref, pos_ref):
    eid = eid_ref[...]

    def body(e, acc):
        return jnp.where(eid == e, pstart[e], acc)

    base = lax.fori_loop(0, pstart.shape[0], body, jnp.zeros(eid.shape, I32))
    pos_ref[...] = base + rank_ref[...]


def _pos(pstarts, eid_t, rank_t, tn):
    k, n = eid_t.shape
    spec = pl.BlockSpec((k, tn), lambda i, ps: (0, i))
    return pl.pallas_call(
        _pos_kernel,
        out_shape=jax.ShapeDtypeStruct((k, n), I32),
        grid_spec=pltpu.PrefetchScalarGridSpec(
            num_scalar_prefetch=1, grid=(n // tn,), in_specs=[spec, spec], out_specs=spec),
        compiler_params=_cparams(("arbitrary",)),
        name="pos",
    )(pstarts, eid_t, rank_t)


def _pad_fill(padstart, padlen, zbuf, xs_ref, sem, wait):
    nbits = zbuf.shape[0].bit_length() - 1
    low_bits = SUBLANES.bit_length() - 1

    def go(cp):
        if wait:
            cp.wait()
        else:
            cp.start()

    def body(e, carry):
        st = padstart[e]
        ln = padlen[e]
        end = st + ln
        for bit in range(nbits - 1, low_bits - 1, -1):
            size = 1 << bit
            back = ((ln >> (bit + 1)) << (bit + 1)) + size

            @pl.when((ln & size) != 0)
            def _():
                start = pl.multiple_of(end - back, SUBLANES)
                go(pltpu.make_async_copy(zbuf.at[pl.ds(0, size)],
                                         xs_ref.at[pl.ds(start, size)], sem))
        for j in range(SUBLANES - 1):
            @pl.when(j < (ln & (SUBLANES - 1)))
            def _():
                go(pltpu.make_async_copy(zbuf.at[pl.ds(0, 1)], xs_ref.at[pl.ds(st + j, 1)], sem))
        return carry

    lax.fori_loop(0, padstart.shape[0], body, 0)


def _padfill_kernel(padstart, padlen, xs_in, xs_ref, zbuf, sem):
    del xs_in
    zbuf[...] = jnp.zeros(zbuf.shape, zbuf.dtype)
    _pad_fill(padstart, padlen, zbuf, xs_ref, sem, wait=False)
    _pad_fill(padstart, padlen, zbuf, xs_ref, sem, wait=True)


def _padfill(xs, padstart, padlen, bm):
    return pl.pallas_call(
        _padfill_kernel,
        out_shape=jax.ShapeDtypeStruct(xs.shape, xs.dtype),
        grid_spec=pltpu.PrefetchScalarGridSpec(
            num_scalar_prefetch=2,
            grid=(1,),
            in_specs=[pl.BlockSpec(memory_space=pl.ANY)],
            out_specs=pl.BlockSpec(memory_space=pl.ANY),
            scratch_shapes=[pltpu.VMEM((bm, xs.shape[1]), xs.dtype), pltpu.SemaphoreType.DMA(())]),
        input_output_aliases={2: 0},
        compiler_params=_cparams(("arbitrary",)),
        name="padfill",
    )(padstart, padlen, xs)


SC_SCATTER_WINDOW = 128


def _sc_mesh():
    return plsc.VectorSubcoreMesh(core_axis_name="core", subcore_axis_name="subcore")


def _sc_scatter_rows(rows, pos, cap):
    n, dw = rows.shape
    top_k = pos.shape[0]
    mesh = _sc_mesh()
    n_workers = mesh.num_cores * mesh.num_subcores
    win = SC_SCATTER_WINDOW
    per_w = n // n_workers
    n_chunks = per_w // win
    assert per_w * n_workers == n and n_chunks * win == per_w
    pos_w = pos.reshape(top_k, n_workers, n_chunks, win).transpose(1, 2, 0, 3)
    pos_w = pos_w.reshape(n_workers, n_chunks * top_k, win)

    @functools.partial(pl.kernel, mesh=mesh,
                       out_type=jax.ShapeDtypeStruct((cap, dw), rows.dtype),
                       scratch_types=[pltpu.VMEM((n_chunks * top_k, win), I32),
                                      pltpu.VMEM((win, dw), rows.dtype),
                                      pltpu.SemaphoreType.DMA])
    def scatter(r_hbm, p_hbm, o_hbm, idx_v, rows_v, sem):
        wid = lax.axis_index("subcore") * mesh.num_cores + lax.axis_index("core")
        pltpu.sync_copy(p_hbm.at[wid], idx_v)

        @pl.loop(0, n_chunks)
        def _(c):
            pltpu.sync_copy(r_hbm.at[pl.ds(wid * per_w + c * win, win)], rows_v)
            copies = [pltpu.async_copy(rows_v, o_hbm.at[idx_v.at[c * top_k + k]], sem)
                      for k in range(top_k)]
            for cp in copies:
                cp.wait()

    return scatter(rows, pos_w)


GMM_SLOTS = 4
GMM_AHEAD = GMM_SLOTS - 1


def _gmm_rows(xw, w13b, w2b, de):
    xb = _unpack_rows(xw).astype(BF16)
    u = jnp.dot(xb, w13b[...], preferred_element_type=F32)
    hmid = _silu(u[:, :de]) * u[:, de:]
    return _pack_rows(jnp.dot(hmid.astype(BF16), w2b[...], preferred_element_type=F32))


def _gmm_kernel(blk0, nblk, w13_ref, w2_ref, xs_ref, ys_ref, w13b, w2b, xbuf, ybuf, isem, osem):
    e = pl.program_id(0)
    n_e = pl.num_programs(0)
    slots, bm = xbuf.shape[0], xbuf.shape[1]
    de = w2_ref.shape[1]
    nb = nblk[e]
    b0 = blk0[e]
    total = blk0[n_e - 1] + nblk[n_e - 1]

    def x_copy(g):
        rows = pl.ds(pl.multiple_of(g * bm, bm), bm)
        slot = g & (slots - 1)
        return pltpu.make_async_copy(xs_ref.at[rows], xbuf.at[slot], isem.at[slot])

    def y_copy(g):
        rows = pl.ds(pl.multiple_of(g * bm, bm), bm)
        slot = g & (slots - 1)
        return pltpu.make_async_copy(ybuf.at[slot], ys_ref.at[rows], osem.at[slot])

    @pl.when(e == 0)
    def _():
        for j in range(GMM_AHEAD):
            @pl.when(j < total)
            def _():
                x_copy(j).start()

    w13b[...] = w13_ref[0].astype(BF16)
    w2b[...] = w2_ref[0].astype(BF16)

    def body(b, carry):
        g = b0 + b
        slot = g & (slots - 1)
        x_copy(g).wait()

        @pl.when(g + GMM_AHEAD < total)
        def _():
            x_copy(g + GMM_AHEAD).start()

        @pl.when(g >= slots)
        def _():
            y_copy(g - slots).wait()

        ybuf[slot] = _gmm_rows(xbuf[slot], w13b, w2b, de)
        y_copy(g).start()
        return carry

    lax.fori_loop(0, nb, body, 0)

    @pl.when(e == n_e - 1)
    def _():
        for j in range(1, slots + 1):
            @pl.when(total - j >= 0)
            def _():
                y_copy(total - j).wait()


def _gmm(xs, w13, w2, blk0, nblk, bm):
    cap, dw = xs.shape
    e, d, de2 = w13.shape
    de = w2.shape[1]
    return pl.pallas_call(
        _gmm_kernel,
        out_shape=jax.ShapeDtypeStruct((cap, dw), U32),
        grid_spec=pltpu.PrefetchScalarGridSpec(
            num_scalar_prefetch=2,
            grid=(e,),
            in_specs=[pl.BlockSpec((1, d, de2), lambda i, b0, nb: (i, 0, 0)),
                      pl.BlockSpec((1, de, d), lambda i, b0, nb: (i, 0, 0)),
                      pl.BlockSpec(memory_space=pl.ANY)],
            out_specs=pl.BlockSpec(memory_space=pl.ANY),
            scratch_shapes=[pltpu.VMEM((d, de2), BF16), pltpu.VMEM((de, d), BF16),
                            pltpu.VMEM((GMM_SLOTS, bm, dw), U32),
                            pltpu.VMEM((GMM_SLOTS, bm, dw), U32),
                            pltpu.SemaphoreType.DMA((GMM_SLOTS,)),
                            pltpu.SemaphoreType.DMA((GMM_SLOTS,))]),
        compiler_params=_cparams(("arbitrary",)),
        name="gmm",
    )(blk0, nblk, w13, w2, xs)


def _sc_gather_rows(table, idx, window):
    n_idx = idx.shape[0]
    dw = table.shape[1]
    mesh = _sc_mesh()
    n_workers = mesh.num_cores * mesh.num_subcores
    per_w = n_idx // n_workers
    n_chunks = per_w // window
    assert per_w * n_workers == n_idx and n_chunks * window == per_w and n_chunks % 2 == 0

    @functools.partial(pl.kernel, mesh=mesh,
                       out_type=jax.ShapeDtypeStruct((n_idx, dw), table.dtype),
                       scratch_types=[pltpu.VMEM((per_w,), I32),
                                      pltpu.VMEM((window, dw), table.dtype),
                                      pltpu.VMEM((window, dw), table.dtype),
                                      pltpu.SemaphoreType.DMA, pltpu.SemaphoreType.DMA,
                                      pltpu.SemaphoreType.DMA, pltpu.SemaphoreType.DMA])
    def gather(x_hbm, i_hbm, o_hbm, idx_v, rows0, rows1, gs0, gs1, os0, os1):
        wid = lax.axis_index("subcore") * mesh.num_cores + lax.axis_index("core")
        base = wid * per_w
        pltpu.sync_copy(i_hbm.at[pl.ds(base, per_w)], idx_v)
        bufs = ((rows0, gs0, os0), (rows1, gs1, os1))

        def fetch(j, slot):
            rows, gs, _ = bufs[slot]
            off = pl.multiple_of(j * window, window)
            return pltpu.async_copy(x_hbm.at[idx_v.at[pl.ds(off, window)]], rows, gs)

        def fetch_wait(slot):
            rows, gs, _ = bufs[slot]
            pltpu.make_async_copy(x_hbm.at[idx_v.at[pl.ds(0, window)]], rows, gs).wait()

        def put(j, slot):
            rows, _, osem = bufs[slot]
            off = pl.multiple_of(j * window, window)
            return pltpu.async_copy(rows, o_hbm.at[pl.ds(base + off, window)], osem)

        def put_wait(slot):
            rows, _, osem = bufs[slot]
            pltpu.make_async_copy(rows, o_hbm.at[pl.ds(base, window)], osem).wait()

        fetch(0, 0)

        @pl.loop(0, n_chunks, step=2)
        def _(j):
            @pl.when(j > 0)
            def _():
                put_wait(1)
            fetch(j + 1, 1)
            fetch_wait(0)
            put(j, 0)
            put_wait(0)

            @pl.when(j + 2 < n_chunks)
            def _():
                fetch(j + 2, 0)
            fetch_wait(1)
            put(j + 1, 1)

        put_wait(1)

    return gather(table, idx)


def _combine_kernel(gate_ref, x1_ref, h2_ref, mod_ref, gain_ref, sw13_ref, sw2_ref, yg_ref, o_ref):
    ds_ = sw2_ref.shape[0]
    hb = _unpack_rows(h2_ref[...]).astype(BF16)
    u = jnp.dot(hb, sw13_ref[...], preferred_element_type=F32)
    hmid = _silu(u[:, :ds_]) * u[:, ds_:]
    y = jnp.dot(hmid.astype(BF16), sw2_ref[...], preferred_element_type=F32)

    gate = gate_ref[...]
    moe = gate[:, 0:1] * _unpack_rows(yg_ref[0])
    for k in range(1, TOP_K):
        moe = moe + gate[:, k:k + 1] * _unpack_rows(yg_ref[k])
    xo = x1_ref[...] + mod_ref[0, 5:6, :] * (moe + y)
    ms = jnp.mean(xo * xo, axis=-1, keepdims=True)
    o_ref[...] = xo * lax.rsqrt(ms + EPS) * gain_ref[...]


def _combine(gate_tok, x1, h2, mods, gain, sw13, sw2, yg, tc, tiles_per_batch):
    n, d = x1.shape
    dw = h2.shape[1]
    nt = n // tc
    ds2 = sw13.shape[1]
    ds_ = sw2.shape[0]
    return pl.pallas_call(
        _combine_kernel,
        out_shape=jax.ShapeDtypeStruct((n, d), F32),
        grid=(nt,),
        in_specs=[pl.BlockSpec((tc, TOP_K), lambda i: (i, 0)),
                  pl.BlockSpec((tc, d), lambda i: (i, 0)),
                  pl.BlockSpec((tc, dw), lambda i: (i, 0)),
                  pl.BlockSpec((1, 8, d), lambda i: (i // tiles_per_batch, 0, 0)),
                  pl.BlockSpec((1, d), lambda i: (0, 0)),
                  pl.BlockSpec((d, ds2), lambda i: (0, 0)),
                  pl.BlockSpec((ds_, d), lambda i: (0, 0)),
                  pl.BlockSpec((TOP_K, tc, dw), lambda i: (0, i, 0))],
        out_specs=pl.BlockSpec((tc, d), lambda i: (i, 0)),
        compiler_params=_cparams(("arbitrary",)),
        name="combine",
    )(gate_tok, x1, h2, mods, gain, sw13, sw2, yg)


def _block_diag_pairs(w, per):
    nb, bd, _ = w.shape
    w = w.reshape(nb // per, per, bd, bd)
    eye = jnp.eye(per, dtype=w.dtype)
    out = jnp.einsum("gpij,pq->gpiqj", w, eye)
    return out.reshape(nb // per, per * bd, per * bd)


def kernel(x, c, ctx, c_ctx, ada_w, ada_b, norm_mix, norm_ffn, norm_final, w_in, w_out,
           lru_conv_w, lru_conv_b, lru_wa, lru_ba, lru_wx, lru_bx, lru_lambda,
           hgrn_lb_logits, hgrn_norm, router_w, router_b, exp_w13, exp_w2, shared_w13, shared_w2):
    assert ada_w.shape[0] == 1, "single-layer block"
    b, s, d = x.shape
    n = b * s
    lru_w = lru_conv_w.shape[2]
    hg_w = hgrn_norm.shape[1]
    n_experts = router_w.shape[2]

    rows = -(-(b + 1) // SUBLANES) * SUBLANES
    cs = jnp.zeros((rows, d), F32).at[:b].set(c).at[b].set(c_ctx)
    mod = _ada(cs, ada_w[0], ada_b[0][None, :]).reshape(rows, 6, d)
    mods = jnp.pad(mod, ((0, 0), (0, 2), (0, 0)))

    w_in_bf = w_in[0].astype(BF16)
    gain1 = norm_mix[0][None, :]
    projx = _inproj(x, mods, gain1, w_in_bf, 512, shared_mod=False)
    projc = _inproj(ctx, mods[b:b + 1], gain1, w_in_bf, ctx.shape[1], shared_mod=True)

    per = LRU_G // (lru_w // LRU_BLOCKS)
    wg = jnp.stack([_block_diag_pairs(lru_wa[0, 0], per), _block_diag_pairs(lru_wx[0, 0], per),
                    _block_diag_pairs(lru_wa[0, 1], per), _block_diag_pairs(lru_wx[0, 1], per)]
                   ).astype(BF16)
    bg = jnp.stack([lru_ba[0, 0], lru_bx[0, 0], lru_ba[0, 1], lru_bx[0, 1]])
    ylru = _lru(projx, projc, lru_conv_w[0], lru_conv_b[0][None, :], wg, bg, lru_lambda[0])

    yhg = _hgrn(projx, projc, hgrn_lb_logits, hgrn_norm[0][None, :], (2 * lru_w) // LANES)

    wo = w_out[0].astype(BF16).reshape(2, lru_w, d)
    x1, h2, eid_t, gate_t, rank_t, cnt = _outproj(
        ylru, yhg, x, mods, norm_ffn[0][None, :], wo,
        router_w[0].T, router_b[0][:, None], 512)

    bm = 256
    counts = cnt[:, 0]
    padded = (counts + bm - 1) // bm * bm
    pends = jnp.cumsum(padded)
    pstarts = pends - padded
    pos = _pos(pstarts, eid_t, rank_t, 2048)
    cap = n * TOP_K + n_experts * bm

    xs = _padfill(_sc_scatter_rows(h2, pos, cap), pstarts + counts, padded - counts, bm)
    ys = _gmm(xs, exp_w13[0], exp_w2[0], pstarts // bm, padded // bm, bm)

    yg = _sc_gather_rows(ys, pos.reshape(-1), 64).reshape(TOP_K, n, d // 2)
    tc = 256
    out = _combine(gate_t.T, x1.reshape(n, d), h2, mods,
                   norm_final[None, :], shared_w13[0].astype(BF16), shared_w2[0].astype(BF16),
                   yg, tc, s // tc)
    return out.reshape(b, s, d)
```

```python
import functools

import jax
import jax.numpy as jnp
from jax import lax
from jax.experimental import pallas as pl
from jax.experimental.pallas import tpu as pltpu
from jax.experimental.pallas import tpu_sc as plsc

F32 = jnp.float32
BF16 = jnp.bfloat16
I32 = jnp.int32
U32 = jnp.uint32

EPS = 1e-6
LRU_C = 8.0
ROUTE_SCALE = 2.5
GRID_W = 64
HG_CHUNK = 32
N_GROUPS = 8
TOPK_GROUPS = 4
TOP_K = 8
LRU_BLOCKS = 8
HG_HEADS = 4
CONV_W = 4
CONV_PAD_L = 1

SUBLANES = 8
LANES = 128
N_SEG = SUBLANES
VMEM_LIMIT = 56 * 1024 * 1024


def _cparams(sem, vmem=VMEM_LIMIT):
    return pltpu.CompilerParams(dimension_semantics=sem, vmem_limit_bytes=vmem)


def _sigmoid(x):
    return jax.nn.sigmoid(x)


def _sigmoid_tanh(x):
    return 0.5 * jnp.tanh(0.5 * x) + 0.5


def _silu(x):
    return x * _sigmoid(x)


def _pack_rows(x):
    w = x.shape[1] // 2
    bits = pltpu.bitcast(x.astype(BF16).astype(F32), U32)
    return (bits[:, :w] >> 16) | (bits[:, w:] & jnp.uint32(0xFFFF0000))


def _unpack_rows(p):
    lo = pltpu.bitcast(p << 16, F32)
    hi = pltpu.bitcast(p & jnp.uint32(0xFFFF0000), F32)
    return jnp.concatenate([lo, hi], axis=1)


def _gelu_tanh(x):
    c = 0.7978845608028654
    return 0.5 * x * (1.0 + jnp.tanh(c * (x + 0.044715 * (x * x * x))))


def _ada_kernel(c_ref, w_ref, b_ref, o_ref):
    s = _silu(c_ref[...])
    o_ref[...] = jnp.dot(s, w_ref[...], preferred_element_type=F32,
                         precision=lax.Precision.HIGHEST) + b_ref[...]


def _ada(cs, w, b):
    rows, d = cs.shape
    n = w.shape[1]
    bn = 1024
    return pl.pallas_call(
        _ada_kernel,
        out_shape=jax.ShapeDtypeStruct((rows, n), F32),
        grid=(n // bn,),
        in_specs=[pl.BlockSpec((rows, d), lambda j: (0, 0)),
                  pl.BlockSpec((d, bn), lambda j: (0, j)),
                  pl.BlockSpec((1, bn), lambda j: (0, j))],
        out_specs=pl.BlockSpec((rows, bn), lambda j: (0, j)),
        compiler_params=_cparams(("arbitrary",)),
        name="ada",
    )(cs, w, b)


def _inproj_kernel(x_ref, mod_ref, gain_ref, w_ref, o_ref):
    x = x_ref[0]
    ms = jnp.mean(x * x, axis=-1, keepdims=True)
    y = x * lax.rsqrt(ms + EPS) * gain_ref[...]
    h = y * (1.0 + mod_ref[0, 1:2, :]) + mod_ref[0, 0:1, :]
    o_ref[0] = jnp.dot(h.astype(BF16), w_ref[...], preferred_element_type=F32)


def _inproj(x, mods, gain, w_bf, tm, shared_mod):
    b, s, d = x.shape
    n = w_bf.shape[1]
    mod_map = (lambda bi, i: (0, 0, 0)) if shared_mod else (lambda bi, i: (bi, 0, 0))
    return pl.pallas_call(
        _inproj_kernel,
        out_shape=jax.ShapeDtypeStruct((b, s, n), F32),
        grid=(b, s // tm),
        in_specs=[pl.BlockSpec((1, tm, d), lambda bi, i: (bi, i, 0)),
                  pl.BlockSpec((1, 8, d), mod_map),
                  pl.BlockSpec((1, d), lambda bi, i: (0, 0)),
                  pl.BlockSpec((d, n), lambda bi, i: (0, 0))],
        out_specs=pl.BlockSpec((1, tm, n), lambda bi, i: (bi, i, 0)),
        compiler_params=_cparams(("arbitrary", "arbitrary")),
        name="inproj",
    )(x, mods, gain, w_bf)


LRU_G = 256
LRU_CHUNK = 256


def _seg_pitch(seg_len):
    return seg_len + SUBLANES


def _lane_store(ref, d, rows, val):
    nl = val.shape[-1] // LANES
    for l in range(nl):
        ref[d * nl + l, rows, :] = val[:, l * LANES:(l + 1) * LANES]


def _lane_load(ref, d, rows, nl):
    return jnp.concatenate([ref[d * nl + l, rows, :] for l in range(nl)], axis=-1)


def _lru_coeffs(pad_ref, t0, rows, cw, cb, wg_ref, bg, sp, a_ref, b_ref, dst0):
    win = pad_ref[pl.ds(t0, rows + 2 * SUBLANES), :]
    u = cb
    for k in range(CONV_W):
        off = SUBLANES - CONV_PAD_L + k
        u = u + win[off:off + rows, :] * cw[k:k + 1, :]
    ub = u.astype(BF16)
    for d in range(2):
        r = _sigmoid_tanh(jnp.dot(ub, wg_ref[2 * d, 0], preferred_element_type=F32)
                          + bg[2 * d:2 * d + 1, :])
        ig = _sigmoid_tanh(jnp.dot(ub, wg_ref[2 * d + 1, 0], preferred_element_type=F32)
                           + bg[2 * d + 1:2 * d + 2, :])
        log_a = (-LRU_C) * r * sp[d:d + 1, :]
        a = jnp.exp(log_a)
        one_minus_a2 = -jnp.tanh(log_a) * (1.0 + a * a)
        bb = jnp.sqrt(one_minus_a2) * (ig * u)
        _lane_store(a_ref, d, pl.ds(dst0, rows), a)
        _lane_store(b_ref, d, pl.ds(dst0, rows), bb)


def _seg_scan(a_ref, b_ref, seg_len, pitch, unroll):
    n_lead = a_ref.shape[0]
    nl = n_lead // 2
    zero = jnp.zeros((N_SEG, LANES), F32)
    one = jnp.ones((N_SEG, LANES), F32)
    init = tuple((zero, one) for _ in range(n_lead))

    def step(t, carry):
        out = []
        for i in range(n_lead):
            h, p = carry[i]
            tt = t if i < nl else seg_len - 1 - t
            rows = pl.ds(tt, N_SEG, stride=pitch)
            a = a_ref[i, rows, :]
            h = a * h + b_ref[i, rows, :]
            p = p * a
            b_ref[i, rows, :] = h
            a_ref[i, rows, :] = p
            out.append((h, p))
        return tuple(out)

    def body(i, carry):
        for j in range(unroll):
            carry = step(i * unroll + j, carry)
        return carry

    ends = lax.fori_loop(0, seg_len // unroll, body, init)
    h_end = [jnp.concatenate([ends[d * nl + l][0] for l in range(nl)], axis=-1) for d in range(2)]
    p_end = [jnp.concatenate([ends[d * nl + l][1] for l in range(nl)], axis=-1) for d in range(2)]
    return h_end, p_end


def _seg_carries(h_end, p_end, h0, reverse):
    order = range(N_SEG - 1, -1, -1) if reverse else range(N_SEG)
    cin = [None] * N_SEG
    c = h0
    for s in order:
        cin[s] = c
        c = p_end[s:s + 1, :] * c + h_end[s:s + 1, :]
    return cin, c


def _lru_kernel(rx_ref, rg_ref, rc_ref, cw_ref, cb_ref, wg_ref, bg_ref, lam_ref, o_ref,
                padl, padc, a_l, b_l, a_c, b_c):
    s_len = rx_ref.shape[1]
    c_len = rc_ref.shape[1]
    g = rx_ref.shape[2]
    nl = g // LANES
    seg_l = s_len // N_SEG
    seg_c = c_len // N_SEG
    pitch_l = _seg_pitch(seg_l)
    pitch_c = _seg_pitch(seg_c)

    zeros = jnp.zeros((SUBLANES, g), F32)
    padl[0:SUBLANES, :] = zeros
    padl[SUBLANES + s_len:2 * SUBLANES + s_len, :] = zeros
    padc[0:SUBLANES, :] = zeros
    padc[SUBLANES + c_len:2 * SUBLANES + c_len, :] = zeros
    for i in range(s_len // LRU_CHUNK):
        padl[SUBLANES + i * LRU_CHUNK:SUBLANES + (i + 1) * LRU_CHUNK, :] = (
            rx_ref[0, i * LRU_CHUNK:(i + 1) * LRU_CHUNK, :])
    padc[SUBLANES:SUBLANES + c_len, :] = rc_ref[0]

    cw = cw_ref[...]
    cb = cb_ref[...]
    bg = bg_ref[...]
    x = -lam_ref[...]
    sp = jnp.maximum(x, 0.0) + jnp.log(1.0 + jnp.exp(-jnp.abs(x)))

    _lru_coeffs(padc, 0, c_len, cw, cb, wg_ref, bg, sp, a_l, b_l, 0)
    for i in range(2 * nl):
        for s in range(N_SEG):
            a_c[i, s * pitch_c:s * pitch_c + seg_c, :] = a_l[i, s * seg_c:(s + 1) * seg_c, :]
            b_c[i, s * pitch_c:s * pitch_c + seg_c, :] = b_l[i, s * seg_c:(s + 1) * seg_c, :]
    h_end, p_end = _seg_scan(a_c, b_c, seg_c, pitch_c, unroll=4)
    zero_row = jnp.zeros((1, g), F32)
    _, h0_f = _seg_carries(h_end[0], p_end[0], zero_row, reverse=False)
    _, h0_b = _seg_carries(h_end[1], p_end[1], zero_row, reverse=True)

    def coeff_body(s, carry):
        t0 = pl.multiple_of(s * seg_l, SUBLANES)
        dst = pl.multiple_of(s * pitch_l, SUBLANES)
        _lru_coeffs(padl, t0, seg_l, cw, cb, wg_ref, bg, sp, a_l, b_l, dst)
        return carry

    lax.fori_loop(0, N_SEG, coeff_body, 0)
    h_end, p_end = _seg_scan(a_l, b_l, seg_l, pitch_l, unroll=4)
    cin_f, _ = _seg_carries(h_end[0], p_end[0], h0_f, reverse=False)
    cin_b, _ = _seg_carries(h_end[1], p_end[1], h0_b, reverse=True)

    for s in range(N_SEG):
        rows = slice(s * pitch_l, s * pitch_l + seg_l)
        h = (_lane_load(b_l, 0, rows, nl) + _lane_load(a_l, 0, rows, nl) * cin_f[s]) + (
            _lane_load(b_l, 1, rows, nl) + _lane_load(a_l, 1, rows, nl) * cin_b[s])
        o_ref[0, s * seg_l:(s + 1) * seg_l, :] = _gelu_tanh(rg_ref[0, s * seg_l:(s + 1) * seg_l, :]) * h


def _lru(projx, projc, cw, cb, wg, bg, lam):
    b, s_len, _ = projx.shape
    c_len = projc.shape[1]
    w = cw.shape[1]
    g = LRU_G
    ng = w // g
    seg_l = s_len // N_SEG
    seg_c = c_len // N_SEG
    assert seg_l == LRU_CHUNK and c_len % (N_SEG * 1) == 0
    return pl.pallas_call(
        _lru_kernel,
        out_shape=jax.ShapeDtypeStruct((b, s_len, w), F32),
        grid=(b, ng),
        in_specs=[pl.BlockSpec((1, s_len, g), lambda bi, j: (bi, 0, j)),
                  pl.BlockSpec((1, s_len, g), lambda bi, j: (bi, 0, ng + j)),
                  pl.BlockSpec((1, c_len, g), lambda bi, j: (bi, 0, j)),
                  pl.BlockSpec((CONV_W, g), lambda bi, j: (0, j)),
                  pl.BlockSpec((1, g), lambda bi, j: (0, j)),
                  pl.BlockSpec((4, 1, g, g), lambda bi, j: (0, j, 0, 0)),
                  pl.BlockSpec((4, g), lambda bi, j: (0, j)),
                  pl.BlockSpec((2, g), lambda bi, j: (0, j))],
        out_specs=pl.BlockSpec((1, s_len, g), lambda bi, j: (bi, 0, j)),
        scratch_shapes=[pltpu.VMEM((s_len + 2 * SUBLANES, g), F32),
                        pltpu.VMEM((c_len + 2 * SUBLANES, g), F32),
                        pltpu.VMEM((2 * g // LANES, N_SEG * _seg_pitch(seg_l), LANES), F32),
                        pltpu.VMEM((2 * g // LANES, N_SEG * _seg_pitch(seg_l), LANES), F32),
                        pltpu.VMEM((2 * g // LANES, N_SEG * _seg_pitch(seg_c), LANES), F32),
                        pltpu.VMEM((2 * g // LANES, N_SEG * _seg_pitch(seg_c), LANES), F32)],
        compiler_params=_cparams(("arbitrary", "arbitrary")),
        name="lru",
    )(projx, projx, projc, cw, cb, wg, bg, lam)


HG_PITCH_L = GRID_W + SUBLANES
HG_PITCH_C = SUBLANES


def _hg_gates(z, lb):
    e = jnp.exp(-jnp.abs(z))
    s = 1.0 / (1.0 + e)
    es = e * s
    pos = z >= 0.0
    sig = jnp.where(pos, s, es)
    nsig = jnp.where(pos, es, s)
    logf = jnp.log(lb + (1.0 - lb) * sig)
    k = (1.0 - lb) * nsig
    return logf, k


def _hg_prepass(load, width, pitch, lb, qd, kd, kl, vs, dec):
    n_pos = HG_CHUNK
    for d, zname in enumerate(("zf", "zb")):
        order = range(n_pos) if d == 0 else range(n_pos - 1, -1, -1)
        g = jnp.zeros((width, LANES), F32)
        lbd = lb[d:d + 1, :]
        for p in order:
            logf, k = _hg_gates(load(zname, p), lbd)
            g = g + logf
            qd[d, p * pitch:p * pitch + width, :] = g
            kl[d, p * pitch:p * pitch + width, :] = k
        g_last = g
        dec[d, 0:width, :] = jnp.exp(g_last)
        for p in range(n_pos):
            rows = slice(p * pitch, p * pitch + width)
            gp = qd[d, rows, :]
            k = kl[d, rows, :]
            qd[d, rows, :] = load("q", p) * jnp.exp(gp)
            kd[d, rows, :] = k * jnp.exp(-gp)
            kl[d, rows, :] = k * jnp.exp(g_last - gp)
    for p in range(n_pos):
        vs[p * pitch:p * pitch + width, :] = load("v", p)


HG_GROUP = 4
HG_UNROLL = 4
HG_INTRA_UNROLL = 4
_NT = (((1,), (1,)), ((), ()))
_TN = (((0,), (0,)), ((), ()))


def _hg_group_rows(ref, lead, c0, pitch):
    parts = []
    for j in range(HG_GROUP):
        rows = pl.ds(c0 + j, HG_CHUNK, stride=pitch)
        parts.append(ref[rows, :] if lead is None else ref[lead, rows, :])
    return jnp.concatenate(parts, axis=0)


def _hg_intra(d, c0, pitch, qd, kd, kl, vs, mask, oi, kv):
    kls = _hg_group_rows(kl, d, c0, pitch).astype(BF16)
    v = _hg_group_rows(vs, None, c0, pitch).astype(BF16)
    if oi is not None:
        q = _hg_group_rows(qd, d, c0, pitch).astype(BF16)
        k = _hg_group_rows(kd, d, c0, pitch).astype(BF16)
        sc = lax.dot_general(q, k, _NT, preferred_element_type=F32)
        sc = jnp.where(mask, sc, 0.0)
        o = jnp.dot(sc.astype(BF16), v, preferred_element_type=F32)
        for j in range(HG_GROUP):
            oi[d, pl.ds(c0 + j, HG_CHUNK, stride=pitch), :] = o[j * HG_CHUNK:(j + 1) * HG_CHUNK, :]
    for j in range(HG_GROUP):
        sl = slice(j * HG_CHUNK, (j + 1) * HG_CHUNK)
        kv[d, c0 + j] = lax.dot_general(v[sl], kls[sl], _TN, preferred_element_type=F32)


def _hg_steps(step0, n_steps, n_chunks, pitch, qd, dec, kv, states, ox):
    states = list(states)
    for u in range(n_steps):
        for d in range(2):
            c = step0 + u if d == 0 else n_chunks - 1 - (step0 + u)
            s_t = states[d]
            if ox is not None:
                rows = pl.ds(c, HG_CHUNK, stride=pitch)
                q = qd[d, rows, :].astype(BF16)
                ox[d, rows, :] = lax.dot_general(q, s_t.astype(BF16), _NT,
                                                 preferred_element_type=F32)
            states[d] = s_t * dec[d, pl.ds(c, 1), :] + kv[d, c]
    return tuple(states)


def _hgrn_kernel(q_ref, v_ref, zf_ref, zb_ref, g_ref, qc_ref, vc_ref, zfc_ref, zbc_ref,
                 lbl_ref, gain_ref, o_ref, qd, kd, kl, vs, dec, oi, ox, kv):
    s_len = q_ref.shape[1]
    c_len = qc_ref.shape[1]
    n_col = s_len // HG_CHUNK
    n_cc = c_len // HG_CHUNK
    assert n_col == GRID_W and n_cc == HG_PITCH_C
    assert n_col % HG_GROUP == 0 and n_cc % HG_GROUP == 0 and n_col % HG_UNROLL == 0

    lg = lbl_ref[...]
    m = jnp.max(lg, axis=1, keepdims=True)
    ex = jnp.exp(lg - m)
    lb = ex[:, 0, :] / jnp.sum(ex, axis=1)

    gr = HG_GROUP * HG_CHUNK
    ii = lax.broadcasted_iota(I32, (gr, gr), 0)
    jj = lax.broadcasted_iota(I32, (gr, gr), 1)
    same = (ii // HG_CHUNK) == (jj // HG_CHUNK)
    masks = (jnp.logical_and(same, jj <= ii), jnp.logical_and(same, jj >= ii))

    c_refs = {"q": qc_ref, "v": vc_ref, "zf": zfc_ref, "zb": zbc_ref}

    def load_c(name, p):
        return c_refs[name][0, pl.ds(p, n_cc, stride=HG_CHUNK), :]

    _hg_prepass(load_c, n_cc, HG_PITCH_C, lb, qd, kd, kl, vs, dec)
    for d in range(2):
        for c0 in range(0, n_cc, HG_GROUP):
            _hg_intra(d, c0, HG_PITCH_C, qd, kd, kl, vs, None, None, kv)
    zero = jnp.zeros((LANES, LANES), F32)
    states = _hg_steps(0, n_cc, n_cc, HG_PITCH_C, qd, dec, kv, (zero, zero), None)

    l_refs = {"q": q_ref, "v": v_ref, "zf": zf_ref, "zb": zb_ref}

    def load_l(name, p):
        return l_refs[name][0, p * n_col:(p + 1) * n_col, :]

    _hg_prepass(load_l, n_col, HG_PITCH_L, lb, qd, kd, kl, vs, dec)

    def intra_body(i, carry):
        for u in range(HG_INTRA_UNROLL):
            c0 = (i * HG_INTRA_UNROLL + u) * HG_GROUP
            _hg_intra(0, c0, HG_PITCH_L, qd, kd, kl, vs, masks[0], oi, kv)
            _hg_intra(1, c0, HG_PITCH_L, qd, kd, kl, vs, masks[1], oi, kv)
        return carry

    lax.fori_loop(0, n_col // (HG_GROUP * HG_INTRA_UNROLL), intra_body, 0)

    def step_body(i, carry):
        return _hg_steps(i * HG_UNROLL, HG_UNROLL, n_col, HG_PITCH_L, qd, dec, kv, carry, ox)

    lax.fori_loop(0, n_col // HG_UNROLL, step_body, states)

    gain = gain_ref[...]
    for r in range(HG_CHUNK):
        rows = slice(r * HG_PITCH_L, r * HG_PITCH_L + n_col)
        o = (oi[0, rows, :] + oi[1, rows, :]) + (ox[0, rows, :] + ox[1, rows, :])
        y = o * lax.rsqrt(jnp.mean(o * o, axis=-1, keepdims=True) + EPS) * gain
        o_ref[0, r * n_col:(r + 1) * n_col, :] = y * _silu(g_ref[0, r * n_col:(r + 1) * n_col, :])


def _hgrn(projx, projc, lbl, gain, col0):
    b, s_len, _ = projx.shape
    c_len = projc.shape[1]
    hw = gain.shape[1]
    nh = hw // LANES
    n_slots = lbl.shape[1]

    def xs(k):
        return pl.BlockSpec((1, s_len, LANES), lambda bi, h, k=k: (bi, 0, col0 + k * nh + h))

    def cs(k):
        return pl.BlockSpec((1, c_len, LANES), lambda bi, h, k=k: (bi, 0, col0 + k * nh + h))

    sc_rows = HG_CHUNK * HG_PITCH_L
    return pl.pallas_call(
        _hgrn_kernel,
        out_shape=jax.ShapeDtypeStruct((b, s_len, hw), F32),
        grid=(b, nh),
        in_specs=[xs(0), xs(1), xs(2), xs(3), xs(4), cs(0), cs(1), cs(2), cs(3),
                  pl.BlockSpec((2, n_slots, LANES), lambda bi, h: (0, 0, h)),
                  pl.BlockSpec((1, LANES), lambda bi, h: (0, h))],
        out_specs=pl.BlockSpec((1, s_len, LANES), lambda bi, h: (bi, 0, h)),
        scratch_shapes=[pltpu.VMEM((2, sc_rows, LANES), F32),
                        pltpu.VMEM((2, sc_rows, LANES), F32),
                        pltpu.VMEM((2, sc_rows, LANES), F32),
                        pltpu.VMEM((sc_rows, LANES), F32),
                        pltpu.VMEM((2, HG_PITCH_L, LANES), F32),
                        pltpu.VMEM((2, sc_rows, LANES), F32),
                        pltpu.VMEM((2, sc_rows, LANES), F32),
                        pltpu.VMEM((2, GRID_W, LANES, LANES), F32)],
        compiler_params=_cparams(("arbitrary", "arbitrary")),
        name="hgrn",
    )(projx, projx, projx, projx, projx, projc, projc, projc, projc, lbl, gain)


def _first_index_of_max(vals, iota, big):
    m = jnp.max(vals, axis=0, keepdims=True)
    idx = jnp.min(jnp.where(vals == m, iota, big), axis=0, keepdims=True)
    return m, idx


def _route(logits_t, rb, n_experts):
    t = logits_t.shape[1]
    gsz = n_experts // N_GROUPS
    neg = -jnp.inf
    scores = _sigmoid(logits_t)
    biased = scores + rb
    iota_g = lax.broadcasted_iota(I32, (gsz, t), 0)
    gscore = []
    for gi in range(N_GROUPS):
        blk = biased[gi * gsz:(gi + 1) * gsz, :]
        m1, i1 = _first_index_of_max(blk, iota_g, gsz)
        m2 = jnp.max(jnp.where(iota_g == i1, neg, blk), axis=0, keepdims=True)
        gscore.append(m1 + m2)
    gs = jnp.concatenate(gscore, axis=0)
    iota_n = lax.broadcasted_iota(I32, (N_GROUPS, t), 0)
    gsel = jnp.zeros((N_GROUPS, t), jnp.bool_)
    for _ in range(TOPK_GROUPS):
        _, gi1 = _first_index_of_max(gs, iota_n, N_GROUPS)
        hit = iota_n == gi1
        gsel = jnp.logical_or(gsel, hit)
        gs = jnp.where(hit, neg, gs)
    emask = jnp.concatenate(
        [jnp.broadcast_to(gsel[gi:gi + 1, :], (gsz, t)) for gi in range(N_GROUPS)], axis=0)
    masked = jnp.where(emask, biased, neg)
    iota_e = lax.broadcasted_iota(I32, (n_experts, t), 0)
    ids, gates = [], []
    sel = jnp.zeros((n_experts, t), F32)
    for _ in range(TOP_K):
        _, ei = _first_index_of_max(masked, iota_e, n_experts)
        hit = iota_e == ei
        ids.append(ei)
        gates.append(jnp.sum(jnp.where(hit, scores, 0.0), axis=0, keepdims=True))
        sel = jnp.where(hit, 1.0, sel)
        masked = jnp.where(hit, neg, masked)
    ids = jnp.concatenate(ids, axis=0)
    gates = jnp.concatenate(gates, axis=0)
    gates = gates / jnp.sum(gates, axis=0, keepdims=True) * ROUTE_SCALE
    return ids, gates, sel


def _outproj_kernel(yl_ref, yh_ref, x_ref, mod_ref, gain_ref, wo_ref, rwt_ref, rb_ref,
                    x1_ref, h2_ref, eid_ref, gate_ref, rank_ref, cnt_ref, carry):
    first = jnp.logical_and(pl.program_id(0) == 0, pl.program_id(1) == 0)

    @pl.when(first)
    def _():
        carry[...] = jnp.zeros(carry.shape, F32)

    y = jnp.dot(yl_ref[0].astype(BF16), wo_ref[0], preferred_element_type=F32)
    y = y + jnp.dot(yh_ref[0].astype(BF16), wo_ref[1], preferred_element_type=F32)
    x1 = x_ref[0] + mod_ref[0, 2:3, :] * y
    x1_ref[0] = x1
    ms = jnp.mean(x1 * x1, axis=-1, keepdims=True)
    h2 = x1 * lax.rsqrt(ms + EPS) * gain_ref[...]
    h2 = h2 * (1.0 + mod_ref[0, 4:5, :]) + mod_ref[0, 3:4, :]
    h2_ref[...] = _pack_rows(h2)

    n_experts = rwt_ref.shape[0]
    t = h2.shape[0]
    logits_t = lax.dot_general(rwt_ref[...], h2.astype(BF16), _NT,
                               preferred_element_type=F32)
    ids, gates, sel = _route(logits_t, rb_ref[...], n_experts)
    eid_ref[...] = ids
    gate_ref[...] = gates

    ti = lax.broadcasted_iota(I32, (t, t), 0)
    tj = lax.broadcasted_iota(I32, (t, t), 1)
    upper = (ti < tj).astype(BF16)
    selb = sel.astype(BF16)
    before = jnp.dot(selb, upper, preferred_element_type=F32) + carry[:, 0:1]
    iota_e = lax.broadcasted_iota(I32, (n_experts, t), 0)
    ranks = [jnp.sum(jnp.where(iota_e == ids[k:k + 1, :], before, 0.0), axis=0, keepdims=True)
             for k in range(TOP_K)]
    rank_ref[...] = jnp.concatenate(ranks, axis=0).astype(I32)
    total = carry[...] + jnp.dot(selb, jnp.ones((t, LANES), BF16), preferred_element_type=F32)
    carry[...] = total
    cnt_ref[...] = total.astype(I32)


def _outproj(ylru, yhg, x, mods, gain, wo, rwt, rb, tm):
    b, s, d = x.shape
    hw = ylru.shape[2]
    e = rwt.shape[0]
    n = b * s
    nt = s // tm
    tok = lambda bi, i: (0, bi * nt + i)
    return pl.pallas_call(
        _outproj_kernel,
        out_shape=(jax.ShapeDtypeStruct((b, s, d), F32),
                   jax.ShapeDtypeStruct((n, d // 2), U32),
                   jax.ShapeDtypeStruct((TOP_K, n), I32),
                   jax.ShapeDtypeStruct((TOP_K, n), F32),
                   jax.ShapeDtypeStruct((TOP_K, n), I32),
                   jax.ShapeDtypeStruct((e, LANES), I32)),
        grid=(b, nt),
        in_specs=[pl.BlockSpec((1, tm, hw), lambda bi, i: (bi, i, 0)),
                  pl.BlockSpec((1, tm, hw), lambda bi, i: (bi, i, 0)),
                  pl.BlockSpec((1, tm, d), lambda bi, i: (bi, i, 0)),
                  pl.BlockSpec((1, 8, d), lambda bi, i: (bi, 0, 0)),
                  pl.BlockSpec((1, d), lambda bi, i: (0, 0)),
                  pl.BlockSpec((2, hw, d), lambda bi, i: (0, 0, 0)),
                  pl.BlockSpec((e, d), lambda bi, i: (0, 0)),
                  pl.BlockSpec((e, 1), lambda bi, i: (0, 0))],
        out_specs=(pl.BlockSpec((1, tm, d), lambda bi, i: (bi, i, 0)),
                   pl.BlockSpec((tm, d // 2), lambda bi, i: (bi * nt + i, 0)),
                   pl.BlockSpec((TOP_K, tm), tok),
                   pl.BlockSpec((TOP_K, tm), tok),
                   pl.BlockSpec((TOP_K, tm), tok),
                   pl.BlockSpec((e, LANES), lambda bi, i: (0, 0))),
        scratch_shapes=[pltpu.VMEM((e, LANES), F32)],
        compiler_params=_cparams(("arbitrary", "arbitrary")),
        name="outproj",
    )(ylru, yhg, x, mods, gain, wo, rwt, rb)


def _pos_kernel(pstart, eid_ref, rank_ref, pos_ref):
    eid = eid_ref[...]

    def body(e, acc):
        return jnp.where(eid == e, pstart[e], acc)

    base = lax.fori_loop(0, pstart.shape[0], body, jnp.zeros(eid.shape, I32))
    pos_ref[...] = base + rank_ref[...]


def _pos(pstarts, eid_t, rank_t, tn):
    k, n = eid_t.shape
    spec = pl.BlockSpec((k, tn), lambda i, ps: (0, i))
    return pl.pallas_call(
        _pos_kernel,
        out_shape=jax.ShapeDtypeStruct((k, n), I32),
        grid_spec=pltpu.PrefetchScalarGridSpec(
            num_scalar_prefetch=1, grid=(n // tn,), in_specs=[spec, spec], out_specs=spec),
        compiler_params=_cparams(("arbitrary",)),
        name="pos",
    )(pstarts, eid_t, rank_t)


def _pad_fill(padstart, padlen, zbuf, xs_ref, sem, wait):
    nbits = zbuf.shape[0].bit_length() - 1
    low_bits = SUBLANES.bit_length() - 1

    def go(cp):
        if wait:
            cp.wait()
        else:
            cp.start()

    def body(e, carry):
        st = padstart[e]
        ln = padlen[e]
        end = st + ln
        for bit in range(nbits - 1, low_bits - 1, -1):
            size = 1 << bit
            back = ((ln >> (bit + 1)) << (bit + 1)) + size

            @pl.when((ln & size) != 0)
            def _():
                start = pl.multiple_of(end - back, SUBLANES)
                go(pltpu.make_async_copy(zbuf.at[pl.ds(0, size)],
                                         xs_ref.at[pl.ds(start, size)], sem))
        for j in range(SUBLANES - 1):
            @pl.when(j < (ln & (SUBLANES - 1)))
            def _():
                go(pltpu.make_async_copy(zbuf.at[pl.ds(0, 1)], xs_ref.at[pl.ds(st + j, 1)], sem))
        return carry

    lax.fori_loop(0, padstart.shape[0], body, 0)


def _padfill_kernel(padstart, padlen, xs_in, xs_ref, zbuf, sem):
    del xs_in
    zbuf[...] = jnp.zeros(zbuf.shape, zbuf.dtype)
    _pad_fill(padstart, padlen, zbuf, xs_ref, sem, wait=False)
    _pad_fill(padstart, padlen, zbuf, xs_ref, sem, wait=True)


def _padfill(xs, padstart, padlen, bm):
    return pl.pallas_call(
        _padfill_kernel,
        out_shape=jax.ShapeDtypeStruct(xs.shape, xs.dtype),
        grid_spec=pltpu.PrefetchScalarGridSpec(
            num_scalar_prefetch=2,
            grid=(1,),
            in_specs=[pl.BlockSpec(memory_space=pl.ANY)],
            out_specs=pl.BlockSpec(memory_space=pl.ANY),
            scratch_shapes=[pltpu.VMEM((bm, xs.shape[1]), xs.dtype), pltpu.SemaphoreType.DMA(())]),
        input_output_aliases={2: 0},
        compiler_params=_cparams(("arbitrary",)),
        name="padfill",
    )(padstart, padlen, xs)


SC_SCATTER_WINDOW = 128


def _sc_mesh():
    return plsc.VectorSubcoreMesh(core_axis_name="core", subcore_axis_name="subcore")


def _sc_scatter_rows(rows, pos, cap):
    n, dw = rows.shape
    top_k = pos.shape[0]
    mesh = _sc_mesh()
    n_workers = mesh.num_cores * mesh.num_subcores
    win = SC_SCATTER_WINDOW
    per_w = n // n_workers
    n_chunks = per_w // win
    assert per_w * n_workers == n and n_chunks * win == per_w
    pos_w = pos.reshape(top_k, n_workers, n_chunks, win).transpose(1, 2, 0, 3)
    pos_w = pos_w.reshape(n_workers, n_chunks * top_k, win)

    @functools.partial(pl.kernel, mesh=mesh,
                       out_type=jax.ShapeDtypeStruct((cap, dw), rows.dtype),
                       scratch_types=[pltpu.VMEM((n_chunks * top_k, win), I32),
                                      pltpu.VMEM((win, dw), rows.dtype),
                                      pltpu.SemaphoreType.DMA])
    def scatter(r_hbm, p_hbm, o_hbm, idx_v, rows_v, sem):
        wid = lax.axis_index("subcore") * mesh.num_cores + lax.axis_index("core")
        pltpu.sync_copy(p_hbm.at[wid], idx_v)

        @pl.loop(0, n_chunks)
        def _(c):
            pltpu.sync_copy(r_hbm.at[pl.ds(wid * per_w + c * win, win)], rows_v)
            copies = [pltpu.async_copy(rows_v, o_hbm.at[idx_v.at[c * top_k + k]], sem)
                      for k in range(top_k)]
            for cp in copies:
                cp.wait()

    return scatter(rows, pos_w)


GMM_SLOTS = 4
GMM_AHEAD = GMM_SLOTS - 1


def _gmm_rows(xw, w13b, w2b, de):
    xb = _unpack_rows(xw).astype(BF16)
    u = jnp.dot(xb, w13b[...], preferred_element_type=F32)
    hmid = _silu(u[:, :de]) * u[:, de:]
    return _pack_rows(jnp.dot(hmid.astype(BF16), w2b[...], preferred_element_type=F32))


def _gmm_kernel(blk0, nblk, w13_ref, w2_ref, xs_ref, ys_ref, w13b, w2b, xbuf, ybuf, isem, osem):
    e = pl.program_id(0)
    n_e = pl.num_programs(0)
    slots, bm = xbuf.shape[0], xbuf.shape[1]
    de = w2_ref.shape[1]
    nb = nblk[e]
    b0 = blk0[e]
    total = blk0[n_e - 1] + nblk[n_e - 1]

    def x_copy(g):
        rows = pl.ds(pl.multiple_of(g * bm, bm), bm)
        slot = g & (slots - 1)
        return pltpu.make_async_copy(xs_ref.at[rows], xbuf.at[slot], isem.at[slot])

    def y_copy(g):
        rows = pl.ds(pl.multiple_of(g * bm, bm), bm)
        slot = g & (slots - 1)
        return pltpu.make_async_copy(ybuf.at[slot], ys_ref.at[rows], osem.at[slot])

    @pl.when(e == 0)
    def _():
        for j in range(GMM_AHEAD):
            @pl.when(j < total)
            def _():
                x_copy(j).start()

    w13b[...] = w13_ref[0].astype(BF16)
    w2b[...] = w2_ref[0].astype(BF16)

    def body(b, carry):
        g = b0 + b
        slot = g & (slots - 1)
        x_copy(g).wait()

        @pl.when(g + GMM_AHEAD < total)
        def _():
            x_copy(g + GMM_AHEAD).start()

        @pl.when(g >= slots)
        def _():
            y_copy(g - slots).wait()

        ybuf[slot] = _gmm_rows(xbuf[slot], w13b, w2b, de)
        y_copy(g).start()
        return carry

    lax.fori_loop(0, nb, body, 0)

    @pl.when(e == n_e - 1)
    def _():
        for j in range(1, slots + 1):
            @pl.when(total - j >= 0)
            def _():
                y_copy(total - j).wait()


def _gmm(xs, w13, w2, blk0, nblk, bm):
    cap, dw = xs.shape
    e, d, de2 = w13.shape
    de = w2.shape[1]
    return pl.pallas_call(
        _gmm_kernel,
        out_shape=jax.ShapeDtypeStruct((cap, dw), U32),
        grid_spec=pltpu.PrefetchScalarGridSpec(
            num_scalar_prefetch=2,
            grid=(e,),
            in_specs=[pl.BlockSpec((1, d, de2), lambda i, b0, nb: (i, 0, 0)),
                      pl.BlockSpec((1, de, d), lambda i, b0, nb: (i, 0, 0)),
                      pl.BlockSpec(memory_space=pl.ANY)],
            out_specs=pl.BlockSpec(memory_space=pl.ANY),
            scratch_shapes=[pltpu.VMEM((d, de2), BF16), pltpu.VMEM((de, d), BF16),
                            pltpu.VMEM((GMM_SLOTS, bm, dw), U32),
                            pltpu.VMEM((GMM_SLOTS, bm, dw), U32),
                            pltpu.SemaphoreType.DMA((GMM_SLOTS,)),
                            pltpu.SemaphoreType.DMA((GMM_SLOTS,))]),
        compiler_params=_cparams(("arbitrary",)),
        name="gmm",
    )(blk0, nblk, w13, w2, xs)


def _sc_gather_rows(table, idx, window):
    n_idx = idx.shape[0]
    dw = table.shape[1]
    mesh = _sc_mesh()
    n_workers = mesh.num_cores * mesh.num_subcores
    per_w = n_idx // n_workers
    n_chunks = per_w // window
    assert per_w * n_workers == n_idx and n_chunks * window == per_w and n_chunks % 2 == 0

    @functools.partial(pl.kernel, mesh=mesh,
                       out_type=jax.ShapeDtypeStruct((n_idx, dw), table.dtype),
                       scratch_types=[pltpu.VMEM((per_w,), I32),
                                      pltpu.VMEM((window, dw), table.dtype),
                                      pltpu.VMEM((window, dw), table.dtype),
                                      pltpu.SemaphoreType.DMA, pltpu.SemaphoreType.DMA,
                                      pltpu.SemaphoreType.DMA, pltpu.SemaphoreType.DMA])
    def gather(x_hbm, i_hbm, o_hbm, idx_v, rows0, rows1, gs0, gs1, os0, os1):
        wid = lax.axis_index("subcore") * mesh.num_cores + lax.axis_index("core")
        base = wid * per_w
        pltpu.sync_copy(i_hbm.at[pl.ds(base, per_w)], idx_v)
        bufs = ((rows0, gs0, os0), (rows1, gs1, os1))

        def fetch(j, slot):
            rows, gs, _ = bufs[slot]
            off = pl.multiple_of(j * window, window)
            return pltpu.async_copy(x_hbm.at[idx_v.at[pl.ds(off, window)]], rows, gs)

        def fetch_wait(slot):
            rows, gs, _ = bufs[slot]
            pltpu.make_async_copy(x_hbm.at[idx_v.at[pl.ds(0, window)]], rows, gs).wait()

        def put(j, slot):
            rows, _, osem = bufs[slot]
            off = pl.multiple_of(j * window, window)
            return pltpu.async_copy(rows, o_hbm.at[pl.ds(base + off, window)], osem)

        def put_wait(slot):
            rows, _, osem = bufs[slot]
            pltpu.make_async_copy(rows, o_hbm.at[pl.ds(base, window)], osem).wait()

        fetch(0, 0)

        @pl.loop(0, n_chunks, step=2)
        def _(j):
            @pl.when(j > 0)
            def _():
                put_wait(1)
            fetch(j + 1, 1)
            fetch_wait(0)
            put(j, 0)
            put_wait(0)

            @pl.when(j + 2 < n_chunks)
            def _():
                fetch(j + 2, 0)
            fetch_wait(1)
            put(j + 1, 1)

        put_wait(1)

    return gather(table, idx)


def _combine_kernel(gate_ref, x1_ref, h2_ref, mod_ref, gain_ref, sw13_ref, sw2_ref, yg_ref, o_ref):
    ds_ = sw2_ref.shape[0]
    hb = _unpack_rows(h2_ref[...]).astype(BF16)
    u = jnp.dot(hb, sw13_ref[...], preferred_element_type=F32)
    hmid = _silu(u[:, :ds_]) * u[:, ds_:]
    y = jnp.dot(hmid.astype(BF16), sw2_ref[...], preferred_element_type=F32)

    gate = gate_ref[...]
    moe = gate[:, 0:1] * _unpack_rows(yg_ref[0])
    for k in range(1, TOP_K):
        moe = moe + gate[:, k:k + 1] * _unpack_rows(yg_ref[k])
    xo = x1_ref[...] + mod_ref[0, 5:6, :] * (moe + y)
    ms = jnp.mean(xo * xo, axis=-1, keepdims=True)
    o_ref[...] = xo * lax.rsqrt(ms + EPS) * gain_ref[...]


def _combine(gate_tok, x1, h2, mods, gain, sw13, sw2, yg, tc, tiles_per_batch):
    n, d = x1.shape
    dw = h2.shape[1]
    nt = n // tc
    ds2 = sw13.shape[1]
    ds_ = sw2.shape[0]
    return pl.pallas_call(
        _combine_kernel,
        out_shape=jax.ShapeDtypeStruct((n, d), F32),
        grid=(nt,),
        in_specs=[pl.BlockSpec((tc, TOP_K), lambda i: (i, 0)),
                  pl.BlockSpec((tc, d), lambda i: (i, 0)),
                  pl.BlockSpec((tc, dw), lambda i: (i, 0)),
                  pl.BlockSpec((1, 8, d), lambda i: (i // tiles_per_batch, 0, 0)),
                  pl.BlockSpec((1, d), lambda i: (0, 0)),
                  pl.BlockSpec((d, ds2), lambda i: (0, 0)),
                  pl.BlockSpec((ds_, d), lambda i: (0, 0)),
                  pl.BlockSpec((TOP_K, tc, dw), lambda i: (0, i, 0))],
        out_specs=pl.BlockSpec((tc, d), lambda i: (i, 0)),
        compiler_params=_cparams(("arbitrary",)),
        name="combine",
    )(gate_tok, x1, h2, mods, gain, sw13, sw2, yg)


def _block_diag_pairs(w, per):
    nb, bd, _ = w.shape
    w = w.reshape(nb // per, per, bd, bd)
    eye = jnp.eye(per, dtype=w.dtype)
    out = jnp.einsum("gpij,pq->gpiqj", w, eye)
    return out.reshape(nb // per, per * bd, per * bd)


def kernel(x, c, ctx, c_ctx, ada_w, ada_b, norm_mix, norm_ffn, norm_final, w_in, w_out,
           lru_conv_w, lru_conv_b, lru_wa, lru_ba, lru_wx, lru_bx, lru_lambda,
           hgrn_lb_logits, hgrn_norm, router_w, router_b, exp_w13, exp_w2, shared_w13, shared_w2):
    assert ada_w.shape[0] == 1, "single-layer block"
    b, s, d = x.shape
    n = b * s
    lru_w = lru_conv_w.shape[2]
    hg_w = hgrn_norm.shape[1]
    n_experts = router_w.shape[2]

    rows = -(-(b + 1) // SUBLANES) * SUBLANES
    cs = jnp.zeros((rows, d), F32).at[:b].set(c).at[b].set(c_ctx)
    mod = _ada(cs, ada_w[0], ada_b[0][None, :]).reshape(rows, 6, d)
    mods = jnp.pad(mod, ((0, 0), (0, 2), (0, 0)))

    w_in_bf = w_in[0].astype(BF16)
    gain1 = norm_mix[0][None, :]
    projx = _inproj(x, mods, gain1, w_in_bf, 512, shared_mod=False)
    projc = _inproj(ctx, mods[b:b + 1], gain1, w_in_bf, ctx.shape[1], shared_mod=True)

    per = LRU_G // (lru_w // LRU_BLOCKS)
    wg = jnp.stack([_block_diag_pairs(lru_wa[0, 0], per), _block_diag_pairs(lru_wx[0, 0], per),
                    _block_diag_pairs(lru_wa[0, 1], per), _block_diag_pairs(lru_wx[0, 1], per)]
                   ).astype(BF16)
    bg = jnp.stack([lru_ba[0, 0], lru_bx[0, 0], lru_ba[0, 1], lru_bx[0, 1]])
    ylru = _lru(projx, projc, lru_conv_w[0], lru_conv_b[0][None, :], wg, bg, lru_lambda[0])

    yhg = _hgrn(projx, projc, hgrn_lb_logits, hgrn_norm[0][None, :], (2 * lru_w) // LANES)

    wo = w_out[0].astype(BF16).reshape(2, lru_w, d)
    x1, h2, eid_t, gate_t, rank_t, cnt = _outproj(
        ylru, yhg, x, mods, norm_ffn[0][None, :], wo,
        router_w[0].T.astype(BF16), router_b[0][:, None], 512)

    bm = 256
    counts = cnt[:, 0]
    padded = (counts + bm - 1) // bm * bm
    pends = jnp.cumsum(padded)
    pstarts = pends - padded
    pos = _pos(pstarts, eid_t, rank_t, 2048)
    cap = n * TOP_K + n_experts * bm

    xs = _padfill(_sc_scatter_rows(h2, pos, cap), pstarts + counts, padded - counts, bm)
    ys = _gmm(xs, exp_w13[0], exp_w2[0], pstarts // bm, padded // bm, bm)

    yg = _sc_gather_rows(ys, pos.reshape(-1), 64).reshape(TOP_K, n, d // 2)
    tc = 256
    out = _combine(gate_t.T, x1.reshape(n, d), h2, mods,
                   norm_final[None, :], shared_w13[0].astype(BF16), shared_w2[0].astype(BF16),
                   yg, tc, s // tc)
    return out.reshape(b, s, d)
```

```python
import functools

import jax
import jax.numpy as jnp
from jax import lax
from jax.experimental import pallas as pl
from jax.experimental.pallas import tpu as pltpu
from jax.experimental.pallas import tpu_sc as plsc

F32 = jnp.float32
BF16 = jnp.bfloat16
I32 = jnp.int32
U32 = jnp.uint32

EPS = 1e-6
LRU_C = 8.0
ROUTE_SCALE = 2.5
GRID_W = 64
HG_CHUNK = 32
N_GROUPS = 8
TOPK_GROUPS = 4
TOP_K = 8
LRU_BLOCKS = 8
HG_HEADS = 4
CONV_W = 4
CONV_PAD_L = 1

SUBLANES = 8
LANES = 128
N_SEG = SUBLANES
VMEM_LIMIT = 56 * 1024 * 1024


def _cparams(sem, vmem=VMEM_LIMIT):
    return pltpu.CompilerParams(dimension_semantics=sem, vmem_limit_bytes=vmem)


def _sigmoid(x):
    return jax.nn.sigmoid(x)


def _sigmoid_tanh(x):
    return 0.5 * jnp.tanh(0.5 * x) + 0.5


def _silu(x):
    return x * _sigmoid(x)


def _pack_rows(x):
    w = x.shape[1] // 2
    bits = pltpu.bitcast(x.astype(BF16).astype(F32), U32)
    return (bits[:, :w] >> 16) | (bits[:, w:] & jnp.uint32(0xFFFF0000))


def _unpack_rows(p):
    lo = pltpu.bitcast(p << 16, F32)
    hi = pltpu.bitcast(p & jnp.uint32(0xFFFF0000), F32)
    return jnp.concatenate([lo, hi], axis=1)


def _gelu_tanh(x):
    c = 0.7978845608028654
    return 0.5 * x * (1.0 + jnp.tanh(c * (x + 0.044715 * (x * x * x))))


def _ada_kernel(c_ref, w_ref, b_ref, o_ref):
    s = _silu(c_ref[...])
    o_ref[...] = jnp.dot(s, w_ref[...], preferred_element_type=F32,
                         precision=lax.Precision.HIGHEST) + b_ref[...]


def _ada(cs, w, b):
    rows, d = cs.shape
    n = w.shape[1]
    bn = 1024
    return pl.pallas_call(
        _ada_kernel,
        out_shape=jax.ShapeDtypeStruct((rows, n), F32),
        grid=(n // bn,),
        in_specs=[pl.BlockSpec((rows, d), lambda j: (0, 0)),
                  pl.BlockSpec((d, bn), lambda j: (0, j)),
                  pl.BlockSpec((1, bn), lambda j: (0, j))],
        out_specs=pl.BlockSpec((rows, bn), lambda j: (0, j)),
        compiler_params=_cparams(("arbitrary",)),
        name="ada",
    )(cs, w, b)


def _inproj_kernel(x_ref, mod_ref, gain_ref, w_ref, o_ref):
    x = x_ref[0]
    ms = jnp.mean(x * x, axis=-1, keepdims=True)
    y = x * lax.rsqrt(ms + EPS) * gain_ref[...]
    h = y * (1.0 + mod_ref[0, 1:2, :]) + mod_ref[0, 0:1, :]
    o_ref[0] = jnp.dot(h.astype(BF16), w_ref[...], preferred_element_type=F32)


def _inproj(x, mods, gain, w_bf, tm, shared_mod):
    b, s, d = x.shape
    n = w_bf.shape[1]
    mod_map = (lambda bi, i: (0, 0, 0)) if shared_mod else (lambda bi, i: (bi, 0, 0))
    return pl.pallas_call(
        _inproj_kernel,
        out_shape=jax.ShapeDtypeStruct((b, s, n), F32),
        grid=(b, s // tm),
        in_specs=[pl.BlockSpec((1, tm, d), lambda bi, i: (bi, i, 0)),
                  pl.BlockSpec((1, 8, d), mod_map),
                  pl.BlockSpec((1, d), lambda bi, i: (0, 0)),
                  pl.BlockSpec((d, n), lambda bi, i: (0, 0))],
        out_specs=pl.BlockSpec((1, tm, n), lambda bi, i: (bi, i, 0)),
        compiler_params=_cparams(("arbitrary", "arbitrary")),
        name="inproj",
    )(x, mods, gain, w_bf)


LRU_G = 256
LRU_CHUNK = 256


def _seg_pitch(seg_len):
    return seg_len + SUBLANES


def _lane_store(ref, d, rows, val):
    nl = val.shape[-1] // LANES
    for l in range(nl):
        ref[d * nl + l, rows, :] = val[:, l * LANES:(l + 1) * LANES]


def _lane_load(ref, d, rows, nl):
    return jnp.concatenate([ref[d * nl + l, rows, :] for l in range(nl)], axis=-1)


def _lru_coeffs(pad_ref, t0, rows, cw, cb, wg_ref, bg, sp, a_ref, b_ref, dst0):
    win = pad_ref[pl.ds(t0, rows + 2 * SUBLANES), :]
    u = cb
    for k in range(CONV_W):
        off = SUBLANES - CONV_PAD_L + k
        u = u + win[off:off + rows, :] * cw[k:k + 1, :]
    ub = u.astype(BF16)
    for d in range(2):
        r = _sigmoid_tanh(jnp.dot(ub, wg_ref[2 * d, 0], preferred_element_type=F32)
                          + bg[2 * d:2 * d + 1, :])
        ig = _sigmoid_tanh(jnp.dot(ub, wg_ref[2 * d + 1, 0], preferred_element_type=F32)
                           + bg[2 * d + 1:2 * d + 2, :])
        log_a = (-LRU_C) * r * sp[d:d + 1, :]
        a = jnp.exp(log_a)
        one_minus_a2 = -jnp.tanh(log_a) * (1.0 + a * a)
        bb = jnp.sqrt(one_minus_a2) * (ig * u)
        _lane_store(a_ref, d, pl.ds(dst0, rows), a)
        _lane_store(b_ref, d, pl.ds(dst0, rows), bb)


def _seg_scan(a_ref, b_ref, h_ref, p_ref, seg_len, pitch, unroll):
    n_lead = a_ref.shape[0]
    nl = n_lead // 2
    zero = jnp.zeros((N_SEG, LANES), F32)
    one = jnp.ones((N_SEG, LANES), F32)
    init = tuple((zero, one) for _ in range(n_lead))

    def step(t, carry):
        out = []
        for i in range(n_lead):
            h, p = carry[i]
            tt = t if i < nl else seg_len - 1 - t
            rows = pl.ds(tt, N_SEG, stride=pitch)
            a = a_ref[i, rows, :]
            h = a * h + b_ref[i, rows, :]
            p = p * a
            if h_ref is not None:
                h_ref[i, rows, :] = h
                p_ref[i, rows, :] = p
            out.append((h, p))
        return tuple(out)

    def body(i, carry):
        for j in range(unroll):
            carry = step(i * unroll + j, carry)
        return carry

    ends = lax.fori_loop(0, seg_len // unroll, body, init)
    h_end = [jnp.concatenate([ends[d * nl + l][0] for l in range(nl)], axis=-1) for d in range(2)]
    p_end = [jnp.concatenate([ends[d * nl + l][1] for l in range(nl)], axis=-1) for d in range(2)]
    return h_end, p_end


def _seg_carries(h_end, p_end, h0, reverse):
    order = range(N_SEG - 1, -1, -1) if reverse else range(N_SEG)
    cin = [None] * N_SEG
    c = h0
    for s in order:
        cin[s] = c
        c = p_end[s:s + 1, :] * c + h_end[s:s + 1, :]
    return cin, c


def _lru_kernel(rx_ref, rg_ref, rc_ref, cw_ref, cb_ref, wg_ref, bg_ref, lam_ref, o_ref,
                padl, padc, a_l, b_l, h_l, p_l, a_c, b_c):
    s_len = rx_ref.shape[1]
    c_len = rc_ref.shape[1]
    g = rx_ref.shape[2]
    nl = g // LANES
    seg_l = s_len // N_SEG
    seg_c = c_len // N_SEG
    pitch_l = _seg_pitch(seg_l)
    pitch_c = _seg_pitch(seg_c)

    zeros = jnp.zeros((SUBLANES, g), F32)
    padl[0:SUBLANES, :] = zeros
    padl[SUBLANES + s_len:2 * SUBLANES + s_len, :] = zeros
    padc[0:SUBLANES, :] = zeros
    padc[SUBLANES + c_len:2 * SUBLANES + c_len, :] = zeros
    for i in range(s_len // LRU_CHUNK):
        padl[SUBLANES + i * LRU_CHUNK:SUBLANES + (i + 1) * LRU_CHUNK, :] = (
            rx_ref[0, i * LRU_CHUNK:(i + 1) * LRU_CHUNK, :])
    padc[SUBLANES:SUBLANES + c_len, :] = rc_ref[0]

    cw = cw_ref[...]
    cb = cb_ref[...]
    bg = bg_ref[...]
    x = -lam_ref[...]
    sp = jnp.maximum(x, 0.0) + jnp.log(1.0 + jnp.exp(-jnp.abs(x)))

    _lru_coeffs(padc, 0, c_len, cw, cb, wg_ref, bg, sp, a_l, b_l, 0)
    for i in range(2 * nl):
        for s in range(N_SEG):
            a_c[i, s * pitch_c:s * pitch_c + seg_c, :] = a_l[i, s * seg_c:(s + 1) * seg_c, :]
            b_c[i, s * pitch_c:s * pitch_c + seg_c, :] = b_l[i, s * seg_c:(s + 1) * seg_c, :]
    h_end, p_end = _seg_scan(a_c, b_c, None, None, seg_c, pitch_c, unroll=4)
    zero_row = jnp.zeros((1, g), F32)
    _, h0_f = _seg_carries(h_end[0], p_end[0], zero_row, reverse=False)
    _, h0_b = _seg_carries(h_end[1], p_end[1], zero_row, reverse=True)

    def coeff_body(s, carry):
        t0 = pl.multiple_of(s * seg_l, SUBLANES)
        dst = pl.multiple_of(s * pitch_l, SUBLANES)
        _lru_coeffs(padl, t0, seg_l, cw, cb, wg_ref, bg, sp, a_l, b_l, dst)
        return carry

    lax.fori_loop(0, N_SEG, coeff_body, 0)
    h_end, p_end = _seg_scan(a_l, b_l, h_l, p_l, seg_l, pitch_l, unroll=4)
    cin_f, _ = _seg_carries(h_end[0], p_end[0], h0_f, reverse=False)
    cin_b, _ = _seg_carries(h_end[1], p_end[1], h0_b, reverse=True)

    for s in range(N_SEG):
        rows = slice(s * pitch_l, s * pitch_l + seg_l)
        h = (_lane_load(h_l, 0, rows, nl) + _lane_load(p_l, 0, rows, nl) * cin_f[s]) + (
            _lane_load(h_l, 1, rows, nl) + _lane_load(p_l, 1, rows, nl) * cin_b[s])
        o_ref[0, s * seg_l:(s + 1) * seg_l, :] = _gelu_tanh(rg_ref[0, s * seg_l:(s + 1) * seg_l, :]) * h


def _lru(projx, projc, cw, cb, wg, bg, lam):
    b, s_len, _ = projx.shape
    c_len = projc.shape[1]
    w = cw.shape[1]
    g = LRU_G
    ng = w // g
    seg_l = s_len // N_SEG
    seg_c = c_len // N_SEG
    assert seg_l == LRU_CHUNK and c_len % (N_SEG * 1) == 0
    return pl.pallas_call(
        _lru_kernel,
        out_shape=jax.ShapeDtypeStruct((b, s_len, w), F32),
        grid=(b, ng),
        in_specs=[pl.BlockSpec((1, s_len, g), lambda bi, j: (bi, 0, j)),
                  pl.BlockSpec((1, s_len, g), lambda bi, j: (bi, 0, ng + j)),
                  pl.BlockSpec((1, c_len, g), lambda bi, j: (bi, 0, j)),
                  pl.BlockSpec((CONV_W, g), lambda bi, j: (0, j)),
                  pl.BlockSpec((1, g), lambda bi, j: (0, j)),
                  pl.BlockSpec((4, 1, g, g), lambda bi, j: (0, j, 0, 0)),
                  pl.BlockSpec((4, g), lambda bi, j: (0, j)),
                  pl.BlockSpec((2, g), lambda bi, j: (0, j))],
        out_specs=pl.BlockSpec((1, s_len, g), lambda bi, j: (bi, 0, j)),
        scratch_shapes=[pltpu.VMEM((s_len + 2 * SUBLANES, g), F32),
                        pltpu.VMEM((c_len + 2 * SUBLANES, g), F32),
                        pltpu.VMEM((2 * g // LANES, N_SEG * _seg_pitch(seg_l), LANES), F32),
                        pltpu.VMEM((2 * g // LANES, N_SEG * _seg_pitch(seg_l), LANES), F32),
                        pltpu.VMEM((2 * g // LANES, N_SEG * _seg_pitch(seg_l), LANES), F32),
                        pltpu.VMEM((2 * g // LANES, N_SEG * _seg_pitch(seg_l), LANES), F32),
                        pltpu.VMEM((2 * g // LANES, N_SEG * _seg_pitch(seg_c), LANES), F32),
                        pltpu.VMEM((2 * g // LANES, N_SEG * _seg_pitch(seg_c), LANES), F32)],
        compiler_params=_cparams(("arbitrary", "arbitrary")),
        name="lru",
    )(projx, projx, projc, cw, cb, wg, bg, lam)


HG_PITCH_L = GRID_W + SUBLANES
HG_PITCH_C = SUBLANES


def _hg_gates(z, lb):
    e = jnp.exp(-jnp.abs(z))
    s = 1.0 / (1.0 + e)
    es = e * s
    pos = z >= 0.0
    sig = jnp.where(pos, s, es)
    nsig = jnp.where(pos, es, s)
    logf = jnp.log(lb + (1.0 - lb) * sig)
    k = (1.0 - lb) * nsig
    return logf, k


def _hg_prepass(load, width, pitch, lb, qd, kd, kl, vs, dec):
    n_pos = HG_CHUNK
    for d, zname in enumerate(("zf", "zb")):
        order = range(n_pos) if d == 0 else range(n_pos - 1, -1, -1)
        g = jnp.zeros((width, LANES), F32)
        lbd = lb[d:d + 1, :]
        for p in order:
            logf, k = _hg_gates(load(zname, p), lbd)
            g = g + logf
            qd[d, p * pitch:p * pitch + width, :] = g
            kl[d, p * pitch:p * pitch + width, :] = k
        g_last = g
        dec[d, 0:width, :] = jnp.exp(g_last)
        for p in range(n_pos):
            rows = slice(p * pitch, p * pitch + width)
            gp = qd[d, rows, :]
            k = kl[d, rows, :]
            qd[d, rows, :] = load("q", p) * jnp.exp(gp)
            kd[d, rows, :] = k * jnp.exp(-gp)
            kl[d, rows, :] = k * jnp.exp(g_last - gp)
    for p in range(n_pos):
        vs[p * pitch:p * pitch + width, :] = load("v", p)


HG_GROUP = 4
HG_UNROLL = 8
HG_INTRA_UNROLL = 4
_NT = (((1,), (1,)), ((), ()))
_TN = (((0,), (0,)), ((), ()))


def _hg_group_rows(ref, lead, c0, pitch):
    parts = []
    for j in range(HG_GROUP):
        rows = pl.ds(c0 + j, HG_CHUNK, stride=pitch)
        parts.append(ref[rows, :] if lead is None else ref[lead, rows, :])
    return jnp.concatenate(parts, axis=0)


def _hg_scores(g, pitch, qd, kd, masks, sc_ref):
    c0 = g * HG_GROUP
    total = None
    for d in range(2):
        q = _hg_group_rows(qd, d, c0, pitch).astype(BF16)
        k = _hg_group_rows(kd, d, c0, pitch).astype(BF16)
        sc = jnp.where(masks[d], lax.dot_general(q, k, _NT, preferred_element_type=F32), 0.0)
        total = sc if total is None else total + sc
    sc_ref[g] = total.astype(BF16)


def _hg_summaries(g, pitch, kl, vs, sc_ref, oi, kv):
    c0 = g * HG_GROUP
    v = _hg_group_rows(vs, None, c0, pitch).astype(BF16)
    if oi is not None:
        o = jnp.dot(sc_ref[g], v, preferred_element_type=F32)
        for j in range(HG_GROUP):
            oi[pl.ds(c0 + j, HG_CHUNK, stride=pitch), :] = o[j * HG_CHUNK:(j + 1) * HG_CHUNK, :]
    kls = jnp.concatenate([_hg_group_rows(kl, 0, c0, pitch), _hg_group_rows(kl, 1, c0, pitch)],
                          axis=1).astype(BF16)
    for j in range(HG_GROUP):
        sl = slice(j * HG_CHUNK, (j + 1) * HG_CHUNK)
        both = lax.dot_general(v[sl], kls[sl], _TN, preferred_element_type=F32)
        kv[0, c0 + j] = both[:, :LANES]
        kv[1, c0 + j] = both[:, LANES:]


def _hg_steps(step0, n_steps, n_chunks, pitch, qd, dec, kv, states, ox):
    states = list(states)
    for u in range(n_steps):
        for d in range(2):
            c = step0 + u if d == 0 else n_chunks - 1 - (step0 + u)
            s_t = states[d]
            if ox is not None:
                rows = pl.ds(c, HG_CHUNK, stride=pitch)
                q = qd[d, rows, :].astype(BF16)
                ox[d, rows, :] = lax.dot_general(q, s_t.astype(BF16), _NT,
                                                 preferred_element_type=F32)
            states[d] = s_t * dec[d, pl.ds(c, 1), :] + kv[d, c]
    return tuple(states)


def _hgrn_kernel(q_ref, v_ref, zf_ref, zb_ref, g_ref, qc_ref, vc_ref, zfc_ref, zbc_ref,
                 lbl_ref, gain_ref, o_ref, qd, kd, kl, vs, dec, oi, ox, kv, sc):
    s_len = q_ref.shape[1]
    c_len = qc_ref.shape[1]
    n_col = s_len // HG_CHUNK
    n_cc = c_len // HG_CHUNK
    assert n_col == GRID_W and n_cc == HG_PITCH_C
    assert n_col % HG_GROUP == 0 and n_cc % HG_GROUP == 0 and n_col % HG_UNROLL == 0

    lg = lbl_ref[...]
    m = jnp.max(lg, axis=1, keepdims=True)
    ex = jnp.exp(lg - m)
    lb = ex[:, 0, :] / jnp.sum(ex, axis=1)

    gr = HG_GROUP * HG_CHUNK
    ii = lax.broadcasted_iota(I32, (gr, gr), 0)
    jj = lax.broadcasted_iota(I32, (gr, gr), 1)
    same = (ii // HG_CHUNK) == (jj // HG_CHUNK)
    masks = (jnp.logical_and(same, jj <= ii), jnp.logical_and(same, jj >= ii))

    c_refs = {"q": qc_ref, "v": vc_ref, "zf": zfc_ref, "zb": zbc_ref}

    def load_c(name, p):
        return c_refs[name][0, pl.ds(p, n_cc, stride=HG_CHUNK), :]

    _hg_prepass(load_c, n_cc, HG_PITCH_C, lb, qd, kd, kl, vs, dec)
    for g in range(n_cc // HG_GROUP):
        _hg_summaries(g, HG_PITCH_C, kl, vs, None, None, kv)
    zero = jnp.zeros((LANES, LANES), F32)
    states = _hg_steps(0, n_cc, n_cc, HG_PITCH_C, qd, dec, kv, (zero, zero), None)

    l_refs = {"q": q_ref, "v": v_ref, "zf": zf_ref, "zb": zb_ref}

    def load_l(name, p):
        return l_refs[name][0, p * n_col:(p + 1) * n_col, :]

    _hg_prepass(load_l, n_col, HG_PITCH_L, lb, qd, kd, kl, vs, dec)

    n_it = n_col // (HG_GROUP * HG_INTRA_UNROLL)

    def scores_body(i, carry):
        for u in range(HG_INTRA_UNROLL):
            _hg_scores(i * HG_INTRA_UNROLL + u, HG_PITCH_L, qd, kd, masks, sc)
        return carry

    lax.fori_loop(0, n_it, scores_body, 0)

    def summaries_body(i, carry):
        for u in range(HG_INTRA_UNROLL):
            _hg_summaries(i * HG_INTRA_UNROLL + u, HG_PITCH_L, kl, vs, sc, oi, kv)
        return carry

    lax.fori_loop(0, n_it, summaries_body, 0)

    def step_body(i, carry):
        return _hg_steps(i * HG_UNROLL, HG_UNROLL, n_col, HG_PITCH_L, qd, dec, kv, carry, ox)

    lax.fori_loop(0, n_col // HG_UNROLL, step_body, states)

    gain = gain_ref[...]
    for r in range(HG_CHUNK):
        rows = slice(r * HG_PITCH_L, r * HG_PITCH_L + n_col)
        o = oi[rows, :] + (ox[0, rows, :] + ox[1, rows, :])
        y = o * lax.rsqrt(jnp.mean(o * o, axis=-1, keepdims=True) + EPS) * gain
        o_ref[0, r * n_col:(r + 1) * n_col, :] = y * _silu(g_ref[0, r * n_col:(r + 1) * n_col, :])


def _hgrn(projx, projc, lbl, gain, col0):
    b, s_len, _ = projx.shape
    c_len = projc.shape[1]
    hw = gain.shape[1]
    nh = hw // LANES
    n_slots = lbl.shape[1]

    def xs(k):
        return pl.BlockSpec((1, s_len, LANES), lambda bi, h, k=k: (bi, 0, col0 + k * nh + h))

    def cs(k):
        return pl.BlockSpec((1, c_len, LANES), lambda bi, h, k=k: (bi, 0, col0 + k * nh + h))

    sc_rows = HG_CHUNK * HG_PITCH_L
    return pl.pallas_call(
        _hgrn_kernel,
        out_shape=jax.ShapeDtypeStruct((b, s_len, hw), F32),
        grid=(b, nh),
        in_specs=[xs(0), xs(1), xs(2), xs(3), xs(4), cs(0), cs(1), cs(2), cs(3),
                  pl.BlockSpec((2, n_slots, LANES), lambda bi, h: (0, 0, h)),
                  pl.BlockSpec((1, LANES), lambda bi, h: (0, h))],
        out_specs=pl.BlockSpec((1, s_len, LANES), lambda bi, h: (bi, 0, h)),
        scratch_shapes=[pltpu.VMEM((2, sc_rows, LANES), F32),
                        pltpu.VMEM((2, sc_rows, LANES), F32),
                        pltpu.VMEM((2, sc_rows, LANES), F32),
                        pltpu.VMEM((sc_rows, LANES), F32),
                        pltpu.VMEM((2, HG_PITCH_L, LANES), F32),
                        pltpu.VMEM((sc_rows, LANES), F32),
                        pltpu.VMEM((2, sc_rows, LANES), F32),
                        pltpu.VMEM((2, GRID_W, LANES, LANES), F32),
                        pltpu.VMEM((GRID_W // HG_GROUP, HG_GROUP * HG_CHUNK,
                                    HG_GROUP * HG_CHUNK), BF16)],
        compiler_params=_cparams(("arbitrary", "arbitrary")),
        name="hgrn",
    )(projx, projx, projx, projx, projx, projc, projc, projc, projc, lbl, gain)


def _first_index_of_max(vals, iota, big):
    m = jnp.max(vals, axis=0, keepdims=True)
    idx = jnp.min(jnp.where(vals == m, iota, big), axis=0, keepdims=True)
    return m, idx


def _route(logits_t, rb, n_experts):
    t = logits_t.shape[1]
    gsz = n_experts // N_GROUPS
    neg = -jnp.inf
    scores = _sigmoid(logits_t)
    biased = scores + rb
    iota_g = lax.broadcasted_iota(I32, (gsz, t), 0)
    gscore = []
    for gi in range(N_GROUPS):
        blk = biased[gi * gsz:(gi + 1) * gsz, :]
        m1, i1 = _first_index_of_max(blk, iota_g, gsz)
        m2 = jnp.max(jnp.where(iota_g == i1, neg, blk), axis=0, keepdims=True)
        gscore.append(m1 + m2)
    gs = jnp.concatenate(gscore, axis=0)
    iota_n = lax.broadcasted_iota(I32, (N_GROUPS, t), 0)
    gsel = jnp.zeros((N_GROUPS, t), jnp.bool_)
    for _ in range(TOPK_GROUPS):
        _, gi1 = _first_index_of_max(gs, iota_n, N_GROUPS)
        hit = iota_n == gi1
        gsel = jnp.logical_or(gsel, hit)
        gs = jnp.where(hit, neg, gs)
    emask = jnp.concatenate(
        [jnp.broadcast_to(gsel[gi:gi + 1, :], (gsz, t)) for gi in range(N_GROUPS)], axis=0)
    masked = jnp.where(emask, biased, neg)
    iota_e = lax.broadcasted_iota(I32, (n_experts, t), 0)
    ids, gates = [], []
    sel = jnp.zeros((n_experts, t), F32)
    for _ in range(TOP_K):
        _, ei = _first_index_of_max(masked, iota_e, n_experts)
        hit = iota_e == ei
        ids.append(ei)
        gates.append(jnp.sum(jnp.where(hit, scores, 0.0), axis=0, keepdims=True))
        sel = jnp.where(hit, 1.0, sel)
        masked = jnp.where(hit, neg, masked)
    ids = jnp.concatenate(ids, axis=0)
    gates = jnp.concatenate(gates, axis=0)
    gates = gates / jnp.sum(gates, axis=0, keepdims=True) * ROUTE_SCALE
    return ids, gates, sel


def _outproj_kernel(yl_ref, yh_ref, x_ref, mod_ref, gain_ref, wo_ref, rwt_ref, rb_ref,
                    x1_ref, h2_ref, eid_ref, gate_ref, rank_ref, cnt_ref, carry):
    first = jnp.logical_and(pl.program_id(0) == 0, pl.program_id(1) == 0)

    @pl.when(first)
    def _():
        carry[...] = jnp.zeros(carry.shape, F32)

    y = jnp.dot(yl_ref[0].astype(BF16), wo_ref[0], preferred_element_type=F32)
    y = y + jnp.dot(yh_ref[0].astype(BF16), wo_ref[1], preferred_element_type=F32)
    x1 = x_ref[0] + mod_ref[0, 2:3, :] * y
    x1_ref[0] = x1
    ms = jnp.mean(x1 * x1, axis=-1, keepdims=True)
    h2 = x1 * lax.rsqrt(ms + EPS) * gain_ref[...]
    h2 = h2 * (1.0 + mod_ref[0, 4:5, :]) + mod_ref[0, 3:4, :]
    h2_ref[...] = _pack_rows(h2)

    n_experts = rwt_ref.shape[0]
    t = h2.shape[0]
    logits_t = lax.dot_general(rwt_ref[...], h2.astype(BF16), _NT,
                               preferred_element_type=F32)
    ids, gates, sel = _route(logits_t, rb_ref[...], n_experts)
    eid_ref[...] = ids
    gate_ref[...] = gates

    ti = lax.broadcasted_iota(I32, (t, t), 0)
    tj = lax.broadcasted_iota(I32, (t, t), 1)
    upper = (ti < tj).astype(BF16)
    selb = sel.astype(BF16)
    before = jnp.dot(selb, upper, preferred_element_type=F32) + carry[:, 0:1]
    iota_e = lax.broadcasted_iota(I32, (n_experts, t), 0)
    ranks = [jnp.sum(jnp.where(iota_e == ids[k:k + 1, :], before, 0.0), axis=0, keepdims=True)
             for k in range(TOP_K)]
    rank_ref[...] = jnp.concatenate(ranks, axis=0).astype(I32)
    total = carry[...] + jnp.dot(selb, jnp.ones((t, LANES), BF16), preferred_element_type=F32)
    carry[...] = total
    cnt_ref[...] = total.astype(I32)


def _outproj(ylru, yhg, x, mods, gain, wo, rwt, rb, tm):
    b, s, d = x.shape
    hw = ylru.shape[2]
    e = rwt.shape[0]
    n = b * s
    nt = s // tm
    tok = lambda bi, i: (0, bi * nt + i)
    return pl.pallas_call(
        _outproj_kernel,
        out_shape=(jax.ShapeDtypeStruct((b, s, d), F32),
                   jax.ShapeDtypeStruct((n, d // 2), U32),
                   jax.ShapeDtypeStruct((TOP_K, n), I32),
                   jax.ShapeDtypeStruct((TOP_K, n), F32),
                   jax.ShapeDtypeStruct((TOP_K, n), I32),
                   jax.ShapeDtypeStruct((e, LANES), I32)),
        grid=(b, nt),
        in_specs=[pl.BlockSpec((1, tm, hw), lambda bi, i: (bi, i, 0)),
                  pl.BlockSpec((1, tm, hw), lambda bi, i: (bi, i, 0)),
                  pl.BlockSpec((1, tm, d), lambda bi, i: (bi, i, 0)),
                  pl.BlockSpec((1, 8, d), lambda bi, i: (bi, 0, 0)),
                  pl.BlockSpec((1, d), lambda bi, i: (0, 0)),
                  pl.BlockSpec((2, hw, d), lambda bi, i: (0, 0, 0)),
                  pl.BlockSpec((e, d), lambda bi, i: (0, 0)),
                  pl.BlockSpec((e, 1), lambda bi, i: (0, 0))],
        out_specs=(pl.BlockSpec((1, tm, d), lambda bi, i: (bi, i, 0)),
                   pl.BlockSpec((tm, d // 2), lambda bi, i: (bi * nt + i, 0)),
                   pl.BlockSpec((TOP_K, tm), tok),
                   pl.BlockSpec((TOP_K, tm), tok),
                   pl.BlockSpec((TOP_K, tm), tok),
                   pl.BlockSpec((e, LANES), lambda bi, i: (0, 0))),
        scratch_shapes=[pltpu.VMEM((e, LANES), F32)],
        compiler_params=_cparams(("arbitrary", "arbitrary")),
        name="outproj",
    )(ylru, yhg, x, mods, gain, wo, rwt, rb)


def _pos_kernel(pstart, eid_ref, rank_ref, pos_ref):
    eid = eid_ref[...]

    def body(e, acc):
        return jnp.where(eid == e, pstart[e], acc)

    base = lax.fori_loop(0, pstart.shape[0], body, jnp.zeros(eid.shape, I32))
    pos_ref[...] = base + rank_ref[...]


def _pos(pstarts, eid_t, rank_t, tn):
    k, n = eid_t.shape
    spec = pl.BlockSpec((k, tn), lambda i, ps: (0, i))
    return pl.pallas_call(
        _pos_kernel,
        out_shape=jax.ShapeDtypeStruct((k, n), I32),
        grid_spec=pltpu.PrefetchScalarGridSpec(
            num_scalar_prefetch=1, grid=(n // tn,), in_specs=[spec, spec], out_specs=spec),
        compiler_params=_cparams(("arbitrary",)),
        name="pos",
    )(pstarts, eid_t, rank_t)


def _pad_fill(padstart, padlen, zbuf, xs_ref, sem, wait):
    nbits = zbuf.shape[0].bit_length() - 1
    low_bits = SUBLANES.bit_length() - 1

    def go(cp):
        if wait:
            cp.wait()
        else:
            cp.start()

    def body(e, carry):
        st = padstart[e]
        ln = padlen[e]
        end = st + ln
        for bit in range(nbits - 1, low_bits - 1, -1):
            size = 1 << bit
            back = ((ln >> (bit + 1)) << (bit + 1)) + size

            @pl.when((ln & size) != 0)
            def _():
                start = pl.multiple_of(end - back, SUBLANES)
                go(pltpu.make_async_copy(zbuf.at[pl.ds(0, size)],
                                         xs_ref.at[pl.ds(start, size)], sem))
        for j in range(SUBLANES - 1):
            @pl.when(j < (ln & (SUBLANES - 1)))
            def _():
                go(pltpu.make_async_copy(zbuf.at[pl.ds(0, 1)], xs_ref.at[pl.ds(st + j, 1)], sem))
        return carry

    lax.fori_loop(0, padstart.shape[0], body, 0)


def _padfill_kernel(padstart, padlen, xs_in, xs_ref, zbuf, sem):
    del xs_in
    zbuf[...] = jnp.zeros(zbuf.shape, zbuf.dtype)
    _pad_fill(padstart, padlen, zbuf, xs_ref, sem, wait=False)
    _pad_fill(padstart, padlen, zbuf, xs_ref, sem, wait=True)


def _padfill(xs, padstart, padlen, bm):
    return pl.pallas_call(
        _padfill_kernel,
        out_shape=jax.ShapeDtypeStruct(xs.shape, xs.dtype),
        grid_spec=pltpu.PrefetchScalarGridSpec(
            num_scalar_prefetch=2,
            grid=(1,),
            in_specs=[pl.BlockSpec(memory_space=pl.ANY)],
            out_specs=pl.BlockSpec(memory_space=pl.ANY),
            scratch_shapes=[pltpu.VMEM((bm, xs.shape[1]), xs.dtype), pltpu.SemaphoreType.DMA(())]),
        input_output_aliases={2: 0},
        compiler_params=_cparams(("arbitrary",)),
        name="padfill",
    )(padstart, padlen, xs)


SC_SCATTER_WINDOW = 128


def _sc_mesh():
    return plsc.VectorSubcoreMesh(core_axis_name="core", subcore_axis_name="subcore")


def _sc_scatter_rows(rows, pos, cap):
    n, dw = rows.shape
    top_k = pos.shape[0]
    mesh = _sc_mesh()
    n_workers = mesh.num_cores * mesh.num_subcores
    win = SC_SCATTER_WINDOW
    per_w = n // n_workers
    n_chunks = per_w // win
    assert per_w * n_workers == n and n_chunks * win == per_w
    pos_w = pos.reshape(top_k, n_workers, n_chunks, win).transpose(1, 2, 0, 3)
    pos_w = pos_w.reshape(n_workers, n_chunks * top_k, win)

    @functools.partial(pl.kernel, mesh=mesh,
                       out_type=jax.ShapeDtypeStruct((cap, dw), rows.dtype),
                       scratch_types=[pltpu.VMEM((n_chunks * top_k, win), I32),
                                      pltpu.VMEM((win, dw), rows.dtype),
                                      pltpu.SemaphoreType.DMA])
    def scatter(r_hbm, p_hbm, o_hbm, idx_v, rows_v, sem):
        wid = lax.axis_index("subcore") * mesh.num_cores + lax.axis_index("core")
        pltpu.sync_copy(p_hbm.at[wid], idx_v)

        @pl.loop(0, n_chunks)
        def _(c):
            pltpu.sync_copy(r_hbm.at[pl.ds(wid * per_w + c * win, win)], rows_v)
            copies = [pltpu.async_copy(rows_v, o_hbm.at[idx_v.at[c * top_k + k]], sem)
                      for k in range(top_k)]
            for cp in copies:
                cp.wait()

    return scatter(rows, pos_w)


GMM_SLOTS = 4
GMM_AHEAD = GMM_SLOTS - 1


def _gmm_rows(xw, w13b, w2b, de):
    xb = _unpack_rows(xw).astype(BF16)
    u = jnp.dot(xb, w13b[...], preferred_element_type=F32)
    hmid = _silu(u[:, :de]) * u[:, de:]
    return _pack_rows(jnp.dot(hmid.astype(BF16), w2b[...], preferred_element_type=F32))


def _gmm_kernel(blk0, nblk, w13_ref, w2_ref, xs_ref, ys_ref, w13b, w2b, xbuf, ybuf, isem, osem):
    e = pl.program_id(0)
    n_e = pl.num_programs(0)
    slots, bm = xbuf.shape[0], xbuf.shape[1]
    de = w2_ref.shape[1]
    nb = nblk[e]
    b0 = blk0[e]
    total = blk0[n_e - 1] + nblk[n_e - 1]

    def x_copy(g):
        rows = pl.ds(pl.multiple_of(g * bm, bm), bm)
        slot = g & (slots - 1)
        return pltpu.make_async_copy(xs_ref.at[rows], xbuf.at[slot], isem.at[slot])

    def y_copy(g):
        rows = pl.ds(pl.multiple_of(g * bm, bm), bm)
        slot = g & (slots - 1)
        return pltpu.make_async_copy(ybuf.at[slot], ys_ref.at[rows], osem.at[slot])

    @pl.when(e == 0)
    def _():
        for j in range(GMM_AHEAD):
            @pl.when(j < total)
            def _():
                x_copy(j).start()

    w13b[...] = w13_ref[0].astype(BF16)
    w2b[...] = w2_ref[0].astype(BF16)

    def body(b, carry):
        g = b0 + b
        slot = g & (slots - 1)
        x_copy(g).wait()

        @pl.when(g + GMM_AHEAD < total)
        def _():
            x_copy(g + GMM_AHEAD).start()

        @pl.when(g >= slots)
        def _():
            y_copy(g - slots).wait()

        ybuf[slot] = _gmm_rows(xbuf[slot], w13b, w2b, de)
        y_copy(g).start()
        return carry

    lax.fori_loop(0, nb, body, 0)

    @pl.when(e == n_e - 1)
    def _():
        for j in range(1, slots + 1):
            @pl.when(total - j >= 0)
            def _():
                y_copy(total - j).wait()


def _gmm(xs, w13, w2, blk0, nblk, bm):
    cap, dw = xs.shape
    e, d, de2 = w13.shape
    de = w2.shape[1]
    return pl.pallas_call(
        _gmm_kernel,
        out_shape=jax.ShapeDtypeStruct((cap, dw), U32),
        grid_spec=pltpu.PrefetchScalarGridSpec(
            num_scalar_prefetch=2,
            grid=(e,),
            in_specs=[pl.BlockSpec((1, d, de2), lambda i, b0, nb: (i, 0, 0)),
                      pl.BlockSpec((1, de, d), lambda i, b0, nb: (i, 0, 0)),
                      pl.BlockSpec(memory_space=pl.ANY)],
            out_specs=pl.BlockSpec(memory_space=pl.ANY),
            scratch_shapes=[pltpu.VMEM((d, de2), BF16), pltpu.VMEM((de, d), BF16),
                            pltpu.VMEM((GMM_SLOTS, bm, dw), U32),
                            pltpu.VMEM((GMM_SLOTS, bm, dw), U32),
                            pltpu.SemaphoreType.DMA((GMM_SLOTS,)),
                            pltpu.SemaphoreType.DMA((GMM_SLOTS,))]),
        compiler_params=_cparams(("arbitrary",)),
        name="gmm",
    )(blk0, nblk, w13, w2, xs)


def _sc_gather_rows(table, idx, window):
    n_idx = idx.shape[0]
    dw = table.shape[1]
    mesh = _sc_mesh()
    n_workers = mesh.num_cores * mesh.num_subcores
    per_w = n_idx // n_workers
    n_chunks = per_w // window
    assert per_w * n_workers == n_idx and n_chunks * window == per_w and n_chunks % 2 == 0

    @functools.partial(pl.kernel, mesh=mesh,
                       out_type=jax.ShapeDtypeStruct((n_idx, dw), table.dtype),
                       scratch_types=[pltpu.VMEM((per_w,), I32),
                                      pltpu.VMEM((window, dw), table.dtype),
                                      pltpu.VMEM((window, dw), table.dtype),
                                      pltpu.SemaphoreType.DMA, pltpu.SemaphoreType.DMA,
                                      pltpu.SemaphoreType.DMA, pltpu.SemaphoreType.DMA])
    def gather(x_hbm, i_hbm, o_hbm, idx_v, rows0, rows1, gs0, gs1, os0, os1):
        wid = lax.axis_index("subcore") * mesh.num_cores + lax.axis_index("core")
        base = wid * per_w
        pltpu.sync_copy(i_hbm.at[pl.ds(base, per_w)], idx_v)
        bufs = ((rows0, gs0, os0), (rows1, gs1, os1))

        def fetch(j, slot):
            rows, gs, _ = bufs[slot]
            off = pl.multiple_of(j * window, window)
            return pltpu.async_copy(x_hbm.at[idx_v.at[pl.ds(off, window)]], rows, gs)

        def fetch_wait(slot):
            rows, gs, _ = bufs[slot]
            pltpu.make_async_copy(x_hbm.at[idx_v.at[pl.ds(0, window)]], rows, gs).wait()

        def put(j, slot):
            rows, _, osem = bufs[slot]
            off = pl.multiple_of(j * window, window)
            return pltpu.async_copy(rows, o_hbm.at[pl.ds(base + off, window)], osem)

        def put_wait(slot):
            rows, _, osem = bufs[slot]
            pltpu.make_async_copy(rows, o_hbm.at[pl.ds(base, window)], osem).wait()

        fetch(0, 0)

        @pl.loop(0, n_chunks, step=2)
        def _(j):
            @pl.when(j > 0)
            def _():
                put_wait(1)
            fetch(j + 1, 1)
            fetch_wait(0)
            put(j, 0)
            put_wait(0)

            @pl.when(j + 2 < n_chunks)
            def _():
                fetch(j + 2, 0)
            fetch_wait(1)
            put(j + 1, 1)

        put_wait(1)

    return gather(table, idx)


def _combine_kernel(gate_ref, x1_ref, h2_ref, mod_ref, gain_ref, sw13_ref, sw2_ref, yg_ref, o_ref):
    ds_ = sw2_ref.shape[0]
    hb = _unpack_rows(h2_ref[...]).astype(BF16)
    u = jnp.dot(hb, sw13_ref[...], preferred_element_type=F32)
    hmid = _silu(u[:, :ds_]) * u[:, ds_:]
    y = jnp.dot(hmid.astype(BF16), sw2_ref[...], preferred_element_type=F32)

    gate = gate_ref[...]
    moe = gate[:, 0:1] * _unpack_rows(yg_ref[0])
    for k in range(1, TOP_K):
        moe = moe + gate[:, k:k + 1] * _unpack_rows(yg_ref[k])
    xo = x1_ref[...] + mod_ref[0, 5:6, :] * (moe + y)
    ms = jnp.mean(xo * xo, axis=-1, keepdims=True)
    o_ref[...] = xo * lax.rsqrt(ms + EPS) * gain_ref[...]


def _combine(gate_tok, x1, h2, mods, gain, sw13, sw2, yg, tc, tiles_per_batch):
    n, d = x1.shape
    dw = h2.shape[1]
    nt = n // tc
    ds2 = sw13.shape[1]
    ds_ = sw2.shape[0]
    return pl.pallas_call(
        _combine_kernel,
        out_shape=jax.ShapeDtypeStruct((n, d), F32),
        grid=(nt,),
        in_specs=[pl.BlockSpec((tc, TOP_K), lambda i: (i, 0)),
                  pl.BlockSpec((tc, d), lambda i: (i, 0)),
                  pl.BlockSpec((tc, dw), lambda i: (i, 0)),
                  pl.BlockSpec((1, 8, d), lambda i: (i // tiles_per_batch, 0, 0)),
                  pl.BlockSpec((1, d), lambda i: (0, 0)),
                  pl.BlockSpec((d, ds2), lambda i: (0, 0)),
                  pl.BlockSpec((ds_, d), lambda i: (0, 0)),
                  pl.BlockSpec((TOP_K, tc, dw), lambda i: (0, i, 0))],
        out_specs=pl.BlockSpec((tc, d), lambda i: (i, 0)),
        compiler_params=_cparams(("arbitrary",)),
        name="combine",
    )(gate_tok, x1, h2, mods, gain, sw13, sw2, yg)


def _block_diag_pairs(w, per):
    nb, bd, _ = w.shape
    w = w.reshape(nb // per, per, bd, bd)
    eye = jnp.eye(per, dtype=w.dtype)
    out = jnp.einsum("gpij,pq->gpiqj", w, eye)
    return out.reshape(nb // per, per * bd, per * bd)


def kernel(x, c, ctx, c_ctx, ada_w, ada_b, norm_mix, norm_ffn, norm_final, w_in, w_out,
           lru_conv_w, lru_conv_b, lru_wa, lru_ba, lru_wx, lru_bx, lru_lambda,
           hgrn_lb_logits, hgrn_norm, router_w, router_b, exp_w13, exp_w2, shared_w13, shared_w2):
    assert ada_w.shape[0] == 1, "single-layer block"
    b, s, d = x.shape
    n = b * s
    lru_w = lru_conv_w.shape[2]
    hg_w = hgrn_norm.shape[1]
    n_experts = router_w.shape[2]

    rows = -(-(b + 1) // SUBLANES) * SUBLANES
    cs = jnp.zeros((rows, d), F32).at[:b].set(c).at[b].set(c_ctx)
    mod = _ada(cs, ada_w[0], ada_b[0][None, :]).reshape(rows, 6, d)
    mods = jnp.pad(mod, ((0, 0), (0, 2), (0, 0)))

    w_in_bf = w_in[0].astype(BF16)
    gain1 = norm_mix[0][None, :]
    projx = _inproj(x, mods, gain1, w_in_bf, 512, shared_mod=False)
    projc = _inproj(ctx, mods[b:b + 1], gain1, w_in_bf, ctx.shape[1], shared_mod=True)

    per = LRU_G // (lru_w // LRU_BLOCKS)
    wg = jnp.stack([_block_diag_pairs(lru_wa[0, 0], per), _block_diag_pairs(lru_wx[0, 0], per),
                    _block_diag_pairs(lru_wa[0, 1], per), _block_diag_pairs(lru_wx[0, 1], per)]
                   ).astype(BF16)
    bg = jnp.stack([lru_ba[0, 0], lru_bx[0, 0], lru_ba[0, 1], lru_bx[0, 1]])
    ylru = _lru(projx, projc, lru_conv_w[0], lru_conv_b[0][None, :], wg, bg, lru_lambda[0])

    yhg = _hgrn(projx, projc, hgrn_lb_logits, hgrn_norm[0][None, :], (2 * lru_w) // LANES)

    wo = w_out[0].astype(BF16).reshape(2, lru_w, d)
    x1, h2, eid_t, gate_t, rank_t, cnt = _outproj(
        ylru, yhg, x, mods, norm_ffn[0][None, :], wo,
        router_w[0].T.astype(BF16), router_b[0][:, None], 512)

    bm = 256
    counts = cnt[:, 0]
    padded = (counts + bm - 1) // bm * bm
    pends = jnp.cumsum(padded)
    pstarts = pends - padded
    pos = _pos(pstarts, eid_t, rank_t, 2048)
    cap = n * TOP_K + n_experts * bm

    xs = _padfill(_sc_scatter_rows(h2, pos, cap), pstarts + counts, padded - counts, bm)
    ys = _gmm(xs, exp_w13[0], exp_w2[0], pstarts // bm, padded // bm, bm)

    yg = _sc_gather_rows(ys, pos.reshape(-1), 64).reshape(TOP_K, n, d // 2)
    tc = 256
    out = _combine(gate_t.T, x1.reshape(n, d), h2, mods,
                   norm_final[None, :], shared_w13[0].astype(BF16), shared_w2[0].astype(BF16),
                   yg, tc, s // tc)
    return out.reshape(b, s, d)
```

```python
import functools

import jax
import jax.numpy as jnp
from jax import lax
from jax.experimental import pallas as pl
from jax.experimental.pallas import tpu as pltpu
from jax.experimental.pallas import tpu_sc as plsc

F32 = jnp.float32
BF16 = jnp.bfloat16
I32 = jnp.int32
U32 = jnp.uint32

EPS = 1e-6
LRU_C = 8.0
ROUTE_SCALE = 2.5
GRID_W = 64
HG_CHUNK = 32
N_GROUPS = 8
TOPK_GROUPS = 4
TOP_K = 8
LRU_BLOCKS = 8
HG_HEADS = 4
CONV_W = 4
CONV_PAD_L = 1

SUBLANES = 8
LANES = 128
N_SEG = SUBLANES
VMEM_LIMIT = 56 * 1024 * 1024


def _cparams(sem, vmem=VMEM_LIMIT):
    return pltpu.CompilerParams(dimension_semantics=sem, vmem_limit_bytes=vmem)


def _sigmoid(x):
    return jax.nn.sigmoid(x)


def _sigmoid_tanh(x):
    return 0.5 * jnp.tanh(0.5 * x) + 0.5


def _silu(x):
    return x * _sigmoid(x)


def _pack_rows(x):
    w = x.shape[1] // 2
    bits = pltpu.bitcast(x.astype(BF16).astype(F32), U32)
    return (bits[:, :w] >> 16) | (bits[:, w:] & jnp.uint32(0xFFFF0000))


def _unpack_rows(p):
    lo = pltpu.bitcast(p << 16, F32)
    hi = pltpu.bitcast(p & jnp.uint32(0xFFFF0000), F32)
    return jnp.concatenate([lo, hi], axis=1)


def _gelu_tanh(x):
    c = 0.7978845608028654
    return 0.5 * x * (1.0 + jnp.tanh(c * (x + 0.044715 * (x * x * x))))


def _ada_kernel(c_ref, w_ref, b_ref, o_ref):
    s = _silu(c_ref[...])
    o_ref[...] = jnp.dot(s, w_ref[...], preferred_element_type=F32,
                         precision=lax.Precision.HIGHEST) + b_ref[...]


def _ada(cs, w, b):
    rows, d = cs.shape
    n = w.shape[1]
    bn = 1024
    return pl.pallas_call(
        _ada_kernel,
        out_shape=jax.ShapeDtypeStruct((rows, n), F32),
        grid=(n // bn,),
        in_specs=[pl.BlockSpec((rows, d), lambda j: (0, 0)),
                  pl.BlockSpec((d, bn), lambda j: (0, j)),
                  pl.BlockSpec((1, bn), lambda j: (0, j))],
        out_specs=pl.BlockSpec((rows, bn), lambda j: (0, j)),
        compiler_params=_cparams(("arbitrary",)),
        name="ada",
    )(cs, w, b)


def _inproj_kernel(x_ref, mod_ref, gain_ref, w_ref, o_ref):
    x = x_ref[0]
    ms = jnp.mean(x * x, axis=-1, keepdims=True)
    y = x * lax.rsqrt(ms + EPS) * gain_ref[...]
    h = y * (1.0 + mod_ref[0, 1:2, :]) + mod_ref[0, 0:1, :]
    o_ref[0] = jnp.dot(h.astype(BF16), w_ref[...], preferred_element_type=F32)


def _inproj(x, mods, gain, w_bf, tm, shared_mod):
    b, s, d = x.shape
    n = w_bf.shape[1]
    mod_map = (lambda bi, i: (0, 0, 0)) if shared_mod else (lambda bi, i: (bi, 0, 0))
    return pl.pallas_call(
        _inproj_kernel,
        out_shape=jax.ShapeDtypeStruct((b, s, n), F32),
        grid=(b, s // tm),
        in_specs=[pl.BlockSpec((1, tm, d), lambda bi, i: (bi, i, 0)),
                  pl.BlockSpec((1, 8, d), mod_map),
                  pl.BlockSpec((1, d), lambda bi, i: (0, 0)),
                  pl.BlockSpec((d, n), lambda bi, i: (0, 0))],
        out_specs=pl.BlockSpec((1, tm, n), lambda bi, i: (bi, i, 0)),
        compiler_params=_cparams(("arbitrary", "arbitrary")),
        name="inproj",
    )(x, mods, gain, w_bf)


LRU_G = 256
LRU_CHUNK = 256


def _seg_pitch(seg_len):
    return seg_len + SUBLANES


def _lane_store(ref, d, rows, val):
    nl = val.shape[-1] // LANES
    for l in range(nl):
        ref[d * nl + l, rows, :] = val[:, l * LANES:(l + 1) * LANES]


def _lane_load(ref, d, rows, nl):
    return jnp.concatenate([ref[d * nl + l, rows, :] for l in range(nl)], axis=-1)


def _lru_coeffs(pad_ref, t0, rows, cw, cb, wg_ref, bg, sp, a_ref, b_ref, dst0):
    win = pad_ref[pl.ds(t0, rows + 2 * SUBLANES), :]
    u = cb
    for k in range(CONV_W):
        off = SUBLANES - CONV_PAD_L + k
        u = u + win[off:off + rows, :] * cw[k:k + 1, :]
    ub = u.astype(BF16)
    for d in range(2):
        r = _sigmoid_tanh(jnp.dot(ub, wg_ref[2 * d, 0], preferred_element_type=F32)
                          + bg[2 * d:2 * d + 1, :])
        ig = _sigmoid_tanh(jnp.dot(ub, wg_ref[2 * d + 1, 0], preferred_element_type=F32)
                           + bg[2 * d + 1:2 * d + 2, :])
        log_a = (-LRU_C) * r * sp[d:d + 1, :]
        a = jnp.exp(log_a)
        one_minus_a2 = -jnp.tanh(log_a) * (1.0 + a * a)
        bb = jnp.sqrt(one_minus_a2) * (ig * u)
        _lane_store(a_ref, d, pl.ds(dst0, rows), a)
        _lane_store(b_ref, d, pl.ds(dst0, rows), bb)


def _seg_scan(a_ref, b_ref, h_ref, p_ref, seg_len, pitch, unroll):
    n_lead = a_ref.shape[0]
    nl = n_lead // 2
    zero = jnp.zeros((N_SEG, LANES), F32)
    one = jnp.ones((N_SEG, LANES), F32)
    init = tuple((zero, one) for _ in range(n_lead))

    def step(t, carry):
        out = []
        for i in range(n_lead):
            h, p = carry[i]
            tt = t if i < nl else seg_len - 1 - t
            rows = pl.ds(tt, N_SEG, stride=pitch)
            a = a_ref[i, rows, :]
            h = a * h + b_ref[i, rows, :]
            p = p * a
            if h_ref is not None:
                h_ref[i, rows, :] = h
                p_ref[i, rows, :] = p
            out.append((h, p))
        return tuple(out)

    def body(i, carry):
        for j in range(unroll):
            carry = step(i * unroll + j, carry)
        return carry

    ends = lax.fori_loop(0, seg_len // unroll, body, init)
    h_end = [jnp.concatenate([ends[d * nl + l][0] for l in range(nl)], axis=-1) for d in range(2)]
    p_end = [jnp.concatenate([ends[d * nl + l][1] for l in range(nl)], axis=-1) for d in range(2)]
    return h_end, p_end


def _seg_carries(h_end, p_end, h0, reverse):
    order = range(N_SEG - 1, -1, -1) if reverse else range(N_SEG)
    cin = [None] * N_SEG
    c = h0
    for s in order:
        cin[s] = c
        c = p_end[s:s + 1, :] * c + h_end[s:s + 1, :]
    return cin, c


def _lru_kernel(rx_ref, rg_ref, rc_ref, cw_ref, cb_ref, wg_ref, bg_ref, lam_ref, o_ref,
                padl, padc, a_l, b_l, h_l, p_l, a_c, b_c):
    s_len = rx_ref.shape[1]
    c_len = rc_ref.shape[1]
    g = rx_ref.shape[2]
    nl = g // LANES
    seg_l = s_len // N_SEG
    seg_c = c_len // N_SEG
    pitch_l = _seg_pitch(seg_l)
    pitch_c = _seg_pitch(seg_c)

    zeros = jnp.zeros((SUBLANES, g), F32)
    padl[0:SUBLANES, :] = zeros
    padl[SUBLANES + s_len:2 * SUBLANES + s_len, :] = zeros
    padc[0:SUBLANES, :] = zeros
    padc[SUBLANES + c_len:2 * SUBLANES + c_len, :] = zeros
    for i in range(s_len // LRU_CHUNK):
        padl[SUBLANES + i * LRU_CHUNK:SUBLANES + (i + 1) * LRU_CHUNK, :] = (
            rx_ref[0, i * LRU_CHUNK:(i + 1) * LRU_CHUNK, :])
    padc[SUBLANES:SUBLANES + c_len, :] = rc_ref[0]

    cw = cw_ref[...]
    cb = cb_ref[...]
    bg = bg_ref[...]
    x = -lam_ref[...]
    sp = jnp.maximum(x, 0.0) + jnp.log(1.0 + jnp.exp(-jnp.abs(x)))

    _lru_coeffs(padc, 0, c_len, cw, cb, wg_ref, bg, sp, a_l, b_l, 0)
    for i in range(2 * nl):
        for s in range(N_SEG):
            a_c[i, s * pitch_c:s * pitch_c + seg_c, :] = a_l[i, s * seg_c:(s + 1) * seg_c, :]
            b_c[i, s * pitch_c:s * pitch_c + seg_c, :] = b_l[i, s * seg_c:(s + 1) * seg_c, :]
    h_end, p_end = _seg_scan(a_c, b_c, None, None, seg_c, pitch_c, unroll=4)
    zero_row = jnp.zeros((1, g), F32)
    _, h0_f = _seg_carries(h_end[0], p_end[0], zero_row, reverse=False)
    _, h0_b = _seg_carries(h_end[1], p_end[1], zero_row, reverse=True)

    def coeff_body(s, carry):
        t0 = pl.multiple_of(s * seg_l, SUBLANES)
        dst = pl.multiple_of(s * pitch_l, SUBLANES)
        _lru_coeffs(padl, t0, seg_l, cw, cb, wg_ref, bg, sp, a_l, b_l, dst)
        return carry

    lax.fori_loop(0, N_SEG, coeff_body, 0)
    h_end, p_end = _seg_scan(a_l, b_l, h_l, p_l, seg_l, pitch_l, unroll=4)
    cin_f, _ = _seg_carries(h_end[0], p_end[0], h0_f, reverse=False)
    cin_b, _ = _seg_carries(h_end[1], p_end[1], h0_b, reverse=True)

    for s in range(N_SEG):
        rows = slice(s * pitch_l, s * pitch_l + seg_l)
        h = (_lane_load(h_l, 0, rows, nl) + _lane_load(p_l, 0, rows, nl) * cin_f[s]) + (
            _lane_load(h_l, 1, rows, nl) + _lane_load(p_l, 1, rows, nl) * cin_b[s])
        o_ref[0, s * seg_l:(s + 1) * seg_l, :] = _gelu_tanh(rg_ref[0, s * seg_l:(s + 1) * seg_l, :]) * h


def _lru(projx, projc, cw, cb, wg, bg, lam):
    b, s_len, _ = projx.shape
    c_len = projc.shape[1]
    w = cw.shape[1]
    g = LRU_G
    ng = w // g
    seg_l = s_len // N_SEG
    seg_c = c_len // N_SEG
    assert seg_l == LRU_CHUNK and c_len % (N_SEG * 1) == 0
    return pl.pallas_call(
        _lru_kernel,
        out_shape=jax.ShapeDtypeStruct((b, s_len, w), F32),
        grid=(b, ng),
        in_specs=[pl.BlockSpec((1, s_len, g), lambda bi, j: (bi, 0, j)),
                  pl.BlockSpec((1, s_len, g), lambda bi, j: (bi, 0, ng + j)),
                  pl.BlockSpec((1, c_len, g), lambda bi, j: (bi, 0, j)),
                  pl.BlockSpec((CONV_W, g), lambda bi, j: (0, j)),
                  pl.BlockSpec((1, g), lambda bi, j: (0, j)),
                  pl.BlockSpec((4, 1, g, g), lambda bi, j: (0, j, 0, 0)),
                  pl.BlockSpec((4, g), lambda bi, j: (0, j)),
                  pl.BlockSpec((2, g), lambda bi, j: (0, j))],
        out_specs=pl.BlockSpec((1, s_len, g), lambda bi, j: (bi, 0, j)),
        scratch_shapes=[pltpu.VMEM((s_len + 2 * SUBLANES, g), F32),
                        pltpu.VMEM((c_len + 2 * SUBLANES, g), F32),
                        pltpu.VMEM((2 * g // LANES, N_SEG * _seg_pitch(seg_l), LANES), F32),
                        pltpu.VMEM((2 * g // LANES, N_SEG * _seg_pitch(seg_l), LANES), F32),
                        pltpu.VMEM((2 * g // LANES, N_SEG * _seg_pitch(seg_l), LANES), F32),
                        pltpu.VMEM((2 * g // LANES, N_SEG * _seg_pitch(seg_l), LANES), F32),
                        pltpu.VMEM((2 * g // LANES, N_SEG * _seg_pitch(seg_c), LANES), F32),
                        pltpu.VMEM((2 * g // LANES, N_SEG * _seg_pitch(seg_c), LANES), F32)],
        compiler_params=_cparams(("arbitrary", "arbitrary")),
        name="lru",
    )(projx, projx, projc, cw, cb, wg, bg, lam)


HG_PITCH_L = GRID_W + SUBLANES
HG_PITCH_C = SUBLANES


def _hg_gates(z, lb):
    e = jnp.exp(-jnp.abs(z))
    s = 1.0 / (1.0 + e)
    es = e * s
    pos = z >= 0.0
    sig = jnp.where(pos, s, es)
    nsig = jnp.where(pos, es, s)
    logf = jnp.log(lb + (1.0 - lb) * sig)
    k = (1.0 - lb) * nsig
    return logf, k


def _hg_prepass(load, width, pitch, lb, qd, kd, kl, vs, dec):
    n_pos = HG_CHUNK
    for d, zname in enumerate(("zf", "zb")):
        order = range(n_pos) if d == 0 else range(n_pos - 1, -1, -1)
        g = jnp.zeros((width, LANES), F32)
        lbd = lb[d:d + 1, :]
        for p in order:
            logf, k = _hg_gates(load(zname, p), lbd)
            g = g + logf
            qd[d, p * pitch:p * pitch + width, :] = g
            kl[d, p * pitch:p * pitch + width, :] = k
        g_last = g
        dec[d, 0:width, :] = jnp.exp(g_last)
        for p in range(n_pos):
            rows = slice(p * pitch, p * pitch + width)
            gp = qd[d, rows, :]
            k = kl[d, rows, :]
            qd[d, rows, :] = load("q", p) * jnp.exp(gp)
            kd[d, rows, :] = k * jnp.exp(-gp)
            kl[d, rows, :] = k * jnp.exp(g_last - gp)
    for p in range(n_pos):
        vs[p * pitch:p * pitch + width, :] = load("v", p)


HG_GROUP = 4
HG_UNROLL = 8
HG_INTRA_UNROLL = 4
_NT = (((1,), (1,)), ((), ()))
_TN = (((0,), (0,)), ((), ()))


def _hg_group_rows(ref, lead, c0, pitch):
    parts = []
    for j in range(HG_GROUP):
        rows = pl.ds(c0 + j, HG_CHUNK, stride=pitch)
        parts.append(ref[rows, :] if lead is None else ref[lead, rows, :])
    return jnp.concatenate(parts, axis=0)


def _hg_scores(g, pitch, qd, kd, masks, sc_ref):
    c0 = g * HG_GROUP
    total = None
    for d in range(2):
        q = _hg_group_rows(qd, d, c0, pitch).astype(BF16)
        k = _hg_group_rows(kd, d, c0, pitch).astype(BF16)
        sc = jnp.where(masks[d], lax.dot_general(q, k, _NT, preferred_element_type=F32), 0.0)
        total = sc if total is None else total + sc
    sc_ref[g] = total.astype(BF16)


def _hg_summaries(g, pitch, kl, vs, sc_ref, oi, kv):
    c0 = g * HG_GROUP
    v = _hg_group_rows(vs, None, c0, pitch).astype(BF16)
    if oi is not None:
        o = jnp.dot(sc_ref[g], v, preferred_element_type=F32)
        for j in range(HG_GROUP):
            oi[pl.ds(c0 + j, HG_CHUNK, stride=pitch), :] = o[j * HG_CHUNK:(j + 1) * HG_CHUNK, :]
    kls = jnp.concatenate([_hg_group_rows(kl, 0, c0, pitch), _hg_group_rows(kl, 1, c0, pitch)],
                          axis=1).astype(BF16)
    for j in range(HG_GROUP):
        sl = slice(j * HG_CHUNK, (j + 1) * HG_CHUNK)
        both = lax.dot_general(v[sl], kls[sl], _TN, preferred_element_type=F32)
        kv[0, c0 + j] = both[:, :LANES]
        kv[1, c0 + j] = both[:, LANES:]


def _hg_steps(step0, n_steps, n_chunks, pitch, qd, dec, kv, states, ox):
    states = list(states)
    for u in range(n_steps):
        for d in range(2):
            c = step0 + u if d == 0 else n_chunks - 1 - (step0 + u)
            s_t = states[d]
            if ox is not None:
                rows = pl.ds(c, HG_CHUNK, stride=pitch)
                q = qd[d, rows, :].astype(BF16)
                ox[d, rows, :] = lax.dot_general(q, s_t.astype(BF16), _NT,
                                                 preferred_element_type=F32)
            states[d] = s_t * dec[d, pl.ds(c, 1), :] + kv[d, c]
    return tuple(states)


def _hgrn_kernel(q_ref, v_ref, zf_ref, zb_ref, g_ref, qc_ref, vc_ref, zfc_ref, zbc_ref,
                 lbl_ref, gain_ref, o_ref, qd, kd, kl, vs, dec, oi, ox, kv, sc):
    s_len = q_ref.shape[1]
    c_len = qc_ref.shape[1]
    n_col = s_len // HG_CHUNK
    n_cc = c_len // HG_CHUNK
    assert n_col == GRID_W and n_cc == HG_PITCH_C
    assert n_col % HG_GROUP == 0 and n_cc % HG_GROUP == 0 and n_col % HG_UNROLL == 0

    lg = lbl_ref[...]
    m = jnp.max(lg, axis=1, keepdims=True)
    ex = jnp.exp(lg - m)
    lb = ex[:, 0, :] / jnp.sum(ex, axis=1)

    gr = HG_GROUP * HG_CHUNK
    ii = lax.broadcasted_iota(I32, (gr, gr), 0)
    jj = lax.broadcasted_iota(I32, (gr, gr), 1)
    same = (ii // HG_CHUNK) == (jj // HG_CHUNK)
    masks = (jnp.logical_and(same, jj <= ii), jnp.logical_and(same, jj >= ii))

    c_refs = {"q": qc_ref, "v": vc_ref, "zf": zfc_ref, "zb": zbc_ref}

    def load_c(name, p):
        return c_refs[name][0, pl.ds(p, n_cc, stride=HG_CHUNK), :]

    _hg_prepass(load_c, n_cc, HG_PITCH_C, lb, qd, kd, kl, vs, dec)
    for g in range(n_cc // HG_GROUP):
        _hg_summaries(g, HG_PITCH_C, kl, vs, None, None, kv)
    zero = jnp.zeros((LANES, LANES), F32)
    states = _hg_steps(0, n_cc, n_cc, HG_PITCH_C, qd, dec, kv, (zero, zero), None)

    l_refs = {"q": q_ref, "v": v_ref, "zf": zf_ref, "zb": zb_ref}

    def load_l(name, p):
        return l_refs[name][0, p * n_col:(p + 1) * n_col, :]

    _hg_prepass(load_l, n_col, HG_PITCH_L, lb, qd, kd, kl, vs, dec)

    n_it = n_col // (HG_GROUP * HG_INTRA_UNROLL)

    def scores_body(i, carry):
        for u in range(HG_INTRA_UNROLL):
            _hg_scores(i * HG_INTRA_UNROLL + u, HG_PITCH_L, qd, kd, masks, sc)
        return carry

    lax.fori_loop(0, n_it, scores_body, 0)

    def summaries_body(i, carry):
        for u in range(HG_INTRA_UNROLL):
            _hg_summaries(i * HG_INTRA_UNROLL + u, HG_PITCH_L, kl, vs, sc, oi, kv)
        return carry

    lax.fori_loop(0, n_it, summaries_body, 0)

    def step_body(i, carry):
        return _hg_steps(i * HG_UNROLL, HG_UNROLL, n_col, HG_PITCH_L, qd, dec, kv, carry, ox)

    lax.fori_loop(0, n_col // HG_UNROLL, step_body, states)

    gain = gain_ref[...]
    for r in range(HG_CHUNK):
        rows = slice(r * HG_PITCH_L, r * HG_PITCH_L + n_col)
        o = oi[rows, :] + (ox[0, rows, :] + ox[1, rows, :])
        y = o * lax.rsqrt(jnp.mean(o * o, axis=-1, keepdims=True) + EPS) * gain
        o_ref[0, r * n_col:(r + 1) * n_col, :] = y * _silu(g_ref[0, r * n_col:(r + 1) * n_col, :])


def _hgrn(projx, projc, lbl, gain, col0):
    b, s_len, _ = projx.shape
    c_len = projc.shape[1]
    hw = gain.shape[1]
    nh = hw // LANES
    n_slots = lbl.shape[1]

    def xs(k):
        return pl.BlockSpec((1, s_len, LANES), lambda bi, h, k=k: (bi, 0, col0 + k * nh + h))

    def cs(k):
        return pl.BlockSpec((1, c_len, LANES), lambda bi, h, k=k: (bi, 0, col0 + k * nh + h))

    sc_rows = HG_CHUNK * HG_PITCH_L
    return pl.pallas_call(
        _hgrn_kernel,
        out_shape=jax.ShapeDtypeStruct((b, s_len, hw), F32),
        grid=(b, nh),
        in_specs=[xs(0), xs(1), xs(2), xs(3), xs(4), cs(0), cs(1), cs(2), cs(3),
                  pl.BlockSpec((2, n_slots, LANES), lambda bi, h: (0, 0, h)),
                  pl.BlockSpec((1, LANES), lambda bi, h: (0, h))],
        out_specs=pl.BlockSpec((1, s_len, LANES), lambda bi, h: (bi, 0, h)),
        scratch_shapes=[pltpu.VMEM((2, sc_rows, LANES), F32),
                        pltpu.VMEM((2, sc_rows, LANES), F32),
                        pltpu.VMEM((2, sc_rows, LANES), F32),
                        pltpu.VMEM((sc_rows, LANES), F32),
                        pltpu.VMEM((2, HG_PITCH_L, LANES), F32),
                        pltpu.VMEM((sc_rows, LANES), F32),
                        pltpu.VMEM((2, sc_rows, LANES), F32),
                        pltpu.VMEM((2, GRID_W, LANES, LANES), F32),
                        pltpu.VMEM((GRID_W // HG_GROUP, HG_GROUP * HG_CHUNK,
                                    HG_GROUP * HG_CHUNK), BF16)],
        compiler_params=_cparams(("arbitrary", "arbitrary")),
        name="hgrn",
    )(projx, projx, projx, projx, projx, projc, projc, projc, projc, lbl, gain)


def _first_index_of_max(vals, iota, big):
    m = jnp.max(vals, axis=0, keepdims=True)
    idx = jnp.min(jnp.where(vals == m, iota, big), axis=0, keepdims=True)
    return m, idx


def _route(logits_t, rb, n_experts):
    t = logits_t.shape[1]
    gsz = n_experts // N_GROUPS
    neg = -jnp.inf
    scores = _sigmoid(logits_t)
    biased = scores + rb
    iota_g = lax.broadcasted_iota(I32, (gsz, t), 0)
    gscore = []
    for gi in range(N_GROUPS):
        blk = biased[gi * gsz:(gi + 1) * gsz, :]
        m1, i1 = _first_index_of_max(blk, iota_g, gsz)
        m2 = jnp.max(jnp.where(iota_g == i1, neg, blk), axis=0, keepdims=True)
        gscore.append(m1 + m2)
    gs = jnp.concatenate(gscore, axis=0)
    iota_n = lax.broadcasted_iota(I32, (N_GROUPS, t), 0)
    gsel = jnp.zeros((N_GROUPS, t), jnp.bool_)
    for _ in range(TOPK_GROUPS):
        _, gi1 = _first_index_of_max(gs, iota_n, N_GROUPS)
        hit = iota_n == gi1
        gsel = jnp.logical_or(gsel, hit)
        gs = jnp.where(hit, neg, gs)
    emask = jnp.concatenate(
        [jnp.broadcast_to(gsel[gi:gi + 1, :], (gsz, t)) for gi in range(N_GROUPS)], axis=0)
    masked = jnp.where(emask, biased, neg)
    iota_e = lax.broadcasted_iota(I32, (n_experts, t), 0)
    ids, gates = [], []
    sel = jnp.zeros((n_experts, t), F32)
    for _ in range(TOP_K):
        _, ei = _first_index_of_max(masked, iota_e, n_experts)
        hit = iota_e == ei
        ids.append(ei)
        gates.append(jnp.sum(jnp.where(hit, scores, 0.0), axis=0, keepdims=True))
        sel = jnp.where(hit, 1.0, sel)
        masked = jnp.where(hit, neg, masked)
    ids = jnp.concatenate(ids, axis=0)
    gates = jnp.concatenate(gates, axis=0)
    gates = gates / jnp.sum(gates, axis=0, keepdims=True) * ROUTE_SCALE
    return ids, gates, sel


def _outproj_kernel(yl_ref, yh_ref, x_ref, mod_ref, gain_ref, wo_ref, rwt_ref, rb_ref,
                    x1_ref, h2_ref, eid_ref, gate_ref, rank_ref, cnt_ref, carry):
    first = jnp.logical_and(pl.program_id(0) == 0, pl.program_id(1) == 0)

    @pl.when(first)
    def _():
        carry[...] = jnp.zeros(carry.shape, F32)

    y = jnp.dot(yl_ref[0].astype(BF16), wo_ref[0], preferred_element_type=F32)
    y = y + jnp.dot(yh_ref[0].astype(BF16), wo_ref[1], preferred_element_type=F32)
    x1 = x_ref[0] + mod_ref[0, 2:3, :] * y
    x1_ref[0] = x1
    ms = jnp.mean(x1 * x1, axis=-1, keepdims=True)
    h2 = x1 * lax.rsqrt(ms + EPS) * gain_ref[...]
    h2 = h2 * (1.0 + mod_ref[0, 4:5, :]) + mod_ref[0, 3:4, :]
    h2_ref[...] = _pack_rows(h2)

    n_experts = rwt_ref.shape[0]
    t = h2.shape[0]
    logits_t = lax.dot_general(rwt_ref[...], h2.astype(BF16), _NT,
                               preferred_element_type=F32)
    ids, gates, sel = _route(logits_t, rb_ref[...], n_experts)
    eid_ref[...] = ids
    gate_ref[...] = gates

    ti = lax.broadcasted_iota(I32, (t, t), 0)
    tj = lax.broadcasted_iota(I32, (t, t), 1)
    upper = (ti < tj).astype(BF16)
    selb = sel.astype(BF16)
    before = jnp.dot(selb, upper, preferred_element_type=F32) + carry[:, 0:1]
    iota_e = lax.broadcasted_iota(I32, (n_experts, t), 0)
    ranks = [jnp.sum(jnp.where(iota_e == ids[k:k + 1, :], before, 0.0), axis=0, keepdims=True)
             for k in range(TOP_K)]
    rank_ref[...] = jnp.concatenate(ranks, axis=0).astype(I32)
    total = carry[...] + jnp.dot(selb, jnp.ones((t, LANES), BF16), preferred_element_type=F32)
    carry[...] = total
    cnt_ref[...] = total.astype(I32)


def _outproj(ylru, yhg, x, mods, gain, wo, rwt, rb, tm):
    b, s, d = x.shape
    hw = ylru.shape[2]
    e = rwt.shape[0]
    n = b * s
    nt = s // tm
    tok = lambda bi, i: (0, bi * nt + i)
    return pl.pallas_call(
        _outproj_kernel,
        out_shape=(jax.ShapeDtypeStruct((b, s, d), F32),
                   jax.ShapeDtypeStruct((n, d // 2), U32),
                   jax.ShapeDtypeStruct((TOP_K, n), I32),
                   jax.ShapeDtypeStruct((TOP_K, n), F32),
                   jax.ShapeDtypeStruct((TOP_K, n), I32),
                   jax.ShapeDtypeStruct((e, LANES), I32)),
        grid=(b, nt),
        in_specs=[pl.BlockSpec((1, tm, hw), lambda bi, i: (bi, i, 0)),
                  pl.BlockSpec((1, tm, hw), lambda bi, i: (bi, i, 0)),
                  pl.BlockSpec((1, tm, d), lambda bi, i: (bi, i, 0)),
                  pl.BlockSpec((1, 8, d), lambda bi, i: (bi, 0, 0)),
                  pl.BlockSpec((1, d), lambda bi, i: (0, 0)),
                  pl.BlockSpec((2, hw, d), lambda bi, i: (0, 0, 0)),
                  pl.BlockSpec((e, d), lambda bi, i: (0, 0)),
                  pl.BlockSpec((e, 1), lambda bi, i: (0, 0))],
        out_specs=(pl.BlockSpec((1, tm, d), lambda bi, i: (bi, i, 0)),
                   pl.BlockSpec((tm, d // 2), lambda bi, i: (bi * nt + i, 0)),
                   pl.BlockSpec((TOP_K, tm), tok),
                   pl.BlockSpec((TOP_K, tm), tok),
                   pl.BlockSpec((TOP_K, tm), tok),
                   pl.BlockSpec((e, LANES), lambda bi, i: (0, 0))),
        scratch_shapes=[pltpu.VMEM((e, LANES), F32)],
        compiler_params=_cparams(("arbitrary", "arbitrary")),
        name="outproj",
    )(ylru, yhg, x, mods, gain, wo, rwt, rb)


def _pos_kernel(pstart, eid_ref, rank_ref, pos_ref):
    eid = eid_ref[...]

    def body(e, acc):
        return jnp.where(eid == e, pstart[e], acc)

    base = lax.fori_loop(0, pstart.shape[0], body, jnp.zeros(eid.shape, I32))
    pos_ref[...] = base + rank_ref[...]


def _pos(pstarts, eid_t, rank_t, tn):
    k, n = eid_t.shape
    spec = pl.BlockSpec((k, tn), lambda i, ps: (0, i))
    return pl.pallas_call(
        _pos_kernel,
        out_shape=jax.ShapeDtypeStruct((k, n), I32),
        grid_spec=pltpu.PrefetchScalarGridSpec(
            num_scalar_prefetch=1, grid=(n // tn,), in_specs=[spec, spec], out_specs=spec),
        compiler_params=_cparams(("arbitrary",)),
        name="pos",
    )(pstarts, eid_t, rank_t)


def _pad_fill(padstart, padlen, zbuf, xs_ref, sem, wait):
    nbits = zbuf.shape[0].bit_length() - 1
    low_bits = SUBLANES.bit_length() - 1

    def go(cp):
        if wait:
            cp.wait()
        else:
            cp.start()

    def body(e, carry):
        st = padstart[e]
        ln = padlen[e]
        end = st + ln
        for bit in range(nbits - 1, low_bits - 1, -1):
            size = 1 << bit
            back = ((ln >> (bit + 1)) << (bit + 1)) + size

            @pl.when((ln & size) != 0)
            def _():
                start = pl.multiple_of(end - back, SUBLANES)
                go(pltpu.make_async_copy(zbuf.at[pl.ds(0, size)],
                                         xs_ref.at[pl.ds(start, size)], sem))
        for j in range(SUBLANES - 1):
            @pl.when(j < (ln & (SUBLANES - 1)))
            def _():
                go(pltpu.make_async_copy(zbuf.at[pl.ds(0, 1)], xs_ref.at[pl.ds(st + j, 1)], sem))
        return carry

    lax.fori_loop(0, padstart.shape[0], body, 0)


def _padfill_kernel(padstart, padlen, xs_in, xs_ref, zbuf, sem):
    del xs_in
    zbuf[...] = jnp.zeros(zbuf.shape, zbuf.dtype)
    _pad_fill(padstart, padlen, zbuf, xs_ref, sem, wait=False)
    _pad_fill(padstart, padlen, zbuf, xs_ref, sem, wait=True)


def _padfill(xs, padstart, padlen, bm):
    return pl.pallas_call(
        _padfill_kernel,
        out_shape=jax.ShapeDtypeStruct(xs.shape, xs.dtype),
        grid_spec=pltpu.PrefetchScalarGridSpec(
            num_scalar_prefetch=2,
            grid=(1,),
            in_specs=[pl.BlockSpec(memory_space=pl.ANY)],
            out_specs=pl.BlockSpec(memory_space=pl.ANY),
            scratch_shapes=[pltpu.VMEM((bm, xs.shape[1]), xs.dtype), pltpu.SemaphoreType.DMA(())]),
        input_output_aliases={2: 0},
        compiler_params=_cparams(("arbitrary",)),
        name="padfill",
    )(padstart, padlen, xs)


SC_SCATTER_WINDOW = 128


def _sc_mesh():
    return plsc.VectorSubcoreMesh(core_axis_name="core", subcore_axis_name="subcore")


def _sc_scatter_rows(rows, pos, cap):
    n, dw = rows.shape
    top_k = pos.shape[0]
    mesh = _sc_mesh()
    n_workers = mesh.num_cores * mesh.num_subcores
    win = SC_SCATTER_WINDOW
    per_w = n // n_workers
    n_chunks = per_w // win
    assert per_w * n_workers == n and n_chunks * win == per_w
    pos_w = pos.reshape(top_k, n_workers, n_chunks, win).transpose(1, 2, 0, 3)
    pos_w = pos_w.reshape(n_workers, n_chunks * top_k, win)

    @functools.partial(pl.kernel, mesh=mesh,
                       out_type=jax.ShapeDtypeStruct((cap, dw), rows.dtype),
                       scratch_types=[pltpu.VMEM((n_chunks * top_k, win), I32),
                                      pltpu.VMEM((win, dw), rows.dtype),
                                      pltpu.SemaphoreType.DMA])
    def scatter(r_hbm, p_hbm, o_hbm, idx_v, rows_v, sem):
        wid = lax.axis_index("subcore") * mesh.num_cores + lax.axis_index("core")
        pltpu.sync_copy(p_hbm.at[wid], idx_v)

        @pl.loop(0, n_chunks)
        def _(c):
            pltpu.sync_copy(r_hbm.at[pl.ds(wid * per_w + c * win, win)], rows_v)
            copies = [pltpu.async_copy(rows_v, o_hbm.at[idx_v.at[c * top_k + k]], sem)
                      for k in range(top_k)]
            for cp in copies:
                cp.wait()

    return scatter(rows, pos_w)


GMM_PAIR = 4
GMM_SLOTS = 16
GMM_AHEAD = GMM_SLOTS - GMM_PAIR


def _gmm_rows(xw, w13b, w2b, de):
    xb = _unpack_rows(xw).astype(BF16)
    u = jnp.dot(xb, w13b[...], preferred_element_type=F32)
    hmid = _silu(u[:, :de]) * u[:, de:]
    return _pack_rows(jnp.dot(hmid.astype(BF16), w2b[...], preferred_element_type=F32))


def _gmm_kernel(blk0, nblk, w13_ref, w2_ref, xs_ref, ys_ref, w13b, w2b, xbuf, ybuf, isem, osem):
    e = pl.program_id(0)
    n_e = pl.num_programs(0)
    slots, bm = xbuf.shape[0], xbuf.shape[1]
    de = w2_ref.shape[1]
    nb = nblk[e]
    b0 = blk0[e]
    total = blk0[n_e - 1] + nblk[n_e - 1]

    def x_copy(g):
        rows = pl.ds(pl.multiple_of(g * bm, bm), bm)
        slot = g & (slots - 1)
        return pltpu.make_async_copy(xs_ref.at[rows], xbuf.at[slot], isem.at[slot])

    def y_copy(g):
        rows = pl.ds(pl.multiple_of(g * bm, bm), bm)
        slot = g & (slots - 1)
        return pltpu.make_async_copy(ybuf.at[slot], ys_ref.at[rows], osem.at[slot])

    @pl.when(e == 0)
    def _():
        for j in range(GMM_AHEAD):
            @pl.when(j < total)
            def _():
                x_copy(j).start()

    w13b[...] = w13_ref[0].astype(BF16)
    w2b[...] = w2_ref[0].astype(BF16)

    def run_blocks(g0, count):
        gs = [g0 + j for j in range(count)]
        for g in gs:
            x_copy(g).wait()
        for g in gs:
            @pl.when(g + GMM_AHEAD < total)
            def _():
                x_copy(g + GMM_AHEAD).start()

            @pl.when(g >= slots)
            def _():
                y_copy(g - slots).wait()
        for g in gs:
            slot = g & (slots - 1)
            ybuf[slot] = _gmm_rows(xbuf[slot], w13b, w2b, de)
        for g in gs:
            y_copy(g).start()

    def pair_body(i, carry):
        run_blocks(b0 + GMM_PAIR * i, GMM_PAIR)
        return carry

    lax.fori_loop(0, nb // GMM_PAIR, pair_body, 0)
    for r in range(1, GMM_PAIR):
        @pl.when(nb % GMM_PAIR == r)
        def _():
            run_blocks(b0 + nb - r, r)

    @pl.when(e == n_e - 1)
    def _():
        for j in range(1, slots + 1):
            @pl.when(total - j >= 0)
            def _():
                y_copy(total - j).wait()


def _gmm(xs, w13, w2, blk0, nblk, bm):
    cap, dw = xs.shape
    e, d, de2 = w13.shape
    de = w2.shape[1]
    return pl.pallas_call(
        _gmm_kernel,
        out_shape=jax.ShapeDtypeStruct((cap, dw), U32),
        grid_spec=pltpu.PrefetchScalarGridSpec(
            num_scalar_prefetch=2,
            grid=(e,),
            in_specs=[pl.BlockSpec((1, d, de2), lambda i, b0, nb: (i, 0, 0)),
                      pl.BlockSpec((1, de, d), lambda i, b0, nb: (i, 0, 0)),
                      pl.BlockSpec(memory_space=pl.ANY)],
            out_specs=pl.BlockSpec(memory_space=pl.ANY),
            scratch_shapes=[pltpu.VMEM((d, de2), BF16), pltpu.VMEM((de, d), BF16),
                            pltpu.VMEM((GMM_SLOTS, bm, dw), U32),
                            pltpu.VMEM((GMM_SLOTS, bm, dw), U32),
                            pltpu.SemaphoreType.DMA((GMM_SLOTS,)),
                            pltpu.SemaphoreType.DMA((GMM_SLOTS,))]),
        compiler_params=_cparams(("arbitrary",)),
        name="gmm",
    )(blk0, nblk, w13, w2, xs)


def _sc_gather_rows(table, idx, window):
    n_idx = idx.shape[0]
    dw = table.shape[1]
    mesh = _sc_mesh()
    n_workers = mesh.num_cores * mesh.num_subcores
    per_w = n_idx // n_workers
    n_chunks = per_w // window
    assert per_w * n_workers == n_idx and n_chunks * window == per_w and n_chunks % 2 == 0

    @functools.partial(pl.kernel, mesh=mesh,
                       out_type=jax.ShapeDtypeStruct((n_idx, dw), table.dtype),
                       scratch_types=[pltpu.VMEM((per_w,), I32),
                                      pltpu.VMEM((window, dw), table.dtype),
                                      pltpu.VMEM((window, dw), table.dtype),
                                      pltpu.SemaphoreType.DMA, pltpu.SemaphoreType.DMA,
                                      pltpu.SemaphoreType.DMA, pltpu.SemaphoreType.DMA])
    def gather(x_hbm, i_hbm, o_hbm, idx_v, rows0, rows1, gs0, gs1, os0, os1):
        wid = lax.axis_index("subcore") * mesh.num_cores + lax.axis_index("core")
        base = wid * per_w
        pltpu.sync_copy(i_hbm.at[pl.ds(base, per_w)], idx_v)
        bufs = ((rows0, gs0, os0), (rows1, gs1, os1))

        def fetch(j, slot):
            rows, gs, _ = bufs[slot]
            off = pl.multiple_of(j * window, window)
            return pltpu.async_copy(x_hbm.at[idx_v.at[pl.ds(off, window)]], rows, gs)

        def fetch_wait(slot):
            rows, gs, _ = bufs[slot]
            pltpu.make_async_copy(x_hbm.at[idx_v.at[pl.ds(0, window)]], rows, gs).wait()

        def put(j, slot):
            rows, _, osem = bufs[slot]
            off = pl.multiple_of(j * window, window)
            return pltpu.async_copy(rows, o_hbm.at[pl.ds(base + off, window)], osem)

        def put_wait(slot):
            rows, _, osem = bufs[slot]
            pltpu.make_async_copy(rows, o_hbm.at[pl.ds(base, window)], osem).wait()

        fetch(0, 0)

        @pl.loop(0, n_chunks, step=2)
        def _(j):
            @pl.when(j > 0)
            def _():
                put_wait(1)
            fetch(j + 1, 1)
            fetch_wait(0)
            put(j, 0)
            put_wait(0)

            @pl.when(j + 2 < n_chunks)
            def _():
                fetch(j + 2, 0)
            fetch_wait(1)
            put(j + 1, 1)

        put_wait(1)

    return gather(table, idx)


def _combine_kernel(gate_ref, x1_ref, h2_ref, mod_ref, gain_ref, sw13_ref, sw2_ref, yg_ref, o_ref):
    ds_ = sw2_ref.shape[0]
    hb = _unpack_rows(h2_ref[...]).astype(BF16)
    u = jnp.dot(hb, sw13_ref[...], preferred_element_type=F32)
    hmid = _silu(u[:, :ds_]) * u[:, ds_:]
    y = jnp.dot(hmid.astype(BF16), sw2_ref[...], preferred_element_type=F32)

    gate = gate_ref[...]
    moe = gate[:, 0:1] * _unpack_rows(yg_ref[0])
    for k in range(1, TOP_K):
        moe = moe + gate[:, k:k + 1] * _unpack_rows(yg_ref[k])
    xo = x1_ref[...] + mod_ref[0, 5:6, :] * (moe + y)
    ms = jnp.mean(xo * xo, axis=-1, keepdims=True)
    o_ref[...] = xo * lax.rsqrt(ms + EPS) * gain_ref[...]


def _combine(gate_tok, x1, h2, mods, gain, sw13, sw2, yg, tc, tiles_per_batch):
    n, d = x1.shape
    dw = h2.shape[1]
    nt = n // tc
    ds2 = sw13.shape[1]
    ds_ = sw2.shape[0]
    return pl.pallas_call(
        _combine_kernel,
        out_shape=jax.ShapeDtypeStruct((n, d), F32),
        grid=(nt,),
        in_specs=[pl.BlockSpec((tc, TOP_K), lambda i: (i, 0)),
                  pl.BlockSpec((tc, d), lambda i: (i, 0)),
                  pl.BlockSpec((tc, dw), lambda i: (i, 0)),
                  pl.BlockSpec((1, 8, d), lambda i: (i // tiles_per_batch, 0, 0)),
                  pl.BlockSpec((1, d), lambda i: (0, 0)),
                  pl.BlockSpec((d, ds2), lambda i: (0, 0)),
                  pl.BlockSpec((ds_, d), lambda i: (0, 0)),
                  pl.BlockSpec((TOP_K, tc, dw), lambda i: (0, i, 0))],
        out_specs=pl.BlockSpec((tc, d), lambda i: (i, 0)),
        compiler_params=_cparams(("arbitrary",)),
        name="combine",
    )(gate_tok, x1, h2, mods, gain, sw13, sw2, yg)


def _block_diag_pairs(w, per):
    nb, bd, _ = w.shape
    w = w.reshape(nb // per, per, bd, bd)
    eye = jnp.eye(per, dtype=w.dtype)
    out = jnp.einsum("gpij,pq->gpiqj", w, eye)
    return out.reshape(nb // per, per * bd, per * bd)


def kernel(x, c, ctx, c_ctx, ada_w, ada_b, norm_mix, norm_ffn, norm_final, w_in, w_out,
           lru_conv_w, lru_conv_b, lru_wa, lru_ba, lru_wx, lru_bx, lru_lambda,
           hgrn_lb_logits, hgrn_norm, router_w, router_b, exp_w13, exp_w2, shared_w13, shared_w2):
    assert ada_w.shape[0] == 1, "single-layer block"
    b, s, d = x.shape
    n = b * s
    lru_w = lru_conv_w.shape[2]
    hg_w = hgrn_norm.shape[1]
    n_experts = router_w.shape[2]

    rows = -(-(b + 1) // SUBLANES) * SUBLANES
    cs = jnp.zeros((rows, d), F32).at[:b].set(c).at[b].set(c_ctx)
    mod = _ada(cs, ada_w[0], ada_b[0][None, :]).reshape(rows, 6, d)
    mods = jnp.pad(mod, ((0, 0), (0, 2), (0, 0)))

    w_in_bf = w_in[0].astype(BF16)
    gain1 = norm_mix[0][None, :]
    projx = _inproj(x, mods, gain1, w_in_bf, 512, shared_mod=False)
    projc = _inproj(ctx, mods[b:b + 1], gain1, w_in_bf, ctx.shape[1], shared_mod=True)

    per = LRU_G // (lru_w // LRU_BLOCKS)
    wg = jnp.stack([_block_diag_pairs(lru_wa[0, 0], per), _block_diag_pairs(lru_wx[0, 0], per),
                    _block_diag_pairs(lru_wa[0, 1], per), _block_diag_pairs(lru_wx[0, 1], per)]
                   ).astype(BF16)
    bg = jnp.stack([lru_ba[0, 0], lru_bx[0, 0], lru_ba[0, 1], lru_bx[0, 1]])
    ylru = _lru(projx, projc, lru_conv_w[0], lru_conv_b[0][None, :], wg, bg, lru_lambda[0])

    yhg = _hgrn(projx, projc, hgrn_lb_logits, hgrn_norm[0][None, :], (2 * lru_w) // LANES)

    wo = w_out[0].astype(BF16).reshape(2, lru_w, d)
    x1, h2, eid_t, gate_t, rank_t, cnt = _outproj(
        ylru, yhg, x, mods, norm_ffn[0][None, :], wo,
        router_w[0].T.astype(BF16), router_b[0][:, None], 512)

    bm = 256
    counts = cnt[:, 0]
    padded = (counts + bm - 1) // bm * bm
    pends = jnp.cumsum(padded)
    pstarts = pends - padded
    pos = _pos(pstarts, eid_t, rank_t, 2048)
    cap = n * TOP_K + n_experts * bm

    xs = _padfill(_sc_scatter_rows(h2, pos, cap), pstarts + counts, padded - counts, bm)
    ys = _gmm(xs, exp_w13[0], exp_w2[0], pstarts // bm, padded // bm, bm)

    yg = _sc_gather_rows(ys, pos.reshape(-1), 64).reshape(TOP_K, n, d // 2)
    tc = 256
    out = _combine(gate_t.T, x1.reshape(n, d), h2, mods,
                   norm_final[None, :], shared_w13[0].astype(BF16), shared_w2[0].astype(BF16),
                   yg, tc, s // tc)
    return out.reshape(b, s, d)
```

```python
import functools

import jax
import jax.numpy as jnp
from jax import lax
from jax.experimental import pallas as pl
from jax.experimental.pallas import tpu as pltpu
from jax.experimental.pallas import tpu_sc as plsc

F32 = jnp.float32
BF16 = jnp.bfloat16
I32 = jnp.int32
U32 = jnp.uint32

EPS = 1e-6
LRU_C = 8.0
ROUTE_SCALE = 2.5
GRID_W = 64
HG_CHUNK = 32
N_GROUPS = 8
TOPK_GROUPS = 4
TOP_K = 8
LRU_BLOCKS = 8
HG_HEADS = 4
CONV_W = 4
CONV_PAD_L = 1

SUBLANES = 8
LANES = 128
N_SEG = SUBLANES
VMEM_LIMIT = 56 * 1024 * 1024


def _cparams(sem, vmem=VMEM_LIMIT):
    return pltpu.CompilerParams(dimension_semantics=sem, vmem_limit_bytes=vmem)


def _sigmoid(x):
    return jax.nn.sigmoid(x)


def _silu(x):
    return x * _sigmoid(x)


def _pack_rows(x):
    w = x.shape[1] // 2
    bits = pltpu.bitcast(x.astype(BF16).astype(F32), U32)
    return (bits[:, :w] >> 16) | (bits[:, w:] & jnp.uint32(0xFFFF0000))


def _unpack_rows(p):
    lo = pltpu.bitcast(p << 16, F32)
    hi = pltpu.bitcast(p & jnp.uint32(0xFFFF0000), F32)
    return jnp.concatenate([lo, hi], axis=1)


def _gelu_tanh(x):
    c = 0.7978845608028654
    return 0.5 * x * (1.0 + jnp.tanh(c * (x + 0.044715 * (x * x * x))))


def _ada_kernel(c_ref, w_ref, b_ref, o_ref):
    s = _silu(c_ref[...])
    o_ref[...] = jnp.dot(s, w_ref[...], preferred_element_type=F32,
                         precision=lax.Precision.HIGHEST) + b_ref[...]


def _ada(cs, w, b):
    rows, d = cs.shape
    n = w.shape[1]
    bn = 1024
    return pl.pallas_call(
        _ada_kernel,
        out_shape=jax.ShapeDtypeStruct((rows, n), F32),
        grid=(n // bn,),
        in_specs=[pl.BlockSpec((rows, d), lambda j: (0, 0)),
                  pl.BlockSpec((d, bn), lambda j: (0, j)),
                  pl.BlockSpec((1, bn), lambda j: (0, j))],
        out_specs=pl.BlockSpec((rows, bn), lambda j: (0, j)),
        compiler_params=_cparams(("arbitrary",)),
        name="ada",
    )(cs, w, b)


def _inproj_kernel(x_ref, mod_ref, gain_ref, w_ref, o_ref):
    x = x_ref[0]
    ms = jnp.mean(x * x, axis=-1, keepdims=True)
    y = x * lax.rsqrt(ms + EPS) * gain_ref[...]
    h = y * (1.0 + mod_ref[0, 1:2, :]) + mod_ref[0, 0:1, :]
    o_ref[0] = jnp.dot(h.astype(BF16), w_ref[...], preferred_element_type=F32)


def _inproj(x, mods, gain, w_bf, tm, shared_mod):
    b, s, d = x.shape
    n = w_bf.shape[1]
    mod_map = (lambda bi, i: (0, 0, 0)) if shared_mod else (lambda bi, i: (bi, 0, 0))
    return pl.pallas_call(
        _inproj_kernel,
        out_shape=jax.ShapeDtypeStruct((b, s, n), F32),
        grid=(b, s // tm),
        in_specs=[pl.BlockSpec((1, tm, d), lambda bi, i: (bi, i, 0)),
                  pl.BlockSpec((1, 8, d), mod_map),
                  pl.BlockSpec((1, d), lambda bi, i: (0, 0)),
                  pl.BlockSpec((d, n), lambda bi, i: (0, 0))],
        out_specs=pl.BlockSpec((1, tm, n), lambda bi, i: (bi, i, 0)),
        compiler_params=_cparams(("arbitrary", "arbitrary")),
        name="inproj",
    )(x, mods, gain, w_bf)


LRU_G = 256
LRU_CHUNK = 256


def _seg_pitch(seg_len):
    return seg_len + SUBLANES


def _lane_store(ref, d, rows, val):
    nl = val.shape[-1] // LANES
    for l in range(nl):
        ref[d * nl + l, rows, :] = val[:, l * LANES:(l + 1) * LANES]


def _lane_load(ref, d, rows, nl):
    return jnp.concatenate([ref[d * nl + l, rows, :] for l in range(nl)], axis=-1)


LRU_CONV_PIECE = 64


def _lru_coeffs(pad_ref, t0, rows, cw, cb, wg_ref, bg, sp, a_ref, b_ref, dst0, u_ref):
    for p0 in range(0, rows, LRU_CONV_PIECE):
        win = pad_ref[pl.ds(t0 + p0, LRU_CONV_PIECE + 2 * SUBLANES), :]
        u = cb
        for k in range(CONV_W):
            off = SUBLANES - CONV_PAD_L + k
            u = u + win[off:off + LRU_CONV_PIECE, :] * cw[k:k + 1, :]
        u_ref[p0:p0 + LRU_CONV_PIECE, :] = u
    u = u_ref[0:rows, :]
    ub = u.astype(BF16)
    half_u = 0.5 * u
    for d in range(2):
        tr = jnp.tanh(jnp.dot(ub, wg_ref[2 * d, 0], preferred_element_type=F32)
                      + bg[2 * d:2 * d + 1, :])
        ti = jnp.tanh(jnp.dot(ub, wg_ref[2 * d + 1, 0], preferred_element_type=F32)
                      + bg[2 * d + 1:2 * d + 2, :])
        half_c = (-0.5 * LRU_C) * sp[d:d + 1, :]
        log_a = half_c * tr + half_c
        a = jnp.exp(log_a)
        one_minus_a2 = -jnp.tanh(log_a) * (1.0 + a * a)
        root = jnp.where(one_minus_a2 > 0.0, one_minus_a2 * lax.rsqrt(one_minus_a2), 0.0)
        bb = root * (ti * half_u + half_u)
        _lane_store(a_ref, d, pl.ds(dst0, rows), a)
        _lane_store(b_ref, d, pl.ds(dst0, rows), bb)


def _seg_scan(a_ref, b_ref, h_ref, p_ref, seg_len, pitch, unroll):
    n_lead = a_ref.shape[0]
    nl = n_lead // 2
    zero = jnp.zeros((N_SEG, LANES), F32)
    one = jnp.ones((N_SEG, LANES), F32)
    init = tuple((zero, one) for _ in range(n_lead))

    def step(t, carry):
        out = []
        for i in range(n_lead):
            h, p = carry[i]
            tt = t if i < nl else seg_len - 1 - t
            rows = pl.ds(tt, N_SEG, stride=pitch)
            a = a_ref[i, rows, :]
            h = a * h + b_ref[i, rows, :]
            p = p * a
            if h_ref is not None:
                h_ref[i, rows, :] = h
                p_ref[i, rows, :] = p
            out.append((h, p))
        return tuple(out)

    def body(i, carry):
        for j in range(unroll):
            carry = step(i * unroll + j, carry)
        return carry

    ends = lax.fori_loop(0, seg_len // unroll, body, init)
    h_end = [jnp.concatenate([ends[d * nl + l][0] for l in range(nl)], axis=-1) for d in range(2)]
    p_end = [jnp.concatenate([ends[d * nl + l][1] for l in range(nl)], axis=-1) for d in range(2)]
    return h_end, p_end


def _seg_carries(h_end, p_end, h0, reverse):
    order = range(N_SEG - 1, -1, -1) if reverse else range(N_SEG)
    cin = [None] * N_SEG
    c = h0
    for s in order:
        cin[s] = c
        c = p_end[s:s + 1, :] * c + h_end[s:s + 1, :]
    return cin, c


def _lru_kernel(rx_ref, rg_ref, rc_ref, cw_ref, cb_ref, wg_ref, bg_ref, lam_ref, o_ref,
                padl, padc, u_s, a_l, b_l, h_l, p_l, a_c, b_c):
    s_len = rx_ref.shape[1]
    c_len = rc_ref.shape[1]
    g = rx_ref.shape[2]
    nl = g // LANES
    seg_l = s_len // N_SEG
    seg_c = c_len // N_SEG
    pitch_l = _seg_pitch(seg_l)
    pitch_c = _seg_pitch(seg_c)

    zeros = jnp.zeros((SUBLANES, g), F32)
    padl[0:SUBLANES, :] = zeros
    padl[SUBLANES + s_len:2 * SUBLANES + s_len, :] = zeros
    padc[0:SUBLANES, :] = zeros
    padc[SUBLANES + c_len:2 * SUBLANES + c_len, :] = zeros
    for i in range(s_len // LRU_CHUNK):
        padl[SUBLANES + i * LRU_CHUNK:SUBLANES + (i + 1) * LRU_CHUNK, :] = (
            rx_ref[0, i * LRU_CHUNK:(i + 1) * LRU_CHUNK, :])
    padc[SUBLANES:SUBLANES + c_len, :] = rc_ref[0]

    cw = cw_ref[...]
    cb = cb_ref[...]
    bg = bg_ref[...]
    x = -lam_ref[...]
    sp = jnp.maximum(x, 0.0) + jnp.log(1.0 + jnp.exp(-jnp.abs(x)))

    _lru_coeffs(padc, 0, c_len, cw, cb, wg_ref, bg, sp, a_l, b_l, 0, u_s)
    for i in range(2 * nl):
        for s in range(N_SEG):
            a_c[i, s * pitch_c:s * pitch_c + seg_c, :] = a_l[i, s * seg_c:(s + 1) * seg_c, :]
            b_c[i, s * pitch_c:s * pitch_c + seg_c, :] = b_l[i, s * seg_c:(s + 1) * seg_c, :]
    h_end, p_end = _seg_scan(a_c, b_c, None, None, seg_c, pitch_c, unroll=4)
    zero_row = jnp.zeros((1, g), F32)
    _, h0_f = _seg_carries(h_end[0], p_end[0], zero_row, reverse=False)
    _, h0_b = _seg_carries(h_end[1], p_end[1], zero_row, reverse=True)

    def coeff_body(s, carry):
        t0 = pl.multiple_of(s * seg_l, SUBLANES)
        dst = pl.multiple_of(s * pitch_l, SUBLANES)
        _lru_coeffs(padl, t0, seg_l, cw, cb, wg_ref, bg, sp, a_l, b_l, dst, u_s)
        return carry

    lax.fori_loop(0, N_SEG, coeff_body, 0)
    h_end, p_end = _seg_scan(a_l, b_l, h_l, p_l, seg_l, pitch_l, unroll=4)
    cin_f, _ = _seg_carries(h_end[0], p_end[0], h0_f, reverse=False)
    cin_b, _ = _seg_carries(h_end[1], p_end[1], h0_b, reverse=True)

    for s in range(N_SEG):
        rows = slice(s * pitch_l, s * pitch_l + seg_l)
        h = (_lane_load(h_l, 0, rows, nl) + _lane_load(p_l, 0, rows, nl) * cin_f[s]) + (
            _lane_load(h_l, 1, rows, nl) + _lane_load(p_l, 1, rows, nl) * cin_b[s])
        o_ref[0, s * seg_l:(s + 1) * seg_l, :] = _gelu_tanh(rg_ref[0, s * seg_l:(s + 1) * seg_l, :]) * h


def _lru(projx, projc, cw, cb, wg, bg, lam):
    b, s_len, _ = projx.shape
    c_len = projc.shape[1]
    w = cw.shape[1]
    g = LRU_G
    ng = w // g
    seg_l = s_len // N_SEG
    seg_c = c_len // N_SEG
    assert seg_l == LRU_CHUNK and c_len % (N_SEG * 1) == 0
    return pl.pallas_call(
        _lru_kernel,
        out_shape=jax.ShapeDtypeStruct((b, s_len, w), F32),
        grid=(b, ng),
        in_specs=[pl.BlockSpec((1, s_len, g), lambda bi, j: (bi, 0, j)),
                  pl.BlockSpec((1, s_len, g), lambda bi, j: (bi, 0, ng + j)),
                  pl.BlockSpec((1, c_len, g), lambda bi, j: (bi, 0, j)),
                  pl.BlockSpec((CONV_W, g), lambda bi, j: (0, j)),
                  pl.BlockSpec((1, g), lambda bi, j: (0, j)),
                  pl.BlockSpec((4, 1, g, g), lambda bi, j: (0, j, 0, 0)),
                  pl.BlockSpec((4, g), lambda bi, j: (0, j)),
                  pl.BlockSpec((2, g), lambda bi, j: (0, j))],
        out_specs=pl.BlockSpec((1, s_len, g), lambda bi, j: (bi, 0, j)),
        scratch_shapes=[pltpu.VMEM((s_len + 2 * SUBLANES, g), F32),
                        pltpu.VMEM((c_len + 2 * SUBLANES, g), F32),
                        pltpu.VMEM((max(seg_l, c_len), g), F32),
                        pltpu.VMEM((2 * g // LANES, N_SEG * _seg_pitch(seg_l), LANES), F32),
                        pltpu.VMEM((2 * g // LANES, N_SEG * _seg_pitch(seg_l), LANES), F32),
                        pltpu.VMEM((2 * g // LANES, N_SEG * _seg_pitch(seg_l), LANES), F32),
                        pltpu.VMEM((2 * g // LANES, N_SEG * _seg_pitch(seg_l), LANES), F32),
                        pltpu.VMEM((2 * g // LANES, N_SEG * _seg_pitch(seg_c), LANES), F32),
                        pltpu.VMEM((2 * g // LANES, N_SEG * _seg_pitch(seg_c), LANES), F32)],
        compiler_params=_cparams(("arbitrary", "arbitrary")),
        name="lru",
    )(projx, projx, projc, cw, cb, wg, bg, lam)


HG_PITCH_L = GRID_W + SUBLANES
HG_PITCH_C = SUBLANES


def _hg_gates(z, lb):
    half_t = 0.5 * jnp.tanh(0.5 * z)
    sig = 0.5 + half_t
    nsig = 0.5 - half_t
    logf = jnp.log(lb + (1.0 - lb) * sig)
    k = (1.0 - lb) * nsig
    return logf, k


def _hg_prepass(load, width, pitch, lb, qd, kd, kl, vs, dec):
    n_pos = HG_CHUNK
    for d, zname in enumerate(("zf", "zb")):
        order = range(n_pos) if d == 0 else range(n_pos - 1, -1, -1)
        g = jnp.zeros((width, LANES), F32)
        lbd = lb[d:d + 1, :]
        for p in order:
            logf, k = _hg_gates(load(zname, p), lbd)
            g = g + logf
            qd[d, p * pitch:p * pitch + width, :] = g
            kl[d, p * pitch:p * pitch + width, :] = k
        g_last = g
        dec[d, 0:width, :] = jnp.exp(g_last)
        for p in range(n_pos):
            rows = slice(p * pitch, p * pitch + width)
            gp = qd[d, rows, :]
            k = kl[d, rows, :]
            qd[d, rows, :] = load("q", p) * jnp.exp(gp)
            kd[d, rows, :] = k * jnp.exp(-gp)
            kl[d, rows, :] = k * jnp.exp(g_last - gp)
    for p in range(n_pos):
        vs[p * pitch:p * pitch + width, :] = load("v", p)


HG_GROUP = 4
HG_UNROLL = 8
HG_INTRA_UNROLL = 4
_NT = (((1,), (1,)), ((), ()))
_TN = (((0,), (0,)), ((), ()))


def _hg_group_rows(ref, lead, c0, pitch):
    parts = []
    for j in range(HG_GROUP):
        rows = pl.ds(c0 + j, HG_CHUNK, stride=pitch)
        parts.append(ref[rows, :] if lead is None else ref[lead, rows, :])
    return jnp.concatenate(parts, axis=0)


def _hg_scores(g, pitch, qd, kd, masks, sc_ref):
    c0 = g * HG_GROUP
    total = None
    for d in range(2):
        q = _hg_group_rows(qd, d, c0, pitch).astype(BF16)
        k = _hg_group_rows(kd, d, c0, pitch).astype(BF16)
        sc = jnp.where(masks[d], lax.dot_general(q, k, _NT, preferred_element_type=F32), 0.0)
        total = sc if total is None else total + sc
    sc_ref[g] = total.astype(BF16)


def _hg_summaries(g, pitch, kl, vs, sc_ref, oi, kv):
    c0 = g * HG_GROUP
    v = _hg_group_rows(vs, None, c0, pitch).astype(BF16)
    if oi is not None:
        o = jnp.dot(sc_ref[g], v, preferred_element_type=F32)
        for j in range(HG_GROUP):
            oi[pl.ds(c0 + j, HG_CHUNK, stride=pitch), :] = o[j * HG_CHUNK:(j + 1) * HG_CHUNK, :]
    kls = jnp.concatenate([_hg_group_rows(kl, 0, c0, pitch), _hg_group_rows(kl, 1, c0, pitch)],
                          axis=1).astype(BF16)
    for j in range(HG_GROUP):
        sl = slice(j * HG_CHUNK, (j + 1) * HG_CHUNK)
        both = lax.dot_general(v[sl], kls[sl], _TN, preferred_element_type=F32)
        kv[0, c0 + j] = both[:, :LANES]
        kv[1, c0 + j] = both[:, LANES:]


def _hg_steps(step0, n_steps, n_chunks, pitch, qd, dec, kv, states, ox):
    states = list(states)
    for u in range(n_steps):
        for d in range(2):
            c = step0 + u if d == 0 else n_chunks - 1 - (step0 + u)
            s_t = states[d]
            if ox is not None:
                rows = pl.ds(c, HG_CHUNK, stride=pitch)
                q = qd[d, rows, :].astype(BF16)
                ox[d, rows, :] = lax.dot_general(q, s_t.astype(BF16), _NT,
                                                 preferred_element_type=F32)
            states[d] = s_t * dec[d, pl.ds(c, 1), :] + kv[d, c]
    return tuple(states)


def _hgrn_kernel(q_ref, v_ref, zf_ref, zb_ref, g_ref, qc_ref, vc_ref, zfc_ref, zbc_ref,
                 lbl_ref, gain_ref, o_ref, qd, kd, kl, vs, dec, oi, ox, kv, sc):
    s_len = q_ref.shape[1]
    c_len = qc_ref.shape[1]
    n_col = s_len // HG_CHUNK
    n_cc = c_len // HG_CHUNK
    assert n_col == GRID_W and n_cc == HG_PITCH_C
    assert n_col % HG_GROUP == 0 and n_cc % HG_GROUP == 0 and n_col % HG_UNROLL == 0

    lg = lbl_ref[...]
    m = jnp.max(lg, axis=1, keepdims=True)
    ex = jnp.exp(lg - m)
    lb = ex[:, 0, :] / jnp.sum(ex, axis=1)

    gr = HG_GROUP * HG_CHUNK
    ii = lax.broadcasted_iota(I32, (gr, gr), 0)
    jj = lax.broadcasted_iota(I32, (gr, gr), 1)
    same = (ii // HG_CHUNK) == (jj // HG_CHUNK)
    masks = (jnp.logical_and(same, jj <= ii), jnp.logical_and(same, jj >= ii))

    c_refs = {"q": qc_ref, "v": vc_ref, "zf": zfc_ref, "zb": zbc_ref}

    def load_c(name, p):
        return c_refs[name][0, pl.ds(p, n_cc, stride=HG_CHUNK), :]

    _hg_prepass(load_c, n_cc, HG_PITCH_C, lb, qd, kd, kl, vs, dec)
    for g in range(n_cc // HG_GROUP):
        _hg_summaries(g, HG_PITCH_C, kl, vs, None, None, kv)
    zero = jnp.zeros((LANES, LANES), F32)
    states = _hg_steps(0, n_cc, n_cc, HG_PITCH_C, qd, dec, kv, (zero, zero), None)

    l_refs = {"q": q_ref, "v": v_ref, "zf": zf_ref, "zb": zb_ref}

    def load_l(name, p):
        return l_refs[name][0, p * n_col:(p + 1) * n_col, :]

    _hg_prepass(load_l, n_col, HG_PITCH_L, lb, qd, kd, kl, vs, dec)

    n_it = n_col // (HG_GROUP * HG_INTRA_UNROLL)

    def scores_body(i, carry):
        for u in range(HG_INTRA_UNROLL):
            _hg_scores(i * HG_INTRA_UNROLL + u, HG_PITCH_L, qd, kd, masks, sc)
        return carry

    lax.fori_loop(0, n_it, scores_body, 0)

    def summaries_body(i, carry):
        for u in range(HG_INTRA_UNROLL):
            _hg_summaries(i * HG_INTRA_UNROLL + u, HG_PITCH_L, kl, vs, sc, oi, kv)
        return carry

    lax.fori_loop(0, n_it, summaries_body, 0)

    def step_body(i, carry):
        return _hg_steps(i * HG_UNROLL, HG_UNROLL, n_col, HG_PITCH_L, qd, dec, kv, carry, ox)

    lax.fori_loop(0, n_col // HG_UNROLL, step_body, states)

    gain = gain_ref[...]
    for r in range(HG_CHUNK):
        rows = slice(r * HG_PITCH_L, r * HG_PITCH_L + n_col)
        o = oi[rows, :] + (ox[0, rows, :] + ox[1, rows, :])
        y = o * lax.rsqrt(jnp.mean(o * o, axis=-1, keepdims=True) + EPS) * gain
        o_ref[0, r * n_col:(r + 1) * n_col, :] = y * _silu(g_ref[0, r * n_col:(r + 1) * n_col, :])


def _hgrn(projx, projc, lbl, gain, col0):
    b, s_len, _ = projx.shape
    c_len = projc.shape[1]
    hw = gain.shape[1]
    nh = hw // LANES
    n_slots = lbl.shape[1]

    def xs(k):
        return pl.BlockSpec((1, s_len, LANES), lambda bi, h, k=k: (bi, 0, col0 + k * nh + h))

    def cs(k):
        return pl.BlockSpec((1, c_len, LANES), lambda bi, h, k=k: (bi, 0, col0 + k * nh + h))

    sc_rows = HG_CHUNK * HG_PITCH_L
    return pl.pallas_call(
        _hgrn_kernel,
        out_shape=jax.ShapeDtypeStruct((b, s_len, hw), F32),
        grid=(b, nh),
        in_specs=[xs(0), xs(1), xs(2), xs(3), xs(4), cs(0), cs(1), cs(2), cs(3),
                  pl.BlockSpec((2, n_slots, LANES), lambda bi, h: (0, 0, h)),
                  pl.BlockSpec((1, LANES), lambda bi, h: (0, h))],
        out_specs=pl.BlockSpec((1, s_len, LANES), lambda bi, h: (bi, 0, h)),
        scratch_shapes=[pltpu.VMEM((2, sc_rows, LANES), F32),
                        pltpu.VMEM((2, sc_rows, LANES), F32),
                        pltpu.VMEM((2, sc_rows, LANES), F32),
                        pltpu.VMEM((sc_rows, LANES), F32),
                        pltpu.VMEM((2, HG_PITCH_L, LANES), F32),
                        pltpu.VMEM((sc_rows, LANES), F32),
                        pltpu.VMEM((2, sc_rows, LANES), F32),
                        pltpu.VMEM((2, GRID_W, LANES, LANES), F32),
                        pltpu.VMEM((GRID_W // HG_GROUP, HG_GROUP * HG_CHUNK,
                                    HG_GROUP * HG_CHUNK), BF16)],
        compiler_params=_cparams(("arbitrary", "arbitrary")),
        name="hgrn",
    )(projx, projx, projx, projx, projx, projc, projc, projc, projc, lbl, gain)


def _first_index_of_max(vals, iota, big):
    m = jnp.max(vals, axis=0, keepdims=True)
    idx = jnp.min(jnp.where(vals == m, iota, big), axis=0, keepdims=True)
    return m, idx


def _route(logits_t, rb, n_experts):
    t = logits_t.shape[1]
    gsz = n_experts // N_GROUPS
    neg = -jnp.inf
    scores = _sigmoid(logits_t)
    biased = scores + rb
    iota_g = lax.broadcasted_iota(I32, (gsz, t), 0)
    gscore = []
    for gi in range(N_GROUPS):
        blk = biased[gi * gsz:(gi + 1) * gsz, :]
        m1, i1 = _first_index_of_max(blk, iota_g, gsz)
        m2 = jnp.max(jnp.where(iota_g == i1, neg, blk), axis=0, keepdims=True)
        gscore.append(m1 + m2)
    gs = jnp.concatenate(gscore, axis=0)
    iota_n = lax.broadcasted_iota(I32, (N_GROUPS, t), 0)
    gsel = jnp.zeros((N_GROUPS, t), jnp.bool_)
    for _ in range(TOPK_GROUPS):
        _, gi1 = _first_index_of_max(gs, iota_n, N_GROUPS)
        hit = iota_n == gi1
        gsel = jnp.logical_or(gsel, hit)
        gs = jnp.where(hit, neg, gs)
    emask = jnp.concatenate(
        [jnp.broadcast_to(gsel[gi:gi + 1, :], (gsz, t)) for gi in range(N_GROUPS)], axis=0)
    masked = jnp.where(emask, biased, neg)
    iota_e = lax.broadcasted_iota(I32, (n_experts, t), 0)
    ids, gates = [], []
    sel = jnp.zeros((n_experts, t), F32)
    for _ in range(TOP_K):
        _, ei = _first_index_of_max(masked, iota_e, n_experts)
        hit = iota_e == ei
        ids.append(ei)
        gates.append(jnp.sum(jnp.where(hit, scores, 0.0), axis=0, keepdims=True))
        sel = jnp.where(hit, 1.0, sel)
        masked = jnp.where(hit, neg, masked)
    ids = jnp.concatenate(ids, axis=0)
    gates = jnp.concatenate(gates, axis=0)
    gates = gates / jnp.sum(gates, axis=0, keepdims=True) * ROUTE_SCALE
    return ids, gates, sel


def _outproj_kernel(yl_ref, yh_ref, x_ref, mod_ref, gain_ref, wo_ref, rwt_ref, rb_ref,
                    x1_ref, h2_ref, eid_ref, gate_ref, rank_ref, cnt_ref, carry):
    first = jnp.logical_and(pl.program_id(0) == 0, pl.program_id(1) == 0)

    @pl.when(first)
    def _():
        carry[...] = jnp.zeros(carry.shape, F32)

    y = jnp.dot(yl_ref[0].astype(BF16), wo_ref[0], preferred_element_type=F32)
    y = y + jnp.dot(yh_ref[0].astype(BF16), wo_ref[1], preferred_element_type=F32)
    x1 = x_ref[0] + mod_ref[0, 2:3, :] * y
    x1_ref[0] = x1
    ms = jnp.mean(x1 * x1, axis=-1, keepdims=True)
    h2 = x1 * lax.rsqrt(ms + EPS) * gain_ref[...]
    h2 = h2 * (1.0 + mod_ref[0, 4:5, :]) + mod_ref[0, 3:4, :]
    h2_ref[...] = _pack_rows(h2)

    n_experts = rwt_ref.shape[0]
    t = h2.shape[0]
    logits_t = lax.dot_general(rwt_ref[...], h2.astype(BF16), _NT,
                               preferred_element_type=F32)
    ids, gates, sel = _route(logits_t, rb_ref[...], n_experts)
    eid_ref[...] = ids
    gate_ref[...] = gates

    ti = lax.broadcasted_iota(I32, (t, t), 0)
    tj = lax.broadcasted_iota(I32, (t, t), 1)
    upper = (ti < tj).astype(BF16)
    selb = sel.astype(BF16)
    before = jnp.dot(selb, upper, preferred_element_type=F32) + carry[:, 0:1]
    iota_e = lax.broadcasted_iota(I32, (n_experts, t), 0)
    ranks = [jnp.sum(jnp.where(iota_e == ids[k:k + 1, :], before, 0.0), axis=0, keepdims=True)
             for k in range(TOP_K)]
    rank_ref[...] = jnp.concatenate(ranks, axis=0).astype(I32)
    total = carry[...] + jnp.dot(selb, jnp.ones((t, LANES), BF16), preferred_element_type=F32)
    carry[...] = total
    cnt_ref[...] = total.astype(I32)


def _outproj(ylru, yhg, x, mods, gain, wo, rwt, rb, tm):
    b, s, d = x.shape
    hw = ylru.shape[2]
    e = rwt.shape[0]
    n = b * s
    nt = s // tm
    tok = lambda bi, i: (0, bi * nt + i)
    return pl.pallas_call(
        _outproj_kernel,
        out_shape=(jax.ShapeDtypeStruct((b, s, d), F32),
                   jax.ShapeDtypeStruct((n, d // 2), U32),
                   jax.ShapeDtypeStruct((TOP_K, n), I32),
                   jax.ShapeDtypeStruct((TOP_K, n), F32),
                   jax.ShapeDtypeStruct((TOP_K, n), I32),
                   jax.ShapeDtypeStruct((e, LANES), I32)),
        grid=(b, nt),
        in_specs=[pl.BlockSpec((1, tm, hw), lambda bi, i: (bi, i, 0)),
                  pl.BlockSpec((1, tm, hw), lambda bi, i: (bi, i, 0)),
                  pl.BlockSpec((1, tm, d), lambda bi, i: (bi, i, 0)),
                  pl.BlockSpec((1, 8, d), lambda bi, i: (bi, 0, 0)),
                  pl.BlockSpec((1, d), lambda bi, i: (0, 0)),
                  pl.BlockSpec((2, hw, d), lambda bi, i: (0, 0, 0)),
                  pl.BlockSpec((e, d), lambda bi, i: (0, 0)),
                  pl.BlockSpec((e, 1), lambda bi, i: (0, 0))],
        out_specs=(pl.BlockSpec((1, tm, d), lambda bi, i: (bi, i, 0)),
                   pl.BlockSpec((tm, d // 2), lambda bi, i: (bi * nt + i, 0)),
                   pl.BlockSpec((TOP_K, tm), tok),
                   pl.BlockSpec((TOP_K, tm), tok),
                   pl.BlockSpec((TOP_K, tm), tok),
                   pl.BlockSpec((e, LANES), lambda bi, i: (0, 0))),
        scratch_shapes=[pltpu.VMEM((e, LANES), F32)],
        compiler_params=_cparams(("arbitrary", "arbitrary")),
        name="outproj",
    )(ylru, yhg, x, mods, gain, wo, rwt, rb)


def _pos_kernel(pstart, eid_ref, rank_ref, pos_ref):
    eid = eid_ref[...]

    def body(e, acc):
        return jnp.where(eid == e, pstart[e], acc)

    base = lax.fori_loop(0, pstart.shape[0], body, jnp.zeros(eid.shape, I32))
    pos_ref[...] = base + rank_ref[...]


def _pos(pstarts, eid_t, rank_t, tn):
    k, n = eid_t.shape
    spec = pl.BlockSpec((k, tn), lambda i, ps: (0, i))
    return pl.pallas_call(
        _pos_kernel,
        out_shape=jax.ShapeDtypeStruct((k, n), I32),
        grid_spec=pltpu.PrefetchScalarGridSpec(
            num_scalar_prefetch=1, grid=(n // tn,), in_specs=[spec, spec], out_specs=spec),
        compiler_params=_cparams(("arbitrary",)),
        name="pos",
    )(pstarts, eid_t, rank_t)


def _pad_fill(padstart, padlen, zbuf, xs_ref, sem, wait):
    nbits = zbuf.shape[0].bit_length() - 1
    low_bits = SUBLANES.bit_length() - 1

    def go(cp):
        if wait:
            cp.wait()
        else:
            cp.start()

    def body(e, carry):
        st = padstart[e]
        ln = padlen[e]
        end = st + ln
        for bit in range(nbits - 1, low_bits - 1, -1):
            size = 1 << bit
            back = ((ln >> (bit + 1)) << (bit + 1)) + size

            @pl.when((ln & size) != 0)
            def _():
                start = pl.multiple_of(end - back, SUBLANES)
                go(pltpu.make_async_copy(zbuf.at[pl.ds(0, size)],
                                         xs_ref.at[pl.ds(start, size)], sem))
        for j in range(SUBLANES - 1):
            @pl.when(j < (ln & (SUBLANES - 1)))
            def _():
                go(pltpu.make_async_copy(zbuf.at[pl.ds(0, 1)], xs_ref.at[pl.ds(st + j, 1)], sem))
        return carry

    lax.fori_loop(0, padstart.shape[0], body, 0)


def _padfill_kernel(padstart, padlen, xs_in, xs_ref, zbuf, sem):
    del xs_in
    zbuf[...] = jnp.zeros(zbuf.shape, zbuf.dtype)
    _pad_fill(padstart, padlen, zbuf, xs_ref, sem, wait=False)
    _pad_fill(padstart, padlen, zbuf, xs_ref, sem, wait=True)


def _padfill(xs, padstart, padlen, bm):
    return pl.pallas_call(
        _padfill_kernel,
        out_shape=jax.ShapeDtypeStruct(xs.shape, xs.dtype),
        grid_spec=pltpu.PrefetchScalarGridSpec(
            num_scalar_prefetch=2,
            grid=(1,),
            in_specs=[pl.BlockSpec(memory_space=pl.ANY)],
            out_specs=pl.BlockSpec(memory_space=pl.ANY),
            scratch_shapes=[pltpu.VMEM((bm, xs.shape[1]), xs.dtype), pltpu.SemaphoreType.DMA(())]),
        input_output_aliases={2: 0},
        compiler_params=_cparams(("arbitrary",)),
        name="padfill",
    )(padstart, padlen, xs)


SC_SCATTER_WINDOW = 128


def _sc_mesh():
    return plsc.VectorSubcoreMesh(core_axis_name="core", subcore_axis_name="subcore")


def _sc_scatter_rows(rows, pos, cap):
    n, dw = rows.shape
    top_k = pos.shape[0]
    mesh = _sc_mesh()
    n_workers = mesh.num_cores * mesh.num_subcores
    win = SC_SCATTER_WINDOW
    per_w = n // n_workers
    n_chunks = per_w // win
    assert per_w * n_workers == n and n_chunks * win == per_w
    pos_w = pos.reshape(top_k, n_workers, n_chunks, win).transpose(1, 2, 0, 3)
    pos_w = pos_w.reshape(n_workers, n_chunks * top_k, win)

    @functools.partial(pl.kernel, mesh=mesh,
                       out_type=jax.ShapeDtypeStruct((cap, dw), rows.dtype),
                       scratch_types=[pltpu.VMEM((n_chunks * top_k, win), I32),
                                      pltpu.VMEM((win, dw), rows.dtype),
                                      pltpu.SemaphoreType.DMA])
    def scatter(r_hbm, p_hbm, o_hbm, idx_v, rows_v, sem):
        wid = lax.axis_index("subcore") * mesh.num_cores + lax.axis_index("core")
        pltpu.sync_copy(p_hbm.at[wid], idx_v)

        @pl.loop(0, n_chunks)
        def _(c):
            pltpu.sync_copy(r_hbm.at[pl.ds(wid * per_w + c * win, win)], rows_v)
            copies = [pltpu.async_copy(rows_v, o_hbm.at[idx_v.at[c * top_k + k]], sem)
                      for k in range(top_k)]
            for cp in copies:
                cp.wait()

    return scatter(rows, pos_w)


GMM_PAIR = 4
GMM_SLOTS = 16
GMM_AHEAD = GMM_SLOTS - GMM_PAIR


def _gmm_rows(xw, w13b, w2b, de):
    xb = _unpack_rows(xw).astype(BF16)
    u = jnp.dot(xb, w13b[...], preferred_element_type=F32)
    hmid = _silu(u[:, :de]) * u[:, de:]
    return _pack_rows(jnp.dot(hmid.astype(BF16), w2b[...], preferred_element_type=F32))


def _gmm_kernel(blk0, nblk, w13_ref, w2_ref, xs_ref, ys_ref, w13b, w2b, xbuf, ybuf, isem, osem):
    e = pl.program_id(0)
    n_e = pl.num_programs(0)
    slots, bm = xbuf.shape[0], xbuf.shape[1]
    de = w2_ref.shape[1]
    nb = nblk[e]
    b0 = blk0[e]
    total = blk0[n_e - 1] + nblk[n_e - 1]

    def x_copy(g):
        rows = pl.ds(pl.multiple_of(g * bm, bm), bm)
        slot = g & (slots - 1)
        return pltpu.make_async_copy(xs_ref.at[rows], xbuf.at[slot], isem.at[slot])

    def y_copy(g):
        rows = pl.ds(pl.multiple_of(g * bm, bm), bm)
        slot = g & (slots - 1)
        return pltpu.make_async_copy(ybuf.at[slot], ys_ref.at[rows], osem.at[slot])

    @pl.when(e == 0)
    def _():
        for j in range(GMM_AHEAD):
            @pl.when(j < total)
            def _():
                x_copy(j).start()

    w13b[...] = w13_ref[0].astype(BF16)
    w2b[...] = w2_ref[0].astype(BF16)

    def run_blocks(g0, count):
        gs = [g0 + j for j in range(count)]
        for g in gs:
            x_copy(g).wait()
        for g in gs:
            @pl.when(g + GMM_AHEAD < total)
            def _():
                x_copy(g + GMM_AHEAD).start()

            @pl.when(g >= slots)
            def _():
                y_copy(g - slots).wait()
        for g in gs:
            slot = g & (slots - 1)
            ybuf[slot] = _gmm_rows(xbuf[slot], w13b, w2b, de)
        for g in gs:
            y_copy(g).start()

    def pair_body(i, carry):
        run_blocks(b0 + GMM_PAIR * i, GMM_PAIR)
        return carry

    lax.fori_loop(0, nb // GMM_PAIR, pair_body, 0)
    for r in range(1, GMM_PAIR):
        @pl.when(nb % GMM_PAIR == r)
        def _():
            run_blocks(b0 + nb - r, r)

    @pl.when(e == n_e - 1)
    def _():
        for j in range(1, slots + 1):
            @pl.when(total - j >= 0)
            def _():
                y_copy(total - j).wait()


def _gmm(xs, w13, w2, blk0, nblk, bm):
    cap, dw = xs.shape
    e, d, de2 = w13.shape
    de = w2.shape[1]
    return pl.pallas_call(
        _gmm_kernel,
        out_shape=jax.ShapeDtypeStruct((cap, dw), U32),
        grid_spec=pltpu.PrefetchScalarGridSpec(
            num_scalar_prefetch=2,
            grid=(e,),
            in_specs=[pl.BlockSpec((1, d, de2), lambda i, b0, nb: (i, 0, 0)),
                      pl.BlockSpec((1, de, d), lambda i, b0, nb: (i, 0, 0)),
                      pl.BlockSpec(memory_space=pl.ANY)],
            out_specs=pl.BlockSpec(memory_space=pl.ANY),
            scratch_shapes=[pltpu.VMEM((d, de2), BF16), pltpu.VMEM((de, d), BF16),
                            pltpu.VMEM((GMM_SLOTS, bm, dw), U32),
                            pltpu.VMEM((GMM_SLOTS, bm, dw), U32),
                            pltpu.SemaphoreType.DMA((GMM_SLOTS,)),
                            pltpu.SemaphoreType.DMA((GMM_SLOTS,))]),
        compiler_params=_cparams(("arbitrary",)),
        name="gmm",
    )(blk0, nblk, w13, w2, xs)


def _sc_gather_rows(table, idx, window):
    n_idx = idx.shape[0]
    dw = table.shape[1]
    mesh = _sc_mesh()
    n_workers = mesh.num_cores * mesh.num_subcores
    per_w = n_idx // n_workers
    n_chunks = per_w // window
    assert per_w * n_workers == n_idx and n_chunks * window == per_w and n_chunks % 2 == 0

    @functools.partial(pl.kernel, mesh=mesh,
                       out_type=jax.ShapeDtypeStruct((n_idx, dw), table.dtype),
                       scratch_types=[pltpu.VMEM((per_w,), I32),
                                      pltpu.VMEM((window, dw), table.dtype),
                                      pltpu.VMEM((window, dw), table.dtype),
                                      pltpu.SemaphoreType.DMA, pltpu.SemaphoreType.DMA,
                                      pltpu.SemaphoreType.DMA, pltpu.SemaphoreType.DMA])
    def gather(x_hbm, i_hbm, o_hbm, idx_v, rows0, rows1, gs0, gs1, os0, os1):
        wid = lax.axis_index("subcore") * mesh.num_cores + lax.axis_index("core")
        base = wid * per_w
        pltpu.sync_copy(i_hbm.at[pl.ds(base, per_w)], idx_v)
        bufs = ((rows0, gs0, os0), (rows1, gs1, os1))

        def fetch(j, slot):
            rows, gs, _ = bufs[slot]
            off = pl.multiple_of(j * window, window)
            return pltpu.async_copy(x_hbm.at[idx_v.at[pl.ds(off, window)]], rows, gs)

        def fetch_wait(slot):
            rows, gs, _ = bufs[slot]
            pltpu.make_async_copy(x_hbm.at[idx_v.at[pl.ds(0, window)]], rows, gs).wait()

        def put(j, slot):
            rows, _, osem = bufs[slot]
            off = pl.multiple_of(j * window, window)
            return pltpu.async_copy(rows, o_hbm.at[pl.ds(base + off, window)], osem)

        def put_wait(slot):
            rows, _, osem = bufs[slot]
            pltpu.make_async_copy(rows, o_hbm.at[pl.ds(base, window)], osem).wait()

        fetch(0, 0)

        @pl.loop(0, n_chunks, step=2)
        def _(j):
            @pl.when(j > 0)
            def _():
                put_wait(1)
            fetch(j + 1, 1)
            fetch_wait(0)
            put(j, 0)
            put_wait(0)

            @pl.when(j + 2 < n_chunks)
            def _():
                fetch(j + 2, 0)
            fetch_wait(1)
            put(j + 1, 1)

        put_wait(1)

    return gather(table, idx)


def _combine_kernel(gate_ref, x1_ref, h2_ref, mod_ref, gain_ref, sw13_ref, sw2_ref, yg_ref, o_ref):
    ds_ = sw2_ref.shape[0]
    hb = _unpack_rows(h2_ref[...]).astype(BF16)
    u = jnp.dot(hb, sw13_ref[...], preferred_element_type=F32)
    hmid = _silu(u[:, :ds_]) * u[:, ds_:]
    y = jnp.dot(hmid.astype(BF16), sw2_ref[...], preferred_element_type=F32)

    gate = gate_ref[...]
    moe = gate[:, 0:1] * _unpack_rows(yg_ref[0])
    for k in range(1, TOP_K):
        moe = moe + gate[:, k:k + 1] * _unpack_rows(yg_ref[k])
    xo = x1_ref[...] + mod_ref[0, 5:6, :] * (moe + y)
    ms = jnp.mean(xo * xo, axis=-1, keepdims=True)
    o_ref[...] = xo * lax.rsqrt(ms + EPS) * gain_ref[...]


def _combine(gate_tok, x1, h2, mods, gain, sw13, sw2, yg, tc, tiles_per_batch):
    n, d = x1.shape
    dw = h2.shape[1]
    nt = n // tc
    ds2 = sw13.shape[1]
    ds_ = sw2.shape[0]
    return pl.pallas_call(
        _combine_kernel,
        out_shape=jax.ShapeDtypeStruct((n, d), F32),
        grid=(nt,),
        in_specs=[pl.BlockSpec((tc, TOP_K), lambda i: (i, 0)),
                  pl.BlockSpec((tc, d), lambda i: (i, 0)),
                  pl.BlockSpec((tc, dw), lambda i: (i, 0)),
                  pl.BlockSpec((1, 8, d), lambda i: (i // tiles_per_batch, 0, 0)),
                  pl.BlockSpec((1, d), lambda i: (0, 0)),
                  pl.BlockSpec((d, ds2), lambda i: (0, 0)),
                  pl.BlockSpec((ds_, d), lambda i: (0, 0)),
                  pl.BlockSpec((TOP_K, tc, dw), lambda i: (0, i, 0))],
        out_specs=pl.BlockSpec((tc, d), lambda i: (i, 0)),
        compiler_params=_cparams(("arbitrary",)),
        name="combine",
    )(gate_tok, x1, h2, mods, gain, sw13, sw2, yg)


def _block_diag_pairs(w, per):
    nb, bd, _ = w.shape
    w = w.reshape(nb // per, per, bd, bd)
    eye = jnp.eye(per, dtype=w.dtype)
    out = jnp.einsum("gpij,pq->gpiqj", w, eye)
    return out.reshape(nb // per, per * bd, per * bd)


def kernel(x, c, ctx, c_ctx, ada_w, ada_b, norm_mix, norm_ffn, norm_final, w_in, w_out,
           lru_conv_w, lru_conv_b, lru_wa, lru_ba, lru_wx, lru_bx, lru_lambda,
           hgrn_lb_logits, hgrn_norm, router_w, router_b, exp_w13, exp_w2, shared_w13, shared_w2):
    assert ada_w.shape[0] == 1, "single-layer block"
    b, s, d = x.shape
    n = b * s
    lru_w = lru_conv_w.shape[2]
    hg_w = hgrn_norm.shape[1]
    n_experts = router_w.shape[2]

    rows = -(-(b + 1) // SUBLANES) * SUBLANES
    cs = jnp.zeros((rows, d), F32).at[:b].set(c).at[b].set(c_ctx)
    mod = _ada(cs, ada_w[0], ada_b[0][None, :]).reshape(rows, 6, d)
    mods = jnp.pad(mod, ((0, 0), (0, 2), (0, 0)))

    w_in_bf = w_in[0].astype(BF16)
    gain1 = norm_mix[0][None, :]
    projx = _inproj(x, mods, gain1, w_in_bf, 512, shared_mod=False)
    projc = _inproj(ctx, mods[b:b + 1], gain1, w_in_bf, ctx.shape[1], shared_mod=True)

    per = LRU_G // (lru_w // LRU_BLOCKS)
    wg = jnp.stack([_block_diag_pairs(lru_wa[0, 0], per), _block_diag_pairs(lru_wx[0, 0], per),
                    _block_diag_pairs(lru_wa[0, 1], per), _block_diag_pairs(lru_wx[0, 1], per)]
                   )
    wg = (0.5 * wg).astype(BF16)
    bg = 0.5 * jnp.stack([lru_ba[0, 0], lru_bx[0, 0], lru_ba[0, 1], lru_bx[0, 1]])
    ylru = _lru(projx, projc, lru_conv_w[0], lru_conv_b[0][None, :], wg, bg, lru_lambda[0])

    yhg = _hgrn(projx, projc, hgrn_lb_logits, hgrn_norm[0][None, :], (2 * lru_w) // LANES)

    wo = w_out[0].astype(BF16).reshape(2, lru_w, d)
    x1, h2, eid_t, gate_t, rank_t, cnt = _outproj(
        ylru, yhg, x, mods, norm_ffn[0][None, :], wo,
        router_w[0].T.astype(BF16), router_b[0][:, None], 512)

    bm = 256
    counts = cnt[:, 0]
    padded = (counts + bm - 1) // bm * bm
    pends = jnp.cumsum(padded)
    pstarts = pends - padded
    pos = _pos(pstarts, eid_t, rank_t, 2048)
    cap = n * TOP_K + n_experts * bm

    xs = _padfill(_sc_scatter_rows(h2, pos, cap), pstarts + counts, padded - counts, bm)
    ys = _gmm(xs, exp_w13[0], exp_w2[0], pstarts // bm, padded // bm, bm)

    yg = _sc_gather_rows(ys, pos.reshape(-1), 64).reshape(TOP_K, n, d // 2)
    tc = 256
    out = _combine(gate_t.T, x1.reshape(n, d), h2, mods,
                   norm_final[None, :], shared_w13[0].astype(BF16), shared_w2[0].astype(BF16),
                   yg, tc, s // tc)
    return out.reshape(b, s, d)
```

```python
import functools

import jax
import jax.numpy as jnp
from jax import lax
from jax.experimental import pallas as pl
from jax.experimental.pallas import tpu as pltpu
from jax.experimental.pallas import tpu_sc as plsc

F32 = jnp.float32
BF16 = jnp.bfloat16
I32 = jnp.int32
U32 = jnp.uint32

EPS = 1e-6
LRU_C = 8.0
ROUTE_SCALE = 2.5
GRID_W = 64
HG_CHUNK = 32
N_GROUPS = 8
TOPK_GROUPS = 4
TOP_K = 8
LRU_BLOCKS = 8
HG_HEADS = 4
CONV_W = 4
CONV_PAD_L = 1

SUBLANES = 8
LANES = 128
N_SEG = SUBLANES
VMEM_LIMIT = 56 * 1024 * 1024

PROJ_TILE = 512
POS_TILE = 2048
GMM_BLOCK = 256
SC_GATHER_WINDOW = 64
COMBINE_TILE = 512


def _cparams(sem, vmem=VMEM_LIMIT):
    return pltpu.CompilerParams(dimension_semantics=sem, vmem_limit_bytes=vmem)


def _sigmoid(x):
    return jax.nn.sigmoid(x)


def _silu(x):
    return x * _sigmoid(x)


def _pack_rows(x):
    w = x.shape[1] // 2
    bits = pltpu.bitcast(x.astype(BF16).astype(F32), U32)
    return (bits[:, :w] >> 16) | (bits[:, w:] & jnp.uint32(0xFFFF0000))


def _unpack_rows(p):
    lo = pltpu.bitcast(p << 16, F32)
    hi = pltpu.bitcast(p & jnp.uint32(0xFFFF0000), F32)
    return jnp.concatenate([lo, hi], axis=1)


def _gelu_tanh(x):
    c = 0.7978845608028654
    return 0.5 * x * (1.0 + jnp.tanh(c * (x + 0.044715 * (x * x * x))))


def _ada_kernel(c_ref, w_ref, b_ref, o_ref):
    s = _silu(c_ref[...])
    o_ref[...] = jnp.dot(s, w_ref[...], preferred_element_type=F32,
                         precision=lax.Precision.HIGHEST) + b_ref[...]


def _ada(cs, w, b):
    rows, d = cs.shape
    n = w.shape[1]
    bn = 1024
    return pl.pallas_call(
        _ada_kernel,
        out_shape=jax.ShapeDtypeStruct((rows, n), F32),
        grid=(n // bn,),
        in_specs=[pl.BlockSpec((rows, d), lambda j: (0, 0)),
                  pl.BlockSpec((d, bn), lambda j: (0, j)),
                  pl.BlockSpec((1, bn), lambda j: (0, j))],
        out_specs=pl.BlockSpec((rows, bn), lambda j: (0, j)),
        compiler_params=_cparams(("arbitrary",)),
        name="ada",
    )(cs, w, b)


def _inproj_kernel(x_ref, mod_ref, gain_ref, w_ref, o_ref):
    x = x_ref[0]
    ms = jnp.mean(x * x, axis=-1, keepdims=True)
    y = x * lax.rsqrt(ms + EPS) * gain_ref[...]
    h = y * (1.0 + mod_ref[0, 1:2, :]) + mod_ref[0, 0:1, :]
    o_ref[0] = jnp.dot(h.astype(BF16), w_ref[...], preferred_element_type=F32)


def _inproj(x, mods, gain, w_bf, tm, shared_mod):
    b, s, d = x.shape
    n = w_bf.shape[1]
    mod_map = (lambda bi, i: (0, 0, 0)) if shared_mod else (lambda bi, i: (bi, 0, 0))
    return pl.pallas_call(
        _inproj_kernel,
        out_shape=jax.ShapeDtypeStruct((b, s, n), F32),
        grid=(b, s // tm),
        in_specs=[pl.BlockSpec((1, tm, d), lambda bi, i: (bi, i, 0)),
                  pl.BlockSpec((1, 8, d), mod_map),
                  pl.BlockSpec((1, d), lambda bi, i: (0, 0)),
                  pl.BlockSpec((d, n), lambda bi, i: (0, 0))],
        out_specs=pl.BlockSpec((1, tm, n), lambda bi, i: (bi, i, 0)),
        compiler_params=_cparams(("arbitrary", "arbitrary")),
        name="inproj",
    )(x, mods, gain, w_bf)


LRU_G = 256
LRU_CHUNK = 256


def _seg_pitch(seg_len):
    return seg_len + SUBLANES


def _lane_store(ref, d, rows, val):
    nl = val.shape[-1] // LANES
    for l in range(nl):
        ref[d * nl + l, rows, :] = val[:, l * LANES:(l + 1) * LANES]


def _lane_load(ref, d, rows, nl):
    return jnp.concatenate([ref[d * nl + l, rows, :] for l in range(nl)], axis=-1)


LRU_CONV_PIECE = 64


def _lru_coeffs(pad_ref, t0, rows, cw, cb, wg_ref, bg, sp, a_ref, b_ref, dst0, u_ref):
    for p0 in range(0, rows, LRU_CONV_PIECE):
        win = pad_ref[pl.ds(t0 + p0, LRU_CONV_PIECE + 2 * SUBLANES), :]
        u = cb
        for k in range(CONV_W):
            off = SUBLANES - CONV_PAD_L + k
            u = u + win[off:off + LRU_CONV_PIECE, :] * cw[k:k + 1, :]
        u_ref[p0:p0 + LRU_CONV_PIECE, :] = u
    u = u_ref[0:rows, :]
    ub = u.astype(BF16)
    half_u = 0.5 * u
    for d in range(2):
        tr = jnp.tanh(jnp.dot(ub, wg_ref[2 * d, 0], preferred_element_type=F32)
                      + bg[2 * d:2 * d + 1, :])
        ti = jnp.tanh(jnp.dot(ub, wg_ref[2 * d + 1, 0], preferred_element_type=F32)
                      + bg[2 * d + 1:2 * d + 2, :])
        half_c = (-0.5 * LRU_C) * sp[d:d + 1, :]
        log_a = half_c * tr + half_c
        a = jnp.exp(log_a)
        one_minus_a2 = -jnp.tanh(log_a) * (1.0 + a * a)
        root = jnp.where(one_minus_a2 > 0.0, one_minus_a2 * lax.rsqrt(one_minus_a2), 0.0)
        bb = root * (ti * half_u + half_u)
        _lane_store(a_ref, d, pl.ds(dst0, rows), a)
        _lane_store(b_ref, d, pl.ds(dst0, rows), bb)


def _seg_scan(a_ref, b_ref, h_ref, p_ref, seg_len, pitch, unroll):
    n_lead = a_ref.shape[0]
    nl = n_lead // 2
    zero = jnp.zeros((N_SEG, LANES), F32)
    one = jnp.ones((N_SEG, LANES), F32)
    init = tuple((zero, one) for _ in range(n_lead))

    def step(t, carry):
        out = []
        for i in range(n_lead):
            h, p = carry[i]
            tt = t if i < nl else seg_len - 1 - t
            rows = pl.ds(tt, N_SEG, stride=pitch)
            a = a_ref[i, rows, :]
            h = a * h + b_ref[i, rows, :]
            p = p * a
            if h_ref is not None:
                h_ref[i, rows, :] = h
                p_ref[i, rows, :] = p
            out.append((h, p))
        return tuple(out)

    def body(i, carry):
        for j in range(unroll):
            carry = step(i * unroll + j, carry)
        return carry

    ends = lax.fori_loop(0, seg_len // unroll, body, init)
    h_end = [jnp.concatenate([ends[d * nl + l][0] for l in range(nl)], axis=-1) for d in range(2)]
    p_end = [jnp.concatenate([ends[d * nl + l][1] for l in range(nl)], axis=-1) for d in range(2)]
    return h_end, p_end


def _seg_carries(h_end, p_end, h0, reverse):
    order = range(N_SEG - 1, -1, -1) if reverse else range(N_SEG)
    cin = [None] * N_SEG
    c = h0
    for s in order:
        cin[s] = c
        c = p_end[s:s + 1, :] * c + h_end[s:s + 1, :]
    return cin, c


def _lru_kernel(rx_ref, rg_ref, rc_ref, cw_ref, cb_ref, wg_ref, bg_ref, lam_ref, o_ref,
                padl, padc, u_s, a_l, b_l, h_l, p_l, a_c, b_c):
    s_len = rx_ref.shape[1]
    c_len = rc_ref.shape[1]
    g = rx_ref.shape[2]
    nl = g // LANES
    seg_l = s_len // N_SEG
    seg_c = c_len // N_SEG
    pitch_l = _seg_pitch(seg_l)
    pitch_c = _seg_pitch(seg_c)

    zeros = jnp.zeros((SUBLANES, g), F32)
    padl[0:SUBLANES, :] = zeros
    padl[SUBLANES + s_len:2 * SUBLANES + s_len, :] = zeros
    padc[0:SUBLANES, :] = zeros
    padc[SUBLANES + c_len:2 * SUBLANES + c_len, :] = zeros
    for i in range(s_len // LRU_CHUNK):
        padl[SUBLANES + i * LRU_CHUNK:SUBLANES + (i + 1) * LRU_CHUNK, :] = (
            rx_ref[0, i * LRU_CHUNK:(i + 1) * LRU_CHUNK, :])
    padc[SUBLANES:SUBLANES + c_len, :] = rc_ref[0]

    cw = cw_ref[...]
    cb = cb_ref[...]
    bg = bg_ref[...]
    x = -lam_ref[...]
    sp = jnp.maximum(x, 0.0) + jnp.log(1.0 + jnp.exp(-jnp.abs(x)))

    _lru_coeffs(padc, 0, c_len, cw, cb, wg_ref, bg, sp, a_l, b_l, 0, u_s)
    for i in range(2 * nl):
        for s in range(N_SEG):
            a_c[i, s * pitch_c:s * pitch_c + seg_c, :] = a_l[i, s * seg_c:(s + 1) * seg_c, :]
            b_c[i, s * pitch_c:s * pitch_c + seg_c, :] = b_l[i, s * seg_c:(s + 1) * seg_c, :]
    h_end, p_end = _seg_scan(a_c, b_c, None, None, seg_c, pitch_c, unroll=4)
    zero_row = jnp.zeros((1, g), F32)
    _, h0_f = _seg_carries(h_end[0], p_end[0], zero_row, reverse=False)
    _, h0_b = _seg_carries(h_end[1], p_end[1], zero_row, reverse=True)

    def coeff_body(s, carry):
        t0 = pl.multiple_of(s * seg_l, SUBLANES)
        dst = pl.multiple_of(s * pitch_l, SUBLANES)
        _lru_coeffs(padl, t0, seg_l, cw, cb, wg_ref, bg, sp, a_l, b_l, dst, u_s)
        return carry

    lax.fori_loop(0, N_SEG, coeff_body, 0)
    h_end, p_end = _seg_scan(a_l, b_l, h_l, p_l, seg_l, pitch_l, unroll=4)
    cin_f, _ = _seg_carries(h_end[0], p_end[0], h0_f, reverse=False)
    cin_b, _ = _seg_carries(h_end[1], p_end[1], h0_b, reverse=True)

    for s in range(N_SEG):
        rows = slice(s * pitch_l, s * pitch_l + seg_l)
        h = (_lane_load(h_l, 0, rows, nl) + _lane_load(p_l, 0, rows, nl) * cin_f[s]) + (
            _lane_load(h_l, 1, rows, nl) + _lane_load(p_l, 1, rows, nl) * cin_b[s])
        o_ref[0, s * seg_l:(s + 1) * seg_l, :] = _gelu_tanh(rg_ref[0, s * seg_l:(s + 1) * seg_l, :]) * h


def _lru(projx, projc, cw, cb, wg, bg, lam):
    b, s_len, _ = projx.shape
    c_len = projc.shape[1]
    w = cw.shape[1]
    g = LRU_G
    ng = w // g
    seg_l = s_len // N_SEG
    seg_c = c_len // N_SEG
    assert seg_l == LRU_CHUNK and c_len % (N_SEG * 1) == 0
    return pl.pallas_call(
        _lru_kernel,
        out_shape=jax.ShapeDtypeStruct((b, s_len, w), F32),
        grid=(b, ng),
        in_specs=[pl.BlockSpec((1, s_len, g), lambda bi, j: (bi, 0, j)),
                  pl.BlockSpec((1, s_len, g), lambda bi, j: (bi, 0, ng + j)),
                  pl.BlockSpec((1, c_len, g), lambda bi, j: (bi, 0, j)),
                  pl.BlockSpec((CONV_W, g), lambda bi, j: (0, j)),
                  pl.BlockSpec((1, g), lambda bi, j: (0, j)),
                  pl.BlockSpec((4, 1, g, g), lambda bi, j: (0, j, 0, 0)),
                  pl.BlockSpec((4, g), lambda bi, j: (0, j)),
                  pl.BlockSpec((2, g), lambda bi, j: (0, j))],
        out_specs=pl.BlockSpec((1, s_len, g), lambda bi, j: (bi, 0, j)),
        scratch_shapes=[pltpu.VMEM((s_len + 2 * SUBLANES, g), F32),
                        pltpu.VMEM((c_len + 2 * SUBLANES, g), F32),
                        pltpu.VMEM((max(seg_l, c_len), g), F32),
                        pltpu.VMEM((2 * g // LANES, N_SEG * _seg_pitch(seg_l), LANES), F32),
                        pltpu.VMEM((2 * g // LANES, N_SEG * _seg_pitch(seg_l), LANES), F32),
                        pltpu.VMEM((2 * g // LANES, N_SEG * _seg_pitch(seg_l), LANES), F32),
                        pltpu.VMEM((2 * g // LANES, N_SEG * _seg_pitch(seg_l), LANES), F32),
                        pltpu.VMEM((2 * g // LANES, N_SEG * _seg_pitch(seg_c), LANES), F32),
                        pltpu.VMEM((2 * g // LANES, N_SEG * _seg_pitch(seg_c), LANES), F32)],
        compiler_params=_cparams(("arbitrary", "arbitrary")),
        name="lru",
    )(projx, projx, projc, cw, cb, wg, bg, lam)


HG_PITCH_L = GRID_W + SUBLANES
HG_PITCH_C = SUBLANES


def _hg_gates(z, lb):
    half_t = 0.5 * jnp.tanh(0.5 * z)
    sig = 0.5 + half_t
    nsig = 0.5 - half_t
    logf = jnp.log(lb + (1.0 - lb) * sig)
    k = (1.0 - lb) * nsig
    return logf, k


def _hg_prepass(load, width, pitch, lb, qd, kd, kl, vs, dec):
    n_pos = HG_CHUNK
    for d, zname in enumerate(("zf", "zb")):
        order = range(n_pos) if d == 0 else range(n_pos - 1, -1, -1)
        g = jnp.zeros((width, LANES), F32)
        lbd = lb[d:d + 1, :]
        for p in order:
            logf, k = _hg_gates(load(zname, p), lbd)
            g = g + logf
            qd[d, p * pitch:p * pitch + width, :] = g
            kl[d, p * pitch:p * pitch + width, :] = k
        g_last = g
        dec[d, 0:width, :] = jnp.exp(g_last)
        for p in range(n_pos):
            rows = slice(p * pitch, p * pitch + width)
            gp = qd[d, rows, :]
            k = kl[d, rows, :]
            qd[d, rows, :] = load("q", p) * jnp.exp(gp)
            kd[d, rows, :] = k * jnp.exp(-gp)
            kl[d, rows, :] = k * jnp.exp(g_last - gp)
    for p in range(n_pos):
        vs[p * pitch:p * pitch + width, :] = load("v", p)


HG_GROUP = 4
HG_UNROLL = 8
HG_INTRA_UNROLL = 4
_NT = (((1,), (1,)), ((), ()))
_TN = (((0,), (0,)), ((), ()))


def _hg_group_rows(ref, lead, c0, pitch):
    parts = []
    for j in range(HG_GROUP):
        rows = pl.ds(c0 + j, HG_CHUNK, stride=pitch)
        parts.append(ref[rows, :] if lead is None else ref[lead, rows, :])
    return jnp.concatenate(parts, axis=0)


def _hg_scores(g, pitch, qd, kd, masks, sc_ref):
    c0 = g * HG_GROUP
    total = None
    for d in range(2):
        q = _hg_group_rows(qd, d, c0, pitch).astype(BF16)
        k = _hg_group_rows(kd, d, c0, pitch).astype(BF16)
        sc = jnp.where(masks[d], lax.dot_general(q, k, _NT, preferred_element_type=F32), 0.0)
        total = sc if total is None else total + sc
    sc_ref[g] = total.astype(BF16)


def _hg_summaries(g, pitch, kl, vs, sc_ref, oi, kv):
    c0 = g * HG_GROUP
    v = _hg_group_rows(vs, None, c0, pitch).astype(BF16)
    if oi is not None:
        o = jnp.dot(sc_ref[g], v, preferred_element_type=F32)
        for j in range(HG_GROUP):
            oi[pl.ds(c0 + j, HG_CHUNK, stride=pitch), :] = o[j * HG_CHUNK:(j + 1) * HG_CHUNK, :]
    kls = jnp.concatenate([_hg_group_rows(kl, 0, c0, pitch), _hg_group_rows(kl, 1, c0, pitch)],
                          axis=1).astype(BF16)
    for j in range(HG_GROUP):
        sl = slice(j * HG_CHUNK, (j + 1) * HG_CHUNK)
        both = lax.dot_general(v[sl], kls[sl], _TN, preferred_element_type=F32)
        kv[0, c0 + j] = both[:, :LANES]
        kv[1, c0 + j] = both[:, LANES:]


def _hg_steps(step0, n_steps, n_chunks, pitch, qd, dec, kv, states, ox):
    states = list(states)
    for u in range(n_steps):
        cs = (step0 + u, n_chunks - 1 - (step0 + u))
        if ox is not None:
            rows = [pl.ds(c, HG_CHUNK, stride=pitch) for c in cs]
            q2 = jnp.concatenate([qd[0, rows[0], :], qd[1, rows[1], :]], axis=0).astype(BF16)
            s2 = jnp.concatenate(states, axis=0).astype(BF16)
            o2 = lax.dot_general(q2, s2, _NT, preferred_element_type=F32)
            ox[0, rows[0], :] = o2[:HG_CHUNK, :LANES]
            ox[1, rows[1], :] = o2[HG_CHUNK:, LANES:]
        for d in range(2):
            states[d] = states[d] * dec[d, pl.ds(cs[d], 1), :] + kv[d, cs[d]]
    return tuple(states)


def _hgrn_kernel(q_ref, v_ref, zf_ref, zb_ref, g_ref, qc_ref, vc_ref, zfc_ref, zbc_ref,
                 lbl_ref, gain_ref, o_ref, qd, kd, kl, vs, dec, oi, ox, kv, sc):
    s_len = q_ref.shape[1]
    c_len = qc_ref.shape[1]
    n_col = s_len // HG_CHUNK
    n_cc = c_len // HG_CHUNK
    assert n_col == GRID_W and n_cc == HG_PITCH_C
    assert n_col % HG_GROUP == 0 and n_cc % HG_GROUP == 0 and n_col % HG_UNROLL == 0

    lg = lbl_ref[...]
    m = jnp.max(lg, axis=1, keepdims=True)
    ex = jnp.exp(lg - m)
    lb = ex[:, 0, :] / jnp.sum(ex, axis=1)

    gr = HG_GROUP * HG_CHUNK
    ii = lax.broadcasted_iota(I32, (gr, gr), 0)
    jj = lax.broadcasted_iota(I32, (gr, gr), 1)
    same = (ii // HG_CHUNK) == (jj // HG_CHUNK)
    masks = (jnp.logical_and(same, jj <= ii), jnp.logical_and(same, jj >= ii))

    c_refs = {"q": qc_ref, "v": vc_ref, "zf": zfc_ref, "zb": zbc_ref}

    def load_c(name, p):
        return c_refs[name][0, pl.ds(p, n_cc, stride=HG_CHUNK), :]

    _hg_prepass(load_c, n_cc, HG_PITCH_C, lb, qd, kd, kl, vs, dec)
    for g in range(n_cc // HG_GROUP):
        _hg_summaries(g, HG_PITCH_C, kl, vs, None, None, kv)
    zero = jnp.zeros((LANES, LANES), F32)
    states = _hg_steps(0, n_cc, n_cc, HG_PITCH_C, qd, dec, kv, (zero, zero), None)

    l_refs = {"q": q_ref, "v": v_ref, "zf": zf_ref, "zb": zb_ref}

    def load_l(name, p):
        return l_refs[name][0, p * n_col:(p + 1) * n_col, :]

    _hg_prepass(load_l, n_col, HG_PITCH_L, lb, qd, kd, kl, vs, dec)

    n_it = n_col // (HG_GROUP * HG_INTRA_UNROLL)

    def scores_body(i, carry):
        for u in range(HG_INTRA_UNROLL):
            _hg_scores(i * HG_INTRA_UNROLL + u, HG_PITCH_L, qd, kd, masks, sc)
        return carry

    lax.fori_loop(0, n_it, scores_body, 0)

    def summaries_body(i, carry):
        for u in range(HG_INTRA_UNROLL):
            _hg_summaries(i * HG_INTRA_UNROLL + u, HG_PITCH_L, kl, vs, sc, oi, kv)
        return carry

    lax.fori_loop(0, n_it, summaries_body, 0)

    def step_body(i, carry):
        return _hg_steps(i * HG_UNROLL, HG_UNROLL, n_col, HG_PITCH_L, qd, dec, kv, carry, ox)

    lax.fori_loop(0, n_col // HG_UNROLL, step_body, states)

    gain = gain_ref[...]
    for r in range(HG_CHUNK):
        rows = slice(r * HG_PITCH_L, r * HG_PITCH_L + n_col)
        o = oi[rows, :] + (ox[0, rows, :] + ox[1, rows, :])
        y = o * lax.rsqrt(jnp.mean(o * o, axis=-1, keepdims=True) + EPS) * gain
        o_ref[0, r * n_col:(r + 1) * n_col, :] = y * _silu(g_ref[0, r * n_col:(r + 1) * n_col, :])


def _hgrn(projx, projc, lbl, gain, col0):
    b, s_len, _ = projx.shape
    c_len = projc.shape[1]
    hw = gain.shape[1]
    nh = hw // LANES
    n_slots = lbl.shape[1]

    def xs(k):
        return pl.BlockSpec((1, s_len, LANES), lambda bi, h, k=k: (bi, 0, col0 + k * nh + h))

    def cs(k):
        return pl.BlockSpec((1, c_len, LANES), lambda bi, h, k=k: (bi, 0, col0 + k * nh + h))

    sc_rows = HG_CHUNK * HG_PITCH_L
    return pl.pallas_call(
        _hgrn_kernel,
        out_shape=jax.ShapeDtypeStruct((b, s_len, hw), F32),
        grid=(b, nh),
        in_specs=[xs(0), xs(1), xs(2), xs(3), xs(4), cs(0), cs(1), cs(2), cs(3),
                  pl.BlockSpec((2, n_slots, LANES), lambda bi, h: (0, 0, h)),
                  pl.BlockSpec((1, LANES), lambda bi, h: (0, h))],
        out_specs=pl.BlockSpec((1, s_len, LANES), lambda bi, h: (bi, 0, h)),
        scratch_shapes=[pltpu.VMEM((2, sc_rows, LANES), F32),
                        pltpu.VMEM((2, sc_rows, LANES), F32),
                        pltpu.VMEM((2, sc_rows, LANES), F32),
                        pltpu.VMEM((sc_rows, LANES), F32),
                        pltpu.VMEM((2, HG_PITCH_L, LANES), F32),
                        pltpu.VMEM((sc_rows, LANES), F32),
                        pltpu.VMEM((2, sc_rows, LANES), F32),
                        pltpu.VMEM((2, GRID_W, LANES, LANES), F32),
                        pltpu.VMEM((GRID_W // HG_GROUP, HG_GROUP * HG_CHUNK,
                                    HG_GROUP * HG_CHUNK), BF16)],
        compiler_params=_cparams(("arbitrary", "arbitrary")),
        name="hgrn",
    )(projx, projx, projx, projx, projx, projc, projc, projc, projc, lbl, gain)


def _first_index_of_max(vals, iota, big):
    m = jnp.max(vals, axis=0, keepdims=True)
    idx = jnp.min(jnp.where(vals == m, iota, big), axis=0, keepdims=True)
    return m, idx


def _route(logits_t, rb, n_experts):
    t = logits_t.shape[1]
    gsz = n_experts // N_GROUPS
    neg = -jnp.inf
    scores = _sigmoid(logits_t)
    biased = scores + rb
    iota_g = lax.broadcasted_iota(I32, (gsz, t), 0)
    gscore = []
    for gi in range(N_GROUPS):
        blk = biased[gi * gsz:(gi + 1) * gsz, :]
        m1, i1 = _first_index_of_max(blk, iota_g, gsz)
        m2 = jnp.max(jnp.where(iota_g == i1, neg, blk), axis=0, keepdims=True)
        gscore.append(m1 + m2)
    gs = jnp.concatenate(gscore, axis=0)
    iota_n = lax.broadcasted_iota(I32, (N_GROUPS, t), 0)
    gsel = jnp.zeros((N_GROUPS, t), jnp.bool_)
    for _ in range(TOPK_GROUPS):
        _, gi1 = _first_index_of_max(gs, iota_n, N_GROUPS)
        hit = iota_n == gi1
        gsel = jnp.logical_or(gsel, hit)
        gs = jnp.where(hit, neg, gs)
    emask = jnp.concatenate(
        [jnp.broadcast_to(gsel[gi:gi + 1, :], (gsz, t)) for gi in range(N_GROUPS)], axis=0)
    masked = jnp.where(emask, biased, neg)
    iota_e = lax.broadcasted_iota(I32, (n_experts, t), 0)
    ids, gates = [], []
    sel = jnp.zeros((n_experts, t), F32)
    for _ in range(TOP_K):
        _, ei = _first_index_of_max(masked, iota_e, n_experts)
        hit = iota_e == ei
        ids.append(ei)
        gates.append(jnp.sum(jnp.where(hit, scores, 0.0), axis=0, keepdims=True))
        sel = jnp.where(hit, 1.0, sel)
        masked = jnp.where(hit, neg, masked)
    ids = jnp.concatenate(ids, axis=0)
    gates = jnp.concatenate(gates, axis=0)
    gates = gates / jnp.sum(gates, axis=0, keepdims=True) * ROUTE_SCALE
    return ids, gates, sel


def _outproj_kernel(yl_ref, yh_ref, x_ref, mod_ref, gain_ref, wo_ref, rwt_ref, rb_ref,
                    x1_ref, h2_ref, eid_ref, gate_ref, rank_ref, cnt_ref, carry):
    first = jnp.logical_and(pl.program_id(0) == 0, pl.program_id(1) == 0)

    @pl.when(first)
    def _():
        carry[...] = jnp.zeros(carry.shape, F32)

    y = jnp.dot(yl_ref[0].astype(BF16), wo_ref[0], preferred_element_type=F32)
    y = y + jnp.dot(yh_ref[0].astype(BF16), wo_ref[1], preferred_element_type=F32)
    x1 = x_ref[0] + mod_ref[0, 2:3, :] * y
    x1_ref[0] = x1
    ms = jnp.mean(x1 * x1, axis=-1, keepdims=True)
    h2 = x1 * lax.rsqrt(ms + EPS) * gain_ref[...]
    h2 = h2 * (1.0 + mod_ref[0, 4:5, :]) + mod_ref[0, 3:4, :]
    h2_ref[...] = _pack_rows(h2)

    n_experts = rwt_ref.shape[0]
    t = h2.shape[0]
    logits_t = lax.dot_general(rwt_ref[...], h2.astype(BF16), _NT,
                               preferred_element_type=F32)
    ids, gates, sel = _route(logits_t, rb_ref[...], n_experts)
    eid_ref[...] = ids
    gate_ref[...] = gates

    ti = lax.broadcasted_iota(I32, (t, t), 0)
    tj = lax.broadcasted_iota(I32, (t, t), 1)
    upper = (ti < tj).astype(BF16)
    selb = sel.astype(BF16)
    before = jnp.dot(selb, upper, preferred_element_type=F32) + carry[:, 0:1]
    iota_e = lax.broadcasted_iota(I32, (n_experts, t), 0)
    ranks = [jnp.sum(jnp.where(iota_e == ids[k:k + 1, :], before, 0.0), axis=0, keepdims=True)
             for k in range(TOP_K)]
    rank_ref[...] = jnp.concatenate(ranks, axis=0).astype(I32)
    total = carry[...] + jnp.dot(selb, jnp.ones((t, LANES), BF16), preferred_element_type=F32)
    carry[...] = total
    cnt_ref[...] = total.astype(I32)


def _outproj(ylru, yhg, x, mods, gain, wo, rwt, rb, tm):
    b, s, d = x.shape
    hw = ylru.shape[2]
    e = rwt.shape[0]
    n = b * s
    nt = s // tm
    tok = lambda bi, i: (0, bi * nt + i)
    return pl.pallas_call(
        _outproj_kernel,
        out_shape=(jax.ShapeDtypeStruct((b, s, d), F32),
                   jax.ShapeDtypeStruct((n, d // 2), U32),
                   jax.ShapeDtypeStruct((TOP_K, n), I32),
                   jax.ShapeDtypeStruct((TOP_K, n), F32),
                   jax.ShapeDtypeStruct((TOP_K, n), I32),
                   jax.ShapeDtypeStruct((e, LANES), I32)),
        grid=(b, nt),
        in_specs=[pl.BlockSpec((1, tm, hw), lambda bi, i: (bi, i, 0)),
                  pl.BlockSpec((1, tm, hw), lambda bi, i: (bi, i, 0)),
                  pl.BlockSpec((1, tm, d), lambda bi, i: (bi, i, 0)),
                  pl.BlockSpec((1, 8, d), lambda bi, i: (bi, 0, 0)),
                  pl.BlockSpec((1, d), lambda bi, i: (0, 0)),
                  pl.BlockSpec((2, hw, d), lambda bi, i: (0, 0, 0)),
                  pl.BlockSpec((e, d), lambda bi, i: (0, 0)),
                  pl.BlockSpec((e, 1), lambda bi, i: (0, 0))],
        out_specs=(pl.BlockSpec((1, tm, d), lambda bi, i: (bi, i, 0)),
                   pl.BlockSpec((tm, d // 2), lambda bi, i: (bi * nt + i, 0)),
                   pl.BlockSpec((TOP_K, tm), tok),
                   pl.BlockSpec((TOP_K, tm), tok),
                   pl.BlockSpec((TOP_K, tm), tok),
                   pl.BlockSpec((e, LANES), lambda bi, i: (0, 0))),
        scratch_shapes=[pltpu.VMEM((e, LANES), F32)],
        compiler_params=_cparams(("arbitrary", "arbitrary")),
        name="outproj",
    )(ylru, yhg, x, mods, gain, wo, rwt, rb)


def _pos_kernel(pstart, eid_ref, rank_ref, pos_ref):
    eid = eid_ref[...]

    def body(e, acc):
        return jnp.where(eid == e, pstart[e], acc)

    base = lax.fori_loop(0, pstart.shape[0], body, jnp.zeros(eid.shape, I32))
    pos_ref[...] = base + rank_ref[...]


def _pos(pstarts, eid_t, rank_t, tn):
    k, n = eid_t.shape
    spec = pl.BlockSpec((k, tn), lambda i, ps: (0, i))
    return pl.pallas_call(
        _pos_kernel,
        out_shape=jax.ShapeDtypeStruct((k, n), I32),
        grid_spec=pltpu.PrefetchScalarGridSpec(
            num_scalar_prefetch=1, grid=(n // tn,), in_specs=[spec, spec], out_specs=spec),
        compiler_params=_cparams(("arbitrary",)),
        name="pos",
    )(pstarts, eid_t, rank_t)


def _pad_fill(padstart, padlen, zbuf, xs_ref, sem, wait):
    nbits = zbuf.shape[0].bit_length() - 1
    low_bits = SUBLANES.bit_length() - 1

    def go(cp):
        if wait:
            cp.wait()
        else:
            cp.start()

    def body(e, carry):
        st = padstart[e]
        ln = padlen[e]
        end = st + ln
        for bit in range(nbits - 1, low_bits - 1, -1):
            size = 1 << bit
            back = ((ln >> (bit + 1)) << (bit + 1)) + size

            @pl.when((ln & size) != 0)
            def _():
                start = pl.multiple_of(end - back, SUBLANES)
                go(pltpu.make_async_copy(zbuf.at[pl.ds(0, size)],
                                         xs_ref.at[pl.ds(start, size)], sem))
        for j in range(SUBLANES - 1):
            @pl.when(j < (ln & (SUBLANES - 1)))
            def _():
                go(pltpu.make_async_copy(zbuf.at[pl.ds(0, 1)], xs_ref.at[pl.ds(st + j, 1)], sem))
        return carry

    lax.fori_loop(0, padstart.shape[0], body, 0)


def _padfill_kernel(padstart, padlen, xs_in, xs_ref, zbuf, sem):
    del xs_in
    zbuf[...] = jnp.zeros(zbuf.shape, zbuf.dtype)
    _pad_fill(padstart, padlen, zbuf, xs_ref, sem, wait=False)
    _pad_fill(padstart, padlen, zbuf, xs_ref, sem, wait=True)


def _padfill(xs, padstart, padlen, bm):
    return pl.pallas_call(
        _padfill_kernel,
        out_shape=jax.ShapeDtypeStruct(xs.shape, xs.dtype),
        grid_spec=pltpu.PrefetchScalarGridSpec(
            num_scalar_prefetch=2,
            grid=(1,),
            in_specs=[pl.BlockSpec(memory_space=pl.ANY)],
            out_specs=pl.BlockSpec(memory_space=pl.ANY),
            scratch_shapes=[pltpu.VMEM((bm, xs.shape[1]), xs.dtype), pltpu.SemaphoreType.DMA(())]),
        input_output_aliases={2: 0},
        compiler_params=_cparams(("arbitrary",)),
        name="padfill",
    )(padstart, padlen, xs)


SC_SCATTER_WINDOW = 128


def _sc_mesh():
    return plsc.VectorSubcoreMesh(core_axis_name="core", subcore_axis_name="subcore")


def _sc_scatter_rows(rows, pos, cap):
    n, dw = rows.shape
    top_k = pos.shape[0]
    mesh = _sc_mesh()
    n_workers = mesh.num_cores * mesh.num_subcores
    win = SC_SCATTER_WINDOW
    per_w = n // n_workers
    n_chunks = per_w // win
    assert per_w * n_workers == n and n_chunks * win == per_w
    pos_w = pos.reshape(top_k, n_workers, n_chunks, win).transpose(1, 2, 0, 3)
    pos_w = pos_w.reshape(n_workers, n_chunks * top_k, win)

    @functools.partial(pl.kernel, mesh=mesh,
                       out_type=jax.ShapeDtypeStruct((cap, dw), rows.dtype),
                       scratch_types=[pltpu.VMEM((n_chunks * top_k, win), I32),
                                      pltpu.VMEM((win, dw), rows.dtype),
                                      pltpu.SemaphoreType.DMA])
    def scatter(r_hbm, p_hbm, o_hbm, idx_v, rows_v, sem):
        wid = lax.axis_index("subcore") * mesh.num_cores + lax.axis_index("core")
        pltpu.sync_copy(p_hbm.at[wid], idx_v)

        @pl.loop(0, n_chunks)
        def _(c):
            pltpu.sync_copy(r_hbm.at[pl.ds(wid * per_w + c * win, win)], rows_v)
            copies = [pltpu.async_copy(rows_v, o_hbm.at[idx_v.at[c * top_k + k]], sem)
                      for k in range(top_k)]
            for cp in copies:
                cp.wait()

    return scatter(rows, pos_w)


GMM_PAIR = 4
GMM_SLOTS = 16
GMM_AHEAD = GMM_SLOTS - GMM_PAIR


def _gmm_rows(xw, w13b, w2b, de):
    xb = _unpack_rows(xw).astype(BF16)
    u = jnp.dot(xb, w13b[...], preferred_element_type=F32)
    hmid = _silu(u[:, :de]) * u[:, de:]
    return _pack_rows(jnp.dot(hmid.astype(BF16), w2b[...], preferred_element_type=F32))


def _gmm_kernel(blk0, nblk, w13_ref, w2_ref, xs_ref, ys_ref, w13b, w2b, xbuf, ybuf, isem, osem):
    e = pl.program_id(0)
    n_e = pl.num_programs(0)
    slots, bm = xbuf.shape[0], xbuf.shape[1]
    de = w2_ref.shape[1]
    nb = nblk[e]
    b0 = blk0[e]
    total = blk0[n_e - 1] + nblk[n_e - 1]

    def x_copy(g):
        rows = pl.ds(pl.multiple_of(g * bm, bm), bm)
        slot = g & (slots - 1)
        return pltpu.make_async_copy(xs_ref.at[rows], xbuf.at[slot], isem.at[slot])

    def y_copy(g):
        rows = pl.ds(pl.multiple_of(g * bm, bm), bm)
        slot = g & (slots - 1)
        return pltpu.make_async_copy(ybuf.at[slot], ys_ref.at[rows], osem.at[slot])

    @pl.when(e == 0)
    def _():
        for j in range(GMM_AHEAD):
            @pl.when(j < total)
            def _():
                x_copy(j).start()

    w13b[...] = w13_ref[0].astype(BF16)
    w2b[...] = w2_ref[0].astype(BF16)

    def run_blocks(g0, count):
        gs = [g0 + j for j in range(count)]
        for g in gs:
            x_copy(g).wait()
        for g in gs:
            @pl.when(g + GMM_AHEAD < total)
            def _():
                x_copy(g + GMM_AHEAD).start()

            @pl.when(g >= slots)
            def _():
                y_copy(g - slots).wait()
        for g in gs:
            slot = g & (slots - 1)
            ybuf[slot] = _gmm_rows(xbuf[slot], w13b, w2b, de)
        for g in gs:
            y_copy(g).start()

    def pair_body(i, carry):
        run_blocks(b0 + GMM_PAIR * i, GMM_PAIR)
        return carry

    lax.fori_loop(0, nb // GMM_PAIR, pair_body, 0)
    for r in range(1, GMM_PAIR):
        @pl.when(nb % GMM_PAIR == r)
        def _():
            run_blocks(b0 + nb - r, r)

    @pl.when(e == n_e - 1)
    def _():
        for j in range(1, slots + 1):
            @pl.when(total - j >= 0)
            def _():
                y_copy(total - j).wait()


def _gmm(xs, w13, w2, blk0, nblk, bm):
    cap, dw = xs.shape
    e, d, de2 = w13.shape
    de = w2.shape[1]
    return pl.pallas_call(
        _gmm_kernel,
        out_shape=jax.ShapeDtypeStruct((cap, dw), U32),
        grid_spec=pltpu.PrefetchScalarGridSpec(
            num_scalar_prefetch=2,
            grid=(e,),
            in_specs=[pl.BlockSpec((1, d, de2), lambda i, b0, nb: (i, 0, 0)),
                      pl.BlockSpec((1, de, d), lambda i, b0, nb: (i, 0, 0)),
                      pl.BlockSpec(memory_space=pl.ANY)],
            out_specs=pl.BlockSpec(memory_space=pl.ANY),
            scratch_shapes=[pltpu.VMEM((d, de2), BF16), pltpu.VMEM((de, d), BF16),
                            pltpu.VMEM((GMM_SLOTS, bm, dw), U32),
                            pltpu.VMEM((GMM_SLOTS, bm, dw), U32),
                            pltpu.SemaphoreType.DMA((GMM_SLOTS,)),
                            pltpu.SemaphoreType.DMA((GMM_SLOTS,))]),
        compiler_params=_cparams(("arbitrary",)),
        name="gmm",
    )(blk0, nblk, w13, w2, xs)


def _sc_gather_rows(table, idx, window):
    n_idx = idx.shape[0]
    dw = table.shape[1]
    mesh = _sc_mesh()
    n_workers = mesh.num_cores * mesh.num_subcores
    per_w = n_idx // n_workers
    n_chunks = per_w // window
    assert per_w * n_workers == n_idx and n_chunks * window == per_w and n_chunks % 2 == 0

    @functools.partial(pl.kernel, mesh=mesh,
                       out_type=jax.ShapeDtypeStruct((n_idx, dw), table.dtype),
                       scratch_types=[pltpu.VMEM((per_w,), I32),
                                      pltpu.VMEM((window, dw), table.dtype),
                                      pltpu.VMEM((window, dw), table.dtype),
                                      pltpu.SemaphoreType.DMA, pltpu.SemaphoreType.DMA,
                                      pltpu.SemaphoreType.DMA, pltpu.SemaphoreType.DMA])
    def gather(x_hbm, i_hbm, o_hbm, idx_v, rows0, rows1, gs0, gs1, os0, os1):
        wid = lax.axis_index("subcore") * mesh.num_cores + lax.axis_index("core")
        base = wid * per_w
        pltpu.sync_copy(i_hbm.at[pl.ds(base, per_w)], idx_v)
        bufs = ((rows0, gs0, os0), (rows1, gs1, os1))

        def fetch(j, slot):
            rows, gs, _ = bufs[slot]
            off = pl.multiple_of(j * window, window)
            return pltpu.async_copy(x_hbm.at[idx_v.at[pl.ds(off, window)]], rows, gs)

        def fetch_wait(slot):
            rows, gs, _ = bufs[slot]
            pltpu.make_async_copy(x_hbm.at[idx_v.at[pl.ds(0, window)]], rows, gs).wait()

        def put(j, slot):
            rows, _, osem = bufs[slot]
            off = pl.multiple_of(j * window, window)
            return pltpu.async_copy(rows, o_hbm.at[pl.ds(base + off, window)], osem)

        def put_wait(slot):
            rows, _, osem = bufs[slot]
            pltpu.make_async_copy(rows, o_hbm.at[pl.ds(base, window)], osem).wait()

        fetch(0, 0)

        @pl.loop(0, n_chunks, step=2)
        def _(j):
            @pl.when(j > 0)
            def _():
                put_wait(1)
            fetch(j + 1, 1)
            fetch_wait(0)
            put(j, 0)
            put_wait(0)

            @pl.when(j + 2 < n_chunks)
            def _():
                fetch(j + 2, 0)
            fetch_wait(1)
            put(j + 1, 1)

        put_wait(1)

    return gather(table, idx)


def _combine_kernel(gate_ref, x1_ref, h2_ref, mod_ref, gain_ref, sw13_ref, sw2_ref, yg_ref, o_ref):
    ds_ = sw2_ref.shape[0]
    hb = _unpack_rows(h2_ref[...]).astype(BF16)
    u = jnp.dot(hb, sw13_ref[...], preferred_element_type=F32)
    hmid = _silu(u[:, :ds_]) * u[:, ds_:]
    y = jnp.dot(hmid.astype(BF16), sw2_ref[...], preferred_element_type=F32)

    gate = gate_ref[...]
    moe = gate[:, 0:1] * _unpack_rows(yg_ref[0])
    for k in range(1, TOP_K):
        moe = moe + gate[:, k:k + 1] * _unpack_rows(yg_ref[k])
    xo = x1_ref[...] + mod_ref[0, 5:6, :] * (moe + y)
    ms = jnp.mean(xo * xo, axis=-1, keepdims=True)
    o_ref[...] = xo * lax.rsqrt(ms + EPS) * gain_ref[...]


def _combine(gate_tok, x1, h2, mods, gain, sw13, sw2, yg, tc, tiles_per_batch):
    n, d = x1.shape
    dw = h2.shape[1]
    nt = n // tc
    ds2 = sw13.shape[1]
    ds_ = sw2.shape[0]
    return pl.pallas_call(
        _combine_kernel,
        out_shape=jax.ShapeDtypeStruct((n, d), F32),
        grid=(nt,),
        in_specs=[pl.BlockSpec((tc, TOP_K), lambda i: (i, 0)),
                  pl.BlockSpec((tc, d), lambda i: (i, 0)),
                  pl.BlockSpec((tc, dw), lambda i: (i, 0)),
                  pl.BlockSpec((1, 8, d), lambda i: (i // tiles_per_batch, 0, 0)),
                  pl.BlockSpec((1, d), lambda i: (0, 0)),
                  pl.BlockSpec((d, ds2), lambda i: (0, 0)),
                  pl.BlockSpec((ds_, d), lambda i: (0, 0)),
                  pl.BlockSpec((TOP_K, tc, dw), lambda i: (0, i, 0))],
        out_specs=pl.BlockSpec((tc, d), lambda i: (i, 0)),
        compiler_params=_cparams(("arbitrary",)),
        name="combine",
    )(gate_tok, x1, h2, mods, gain, sw13, sw2, yg)


def _block_diag_pairs(w, per):
    nb, bd, _ = w.shape
    w = w.reshape(nb // per, per, bd, bd)
    eye = jnp.eye(per, dtype=w.dtype)
    out = jnp.einsum("gpij,pq->gpiqj", w, eye)
    return out.reshape(nb // per, per * bd, per * bd)


def kernel(x, c, ctx, c_ctx, ada_w, ada_b, norm_mix, norm_ffn, norm_final, w_in, w_out,
           lru_conv_w, lru_conv_b, lru_wa, lru_ba, lru_wx, lru_bx, lru_lambda,
           hgrn_lb_logits, hgrn_norm, router_w, router_b, exp_w13, exp_w2, shared_w13, shared_w2):
    assert ada_w.shape[0] == 1, "single-layer block"
    b, s, d = x.shape
    n = b * s
    lru_w = lru_conv_w.shape[2]
    n_experts = router_w.shape[2]

    rows = -(-(b + 1) // SUBLANES) * SUBLANES
    cs = jnp.zeros((rows, d), F32).at[:b].set(c).at[b].set(c_ctx)
    mod = _ada(cs, ada_w[0], ada_b[0][None, :]).reshape(rows, 6, d)
    mods = jnp.pad(mod, ((0, 0), (0, 2), (0, 0)))

    w_in_bf = w_in[0].astype(BF16)
    gain1 = norm_mix[0][None, :]
    projx = _inproj(x, mods, gain1, w_in_bf, PROJ_TILE, shared_mod=False)
    projc = _inproj(ctx, mods[b:b + 1], gain1, w_in_bf, ctx.shape[1], shared_mod=True)

    per = LRU_G // (lru_w // LRU_BLOCKS)
    wg = jnp.stack([_block_diag_pairs(lru_wa[0, 0], per), _block_diag_pairs(lru_wx[0, 0], per),
                    _block_diag_pairs(lru_wa[0, 1], per), _block_diag_pairs(lru_wx[0, 1], per)]
                   )
    wg = (0.5 * wg).astype(BF16)
    bg = 0.5 * jnp.stack([lru_ba[0, 0], lru_bx[0, 0], lru_ba[0, 1], lru_bx[0, 1]])
    ylru = _lru(projx, projc, lru_conv_w[0], lru_conv_b[0][None, :], wg, bg, lru_lambda[0])

    yhg = _hgrn(projx, projc, hgrn_lb_logits, hgrn_norm[0][None, :], (2 * lru_w) // LANES)

    wo = w_out[0].astype(BF16).reshape(2, lru_w, d)
    x1, h2, eid_t, gate_t, rank_t, cnt = _outproj(
        ylru, yhg, x, mods, norm_ffn[0][None, :], wo,
        router_w[0].T.astype(BF16), router_b[0][:, None], PROJ_TILE)

    bm = GMM_BLOCK
    counts = cnt[:, 0]
    padded = (counts + bm - 1) // bm * bm
    pends = jnp.cumsum(padded)
    pstarts = pends - padded
    pos = _pos(pstarts, eid_t, rank_t, POS_TILE)
    cap = n * TOP_K + n_experts * bm

    xs = _padfill(_sc_scatter_rows(h2, pos, cap), pstarts + counts, padded - counts, bm)
    ys = _gmm(xs, exp_w13[0], exp_w2[0], pstarts // bm, padded // bm, bm)

    yg = _sc_gather_rows(ys, pos.reshape(-1), SC_GATHER_WINDOW).reshape(TOP_K, n, d // 2)
    tc = COMBINE_TILE
    out = _combine(gate_t.T, x1.reshape(n, d), h2, mods,
                   norm_final[None, :], shared_w13[0].astype(BF16), shared_w2[0].astype(BF16),
                   yg, tc, s // tc)
    return out.reshape(b, s, d)
```

```python
import functools

import jax
import jax.numpy as jnp
from jax import lax
from jax.experimental import pallas as pl
from jax.experimental.pallas import tpu as pltpu
from jax.experimental.pallas import tpu_sc as plsc

F32 = jnp.float32
BF16 = jnp.bfloat16
I32 = jnp.int32
U32 = jnp.uint32

EPS = 1e-6
LRU_C = 8.0
ROUTE_SCALE = 2.5
GRID_W = 64
HG_CHUNK = 32
N_GROUPS = 8
TOPK_GROUPS = 4
TOP_K = 8
LRU_BLOCKS = 8
HG_HEADS = 4
CONV_W = 4
CONV_PAD_L = 1

SUBLANES = 8
LANES = 128
N_SEG = SUBLANES
VMEM_LIMIT = 56 * 1024 * 1024

PROJ_TILE = 512
POS_TILE = 2048
GMM_BLOCK = 256
SC_GATHER_WINDOW = 64
COMBINE_TILE = 512


def _cparams(sem, vmem=VMEM_LIMIT):
    return pltpu.CompilerParams(dimension_semantics=sem, vmem_limit_bytes=vmem)


def _sigmoid(x):
    return jax.nn.sigmoid(x)


def _silu(x):
    return x * _sigmoid(x)


def _pack_rows(x):
    w = x.shape[1] // 2
    bits = pltpu.bitcast(x.astype(BF16).astype(F32), U32)
    return (bits[:, :w] >> 16) | (bits[:, w:] & jnp.uint32(0xFFFF0000))


def _unpack_rows(p):
    lo = pltpu.bitcast(p << 16, F32)
    hi = pltpu.bitcast(p & jnp.uint32(0xFFFF0000), F32)
    return jnp.concatenate([lo, hi], axis=1)


def _gelu_tanh(x):
    c = 0.7978845608028654
    return 0.5 * x * (1.0 + jnp.tanh(c * (x + 0.044715 * (x * x * x))))


def _ada_kernel(c_ref, w_ref, b_ref, o_ref):
    s = _silu(c_ref[...])
    o_ref[...] = jnp.dot(s, w_ref[...], preferred_element_type=F32,
                         precision=lax.Precision.HIGHEST) + b_ref[...]


def _ada(cs, w, b):
    rows, d = cs.shape
    n = w.shape[1]
    bn = 1024
    return pl.pallas_call(
        _ada_kernel,
        out_shape=jax.ShapeDtypeStruct((rows, n), F32),
        grid=(n // bn,),
        in_specs=[pl.BlockSpec((rows, d), lambda j: (0, 0)),
                  pl.BlockSpec((d, bn), lambda j: (0, j)),
                  pl.BlockSpec((1, bn), lambda j: (0, j))],
        out_specs=pl.BlockSpec((rows, bn), lambda j: (0, j)),
        compiler_params=_cparams(("arbitrary",)),
        name="ada",
    )(cs, w, b)


def _inproj_kernel(x_ref, mod_ref, gain_ref, w_ref, o_ref):
    x = x_ref[0]
    ms = jnp.mean(x * x, axis=-1, keepdims=True)
    y = x * lax.rsqrt(ms + EPS) * gain_ref[...]
    h = y * (1.0 + mod_ref[0, 1:2, :]) + mod_ref[0, 0:1, :]
    o_ref[0] = jnp.dot(h.astype(BF16), w_ref[...], preferred_element_type=F32)


def _inproj(x, mods, gain, w_bf, tm, shared_mod):
    b, s, d = x.shape
    n = w_bf.shape[1]
    mod_map = (lambda bi, i: (0, 0, 0)) if shared_mod else (lambda bi, i: (bi, 0, 0))
    return pl.pallas_call(
        _inproj_kernel,
        out_shape=jax.ShapeDtypeStruct((b, s, n), F32),
        grid=(b, s // tm),
        in_specs=[pl.BlockSpec((1, tm, d), lambda bi, i: (bi, i, 0)),
                  pl.BlockSpec((1, 8, d), mod_map),
                  pl.BlockSpec((1, d), lambda bi, i: (0, 0)),
                  pl.BlockSpec((d, n), lambda bi, i: (0, 0))],
        out_specs=pl.BlockSpec((1, tm, n), lambda bi, i: (bi, i, 0)),
        compiler_params=_cparams(("arbitrary", "arbitrary")),
        name="inproj",
    )(x, mods, gain, w_bf)


LRU_G = 256
LRU_CHUNK = 256


def _seg_pitch(seg_len):
    return seg_len + SUBLANES


def _lane_store(ref, d, rows, val):
    nl = val.shape[-1] // LANES
    for l in range(nl):
        ref[d * nl + l, rows, :] = val[:, l * LANES:(l + 1) * LANES]


def _lane_load(ref, d, rows, nl):
    return jnp.concatenate([ref[d * nl + l, rows, :] for l in range(nl)], axis=-1)


LRU_CONV_PIECE = 64


def _lru_coeffs(pad_ref, t0, rows, cw, cb, wg_ref, bg, sp, a_ref, b_ref, dst0, u_ref):
    for p0 in range(0, rows, LRU_CONV_PIECE):
        win = pad_ref[pl.ds(t0 + p0, LRU_CONV_PIECE + 2 * SUBLANES), :]
        u = cb
        for k in range(CONV_W):
            off = SUBLANES - CONV_PAD_L + k
            u = u + win[off:off + LRU_CONV_PIECE, :] * cw[k:k + 1, :]
        u_ref[p0:p0 + LRU_CONV_PIECE, :] = u
    u = u_ref[0:rows, :]
    ub = u.astype(BF16)
    half_u = 0.5 * u
    for d in range(2):
        tr = jnp.tanh(jnp.dot(ub, wg_ref[2 * d, 0], preferred_element_type=F32)
                      + bg[2 * d:2 * d + 1, :])
        ti = jnp.tanh(jnp.dot(ub, wg_ref[2 * d + 1, 0], preferred_element_type=F32)
                      + bg[2 * d + 1:2 * d + 2, :])
        half_c = (-0.5 * LRU_C) * sp[d:d + 1, :]
        log_a = half_c * tr + half_c
        a = jnp.exp(log_a)
        one_minus_a2 = -jnp.tanh(log_a) * (1.0 + a * a)
        root = jnp.where(one_minus_a2 > 0.0, one_minus_a2 * lax.rsqrt(one_minus_a2), 0.0)
        bb = root * (ti * half_u + half_u)
        _lane_store(a_ref, d, pl.ds(dst0, rows), a)
        _lane_store(b_ref, d, pl.ds(dst0, rows), bb)


def _seg_scan(a_ref, b_ref, h_ref, p_ref, seg_len, pitch, unroll):
    n_lead = a_ref.shape[0]
    nl = n_lead // 2
    zero = jnp.zeros((N_SEG, LANES), F32)
    one = jnp.ones((N_SEG, LANES), F32)
    init = tuple((zero, one) for _ in range(n_lead))

    def step(t, carry):
        out = []
        for i in range(n_lead):
            h, p = carry[i]
            tt = t if i < nl else seg_len - 1 - t
            rows = pl.ds(tt, N_SEG, stride=pitch)
            a = a_ref[i, rows, :]
            h = a * h + b_ref[i, rows, :]
            p = p * a
            if h_ref is not None:
                h_ref[i, rows, :] = h
                p_ref[i, rows, :] = p
            out.append((h, p))
        return tuple(out)

    def body(i, carry):
        for j in range(unroll):
            carry = step(i * unroll + j, carry)
        return carry

    ends = lax.fori_loop(0, seg_len // unroll, body, init)
    h_end = [jnp.concatenate([ends[d * nl + l][0] for l in range(nl)], axis=-1) for d in range(2)]
    p_end = [jnp.concatenate([ends[d * nl + l][1] for l in range(nl)], axis=-1) for d in range(2)]
    return h_end, p_end


def _seg_carries(h_end, p_end, h0, reverse):
    order = range(N_SEG - 1, -1, -1) if reverse else range(N_SEG)
    cin = [None] * N_SEG
    c = h0
    for s in order:
        cin[s] = c
        c = p_end[s:s + 1, :] * c + h_end[s:s + 1, :]
    return cin, c


def _lru_kernel(rx_ref, rg_ref, rc_ref, cw_ref, cb_ref, wg_ref, bg_ref, lam_ref, o_ref,
                padl, padc, u_s, a_l, b_l, h_l, p_l, a_c, b_c):
    s_len = rx_ref.shape[1]
    c_len = rc_ref.shape[1]
    g = rx_ref.shape[2]
    nl = g // LANES
    seg_l = s_len // N_SEG
    seg_c = c_len // N_SEG
    pitch_l = _seg_pitch(seg_l)
    pitch_c = _seg_pitch(seg_c)

    zeros = jnp.zeros((SUBLANES, g), F32)
    padl[0:SUBLANES, :] = zeros
    padl[SUBLANES + s_len:2 * SUBLANES + s_len, :] = zeros
    padc[0:SUBLANES, :] = zeros
    padc[SUBLANES + c_len:2 * SUBLANES + c_len, :] = zeros
    for i in range(s_len // LRU_CHUNK):
        padl[SUBLANES + i * LRU_CHUNK:SUBLANES + (i + 1) * LRU_CHUNK, :] = (
            rx_ref[0, i * LRU_CHUNK:(i + 1) * LRU_CHUNK, :])
    padc[SUBLANES:SUBLANES + c_len, :] = rc_ref[0]

    cw = cw_ref[...]
    cb = cb_ref[...]
    bg = bg_ref[...]
    x = -lam_ref[...]
    sp = jnp.maximum(x, 0.0) + jnp.log(1.0 + jnp.exp(-jnp.abs(x)))

    _lru_coeffs(padc, 0, c_len, cw, cb, wg_ref, bg, sp, a_l, b_l, 0, u_s)
    for i in range(2 * nl):
        for s in range(N_SEG):
            a_c[i, s * pitch_c:s * pitch_c + seg_c, :] = a_l[i, s * seg_c:(s + 1) * seg_c, :]
            b_c[i, s * pitch_c:s * pitch_c + seg_c, :] = b_l[i, s * seg_c:(s + 1) * seg_c, :]
    h_end, p_end = _seg_scan(a_c, b_c, None, None, seg_c, pitch_c, unroll=4)
    zero_row = jnp.zeros((1, g), F32)
    _, h0_f = _seg_carries(h_end[0], p_end[0], zero_row, reverse=False)
    _, h0_b = _seg_carries(h_end[1], p_end[1], zero_row, reverse=True)

    def coeff_body(s, carry):
        t0 = pl.multiple_of(s * seg_l, SUBLANES)
        dst = pl.multiple_of(s * pitch_l, SUBLANES)
        _lru_coeffs(padl, t0, seg_l, cw, cb, wg_ref, bg, sp, a_l, b_l, dst, u_s)
        return carry

    lax.fori_loop(0, N_SEG, coeff_body, 0)
    h_end, p_end = _seg_scan(a_l, b_l, h_l, p_l, seg_l, pitch_l, unroll=4)
    cin_f, _ = _seg_carries(h_end[0], p_end[0], h0_f, reverse=False)
    cin_b, _ = _seg_carries(h_end[1], p_end[1], h0_b, reverse=True)

    for s in range(N_SEG):
        rows = slice(s * pitch_l, s * pitch_l + seg_l)
        h = (_lane_load(h_l, 0, rows, nl) + _lane_load(p_l, 0, rows, nl) * cin_f[s]) + (
            _lane_load(h_l, 1, rows, nl) + _lane_load(p_l, 1, rows, nl) * cin_b[s])
        o_ref[0, s * seg_l:(s + 1) * seg_l, :] = _gelu_tanh(rg_ref[0, s * seg_l:(s + 1) * seg_l, :]) * h


def _lru(projx, projc, cw, cb, wg, bg, lam):
    b, s_len, _ = projx.shape
    c_len = projc.shape[1]
    w = cw.shape[1]
    g = LRU_G
    ng = w // g
    seg_l = s_len // N_SEG
    seg_c = c_len // N_SEG
    assert seg_l == LRU_CHUNK and c_len % (N_SEG * 1) == 0
    return pl.pallas_call(
        _lru_kernel,
        out_shape=jax.ShapeDtypeStruct((b, s_len, w), F32),
        grid=(b, ng),
        in_specs=[pl.BlockSpec((1, s_len, g), lambda bi, j: (bi, 0, j)),
                  pl.BlockSpec((1, s_len, g), lambda bi, j: (bi, 0, ng + j)),
                  pl.BlockSpec((1, c_len, g), lambda bi, j: (bi, 0, j)),
                  pl.BlockSpec((CONV_W, g), lambda bi, j: (0, j)),
                  pl.BlockSpec((1, g), lambda bi, j: (0, j)),
                  pl.BlockSpec((4, 1, g, g), lambda bi, j: (0, j, 0, 0)),
                  pl.BlockSpec((4, g), lambda bi, j: (0, j)),
                  pl.BlockSpec((2, g), lambda bi, j: (0, j))],
        out_specs=pl.BlockSpec((1, s_len, g), lambda bi, j: (bi, 0, j)),
        scratch_shapes=[pltpu.VMEM((s_len + 2 * SUBLANES, g), F32),
                        pltpu.VMEM((c_len + 2 * SUBLANES, g), F32),
                        pltpu.VMEM((max(seg_l, c_len), g), F32),
                        pltpu.VMEM((2 * g // LANES, N_SEG * _seg_pitch(seg_l), LANES), F32),
                        pltpu.VMEM((2 * g // LANES, N_SEG * _seg_pitch(seg_l), LANES), F32),
                        pltpu.VMEM((2 * g // LANES, N_SEG * _seg_pitch(seg_l), LANES), F32),
                        pltpu.VMEM((2 * g // LANES, N_SEG * _seg_pitch(seg_l), LANES), F32),
                        pltpu.VMEM((2 * g // LANES, N_SEG * _seg_pitch(seg_c), LANES), F32),
                        pltpu.VMEM((2 * g // LANES, N_SEG * _seg_pitch(seg_c), LANES), F32)],
        compiler_params=_cparams(("arbitrary", "arbitrary")),
        name="lru",
    )(projx, projx, projc, cw, cb, wg, bg, lam)


HG_PITCH_L = GRID_W + SUBLANES
HG_PITCH_C = SUBLANES


def _hg_gates(z, lb):
    half_t = 0.5 * jnp.tanh(0.5 * z)
    sig = 0.5 + half_t
    nsig = 0.5 - half_t
    logf = jnp.log(lb + (1.0 - lb) * sig)
    k = (1.0 - lb) * nsig
    return logf, k


def _hg_prepass(load, width, pitch, lb, qd, kd, kl, vs, dec):
    n_pos = HG_CHUNK
    for d, zname in enumerate(("zf", "zb")):
        order = range(n_pos) if d == 0 else range(n_pos - 1, -1, -1)
        g = jnp.zeros((width, LANES), F32)
        lbd = lb[d:d + 1, :]
        for p in order:
            logf, k = _hg_gates(load(zname, p), lbd)
            g = g + logf
            qd[d, p * pitch:p * pitch + width, :] = g
            kl[d, p * pitch:p * pitch + width, :] = k
        g_last = g
        dec[d, 0:width, :] = jnp.exp(g_last)
        for p in range(n_pos):
            rows = slice(p * pitch, p * pitch + width)
            gp = qd[d, rows, :]
            k = kl[d, rows, :]
            qd[d, rows, :] = load("q", p) * jnp.exp(gp)
            kd[d, rows, :] = k * jnp.exp(-gp)
            kl[d, rows, :] = k * jnp.exp(g_last - gp)
    for p in range(n_pos):
        vs[p * pitch:p * pitch + width, :] = load("v", p)


HG_GROUP = 4
HG_UNROLL = 8
HG_INTRA_UNROLL = 4
_NT = (((1,), (1,)), ((), ()))
_TN = (((0,), (0,)), ((), ()))


def _hg_group_rows(ref, lead, c0, pitch):
    parts = []
    for j in range(HG_GROUP):
        rows = pl.ds(c0 + j, HG_CHUNK, stride=pitch)
        parts.append(ref[rows, :] if lead is None else ref[lead, rows, :])
    return jnp.concatenate(parts, axis=0)


def _hg_scores(g, pitch, qd, kd, masks, sc_ref):
    c0 = g * HG_GROUP
    total = None
    for d in range(2):
        q = _hg_group_rows(qd, d, c0, pitch).astype(BF16)
        k = _hg_group_rows(kd, d, c0, pitch).astype(BF16)
        sc = jnp.where(masks[d], lax.dot_general(q, k, _NT, preferred_element_type=F32), 0.0)
        total = sc if total is None else total + sc
    sc_ref[g] = total.astype(BF16)


def _hg_summaries(g, pitch, kl, vs, sc_ref, oi, kv):
    c0 = g * HG_GROUP
    v = _hg_group_rows(vs, None, c0, pitch).astype(BF16)
    if oi is not None:
        o = jnp.dot(sc_ref[g], v, preferred_element_type=F32)
        for j in range(HG_GROUP):
            oi[pl.ds(c0 + j, HG_CHUNK, stride=pitch), :] = o[j * HG_CHUNK:(j + 1) * HG_CHUNK, :]
    kls = jnp.concatenate([_hg_group_rows(kl, 0, c0, pitch), _hg_group_rows(kl, 1, c0, pitch)],
                          axis=1).astype(BF16)
    for j in range(HG_GROUP):
        sl = slice(j * HG_CHUNK, (j + 1) * HG_CHUNK)
        both = lax.dot_general(v[sl], kls[sl], _TN, preferred_element_type=F32)
        kv[0, c0 + j] = both[:, :LANES]
        kv[1, c0 + j] = both[:, LANES:]


def _hg_steps(step0, n_steps, n_chunks, pitch, qd, dec, kv, states, ox):
    states = list(states)
    for u in range(n_steps):
        cs = (step0 + u, n_chunks - 1 - (step0 + u))
        if ox is not None:
            rows = [pl.ds(c, HG_CHUNK, stride=pitch) for c in cs]
            q2 = jnp.concatenate([qd[0, rows[0], :], qd[1, rows[1], :]], axis=0).astype(BF16)
            s2 = jnp.concatenate(states, axis=0).astype(BF16)
            o2 = lax.dot_general(q2, s2, _NT, preferred_element_type=F32)
            ox[0, rows[0], :] = o2[:HG_CHUNK, :LANES]
            ox[1, rows[1], :] = o2[HG_CHUNK:, LANES:]
        for d in range(2):
            states[d] = states[d] * dec[d, pl.ds(cs[d], 1), :] + kv[d, cs[d]]
    return tuple(states)


def _hgrn_kernel(q_ref, v_ref, zf_ref, zb_ref, g_ref, qc_ref, vc_ref, zfc_ref, zbc_ref,
                 lbl_ref, gain_ref, o_ref, qd, kd, kl, vs, dec, oi, ox, kv, sc):
    s_len = q_ref.shape[1]
    c_len = qc_ref.shape[1]
    n_col = s_len // HG_CHUNK
    n_cc = c_len // HG_CHUNK
    assert n_col == GRID_W and n_cc == HG_PITCH_C
    assert n_col % HG_GROUP == 0 and n_cc % HG_GROUP == 0 and n_col % HG_UNROLL == 0

    lg = lbl_ref[...]
    m = jnp.max(lg, axis=1, keepdims=True)
    ex = jnp.exp(lg - m)
    lb = ex[:, 0, :] / jnp.sum(ex, axis=1)

    gr = HG_GROUP * HG_CHUNK
    ii = lax.broadcasted_iota(I32, (gr, gr), 0)
    jj = lax.broadcasted_iota(I32, (gr, gr), 1)
    same = (ii // HG_CHUNK) == (jj // HG_CHUNK)
    masks = (jnp.logical_and(same, jj <= ii), jnp.logical_and(same, jj >= ii))

    c_refs = {"q": qc_ref, "v": vc_ref, "zf": zfc_ref, "zb": zbc_ref}

    def load_c(name, p):
        return c_refs[name][0, pl.ds(p, n_cc, stride=HG_CHUNK), :]

    _hg_prepass(load_c, n_cc, HG_PITCH_C, lb, qd, kd, kl, vs, dec)
    for g in range(n_cc // HG_GROUP):
        _hg_summaries(g, HG_PITCH_C, kl, vs, None, None, kv)
    zero = jnp.zeros((LANES, LANES), F32)
    states = _hg_steps(0, n_cc, n_cc, HG_PITCH_C, qd, dec, kv, (zero, zero), None)

    l_refs = {"q": q_ref, "v": v_ref, "zf": zf_ref, "zb": zb_ref}

    def load_l(name, p):
        return l_refs[name][0, p * n_col:(p + 1) * n_col, :]

    _hg_prepass(load_l, n_col, HG_PITCH_L, lb, qd, kd, kl, vs, dec)

    n_it = n_col // (HG_GROUP * HG_INTRA_UNROLL)

    def scores_body(i, carry):
        for u in range(HG_INTRA_UNROLL):
            _hg_scores(i * HG_INTRA_UNROLL + u, HG_PITCH_L, qd, kd, masks, sc)
        return carry

    lax.fori_loop(0, n_it, scores_body, 0)

    def summaries_body(i, carry):
        for u in range(HG_INTRA_UNROLL):
            _hg_summaries(i * HG_INTRA_UNROLL + u, HG_PITCH_L, kl, vs, sc, oi, kv)
        return carry

    lax.fori_loop(0, n_it, summaries_body, 0)

    def step_body(i, carry):
        return _hg_steps(i * HG_UNROLL, HG_UNROLL, n_col, HG_PITCH_L, qd, dec, kv, carry, ox)

    lax.fori_loop(0, n_col // HG_UNROLL, step_body, states)

    gain = gain_ref[...]
    for r in range(HG_CHUNK):
        rows = slice(r * HG_PITCH_L, r * HG_PITCH_L + n_col)
        o = oi[rows, :] + (ox[0, rows, :] + ox[1, rows, :])
        y = o * lax.rsqrt(jnp.mean(o * o, axis=-1, keepdims=True) + EPS) * gain
        o_ref[0, r * n_col:(r + 1) * n_col, :] = y * _silu(g_ref[0, r * n_col:(r + 1) * n_col, :])


def _hgrn(projx, projc, lbl, gain, col0):
    b, s_len, _ = projx.shape
    c_len = projc.shape[1]
    hw = gain.shape[1]
    nh = hw // LANES
    n_slots = lbl.shape[1]

    def xs(k):
        return pl.BlockSpec((1, s_len, LANES), lambda bi, h, k=k: (bi, 0, col0 + k * nh + h))

    def cs(k):
        return pl.BlockSpec((1, c_len, LANES), lambda bi, h, k=k: (bi, 0, col0 + k * nh + h))

    sc_rows = HG_CHUNK * HG_PITCH_L
    return pl.pallas_call(
        _hgrn_kernel,
        out_shape=jax.ShapeDtypeStruct((b, s_len, hw), F32),
        grid=(b, nh),
        in_specs=[xs(0), xs(1), xs(2), xs(3), xs(4), cs(0), cs(1), cs(2), cs(3),
                  pl.BlockSpec((2, n_slots, LANES), lambda bi, h: (0, 0, h)),
                  pl.BlockSpec((1, LANES), lambda bi, h: (0, h))],
        out_specs=pl.BlockSpec((1, s_len, LANES), lambda bi, h: (bi, 0, h)),
        scratch_shapes=[pltpu.VMEM((2, sc_rows, LANES), F32),
                        pltpu.VMEM((2, sc_rows, LANES), F32),
                        pltpu.VMEM((2, sc_rows, LANES), F32),
                        pltpu.VMEM((sc_rows, LANES), F32),
                        pltpu.VMEM((2, HG_PITCH_L, LANES), F32),
                        pltpu.VMEM((sc_rows, LANES), F32),
                        pltpu.VMEM((2, sc_rows, LANES), F32),
                        pltpu.VMEM((2, GRID_W, LANES, LANES), F32),
                        pltpu.VMEM((GRID_W // HG_GROUP, HG_GROUP * HG_CHUNK,
                                    HG_GROUP * HG_CHUNK), BF16)],
        compiler_params=_cparams(("arbitrary", "arbitrary")),
        name="hgrn",
    )(projx, projx, projx, projx, projx, projc, projc, projc, projc, lbl, gain)


def _first_index_of_max(vals, iota, big):
    m = jnp.max(vals, axis=0, keepdims=True)
    idx = jnp.min(jnp.where(vals == m, iota, big), axis=0, keepdims=True)
    return m, idx


def _route(logits_t, rb, n_experts):
    t = logits_t.shape[1]
    gsz = n_experts // N_GROUPS
    neg = -jnp.inf
    scores = _sigmoid(logits_t)
    biased = scores + rb
    iota_g = lax.broadcasted_iota(I32, (gsz, t), 0)
    gscore = []
    for gi in range(N_GROUPS):
        blk = biased[gi * gsz:(gi + 1) * gsz, :]
        m1, i1 = _first_index_of_max(blk, iota_g, gsz)
        m2 = jnp.max(jnp.where(iota_g == i1, neg, blk), axis=0, keepdims=True)
        gscore.append(m1 + m2)
    gs = jnp.concatenate(gscore, axis=0)
    iota_n = lax.broadcasted_iota(I32, (N_GROUPS, t), 0)
    gsel = jnp.zeros((N_GROUPS, t), jnp.bool_)
    for _ in range(TOPK_GROUPS):
        _, gi1 = _first_index_of_max(gs, iota_n, N_GROUPS)
        hit = iota_n == gi1
        gsel = jnp.logical_or(gsel, hit)
        gs = jnp.where(hit, neg, gs)
    emask = jnp.concatenate(
        [jnp.broadcast_to(gsel[gi:gi + 1, :], (gsz, t)) for gi in range(N_GROUPS)], axis=0)
    masked = jnp.where(emask, biased, neg)
    iota_e = lax.broadcasted_iota(I32, (n_experts, t), 0)
    ids, gates = [], []
    sel = jnp.zeros((n_experts, t), F32)
    for _ in range(TOP_K):
        _, ei = _first_index_of_max(masked, iota_e, n_experts)
        hit = iota_e == ei
        ids.append(ei)
        gates.append(jnp.sum(jnp.where(hit, scores, 0.0), axis=0, keepdims=True))
        sel = jnp.where(hit, 1.0, sel)
        masked = jnp.where(hit, neg, masked)
    ids = jnp.concatenate(ids, axis=0)
    gates = jnp.concatenate(gates, axis=0)
    gates = gates / jnp.sum(gates, axis=0, keepdims=True) * ROUTE_SCALE
    return ids, gates, sel


def _outproj_kernel(yl_ref, yh_ref, x_ref, mod_ref, gain_ref, wo_ref, rwt_ref, rb_ref,
                    x1_ref, h2_ref, eid_ref, gate_ref, rank_ref, cnt_ref, carry):
    first = jnp.logical_and(pl.program_id(0) == 0, pl.program_id(1) == 0)

    @pl.when(first)
    def _():
        carry[...] = jnp.zeros(carry.shape, F32)

    y = jnp.dot(yl_ref[0].astype(BF16), wo_ref[0], preferred_element_type=F32)
    y = y + jnp.dot(yh_ref[0].astype(BF16), wo_ref[1], preferred_element_type=F32)
    x1 = x_ref[0] + mod_ref[0, 2:3, :] * y
    x1_ref[0] = x1
    ms = jnp.mean(x1 * x1, axis=-1, keepdims=True)
    h2 = x1 * lax.rsqrt(ms + EPS) * gain_ref[...]
    h2 = h2 * (1.0 + mod_ref[0, 4:5, :]) + mod_ref[0, 3:4, :]
    h2_ref[...] = _pack_rows(h2)

    n_experts = rwt_ref.shape[0]
    t = h2.shape[0]
    logits_t = lax.dot_general(rwt_ref[...], h2.astype(BF16), _NT,
                               preferred_element_type=F32)
    ids, gates, sel = _route(logits_t, rb_ref[...], n_experts)
    eid_ref[...] = ids
    gate_ref[...] = gates

    ti = lax.broadcasted_iota(I32, (t, t), 0)
    tj = lax.broadcasted_iota(I32, (t, t), 1)
    upper = (ti < tj).astype(BF16)
    selb = sel.astype(BF16)
    before = jnp.dot(selb, upper, preferred_element_type=F32) + carry[:, 0:1]
    iota_e = lax.broadcasted_iota(I32, (n_experts, t), 0)
    ranks = [jnp.sum(jnp.where(iota_e == ids[k:k + 1, :], before, 0.0), axis=0, keepdims=True)
             for k in range(TOP_K)]
    rank_ref[...] = jnp.concatenate(ranks, axis=0).astype(I32)
    total = carry[...] + jnp.dot(selb, jnp.ones((t, LANES), BF16), preferred_element_type=F32)
    carry[...] = total
    cnt_ref[...] = total.astype(I32)


def _outproj(ylru, yhg, x, mods, gain, wo, rwt, rb, tm):
    b, s, d = x.shape
    hw = ylru.shape[2]
    e = rwt.shape[0]
    n = b * s
    nt = s // tm
    tok = lambda bi, i: (0, bi * nt + i)
    return pl.pallas_call(
        _outproj_kernel,
        out_shape=(jax.ShapeDtypeStruct((b, s, d), F32),
                   jax.ShapeDtypeStruct((n, d // 2), U32),
                   jax.ShapeDtypeStruct((TOP_K, n), I32),
                   jax.ShapeDtypeStruct((TOP_K, n), F32),
                   jax.ShapeDtypeStruct((TOP_K, n), I32),
                   jax.ShapeDtypeStruct((e, LANES), I32)),
        grid=(b, nt),
        in_specs=[pl.BlockSpec((1, tm, hw), lambda bi, i: (bi, i, 0)),
                  pl.BlockSpec((1, tm, hw), lambda bi, i: (bi, i, 0)),
                  pl.BlockSpec((1, tm, d), lambda bi, i: (bi, i, 0)),
                  pl.BlockSpec((1, 8, d), lambda bi, i: (bi, 0, 0)),
                  pl.BlockSpec((1, d), lambda bi, i: (0, 0)),
                  pl.BlockSpec((2, hw, d), lambda bi, i: (0, 0, 0)),
                  pl.BlockSpec((e, d), lambda bi, i: (0, 0)),
                  pl.BlockSpec((e, 1), lambda bi, i: (0, 0))],
        out_specs=(pl.BlockSpec((1, tm, d), lambda bi, i: (bi, i, 0)),
                   pl.BlockSpec((tm, d // 2), lambda bi, i: (bi * nt + i, 0)),
                   pl.BlockSpec((TOP_K, tm), tok),
                   pl.BlockSpec((TOP_K, tm), tok),
                   pl.BlockSpec((TOP_K, tm), tok),
                   pl.BlockSpec((e, LANES), lambda bi, i: (0, 0))),
        scratch_shapes=[pltpu.VMEM((e, LANES), F32)],
        compiler_params=_cparams(("arbitrary", "arbitrary")),
        name="outproj",
    )(ylru, yhg, x, mods, gain, wo, rwt, rb)


def _pos_kernel(pstart, eid_ref, rank_ref, pos_ref):
    eid = eid_ref[...]

    def body(e, acc):
        return jnp.where(eid == e, pstart[e], acc)

    base = lax.fori_loop(0, pstart.shape[0], body, jnp.zeros(eid.shape, I32))
    pos_ref[...] = base + rank_ref[...]


def _pos(pstarts, eid_t, rank_t, tn):
    k, n = eid_t.shape
    spec = pl.BlockSpec((k, tn), lambda i, ps: (0, i))
    return pl.pallas_call(
        _pos_kernel,
        out_shape=jax.ShapeDtypeStruct((k, n), I32),
        grid_spec=pltpu.PrefetchScalarGridSpec(
            num_scalar_prefetch=1, grid=(n // tn,), in_specs=[spec, spec], out_specs=spec),
        compiler_params=_cparams(("arbitrary",)),
        name="pos",
    )(pstarts, eid_t, rank_t)


def _pad_fill(padstart, padlen, zbuf, xs_ref, sem, wait):
    nbits = zbuf.shape[0].bit_length() - 1
    low_bits = SUBLANES.bit_length() - 1

    def go(cp):
        if wait:
            cp.wait()
        else:
            cp.start()

    def body(e, carry):
        st = padstart[e]
        ln = padlen[e]
        end = st + ln
        for bit in range(nbits - 1, low_bits - 1, -1):
            size = 1 << bit
            back = ((ln >> (bit + 1)) << (bit + 1)) + size

            @pl.when((ln & size) != 0)
            def _():
                start = pl.multiple_of(end - back, SUBLANES)
                go(pltpu.make_async_copy(zbuf.at[pl.ds(0, size)],
                                         xs_ref.at[pl.ds(start, size)], sem))
        for j in range(SUBLANES - 1):
            @pl.when(j < (ln & (SUBLANES - 1)))
            def _():
                go(pltpu.make_async_copy(zbuf.at[pl.ds(0, 1)], xs_ref.at[pl.ds(st + j, 1)], sem))
        return carry

    lax.fori_loop(0, padstart.shape[0], body, 0)


def _padfill_kernel(padstart, padlen, xs_in, xs_ref, zbuf, sem):
    del xs_in
    zbuf[...] = jnp.zeros(zbuf.shape, zbuf.dtype)
    _pad_fill(padstart, padlen, zbuf, xs_ref, sem, wait=False)
    _pad_fill(padstart, padlen, zbuf, xs_ref, sem, wait=True)


def _padfill(xs, padstart, padlen, bm):
    return pl.pallas_call(
        _padfill_kernel,
        out_shape=jax.ShapeDtypeStruct(xs.shape, xs.dtype),
        grid_spec=pltpu.PrefetchScalarGridSpec(
            num_scalar_prefetch=2,
            grid=(1,),
            in_specs=[pl.BlockSpec(memory_space=pl.ANY)],
            out_specs=pl.BlockSpec(memory_space=pl.ANY),
            scratch_shapes=[pltpu.VMEM((bm, xs.shape[1]), xs.dtype), pltpu.SemaphoreType.DMA(())]),
        input_output_aliases={2: 0},
        compiler_params=_cparams(("arbitrary",)),
        name="padfill",
    )(padstart, padlen, xs)


SC_SCATTER_WINDOW = 128


def _sc_mesh():
    return plsc.VectorSubcoreMesh(core_axis_name="core", subcore_axis_name="subcore")


def _sc_scatter_rows(rows, pos, cap):
    n, dw = rows.shape
    top_k = pos.shape[0]
    mesh = _sc_mesh()
    n_workers = mesh.num_cores * mesh.num_subcores
    win = SC_SCATTER_WINDOW
    per_w = n // n_workers
    n_chunks = per_w // win
    assert per_w * n_workers == n and n_chunks * win == per_w
    pos_w = pos.reshape(top_k, n_workers, n_chunks, win).transpose(1, 2, 0, 3)
    pos_w = pos_w.reshape(n_workers, n_chunks * top_k, win)

    @functools.partial(pl.kernel, mesh=mesh,
                       out_type=jax.ShapeDtypeStruct((cap, dw), rows.dtype),
                       scratch_types=[pltpu.VMEM((n_chunks * top_k, win), I32),
                                      pltpu.VMEM((win, dw), rows.dtype),
                                      pltpu.SemaphoreType.DMA])
    def scatter(r_hbm, p_hbm, o_hbm, idx_v, rows_v, sem):
        wid = lax.axis_index("subcore") * mesh.num_cores + lax.axis_index("core")
        pltpu.sync_copy(p_hbm.at[wid], idx_v)

        @pl.loop(0, n_chunks)
        def _(c):
            pltpu.sync_copy(r_hbm.at[pl.ds(wid * per_w + c * win, win)], rows_v)
            copies = [pltpu.async_copy(rows_v, o_hbm.at[idx_v.at[c * top_k + k]], sem)
                      for k in range(top_k)]
            for cp in copies:
                cp.wait()

    return scatter(rows, pos_w)


GMM_EXPERTS = 4
GMM_PAIR = 4
GMM_SLOTS = 16
GMM_AHEAD = GMM_SLOTS - GMM_PAIR


def _gmm_rows(xw, w13b, w2b, de):
    xb = _unpack_rows(xw).astype(BF16)
    u = jnp.dot(xb, w13b[...], preferred_element_type=F32)
    hmid = _silu(u[:, :de]) * u[:, de:]
    return _pack_rows(jnp.dot(hmid.astype(BF16), w2b[...], preferred_element_type=F32))


def _gmm_kernel(blk0, nblk, w13_ref, w2_ref, xs_ref, ys_ref, w13b, w2b, xbuf, ybuf, isem, osem):
    step = pl.program_id(0)
    n_e = nblk.shape[0]
    slots, bm = xbuf.shape[0], xbuf.shape[1]
    de = w2_ref.shape[1]
    total = blk0[n_e - 1] + nblk[n_e - 1]

    def x_copy(g):
        rows = pl.ds(pl.multiple_of(g * bm, bm), bm)
        slot = g & (slots - 1)
        return pltpu.make_async_copy(xs_ref.at[rows], xbuf.at[slot], isem.at[slot])

    def y_copy(g):
        rows = pl.ds(pl.multiple_of(g * bm, bm), bm)
        slot = g & (slots - 1)
        return pltpu.make_async_copy(ybuf.at[slot], ys_ref.at[rows], osem.at[slot])

    @pl.when(step == 0)
    def _():
        for j in range(GMM_AHEAD):
            @pl.when(j < total)
            def _():
                x_copy(j).start()

    def run_blocks(g0, count):
        gs = [g0 + j for j in range(count)]
        for g in gs:
            x_copy(g).wait()
        for g in gs:
            @pl.when(g + GMM_AHEAD < total)
            def _():
                x_copy(g + GMM_AHEAD).start()

            @pl.when(g >= slots)
            def _():
                y_copy(g - slots).wait()
        for g in gs:
            slot = g & (slots - 1)
            ybuf[slot] = _gmm_rows(xbuf[slot], w13b, w2b, de)
        for g in gs:
            y_copy(g).start()

    def expert_body(j, carry):
        e = step * GMM_EXPERTS + j
        nb = nblk[e]
        b0 = blk0[e]
        w13b[...] = w13_ref[j].astype(BF16)
        w2b[...] = w2_ref[j].astype(BF16)

        def pair_body(i, c):
            run_blocks(b0 + GMM_PAIR * i, GMM_PAIR)
            return c

        lax.fori_loop(0, nb // GMM_PAIR, pair_body, 0)
        for r in range(1, GMM_PAIR):
            @pl.when(nb % GMM_PAIR == r)
            def _():
                run_blocks(b0 + nb - r, r)
        return carry

    lax.fori_loop(0, GMM_EXPERTS, expert_body, 0)

    @pl.when(step == pl.num_programs(0) - 1)
    def _():
        for j in range(1, slots + 1):
            @pl.when(total - j >= 0)
            def _():
                y_copy(total - j).wait()


def _gmm(xs, w13, w2, blk0, nblk, bm):
    cap, dw = xs.shape
    e, d, de2 = w13.shape
    de = w2.shape[1]
    return pl.pallas_call(
        _gmm_kernel,
        out_shape=jax.ShapeDtypeStruct((cap, dw), U32),
        grid_spec=pltpu.PrefetchScalarGridSpec(
            num_scalar_prefetch=2,
            grid=(e // GMM_EXPERTS,),
            in_specs=[pl.BlockSpec((GMM_EXPERTS, d, de2), lambda i, b0, nb: (i, 0, 0)),
                      pl.BlockSpec((GMM_EXPERTS, de, d), lambda i, b0, nb: (i, 0, 0)),
                      pl.BlockSpec(memory_space=pl.ANY)],
            out_specs=pl.BlockSpec(memory_space=pl.ANY),
            scratch_shapes=[pltpu.VMEM((d, de2), BF16), pltpu.VMEM((de, d), BF16),
                            pltpu.VMEM((GMM_SLOTS, bm, dw), U32),
                            pltpu.VMEM((GMM_SLOTS, bm, dw), U32),
                            pltpu.SemaphoreType.DMA((GMM_SLOTS,)),
                            pltpu.SemaphoreType.DMA((GMM_SLOTS,))]),
        compiler_params=_cparams(("arbitrary",)),
        name="gmm",
    )(blk0, nblk, w13, w2, xs)


def _sc_gather_rows(table, idx, window):
    n_idx = idx.shape[0]
    dw = table.shape[1]
    mesh = _sc_mesh()
    n_workers = mesh.num_cores * mesh.num_subcores
    per_w = n_idx // n_workers
    n_chunks = per_w // window
    assert per_w * n_workers == n_idx and n_chunks * window == per_w and n_chunks % 2 == 0

    @functools.partial(pl.kernel, mesh=mesh,
                       out_type=jax.ShapeDtypeStruct((n_idx, dw), table.dtype),
                       scratch_types=[pltpu.VMEM((per_w,), I32),
                                      pltpu.VMEM((window, dw), table.dtype),
                                      pltpu.VMEM((window, dw), table.dtype),
                                      pltpu.SemaphoreType.DMA, pltpu.SemaphoreType.DMA,
                                      pltpu.SemaphoreType.DMA, pltpu.SemaphoreType.DMA])
    def gather(x_hbm, i_hbm, o_hbm, idx_v, rows0, rows1, gs0, gs1, os0, os1):
        wid = lax.axis_index("subcore") * mesh.num_cores + lax.axis_index("core")
        base = wid * per_w
        pltpu.sync_copy(i_hbm.at[pl.ds(base, per_w)], idx_v)
        bufs = ((rows0, gs0, os0), (rows1, gs1, os1))

        def fetch(j, slot):
            rows, gs, _ = bufs[slot]
            off = pl.multiple_of(j * window, window)
            return pltpu.async_copy(x_hbm.at[idx_v.at[pl.ds(off, window)]], rows, gs)

        def fetch_wait(slot):
            rows, gs, _ = bufs[slot]
            pltpu.make_async_copy(x_hbm.at[idx_v.at[pl.ds(0, window)]], rows, gs).wait()

        def put(j, slot):
            rows, _, osem = bufs[slot]
            off = pl.multiple_of(j * window, window)
            return pltpu.async_copy(rows, o_hbm.at[pl.ds(base + off, window)], osem)

        def put_wait(slot):
            rows, _, osem = bufs[slot]
            pltpu.make_async_copy(rows, o_hbm.at[pl.ds(base, window)], osem).wait()

        fetch(0, 0)

        @pl.loop(0, n_chunks, step=2)
        def _(j):
            @pl.when(j > 0)
            def _():
                put_wait(1)
            fetch(j + 1, 1)
            fetch_wait(0)
            put(j, 0)
            put_wait(0)

            @pl.when(j + 2 < n_chunks)
            def _():
                fetch(j + 2, 0)
            fetch_wait(1)
            put(j + 1, 1)

        put_wait(1)

    return gather(table, idx)


def _combine_kernel(gate_ref, x1_ref, h2_ref, mod_ref, gain_ref, sw13_ref, sw2_ref, yg_ref, o_ref):
    ds_ = sw2_ref.shape[0]
    hb = _unpack_rows(h2_ref[...]).astype(BF16)
    u = jnp.dot(hb, sw13_ref[...], preferred_element_type=F32)
    hmid = _silu(u[:, :ds_]) * u[:, ds_:]
    y = jnp.dot(hmid.astype(BF16), sw2_ref[...], preferred_element_type=F32)

    gate = gate_ref[...]
    moe = gate[:, 0:1] * _unpack_rows(yg_ref[0])
    for k in range(1, TOP_K):
        moe = moe + gate[:, k:k + 1] * _unpack_rows(yg_ref[k])
    xo = x1_ref[...] + mod_ref[0, 5:6, :] * (moe + y)
    ms = jnp.mean(xo * xo, axis=-1, keepdims=True)
    o_ref[...] = xo * lax.rsqrt(ms + EPS) * gain_ref[...]


def _combine(gate_tok, x1, h2, mods, gain, sw13, sw2, yg, tc, tiles_per_batch):
    n, d = x1.shape
    dw = h2.shape[1]
    nt = n // tc
    ds2 = sw13.shape[1]
    ds_ = sw2.shape[0]
    return pl.pallas_call(
        _combine_kernel,
        out_shape=jax.ShapeDtypeStruct((n, d), F32),
        grid=(nt,),
        in_specs=[pl.BlockSpec((tc, TOP_K), lambda i: (i, 0)),
                  pl.BlockSpec((tc, d), lambda i: (i, 0)),
                  pl.BlockSpec((tc, dw), lambda i: (i, 0)),
                  pl.BlockSpec((1, 8, d), lambda i: (i // tiles_per_batch, 0, 0)),
                  pl.BlockSpec((1, d), lambda i: (0, 0)),
                  pl.BlockSpec((d, ds2), lambda i: (0, 0)),
                  pl.BlockSpec((ds_, d), lambda i: (0, 0)),
                  pl.BlockSpec((TOP_K, tc, dw), lambda i: (0, i, 0))],
        out_specs=pl.BlockSpec((tc, d), lambda i: (i, 0)),
        compiler_params=_cparams(("arbitrary",)),
        name="combine",
    )(gate_tok, x1, h2, mods, gain, sw13, sw2, yg)


def _block_diag_pairs(w, per):
    nb, bd, _ = w.shape
    w = w.reshape(nb // per, per, bd, bd)
    eye = jnp.eye(per, dtype=w.dtype)
    out = jnp.einsum("gpij,pq->gpiqj", w, eye)
    return out.reshape(nb // per, per * bd, per * bd)


def kernel(x, c, ctx, c_ctx, ada_w, ada_b, norm_mix, norm_ffn, norm_final, w_in, w_out,
           lru_conv_w, lru_conv_b, lru_wa, lru_ba, lru_wx, lru_bx, lru_lambda,
           hgrn_lb_logits, hgrn_norm, router_w, router_b, exp_w13, exp_w2, shared_w13, shared_w2):
    assert ada_w.shape[0] == 1, "single-layer block"
    b, s, d = x.shape
    n = b * s
    lru_w = lru_conv_w.shape[2]
    n_experts = router_w.shape[2]

    rows = -(-(b + 1) // SUBLANES) * SUBLANES
    cs = jnp.zeros((rows, d), F32).at[:b].set(c).at[b].set(c_ctx)
    mod = _ada(cs, ada_w[0], ada_b[0][None, :]).reshape(rows, 6, d)
    mods = jnp.pad(mod, ((0, 0), (0, 2), (0, 0)))

    w_in_bf = w_in[0].astype(BF16)
    gain1 = norm_mix[0][None, :]
    projx = _inproj(x, mods, gain1, w_in_bf, PROJ_TILE, shared_mod=False)
    projc = _inproj(ctx, mods[b:b + 1], gain1, w_in_bf, ctx.shape[1], shared_mod=True)

    per = LRU_G // (lru_w // LRU_BLOCKS)
    wg = jnp.stack([_block_diag_pairs(lru_wa[0, 0], per), _block_diag_pairs(lru_wx[0, 0], per),
                    _block_diag_pairs(lru_wa[0, 1], per), _block_diag_pairs(lru_wx[0, 1], per)]
                   )
    wg = (0.5 * wg).astype(BF16)
    bg = 0.5 * jnp.stack([lru_ba[0, 0], lru_bx[0, 0], lru_ba[0, 1], lru_bx[0, 1]])
    ylru = _lru(projx, projc, lru_conv_w[0], lru_conv_b[0][None, :], wg, bg, lru_lambda[0])

    yhg = _hgrn(projx, projc, hgrn_lb_logits, hgrn_norm[0][None, :], (2 * lru_w) // LANES)

    wo = w_out[0].astype(BF16).reshape(2, lru_w, d)
    x1, h2, eid_t, gate_t, rank_t, cnt = _outproj(
        ylru, yhg, x, mods, norm_ffn[0][None, :], wo,
        router_w[0].T.astype(BF16), router_b[0][:, None], PROJ_TILE)

    bm = GMM_BLOCK
    counts = cnt[:, 0]
    padded = (counts + bm - 1) // bm * bm
    pends = jnp.cumsum(padded)
    pstarts = pends - padded
    pos = _pos(pstarts, eid_t, rank_t, POS_TILE)
    cap = n * TOP_K + n_experts * bm

    xs = _padfill(_sc_scatter_rows(h2, pos, cap), pstarts + counts, padded - counts, bm)
    ys = _gmm(xs, exp_w13[0], exp_w2[0], pstarts // bm, padded // bm, bm)

    yg = _sc_gather_rows(ys, pos.reshape(-1), SC_GATHER_WINDOW).reshape(TOP_K, n, d // 2)
    tc = COMBINE_TILE
    out = _combine(gate_t.T, x1.reshape(n, d), h2, mods,
                   norm_final[None, :], shared_w13[0].astype(BF16), shared_w2[0].astype(BF16),
                   yg, tc, s // tc)
    return out.reshape(b, s, d)
```

```python
import functools

import jax
import jax.numpy as jnp
from jax import lax
from jax.experimental import pallas as pl
from jax.experimental.pallas import tpu as pltpu
from jax.experimental.pallas import tpu_sc as plsc

F32 = jnp.float32
BF16 = jnp.bfloat16
I32 = jnp.int32
U32 = jnp.uint32

EPS = 1e-6
LRU_C = 8.0
ROUTE_SCALE = 2.5
GRID_W = 64
HG_CHUNK = 32
N_GROUPS = 8
TOPK_GROUPS = 4
TOP_K = 8
LRU_BLOCKS = 8
HG_HEADS = 4
CONV_W = 4
CONV_PAD_L = 1

SUBLANES = 8
LANES = 128
N_SEG = SUBLANES
VMEM_LIMIT = 56 * 1024 * 1024

PROJ_TILE = 512
POS_TILE = 2048
GMM_BLOCK = 256
SC_GATHER_WINDOW = 64
COMBINE_TILE = 512


def _cparams(sem, vmem=VMEM_LIMIT):
    return pltpu.CompilerParams(dimension_semantics=sem, vmem_limit_bytes=vmem)


def _sigmoid(x):
    return jax.nn.sigmoid(x)


def _silu(x):
    return x * _sigmoid(x)


def _pack_rows(x):
    w = x.shape[1] // 2
    bits = pltpu.bitcast(x.astype(BF16).astype(F32), U32)
    return (bits[:, :w] >> 16) | (bits[:, w:] & jnp.uint32(0xFFFF0000))


def _unpack_rows(p):
    lo = pltpu.bitcast(p << 16, F32)
    hi = pltpu.bitcast(p & jnp.uint32(0xFFFF0000), F32)
    return jnp.concatenate([lo, hi], axis=1)


def _gelu_tanh(x):
    c = 0.7978845608028654
    return 0.5 * x * (1.0 + jnp.tanh(c * (x + 0.044715 * (x * x * x))))


def _ada_kernel(c_ref, w_ref, b_ref, o_ref):
    s = _silu(c_ref[...])
    o_ref[...] = jnp.dot(s, w_ref[...], preferred_element_type=F32,
                         precision=lax.Precision.HIGHEST) + b_ref[...]


def _ada(cs, w, b):
    rows, d = cs.shape
    n = w.shape[1]
    bn = 1024
    return pl.pallas_call(
        _ada_kernel,
        out_shape=jax.ShapeDtypeStruct((rows, n), F32),
        grid=(n // bn,),
        in_specs=[pl.BlockSpec((rows, d), lambda j: (0, 0)),
                  pl.BlockSpec((d, bn), lambda j: (0, j)),
                  pl.BlockSpec((1, bn), lambda j: (0, j))],
        out_specs=pl.BlockSpec((rows, bn), lambda j: (0, j)),
        compiler_params=_cparams(("arbitrary",)),
        name="ada",
    )(cs, w, b)


def _inproj_kernel(x_ref, mod_ref, gain_ref, w_ref, o_ref):
    x = x_ref[0]
    ms = jnp.mean(x * x, axis=-1, keepdims=True)
    y = x * lax.rsqrt(ms + EPS) * gain_ref[...]
    h = y * (1.0 + mod_ref[0, 1:2, :]) + mod_ref[0, 0:1, :]
    o_ref[0] = jnp.dot(h.astype(BF16), w_ref[...], preferred_element_type=F32)


def _inproj(x, mods, gain, w_bf, tm, shared_mod):
    b, s, d = x.shape
    n = w_bf.shape[1]
    mod_map = (lambda bi, i: (0, 0, 0)) if shared_mod else (lambda bi, i: (bi, 0, 0))
    return pl.pallas_call(
        _inproj_kernel,
        out_shape=jax.ShapeDtypeStruct((b, s, n), F32),
        grid=(b, s // tm),
        in_specs=[pl.BlockSpec((1, tm, d), lambda bi, i: (bi, i, 0)),
                  pl.BlockSpec((1, 8, d), mod_map),
                  pl.BlockSpec((1, d), lambda bi, i: (0, 0)),
                  pl.BlockSpec((d, n), lambda bi, i: (0, 0))],
        out_specs=pl.BlockSpec((1, tm, n), lambda bi, i: (bi, i, 0)),
        compiler_params=_cparams(("arbitrary", "arbitrary")),
        name="inproj",
    )(x, mods, gain, w_bf)


LRU_G = 256
LRU_CHUNK = 256


def _seg_rows(s, seg_len):
    return pl.ds(s, seg_len, stride=N_SEG)


def _out_pitch(seg_len):
    return seg_len + SUBLANES


def _lane_store(ref, d, rows, val):
    nl = val.shape[-1] // LANES
    for l in range(nl):
        ref[d * nl + l, rows, :] = val[:, l * LANES:(l + 1) * LANES]


def _lane_load(ref, d, rows, nl):
    return jnp.concatenate([ref[d * nl + l, rows, :] for l in range(nl)], axis=-1)


LRU_CONV_PIECE = 64


def _lru_coeffs(pad_ref, t0, rows, cw, cb, wg_ref, bg, sp, a_ref, b_ref, dst_rows, u_ref):
    for p0 in range(0, rows, LRU_CONV_PIECE):
        win = pad_ref[pl.ds(t0 + p0, LRU_CONV_PIECE + 2 * SUBLANES), :]
        u = cb
        for k in range(CONV_W):
            off = SUBLANES - CONV_PAD_L + k
            u = u + win[off:off + LRU_CONV_PIECE, :] * cw[k:k + 1, :]
        u_ref[p0:p0 + LRU_CONV_PIECE, :] = u
    u = u_ref[0:rows, :]
    ub = u.astype(BF16)
    half_u = 0.5 * u
    for d in range(2):
        tr = jnp.tanh(jnp.dot(ub, wg_ref[2 * d, 0], preferred_element_type=F32)
                      + bg[2 * d:2 * d + 1, :])
        ti = jnp.tanh(jnp.dot(ub, wg_ref[2 * d + 1, 0], preferred_element_type=F32)
                      + bg[2 * d + 1:2 * d + 2, :])
        half_c = (-0.5 * LRU_C) * sp[d:d + 1, :]
        log_a = half_c * tr + half_c
        a = jnp.exp(log_a)
        one_minus_a2 = -jnp.tanh(log_a) * (1.0 + a * a)
        root = jnp.where(one_minus_a2 > 0.0, one_minus_a2 * lax.rsqrt(one_minus_a2), 0.0)
        bb = root * (ti * half_u + half_u)
        _lane_store(a_ref, d, dst_rows, a)
        _lane_store(b_ref, d, dst_rows, bb)


def _seg_scan(a_ref, b_ref, h_ref, p_ref, seg_len, unroll):
    n_lead = a_ref.shape[0]
    nl = n_lead // 2
    zero = jnp.zeros((N_SEG, LANES), F32)
    one = jnp.ones((N_SEG, LANES), F32)
    init = tuple((zero, one) for _ in range(n_lead))

    def step(t, carry):
        out = []
        for i in range(n_lead):
            h, p = carry[i]
            tt = t if i < nl else seg_len - 1 - t
            rows = pl.ds(pl.multiple_of(tt * N_SEG, N_SEG), N_SEG)
            a = a_ref[i, rows, :]
            h = a * h + b_ref[i, rows, :]
            p = p * a
            if h_ref is not None:
                out_rows = pl.ds(tt, N_SEG, stride=_out_pitch(seg_len))
                h_ref[i, out_rows, :] = h
                p_ref[i, out_rows, :] = p
            out.append((h, p))
        return tuple(out)

    def body(i, carry):
        for j in range(unroll):
            carry = step(i * unroll + j, carry)
        return carry

    ends = lax.fori_loop(0, seg_len // unroll, body, init)
    h_end = [jnp.concatenate([ends[d * nl + l][0] for l in range(nl)], axis=-1) for d in range(2)]
    p_end = [jnp.concatenate([ends[d * nl + l][1] for l in range(nl)], axis=-1) for d in range(2)]
    return h_end, p_end


def _seg_carries(h_end, p_end, h0, reverse):
    order = range(N_SEG - 1, -1, -1) if reverse else range(N_SEG)
    cin = [None] * N_SEG
    c = h0
    for s in order:
        cin[s] = c
        c = p_end[s:s + 1, :] * c + h_end[s:s + 1, :]
    return cin, c


def _lru_kernel(rx_ref, rg_ref, rc_ref, cw_ref, cb_ref, wg_ref, bg_ref, lam_ref, o_ref,
                padl, padc, u_s, a_l, b_l, h_l, p_l, a_c, b_c):
    s_len = rx_ref.shape[1]
    c_len = rc_ref.shape[1]
    g = rx_ref.shape[2]
    nl = g // LANES
    seg_l = s_len // N_SEG
    seg_c = c_len // N_SEG

    zeros = jnp.zeros((SUBLANES, g), F32)
    padl[0:SUBLANES, :] = zeros
    padl[SUBLANES + s_len:2 * SUBLANES + s_len, :] = zeros
    padc[0:SUBLANES, :] = zeros
    padc[SUBLANES + c_len:2 * SUBLANES + c_len, :] = zeros
    for i in range(s_len // LRU_CHUNK):
        padl[SUBLANES + i * LRU_CHUNK:SUBLANES + (i + 1) * LRU_CHUNK, :] = (
            rx_ref[0, i * LRU_CHUNK:(i + 1) * LRU_CHUNK, :])
    padc[SUBLANES:SUBLANES + c_len, :] = rc_ref[0]

    cw = cw_ref[...]
    cb = cb_ref[...]
    bg = bg_ref[...]
    x = -lam_ref[...]
    sp = jnp.maximum(x, 0.0) + jnp.log(1.0 + jnp.exp(-jnp.abs(x)))

    _lru_coeffs(padc, 0, c_len, cw, cb, wg_ref, bg, sp, a_l, b_l, pl.ds(0, c_len), u_s)
    for i in range(2 * nl):
        for s in range(N_SEG):
            a_c[i, _seg_rows(s, seg_c), :] = a_l[i, s * seg_c:(s + 1) * seg_c, :]
            b_c[i, _seg_rows(s, seg_c), :] = b_l[i, s * seg_c:(s + 1) * seg_c, :]
    h_end, p_end = _seg_scan(a_c, b_c, None, None, seg_c, unroll=4)
    zero_row = jnp.zeros((1, g), F32)
    _, h0_f = _seg_carries(h_end[0], p_end[0], zero_row, reverse=False)
    _, h0_b = _seg_carries(h_end[1], p_end[1], zero_row, reverse=True)

    def coeff_body(s, carry):
        t0 = pl.multiple_of(s * seg_l, SUBLANES)
        _lru_coeffs(padl, t0, seg_l, cw, cb, wg_ref, bg, sp, a_l, b_l, _seg_rows(s, seg_l), u_s)
        return carry

    lax.fori_loop(0, N_SEG, coeff_body, 0)
    h_end, p_end = _seg_scan(a_l, b_l, h_l, p_l, seg_l, unroll=4)
    cin_f, _ = _seg_carries(h_end[0], p_end[0], h0_f, reverse=False)
    cin_b, _ = _seg_carries(h_end[1], p_end[1], h0_b, reverse=True)

    for s in range(N_SEG):
        rows = pl.ds(s * _out_pitch(seg_l), seg_l)
        h = (_lane_load(h_l, 0, rows, nl) + _lane_load(p_l, 0, rows, nl) * cin_f[s]) + (
            _lane_load(h_l, 1, rows, nl) + _lane_load(p_l, 1, rows, nl) * cin_b[s])
        o_ref[0, s * seg_l:(s + 1) * seg_l, :] = _gelu_tanh(rg_ref[0, s * seg_l:(s + 1) * seg_l, :]) * h


def _lru(projx, projc, cw, cb, wg, bg, lam):
    b, s_len, _ = projx.shape
    c_len = projc.shape[1]
    w = cw.shape[1]
    g = LRU_G
    ng = w // g
    seg_l = s_len // N_SEG
    seg_c = c_len // N_SEG
    assert seg_l == LRU_CHUNK and c_len % (N_SEG * 1) == 0
    return pl.pallas_call(
        _lru_kernel,
        out_shape=jax.ShapeDtypeStruct((b, s_len, w), F32),
        grid=(b, ng),
        in_specs=[pl.BlockSpec((1, s_len, g), lambda bi, j: (bi, 0, j)),
                  pl.BlockSpec((1, s_len, g), lambda bi, j: (bi, 0, ng + j)),
                  pl.BlockSpec((1, c_len, g), lambda bi, j: (bi, 0, j)),
                  pl.BlockSpec((CONV_W, g), lambda bi, j: (0, j)),
                  pl.BlockSpec((1, g), lambda bi, j: (0, j)),
                  pl.BlockSpec((4, 1, g, g), lambda bi, j: (0, j, 0, 0)),
                  pl.BlockSpec((4, g), lambda bi, j: (0, j)),
                  pl.BlockSpec((2, g), lambda bi, j: (0, j))],
        out_specs=pl.BlockSpec((1, s_len, g), lambda bi, j: (bi, 0, j)),
        scratch_shapes=[pltpu.VMEM((s_len + 2 * SUBLANES, g), F32),
                        pltpu.VMEM((c_len + 2 * SUBLANES, g), F32),
                        pltpu.VMEM((max(seg_l, c_len), g), F32),
                        pltpu.VMEM((2 * g // LANES, s_len, LANES), F32),
                        pltpu.VMEM((2 * g // LANES, s_len, LANES), F32),
                        pltpu.VMEM((2 * g // LANES, N_SEG * _out_pitch(seg_l), LANES), F32),
                        pltpu.VMEM((2 * g // LANES, N_SEG * _out_pitch(seg_l), LANES), F32),
                        pltpu.VMEM((2 * g // LANES, c_len, LANES), F32),
                        pltpu.VMEM((2 * g // LANES, c_len, LANES), F32)],
        compiler_params=_cparams(("arbitrary", "arbitrary")),
        name="lru",
    )(projx, projx, projc, cw, cb, wg, bg, lam)


LOG2_E = 1.4426950408889634
HG_PITCH_L = GRID_W + SUBLANES
HG_PITCH_C = SUBLANES


def _hg_gates(half_z, lb):
    c = 0.5 * (1.0 - lb)
    ct = c * jnp.tanh(half_z)
    logf = jnp.log(0.5 * (1.0 + lb) + ct)
    k = c - ct
    return logf, k


def _hg_prepass(load, width, pitch, lb, qd, kd, kl, vs, dec):
    n_pos = HG_CHUNK
    for d, zname in enumerate(("zf", "zb")):
        order = range(n_pos) if d == 0 else range(n_pos - 1, -1, -1)
        g = jnp.zeros((width, LANES), F32)
        lbd = lb[d:d + 1, :]
        for p in order:
            logf, k = _hg_gates(load(zname, p), lbd)
            g = g + logf
            qd[d, p * pitch:p * pitch + width, :] = g
            kl[d, p * pitch:p * pitch + width, :] = k
        g_last = g * LOG2_E
        dec[d, 0:width, :] = jnp.exp2(g_last)
        for p in range(n_pos):
            rows = slice(p * pitch, p * pitch + width)
            gp = qd[d, rows, :] * LOG2_E
            k = kl[d, rows, :]
            qd[d, rows, :] = load("q", p) * jnp.exp2(gp)
            kd[d, rows, :] = k * jnp.exp2(-gp)
            kl[d, rows, :] = k * jnp.exp2(g_last - gp)
    for p in range(n_pos):
        vs[p * pitch:p * pitch + width, :] = load("v", p)


HG_GROUP = 4
HG_UNROLL = 8
HG_INTRA_UNROLL = 4
_NT = (((1,), (1,)), ((), ()))
_TN = (((0,), (0,)), ((), ()))


def _hg_group_rows(ref, lead, c0, pitch):
    parts = []
    for j in range(HG_GROUP):
        rows = pl.ds(c0 + j, HG_CHUNK, stride=pitch)
        parts.append(ref[rows, :] if lead is None else ref[lead, rows, :])
    return jnp.concatenate(parts, axis=0)


def _hg_scores(g, pitch, qd, kd, masks, sc_ref):
    c0 = g * HG_GROUP
    total = None
    for d in range(2):
        q = _hg_group_rows(qd, d, c0, pitch).astype(BF16)
        k = _hg_group_rows(kd, d, c0, pitch).astype(BF16)
        sc = jnp.where(masks[d], lax.dot_general(q, k, _NT, preferred_element_type=F32), 0.0)
        total = sc if total is None else total + sc
    sc_ref[g] = total.astype(BF16)


def _hg_summaries(g, pitch, kl, vs, sc_ref, oi, kv):
    c0 = g * HG_GROUP
    v = _hg_group_rows(vs, None, c0, pitch).astype(BF16)
    if oi is not None:
        o = jnp.dot(sc_ref[g], v, preferred_element_type=F32)
        for j in range(HG_GROUP):
            oi[pl.ds(c0 + j, HG_CHUNK, stride=pitch), :] = o[j * HG_CHUNK:(j + 1) * HG_CHUNK, :]
    kls = jnp.concatenate([_hg_group_rows(kl, 0, c0, pitch), _hg_group_rows(kl, 1, c0, pitch)],
                          axis=1).astype(BF16)
    for j in range(HG_GROUP):
        sl = slice(j * HG_CHUNK, (j + 1) * HG_CHUNK)
        both = lax.dot_general(v[sl], kls[sl], _TN, preferred_element_type=F32)
        kv[0, c0 + j] = both[:, :LANES]
        kv[1, c0 + j] = both[:, LANES:]


def _hg_steps(step0, n_steps, n_chunks, pitch, qd, dec, kv, states, ox):
    states = list(states)
    for u in range(n_steps):
        cs = (step0 + u, n_chunks - 1 - (step0 + u))
        if ox is not None:
            rows = [pl.ds(c, HG_CHUNK, stride=pitch) for c in cs]
            q2 = jnp.concatenate([qd[0, rows[0], :], qd[1, rows[1], :]], axis=0).astype(BF16)
            s2 = jnp.concatenate(states, axis=0).astype(BF16)
            o2 = lax.dot_general(q2, s2, _NT, preferred_element_type=F32)
            ox[0, rows[0], :] = o2[:HG_CHUNK, :LANES]
            ox[1, rows[1], :] = o2[HG_CHUNK:, LANES:]
        for d in range(2):
            states[d] = states[d] * dec[d, pl.ds(cs[d], 1), :] + kv[d, cs[d]]
    return tuple(states)


def _hgrn_kernel(q_ref, v_ref, zf_ref, zb_ref, g_ref, qc_ref, vc_ref, zfc_ref, zbc_ref,
                 lbl_ref, gain_ref, o_ref, qd, kd, kl, vs, dec, oi, ox, kv, sc):
    s_len = q_ref.shape[1]
    c_len = qc_ref.shape[1]
    n_col = s_len // HG_CHUNK
    n_cc = c_len // HG_CHUNK
    assert n_col == GRID_W and n_cc == HG_PITCH_C
    assert n_col % HG_GROUP == 0 and n_cc % HG_GROUP == 0 and n_col % HG_UNROLL == 0

    lg = lbl_ref[...]
    m = jnp.max(lg, axis=1, keepdims=True)
    ex = jnp.exp(lg - m)
    lb = ex[:, 0, :] / jnp.sum(ex, axis=1)

    gr = HG_GROUP * HG_CHUNK
    ii = lax.broadcasted_iota(I32, (gr, gr), 0)
    jj = lax.broadcasted_iota(I32, (gr, gr), 1)
    same = (ii // HG_CHUNK) == (jj // HG_CHUNK)
    masks = (jnp.logical_and(same, jj <= ii), jnp.logical_and(same, jj >= ii))

    c_refs = {"q": qc_ref, "v": vc_ref, "zf": zfc_ref, "zb": zbc_ref}

    def load_c(name, p):
        return c_refs[name][0, pl.ds(p, n_cc, stride=HG_CHUNK), :]

    _hg_prepass(load_c, n_cc, HG_PITCH_C, lb, qd, kd, kl, vs, dec)
    for g in range(n_cc // HG_GROUP):
        _hg_summaries(g, HG_PITCH_C, kl, vs, None, None, kv)
    zero = jnp.zeros((LANES, LANES), F32)
    states = _hg_steps(0, n_cc, n_cc, HG_PITCH_C, qd, dec, kv, (zero, zero), None)

    l_refs = {"q": q_ref, "v": v_ref, "zf": zf_ref, "zb": zb_ref}

    def load_l(name, p):
        return l_refs[name][0, p * n_col:(p + 1) * n_col, :]

    _hg_prepass(load_l, n_col, HG_PITCH_L, lb, qd, kd, kl, vs, dec)

    n_it = n_col // (HG_GROUP * HG_INTRA_UNROLL)

    def scores_body(i, carry):
        for u in range(HG_INTRA_UNROLL):
            _hg_scores(i * HG_INTRA_UNROLL + u, HG_PITCH_L, qd, kd, masks, sc)
        return carry

    lax.fori_loop(0, n_it, scores_body, 0)

    def summaries_body(i, carry):
        for u in range(HG_INTRA_UNROLL):
            _hg_summaries(i * HG_INTRA_UNROLL + u, HG_PITCH_L, kl, vs, sc, oi, kv)
        return carry

    lax.fori_loop(0, n_it, summaries_body, 0)

    def step_body(i, carry):
        return _hg_steps(i * HG_UNROLL, HG_UNROLL, n_col, HG_PITCH_L, qd, dec, kv, carry, ox)

    lax.fori_loop(0, n_col // HG_UNROLL, step_body, states)

    gain = gain_ref[...]
    for r in range(HG_CHUNK):
        rows = slice(r * HG_PITCH_L, r * HG_PITCH_L + n_col)
        o = oi[rows, :] + (ox[0, rows, :] + ox[1, rows, :])
        y = o * lax.rsqrt(jnp.mean(o * o, axis=-1, keepdims=True) + EPS) * gain
        o_ref[0, r * n_col:(r + 1) * n_col, :] = y * _silu(g_ref[0, r * n_col:(r + 1) * n_col, :])


def _hgrn(projx, projc, lbl, gain, col0):
    b, s_len, _ = projx.shape
    c_len = projc.shape[1]
    hw = gain.shape[1]
    nh = hw // LANES
    n_slots = lbl.shape[1]

    def xs(k):
        return pl.BlockSpec((1, s_len, LANES), lambda bi, h, k=k: (bi, 0, col0 + k * nh + h))

    def cs(k):
        return pl.BlockSpec((1, c_len, LANES), lambda bi, h, k=k: (bi, 0, col0 + k * nh + h))

    sc_rows = HG_CHUNK * HG_PITCH_L
    return pl.pallas_call(
        _hgrn_kernel,
        out_shape=jax.ShapeDtypeStruct((b, s_len, hw), F32),
        grid=(b, nh),
        in_specs=[xs(0), xs(1), xs(2), xs(3), xs(4), cs(0), cs(1), cs(2), cs(3),
                  pl.BlockSpec((2, n_slots, LANES), lambda bi, h: (0, 0, h)),
                  pl.BlockSpec((1, LANES), lambda bi, h: (0, h))],
        out_specs=pl.BlockSpec((1, s_len, LANES), lambda bi, h: (bi, 0, h)),
        scratch_shapes=[pltpu.VMEM((2, sc_rows, LANES), F32),
                        pltpu.VMEM((2, sc_rows, LANES), F32),
                        pltpu.VMEM((2, sc_rows, LANES), F32),
                        pltpu.VMEM((sc_rows, LANES), F32),
                        pltpu.VMEM((2, HG_PITCH_L, LANES), F32),
                        pltpu.VMEM((sc_rows, LANES), F32),
                        pltpu.VMEM((2, sc_rows, LANES), F32),
                        pltpu.VMEM((2, GRID_W, LANES, LANES), F32),
                        pltpu.VMEM((GRID_W // HG_GROUP, HG_GROUP * HG_CHUNK,
                                    HG_GROUP * HG_CHUNK), BF16)],
        compiler_params=_cparams(("arbitrary", "arbitrary")),
        name="hgrn",
    )(projx, projx, projx, projx, projx, projc, projc, projc, projc, lbl, gain)


def _first_index_of_max(vals, iota, big):
    m = jnp.max(vals, axis=0, keepdims=True)
    idx = jnp.min(jnp.where(vals == m, iota, big), axis=0, keepdims=True)
    return m, idx


def _route(logits_t, rb, n_experts):
    t = logits_t.shape[1]
    gsz = n_experts // N_GROUPS
    neg = -jnp.inf
    scores = _sigmoid(logits_t)
    biased = scores + rb
    iota_g = lax.broadcasted_iota(I32, (gsz, t), 0)
    gscore = []
    for gi in range(N_GROUPS):
        blk = biased[gi * gsz:(gi + 1) * gsz, :]
        m1, i1 = _first_index_of_max(blk, iota_g, gsz)
        m2 = jnp.max(jnp.where(iota_g == i1, neg, blk), axis=0, keepdims=True)
        gscore.append(m1 + m2)
    gs = jnp.concatenate(gscore, axis=0)
    iota_n = lax.broadcasted_iota(I32, (N_GROUPS, t), 0)
    gsel = jnp.zeros((N_GROUPS, t), jnp.bool_)
    for _ in range(TOPK_GROUPS):
        _, gi1 = _first_index_of_max(gs, iota_n, N_GROUPS)
        hit = iota_n == gi1
        gsel = jnp.logical_or(gsel, hit)
        gs = jnp.where(hit, neg, gs)
    emask = jnp.concatenate(
        [jnp.broadcast_to(gsel[gi:gi + 1, :], (gsz, t)) for gi in range(N_GROUPS)], axis=0)
    masked = jnp.where(emask, biased, neg)
    iota_e = lax.broadcasted_iota(I32, (n_experts, t), 0)
    ids, gates = [], []
    sel = jnp.zeros((n_experts, t), F32)
    for _ in range(TOP_K):
        _, ei = _first_index_of_max(masked, iota_e, n_experts)
        hit = iota_e == ei
        ids.append(ei)
        gates.append(jnp.sum(jnp.where(hit, scores, 0.0), axis=0, keepdims=True))
        sel = jnp.where(hit, 1.0, sel)
        masked = jnp.where(hit, neg, masked)
    ids = jnp.concatenate(ids, axis=0)
    gates = jnp.concatenate(gates, axis=0)
    gates = gates / jnp.sum(gates, axis=0, keepdims=True) * ROUTE_SCALE
    return ids, gates, sel


def _outproj_kernel(yl_ref, yh_ref, x_ref, mod_ref, gain_ref, wo_ref, rwt_ref, rb_ref,
                    x1_ref, h2_ref, eid_ref, gate_ref, rank_ref, cnt_ref, carry):
    first = jnp.logical_and(pl.program_id(0) == 0, pl.program_id(1) == 0)

    @pl.when(first)
    def _():
        carry[...] = jnp.zeros(carry.shape, F32)

    y = jnp.dot(yl_ref[0].astype(BF16), wo_ref[0], preferred_element_type=F32)
    y = y + jnp.dot(yh_ref[0].astype(BF16), wo_ref[1], preferred_element_type=F32)
    x1 = x_ref[0] + mod_ref[0, 2:3, :] * y
    x1_ref[0] = x1
    ms = jnp.mean(x1 * x1, axis=-1, keepdims=True)
    h2 = x1 * lax.rsqrt(ms + EPS) * gain_ref[...]
    h2 = h2 * (1.0 + mod_ref[0, 4:5, :]) + mod_ref[0, 3:4, :]
    h2_ref[...] = _pack_rows(h2)

    n_experts = rwt_ref.shape[0]
    t = h2.shape[0]
    logits_t = lax.dot_general(rwt_ref[...], h2.astype(BF16), _NT,
                               preferred_element_type=F32)
    ids, gates, sel = _route(logits_t, rb_ref[...], n_experts)
    eid_ref[...] = ids
    gate_ref[...] = gates

    ti = lax.broadcasted_iota(I32, (t, t), 0)
    tj = lax.broadcasted_iota(I32, (t, t), 1)
    upper = (ti < tj).astype(BF16)
    selb = sel.astype(BF16)
    before = jnp.dot(selb, upper, preferred_element_type=F32) + carry[:, 0:1]
    iota_e = lax.broadcasted_iota(I32, (n_experts, t), 0)
    ranks = [jnp.sum(jnp.where(iota_e == ids[k:k + 1, :], before, 0.0), axis=0, keepdims=True)
             for k in range(TOP_K)]
    rank_ref[...] = jnp.concatenate(ranks, axis=0).astype(I32)
    total = carry[...] + jnp.dot(selb, jnp.ones((t, LANES), BF16), preferred_element_type=F32)
    carry[...] = total
    cnt_ref[...] = total.astype(I32)


def _outproj(ylru, yhg, x, mods, gain, wo, rwt, rb, tm):
    b, s, d = x.shape
    hw = ylru.shape[2]
    e = rwt.shape[0]
    n = b * s
    nt = s // tm
    tok = lambda bi, i: (0, bi * nt + i)
    return pl.pallas_call(
        _outproj_kernel,
        out_shape=(jax.ShapeDtypeStruct((b, s, d), F32),
                   jax.ShapeDtypeStruct((n, d // 2), U32),
                   jax.ShapeDtypeStruct((TOP_K, n), I32),
                   jax.ShapeDtypeStruct((TOP_K, n), F32),
                   jax.ShapeDtypeStruct((TOP_K, n), I32),
                   jax.ShapeDtypeStruct((e, LANES), I32)),
        grid=(b, nt),
        in_specs=[pl.BlockSpec((1, tm, hw), lambda bi, i: (bi, i, 0)),
                  pl.BlockSpec((1, tm, hw), lambda bi, i: (bi, i, 0)),
                  pl.BlockSpec((1, tm, d), lambda bi, i: (bi, i, 0)),
                  pl.BlockSpec((1, 8, d), lambda bi, i: (bi, 0, 0)),
                  pl.BlockSpec((1, d), lambda bi, i: (0, 0)),
                  pl.BlockSpec((2, hw, d), lambda bi, i: (0, 0, 0)),
                  pl.BlockSpec((e, d), lambda bi, i: (0, 0)),
                  pl.BlockSpec((e, 1), lambda bi, i: (0, 0))],
        out_specs=(pl.BlockSpec((1, tm, d), lambda bi, i: (bi, i, 0)),
                   pl.BlockSpec((tm, d // 2), lambda bi, i: (bi * nt + i, 0)),
                   pl.BlockSpec((TOP_K, tm), tok),
                   pl.BlockSpec((TOP_K, tm), tok),
                   pl.BlockSpec((TOP_K, tm), tok),
                   pl.BlockSpec((e, LANES), lambda bi, i: (0, 0))),
        scratch_shapes=[pltpu.VMEM((e, LANES), F32)],
        compiler_params=_cparams(("arbitrary", "arbitrary")),
        name="outproj",
    )(ylru, yhg, x, mods, gain, wo, rwt, rb)


def _pos_kernel(pstart, eid_ref, rank_ref, pos_ref):
    eid = eid_ref[...]

    def body(e, acc):
        return jnp.where(eid == e, pstart[e], acc)

    base = lax.fori_loop(0, pstart.shape[0], body, jnp.zeros(eid.shape, I32))
    pos_ref[...] = base + rank_ref[...]


def _pos(pstarts, eid_t, rank_t, tn):
    k, n = eid_t.shape
    spec = pl.BlockSpec((k, tn), lambda i, ps: (0, i))
    return pl.pallas_call(
        _pos_kernel,
        out_shape=jax.ShapeDtypeStruct((k, n), I32),
        grid_spec=pltpu.PrefetchScalarGridSpec(
            num_scalar_prefetch=1, grid=(n // tn,), in_specs=[spec, spec], out_specs=spec),
        compiler_params=_cparams(("arbitrary",)),
        name="pos",
    )(pstarts, eid_t, rank_t)


def _pad_fill(padstart, padlen, zbuf, xs_ref, sem, wait):
    nbits = zbuf.shape[0].bit_length() - 1
    low_bits = SUBLANES.bit_length() - 1

    def go(cp):
        if wait:
            cp.wait()
        else:
            cp.start()

    def body(e, carry):
        st = padstart[e]
        ln = padlen[e]
        end = st + ln
        for bit in range(nbits - 1, low_bits - 1, -1):
            size = 1 << bit
            back = ((ln >> (bit + 1)) << (bit + 1)) + size

            @pl.when((ln & size) != 0)
            def _():
                start = pl.multiple_of(end - back, SUBLANES)
                go(pltpu.make_async_copy(zbuf.at[pl.ds(0, size)],
                                         xs_ref.at[pl.ds(start, size)], sem))
        for j in range(SUBLANES - 1):
            @pl.when(j < (ln & (SUBLANES - 1)))
            def _():
                go(pltpu.make_async_copy(zbuf.at[pl.ds(0, 1)], xs_ref.at[pl.ds(st + j, 1)], sem))
        return carry

    lax.fori_loop(0, padstart.shape[0], body, 0)


def _padfill_kernel(padstart, padlen, xs_in, xs_ref, zbuf, sem):
    del xs_in
    zbuf[...] = jnp.zeros(zbuf.shape, zbuf.dtype)
    _pad_fill(padstart, padlen, zbuf, xs_ref, sem, wait=False)
    _pad_fill(padstart, padlen, zbuf, xs_ref, sem, wait=True)


def _padfill(xs, padstart, padlen, bm):
    return pl.pallas_call(
        _padfill_kernel,
        out_shape=jax.ShapeDtypeStruct(xs.shape, xs.dtype),
        grid_spec=pltpu.PrefetchScalarGridSpec(
            num_scalar_prefetch=2,
            grid=(1,),
            in_specs=[pl.BlockSpec(memory_space=pl.ANY)],
            out_specs=pl.BlockSpec(memory_space=pl.ANY),
            scratch_shapes=[pltpu.VMEM((bm, xs.shape[1]), xs.dtype), pltpu.SemaphoreType.DMA(())]),
        input_output_aliases={2: 0},
        compiler_params=_cparams(("arbitrary",)),
        name="padfill",
    )(padstart, padlen, xs)


SC_SCATTER_WINDOW = 128


def _sc_mesh():
    return plsc.VectorSubcoreMesh(core_axis_name="core", subcore_axis_name="subcore")


def _sc_scatter_rows(rows, pos, cap):
    n, dw = rows.shape
    top_k = pos.shape[0]
    mesh = _sc_mesh()
    n_workers = mesh.num_cores * mesh.num_subcores
    win = SC_SCATTER_WINDOW
    per_w = n // n_workers
    n_chunks = per_w // win
    assert per_w * n_workers == n and n_chunks * win == per_w
    pos_w = pos.reshape(top_k, n_workers, n_chunks, win).transpose(1, 2, 0, 3)
    pos_w = pos_w.reshape(n_workers, n_chunks * top_k, win)

    @functools.partial(pl.kernel, mesh=mesh,
                       out_type=jax.ShapeDtypeStruct((cap, dw), rows.dtype),
                       scratch_types=[pltpu.VMEM((n_chunks * top_k, win), I32),
                                      pltpu.VMEM((win, dw), rows.dtype),
                                      pltpu.SemaphoreType.DMA])
    def scatter(r_hbm, p_hbm, o_hbm, idx_v, rows_v, sem):
        wid = lax.axis_index("subcore") * mesh.num_cores + lax.axis_index("core")
        pltpu.sync_copy(p_hbm.at[wid], idx_v)

        @pl.loop(0, n_chunks)
        def _(c):
            pltpu.sync_copy(r_hbm.at[pl.ds(wid * per_w + c * win, win)], rows_v)
            copies = [pltpu.async_copy(rows_v, o_hbm.at[idx_v.at[c * top_k + k]], sem)
                      for k in range(top_k)]
            for cp in copies:
                cp.wait()

    return scatter(rows, pos_w)


GMM_EXPERTS = 4
GMM_PAIR = 4
GMM_SLOTS = 16
GMM_AHEAD = GMM_SLOTS - GMM_PAIR


def _gmm_rows(xw, w13b, w2b, de):
    xb = _unpack_rows(xw).astype(BF16)
    u = jnp.dot(xb, w13b[...], preferred_element_type=F32)
    hmid = _silu(u[:, :de]) * u[:, de:]
    return _pack_rows(jnp.dot(hmid.astype(BF16), w2b[...], preferred_element_type=F32))


def _gmm_kernel(blk0, nblk, w13_ref, w2_ref, xs_ref, ys_ref, w13b, w2b, xbuf, ybuf, isem, osem):
    step = pl.program_id(0)
    n_e = nblk.shape[0]
    slots, bm = xbuf.shape[0], xbuf.shape[1]
    de = w2_ref.shape[1]
    total = blk0[n_e - 1] + nblk[n_e - 1]

    def x_copy(g):
        rows = pl.ds(pl.multiple_of(g * bm, bm), bm)
        slot = g & (slots - 1)
        return pltpu.make_async_copy(xs_ref.at[rows], xbuf.at[slot], isem.at[slot])

    def y_copy(g):
        rows = pl.ds(pl.multiple_of(g * bm, bm), bm)
        slot = g & (slots - 1)
        return pltpu.make_async_copy(ybuf.at[slot], ys_ref.at[rows], osem.at[slot])

    @pl.when(step == 0)
    def _():
        for j in range(GMM_AHEAD):
            @pl.when(j < total)
            def _():
                x_copy(j).start()

    def run_blocks(g0, count):
        gs = [g0 + j for j in range(count)]
        for g in gs:
            x_copy(g).wait()
        for g in gs:
            @pl.when(g + GMM_AHEAD < total)
            def _():
                x_copy(g + GMM_AHEAD).start()

            @pl.when(g >= slots)
            def _():
                y_copy(g - slots).wait()
        for g in gs:
            slot = g & (slots - 1)
            ybuf[slot] = _gmm_rows(xbuf[slot], w13b, w2b, de)
        for g in gs:
            y_copy(g).start()

    def expert_body(j, carry):
        e = step * GMM_EXPERTS + j
        nb = nblk[e]
        b0 = blk0[e]
        w13b[...] = w13_ref[j].astype(BF16)
        w2b[...] = w2_ref[j].astype(BF16)

        def pair_body(i, c):
            run_blocks(b0 + GMM_PAIR * i, GMM_PAIR)
            return c

        lax.fori_loop(0, nb // GMM_PAIR, pair_body, 0)
        for r in range(1, GMM_PAIR):
            @pl.when(nb % GMM_PAIR == r)
            def _():
                run_blocks(b0 + nb - r, r)
        return carry

    lax.fori_loop(0, GMM_EXPERTS, expert_body, 0)

    @pl.when(step == pl.num_programs(0) - 1)
    def _():
        for j in range(1, slots + 1):
            @pl.when(total - j >= 0)
            def _():
                y_copy(total - j).wait()


def _gmm(xs, w13, w2, blk0, nblk, bm):
    cap, dw = xs.shape
    e, d, de2 = w13.shape
    de = w2.shape[1]
    return pl.pallas_call(
        _gmm_kernel,
        out_shape=jax.ShapeDtypeStruct((cap, dw), U32),
        grid_spec=pltpu.PrefetchScalarGridSpec(
            num_scalar_prefetch=2,
            grid=(e // GMM_EXPERTS,),
            in_specs=[pl.BlockSpec((GMM_EXPERTS, d, de2), lambda i, b0, nb: (i, 0, 0)),
                      pl.BlockSpec((GMM_EXPERTS, de, d), lambda i, b0, nb: (i, 0, 0)),
                      pl.BlockSpec(memory_space=pl.ANY)],
            out_specs=pl.BlockSpec(memory_space=pl.ANY),
            scratch_shapes=[pltpu.VMEM((d, de2), BF16), pltpu.VMEM((de, d), BF16),
                            pltpu.VMEM((GMM_SLOTS, bm, dw), U32),
                            pltpu.VMEM((GMM_SLOTS, bm, dw), U32),
                            pltpu.SemaphoreType.DMA((GMM_SLOTS,)),
                            pltpu.SemaphoreType.DMA((GMM_SLOTS,))]),
        compiler_params=_cparams(("arbitrary",)),
        name="gmm",
    )(blk0, nblk, w13, w2, xs)


def _sc_gather_rows(table, idx, window):
    n_idx = idx.shape[0]
    dw = table.shape[1]
    mesh = _sc_mesh()
    n_workers = mesh.num_cores * mesh.num_subcores
    per_w = n_idx // n_workers
    n_chunks = per_w // window
    assert per_w * n_workers == n_idx and n_chunks * window == per_w and n_chunks % 2 == 0

    @functools.partial(pl.kernel, mesh=mesh,
                       out_type=jax.ShapeDtypeStruct((n_idx, dw), table.dtype),
                       scratch_types=[pltpu.VMEM((per_w,), I32),
                                      pltpu.VMEM((window, dw), table.dtype),
                                      pltpu.VMEM((window, dw), table.dtype),
                                      pltpu.SemaphoreType.DMA, pltpu.SemaphoreType.DMA,
                                      pltpu.SemaphoreType.DMA, pltpu.SemaphoreType.DMA])
    def gather(x_hbm, i_hbm, o_hbm, idx_v, rows0, rows1, gs0, gs1, os0, os1):
        wid = lax.axis_index("subcore") * mesh.num_cores + lax.axis_index("core")
        base = wid * per_w
        pltpu.sync_copy(i_hbm.at[pl.ds(base, per_w)], idx_v)
        bufs = ((rows0, gs0, os0), (rows1, gs1, os1))

        def fetch(j, slot):
            rows, gs, _ = bufs[slot]
            off = pl.multiple_of(j * window, window)
            return pltpu.async_copy(x_hbm.at[idx_v.at[pl.ds(off, window)]], rows, gs)

        def fetch_wait(slot):
            rows, gs, _ = bufs[slot]
            pltpu.make_async_copy(x_hbm.at[idx_v.at[pl.ds(0, window)]], rows, gs).wait()

        def put(j, slot):
            rows, _, osem = bufs[slot]
            off = pl.multiple_of(j * window, window)
            return pltpu.async_copy(rows, o_hbm.at[pl.ds(base + off, window)], osem)

        def put_wait(slot):
            rows, _, osem = bufs[slot]
            pltpu.make_async_copy(rows, o_hbm.at[pl.ds(base, window)], osem).wait()

        fetch(0, 0)

        @pl.loop(0, n_chunks, step=2)
        def _(j):
            @pl.when(j > 0)
            def _():
                put_wait(1)
            fetch(j + 1, 1)
            fetch_wait(0)
            put(j, 0)
            put_wait(0)

            @pl.when(j + 2 < n_chunks)
            def _():
                fetch(j + 2, 0)
            fetch_wait(1)
            put(j + 1, 1)

        put_wait(1)

    return gather(table, idx)


def _combine_kernel(gate_ref, x1_ref, h2_ref, mod_ref, gain_ref, sw13_ref, sw2_ref, yg_ref, o_ref):
    ds_ = sw2_ref.shape[0]
    hb = _unpack_rows(h2_ref[...]).astype(BF16)
    u = jnp.dot(hb, sw13_ref[...], preferred_element_type=F32)
    hmid = _silu(u[:, :ds_]) * u[:, ds_:]
    y = jnp.dot(hmid.astype(BF16), sw2_ref[...], preferred_element_type=F32)

    gate = gate_ref[...]
    moe = gate[:, 0:1] * _unpack_rows(yg_ref[0])
    for k in range(1, TOP_K):
        moe = moe + gate[:, k:k + 1] * _unpack_rows(yg_ref[k])
    xo = x1_ref[...] + mod_ref[0, 5:6, :] * (moe + y)
    ms = jnp.mean(xo * xo, axis=-1, keepdims=True)
    o_ref[...] = xo * lax.rsqrt(ms + EPS) * gain_ref[...]


def _combine(gate_tok, x1, h2, mods, gain, sw13, sw2, yg, tc, tiles_per_batch):
    n, d = x1.shape
    dw = h2.shape[1]
    nt = n // tc
    ds2 = sw13.shape[1]
    ds_ = sw2.shape[0]
    return pl.pallas_call(
        _combine_kernel,
        out_shape=jax.ShapeDtypeStruct((n, d), F32),
        grid=(nt,),
        in_specs=[pl.BlockSpec((tc, TOP_K), lambda i: (i, 0)),
                  pl.BlockSpec((tc, d), lambda i: (i, 0)),
                  pl.BlockSpec((tc, dw), lambda i: (i, 0)),
                  pl.BlockSpec((1, 8, d), lambda i: (i // tiles_per_batch, 0, 0)),
                  pl.BlockSpec((1, d), lambda i: (0, 0)),
                  pl.BlockSpec((d, ds2), lambda i: (0, 0)),
                  pl.BlockSpec((ds_, d), lambda i: (0, 0)),
                  pl.BlockSpec((TOP_K, tc, dw), lambda i: (0, i, 0))],
        out_specs=pl.BlockSpec((tc, d), lambda i: (i, 0)),
        compiler_params=_cparams(("arbitrary",)),
        name="combine",
    )(gate_tok, x1, h2, mods, gain, sw13, sw2, yg)


def _prep_w_in(w, lru_w, hg_w):
    z0 = 2 * lru_w + 2 * hg_w
    col = jnp.arange(w.shape[1])
    scale = jnp.where((col >= z0) & (col < z0 + 2 * hg_w), 0.5, 1.0).astype(w.dtype)
    return (w * scale[None, :]).astype(BF16)


def _block_diag_pairs(w, per):
    nb, bd, _ = w.shape
    w = w.reshape(nb // per, per, bd, bd)
    eye = jnp.eye(per, dtype=w.dtype)
    out = jnp.einsum("gpij,pq->gpiqj", w, eye)
    return out.reshape(nb // per, per * bd, per * bd)


def kernel(x, c, ctx, c_ctx, ada_w, ada_b, norm_mix, norm_ffn, norm_final, w_in, w_out,
           lru_conv_w, lru_conv_b, lru_wa, lru_ba, lru_wx, lru_bx, lru_lambda,
           hgrn_lb_logits, hgrn_norm, router_w, router_b, exp_w13, exp_w2, shared_w13, shared_w2):
    assert ada_w.shape[0] == 1, "single-layer block"
    b, s, d = x.shape
    n = b * s
    lru_w = lru_conv_w.shape[2]
    n_experts = router_w.shape[2]

    rows = -(-(b + 1) // SUBLANES) * SUBLANES
    cs = jnp.zeros((rows, d), F32).at[:b].set(c).at[b].set(c_ctx)
    mod = _ada(cs, ada_w[0], ada_b[0][None, :]).reshape(rows, 6, d)
    mods = jnp.pad(mod, ((0, 0), (0, 2), (0, 0)))

    w_in_bf = _prep_w_in(w_in[0], lru_w, hgrn_norm.shape[1])
    gain1 = norm_mix[0][None, :]
    projx = _inproj(x, mods, gain1, w_in_bf, PROJ_TILE, shared_mod=False)
    projc = _inproj(ctx, mods[b:b + 1], gain1, w_in_bf, ctx.shape[1], shared_mod=True)

    per = LRU_G // (lru_w // LRU_BLOCKS)
    wg = jnp.stack([_block_diag_pairs(lru_wa[0, 0], per), _block_diag_pairs(lru_wx[0, 0], per),
                    _block_diag_pairs(lru_wa[0, 1], per), _block_diag_pairs(lru_wx[0, 1], per)]
                   )
    wg = (0.5 * wg).astype(BF16)
    bg = 0.5 * jnp.stack([lru_ba[0, 0], lru_bx[0, 0], lru_ba[0, 1], lru_bx[0, 1]])
    ylru = _lru(projx, projc, lru_conv_w[0], lru_conv_b[0][None, :], wg, bg, lru_lambda[0])

    yhg = _hgrn(projx, projc, hgrn_lb_logits, hgrn_norm[0][None, :], (2 * lru_w) // LANES)

    wo = w_out[0].astype(BF16).reshape(2, lru_w, d)
    x1, h2, eid_t, gate_t, rank_t, cnt = _outproj(
        ylru, yhg, x, mods, norm_ffn[0][None, :], wo,
        router_w[0].T.astype(BF16), router_b[0][:, None], PROJ_TILE)

    bm = GMM_BLOCK
    counts = cnt[:, 0]
    padded = (counts + bm - 1) // bm * bm
    pends = jnp.cumsum(padded)
    pstarts = pends - padded
    pos = _pos(pstarts, eid_t, rank_t, POS_TILE)
    cap = n * TOP_K + n_experts * bm

    xs = _padfill(_sc_scatter_rows(h2, pos, cap), pstarts + counts, padded - counts, bm)
    ys = _gmm(xs, exp_w13[0], exp_w2[0], pstarts // bm, padded // bm, bm)

    yg = _sc_gather_rows(ys, pos.reshape(-1), SC_GATHER_WINDOW).reshape(TOP_K, n, d // 2)
    tc = COMBINE_TILE
    out = _combine(gate_t.T, x1.reshape(n, d), h2, mods,
                   norm_final[None, :], shared_w13[0].astype(BF16), shared_w2[0].astype(BF16),
                   yg, tc, s // tc)
    return out.reshape(b, s, d)
```

```python
import functools

import jax
import jax.numpy as jnp
from jax import lax
from jax.experimental import pallas as pl
from jax.experimental.pallas import tpu as pltpu
from jax.experimental.pallas import tpu_sc as plsc

F32 = jnp.float32
BF16 = jnp.bfloat16
I32 = jnp.int32
U32 = jnp.uint32

EPS = 1e-6
LRU_C = 8.0
ROUTE_SCALE = 2.5
GRID_W = 64
HG_CHUNK = 32
N_GROUPS = 8
TOPK_GROUPS = 4
TOP_K = 8
LRU_BLOCKS = 8
HG_HEADS = 4
CONV_W = 4
CONV_PAD_L = 1

SUBLANES = 8
LANES = 128
N_SEG = SUBLANES
VMEM_LIMIT = 56 * 1024 * 1024

PROJ_TILE = 512
POS_TILE = 2048
GMM_BLOCK = 256
SC_GATHER_WINDOW = 64
COMBINE_TILE = 512
MOE_TAIL_PARTS = 2


def _cparams(sem, vmem=VMEM_LIMIT):
    return pltpu.CompilerParams(dimension_semantics=sem, vmem_limit_bytes=vmem)


def _sigmoid(x):
    return jax.nn.sigmoid(x)


def _silu(x):
    return x * _sigmoid(x)


def _pack_rows(x):
    w = x.shape[1] // 2
    bits = pltpu.bitcast(x.astype(BF16).astype(F32), U32)
    return (bits[:, :w] >> 16) | (bits[:, w:] & jnp.uint32(0xFFFF0000))


def _unpack_rows(p):
    lo = pltpu.bitcast(p << 16, F32)
    hi = pltpu.bitcast(p & jnp.uint32(0xFFFF0000), F32)
    return jnp.concatenate([lo, hi], axis=1)


def _gelu_tanh(x):
    c = 0.7978845608028654
    return 0.5 * x * (1.0 + jnp.tanh(c * (x + 0.044715 * (x * x * x))))


def _ada_kernel(c_ref, w_ref, b_ref, o_ref):
    s = _silu(c_ref[...])
    o_ref[...] = jnp.dot(s, w_ref[...], preferred_element_type=F32,
                         precision=lax.Precision.HIGHEST) + b_ref[...]


def _ada(cs, w, b):
    rows, d = cs.shape
    n = w.shape[1]
    bn = 1024
    return pl.pallas_call(
        _ada_kernel,
        out_shape=jax.ShapeDtypeStruct((rows, n), F32),
        grid=(n // bn,),
        in_specs=[pl.BlockSpec((rows, d), lambda j: (0, 0)),
                  pl.BlockSpec((d, bn), lambda j: (0, j)),
                  pl.BlockSpec((1, bn), lambda j: (0, j))],
        out_specs=pl.BlockSpec((rows, bn), lambda j: (0, j)),
        compiler_params=_cparams(("arbitrary",)),
        name="ada",
    )(cs, w, b)


def _inproj_kernel(x_ref, mod_ref, gain_ref, w_ref, o_ref):
    x = x_ref[0]
    ms = jnp.mean(x * x, axis=-1, keepdims=True)
    y = x * lax.rsqrt(ms + EPS) * gain_ref[...]
    h = y * (1.0 + mod_ref[0, 1:2, :]) + mod_ref[0, 0:1, :]
    o_ref[0] = jnp.dot(h.astype(BF16), w_ref[...], preferred_element_type=F32)


def _inproj(x, mods, gain, w_bf, tm, shared_mod):
    b, s, d = x.shape
    n = w_bf.shape[1]
    mod_map = (lambda bi, i: (0, 0, 0)) if shared_mod else (lambda bi, i: (bi, 0, 0))
    return pl.pallas_call(
        _inproj_kernel,
        out_shape=jax.ShapeDtypeStruct((b, s, n), F32),
        grid=(b, s // tm),
        in_specs=[pl.BlockSpec((1, tm, d), lambda bi, i: (bi, i, 0)),
                  pl.BlockSpec((1, 8, d), mod_map),
                  pl.BlockSpec((1, d), lambda bi, i: (0, 0)),
                  pl.BlockSpec((d, n), lambda bi, i: (0, 0))],
        out_specs=pl.BlockSpec((1, tm, n), lambda bi, i: (bi, i, 0)),
        compiler_params=_cparams(("arbitrary", "arbitrary")),
        name="inproj",
    )(x, mods, gain, w_bf)


LRU_G = 256
LRU_CHUNK = 256


def _seg_rows(s, seg_len):
    return pl.ds(s, seg_len, stride=N_SEG)


def _out_pitch(seg_len):
    return seg_len + SUBLANES


def _lane_store(ref, d, rows, val):
    nl = val.shape[-1] // LANES
    for l in range(nl):
        ref[d * nl + l, rows, :] = val[:, l * LANES:(l + 1) * LANES]


def _lane_load(ref, d, rows, nl):
    return jnp.concatenate([ref[d * nl + l, rows, :] for l in range(nl)], axis=-1)


LRU_CONV_PIECE = 64


def _lru_coeffs(pad_ref, t0, rows, cw, cb, wg_ref, bg, sp, a_ref, b_ref, dst_rows, u_ref):
    for p0 in range(0, rows, LRU_CONV_PIECE):
        win = pad_ref[pl.ds(t0 + p0, LRU_CONV_PIECE + 2 * SUBLANES), :]
        u = cb
        for k in range(CONV_W):
            off = SUBLANES - CONV_PAD_L + k
            u = u + win[off:off + LRU_CONV_PIECE, :] * cw[k:k + 1, :]
        u_ref[p0:p0 + LRU_CONV_PIECE, :] = u
    u = u_ref[0:rows, :]
    ub = u.astype(BF16)
    half_u = 0.5 * u
    for d in range(2):
        tr = jnp.tanh(jnp.dot(ub, wg_ref[2 * d, 0], preferred_element_type=F32)
                      + bg[2 * d:2 * d + 1, :])
        ti = jnp.tanh(jnp.dot(ub, wg_ref[2 * d + 1, 0], preferred_element_type=F32)
                      + bg[2 * d + 1:2 * d + 2, :])
        half_c = (-0.5 * LRU_C) * sp[d:d + 1, :]
        log_a = half_c * tr + half_c
        a = jnp.exp(log_a)
        one_minus_a2 = -jnp.tanh(log_a) * (1.0 + a * a)
        root = jnp.where(one_minus_a2 > 0.0, one_minus_a2 * lax.rsqrt(one_minus_a2), 0.0)
        bb = root * (ti * half_u + half_u)
        _lane_store(a_ref, d, dst_rows, a)
        _lane_store(b_ref, d, dst_rows, bb)


def _seg_scan(a_ref, b_ref, h_ref, p_ref, seg_len, unroll):
    n_lead = a_ref.shape[0]
    nl = n_lead // 2
    zero = jnp.zeros((N_SEG, LANES), F32)
    one = jnp.ones((N_SEG, LANES), F32)
    init = tuple((zero, one) for _ in range(n_lead))

    def step(t, carry):
        out = []
        for i in range(n_lead):
            h, p = carry[i]
            tt = t if i < nl else seg_len - 1 - t
            rows = pl.ds(pl.multiple_of(tt * N_SEG, N_SEG), N_SEG)
            a = a_ref[i, rows, :]
            h = a * h + b_ref[i, rows, :]
            p = p * a
            if h_ref is not None:
                out_rows = pl.ds(tt, N_SEG, stride=_out_pitch(seg_len))
                h_ref[i, out_rows, :] = h
                p_ref[i, out_rows, :] = p
            out.append((h, p))
        return tuple(out)

    def body(i, carry):
        for j in range(unroll):
            carry = step(i * unroll + j, carry)
        return carry

    ends = lax.fori_loop(0, seg_len // unroll, body, init)
    h_end = [jnp.concatenate([ends[d * nl + l][0] for l in range(nl)], axis=-1) for d in range(2)]
    p_end = [jnp.concatenate([ends[d * nl + l][1] for l in range(nl)], axis=-1) for d in range(2)]
    return h_end, p_end


def _seg_carries(h_end, p_end, h0, reverse):
    order = range(N_SEG - 1, -1, -1) if reverse else range(N_SEG)
    cin = [None] * N_SEG
    c = h0
    for s in order:
        cin[s] = c
        c = p_end[s:s + 1, :] * c + h_end[s:s + 1, :]
    return cin, c


def _lru_kernel(rx_ref, rg_ref, rc_ref, cw_ref, cb_ref, wg_ref, bg_ref, lam_ref, o_ref,
                padl, padc, u_s, a_l, b_l, h_l, p_l, a_c, b_c):
    s_len = rx_ref.shape[1]
    c_len = rc_ref.shape[1]
    g = rx_ref.shape[2]
    nl = g // LANES
    seg_l = s_len // N_SEG
    seg_c = c_len // N_SEG

    zeros = jnp.zeros((SUBLANES, g), F32)
    padl[0:SUBLANES, :] = zeros
    padl[SUBLANES + s_len:2 * SUBLANES + s_len, :] = zeros
    padc[0:SUBLANES, :] = zeros
    padc[SUBLANES + c_len:2 * SUBLANES + c_len, :] = zeros
    for i in range(s_len // LRU_CHUNK):
        padl[SUBLANES + i * LRU_CHUNK:SUBLANES + (i + 1) * LRU_CHUNK, :] = (
            rx_ref[0, i * LRU_CHUNK:(i + 1) * LRU_CHUNK, :])
    padc[SUBLANES:SUBLANES + c_len, :] = rc_ref[0]

    cw = cw_ref[...]
    cb = cb_ref[...]
    bg = bg_ref[...]
    x = -lam_ref[...]
    sp = jnp.maximum(x, 0.0) + jnp.log(1.0 + jnp.exp(-jnp.abs(x)))

    _lru_coeffs(padc, 0, c_len, cw, cb, wg_ref, bg, sp, a_l, b_l, pl.ds(0, c_len), u_s)
    for i in range(2 * nl):
        for s in range(N_SEG):
            a_c[i, _seg_rows(s, seg_c), :] = a_l[i, s * seg_c:(s + 1) * seg_c, :]
            b_c[i, _seg_rows(s, seg_c), :] = b_l[i, s * seg_c:(s + 1) * seg_c, :]
    h_end, p_end = _seg_scan(a_c, b_c, None, None, seg_c, unroll=4)
    zero_row = jnp.zeros((1, g), F32)
    _, h0_f = _seg_carries(h_end[0], p_end[0], zero_row, reverse=False)
    _, h0_b = _seg_carries(h_end[1], p_end[1], zero_row, reverse=True)

    def coeff_body(s, carry):
        t0 = pl.multiple_of(s * seg_l, SUBLANES)
        _lru_coeffs(padl, t0, seg_l, cw, cb, wg_ref, bg, sp, a_l, b_l, _seg_rows(s, seg_l), u_s)
        return carry

    lax.fori_loop(0, N_SEG, coeff_body, 0)
    h_end, p_end = _seg_scan(a_l, b_l, h_l, p_l, seg_l, unroll=4)
    cin_f, _ = _seg_carries(h_end[0], p_end[0], h0_f, reverse=False)
    cin_b, _ = _seg_carries(h_end[1], p_end[1], h0_b, reverse=True)

    for s in range(N_SEG):
        rows = pl.ds(s * _out_pitch(seg_l), seg_l)
        h = (_lane_load(h_l, 0, rows, nl) + _lane_load(p_l, 0, rows, nl) * cin_f[s]) + (
            _lane_load(h_l, 1, rows, nl) + _lane_load(p_l, 1, rows, nl) * cin_b[s])
        o_ref[0, s * seg_l:(s + 1) * seg_l, :] = _gelu_tanh(rg_ref[0, s * seg_l:(s + 1) * seg_l, :]) * h


def _lru(projx, projc, cw, cb, wg, bg, lam):
    b, s_len, _ = projx.shape
    c_len = projc.shape[1]
    w = cw.shape[1]
    g = LRU_G
    ng = w // g
    seg_l = s_len // N_SEG
    seg_c = c_len // N_SEG
    assert seg_l == LRU_CHUNK and c_len % (N_SEG * 1) == 0
    return pl.pallas_call(
        _lru_kernel,
        out_shape=jax.ShapeDtypeStruct((b, s_len, w), F32),
        grid=(b, ng),
        in_specs=[pl.BlockSpec((1, s_len, g), lambda bi, j: (bi, 0, j)),
                  pl.BlockSpec((1, s_len, g), lambda bi, j: (bi, 0, ng + j)),
                  pl.BlockSpec((1, c_len, g), lambda bi, j: (bi, 0, j)),
                  pl.BlockSpec((CONV_W, g), lambda bi, j: (0, j)),
                  pl.BlockSpec((1, g), lambda bi, j: (0, j)),
                  pl.BlockSpec((4, 1, g, g), lambda bi, j: (0, j, 0, 0)),
                  pl.BlockSpec((4, g), lambda bi, j: (0, j)),
                  pl.BlockSpec((2, g), lambda bi, j: (0, j))],
        out_specs=pl.BlockSpec((1, s_len, g), lambda bi, j: (bi, 0, j)),
        scratch_shapes=[pltpu.VMEM((s_len + 2 * SUBLANES, g), F32),
                        pltpu.VMEM((c_len + 2 * SUBLANES, g), F32),
                        pltpu.VMEM((max(seg_l, c_len), g), F32),
                        pltpu.VMEM((2 * g // LANES, s_len, LANES), F32),
                        pltpu.VMEM((2 * g // LANES, s_len, LANES), F32),
                        pltpu.VMEM((2 * g // LANES, N_SEG * _out_pitch(seg_l), LANES), F32),
                        pltpu.VMEM((2 * g // LANES, N_SEG * _out_pitch(seg_l), LANES), F32),
                        pltpu.VMEM((2 * g // LANES, c_len, LANES), F32),
                        pltpu.VMEM((2 * g // LANES, c_len, LANES), F32)],
        compiler_params=_cparams(("arbitrary", "arbitrary")),
        name="lru",
    )(projx, projx, projc, cw, cb, wg, bg, lam)


LOG2_E = 1.4426950408889634
HG_PITCH_L = GRID_W + SUBLANES
HG_PITCH_C = SUBLANES


def _hg_gates(half_z, lb):
    c = 0.5 * (1.0 - lb)
    ct = c * jnp.tanh(half_z)
    logf = jnp.log(0.5 * (1.0 + lb) + ct)
    k = c - ct
    return logf, k


def _hg_prepass(load, width, pitch, lb, qd, kd, kl, vs, dec):
    n_pos = HG_CHUNK
    for d, zname in enumerate(("zf", "zb")):
        order = range(n_pos) if d == 0 else range(n_pos - 1, -1, -1)
        g = jnp.zeros((width, LANES), F32)
        lbd = lb[d:d + 1, :]
        for p in order:
            logf, k = _hg_gates(load(zname, p), lbd)
            g = g + logf
            qd[d, p * pitch:p * pitch + width, :] = g
            kl[d, p * pitch:p * pitch + width, :] = k
        g_last = g * LOG2_E
        dec[d, 0:width, :] = jnp.exp2(g_last)
        for p in range(n_pos):
            rows = slice(p * pitch, p * pitch + width)
            gp = qd[d, rows, :] * LOG2_E
            k = kl[d, rows, :]
            qd[d, rows, :] = load("q", p) * jnp.exp2(gp)
            kd[d, rows, :] = k * jnp.exp2(-gp)
            kl[d, rows, :] = k * jnp.exp2(g_last - gp)
    for p in range(n_pos):
        vs[p * pitch:p * pitch + width, :] = load("v", p)


HG_GROUP = 4
HG_UNROLL = 8
HG_INTRA_UNROLL = 4
_NT = (((1,), (1,)), ((), ()))
_TN = (((0,), (0,)), ((), ()))


def _hg_group_rows(ref, lead, c0, pitch):
    parts = []
    for j in range(HG_GROUP):
        rows = pl.ds(c0 + j, HG_CHUNK, stride=pitch)
        parts.append(ref[rows, :] if lead is None else ref[lead, rows, :])
    return jnp.concatenate(parts, axis=0)


def _hg_scores(g, pitch, qd, kd, masks, sc_ref):
    c0 = g * HG_GROUP
    total = None
    for d in range(2):
        q = _hg_group_rows(qd, d, c0, pitch).astype(BF16)
        k = _hg_group_rows(kd, d, c0, pitch).astype(BF16)
        sc = jnp.where(masks[d], lax.dot_general(q, k, _NT, preferred_element_type=F32), 0.0)
        total = sc if total is None else total + sc
    sc_ref[g] = total.astype(BF16)


def _hg_summaries(g, pitch, kl, vs, sc_ref, oi, kv):
    c0 = g * HG_GROUP
    v = _hg_group_rows(vs, None, c0, pitch).astype(BF16)
    if oi is not None:
        o = jnp.dot(sc_ref[g], v, preferred_element_type=F32)
        for j in range(HG_GROUP):
            oi[pl.ds(c0 + j, HG_CHUNK, stride=pitch), :] = o[j * HG_CHUNK:(j + 1) * HG_CHUNK, :]
    kls = jnp.concatenate([_hg_group_rows(kl, 0, c0, pitch), _hg_group_rows(kl, 1, c0, pitch)],
                          axis=1).astype(BF16)
    for j in range(HG_GROUP):
        sl = slice(j * HG_CHUNK, (j + 1) * HG_CHUNK)
        both = lax.dot_general(v[sl], kls[sl], _TN, preferred_element_type=F32)
        kv[0, c0 + j] = both[:, :LANES]
        kv[1, c0 + j] = both[:, LANES:]


def _hg_steps(step0, n_steps, n_chunks, pitch, qd, dec, kv, states, ox):
    states = list(states)
    for u in range(n_steps):
        cs = (step0 + u, n_chunks - 1 - (step0 + u))
        if ox is not None:
            rows = [pl.ds(c, HG_CHUNK, stride=pitch) for c in cs]
            q2 = jnp.concatenate([qd[0, rows[0], :], qd[1, rows[1], :]], axis=0).astype(BF16)
            s2 = jnp.concatenate(states, axis=0).astype(BF16)
            o2 = lax.dot_general(q2, s2, _NT, preferred_element_type=F32)
            ox[0, rows[0], :] = o2[:HG_CHUNK, :LANES]
            ox[1, rows[1], :] = o2[HG_CHUNK:, LANES:]
        for d in range(2):
            states[d] = states[d] * dec[d, pl.ds(cs[d], 1), :] + kv[d, cs[d]]
    return tuple(states)


def _hgrn_kernel(q_ref, v_ref, zf_ref, zb_ref, g_ref, qc_ref, vc_ref, zfc_ref, zbc_ref,
                 lbl_ref, gain_ref, o_ref, qd, kd, kl, vs, dec, oi, ox, kv, sc):
    s_len = q_ref.shape[1]
    c_len = qc_ref.shape[1]
    n_col = s_len // HG_CHUNK
    n_cc = c_len // HG_CHUNK
    assert n_col == GRID_W and n_cc == HG_PITCH_C
    assert n_col % HG_GROUP == 0 and n_cc % HG_GROUP == 0 and n_col % HG_UNROLL == 0

    lg = lbl_ref[...]
    m = jnp.max(lg, axis=1, keepdims=True)
    ex = jnp.exp(lg - m)
    lb = ex[:, 0, :] / jnp.sum(ex, axis=1)

    gr = HG_GROUP * HG_CHUNK
    ii = lax.broadcasted_iota(I32, (gr, gr), 0)
    jj = lax.broadcasted_iota(I32, (gr, gr), 1)
    same = (ii // HG_CHUNK) == (jj // HG_CHUNK)
    masks = (jnp.logical_and(same, jj <= ii), jnp.logical_and(same, jj >= ii))

    c_refs = {"q": qc_ref, "v": vc_ref, "zf": zfc_ref, "zb": zbc_ref}

    def load_c(name, p):
        return c_refs[name][0, pl.ds(p, n_cc, stride=HG_CHUNK), :]

    _hg_prepass(load_c, n_cc, HG_PITCH_C, lb, qd, kd, kl, vs, dec)
    for g in range(n_cc // HG_GROUP):
        _hg_summaries(g, HG_PITCH_C, kl, vs, None, None, kv)
    zero = jnp.zeros((LANES, LANES), F32)
    states = _hg_steps(0, n_cc, n_cc, HG_PITCH_C, qd, dec, kv, (zero, zero), None)

    l_refs = {"q": q_ref, "v": v_ref, "zf": zf_ref, "zb": zb_ref}

    def load_l(name, p):
        return l_refs[name][0, p * n_col:(p + 1) * n_col, :]

    _hg_prepass(load_l, n_col, HG_PITCH_L, lb, qd, kd, kl, vs, dec)

    n_it = n_col // (HG_GROUP * HG_INTRA_UNROLL)

    def scores_body(i, carry):
        for u in range(HG_INTRA_UNROLL):
            _hg_scores(i * HG_INTRA_UNROLL + u, HG_PITCH_L, qd, kd, masks, sc)
        return carry

    lax.fori_loop(0, n_it, scores_body, 0)

    def summaries_body(i, carry):
        for u in range(HG_INTRA_UNROLL):
            _hg_summaries(i * HG_INTRA_UNROLL + u, HG_PITCH_L, kl, vs, sc, oi, kv)
        return carry

    lax.fori_loop(0, n_it, summaries_body, 0)

    def step_body(i, carry):
        return _hg_steps(i * HG_UNROLL, HG_UNROLL, n_col, HG_PITCH_L, qd, dec, kv, carry, ox)

    lax.fori_loop(0, n_col // HG_UNROLL, step_body, states)

    gain = gain_ref[...]
    for r in range(HG_CHUNK):
        rows = slice(r * HG_PITCH_L, r * HG_PITCH_L + n_col)
        o = oi[rows, :] + (ox[0, rows, :] + ox[1, rows, :])
        y = o * lax.rsqrt(jnp.mean(o * o, axis=-1, keepdims=True) + EPS) * gain
        o_ref[0, r * n_col:(r + 1) * n_col, :] = y * _silu(g_ref[0, r * n_col:(r + 1) * n_col, :])


def _hgrn(projx, projc, lbl, gain, col0):
    b, s_len, _ = projx.shape
    c_len = projc.shape[1]
    hw = gain.shape[1]
    nh = hw // LANES
    n_slots = lbl.shape[1]

    def xs(k):
        return pl.BlockSpec((1, s_len, LANES), lambda bi, h, k=k: (bi, 0, col0 + k * nh + h))

    def cs(k):
        return pl.BlockSpec((1, c_len, LANES), lambda bi, h, k=k: (bi, 0, col0 + k * nh + h))

    sc_rows = HG_CHUNK * HG_PITCH_L
    return pl.pallas_call(
        _hgrn_kernel,
        out_shape=jax.ShapeDtypeStruct((b, s_len, hw), F32),
        grid=(b, nh),
        in_specs=[xs(0), xs(1), xs(2), xs(3), xs(4), cs(0), cs(1), cs(2), cs(3),
                  pl.BlockSpec((2, n_slots, LANES), lambda bi, h: (0, 0, h)),
                  pl.BlockSpec((1, LANES), lambda bi, h: (0, h))],
        out_specs=pl.BlockSpec((1, s_len, LANES), lambda bi, h: (bi, 0, h)),
        scratch_shapes=[pltpu.VMEM((2, sc_rows, LANES), F32),
                        pltpu.VMEM((2, sc_rows, LANES), F32),
                        pltpu.VMEM((2, sc_rows, LANES), F32),
                        pltpu.VMEM((sc_rows, LANES), F32),
                        pltpu.VMEM((2, HG_PITCH_L, LANES), F32),
                        pltpu.VMEM((sc_rows, LANES), F32),
                        pltpu.VMEM((2, sc_rows, LANES), F32),
                        pltpu.VMEM((2, GRID_W, LANES, LANES), F32),
                        pltpu.VMEM((GRID_W // HG_GROUP, HG_GROUP * HG_CHUNK,
                                    HG_GROUP * HG_CHUNK), BF16)],
        compiler_params=_cparams(("arbitrary", "arbitrary")),
        name="hgrn",
    )(projx, projx, projx, projx, projx, projc, projc, projc, projc, lbl, gain)


def _first_index_of_max(vals, iota, big):
    m = jnp.max(vals, axis=0, keepdims=True)
    idx = jnp.min(jnp.where(vals == m, iota, big), axis=0, keepdims=True)
    return m, idx


def _route(logits_t, rb, n_experts):
    t = logits_t.shape[1]
    gsz = n_experts // N_GROUPS
    neg = -jnp.inf
    scores = _sigmoid(logits_t)
    biased = scores + rb
    iota_g = lax.broadcasted_iota(I32, (gsz, t), 0)
    gscore = []
    for gi in range(N_GROUPS):
        blk = biased[gi * gsz:(gi + 1) * gsz, :]
        m1, i1 = _first_index_of_max(blk, iota_g, gsz)
        m2 = jnp.max(jnp.where(iota_g == i1, neg, blk), axis=0, keepdims=True)
        gscore.append(m1 + m2)
    gs = jnp.concatenate(gscore, axis=0)
    iota_n = lax.broadcasted_iota(I32, (N_GROUPS, t), 0)
    gsel = jnp.zeros((N_GROUPS, t), jnp.bool_)
    for _ in range(TOPK_GROUPS):
        _, gi1 = _first_index_of_max(gs, iota_n, N_GROUPS)
        hit = iota_n == gi1
        gsel = jnp.logical_or(gsel, hit)
        gs = jnp.where(hit, neg, gs)
    emask = jnp.concatenate(
        [jnp.broadcast_to(gsel[gi:gi + 1, :], (gsz, t)) for gi in range(N_GROUPS)], axis=0)
    masked = jnp.where(emask, biased, neg)
    iota_e = lax.broadcasted_iota(I32, (n_experts, t), 0)
    ids, gates = [], []
    sel = jnp.zeros((n_experts, t), F32)
    for _ in range(TOP_K):
        _, ei = _first_index_of_max(masked, iota_e, n_experts)
        hit = iota_e == ei
        ids.append(ei)
        gates.append(jnp.sum(jnp.where(hit, scores, 0.0), axis=0, keepdims=True))
        sel = jnp.where(hit, 1.0, sel)
        masked = jnp.where(hit, neg, masked)
    ids = jnp.concatenate(ids, axis=0)
    gates = jnp.concatenate(gates, axis=0)
    gates = gates / jnp.sum(gates, axis=0, keepdims=True) * ROUTE_SCALE
    return ids, gates, sel


def _outproj_kernel(yl_ref, yh_ref, x_ref, mod_ref, gain_ref, wo_ref, rwt_ref, rb_ref,
                    x1_ref, h2_ref, eid_ref, gate_ref, rank_ref, cnt_ref, carry):
    first = jnp.logical_and(pl.program_id(0) == 0, pl.program_id(1) == 0)

    @pl.when(first)
    def _():
        carry[...] = jnp.zeros(carry.shape, F32)

    y = jnp.dot(yl_ref[0].astype(BF16), wo_ref[0], preferred_element_type=F32)
    y = y + jnp.dot(yh_ref[0].astype(BF16), wo_ref[1], preferred_element_type=F32)
    x1 = x_ref[0] + mod_ref[0, 2:3, :] * y
    x1_ref[0] = x1
    ms = jnp.mean(x1 * x1, axis=-1, keepdims=True)
    h2 = x1 * lax.rsqrt(ms + EPS) * gain_ref[...]
    h2 = h2 * (1.0 + mod_ref[0, 4:5, :]) + mod_ref[0, 3:4, :]
    h2_ref[...] = _pack_rows(h2)

    n_experts = rwt_ref.shape[0]
    t = h2.shape[0]
    logits_t = lax.dot_general(rwt_ref[...], h2.astype(BF16), _NT,
                               preferred_element_type=F32)
    ids, gates, sel = _route(logits_t, rb_ref[...], n_experts)
    eid_ref[...] = ids
    gate_ref[...] = gates

    ti = lax.broadcasted_iota(I32, (t, t), 0)
    tj = lax.broadcasted_iota(I32, (t, t), 1)
    upper = (ti < tj).astype(BF16)
    selb = sel.astype(BF16)
    before = jnp.dot(selb, upper, preferred_element_type=F32) + carry[:, 0:1]
    iota_e = lax.broadcasted_iota(I32, (n_experts, t), 0)
    ranks = [jnp.sum(jnp.where(iota_e == ids[k:k + 1, :], before, 0.0), axis=0, keepdims=True)
             for k in range(TOP_K)]
    rank_ref[...] = jnp.concatenate(ranks, axis=0).astype(I32)
    total = carry[...] + jnp.dot(selb, jnp.ones((t, LANES), BF16), preferred_element_type=F32)
    carry[...] = total
    cnt_ref[...] = total.astype(I32)


def _outproj(ylru, yhg, x, mods, gain, wo, rwt, rb, tm):
    b, s, d = x.shape
    hw = ylru.shape[2]
    e = rwt.shape[0]
    n = b * s
    nt = s // tm
    tok = lambda bi, i: (0, bi * nt + i)
    return pl.pallas_call(
        _outproj_kernel,
        out_shape=(jax.ShapeDtypeStruct((b, s, d), F32),
                   jax.ShapeDtypeStruct((n, d // 2), U32),
                   jax.ShapeDtypeStruct((TOP_K, n), I32),
                   jax.ShapeDtypeStruct((TOP_K, n), F32),
                   jax.ShapeDtypeStruct((TOP_K, n), I32),
                   jax.ShapeDtypeStruct((e, LANES), I32)),
        grid=(b, nt),
        in_specs=[pl.BlockSpec((1, tm, hw), lambda bi, i: (bi, i, 0)),
                  pl.BlockSpec((1, tm, hw), lambda bi, i: (bi, i, 0)),
                  pl.BlockSpec((1, tm, d), lambda bi, i: (bi, i, 0)),
                  pl.BlockSpec((1, 8, d), lambda bi, i: (bi, 0, 0)),
                  pl.BlockSpec((1, d), lambda bi, i: (0, 0)),
                  pl.BlockSpec((2, hw, d), lambda bi, i: (0, 0, 0)),
                  pl.BlockSpec((e, d), lambda bi, i: (0, 0)),
                  pl.BlockSpec((e, 1), lambda bi, i: (0, 0))],
        out_specs=(pl.BlockSpec((1, tm, d), lambda bi, i: (bi, i, 0)),
                   pl.BlockSpec((tm, d // 2), lambda bi, i: (bi * nt + i, 0)),
                   pl.BlockSpec((TOP_K, tm), tok),
                   pl.BlockSpec((TOP_K, tm), tok),
                   pl.BlockSpec((TOP_K, tm), tok),
                   pl.BlockSpec((e, LANES), lambda bi, i: (0, 0))),
        scratch_shapes=[pltpu.VMEM((e, LANES), F32)],
        compiler_params=_cparams(("arbitrary", "arbitrary")),
        name="outproj",
    )(ylru, yhg, x, mods, gain, wo, rwt, rb)


def _pos_kernel(pstart, eid_ref, rank_ref, pos_ref):
    eid = eid_ref[...]

    def body(e, acc):
        return jnp.where(eid == e, pstart[e], acc)

    base = lax.fori_loop(0, pstart.shape[0], body, jnp.zeros(eid.shape, I32))
    pos_ref[...] = base + rank_ref[...]


def _pos(pstarts, eid_t, rank_t, tn):
    k, n = eid_t.shape
    spec = pl.BlockSpec((k, tn), lambda i, ps: (0, i))
    return pl.pallas_call(
        _pos_kernel,
        out_shape=jax.ShapeDtypeStruct((k, n), I32),
        grid_spec=pltpu.PrefetchScalarGridSpec(
            num_scalar_prefetch=1, grid=(n // tn,), in_specs=[spec, spec], out_specs=spec),
        compiler_params=_cparams(("arbitrary",)),
        name="pos",
    )(pstarts, eid_t, rank_t)


def _pad_fill(padstart, padlen, zbuf, xs_ref, sem, wait):
    nbits = zbuf.shape[0].bit_length() - 1
    low_bits = SUBLANES.bit_length() - 1

    def go(cp):
        if wait:
            cp.wait()
        else:
            cp.start()

    def body(e, carry):
        st = padstart[e]
        ln = padlen[e]
        end = st + ln
        for bit in range(nbits - 1, low_bits - 1, -1):
            size = 1 << bit
            back = ((ln >> (bit + 1)) << (bit + 1)) + size

            @pl.when((ln & size) != 0)
            def _():
                start = pl.multiple_of(end - back, SUBLANES)
                go(pltpu.make_async_copy(zbuf.at[pl.ds(0, size)],
                                         xs_ref.at[pl.ds(start, size)], sem))
        for j in range(SUBLANES - 1):
            @pl.when(j < (ln & (SUBLANES - 1)))
            def _():
                go(pltpu.make_async_copy(zbuf.at[pl.ds(0, 1)], xs_ref.at[pl.ds(st + j, 1)], sem))
        return carry

    lax.fori_loop(0, padstart.shape[0], body, 0)


def _padfill_kernel(padstart, padlen, xs_in, xs_ref, zbuf, sem):
    del xs_in
    zbuf[...] = jnp.zeros(zbuf.shape, zbuf.dtype)
    _pad_fill(padstart, padlen, zbuf, xs_ref, sem, wait=False)
    _pad_fill(padstart, padlen, zbuf, xs_ref, sem, wait=True)


def _padfill(xs, padstart, padlen, bm):
    return pl.pallas_call(
        _padfill_kernel,
        out_shape=jax.ShapeDtypeStruct(xs.shape, xs.dtype),
        grid_spec=pltpu.PrefetchScalarGridSpec(
            num_scalar_prefetch=2,
            grid=(1,),
            in_specs=[pl.BlockSpec(memory_space=pl.ANY)],
            out_specs=pl.BlockSpec(memory_space=pl.ANY),
            scratch_shapes=[pltpu.VMEM((bm, xs.shape[1]), xs.dtype), pltpu.SemaphoreType.DMA(())]),
        input_output_aliases={2: 0},
        compiler_params=_cparams(("arbitrary",)),
        name="padfill",
    )(padstart, padlen, xs)


SC_SCATTER_WINDOW = 128


def _sc_mesh():
    return plsc.VectorSubcoreMesh(core_axis_name="core", subcore_axis_name="subcore")


def _sc_scatter_rows(rows, pos, cap):
    n, dw = rows.shape
    top_k = pos.shape[0]
    mesh = _sc_mesh()
    n_workers = mesh.num_cores * mesh.num_subcores
    win = SC_SCATTER_WINDOW
    per_w = n // n_workers
    n_chunks = per_w // win
    assert per_w * n_workers == n and n_chunks * win == per_w
    pos_w = pos.reshape(top_k, n_workers, n_chunks, win).transpose(1, 2, 0, 3)
    pos_w = pos_w.reshape(n_workers, n_chunks * top_k, win)

    @functools.partial(pl.kernel, mesh=mesh,
                       out_type=jax.ShapeDtypeStruct((cap, dw), rows.dtype),
                       scratch_types=[pltpu.VMEM((n_chunks * top_k, win), I32),
                                      pltpu.VMEM((win, dw), rows.dtype),
                                      pltpu.SemaphoreType.DMA])
    def scatter(r_hbm, p_hbm, o_hbm, idx_v, rows_v, sem):
        wid = lax.axis_index("subcore") * mesh.num_cores + lax.axis_index("core")
        pltpu.sync_copy(p_hbm.at[wid], idx_v)

        @pl.loop(0, n_chunks)
        def _(c):
            pltpu.sync_copy(r_hbm.at[pl.ds(wid * per_w + c * win, win)], rows_v)
            copies = [pltpu.async_copy(rows_v, o_hbm.at[idx_v.at[c * top_k + k]], sem)
                      for k in range(top_k)]
            for cp in copies:
                cp.wait()

    return scatter(rows, pos_w)


GMM_EXPERTS = 4
GMM_PAIR = 4
GMM_SLOTS = 16
GMM_AHEAD = GMM_SLOTS - GMM_PAIR


def _gmm_rows(xw, w13b, w2b, de):
    xb = _unpack_rows(xw).astype(BF16)
    u = jnp.dot(xb, w13b[...], preferred_element_type=F32)
    hmid = _silu(u[:, :de]) * u[:, de:]
    return _pack_rows(jnp.dot(hmid.astype(BF16), w2b[...], preferred_element_type=F32))


def _gmm_kernel(blk0, nblk, w13_ref, w2_ref, xs_ref, ys_ref, w13b, w2b, xbuf, ybuf, isem, osem):
    step = pl.program_id(0)
    n_e = nblk.shape[0]
    slots, bm = xbuf.shape[0], xbuf.shape[1]
    de = w2_ref.shape[1]
    total = blk0[n_e - 1] + nblk[n_e - 1]

    def x_copy(g):
        rows = pl.ds(pl.multiple_of(g * bm, bm), bm)
        slot = g & (slots - 1)
        return pltpu.make_async_copy(xs_ref.at[rows], xbuf.at[slot], isem.at[slot])

    def y_copy(g):
        rows = pl.ds(pl.multiple_of(g * bm, bm), bm)
        slot = g & (slots - 1)
        return pltpu.make_async_copy(ybuf.at[slot], ys_ref.at[rows], osem.at[slot])

    @pl.when(step == 0)
    def _():
        for j in range(GMM_AHEAD):
            @pl.when(j < total)
            def _():
                x_copy(j).start()

    def run_blocks(g0, count):
        gs = [g0 + j for j in range(count)]
        for g in gs:
            x_copy(g).wait()
        for g in gs:
            @pl.when(g + GMM_AHEAD < total)
            def _():
                x_copy(g + GMM_AHEAD).start()

            @pl.when(g >= slots)
            def _():
                y_copy(g - slots).wait()
        for g in gs:
            slot = g & (slots - 1)
            ybuf[slot] = _gmm_rows(xbuf[slot], w13b, w2b, de)
        for g in gs:
            y_copy(g).start()

    def expert_body(j, carry):
        e = step * GMM_EXPERTS + j
        nb = nblk[e]
        b0 = blk0[e]
        w13b[...] = w13_ref[j].astype(BF16)
        w2b[...] = w2_ref[j].astype(BF16)

        def pair_body(i, c):
            run_blocks(b0 + GMM_PAIR * i, GMM_PAIR)
            return c

        lax.fori_loop(0, nb // GMM_PAIR, pair_body, 0)
        for r in range(1, GMM_PAIR):
            @pl.when(nb % GMM_PAIR == r)
            def _():
                run_blocks(b0 + nb - r, r)
        return carry

    lax.fori_loop(0, GMM_EXPERTS, expert_body, 0)

    @pl.when(step == pl.num_programs(0) - 1)
    def _():
        for j in range(1, slots + 1):
            @pl.when(total - j >= 0)
            def _():
                y_copy(total - j).wait()


def _gmm(xs, w13, w2, blk0, nblk, bm):
    cap, dw = xs.shape
    e, d, de2 = w13.shape
    de = w2.shape[1]
    return pl.pallas_call(
        _gmm_kernel,
        out_shape=jax.ShapeDtypeStruct((cap, dw), U32),
        grid_spec=pltpu.PrefetchScalarGridSpec(
            num_scalar_prefetch=2,
            grid=(e // GMM_EXPERTS,),
            in_specs=[pl.BlockSpec((GMM_EXPERTS, d, de2), lambda i, b0, nb: (i, 0, 0)),
                      pl.BlockSpec((GMM_EXPERTS, de, d), lambda i, b0, nb: (i, 0, 0)),
                      pl.BlockSpec(memory_space=pl.ANY)],
            out_specs=pl.BlockSpec(memory_space=pl.ANY),
            scratch_shapes=[pltpu.VMEM((d, de2), BF16), pltpu.VMEM((de, d), BF16),
                            pltpu.VMEM((GMM_SLOTS, bm, dw), U32),
                            pltpu.VMEM((GMM_SLOTS, bm, dw), U32),
                            pltpu.SemaphoreType.DMA((GMM_SLOTS,)),
                            pltpu.SemaphoreType.DMA((GMM_SLOTS,))]),
        compiler_params=_cparams(("arbitrary",)),
        name="gmm",
    )(blk0, nblk, w13, w2, xs)


def _sc_gather_rows(table, idx, window):
    n_idx = idx.shape[0]
    dw = table.shape[1]
    mesh = _sc_mesh()
    n_workers = mesh.num_cores * mesh.num_subcores
    per_w = n_idx // n_workers
    n_chunks = per_w // window
    assert per_w * n_workers == n_idx and n_chunks * window == per_w and n_chunks % 2 == 0

    @functools.partial(pl.kernel, mesh=mesh,
                       out_type=jax.ShapeDtypeStruct((n_idx, dw), table.dtype),
                       scratch_types=[pltpu.VMEM((per_w,), I32),
                                      pltpu.VMEM((window, dw), table.dtype),
                                      pltpu.VMEM((window, dw), table.dtype),
                                      pltpu.SemaphoreType.DMA, pltpu.SemaphoreType.DMA,
                                      pltpu.SemaphoreType.DMA, pltpu.SemaphoreType.DMA])
    def gather(x_hbm, i_hbm, o_hbm, idx_v, rows0, rows1, gs0, gs1, os0, os1):
        wid = lax.axis_index("subcore") * mesh.num_cores + lax.axis_index("core")
        base = wid * per_w
        pltpu.sync_copy(i_hbm.at[pl.ds(base, per_w)], idx_v)
        bufs = ((rows0, gs0, os0), (rows1, gs1, os1))

        def fetch(j, slot):
            rows, gs, _ = bufs[slot]
            off = pl.multiple_of(j * window, window)
            return pltpu.async_copy(x_hbm.at[idx_v.at[pl.ds(off, window)]], rows, gs)

        def fetch_wait(slot):
            rows, gs, _ = bufs[slot]
            pltpu.make_async_copy(x_hbm.at[idx_v.at[pl.ds(0, window)]], rows, gs).wait()

        def put(j, slot):
            rows, _, osem = bufs[slot]
            off = pl.multiple_of(j * window, window)
            return pltpu.async_copy(rows, o_hbm.at[pl.ds(base + off, window)], osem)

        def put_wait(slot):
            rows, _, osem = bufs[slot]
            pltpu.make_async_copy(rows, o_hbm.at[pl.ds(base, window)], osem).wait()

        fetch(0, 0)

        @pl.loop(0, n_chunks, step=2)
        def _(j):
            @pl.when(j > 0)
            def _():
                put_wait(1)
            fetch(j + 1, 1)
            fetch_wait(0)
            put(j, 0)
            put_wait(0)

            @pl.when(j + 2 < n_chunks)
            def _():
                fetch(j + 2, 0)
            fetch_wait(1)
            put(j + 1, 1)

        put_wait(1)

    return gather(table, idx)


def _combine_kernel(gate_ref, x1_ref, h2_ref, mod_ref, gain_ref, sw13_ref, sw2_ref, yg_ref,
                    *rest):
    o_ref = rest[-1]
    ds_ = sw2_ref.shape[0]
    hb = _unpack_rows(h2_ref[...]).astype(BF16)
    u = jnp.dot(hb, sw13_ref[...], preferred_element_type=F32)
    hmid = _silu(u[:, :ds_]) * u[:, ds_:]
    y = jnp.dot(hmid.astype(BF16), sw2_ref[...], preferred_element_type=F32)

    gate = gate_ref[...]
    moe = gate[:, 0:1] * _unpack_rows(yg_ref[0])
    for k in range(1, TOP_K):
        moe = moe + gate[:, k:k + 1] * _unpack_rows(yg_ref[k])
    xo = x1_ref[...] + mod_ref[0, 5:6, :] * (moe + y)
    ms = jnp.mean(xo * xo, axis=-1, keepdims=True)
    o_ref[...] = xo * lax.rsqrt(ms + EPS) * gain_ref[...]


def _combine(gate_tok, x1, h2, mods, gain, sw13, sw2, yg, tc, tiles_per_batch, tile0, out_prev):
    n, d = x1.shape
    dw = h2.shape[1]
    nt = yg.shape[1] // tc
    ds2 = sw13.shape[1]
    ds_ = sw2.shape[0]
    tok = lambda i: (i + tile0, 0)
    in_specs = [pl.BlockSpec((tc, TOP_K), tok),
                pl.BlockSpec((tc, d), tok),
                pl.BlockSpec((tc, dw), tok),
                pl.BlockSpec((1, 8, d), lambda i: ((i + tile0) // tiles_per_batch, 0, 0)),
                pl.BlockSpec((1, d), lambda i: (0, 0)),
                pl.BlockSpec((d, ds2), lambda i: (0, 0)),
                pl.BlockSpec((ds_, d), lambda i: (0, 0)),
                pl.BlockSpec((TOP_K, tc, dw), lambda i: (0, i, 0))]
    args = [gate_tok, x1, h2, mods, gain, sw13, sw2, yg]
    aliases = {}
    if out_prev is not None:
        in_specs.append(pl.BlockSpec(memory_space=pl.ANY))
        args.append(out_prev)
        aliases = {len(args) - 1: 0}
    return pl.pallas_call(
        _combine_kernel,
        out_shape=jax.ShapeDtypeStruct((n, d), F32),
        grid=(nt,),
        in_specs=in_specs,
        out_specs=pl.BlockSpec((tc, d), tok),
        input_output_aliases=aliases,
        compiler_params=_cparams(("arbitrary",)),
        name="combine",
    )(*args)


def _prep_w_in(w, lru_w, hg_w):
    z0 = 2 * lru_w + 2 * hg_w
    col = jnp.arange(w.shape[1])
    scale = jnp.where((col >= z0) & (col < z0 + 2 * hg_w), 0.5, 1.0).astype(w.dtype)
    return (w * scale[None, :]).astype(BF16)


def _block_diag_pairs(w, per):
    nb, bd, _ = w.shape
    w = w.reshape(nb // per, per, bd, bd)
    eye = jnp.eye(per, dtype=w.dtype)
    out = jnp.einsum("gpij,pq->gpiqj", w, eye)
    return out.reshape(nb // per, per * bd, per * bd)


def kernel(x, c, ctx, c_ctx, ada_w, ada_b, norm_mix, norm_ffn, norm_final, w_in, w_out,
           lru_conv_w, lru_conv_b, lru_wa, lru_ba, lru_wx, lru_bx, lru_lambda,
           hgrn_lb_logits, hgrn_norm, router_w, router_b, exp_w13, exp_w2, shared_w13, shared_w2):
    assert ada_w.shape[0] == 1, "single-layer block"
    b, s, d = x.shape
    n = b * s
    lru_w = lru_conv_w.shape[2]
    n_experts = router_w.shape[2]

    rows = -(-(b + 1) // SUBLANES) * SUBLANES
    cs = jnp.zeros((rows, d), F32).at[:b].set(c).at[b].set(c_ctx)
    mod = _ada(cs, ada_w[0], ada_b[0][None, :]).reshape(rows, 6, d)
    mods = jnp.pad(mod, ((0, 0), (0, 2), (0, 0)))

    w_in_bf = _prep_w_in(w_in[0], lru_w, hgrn_norm.shape[1])
    gain1 = norm_mix[0][None, :]
    projx = _inproj(x, mods, gain1, w_in_bf, PROJ_TILE, shared_mod=False)
    projc = _inproj(ctx, mods[b:b + 1], gain1, w_in_bf, ctx.shape[1], shared_mod=True)

    per = LRU_G // (lru_w // LRU_BLOCKS)
    wg = jnp.stack([_block_diag_pairs(lru_wa[0, 0], per), _block_diag_pairs(lru_wx[0, 0], per),
                    _block_diag_pairs(lru_wa[0, 1], per), _block_diag_pairs(lru_wx[0, 1], per)]
                   )
    wg = (0.5 * wg).astype(BF16)
    bg = 0.5 * jnp.stack([lru_ba[0, 0], lru_bx[0, 0], lru_ba[0, 1], lru_bx[0, 1]])
    ylru = _lru(projx, projc, lru_conv_w[0], lru_conv_b[0][None, :], wg, bg, lru_lambda[0])

    yhg = _hgrn(projx, projc, hgrn_lb_logits, hgrn_norm[0][None, :], (2 * lru_w) // LANES)

    wo = w_out[0].astype(BF16).reshape(2, lru_w, d)
    x1, h2, eid_t, gate_t, rank_t, cnt = _outproj(
        ylru, yhg, x, mods, norm_ffn[0][None, :], wo,
        router_w[0].T.astype(BF16), router_b[0][:, None], PROJ_TILE)

    bm = GMM_BLOCK
    counts = cnt[:, 0]
    padded = (counts + bm - 1) // bm * bm
    pends = jnp.cumsum(padded)
    pstarts = pends - padded
    pos = _pos(pstarts, eid_t, rank_t, POS_TILE)
    cap = n * TOP_K + n_experts * bm

    xs = _padfill(_sc_scatter_rows(h2, pos, cap), pstarts + counts, padded - counts, bm)
    ys = _gmm(xs, exp_w13[0], exp_w2[0], pstarts // bm, padded // bm, bm)

    tc = COMBINE_TILE
    n_part = n // MOE_TAIL_PARTS
    gate_tok = gate_t.T
    x1f = x1.reshape(n, d)
    sw13 = shared_w13[0].astype(BF16)
    sw2 = shared_w2[0].astype(BF16)
    out = None
    for part in range(MOE_TAIL_PARTS):
        idx = pos[:, part * n_part:(part + 1) * n_part].reshape(-1)
        yg = _sc_gather_rows(ys, idx, SC_GATHER_WINDOW).reshape(TOP_K, n_part, d // 2)
        out = _combine(gate_tok, x1f, h2, mods, norm_final[None, :], sw13, sw2, yg, tc, s // tc,
                       part * (n_part // tc), out)
    return out.reshape(b, s, d)
```

```python
import functools

import jax
import jax.numpy as jnp
from jax import lax
from jax.experimental import pallas as pl
from jax.experimental.pallas import tpu as pltpu
from jax.experimental.pallas import tpu_sc as plsc

F32 = jnp.float32
BF16 = jnp.bfloat16
I32 = jnp.int32
U32 = jnp.uint32

EPS = 1e-6
LRU_C = 8.0
ROUTE_SCALE = 2.5
GRID_W = 64
HG_CHUNK = 32
N_GROUPS = 8
TOPK_GROUPS = 4
TOP_K = 8
LRU_BLOCKS = 8
HG_HEADS = 4
CONV_W = 4
CONV_PAD_L = 1

SUBLANES = 8
LANES = 128
N_SEG = SUBLANES
VMEM_LIMIT = 56 * 1024 * 1024

PROJ_TILE = 512
POS_TILE = 2048
GMM_BLOCK = 256
SC_GATHER_WINDOW = 64
COMBINE_TILE = 512


def _cparams(sem, vmem=VMEM_LIMIT):
    return pltpu.CompilerParams(dimension_semantics=sem, vmem_limit_bytes=vmem)


def _sigmoid(x):
    return jax.nn.sigmoid(x)


def _silu(x):
    return x * _sigmoid(x)


def _pack_rows(x):
    w = x.shape[1] // 2
    bits = pltpu.bitcast(x.astype(BF16).astype(F32), U32)
    return (bits[:, :w] >> 16) | (bits[:, w:] & jnp.uint32(0xFFFF0000))


def _unpack_rows(p):
    lo = pltpu.bitcast(p << 16, F32)
    hi = pltpu.bitcast(p & jnp.uint32(0xFFFF0000), F32)
    return jnp.concatenate([lo, hi], axis=1)


def _gelu_tanh(x):
    c = 0.7978845608028654
    return 0.5 * x * (1.0 + jnp.tanh(c * (x + 0.044715 * (x * x * x))))


def _ada_kernel(c_ref, w_ref, b_ref, o_ref):
    s = _silu(c_ref[...])
    o_ref[...] = jnp.dot(s, w_ref[...], preferred_element_type=F32,
                         precision=lax.Precision.HIGHEST) + b_ref[...]


def _ada(cs, w, b):
    rows, d = cs.shape
    n = w.shape[1]
    bn = 1024
    return pl.pallas_call(
        _ada_kernel,
        out_shape=jax.ShapeDtypeStruct((rows, n), F32),
        grid=(n // bn,),
        in_specs=[pl.BlockSpec((rows, d), lambda j: (0, 0)),
                  pl.BlockSpec((d, bn), lambda j: (0, j)),
                  pl.BlockSpec((1, bn), lambda j: (0, j))],
        out_specs=pl.BlockSpec((rows, bn), lambda j: (0, j)),
        compiler_params=_cparams(("arbitrary",)),
        name="ada",
    )(cs, w, b)


def _inproj_kernel(x_ref, mod_ref, gain_ref, w_ref, o_ref):
    x = x_ref[0]
    ms = jnp.mean(x * x, axis=-1, keepdims=True)
    y = x * lax.rsqrt(ms + EPS) * gain_ref[...]
    h = y * (1.0 + mod_ref[0, 1:2, :]) + mod_ref[0, 0:1, :]
    o_ref[0] = jnp.dot(h.astype(BF16), w_ref[...], preferred_element_type=F32)


def _inproj(x, mods, gain, w_bf, tm, shared_mod):
    b, s, d = x.shape
    n = w_bf.shape[1]
    mod_map = (lambda bi, i: (0, 0, 0)) if shared_mod else (lambda bi, i: (bi, 0, 0))
    return pl.pallas_call(
        _inproj_kernel,
        out_shape=jax.ShapeDtypeStruct((b, s, n), F32),
        grid=(b, s // tm),
        in_specs=[pl.BlockSpec((1, tm, d), lambda bi, i: (bi, i, 0)),
                  pl.BlockSpec((1, 8, d), mod_map),
                  pl.BlockSpec((1, d), lambda bi, i: (0, 0)),
                  pl.BlockSpec((d, n), lambda bi, i: (0, 0))],
        out_specs=pl.BlockSpec((1, tm, n), lambda bi, i: (bi, i, 0)),
        compiler_params=_cparams(("arbitrary", "arbitrary")),
        name="inproj",
    )(x, mods, gain, w_bf)


LRU_G = 256
LRU_CHUNK = 256


def _seg_rows(s, seg_len):
    return pl.ds(s, seg_len, stride=N_SEG)


def _out_pitch(seg_len):
    return seg_len + SUBLANES


def _lane_store(ref, d, rows, val):
    nl = val.shape[-1] // LANES
    for l in range(nl):
        ref[d * nl + l, rows, :] = val[:, l * LANES:(l + 1) * LANES]


def _lane_load(ref, d, rows, nl):
    return jnp.concatenate([ref[d * nl + l, rows, :] for l in range(nl)], axis=-1)


LRU_CONV_PIECE = 64


def _lru_coeffs(pad_ref, t0, rows, cw, cb, wg_ref, bg, sp, a_ref, b_ref, dst_rows, u_ref):
    for p0 in range(0, rows, LRU_CONV_PIECE):
        win = pad_ref[pl.ds(t0 + p0, LRU_CONV_PIECE + 2 * SUBLANES), :]
        u = cb
        for k in range(CONV_W):
            off = SUBLANES - CONV_PAD_L + k
            u = u + win[off:off + LRU_CONV_PIECE, :] * cw[k:k + 1, :]
        u_ref[p0:p0 + LRU_CONV_PIECE, :] = u
    u = u_ref[0:rows, :]
    ub = u.astype(BF16)
    half_u = 0.5 * u
    for d in range(2):
        tr = jnp.tanh(jnp.dot(ub, wg_ref[2 * d, 0], preferred_element_type=F32)
                      + bg[2 * d:2 * d + 1, :])
        ti = jnp.tanh(jnp.dot(ub, wg_ref[2 * d + 1, 0], preferred_element_type=F32)
                      + bg[2 * d + 1:2 * d + 2, :])
        half_c = (-0.5 * LRU_C) * sp[d:d + 1, :]
        log_a = half_c * tr + half_c
        a = jnp.exp(log_a)
        one_minus_a2 = -jnp.tanh(log_a) * (1.0 + a * a)
        root = jnp.where(one_minus_a2 > 0.0, one_minus_a2 * lax.rsqrt(one_minus_a2), 0.0)
        bb = root * (ti * half_u + half_u)
        _lane_store(a_ref, d, dst_rows, a)
        _lane_store(b_ref, d, dst_rows, bb)


def _seg_scan(a_ref, b_ref, h_ref, p_ref, seg_len, unroll):
    n_lead = a_ref.shape[0]
    nl = n_lead // 2
    zero = jnp.zeros((N_SEG, LANES), F32)
    one = jnp.ones((N_SEG, LANES), F32)
    init = tuple((zero, one) for _ in range(n_lead))

    def step(t, carry):
        out = []
        for i in range(n_lead):
            h, p = carry[i]
            tt = t if i < nl else seg_len - 1 - t
            rows = pl.ds(pl.multiple_of(tt * N_SEG, N_SEG), N_SEG)
            a = a_ref[i, rows, :]
            h = a * h + b_ref[i, rows, :]
            p = p * a
            if h_ref is not None:
                out_rows = pl.ds(tt, N_SEG, stride=_out_pitch(seg_len))
                h_ref[i, out_rows, :] = h
                p_ref[i, out_rows, :] = p
            out.append((h, p))
        return tuple(out)

    def body(i, carry):
        for j in range(unroll):
            carry = step(i * unroll + j, carry)
        return carry

    ends = lax.fori_loop(0, seg_len // unroll, body, init)
    h_end = [jnp.concatenate([ends[d * nl + l][0] for l in range(nl)], axis=-1) for d in range(2)]
    p_end = [jnp.concatenate([ends[d * nl + l][1] for l in range(nl)], axis=-1) for d in range(2)]
    return h_end, p_end


def _seg_carries(h_end, p_end, h0, reverse):
    order = range(N_SEG - 1, -1, -1) if reverse else range(N_SEG)
    cin = [None] * N_SEG
    c = h0
    for s in order:
        cin[s] = c
        c = p_end[s:s + 1, :] * c + h_end[s:s + 1, :]
    return cin, c


def _lru_kernel(rx_ref, rg_ref, rc_ref, cw_ref, cb_ref, wg_ref, bg_ref, lam_ref, o_ref,
                padl, padc, u_s, a_l, b_l, h_l, p_l, a_c, b_c):
    s_len = rx_ref.shape[1]
    c_len = rc_ref.shape[1]
    g = rx_ref.shape[2]
    nl = g // LANES
    seg_l = s_len // N_SEG
    seg_c = c_len // N_SEG

    zeros = jnp.zeros((SUBLANES, g), F32)
    padl[0:SUBLANES, :] = zeros
    padl[SUBLANES + s_len:2 * SUBLANES + s_len, :] = zeros
    padc[0:SUBLANES, :] = zeros
    padc[SUBLANES + c_len:2 * SUBLANES + c_len, :] = zeros
    for i in range(s_len // LRU_CHUNK):
        padl[SUBLANES + i * LRU_CHUNK:SUBLANES + (i + 1) * LRU_CHUNK, :] = (
            rx_ref[0, i * LRU_CHUNK:(i + 1) * LRU_CHUNK, :])
    padc[SUBLANES:SUBLANES + c_len, :] = rc_ref[0]

    cw = cw_ref[...]
    cb = cb_ref[...]
    bg = bg_ref[...]
    x = -lam_ref[...]
    sp = jnp.maximum(x, 0.0) + jnp.log(1.0 + jnp.exp(-jnp.abs(x)))

    _lru_coeffs(padc, 0, c_len, cw, cb, wg_ref, bg, sp, a_l, b_l, pl.ds(0, c_len), u_s)
    for i in range(2 * nl):
        for s in range(N_SEG):
            a_c[i, _seg_rows(s, seg_c), :] = a_l[i, s * seg_c:(s + 1) * seg_c, :]
            b_c[i, _seg_rows(s, seg_c), :] = b_l[i, s * seg_c:(s + 1) * seg_c, :]
    h_end, p_end = _seg_scan(a_c, b_c, None, None, seg_c, unroll=4)
    zero_row = jnp.zeros((1, g), F32)
    _, h0_f = _seg_carries(h_end[0], p_end[0], zero_row, reverse=False)
    _, h0_b = _seg_carries(h_end[1], p_end[1], zero_row, reverse=True)

    def coeff_body(s, carry):
        t0 = pl.multiple_of(s * seg_l, SUBLANES)
        _lru_coeffs(padl, t0, seg_l, cw, cb, wg_ref, bg, sp, a_l, b_l, _seg_rows(s, seg_l), u_s)
        return carry

    lax.fori_loop(0, N_SEG, coeff_body, 0)
    h_end, p_end = _seg_scan(a_l, b_l, h_l, p_l, seg_l, unroll=4)
    cin_f, _ = _seg_carries(h_end[0], p_end[0], h0_f, reverse=False)
    cin_b, _ = _seg_carries(h_end[1], p_end[1], h0_b, reverse=True)

    for s in range(N_SEG):
        rows = pl.ds(s * _out_pitch(seg_l), seg_l)
        h = (_lane_load(h_l, 0, rows, nl) + _lane_load(p_l, 0, rows, nl) * cin_f[s]) + (
            _lane_load(h_l, 1, rows, nl) + _lane_load(p_l, 1, rows, nl) * cin_b[s])
        o_ref[0, s * seg_l:(s + 1) * seg_l, :] = _gelu_tanh(rg_ref[0, s * seg_l:(s + 1) * seg_l, :]) * h


def _lru(projx, projc, cw, cb, wg, bg, lam):
    b, s_len, _ = projx.shape
    c_len = projc.shape[1]
    w = cw.shape[1]
    g = LRU_G
    ng = w // g
    seg_l = s_len // N_SEG
    seg_c = c_len // N_SEG
    assert seg_l == LRU_CHUNK and c_len % (N_SEG * 1) == 0
    return pl.pallas_call(
        _lru_kernel,
        out_shape=jax.ShapeDtypeStruct((b, s_len, w), F32),
        grid=(b, ng),
        in_specs=[pl.BlockSpec((1, s_len, g), lambda bi, j: (bi, 0, j)),
                  pl.BlockSpec((1, s_len, g), lambda bi, j: (bi, 0, ng + j)),
                  pl.BlockSpec((1, c_len, g), lambda bi, j: (bi, 0, j)),
                  pl.BlockSpec((CONV_W, g), lambda bi, j: (0, j)),
                  pl.BlockSpec((1, g), lambda bi, j: (0, j)),
                  pl.BlockSpec((4, 1, g, g), lambda bi, j: (0, j, 0, 0)),
                  pl.BlockSpec((4, g), lambda bi, j: (0, j)),
                  pl.BlockSpec((2, g), lambda bi, j: (0, j))],
        out_specs=pl.BlockSpec((1, s_len, g), lambda bi, j: (bi, 0, j)),
        scratch_shapes=[pltpu.VMEM((s_len + 2 * SUBLANES, g), F32),
                        pltpu.VMEM((c_len + 2 * SUBLANES, g), F32),
                        pltpu.VMEM((max(seg_l, c_len), g), F32),
                        pltpu.VMEM((2 * g // LANES, s_len, LANES), F32),
                        pltpu.VMEM((2 * g // LANES, s_len, LANES), F32),
                        pltpu.VMEM((2 * g // LANES, N_SEG * _out_pitch(seg_l), LANES), F32),
                        pltpu.VMEM((2 * g // LANES, N_SEG * _out_pitch(seg_l), LANES), F32),
                        pltpu.VMEM((2 * g // LANES, c_len, LANES), F32),
                        pltpu.VMEM((2 * g // LANES, c_len, LANES), F32)],
        compiler_params=_cparams(("arbitrary", "arbitrary")),
        name="lru",
    )(projx, projx, projc, cw, cb, wg, bg, lam)


LOG2_E = 1.4426950408889634
HG_PITCH_L = GRID_W + SUBLANES
HG_PITCH_C = SUBLANES


def _hg_gates(half_z, lb):
    c = 0.5 * (1.0 - lb)
    ct = c * jnp.tanh(half_z)
    logf = jnp.log(0.5 * (1.0 + lb) + ct)
    k = c - ct
    return logf, k


def _hg_prepass(load, width, pitch, lb, qd, kd, kl, vs, dec):
    n_pos = HG_CHUNK
    for d, zname in enumerate(("zf", "zb")):
        order = range(n_pos) if d == 0 else range(n_pos - 1, -1, -1)
        g = jnp.zeros((width, LANES), F32)
        lbd = lb[d:d + 1, :]
        for p in order:
            logf, k = _hg_gates(load(zname, p), lbd)
            g = g + logf
            qd[d, p * pitch:p * pitch + width, :] = g
            kl[d, p * pitch:p * pitch + width, :] = k
        g_last = g * LOG2_E
        dec[d, 0:width, :] = jnp.exp2(g_last)
        for p in range(n_pos):
            rows = slice(p * pitch, p * pitch + width)
            gp = qd[d, rows, :] * LOG2_E
            k = kl[d, rows, :]
            qd[d, rows, :] = load("q", p) * jnp.exp2(gp)
            kd[d, rows, :] = k * jnp.exp2(-gp)
            kl[d, rows, :] = k * jnp.exp2(g_last - gp)
    for p in range(n_pos):
        vs[p * pitch:p * pitch + width, :] = load("v", p)


HG_GROUP = 4
HG_UNROLL = 8
HG_INTRA_UNROLL = 4
_NT = (((1,), (1,)), ((), ()))
_TN = (((0,), (0,)), ((), ()))


def _hg_group_rows(ref, lead, c0, pitch):
    parts = []
    for j in range(HG_GROUP):
        rows = pl.ds(c0 + j, HG_CHUNK, stride=pitch)
        parts.append(ref[rows, :] if lead is None else ref[lead, rows, :])
    return jnp.concatenate(parts, axis=0)


def _hg_scores(g, pitch, qd, kd, masks, sc_ref):
    c0 = g * HG_GROUP
    total = None
    for d in range(2):
        q = _hg_group_rows(qd, d, c0, pitch).astype(BF16)
        k = _hg_group_rows(kd, d, c0, pitch).astype(BF16)
        sc = jnp.where(masks[d], lax.dot_general(q, k, _NT, preferred_element_type=F32), 0.0)
        total = sc if total is None else total + sc
    sc_ref[g] = total.astype(BF16)


def _hg_summaries(g, pitch, kl, vs, sc_ref, oi, kv):
    c0 = g * HG_GROUP
    v = _hg_group_rows(vs, None, c0, pitch).astype(BF16)
    if oi is not None:
        o = jnp.dot(sc_ref[g], v, preferred_element_type=F32)
        for j in range(HG_GROUP):
            oi[pl.ds(c0 + j, HG_CHUNK, stride=pitch), :] = o[j * HG_CHUNK:(j + 1) * HG_CHUNK, :]
    kls = jnp.concatenate([_hg_group_rows(kl, 0, c0, pitch), _hg_group_rows(kl, 1, c0, pitch)],
                          axis=1).astype(BF16)
    for j in range(HG_GROUP):
        sl = slice(j * HG_CHUNK, (j + 1) * HG_CHUNK)
        both = lax.dot_general(v[sl], kls[sl], _TN, preferred_element_type=F32)
        kv[0, c0 + j] = both[:, :LANES]
        kv[1, c0 + j] = both[:, LANES:]


def _hg_steps(step0, n_steps, n_chunks, pitch, qd, dec, kv, states, ox):
    states = list(states)
    for u in range(n_steps):
        cs = (step0 + u, n_chunks - 1 - (step0 + u))
        if ox is not None:
            rows = [pl.ds(c, HG_CHUNK, stride=pitch) for c in cs]
            q2 = jnp.concatenate([qd[0, rows[0], :], qd[1, rows[1], :]], axis=0).astype(BF16)
            s2 = jnp.concatenate(states, axis=0).astype(BF16)
            o2 = lax.dot_general(q2, s2, _NT, preferred_element_type=F32)
            ox[0, rows[0], :] = o2[:HG_CHUNK, :LANES]
            ox[1, rows[1], :] = o2[HG_CHUNK:, LANES:]
        for d in range(2):
            states[d] = states[d] * dec[d, pl.ds(cs[d], 1), :] + kv[d, cs[d]]
    return tuple(states)


def _hgrn_kernel(q_ref, v_ref, zf_ref, zb_ref, g_ref, qc_ref, vc_ref, zfc_ref, zbc_ref,
                 lbl_ref, gain_ref, o_ref, qd, kd, kl, vs, dec, oi, ox, kv, sc):
    s_len = q_ref.shape[1]
    c_len = qc_ref.shape[1]
    n_col = s_len // HG_CHUNK
    n_cc = c_len // HG_CHUNK
    assert n_col == GRID_W and n_cc == HG_PITCH_C
    assert n_col % HG_GROUP == 0 and n_cc % HG_GROUP == 0 and n_col % HG_UNROLL == 0

    lg = lbl_ref[...]
    m = jnp.max(lg, axis=1, keepdims=True)
    ex = jnp.exp(lg - m)
    lb = ex[:, 0, :] / jnp.sum(ex, axis=1)

    gr = HG_GROUP * HG_CHUNK
    ii = lax.broadcasted_iota(I32, (gr, gr), 0)
    jj = lax.broadcasted_iota(I32, (gr, gr), 1)
    same = (ii // HG_CHUNK) == (jj // HG_CHUNK)
    masks = (jnp.logical_and(same, jj <= ii), jnp.logical_and(same, jj >= ii))

    c_refs = {"q": qc_ref, "v": vc_ref, "zf": zfc_ref, "zb": zbc_ref}

    def load_c(name, p):
        return c_refs[name][0, pl.ds(p, n_cc, stride=HG_CHUNK), :]

    _hg_prepass(load_c, n_cc, HG_PITCH_C, lb, qd, kd, kl, vs, dec)
    for g in range(n_cc // HG_GROUP):
        _hg_summaries(g, HG_PITCH_C, kl, vs, None, None, kv)
    zero = jnp.zeros((LANES, LANES), F32)
    states = _hg_steps(0, n_cc, n_cc, HG_PITCH_C, qd, dec, kv, (zero, zero), None)

    l_refs = {"q": q_ref, "v": v_ref, "zf": zf_ref, "zb": zb_ref}

    def load_l(name, p):
        return l_refs[name][0, p * n_col:(p + 1) * n_col, :]

    _hg_prepass(load_l, n_col, HG_PITCH_L, lb, qd, kd, kl, vs, dec)

    n_it = n_col // (HG_GROUP * HG_INTRA_UNROLL)

    def scores_body(i, carry):
        for u in range(HG_INTRA_UNROLL):
            _hg_scores(i * HG_INTRA_UNROLL + u, HG_PITCH_L, qd, kd, masks, sc)
        return carry

    lax.fori_loop(0, n_it, scores_body, 0)

    def summaries_body(i, carry):
        for u in range(HG_INTRA_UNROLL):
            _hg_summaries(i * HG_INTRA_UNROLL + u, HG_PITCH_L, kl, vs, sc, oi, kv)
        return carry

    lax.fori_loop(0, n_it, summaries_body, 0)

    def step_body(i, carry):
        return _hg_steps(i * HG_UNROLL, HG_UNROLL, n_col, HG_PITCH_L, qd, dec, kv, carry, ox)

    lax.fori_loop(0, n_col // HG_UNROLL, step_body, states)

    gain = gain_ref[...]
    for r in range(HG_CHUNK):
        rows = slice(r * HG_PITCH_L, r * HG_PITCH_L + n_col)
        o = oi[rows, :] + (ox[0, rows, :] + ox[1, rows, :])
        y = o * lax.rsqrt(jnp.mean(o * o, axis=-1, keepdims=True) + EPS) * gain
        o_ref[0, r * n_col:(r + 1) * n_col, :] = y * _silu(g_ref[0, r * n_col:(r + 1) * n_col, :])


def _hgrn(projx, projc, lbl, gain, col0):
    b, s_len, _ = projx.shape
    c_len = projc.shape[1]
    hw = gain.shape[1]
    nh = hw // LANES
    n_slots = lbl.shape[1]

    def xs(k):
        return pl.BlockSpec((1, s_len, LANES), lambda bi, h, k=k: (bi, 0, col0 + k * nh + h))

    def cs(k):
        return pl.BlockSpec((1, c_len, LANES), lambda bi, h, k=k: (bi, 0, col0 + k * nh + h))

    sc_rows = HG_CHUNK * HG_PITCH_L
    return pl.pallas_call(
        _hgrn_kernel,
        out_shape=jax.ShapeDtypeStruct((b, s_len, hw), F32),
        grid=(b, nh),
        in_specs=[xs(0), xs(1), xs(2), xs(3), xs(4), cs(0), cs(1), cs(2), cs(3),
                  pl.BlockSpec((2, n_slots, LANES), lambda bi, h: (0, 0, h)),
                  pl.BlockSpec((1, LANES), lambda bi, h: (0, h))],
        out_specs=pl.BlockSpec((1, s_len, LANES), lambda bi, h: (bi, 0, h)),
        scratch_shapes=[pltpu.VMEM((2, sc_rows, LANES), F32),
                        pltpu.VMEM((2, sc_rows, LANES), F32),
                        pltpu.VMEM((2, sc_rows, LANES), F32),
                        pltpu.VMEM((sc_rows, LANES), F32),
                        pltpu.VMEM((2, HG_PITCH_L, LANES), F32),
                        pltpu.VMEM((sc_rows, LANES), F32),
                        pltpu.VMEM((2, sc_rows, LANES), F32),
                        pltpu.VMEM((2, GRID_W, LANES, LANES), F32),
                        pltpu.VMEM((GRID_W // HG_GROUP, HG_GROUP * HG_CHUNK,
                                    HG_GROUP * HG_CHUNK), BF16)],
        compiler_params=_cparams(("arbitrary", "arbitrary")),
        name="hgrn",
    )(projx, projx, projx, projx, projx, projc, projc, projc, projc, lbl, gain)


def _first_index_of_max(vals, iota, big):
    m = jnp.max(vals, axis=0, keepdims=True)
    idx = jnp.min(jnp.where(vals == m, iota, big), axis=0, keepdims=True)
    return m, idx


def _route(logits_t, rb, n_experts):
    t = logits_t.shape[1]
    gsz = n_experts // N_GROUPS
    neg = -jnp.inf
    scores = _sigmoid(logits_t)
    biased = scores + rb
    iota_g = lax.broadcasted_iota(I32, (gsz, t), 0)
    gscore = []
    for gi in range(N_GROUPS):
        blk = biased[gi * gsz:(gi + 1) * gsz, :]
        m1, i1 = _first_index_of_max(blk, iota_g, gsz)
        m2 = jnp.max(jnp.where(iota_g == i1, neg, blk), axis=0, keepdims=True)
        gscore.append(m1 + m2)
    gs = jnp.concatenate(gscore, axis=0)
    iota_n = lax.broadcasted_iota(I32, (N_GROUPS, t), 0)
    gsel = jnp.zeros((N_GROUPS, t), jnp.bool_)
    for _ in range(TOPK_GROUPS):
        _, gi1 = _first_index_of_max(gs, iota_n, N_GROUPS)
        hit = iota_n == gi1
        gsel = jnp.logical_or(gsel, hit)
        gs = jnp.where(hit, neg, gs)
    emask = jnp.concatenate(
        [jnp.broadcast_to(gsel[gi:gi + 1, :], (gsz, t)) for gi in range(N_GROUPS)], axis=0)
    masked = jnp.where(emask, biased, neg)
    iota_e = lax.broadcasted_iota(I32, (n_experts, t), 0)
    ids, gates = [], []
    sel = jnp.zeros((n_experts, t), F32)
    for _ in range(TOP_K):
        _, ei = _first_index_of_max(masked, iota_e, n_experts)
        hit = iota_e == ei
        ids.append(ei)
        gates.append(jnp.sum(jnp.where(hit, scores, 0.0), axis=0, keepdims=True))
        sel = jnp.where(hit, 1.0, sel)
        masked = jnp.where(hit, neg, masked)
    ids = jnp.concatenate(ids, axis=0)
    gates = jnp.concatenate(gates, axis=0)
    gates = gates / jnp.sum(gates, axis=0, keepdims=True) * ROUTE_SCALE
    return ids, gates, sel


def _outproj_kernel(yl_ref, yh_ref, x_ref, mod_ref, gain_ref, wo_ref, rwt_ref, rb_ref,
                    x1_ref, h2_ref, eid_ref, gate_ref, rank_ref, cnt_ref, carry):
    first = jnp.logical_and(pl.program_id(0) == 0, pl.program_id(1) == 0)

    @pl.when(first)
    def _():
        carry[...] = jnp.zeros(carry.shape, F32)

    y = jnp.dot(yl_ref[0].astype(BF16), wo_ref[0], preferred_element_type=F32)
    y = y + jnp.dot(yh_ref[0].astype(BF16), wo_ref[1], preferred_element_type=F32)
    x1 = x_ref[0] + mod_ref[0, 2:3, :] * y
    x1_ref[0] = x1
    ms = jnp.mean(x1 * x1, axis=-1, keepdims=True)
    h2 = x1 * lax.rsqrt(ms + EPS) * gain_ref[...]
    h2 = h2 * (1.0 + mod_ref[0, 4:5, :]) + mod_ref[0, 3:4, :]
    h2_ref[...] = _pack_rows(h2)

    n_experts = rwt_ref.shape[0]
    t = h2.shape[0]
    logits_t = lax.dot_general(rwt_ref[...], h2.astype(BF16), _NT,
                               preferred_element_type=F32)
    ids, gates, sel = _route(logits_t, rb_ref[...], n_experts)
    eid_ref[...] = ids
    gate_ref[...] = gates

    ti = lax.broadcasted_iota(I32, (t, t), 0)
    tj = lax.broadcasted_iota(I32, (t, t), 1)
    upper = (ti < tj).astype(BF16)
    selb = sel.astype(BF16)
    before = jnp.dot(selb, upper, preferred_element_type=F32) + carry[:, 0:1]
    iota_e = lax.broadcasted_iota(I32, (n_experts, t), 0)
    ranks = [jnp.sum(jnp.where(iota_e == ids[k:k + 1, :], before, 0.0), axis=0, keepdims=True)
             for k in range(TOP_K)]
    rank_ref[...] = jnp.concatenate(ranks, axis=0).astype(I32)
    total = carry[...] + jnp.dot(selb, jnp.ones((t, LANES), BF16), preferred_element_type=F32)
    carry[...] = total
    cnt_ref[...] = total.astype(I32)


def _outproj(ylru, yhg, x, mods, gain, wo, rwt, rb, tm):
    b, s, d = x.shape
    hw = ylru.shape[2]
    e = rwt.shape[0]
    n = b * s
    nt = s // tm
    tok = lambda bi, i: (0, bi * nt + i)
    return pl.pallas_call(
        _outproj_kernel,
        out_shape=(jax.ShapeDtypeStruct((b, s, d), F32),
                   jax.ShapeDtypeStruct((n, d // 2), U32),
                   jax.ShapeDtypeStruct((TOP_K, n), I32),
                   jax.ShapeDtypeStruct((TOP_K, n), F32),
                   jax.ShapeDtypeStruct((TOP_K, n), I32),
                   jax.ShapeDtypeStruct((e, LANES), I32)),
        grid=(b, nt),
        in_specs=[pl.BlockSpec((1, tm, hw), lambda bi, i: (bi, i, 0)),
                  pl.BlockSpec((1, tm, hw), lambda bi, i: (bi, i, 0)),
                  pl.BlockSpec((1, tm, d), lambda bi, i: (bi, i, 0)),
                  pl.BlockSpec((1, 8, d), lambda bi, i: (bi, 0, 0)),
                  pl.BlockSpec((1, d), lambda bi, i: (0, 0)),
                  pl.BlockSpec((2, hw, d), lambda bi, i: (0, 0, 0)),
                  pl.BlockSpec((e, d), lambda bi, i: (0, 0)),
                  pl.BlockSpec((e, 1), lambda bi, i: (0, 0))],
        out_specs=(pl.BlockSpec((1, tm, d), lambda bi, i: (bi, i, 0)),
                   pl.BlockSpec((tm, d // 2), lambda bi, i: (bi * nt + i, 0)),
                   pl.BlockSpec((TOP_K, tm), tok),
                   pl.BlockSpec((TOP_K, tm), tok),
                   pl.BlockSpec((TOP_K, tm), tok),
                   pl.BlockSpec((e, LANES), lambda bi, i: (0, 0))),
        scratch_shapes=[pltpu.VMEM((e, LANES), F32)],
        compiler_params=_cparams(("arbitrary", "arbitrary")),
        name="outproj",
    )(ylru, yhg, x, mods, gain, wo, rwt, rb)


def _pos_kernel(pstart, eid_ref, rank_ref, pos_ref):
    eid = eid_ref[...]

    def body(e, acc):
        return jnp.where(eid == e, pstart[e], acc)

    base = lax.fori_loop(0, pstart.shape[0], body, jnp.zeros(eid.shape, I32))
    pos_ref[...] = base + rank_ref[...]


def _pos(pstarts, eid_t, rank_t, tn):
    k, n = eid_t.shape
    spec = pl.BlockSpec((k, tn), lambda i, ps: (0, i))
    return pl.pallas_call(
        _pos_kernel,
        out_shape=jax.ShapeDtypeStruct((k, n), I32),
        grid_spec=pltpu.PrefetchScalarGridSpec(
            num_scalar_prefetch=1, grid=(n // tn,), in_specs=[spec, spec], out_specs=spec),
        compiler_params=_cparams(("arbitrary",)),
        name="pos",
    )(pstarts, eid_t, rank_t)


def _pad_fill(padstart, padlen, zbuf, xs_ref, sem, wait):
    nbits = zbuf.shape[0].bit_length() - 1
    low_bits = SUBLANES.bit_length() - 1

    def go(cp):
        if wait:
            cp.wait()
        else:
            cp.start()

    def body(e, carry):
        st = padstart[e]
        ln = padlen[e]
        end = st + ln
        for bit in range(nbits - 1, low_bits - 1, -1):
            size = 1 << bit
            back = ((ln >> (bit + 1)) << (bit + 1)) + size

            @pl.when((ln & size) != 0)
            def _():
                start = pl.multiple_of(end - back, SUBLANES)
                go(pltpu.make_async_copy(zbuf.at[pl.ds(0, size)],
                                         xs_ref.at[pl.ds(start, size)], sem))
        for j in range(SUBLANES - 1):
            @pl.when(j < (ln & (SUBLANES - 1)))
            def _():
                go(pltpu.make_async_copy(zbuf.at[pl.ds(0, 1)], xs_ref.at[pl.ds(st + j, 1)], sem))
        return carry

    lax.fori_loop(0, padstart.shape[0], body, 0)


def _padfill_kernel(padstart, padlen, xs_in, xs_ref, zbuf, sem):
    del xs_in
    zbuf[...] = jnp.zeros(zbuf.shape, zbuf.dtype)
    _pad_fill(padstart, padlen, zbuf, xs_ref, sem, wait=False)
    _pad_fill(padstart, padlen, zbuf, xs_ref, sem, wait=True)


def _padfill(xs, padstart, padlen, bm):
    return pl.pallas_call(
        _padfill_kernel,
        out_shape=jax.ShapeDtypeStruct(xs.shape, xs.dtype),
        grid_spec=pltpu.PrefetchScalarGridSpec(
            num_scalar_prefetch=2,
            grid=(1,),
            in_specs=[pl.BlockSpec(memory_space=pl.ANY)],
            out_specs=pl.BlockSpec(memory_space=pl.ANY),
            scratch_shapes=[pltpu.VMEM((bm, xs.shape[1]), xs.dtype), pltpu.SemaphoreType.DMA(())]),
        input_output_aliases={2: 0},
        compiler_params=_cparams(("arbitrary",)),
        name="padfill",
    )(padstart, padlen, xs)


SC_SCATTER_WINDOW = 128


def _sc_mesh():
    return plsc.VectorSubcoreMesh(core_axis_name="core", subcore_axis_name="subcore")


def _sc_scatter_rows(rows, pos, cap):
    n, dw = rows.shape
    top_k = pos.shape[0]
    mesh = _sc_mesh()
    n_workers = mesh.num_cores * mesh.num_subcores
    win = SC_SCATTER_WINDOW
    per_w = n // n_workers
    n_chunks = per_w // win
    assert per_w * n_workers == n and n_chunks * win == per_w
    pos_w = pos.reshape(top_k, n_workers, n_chunks, win).transpose(1, 2, 0, 3)
    pos_w = pos_w.reshape(n_workers, n_chunks * top_k, win)

    @functools.partial(pl.kernel, mesh=mesh,
                       out_type=jax.ShapeDtypeStruct((cap, dw), rows.dtype),
                       scratch_types=[pltpu.VMEM((n_chunks * top_k, win), I32),
                                      pltpu.VMEM((win, dw), rows.dtype),
                                      pltpu.SemaphoreType.DMA])
    def scatter(r_hbm, p_hbm, o_hbm, idx_v, rows_v, sem):
        wid = lax.axis_index("subcore") * mesh.num_cores + lax.axis_index("core")
        pltpu.sync_copy(p_hbm.at[wid], idx_v)

        @pl.loop(0, n_chunks)
        def _(c):
            pltpu.sync_copy(r_hbm.at[pl.ds(wid * per_w + c * win, win)], rows_v)
            copies = [pltpu.async_copy(rows_v, o_hbm.at[idx_v.at[c * top_k + k]], sem)
                      for k in range(top_k)]
            for cp in copies:
                cp.wait()

    return scatter(rows, pos_w)


GMM_EXPERTS = 4
GMM_PAIR = 4
GMM_SLOTS = 16
GMM_AHEAD = GMM_SLOTS - GMM_PAIR


def _gmm_rows(xw, w13b, w2b, de):
    xb = _unpack_rows(xw).astype(BF16)
    u = jnp.dot(xb, w13b[...], preferred_element_type=F32)
    hmid = _silu(u[:, :de]) * u[:, de:]
    return _pack_rows(jnp.dot(hmid.astype(BF16), w2b[...], preferred_element_type=F32))


def _gmm_kernel(blk0, nblk, w13_ref, w2_ref, xs_ref, ys_ref, w13b, w2b, xbuf, ybuf, isem, osem):
    step = pl.program_id(0)
    n_e = nblk.shape[0]
    slots, bm = xbuf.shape[0], xbuf.shape[1]
    de = w2b.shape[0]
    total = blk0[n_e - 1] + nblk[n_e - 1]

    def x_copy(g):
        rows = pl.ds(pl.multiple_of(g * bm, bm), bm)
        slot = g & (slots - 1)
        return pltpu.make_async_copy(xs_ref.at[rows], xbuf.at[slot], isem.at[slot])

    def y_copy(g):
        rows = pl.ds(pl.multiple_of(g * bm, bm), bm)
        slot = g & (slots - 1)
        return pltpu.make_async_copy(ybuf.at[slot], ys_ref.at[rows], osem.at[slot])

    @pl.when(step == 0)
    def _():
        for j in range(GMM_AHEAD):
            @pl.when(j < total)
            def _():
                x_copy(j).start()

    def run_blocks(g0, count):
        gs = [g0 + j for j in range(count)]
        for g in gs:
            x_copy(g).wait()
        for g in gs:
            @pl.when(g + GMM_AHEAD < total)
            def _():
                x_copy(g + GMM_AHEAD).start()

            @pl.when(g >= slots)
            def _():
                y_copy(g - slots).wait()
        for g in gs:
            slot = g & (slots - 1)
            ybuf[slot] = _gmm_rows(xbuf[slot], w13b, w2b, de)
        for g in gs:
            y_copy(g).start()

    def expert_body(j, carry):
        e = step * GMM_EXPERTS + j
        nb = nblk[e]
        b0 = blk0[e]
        w13b[...] = pltpu.bitcast(w13_ref[j], BF16)
        w2b[...] = pltpu.bitcast(w2_ref[j], BF16)

        def pair_body(i, c):
            run_blocks(b0 + GMM_PAIR * i, GMM_PAIR)
            return c

        lax.fori_loop(0, nb // GMM_PAIR, pair_body, 0)
        for r in range(1, GMM_PAIR):
            @pl.when(nb % GMM_PAIR == r)
            def _():
                run_blocks(b0 + nb - r, r)
        return carry

    lax.fori_loop(0, GMM_EXPERTS, expert_body, 0)

    @pl.when(step == pl.num_programs(0) - 1)
    def _():
        for j in range(1, slots + 1):
            @pl.when(total - j >= 0)
            def _():
                y_copy(total - j).wait()


def _gmm(xs, w13p, w2p, blk0, nblk, bm):
    cap, dw = xs.shape
    e, dh, de2 = w13p.shape
    d, de = 2 * dh, 2 * w2p.shape[1]
    return pl.pallas_call(
        _gmm_kernel,
        out_shape=jax.ShapeDtypeStruct((cap, dw), U32),
        grid_spec=pltpu.PrefetchScalarGridSpec(
            num_scalar_prefetch=2,
            grid=(e // GMM_EXPERTS,),
            in_specs=[pl.BlockSpec((GMM_EXPERTS, dh, de2), lambda i, b0, nb: (i, 0, 0)),
                      pl.BlockSpec((GMM_EXPERTS, de // 2, d), lambda i, b0, nb: (i, 0, 0)),
                      pl.BlockSpec(memory_space=pl.ANY)],
            out_specs=pl.BlockSpec(memory_space=pl.ANY),
            scratch_shapes=[pltpu.VMEM((d, de2), BF16), pltpu.VMEM((de, d), BF16),
                            pltpu.VMEM((GMM_SLOTS, bm, dw), U32),
                            pltpu.VMEM((GMM_SLOTS, bm, dw), U32),
                            pltpu.SemaphoreType.DMA((GMM_SLOTS,)),
                            pltpu.SemaphoreType.DMA((GMM_SLOTS,))]),
        compiler_params=_cparams(("arbitrary",)),
        name="gmm",
    )(blk0, nblk, w13p, w2p, xs)


def _sc_pack_bf16_rows(w, rb):
    r, c = w.shape
    mesh = _sc_mesh()
    n_workers = mesh.num_cores * mesh.num_subcores
    per_w = r // n_workers
    n_chunks = per_w // rb
    lanes = 16
    assert per_w * n_workers == r and n_chunks * rb == per_w and rb % 16 == 0 and c % lanes == 0

    def to_bf16_bits(bits):
        return (bits + jnp.uint32(0x7FFF) + ((bits >> 16) & jnp.uint32(1))) >> 16

    @functools.partial(pl.kernel, mesh=mesh,
                       out_type=jax.ShapeDtypeStruct((r // 2, c), U32),
                       scratch_types=[pltpu.VMEM((rb, c), U32), pltpu.VMEM((rb // 2, c), U32)])
    def pack(w_hbm, o_hbm, in_v, out_v):
        wid = lax.axis_index("subcore") * mesh.num_cores + lax.axis_index("core")

        @pl.loop(0, n_chunks)
        def _(ch):
            row0 = pl.multiple_of(wid * per_w + ch * rb, 16)
            pltpu.sync_copy(w_hbm.at[pl.ds(row0, rb)], in_v)

            @pl.loop(0, rb // 2)
            def _(i):
                for c0 in range(0, c, lanes):
                    lo = to_bf16_bits(in_v[2 * i, pl.ds(c0, lanes)])
                    hi = to_bf16_bits(in_v[2 * i + 1, pl.ds(c0, lanes)])
                    out_v[i, pl.ds(c0, lanes)] = lo | (hi << 16)

            pltpu.sync_copy(out_v, o_hbm.at[pl.ds(pl.multiple_of(row0 // 2, 8), rb // 2)])

    return pack(lax.bitcast_convert_type(w, U32))


def _sc_gather_rows(table, idx, window):
    n_idx = idx.shape[0]
    dw = table.shape[1]
    mesh = _sc_mesh()
    n_workers = mesh.num_cores * mesh.num_subcores
    per_w = n_idx // n_workers
    n_chunks = per_w // window
    assert per_w * n_workers == n_idx and n_chunks * window == per_w and n_chunks % 2 == 0

    @functools.partial(pl.kernel, mesh=mesh,
                       out_type=jax.ShapeDtypeStruct((n_idx, dw), table.dtype),
                       scratch_types=[pltpu.VMEM((per_w,), I32),
                                      pltpu.VMEM((window, dw), table.dtype),
                                      pltpu.VMEM((window, dw), table.dtype),
                                      pltpu.SemaphoreType.DMA, pltpu.SemaphoreType.DMA,
                                      pltpu.SemaphoreType.DMA, pltpu.SemaphoreType.DMA])
    def gather(x_hbm, i_hbm, o_hbm, idx_v, rows0, rows1, gs0, gs1, os0, os1):
        wid = lax.axis_index("subcore") * mesh.num_cores + lax.axis_index("core")
        base = wid * per_w
        pltpu.sync_copy(i_hbm.at[pl.ds(base, per_w)], idx_v)
        bufs = ((rows0, gs0, os0), (rows1, gs1, os1))

        def fetch(j, slot):
            rows, gs, _ = bufs[slot]
            off = pl.multiple_of(j * window, window)
            return pltpu.async_copy(x_hbm.at[idx_v.at[pl.ds(off, window)]], rows, gs)

        def fetch_wait(slot):
            rows, gs, _ = bufs[slot]
            pltpu.make_async_copy(x_hbm.at[idx_v.at[pl.ds(0, window)]], rows, gs).wait()

        def put(j, slot):
            rows, _, osem = bufs[slot]
            off = pl.multiple_of(j * window, window)
            return pltpu.async_copy(rows, o_hbm.at[pl.ds(base + off, window)], osem)

        def put_wait(slot):
            rows, _, osem = bufs[slot]
            pltpu.make_async_copy(rows, o_hbm.at[pl.ds(base, window)], osem).wait()

        fetch(0, 0)

        @pl.loop(0, n_chunks, step=2)
        def _(j):
            @pl.when(j > 0)
            def _():
                put_wait(1)
            fetch(j + 1, 1)
            fetch_wait(0)
            put(j, 0)
            put_wait(0)

            @pl.when(j + 2 < n_chunks)
            def _():
                fetch(j + 2, 0)
            fetch_wait(1)
            put(j + 1, 1)

        put_wait(1)

    return gather(table, idx)


def _combine_kernel(gate_ref, x1_ref, h2_ref, mod_ref, gain_ref, sw13_ref, sw2_ref, yg_ref, o_ref):
    ds_ = sw2_ref.shape[0]
    hb = _unpack_rows(h2_ref[...]).astype(BF16)
    u = jnp.dot(hb, sw13_ref[...], preferred_element_type=F32)
    hmid = _silu(u[:, :ds_]) * u[:, ds_:]
    y = jnp.dot(hmid.astype(BF16), sw2_ref[...], preferred_element_type=F32)

    gate = gate_ref[...]
    moe = gate[:, 0:1] * _unpack_rows(yg_ref[0])
    for k in range(1, TOP_K):
        moe = moe + gate[:, k:k + 1] * _unpack_rows(yg_ref[k])
    xo = x1_ref[...] + mod_ref[0, 5:6, :] * (moe + y)
    ms = jnp.mean(xo * xo, axis=-1, keepdims=True)
    o_ref[...] = xo * lax.rsqrt(ms + EPS) * gain_ref[...]


def _combine(gate_tok, x1, h2, mods, gain, sw13, sw2, yg, tc, tiles_per_batch):
    n, d = x1.shape
    dw = h2.shape[1]
    nt = n // tc
    ds2 = sw13.shape[1]
    ds_ = sw2.shape[0]
    return pl.pallas_call(
        _combine_kernel,
        out_shape=jax.ShapeDtypeStruct((n, d), F32),
        grid=(nt,),
        in_specs=[pl.BlockSpec((tc, TOP_K), lambda i: (i, 0)),
                  pl.BlockSpec((tc, d), lambda i: (i, 0)),
                  pl.BlockSpec((tc, dw), lambda i: (i, 0)),
                  pl.BlockSpec((1, 8, d), lambda i: (i // tiles_per_batch, 0, 0)),
                  pl.BlockSpec((1, d), lambda i: (0, 0)),
                  pl.BlockSpec((d, ds2), lambda i: (0, 0)),
                  pl.BlockSpec((ds_, d), lambda i: (0, 0)),
                  pl.BlockSpec((TOP_K, tc, dw), lambda i: (0, i, 0))],
        out_specs=pl.BlockSpec((tc, d), lambda i: (i, 0)),
        compiler_params=_cparams(("arbitrary",)),
        name="combine",
    )(gate_tok, x1, h2, mods, gain, sw13, sw2, yg)


def _prep_w_in(w, lru_w, hg_w):
    z0 = 2 * lru_w + 2 * hg_w
    col = jnp.arange(w.shape[1])
    scale = jnp.where((col >= z0) & (col < z0 + 2 * hg_w), 0.5, 1.0).astype(w.dtype)
    return (w * scale[None, :]).astype(BF16)


def _block_diag_pairs(w, per):
    nb, bd, _ = w.shape
    w = w.reshape(nb // per, per, bd, bd)
    eye = jnp.eye(per, dtype=w.dtype)
    out = jnp.einsum("gpij,pq->gpiqj", w, eye)
    return out.reshape(nb // per, per * bd, per * bd)


def kernel(x, c, ctx, c_ctx, ada_w, ada_b, norm_mix, norm_ffn, norm_final, w_in, w_out,
           lru_conv_w, lru_conv_b, lru_wa, lru_ba, lru_wx, lru_bx, lru_lambda,
           hgrn_lb_logits, hgrn_norm, router_w, router_b, exp_w13, exp_w2, shared_w13, shared_w2):
    assert ada_w.shape[0] == 1, "single-layer block"
    b, s, d = x.shape
    n = b * s
    lru_w = lru_conv_w.shape[2]
    n_experts = router_w.shape[2]

    de2 = exp_w13.shape[3]
    de = exp_w2.shape[2]
    w13p = _sc_pack_bf16_rows(exp_w13[0].reshape(n_experts * d, de2), 64)
    w13p = w13p.reshape(n_experts, d // 2, de2)
    w2p = _sc_pack_bf16_rows(exp_w2[0].reshape(n_experts * de, d), 32)
    w2p = w2p.reshape(n_experts, de // 2, d)

    rows = -(-(b + 1) // SUBLANES) * SUBLANES
    cs = jnp.zeros((rows, d), F32).at[:b].set(c).at[b].set(c_ctx)
    mod = _ada(cs, ada_w[0], ada_b[0][None, :]).reshape(rows, 6, d)
    mods = jnp.pad(mod, ((0, 0), (0, 2), (0, 0)))

    w_in_bf = _prep_w_in(w_in[0], lru_w, hgrn_norm.shape[1])
    gain1 = norm_mix[0][None, :]
    projx = _inproj(x, mods, gain1, w_in_bf, PROJ_TILE, shared_mod=False)
    projc = _inproj(ctx, mods[b:b + 1], gain1, w_in_bf, ctx.shape[1], shared_mod=True)

    per = LRU_G // (lru_w // LRU_BLOCKS)
    wg = jnp.stack([_block_diag_pairs(lru_wa[0, 0], per), _block_diag_pairs(lru_wx[0, 0], per),
                    _block_diag_pairs(lru_wa[0, 1], per), _block_diag_pairs(lru_wx[0, 1], per)]
                   )
    wg = (0.5 * wg).astype(BF16)
    bg = 0.5 * jnp.stack([lru_ba[0, 0], lru_bx[0, 0], lru_ba[0, 1], lru_bx[0, 1]])
    ylru = _lru(projx, projc, lru_conv_w[0], lru_conv_b[0][None, :], wg, bg, lru_lambda[0])

    yhg = _hgrn(projx, projc, hgrn_lb_logits, hgrn_norm[0][None, :], (2 * lru_w) // LANES)

    wo = w_out[0].astype(BF16).reshape(2, lru_w, d)
    x1, h2, eid_t, gate_t, rank_t, cnt = _outproj(
        ylru, yhg, x, mods, norm_ffn[0][None, :], wo,
        router_w[0].T.astype(BF16), router_b[0][:, None], PROJ_TILE)

    bm = GMM_BLOCK
    counts = cnt[:, 0]
    padded = (counts + bm - 1) // bm * bm
    pends = jnp.cumsum(padded)
    pstarts = pends - padded
    pos = _pos(pstarts, eid_t, rank_t, POS_TILE)
    cap = n * TOP_K + n_experts * bm

    xs = _padfill(_sc_scatter_rows(h2, pos, cap), pstarts + counts, padded - counts, bm)
    ys = _gmm(xs, w13p, w2p, pstarts // bm, padded // bm, bm)

    yg = _sc_gather_rows(ys, pos.reshape(-1), SC_GATHER_WINDOW).reshape(TOP_K, n, d // 2)
    tc = COMBINE_TILE
    out = _combine(gate_t.T, x1.reshape(n, d), h2, mods,
                   norm_final[None, :], shared_w13[0].astype(BF16), shared_w2[0].astype(BF16),
                   yg, tc, s // tc)
    return out.reshape(b, s, d)
```

```python
import functools

import jax
import jax.numpy as jnp
from jax import lax
from jax.experimental import pallas as pl
from jax.experimental.pallas import tpu as pltpu
from jax.experimental.pallas import tpu_sc as plsc

F32 = jnp.float32
BF16 = jnp.bfloat16
I32 = jnp.int32
U32 = jnp.uint32

EPS = 1e-6
LRU_C = 8.0
ROUTE_SCALE = 2.5
GRID_W = 64
HG_CHUNK = 32
N_GROUPS = 8
TOPK_GROUPS = 4
TOP_K = 8
LRU_BLOCKS = 8
HG_HEADS = 4
CONV_W = 4
CONV_PAD_L = 1

SUBLANES = 8
LANES = 128
N_SEG = SUBLANES
VMEM_LIMIT = 56 * 1024 * 1024

PROJ_TILE = 512
POS_TILE = 2048
GMM_BLOCK = 256
SC_GATHER_WINDOW = 64
COMBINE_TILE = 512


def _cparams(sem, vmem=VMEM_LIMIT):
    return pltpu.CompilerParams(dimension_semantics=sem, vmem_limit_bytes=vmem)


def _sigmoid(x):
    return jax.nn.sigmoid(x)


def _silu(x):
    return x * _sigmoid(x)


def _pack_rows(x):
    w = x.shape[1] // 2
    bits = pltpu.bitcast(x.astype(BF16).astype(F32), U32)
    return (bits[:, :w] >> 16) | (bits[:, w:] & jnp.uint32(0xFFFF0000))


def _unpack_rows(p):
    lo = pltpu.bitcast(p << 16, F32)
    hi = pltpu.bitcast(p & jnp.uint32(0xFFFF0000), F32)
    return jnp.concatenate([lo, hi], axis=1)


def _gelu_tanh(x):
    c = 0.7978845608028654
    return 0.5 * x * (1.0 + jnp.tanh(c * (x + 0.044715 * (x * x * x))))


def _ada_kernel(c_ref, w_ref, b_ref, o_ref):
    s = _silu(c_ref[...])
    o_ref[...] = jnp.dot(s, w_ref[...], preferred_element_type=F32,
                         precision=lax.Precision.HIGHEST) + b_ref[...]


def _ada(cs, w, b):
    rows, d = cs.shape
    n = w.shape[1]
    bn = 1024
    return pl.pallas_call(
        _ada_kernel,
        out_shape=jax.ShapeDtypeStruct((rows, n), F32),
        grid=(n // bn,),
        in_specs=[pl.BlockSpec((rows, d), lambda j: (0, 0)),
                  pl.BlockSpec((d, bn), lambda j: (0, j)),
                  pl.BlockSpec((1, bn), lambda j: (0, j))],
        out_specs=pl.BlockSpec((rows, bn), lambda j: (0, j)),
        compiler_params=_cparams(("arbitrary",)),
        name="ada",
    )(cs, w, b)


def _inproj_kernel(x_ref, mod_ref, gain_ref, w_ref, o_ref):
    x = x_ref[0]
    ms = jnp.mean(x * x, axis=-1, keepdims=True)
    y = x * lax.rsqrt(ms + EPS) * gain_ref[...]
    h = y * (1.0 + mod_ref[0, 1:2, :]) + mod_ref[0, 0:1, :]
    o_ref[0] = jnp.dot(h.astype(BF16), w_ref[...], preferred_element_type=F32)


def _inproj(x, mods, gain, w_bf, tm, shared_mod):
    b, s, d = x.shape
    n = w_bf.shape[1]
    mod_map = (lambda bi, i: (0, 0, 0)) if shared_mod else (lambda bi, i: (bi, 0, 0))
    return pl.pallas_call(
        _inproj_kernel,
        out_shape=jax.ShapeDtypeStruct((b, s, n), F32),
        grid=(b, s // tm),
        in_specs=[pl.BlockSpec((1, tm, d), lambda bi, i: (bi, i, 0)),
                  pl.BlockSpec((1, 8, d), mod_map),
                  pl.BlockSpec((1, d), lambda bi, i: (0, 0)),
                  pl.BlockSpec((d, n), lambda bi, i: (0, 0))],
        out_specs=pl.BlockSpec((1, tm, n), lambda bi, i: (bi, i, 0)),
        compiler_params=_cparams(("arbitrary", "arbitrary")),
        name="inproj",
    )(x, mods, gain, w_bf)


LRU_G = 256
LRU_CHUNK = 256


def _seg_rows(s, seg_len):
    return pl.ds(s, seg_len, stride=N_SEG)


def _out_pitch(seg_len):
    return seg_len + SUBLANES


def _lane_store(ref, d, rows, val):
    nl = val.shape[-1] // LANES
    for l in range(nl):
        ref[d * nl + l, rows, :] = val[:, l * LANES:(l + 1) * LANES]


def _lane_load(ref, d, rows, nl):
    return jnp.concatenate([ref[d * nl + l, rows, :] for l in range(nl)], axis=-1)


LRU_CONV_PIECE = 64


def _lru_coeffs(pad_ref, t0, rows, cw, cb, wg_ref, bg, sp, a_ref, b_ref, dst_rows, u_ref):
    for p0 in range(0, rows, LRU_CONV_PIECE):
        win = pad_ref[pl.ds(t0 + p0, LRU_CONV_PIECE + 2 * SUBLANES), :]
        u = cb
        for k in range(CONV_W):
            off = SUBLANES - CONV_PAD_L + k
            u = u + win[off:off + LRU_CONV_PIECE, :] * cw[k:k + 1, :]
        u_ref[p0:p0 + LRU_CONV_PIECE, :] = u
    u = u_ref[0:rows, :]
    ub = u.astype(BF16)
    half_u = 0.5 * u
    for d in range(2):
        tr = jnp.tanh(jnp.dot(ub, wg_ref[2 * d, 0], preferred_element_type=F32)
                      + bg[2 * d:2 * d + 1, :])
        ti = jnp.tanh(jnp.dot(ub, wg_ref[2 * d + 1, 0], preferred_element_type=F32)
                      + bg[2 * d + 1:2 * d + 2, :])
        half_c = (-0.5 * LRU_C) * sp[d:d + 1, :]
        log_a = half_c * tr + half_c
        a = jnp.exp(log_a)
        one_minus_a2 = -jnp.tanh(log_a) * (1.0 + a * a)
        root = jnp.where(one_minus_a2 > 0.0, one_minus_a2 * lax.rsqrt(one_minus_a2), 0.0)
        bb = root * (ti * half_u + half_u)
        _lane_store(a_ref, d, dst_rows, a)
        _lane_store(b_ref, d, dst_rows, bb)


def _seg_scan(a_ref, b_ref, h_ref, p_ref, seg_len, unroll):
    n_lead = a_ref.shape[0]
    nl = n_lead // 2
    zero = jnp.zeros((N_SEG, LANES), F32)
    one = jnp.ones((N_SEG, LANES), F32)
    init = tuple((zero, one) for _ in range(n_lead))

    def step(t, carry):
        out = []
        for i in range(n_lead):
            h, p = carry[i]
            tt = t if i < nl else seg_len - 1 - t
            rows = pl.ds(pl.multiple_of(tt * N_SEG, N_SEG), N_SEG)
            a = a_ref[i, rows, :]
            h = a * h + b_ref[i, rows, :]
            p = p * a
            if h_ref is not None:
                out_rows = pl.ds(tt, N_SEG, stride=_out_pitch(seg_len))
                h_ref[i, out_rows, :] = h
                p_ref[i, out_rows, :] = p
            out.append((h, p))
        return tuple(out)

    def body(i, carry):
        for j in range(unroll):
            carry = step(i * unroll + j, carry)
        return carry

    ends = lax.fori_loop(0, seg_len // unroll, body, init)
    h_end = [jnp.concatenate([ends[d * nl + l][0] for l in range(nl)], axis=-1) for d in range(2)]
    p_end = [jnp.concatenate([ends[d * nl + l][1] for l in range(nl)], axis=-1) for d in range(2)]
    return h_end, p_end


def _seg_carries(h_end, p_end, h0, reverse):
    order = range(N_SEG - 1, -1, -1) if reverse else range(N_SEG)
    cin = [None] * N_SEG
    c = h0
    for s in order:
        cin[s] = c
        c = p_end[s:s + 1, :] * c + h_end[s:s + 1, :]
    return cin, c


def _lru_kernel(rx_ref, rg_ref, rc_ref, cw_ref, cb_ref, wg_ref, bg_ref, lam_ref, o_ref,
                padl, padc, u_s, a_l, b_l, h_l, p_l, a_c, b_c):
    s_len = rx_ref.shape[1]
    c_len = rc_ref.shape[1]
    g = rx_ref.shape[2]
    nl = g // LANES
    seg_l = s_len // N_SEG
    seg_c = c_len // N_SEG

    zeros = jnp.zeros((SUBLANES, g), F32)
    padl[0:SUBLANES, :] = zeros
    padl[SUBLANES + s_len:2 * SUBLANES + s_len, :] = zeros
    padc[0:SUBLANES, :] = zeros
    padc[SUBLANES + c_len:2 * SUBLANES + c_len, :] = zeros
    for i in range(s_len // LRU_CHUNK):
        padl[SUBLANES + i * LRU_CHUNK:SUBLANES + (i + 1) * LRU_CHUNK, :] = (
            rx_ref[0, i * LRU_CHUNK:(i + 1) * LRU_CHUNK, :])
    padc[SUBLANES:SUBLANES + c_len, :] = rc_ref[0]

    cw = cw_ref[...]
    cb = cb_ref[...]
    bg = bg_ref[...]
    x = -lam_ref[...]
    sp = jnp.maximum(x, 0.0) + jnp.log(1.0 + jnp.exp(-jnp.abs(x)))

    _lru_coeffs(padc, 0, c_len, cw, cb, wg_ref, bg, sp, a_l, b_l, pl.ds(0, c_len), u_s)
    for i in range(2 * nl):
        for s in range(N_SEG):
            a_c[i, _seg_rows(s, seg_c), :] = a_l[i, s * seg_c:(s + 1) * seg_c, :]
            b_c[i, _seg_rows(s, seg_c), :] = b_l[i, s * seg_c:(s + 1) * seg_c, :]
    h_end, p_end = _seg_scan(a_c, b_c, None, None, seg_c, unroll=4)
    zero_row = jnp.zeros((1, g), F32)
    _, h0_f = _seg_carries(h_end[0], p_end[0], zero_row, reverse=False)
    _, h0_b = _seg_carries(h_end[1], p_end[1], zero_row, reverse=True)

    def coeff_body(s, carry):
        t0 = pl.multiple_of(s * seg_l, SUBLANES)
        _lru_coeffs(padl, t0, seg_l, cw, cb, wg_ref, bg, sp, a_l, b_l, _seg_rows(s, seg_l), u_s)
        return carry

    lax.fori_loop(0, N_SEG, coeff_body, 0)
    h_end, p_end = _seg_scan(a_l, b_l, h_l, p_l, seg_l, unroll=4)
    cin_f, _ = _seg_carries(h_end[0], p_end[0], h0_f, reverse=False)
    cin_b, _ = _seg_carries(h_end[1], p_end[1], h0_b, reverse=True)

    for s in range(N_SEG):
        rows = pl.ds(s * _out_pitch(seg_l), seg_l)
        h = (_lane_load(h_l, 0, rows, nl) + _lane_load(p_l, 0, rows, nl) * cin_f[s]) + (
            _lane_load(h_l, 1, rows, nl) + _lane_load(p_l, 1, rows, nl) * cin_b[s])
        o_ref[0, s * seg_l:(s + 1) * seg_l, :] = _gelu_tanh(rg_ref[0, s * seg_l:(s + 1) * seg_l, :]) * h


def _lru(projx, projc, cw, cb, wg, bg, lam):
    b, s_len, _ = projx.shape
    c_len = projc.shape[1]
    w = cw.shape[1]
    g = LRU_G
    ng = w // g
    seg_l = s_len // N_SEG
    seg_c = c_len // N_SEG
    assert seg_l == LRU_CHUNK and c_len % (N_SEG * 1) == 0
    return pl.pallas_call(
        _lru_kernel,
        out_shape=jax.ShapeDtypeStruct((b, s_len, w), F32),
        grid=(b, ng),
        in_specs=[pl.BlockSpec((1, s_len, g), lambda bi, j: (bi, 0, j)),
                  pl.BlockSpec((1, s_len, g), lambda bi, j: (bi, 0, ng + j)),
                  pl.BlockSpec((1, c_len, g), lambda bi, j: (bi, 0, j)),
                  pl.BlockSpec((CONV_W, g), lambda bi, j: (0, j)),
                  pl.BlockSpec((1, g), lambda bi, j: (0, j)),
                  pl.BlockSpec((4, 1, g, g), lambda bi, j: (0, j, 0, 0)),
                  pl.BlockSpec((4, g), lambda bi, j: (0, j)),
                  pl.BlockSpec((2, g), lambda bi, j: (0, j))],
        out_specs=pl.BlockSpec((1, s_len, g), lambda bi, j: (bi, 0, j)),
        scratch_shapes=[pltpu.VMEM((s_len + 2 * SUBLANES, g), F32),
                        pltpu.VMEM((c_len + 2 * SUBLANES, g), F32),
                        pltpu.VMEM((max(seg_l, c_len), g), F32),
                        pltpu.VMEM((2 * g // LANES, s_len, LANES), F32),
                        pltpu.VMEM((2 * g // LANES, s_len, LANES), F32),
                        pltpu.VMEM((2 * g // LANES, N_SEG * _out_pitch(seg_l), LANES), F32),
                        pltpu.VMEM((2 * g // LANES, N_SEG * _out_pitch(seg_l), LANES), F32),
                        pltpu.VMEM((2 * g // LANES, c_len, LANES), F32),
                        pltpu.VMEM((2 * g // LANES, c_len, LANES), F32)],
        compiler_params=_cparams(("arbitrary", "arbitrary")),
        name="lru",
    )(projx, projx, projc, cw, cb, wg, bg, lam)


LOG2_E = 1.4426950408889634
HG_PITCH_L = GRID_W + SUBLANES
HG_PITCH_C = SUBLANES


def _hg_gates(half_z, lb):
    c = 0.5 * (1.0 - lb)
    ct = c * jnp.tanh(half_z)
    logf = jnp.log(0.5 * (1.0 + lb) + ct)
    k = c - ct
    return logf, k


def _hg_prepass(load, width, pitch, lb, qd, kd, kl, vs, dec):
    n_pos = HG_CHUNK
    for d, zname in enumerate(("zf", "zb")):
        order = range(n_pos) if d == 0 else range(n_pos - 1, -1, -1)
        g = jnp.zeros((width, LANES), F32)
        lbd = lb[d:d + 1, :]
        for p in order:
            logf, k = _hg_gates(load(zname, p), lbd)
            g = g + logf
            qd[d, p * pitch:p * pitch + width, :] = g
            kl[d, p * pitch:p * pitch + width, :] = k
        g_last = g * LOG2_E
        dec[d, 0:width, :] = jnp.exp2(g_last)
        for p in range(n_pos):
            rows = slice(p * pitch, p * pitch + width)
            gp = qd[d, rows, :] * LOG2_E
            k = kl[d, rows, :]
            qd[d, rows, :] = load("q", p) * jnp.exp2(gp)
            kd[d, rows, :] = k * jnp.exp2(-gp)
            kl[d, rows, :] = k * jnp.exp2(g_last - gp)
    for p in range(n_pos):
        vs[p * pitch:p * pitch + width, :] = load("v", p)


HG_GROUP = 4
_NT = (((1,), (1,)), ((), ()))
_TN = (((0,), (0,)), ((), ()))


def _hg_group_rows(ref, lead, c0, pitch):
    parts = []
    for j in range(HG_GROUP):
        rows = pl.ds(c0 + j, HG_CHUNK, stride=pitch)
        parts.append(ref[rows, :] if lead is None else ref[lead, rows, :])
    return jnp.concatenate(parts, axis=0)


def _hg_scores(g, pitch, qd, kd, masks, sc_ref):
    c0 = g * HG_GROUP
    total = None
    for d in range(2):
        q = _hg_group_rows(qd, d, c0, pitch).astype(BF16)
        k = _hg_group_rows(kd, d, c0, pitch).astype(BF16)
        sc = jnp.where(masks[d], lax.dot_general(q, k, _NT, preferred_element_type=F32), 0.0)
        total = sc if total is None else total + sc
    sc_ref[g] = total.astype(BF16)


def _hg_summaries(g, pitch, kl, vs, sc_ref, oi, kv):
    c0 = g * HG_GROUP
    v = _hg_group_rows(vs, None, c0, pitch).astype(BF16)
    if oi is not None:
        o = jnp.dot(sc_ref[g], v, preferred_element_type=F32)
        for j in range(HG_GROUP):
            oi[pl.ds(c0 + j, HG_CHUNK, stride=pitch), :] = o[j * HG_CHUNK:(j + 1) * HG_CHUNK, :]
    kls = jnp.concatenate([_hg_group_rows(kl, 0, c0, pitch), _hg_group_rows(kl, 1, c0, pitch)],
                          axis=1).astype(BF16)
    for j in range(HG_GROUP):
        sl = slice(j * HG_CHUNK, (j + 1) * HG_CHUNK)
        both = lax.dot_general(v[sl], kls[sl], _TN, preferred_element_type=F32)
        kv[0, c0 + j] = both[:, :LANES]
        kv[1, c0 + j] = both[:, LANES:]


def _hg_steps(step0, n_steps, n_chunks, pitch, qd, dec, kv, states, ox):
    states = list(states)
    for u in range(n_steps):
        cs = (step0 + u, n_chunks - 1 - (step0 + u))
        if ox is not None:
            rows = [pl.ds(c, HG_CHUNK, stride=pitch) for c in cs]
            q2 = jnp.concatenate([qd[0, rows[0], :], qd[1, rows[1], :]], axis=0).astype(BF16)
            s2 = jnp.concatenate(states, axis=0).astype(BF16)
            o2 = lax.dot_general(q2, s2, _NT, preferred_element_type=F32)
            ox[0, rows[0], :] = o2[:HG_CHUNK, :LANES]
            ox[1, rows[1], :] = o2[HG_CHUNK:, LANES:]
        for d in range(2):
            states[d] = states[d] * dec[d, pl.ds(cs[d], 1), :] + kv[d, cs[d]]
    return tuple(states)


def _hgrn_kernel(q_ref, v_ref, zf_ref, zb_ref, g_ref, qc_ref, vc_ref, zfc_ref, zbc_ref,
                 lbl_ref, gain_ref, o_ref, qd, kd, kl, vs, dec, oi, ox, kv, sc):
    s_len = q_ref.shape[1]
    c_len = qc_ref.shape[1]
    n_col = s_len // HG_CHUNK
    n_cc = c_len // HG_CHUNK
    assert n_col == GRID_W and n_cc == HG_PITCH_C
    assert n_col % HG_GROUP == 0 and n_cc % HG_GROUP == 0

    lg = lbl_ref[...]
    m = jnp.max(lg, axis=1, keepdims=True)
    ex = jnp.exp(lg - m)
    lb = ex[:, 0, :] / jnp.sum(ex, axis=1)

    gr = HG_GROUP * HG_CHUNK
    ii = lax.broadcasted_iota(I32, (gr, gr), 0)
    jj = lax.broadcasted_iota(I32, (gr, gr), 1)
    same = (ii // HG_CHUNK) == (jj // HG_CHUNK)
    masks = (jnp.logical_and(same, jj <= ii), jnp.logical_and(same, jj >= ii))

    c_refs = {"q": qc_ref, "v": vc_ref, "zf": zfc_ref, "zb": zbc_ref}

    def load_c(name, p):
        return c_refs[name][0, pl.ds(p, n_cc, stride=HG_CHUNK), :]

    _hg_prepass(load_c, n_cc, HG_PITCH_C, lb, qd, kd, kl, vs, dec)
    for g in range(n_cc // HG_GROUP):
        _hg_summaries(g, HG_PITCH_C, kl, vs, None, None, kv)
    zero = jnp.zeros((LANES, LANES), F32)
    states = _hg_steps(0, n_cc, n_cc, HG_PITCH_C, qd, dec, kv, (zero, zero), None)

    l_refs = {"q": q_ref, "v": v_ref, "zf": zf_ref, "zb": zb_ref}

    def load_l(name, p):
        return l_refs[name][0, p * n_col:(p + 1) * n_col, :]

    _hg_prepass(load_l, n_col, HG_PITCH_L, lb, qd, kd, kl, vs, dec)

    for g in range(n_col // HG_GROUP):
        _hg_scores(g, HG_PITCH_L, qd, kd, masks, sc)
    for g in range(n_col // HG_GROUP):
        _hg_summaries(g, HG_PITCH_L, kl, vs, sc, oi, kv)
    _hg_steps(0, n_col, n_col, HG_PITCH_L, qd, dec, kv, states, ox)

    gain = gain_ref[...]
    for r in range(HG_CHUNK):
        rows = slice(r * HG_PITCH_L, r * HG_PITCH_L + n_col)
        o = oi[rows, :] + (ox[0, rows, :] + ox[1, rows, :])
        y = o * lax.rsqrt(jnp.mean(o * o, axis=-1, keepdims=True) + EPS) * gain
        o_ref[0, r * n_col:(r + 1) * n_col, :] = y * _silu(g_ref[0, r * n_col:(r + 1) * n_col, :])


def _hgrn(projx, projc, lbl, gain, col0):
    b, s_len, _ = projx.shape
    c_len = projc.shape[1]
    hw = gain.shape[1]
    nh = hw // LANES
    n_slots = lbl.shape[1]

    def xs(k):
        return pl.BlockSpec((1, s_len, LANES), lambda bi, h, k=k: (bi, 0, col0 + k * nh + h))

    def cs(k):
        return pl.BlockSpec((1, c_len, LANES), lambda bi, h, k=k: (bi, 0, col0 + k * nh + h))

    sc_rows = HG_CHUNK * HG_PITCH_L
    return pl.pallas_call(
        _hgrn_kernel,
        out_shape=jax.ShapeDtypeStruct((b, s_len, hw), F32),
        grid=(b, nh),
        in_specs=[xs(0), xs(1), xs(2), xs(3), xs(4), cs(0), cs(1), cs(2), cs(3),
                  pl.BlockSpec((2, n_slots, LANES), lambda bi, h: (0, 0, h)),
                  pl.BlockSpec((1, LANES), lambda bi, h: (0, h))],
        out_specs=pl.BlockSpec((1, s_len, LANES), lambda bi, h: (bi, 0, h)),
        scratch_shapes=[pltpu.VMEM((2, sc_rows, LANES), F32),
                        pltpu.VMEM((2, sc_rows, LANES), F32),
                        pltpu.VMEM((2, sc_rows, LANES), F32),
                        pltpu.VMEM((sc_rows, LANES), F32),
                        pltpu.VMEM((2, HG_PITCH_L, LANES), F32),
                        pltpu.VMEM((sc_rows, LANES), F32),
                        pltpu.VMEM((2, sc_rows, LANES), F32),
                        pltpu.VMEM((2, GRID_W, LANES, LANES), F32),
                        pltpu.VMEM((GRID_W // HG_GROUP, HG_GROUP * HG_CHUNK,
                                    HG_GROUP * HG_CHUNK), BF16)],
        compiler_params=_cparams(("arbitrary", "arbitrary")),
        name="hgrn",
    )(projx, projx, projx, projx, projx, projc, projc, projc, projc, lbl, gain)


def _first_index_of_max(vals, iota, big):
    m = jnp.max(vals, axis=0, keepdims=True)
    idx = jnp.min(jnp.where(vals == m, iota, big), axis=0, keepdims=True)
    return m, idx


def _route(logits_t, rb, n_experts):
    t = logits_t.shape[1]
    gsz = n_experts // N_GROUPS
    neg = -jnp.inf
    scores = _sigmoid(logits_t)
    biased = scores + rb
    iota_g = lax.broadcasted_iota(I32, (gsz, t), 0)
    gscore = []
    for gi in range(N_GROUPS):
        blk = biased[gi * gsz:(gi + 1) * gsz, :]
        m1, i1 = _first_index_of_max(blk, iota_g, gsz)
        m2 = jnp.max(jnp.where(iota_g == i1, neg, blk), axis=0, keepdims=True)
        gscore.append(m1 + m2)
    gs = jnp.concatenate(gscore, axis=0)
    iota_n = lax.broadcasted_iota(I32, (N_GROUPS, t), 0)
    gsel = jnp.zeros((N_GROUPS, t), jnp.bool_)
    for _ in range(TOPK_GROUPS):
        _, gi1 = _first_index_of_max(gs, iota_n, N_GROUPS)
        hit = iota_n == gi1
        gsel = jnp.logical_or(gsel, hit)
        gs = jnp.where(hit, neg, gs)
    emask = jnp.concatenate(
        [jnp.broadcast_to(gsel[gi:gi + 1, :], (gsz, t)) for gi in range(N_GROUPS)], axis=0)
    masked = jnp.where(emask, biased, neg)
    iota_e = lax.broadcasted_iota(I32, (n_experts, t), 0)
    ids, gates = [], []
    sel = jnp.zeros((n_experts, t), F32)
    for _ in range(TOP_K):
        _, ei = _first_index_of_max(masked, iota_e, n_experts)
        hit = iota_e == ei
        ids.append(ei)
        gates.append(jnp.sum(jnp.where(hit, scores, 0.0), axis=0, keepdims=True))
        sel = jnp.where(hit, 1.0, sel)
        masked = jnp.where(hit, neg, masked)
    ids = jnp.concatenate(ids, axis=0)
    gates = jnp.concatenate(gates, axis=0)
    gates = gates / jnp.sum(gates, axis=0, keepdims=True) * ROUTE_SCALE
    return ids, gates, sel


def _outproj_kernel(yl_ref, yh_ref, x_ref, mod_ref, gain_ref, wo_ref, rwt_ref, rb_ref,
                    x1_ref, h2_ref, eid_ref, gate_ref, rank_ref, cnt_ref, carry):
    first = jnp.logical_and(pl.program_id(0) == 0, pl.program_id(1) == 0)

    @pl.when(first)
    def _():
        carry[...] = jnp.zeros(carry.shape, F32)

    y = jnp.dot(yl_ref[0].astype(BF16), wo_ref[0], preferred_element_type=F32)
    y = y + jnp.dot(yh_ref[0].astype(BF16), wo_ref[1], preferred_element_type=F32)
    x1 = x_ref[0] + mod_ref[0, 2:3, :] * y
    x1_ref[0] = x1
    ms = jnp.mean(x1 * x1, axis=-1, keepdims=True)
    h2 = x1 * lax.rsqrt(ms + EPS) * gain_ref[...]
    h2 = h2 * (1.0 + mod_ref[0, 4:5, :]) + mod_ref[0, 3:4, :]
    h2_ref[...] = _pack_rows(h2)

    n_experts = rwt_ref.shape[0]
    t = h2.shape[0]
    logits_t = lax.dot_general(rwt_ref[...], h2.astype(BF16), _NT,
                               preferred_element_type=F32)
    ids, gates, sel = _route(logits_t, rb_ref[...], n_experts)
    eid_ref[...] = ids
    gate_ref[...] = gates

    ti = lax.broadcasted_iota(I32, (t, t), 0)
    tj = lax.broadcasted_iota(I32, (t, t), 1)
    upper = (ti < tj).astype(BF16)
    selb = sel.astype(BF16)
    before = jnp.dot(selb, upper, preferred_element_type=F32) + carry[:, 0:1]
    iota_e = lax.broadcasted_iota(I32, (n_experts, t), 0)
    ranks = [jnp.sum(jnp.where(iota_e == ids[k:k + 1, :], before, 0.0), axis=0, keepdims=True)
             for k in range(TOP_K)]
    rank_ref[...] = jnp.concatenate(ranks, axis=0).astype(I32)
    total = carry[...] + jnp.dot(selb, jnp.ones((t, LANES), BF16), preferred_element_type=F32)
    carry[...] = total
    cnt_ref[...] = total.astype(I32)


def _outproj(ylru, yhg, x, mods, gain, wo, rwt, rb, tm):
    b, s, d = x.shape
    hw = ylru.shape[2]
    e = rwt.shape[0]
    n = b * s
    nt = s // tm
    tok = lambda bi, i: (0, bi * nt + i)
    return pl.pallas_call(
        _outproj_kernel,
        out_shape=(jax.ShapeDtypeStruct((b, s, d), F32),
                   jax.ShapeDtypeStruct((n, d // 2), U32),
                   jax.ShapeDtypeStruct((TOP_K, n), I32),
                   jax.ShapeDtypeStruct((TOP_K, n), F32),
                   jax.ShapeDtypeStruct((TOP_K, n), I32),
                   jax.ShapeDtypeStruct((e, LANES), I32)),
        grid=(b, nt),
        in_specs=[pl.BlockSpec((1, tm, hw), lambda bi, i: (bi, i, 0)),
                  pl.BlockSpec((1, tm, hw), lambda bi, i: (bi, i, 0)),
                  pl.BlockSpec((1, tm, d), lambda bi, i: (bi, i, 0)),
                  pl.BlockSpec((1, 8, d), lambda bi, i: (bi, 0, 0)),
                  pl.BlockSpec((1, d), lambda bi, i: (0, 0)),
                  pl.BlockSpec((2, hw, d), lambda bi, i: (0, 0, 0)),
                  pl.BlockSpec((e, d), lambda bi, i: (0, 0)),
                  pl.BlockSpec((e, 1), lambda bi, i: (0, 0))],
        out_specs=(pl.BlockSpec((1, tm, d), lambda bi, i: (bi, i, 0)),
                   pl.BlockSpec((tm, d // 2), lambda bi, i: (bi * nt + i, 0)),
                   pl.BlockSpec((TOP_K, tm), tok),
                   pl.BlockSpec((TOP_K, tm), tok),
                   pl.BlockSpec((TOP_K, tm), tok),
                   pl.BlockSpec((e, LANES), lambda bi, i: (0, 0))),
        scratch_shapes=[pltpu.VMEM((e, LANES), F32)],
        compiler_params=_cparams(("arbitrary", "arbitrary")),
        name="outproj",
    )(ylru, yhg, x, mods, gain, wo, rwt, rb)


def _pos_kernel(pstart, eid_ref, rank_ref, pos_ref):
    eid = eid_ref[...]

    def body(e, acc):
        return jnp.where(eid == e, pstart[e], acc)

    base = lax.fori_loop(0, pstart.shape[0], body, jnp.zeros(eid.shape, I32))
    pos_ref[...] = base + rank_ref[...]


def _pos(pstarts, eid_t, rank_t, tn):
    k, n = eid_t.shape
    spec = pl.BlockSpec((k, tn), lambda i, ps: (0, i))
    return pl.pallas_call(
        _pos_kernel,
        out_shape=jax.ShapeDtypeStruct((k, n), I32),
        grid_spec=pltpu.PrefetchScalarGridSpec(
            num_scalar_prefetch=1, grid=(n // tn,), in_specs=[spec, spec], out_specs=spec),
        compiler_params=_cparams(("arbitrary",)),
        name="pos",
    )(pstarts, eid_t, rank_t)


def _pad_fill(padstart, padlen, zbuf, xs_ref, sem, wait):
    nbits = zbuf.shape[0].bit_length() - 1
    low_bits = SUBLANES.bit_length() - 1

    def go(cp):
        if wait:
            cp.wait()
        else:
            cp.start()

    def body(e, carry):
        st = padstart[e]
        ln = padlen[e]
        end = st + ln
        for bit in range(nbits - 1, low_bits - 1, -1):
            size = 1 << bit
            back = ((ln >> (bit + 1)) << (bit + 1)) + size

            @pl.when((ln & size) != 0)
            def _():
                start = pl.multiple_of(end - back, SUBLANES)
                go(pltpu.make_async_copy(zbuf.at[pl.ds(0, size)],
                                         xs_ref.at[pl.ds(start, size)], sem))
        for j in range(SUBLANES - 1):
            @pl.when(j < (ln & (SUBLANES - 1)))
            def _():
                go(pltpu.make_async_copy(zbuf.at[pl.ds(0, 1)], xs_ref.at[pl.ds(st + j, 1)], sem))
        return carry

    lax.fori_loop(0, padstart.shape[0], body, 0)


def _padfill_kernel(padstart, padlen, xs_in, xs_ref, zbuf, sem):
    del xs_in
    zbuf[...] = jnp.zeros(zbuf.shape, zbuf.dtype)
    _pad_fill(padstart, padlen, zbuf, xs_ref, sem, wait=False)
    _pad_fill(padstart, padlen, zbuf, xs_ref, sem, wait=True)


def _padfill(xs, padstart, padlen, bm):
    return pl.pallas_call(
        _padfill_kernel,
        out_shape=jax.ShapeDtypeStruct(xs.shape, xs.dtype),
        grid_spec=pltpu.PrefetchScalarGridSpec(
            num_scalar_prefetch=2,
            grid=(1,),
            in_specs=[pl.BlockSpec(memory_space=pl.ANY)],
            out_specs=pl.BlockSpec(memory_space=pl.ANY),
            scratch_shapes=[pltpu.VMEM((bm, xs.shape[1]), xs.dtype), pltpu.SemaphoreType.DMA(())]),
        input_output_aliases={2: 0},
        compiler_params=_cparams(("arbitrary",)),
        name="padfill",
    )(padstart, padlen, xs)


SC_SCATTER_WINDOW = 128


def _sc_mesh():
    return plsc.VectorSubcoreMesh(core_axis_name="core", subcore_axis_name="subcore")


def _sc_scatter_rows(rows, pos, cap):
    n, dw = rows.shape
    top_k = pos.shape[0]
    mesh = _sc_mesh()
    n_workers = mesh.num_cores * mesh.num_subcores
    win = SC_SCATTER_WINDOW
    per_w = n // n_workers
    n_chunks = per_w // win
    assert per_w * n_workers == n and n_chunks * win == per_w
    pos_w = pos.reshape(top_k, n_workers, n_chunks, win).transpose(1, 2, 0, 3)
    pos_w = pos_w.reshape(n_workers, n_chunks * top_k, win)

    @functools.partial(pl.kernel, mesh=mesh,
                       out_type=jax.ShapeDtypeStruct((cap, dw), rows.dtype),
                       scratch_types=[pltpu.VMEM((n_chunks * top_k, win), I32),
                                      pltpu.VMEM((win, dw), rows.dtype),
                                      pltpu.SemaphoreType.DMA])
    def scatter(r_hbm, p_hbm, o_hbm, idx_v, rows_v, sem):
        wid = lax.axis_index("subcore") * mesh.num_cores + lax.axis_index("core")
        pltpu.sync_copy(p_hbm.at[wid], idx_v)

        @pl.loop(0, n_chunks)
        def _(c):
            pltpu.sync_copy(r_hbm.at[pl.ds(wid * per_w + c * win, win)], rows_v)
            copies = [pltpu.async_copy(rows_v, o_hbm.at[idx_v.at[c * top_k + k]], sem)
                      for k in range(top_k)]
            for cp in copies:
                cp.wait()

    return scatter(rows, pos_w)


GMM_EXPERTS = 4
GMM_PAIR = 4
GMM_SLOTS = 16
GMM_AHEAD = GMM_SLOTS - GMM_PAIR


def _gmm_rows(xw, w13b, w2b, de):
    xb = _unpack_rows(xw).astype(BF16)
    u = jnp.dot(xb, w13b[...], preferred_element_type=F32)
    hmid = _silu(u[:, :de]) * u[:, de:]
    return _pack_rows(jnp.dot(hmid.astype(BF16), w2b[...], preferred_element_type=F32))


def _gmm_kernel(blk0, nblk, w13_ref, w2_ref, xs_ref, ys_ref, w13b, w2b, xbuf, ybuf, isem, osem):
    step = pl.program_id(0)
    n_e = nblk.shape[0]
    slots, bm = xbuf.shape[0], xbuf.shape[1]
    de = w2_ref.shape[1]
    total = blk0[n_e - 1] + nblk[n_e - 1]

    def x_copy(g):
        rows = pl.ds(pl.multiple_of(g * bm, bm), bm)
        slot = g & (slots - 1)
        return pltpu.make_async_copy(xs_ref.at[rows], xbuf.at[slot], isem.at[slot])

    def y_copy(g):
        rows = pl.ds(pl.multiple_of(g * bm, bm), bm)
        slot = g & (slots - 1)
        return pltpu.make_async_copy(ybuf.at[slot], ys_ref.at[rows], osem.at[slot])

    @pl.when(step == 0)
    def _():
        for j in range(GMM_AHEAD):
            @pl.when(j < total)
            def _():
                x_copy(j).start()

    def run_blocks(g0, count):
        gs = [g0 + j for j in range(count)]
        for g in gs:
            x_copy(g).wait()
        for g in gs:
            @pl.when(g + GMM_AHEAD < total)
            def _():
                x_copy(g + GMM_AHEAD).start()

            @pl.when(g >= slots)
            def _():
                y_copy(g - slots).wait()
        for g in gs:
            slot = g & (slots - 1)
            ybuf[slot] = _gmm_rows(xbuf[slot], w13b, w2b, de)
        for g in gs:
            y_copy(g).start()

    def expert_body(j, carry):
        e = step * GMM_EXPERTS + j
        nb = nblk[e]
        b0 = blk0[e]
        w13b[...] = w13_ref[j].astype(BF16)
        w2b[...] = w2_ref[j].astype(BF16)

        def pair_body(i, c):
            run_blocks(b0 + GMM_PAIR * i, GMM_PAIR)
            return c

        lax.fori_loop(0, nb // GMM_PAIR, pair_body, 0)
        for r in range(1, GMM_PAIR):
            @pl.when(nb % GMM_PAIR == r)
            def _():
                run_blocks(b0 + nb - r, r)
        return carry

    lax.fori_loop(0, GMM_EXPERTS, expert_body, 0)

    @pl.when(step == pl.num_programs(0) - 1)
    def _():
        for j in range(1, slots + 1):
            @pl.when(total - j >= 0)
            def _():
                y_copy(total - j).wait()


def _gmm(xs, w13, w2, blk0, nblk, bm):
    cap, dw = xs.shape
    e, d, de2 = w13.shape
    de = w2.shape[1]
    return pl.pallas_call(
        _gmm_kernel,
        out_shape=jax.ShapeDtypeStruct((cap, dw), U32),
        grid_spec=pltpu.PrefetchScalarGridSpec(
            num_scalar_prefetch=2,
            grid=(e // GMM_EXPERTS,),
            in_specs=[pl.BlockSpec((GMM_EXPERTS, d, de2), lambda i, b0, nb: (i, 0, 0)),
                      pl.BlockSpec((GMM_EXPERTS, de, d), lambda i, b0, nb: (i, 0, 0)),
                      pl.BlockSpec(memory_space=pl.ANY)],
            out_specs=pl.BlockSpec(memory_space=pl.ANY),
            scratch_shapes=[pltpu.VMEM((d, de2), BF16), pltpu.VMEM((de, d), BF16),
                            pltpu.VMEM((GMM_SLOTS, bm, dw), U32),
                            pltpu.VMEM((GMM_SLOTS, bm, dw), U32),
                            pltpu.SemaphoreType.DMA((GMM_SLOTS,)),
                            pltpu.SemaphoreType.DMA((GMM_SLOTS,))]),
        compiler_params=_cparams(("arbitrary",)),
        name="gmm",
    )(blk0, nblk, w13, w2, xs)


def _sc_gather_rows(table, idx, window):
    n_idx = idx.shape[0]
    dw = table.shape[1]
    mesh = _sc_mesh()
    n_workers = mesh.num_cores * mesh.num_subcores
    per_w = n_idx // n_workers
    n_chunks = per_w // window
    assert per_w * n_workers == n_idx and n_chunks * window == per_w and n_chunks % 2 == 0

    @functools.partial(pl.kernel, mesh=mesh,
                       out_type=jax.ShapeDtypeStruct((n_idx, dw), table.dtype),
                       scratch_types=[pltpu.VMEM((per_w,), I32),
                                      pltpu.VMEM((window, dw), table.dtype),
                                      pltpu.VMEM((window, dw), table.dtype),
                                      pltpu.SemaphoreType.DMA, pltpu.SemaphoreType.DMA,
                                      pltpu.SemaphoreType.DMA, pltpu.SemaphoreType.DMA])
    def gather(x_hbm, i_hbm, o_hbm, idx_v, rows0, rows1, gs0, gs1, os0, os1):
        wid = lax.axis_index("subcore") * mesh.num_cores + lax.axis_index("core")
        base = wid * per_w
        pltpu.sync_copy(i_hbm.at[pl.ds(base, per_w)], idx_v)
        bufs = ((rows0, gs0, os0), (rows1, gs1, os1))

        def fetch(j, slot):
            rows, gs, _ = bufs[slot]
            off = pl.multiple_of(j * window, window)
            return pltpu.async_copy(x_hbm.at[idx_v.at[pl.ds(off, window)]], rows, gs)

        def fetch_wait(slot):
            rows, gs, _ = bufs[slot]
            pltpu.make_async_copy(x_hbm.at[idx_v.at[pl.ds(0, window)]], rows, gs).wait()

        def put(j, slot):
            rows, _, osem = bufs[slot]
            off = pl.multiple_of(j * window, window)
            return pltpu.async_copy(rows, o_hbm.at[pl.ds(base + off, window)], osem)

        def put_wait(slot):
            rows, _, osem = bufs[slot]
            pltpu.make_async_copy(rows, o_hbm.at[pl.ds(base, window)], osem).wait()

        fetch(0, 0)

        @pl.loop(0, n_chunks, step=2)
        def _(j):
            @pl.when(j > 0)
            def _():
                put_wait(1)
            fetch(j + 1, 1)
            fetch_wait(0)
            put(j, 0)
            put_wait(0)

            @pl.when(j + 2 < n_chunks)
            def _():
                fetch(j + 2, 0)
            fetch_wait(1)
            put(j + 1, 1)

        put_wait(1)

    return gather(table, idx)


def _combine_kernel(gate_ref, x1_ref, h2_ref, mod_ref, gain_ref, sw13_ref, sw2_ref, yg_ref, o_ref):
    ds_ = sw2_ref.shape[0]
    hb = _unpack_rows(h2_ref[...]).astype(BF16)
    u = jnp.dot(hb, sw13_ref[...], preferred_element_type=F32)
    hmid = _silu(u[:, :ds_]) * u[:, ds_:]
    y = jnp.dot(hmid.astype(BF16), sw2_ref[...], preferred_element_type=F32)

    gate = gate_ref[...]
    moe = gate[:, 0:1] * _unpack_rows(yg_ref[0])
    for k in range(1, TOP_K):
        moe = moe + gate[:, k:k + 1] * _unpack_rows(yg_ref[k])
    xo = x1_ref[...] + mod_ref[0, 5:6, :] * (moe + y)
    ms = jnp.mean(xo * xo, axis=-1, keepdims=True)
    o_ref[...] = xo * lax.rsqrt(ms + EPS) * gain_ref[...]


def _combine(gate_tok, x1, h2, mods, gain, sw13, sw2, yg, tc, tiles_per_batch):
    n, d = x1.shape
    dw = h2.shape[1]
    nt = n // tc
    ds2 = sw13.shape[1]
    ds_ = sw2.shape[0]
    return pl.pallas_call(
        _combine_kernel,
        out_shape=jax.ShapeDtypeStruct((n, d), F32),
        grid=(nt,),
        in_specs=[pl.BlockSpec((tc, TOP_K), lambda i: (i, 0)),
                  pl.BlockSpec((tc, d), lambda i: (i, 0)),
                  pl.BlockSpec((tc, dw), lambda i: (i, 0)),
                  pl.BlockSpec((1, 8, d), lambda i: (i // tiles_per_batch, 0, 0)),
                  pl.BlockSpec((1, d), lambda i: (0, 0)),
                  pl.BlockSpec((d, ds2), lambda i: (0, 0)),
                  pl.BlockSpec((ds_, d), lambda i: (0, 0)),
                  pl.BlockSpec((TOP_K, tc, dw), lambda i: (0, i, 0))],
        out_specs=pl.BlockSpec((tc, d), lambda i: (i, 0)),
        compiler_params=_cparams(("arbitrary",)),
        name="combine",
    )(gate_tok, x1, h2, mods, gain, sw13, sw2, yg)


def _prep_w_in(w, lru_w, hg_w):
    z0 = 2 * lru_w + 2 * hg_w
    col = jnp.arange(w.shape[1])
    scale = jnp.where((col >= z0) & (col < z0 + 2 * hg_w), 0.5, 1.0).astype(w.dtype)
    return (w * scale[None, :]).astype(BF16)


def _block_diag_pairs(w, per):
    nb, bd, _ = w.shape
    w = w.reshape(nb // per, per, bd, bd)
    eye = jnp.eye(per, dtype=w.dtype)
    out = jnp.einsum("gpij,pq->gpiqj", w, eye)
    return out.reshape(nb // per, per * bd, per * bd)


def kernel(x, c, ctx, c_ctx, ada_w, ada_b, norm_mix, norm_ffn, norm_final, w_in, w_out,
           lru_conv_w, lru_conv_b, lru_wa, lru_ba, lru_wx, lru_bx, lru_lambda,
           hgrn_lb_logits, hgrn_norm, router_w, router_b, exp_w13, exp_w2, shared_w13, shared_w2):
    assert ada_w.shape[0] == 1, "single-layer block"
    b, s, d = x.shape
    n = b * s
    lru_w = lru_conv_w.shape[2]
    n_experts = router_w.shape[2]

    rows = -(-(b + 1) // SUBLANES) * SUBLANES
    cs = jnp.zeros((rows, d), F32).at[:b].set(c).at[b].set(c_ctx)
    mod = _ada(cs, ada_w[0], ada_b[0][None, :]).reshape(rows, 6, d)
    mods = jnp.pad(mod, ((0, 0), (0, 2), (0, 0)))

    w_in_bf = _prep_w_in(w_in[0], lru_w, hgrn_norm.shape[1])
    gain1 = norm_mix[0][None, :]
    projx = _inproj(x, mods, gain1, w_in_bf, PROJ_TILE, shared_mod=False)
    projc = _inproj(ctx, mods[b:b + 1], gain1, w_in_bf, ctx.shape[1], shared_mod=True)

    per = LRU_G // (lru_w // LRU_BLOCKS)
    wg = jnp.stack([_block_diag_pairs(lru_wa[0, 0], per), _block_diag_pairs(lru_wx[0, 0], per),
                    _block_diag_pairs(lru_wa[0, 1], per), _block_diag_pairs(lru_wx[0, 1], per)]
                   )
    wg = (0.5 * wg).astype(BF16)
    bg = 0.5 * jnp.stack([lru_ba[0, 0], lru_bx[0, 0], lru_ba[0, 1], lru_bx[0, 1]])
    ylru = _lru(projx, projc, lru_conv_w[0], lru_conv_b[0][None, :], wg, bg, lru_lambda[0])

    yhg = _hgrn(projx, projc, hgrn_lb_logits, hgrn_norm[0][None, :], (2 * lru_w) // LANES)

    wo = w_out[0].astype(BF16).reshape(2, lru_w, d)
    x1, h2, eid_t, gate_t, rank_t, cnt = _outproj(
        ylru, yhg, x, mods, norm_ffn[0][None, :], wo,
        router_w[0].T.astype(BF16), router_b[0][:, None], PROJ_TILE)

    bm = GMM_BLOCK
    counts = cnt[:, 0]
    padded = (counts + bm - 1) // bm * bm
    pends = jnp.cumsum(padded)
    pstarts = pends - padded
    pos = _pos(pstarts, eid_t, rank_t, POS_TILE)
    cap = n * TOP_K + n_experts * bm

    xs = _padfill(_sc_scatter_rows(h2, pos, cap), pstarts + counts, padded - counts, bm)
    ys = _gmm(xs, exp_w13[0], exp_w2[0], pstarts // bm, padded // bm, bm)

    yg = _sc_gather_rows(ys, pos.reshape(-1), SC_GATHER_WINDOW).reshape(TOP_K, n, d // 2)
    tc = COMBINE_TILE
    out = _combine(gate_t.T, x1.reshape(n, d), h2, mods,
                   norm_final[None, :], shared_w13[0].astype(BF16), shared_w2[0].astype(BF16),
                   yg, tc, s // tc)
    return out.reshape(b, s, d)
```

```python
import functools

import jax
import jax.numpy as jnp
from jax import lax
from jax.experimental import pallas as pl
from jax.experimental.pallas import tpu as pltpu
from jax.experimental.pallas import tpu_sc as plsc

F32 = jnp.float32
BF16 = jnp.bfloat16
I32 = jnp.int32
U32 = jnp.uint32

EPS = 1e-6
LRU_C = 8.0
ROUTE_SCALE = 2.5
GRID_W = 64
HG_CHUNK = 32
N_GROUPS = 8
TOPK_GROUPS = 4
TOP_K = 8
LRU_BLOCKS = 8
HG_HEADS = 4
CONV_W = 4
CONV_PAD_L = 1

SUBLANES = 8
LANES = 128
N_SEG = SUBLANES
VMEM_LIMIT = 56 * 1024 * 1024

PROJ_TILE = 512
POS_TILE = 2048
GMM_BLOCK = 256
SC_GATHER_WINDOW = 64
COMBINE_TILE = 512


def _cparams(sem, vmem=VMEM_LIMIT):
    return pltpu.CompilerParams(dimension_semantics=sem, vmem_limit_bytes=vmem)


def _sigmoid(x):
    return jax.nn.sigmoid(x)


def _silu(x):
    half = 0.5 * x
    return half * jnp.tanh(half) + half


def _pack_rows(x):
    w = x.shape[1] // 2
    bits = pltpu.bitcast(x.astype(BF16).astype(F32), U32)
    return (bits[:, :w] >> 16) | (bits[:, w:] & jnp.uint32(0xFFFF0000))


def _unpack_rows(p):
    lo = pltpu.bitcast(p << 16, F32)
    hi = pltpu.bitcast(p & jnp.uint32(0xFFFF0000), F32)
    return jnp.concatenate([lo, hi], axis=1)


def _gelu_tanh(x):
    c = 0.7978845608028654
    return 0.5 * x * (1.0 + jnp.tanh(c * (x + 0.044715 * (x * x * x))))


def _ada_kernel(c_ref, w_ref, b_ref, o_ref):
    s = _silu(c_ref[...])
    o_ref[...] = jnp.dot(s, w_ref[...], preferred_element_type=F32,
                         precision=lax.Precision.HIGHEST) + b_ref[...]


def _ada(cs, w, b):
    rows, d = cs.shape
    n = w.shape[1]
    bn = 1024
    return pl.pallas_call(
        _ada_kernel,
        out_shape=jax.ShapeDtypeStruct((rows, n), F32),
        grid=(n // bn,),
        in_specs=[pl.BlockSpec((rows, d), lambda j: (0, 0)),
                  pl.BlockSpec((d, bn), lambda j: (0, j)),
                  pl.BlockSpec((1, bn), lambda j: (0, j))],
        out_specs=pl.BlockSpec((rows, bn), lambda j: (0, j)),
        compiler_params=_cparams(("arbitrary",)),
        name="ada",
    )(cs, w, b)


def _inproj_kernel(x_ref, mod_ref, gain_ref, w_ref, o_ref):
    x = x_ref[0]
    ms = jnp.mean(x * x, axis=-1, keepdims=True)
    y = x * lax.rsqrt(ms + EPS) * gain_ref[...]
    h = y * (1.0 + mod_ref[0, 1:2, :]) + mod_ref[0, 0:1, :]
    o_ref[0] = jnp.dot(h.astype(BF16), w_ref[...], preferred_element_type=F32)


def _inproj(x, mods, gain, w_bf, tm, shared_mod):
    b, s, d = x.shape
    n = w_bf.shape[1]
    mod_map = (lambda bi, i: (0, 0, 0)) if shared_mod else (lambda bi, i: (bi, 0, 0))
    return pl.pallas_call(
        _inproj_kernel,
        out_shape=jax.ShapeDtypeStruct((b, s, n), F32),
        grid=(b, s // tm),
        in_specs=[pl.BlockSpec((1, tm, d), lambda bi, i: (bi, i, 0)),
                  pl.BlockSpec((1, 8, d), mod_map),
                  pl.BlockSpec((1, d), lambda bi, i: (0, 0)),
                  pl.BlockSpec((d, n), lambda bi, i: (0, 0))],
        out_specs=pl.BlockSpec((1, tm, n), lambda bi, i: (bi, i, 0)),
        compiler_params=_cparams(("arbitrary", "arbitrary")),
        name="inproj",
    )(x, mods, gain, w_bf)


LRU_G = 256
LRU_CHUNK = 256


def _seg_rows(s, seg_len):
    return pl.ds(s, seg_len, stride=N_SEG)


def _out_pitch(seg_len):
    return seg_len + SUBLANES


def _lane_store(ref, d, rows, val):
    nl = val.shape[-1] // LANES
    for l in range(nl):
        ref[d * nl + l, rows, :] = val[:, l * LANES:(l + 1) * LANES]


def _lane_load(ref, d, rows, nl):
    return jnp.concatenate([ref[d * nl + l, rows, :] for l in range(nl)], axis=-1)


LRU_CONV_PIECE = 64


def _lru_coeffs(pad_ref, t0, rows, cw, cb, wg_ref, bg, sp, a_ref, b_ref, dst_rows, u_ref):
    for p0 in range(0, rows, LRU_CONV_PIECE):
        win = pad_ref[pl.ds(t0 + p0, LRU_CONV_PIECE + 2 * SUBLANES), :]
        u = cb
        for k in range(CONV_W):
            off = SUBLANES - CONV_PAD_L + k
            u = u + win[off:off + LRU_CONV_PIECE, :] * cw[k:k + 1, :]
        u_ref[p0:p0 + LRU_CONV_PIECE, :] = u
    u = u_ref[0:rows, :]
    ub = u.astype(BF16)
    half_u = 0.5 * u
    for d in range(2):
        tr = jnp.tanh(jnp.dot(ub, wg_ref[2 * d, 0], preferred_element_type=F32)
                      + bg[2 * d:2 * d + 1, :])
        ti = jnp.tanh(jnp.dot(ub, wg_ref[2 * d + 1, 0], preferred_element_type=F32)
                      + bg[2 * d + 1:2 * d + 2, :])
        half_c = (-0.5 * LRU_C) * sp[d:d + 1, :]
        log_a = half_c * tr + half_c
        a = jnp.exp(log_a)
        one_minus_a2 = -jnp.tanh(log_a) * (1.0 + a * a)
        root = jnp.where(one_minus_a2 > 0.0, one_minus_a2 * lax.rsqrt(one_minus_a2), 0.0)
        bb = root * (ti * half_u + half_u)
        _lane_store(a_ref, d, dst_rows, a)
        _lane_store(b_ref, d, dst_rows, bb)


def _seg_scan(a_ref, b_ref, h_ref, p_ref, seg_len, unroll):
    n_lead = a_ref.shape[0]
    nl = n_lead // 2
    zero = jnp.zeros((N_SEG, LANES), F32)
    one = jnp.ones((N_SEG, LANES), F32)
    init = tuple((zero, one) for _ in range(n_lead))

    def step(t, carry):
        out = []
        for i in range(n_lead):
            h, p = carry[i]
            tt = t if i < nl else seg_len - 1 - t
            rows = pl.ds(pl.multiple_of(tt * N_SEG, N_SEG), N_SEG)
            a = a_ref[i, rows, :]
            h = a * h + b_ref[i, rows, :]
            p = p * a
            if h_ref is not None:
                out_rows = pl.ds(tt, N_SEG, stride=_out_pitch(seg_len))
                h_ref[i, out_rows, :] = h
                p_ref[i, out_rows, :] = p
            out.append((h, p))
        return tuple(out)

    def body(i, carry):
        for j in range(unroll):
            carry = step(i * unroll + j, carry)
        return carry

    ends = lax.fori_loop(0, seg_len // unroll, body, init)
    h_end = [jnp.concatenate([ends[d * nl + l][0] for l in range(nl)], axis=-1) for d in range(2)]
    p_end = [jnp.concatenate([ends[d * nl + l][1] for l in range(nl)], axis=-1) for d in range(2)]
    return h_end, p_end


def _seg_carries(h_end, p_end, h0, reverse):
    order = range(N_SEG - 1, -1, -1) if reverse else range(N_SEG)
    cin = [None] * N_SEG
    c = h0
    for s in order:
        cin[s] = c
        c = p_end[s:s + 1, :] * c + h_end[s:s + 1, :]
    return cin, c


def _lru_kernel(rx_ref, rg_ref, rc_ref, cw_ref, cb_ref, wg_ref, bg_ref, lam_ref, o_ref,
                padl, padc, u_s, a_l, b_l, h_l, p_l, a_c, b_c):
    s_len = rx_ref.shape[1]
    c_len = rc_ref.shape[1]
    g = rx_ref.shape[2]
    nl = g // LANES
    seg_l = s_len // N_SEG
    seg_c = c_len // N_SEG

    zeros = jnp.zeros((SUBLANES, g), F32)
    padl[0:SUBLANES, :] = zeros
    padl[SUBLANES + s_len:2 * SUBLANES + s_len, :] = zeros
    padc[0:SUBLANES, :] = zeros
    padc[SUBLANES + c_len:2 * SUBLANES + c_len, :] = zeros
    for i in range(s_len // LRU_CHUNK):
        padl[SUBLANES + i * LRU_CHUNK:SUBLANES + (i + 1) * LRU_CHUNK, :] = (
            rx_ref[0, i * LRU_CHUNK:(i + 1) * LRU_CHUNK, :])
    padc[SUBLANES:SUBLANES + c_len, :] = rc_ref[0]

    cw = cw_ref[...]
    cb = cb_ref[...]
    bg = bg_ref[...]
    x = -lam_ref[...]
    sp = jnp.maximum(x, 0.0) + jnp.log(1.0 + jnp.exp(-jnp.abs(x)))

    _lru_coeffs(padc, 0, c_len, cw, cb, wg_ref, bg, sp, a_l, b_l, pl.ds(0, c_len), u_s)
    for i in range(2 * nl):
        for s in range(N_SEG):
            a_c[i, _seg_rows(s, seg_c), :] = a_l[i, s * seg_c:(s + 1) * seg_c, :]
            b_c[i, _seg_rows(s, seg_c), :] = b_l[i, s * seg_c:(s + 1) * seg_c, :]
    h_end, p_end = _seg_scan(a_c, b_c, None, None, seg_c, unroll=4)
    zero_row = jnp.zeros((1, g), F32)
    _, h0_f = _seg_carries(h_end[0], p_end[0], zero_row, reverse=False)
    _, h0_b = _seg_carries(h_end[1], p_end[1], zero_row, reverse=True)

    def coeff_body(s, carry):
        t0 = pl.multiple_of(s * seg_l, SUBLANES)
        _lru_coeffs(padl, t0, seg_l, cw, cb, wg_ref, bg, sp, a_l, b_l, _seg_rows(s, seg_l), u_s)
        return carry

    lax.fori_loop(0, N_SEG, coeff_body, 0)
    h_end, p_end = _seg_scan(a_l, b_l, h_l, p_l, seg_l, unroll=4)
    cin_f, _ = _seg_carries(h_end[0], p_end[0], h0_f, reverse=False)
    cin_b, _ = _seg_carries(h_end[1], p_end[1], h0_b, reverse=True)

    for s in range(N_SEG):
        rows = pl.ds(s * _out_pitch(seg_l), seg_l)
        h = (_lane_load(h_l, 0, rows, nl) + _lane_load(p_l, 0, rows, nl) * cin_f[s]) + (
            _lane_load(h_l, 1, rows, nl) + _lane_load(p_l, 1, rows, nl) * cin_b[s])
        o_ref[0, s * seg_l:(s + 1) * seg_l, :] = _gelu_tanh(rg_ref[0, s * seg_l:(s + 1) * seg_l, :]) * h


def _lru(projx, projc, cw, cb, wg, bg, lam):
    b, s_len, _ = projx.shape
    c_len = projc.shape[1]
    w = cw.shape[1]
    g = LRU_G
    ng = w // g
    seg_l = s_len // N_SEG
    seg_c = c_len // N_SEG
    assert seg_l == LRU_CHUNK and c_len % (N_SEG * 1) == 0
    return pl.pallas_call(
        _lru_kernel,
        out_shape=jax.ShapeDtypeStruct((b, s_len, w), F32),
        grid=(b, ng),
        in_specs=[pl.BlockSpec((1, s_len, g), lambda bi, j: (bi, 0, j)),
                  pl.BlockSpec((1, s_len, g), lambda bi, j: (bi, 0, ng + j)),
                  pl.BlockSpec((1, c_len, g), lambda bi, j: (bi, 0, j)),
                  pl.BlockSpec((CONV_W, g), lambda bi, j: (0, j)),
                  pl.BlockSpec((1, g), lambda bi, j: (0, j)),
                  pl.BlockSpec((4, 1, g, g), lambda bi, j: (0, j, 0, 0)),
                  pl.BlockSpec((4, g), lambda bi, j: (0, j)),
                  pl.BlockSpec((2, g), lambda bi, j: (0, j))],
        out_specs=pl.BlockSpec((1, s_len, g), lambda bi, j: (bi, 0, j)),
        scratch_shapes=[pltpu.VMEM((s_len + 2 * SUBLANES, g), F32),
                        pltpu.VMEM((c_len + 2 * SUBLANES, g), F32),
                        pltpu.VMEM((max(seg_l, c_len), g), F32),
                        pltpu.VMEM((2 * g // LANES, s_len, LANES), F32),
                        pltpu.VMEM((2 * g // LANES, s_len, LANES), F32),
                        pltpu.VMEM((2 * g // LANES, N_SEG * _out_pitch(seg_l), LANES), F32),
                        pltpu.VMEM((2 * g // LANES, N_SEG * _out_pitch(seg_l), LANES), F32),
                        pltpu.VMEM((2 * g // LANES, c_len, LANES), F32),
                        pltpu.VMEM((2 * g // LANES, c_len, LANES), F32)],
        compiler_params=_cparams(("arbitrary", "arbitrary")),
        name="lru",
    )(projx, projx, projc, cw, cb, wg, bg, lam)


LOG2_E = 1.4426950408889634
HG_PITCH_L = GRID_W + SUBLANES
HG_PITCH_C = SUBLANES


def _hg_gates(half_z, lb):
    c = 0.5 * (1.0 - lb)
    ct = c * jnp.tanh(half_z)
    logf = jnp.log(0.5 * (1.0 + lb) + ct)
    k = c - ct
    return logf, k


def _hg_prepass(load, width, pitch, lb, qd, kd, kl, vs, dec):
    n_pos = HG_CHUNK
    for d, zname in enumerate(("zf", "zb")):
        order = range(n_pos) if d == 0 else range(n_pos - 1, -1, -1)
        g = jnp.zeros((width, LANES), F32)
        lbd = lb[d:d + 1, :]
        for p in order:
            logf, k = _hg_gates(load(zname, p), lbd)
            g = g + logf
            qd[d, p * pitch:p * pitch + width, :] = g
            kl[d, p * pitch:p * pitch + width, :] = k
        g_last = g * LOG2_E
        dec[d, 0:width, :] = jnp.exp2(g_last)
        for p in range(n_pos):
            rows = slice(p * pitch, p * pitch + width)
            gp = qd[d, rows, :] * LOG2_E
            k = kl[d, rows, :]
            qd[d, rows, :] = load("q", p) * jnp.exp2(gp)
            kd[d, rows, :] = k * jnp.exp2(-gp)
            kl[d, rows, :] = k * jnp.exp2(g_last - gp)
    for p in range(n_pos):
        vs[p * pitch:p * pitch + width, :] = load("v", p)


HG_GROUP = 4
_NT = (((1,), (1,)), ((), ()))
_TN = (((0,), (0,)), ((), ()))


def _hg_group_rows(ref, lead, c0, pitch):
    parts = []
    for j in range(HG_GROUP):
        rows = pl.ds(c0 + j, HG_CHUNK, stride=pitch)
        parts.append(ref[rows, :] if lead is None else ref[lead, rows, :])
    return jnp.concatenate(parts, axis=0)


def _hg_scores(g, pitch, qd, kd, masks, sc_ref):
    c0 = g * HG_GROUP
    total = None
    for d in range(2):
        q = _hg_group_rows(qd, d, c0, pitch).astype(BF16)
        k = _hg_group_rows(kd, d, c0, pitch).astype(BF16)
        sc = jnp.where(masks[d], lax.dot_general(q, k, _NT, preferred_element_type=F32), 0.0)
        total = sc if total is None else total + sc
    sc_ref[g] = total.astype(BF16)


def _hg_summaries(g, pitch, kl, vs, sc_ref, oi, kv):
    c0 = g * HG_GROUP
    v = _hg_group_rows(vs, None, c0, pitch).astype(BF16)
    if oi is not None:
        o = jnp.dot(sc_ref[g], v, preferred_element_type=F32)
        for j in range(HG_GROUP):
            oi[pl.ds(c0 + j, HG_CHUNK, stride=pitch), :] = o[j * HG_CHUNK:(j + 1) * HG_CHUNK, :]
    kls = jnp.concatenate([_hg_group_rows(kl, 0, c0, pitch), _hg_group_rows(kl, 1, c0, pitch)],
                          axis=1).astype(BF16)
    for j in range(HG_GROUP):
        sl = slice(j * HG_CHUNK, (j + 1) * HG_CHUNK)
        both = lax.dot_general(v[sl], kls[sl], _TN, preferred_element_type=F32)
        kv[0, c0 + j] = both[:, :LANES]
        kv[1, c0 + j] = both[:, LANES:]


def _hg_steps(step0, n_steps, n_chunks, pitch, qd, dec, kv, states, ox):
    states = list(states)
    for u in range(n_steps):
        cs = (step0 + u, n_chunks - 1 - (step0 + u))
        if ox is not None:
            rows = [pl.ds(c, HG_CHUNK, stride=pitch) for c in cs]
            q2 = jnp.concatenate([qd[0, rows[0], :], qd[1, rows[1], :]], axis=0).astype(BF16)
            s2 = jnp.concatenate(states, axis=0).astype(BF16)
            o2 = lax.dot_general(q2, s2, _NT, preferred_element_type=F32)
            ox[0, rows[0], :] = o2[:HG_CHUNK, :LANES]
            ox[1, rows[1], :] = o2[HG_CHUNK:, LANES:]
        for d in range(2):
            states[d] = states[d] * dec[d, pl.ds(cs[d], 1), :] + kv[d, cs[d]]
    return tuple(states)


def _hgrn_kernel(q_ref, v_ref, zf_ref, zb_ref, g_ref, qc_ref, vc_ref, zfc_ref, zbc_ref,
                 lbl_ref, gain_ref, o_ref, qd, kd, kl, vs, dec, oi, ox, kv, sc):
    s_len = q_ref.shape[1]
    c_len = qc_ref.shape[1]
    n_col = s_len // HG_CHUNK
    n_cc = c_len // HG_CHUNK
    assert n_col == GRID_W and n_cc == HG_PITCH_C
    assert n_col % HG_GROUP == 0 and n_cc % HG_GROUP == 0

    lg = lbl_ref[...]
    m = jnp.max(lg, axis=1, keepdims=True)
    ex = jnp.exp(lg - m)
    lb = ex[:, 0, :] / jnp.sum(ex, axis=1)

    gr = HG_GROUP * HG_CHUNK
    ii = lax.broadcasted_iota(I32, (gr, gr), 0)
    jj = lax.broadcasted_iota(I32, (gr, gr), 1)
    same = (ii // HG_CHUNK) == (jj // HG_CHUNK)
    masks = (jnp.logical_and(same, jj <= ii), jnp.logical_and(same, jj >= ii))

    c_refs = {"q": qc_ref, "v": vc_ref, "zf": zfc_ref, "zb": zbc_ref}

    def load_c(name, p):
        return c_refs[name][0, pl.ds(p, n_cc, stride=HG_CHUNK), :]

    _hg_prepass(load_c, n_cc, HG_PITCH_C, lb, qd, kd, kl, vs, dec)
    for g in range(n_cc // HG_GROUP):
        _hg_summaries(g, HG_PITCH_C, kl, vs, None, None, kv)
    zero = jnp.zeros((LANES, LANES), F32)
    states = _hg_steps(0, n_cc, n_cc, HG_PITCH_C, qd, dec, kv, (zero, zero), None)

    l_refs = {"q": q_ref, "v": v_ref, "zf": zf_ref, "zb": zb_ref}

    def load_l(name, p):
        return l_refs[name][0, p * n_col:(p + 1) * n_col, :]

    _hg_prepass(load_l, n_col, HG_PITCH_L, lb, qd, kd, kl, vs, dec)

    for g in range(n_col // HG_GROUP):
        _hg_scores(g, HG_PITCH_L, qd, kd, masks, sc)
    for g in range(n_col // HG_GROUP):
        _hg_summaries(g, HG_PITCH_L, kl, vs, sc, oi, kv)
    _hg_steps(0, n_col, n_col, HG_PITCH_L, qd, dec, kv, states, ox)

    gain = gain_ref[...]
    for r in range(HG_CHUNK):
        rows = slice(r * HG_PITCH_L, r * HG_PITCH_L + n_col)
        o = oi[rows, :] + (ox[0, rows, :] + ox[1, rows, :])
        y = o * lax.rsqrt(jnp.mean(o * o, axis=-1, keepdims=True) + EPS) * gain
        o_ref[0, r * n_col:(r + 1) * n_col, :] = y * _silu(g_ref[0, r * n_col:(r + 1) * n_col, :])


def _hgrn(projx, projc, lbl, gain, col0):
    b, s_len, _ = projx.shape
    c_len = projc.shape[1]
    hw = gain.shape[1]
    nh = hw // LANES
    n_slots = lbl.shape[1]

    def xs(k):
        return pl.BlockSpec((1, s_len, LANES), lambda bi, h, k=k: (bi, 0, col0 + k * nh + h))

    def cs(k):
        return pl.BlockSpec((1, c_len, LANES), lambda bi, h, k=k: (bi, 0, col0 + k * nh + h))

    sc_rows = HG_CHUNK * HG_PITCH_L
    return pl.pallas_call(
        _hgrn_kernel,
        out_shape=jax.ShapeDtypeStruct((b, s_len, hw), F32),
        grid=(b, nh),
        in_specs=[xs(0), xs(1), xs(2), xs(3), xs(4), cs(0), cs(1), cs(2), cs(3),
                  pl.BlockSpec((2, n_slots, LANES), lambda bi, h: (0, 0, h)),
                  pl.BlockSpec((1, LANES), lambda bi, h: (0, h))],
        out_specs=pl.BlockSpec((1, s_len, LANES), lambda bi, h: (bi, 0, h)),
        scratch_shapes=[pltpu.VMEM((2, sc_rows, LANES), F32),
                        pltpu.VMEM((2, sc_rows, LANES), F32),
                        pltpu.VMEM((2, sc_rows, LANES), F32),
                        pltpu.VMEM((sc_rows, LANES), F32),
                        pltpu.VMEM((2, HG_PITCH_L, LANES), F32),
                        pltpu.VMEM((sc_rows, LANES), F32),
                        pltpu.VMEM((2, sc_rows, LANES), F32),
                        pltpu.VMEM((2, GRID_W, LANES, LANES), F32),
                        pltpu.VMEM((GRID_W // HG_GROUP, HG_GROUP * HG_CHUNK,
                                    HG_GROUP * HG_CHUNK), BF16)],
        compiler_params=_cparams(("arbitrary", "arbitrary")),
        name="hgrn",
    )(projx, projx, projx, projx, projx, projc, projc, projc, projc, lbl, gain)


def _first_index_of_max(vals, iota, big):
    m = jnp.max(vals, axis=0, keepdims=True)
    idx = jnp.min(jnp.where(vals == m, iota, big), axis=0, keepdims=True)
    return m, idx


def _route(logits_t, rb, n_experts):
    t = logits_t.shape[1]
    gsz = n_experts // N_GROUPS
    neg = -jnp.inf
    scores = _sigmoid(logits_t)
    biased = scores + rb
    iota_g = lax.broadcasted_iota(I32, (gsz, t), 0)
    gscore = []
    for gi in range(N_GROUPS):
        blk = biased[gi * gsz:(gi + 1) * gsz, :]
        m1, i1 = _first_index_of_max(blk, iota_g, gsz)
        m2 = jnp.max(jnp.where(iota_g == i1, neg, blk), axis=0, keepdims=True)
        gscore.append(m1 + m2)
    gs = jnp.concatenate(gscore, axis=0)
    iota_n = lax.broadcasted_iota(I32, (N_GROUPS, t), 0)
    gsel = jnp.zeros((N_GROUPS, t), jnp.bool_)
    for _ in range(TOPK_GROUPS):
        _, gi1 = _first_index_of_max(gs, iota_n, N_GROUPS)
        hit = iota_n == gi1
        gsel = jnp.logical_or(gsel, hit)
        gs = jnp.where(hit, neg, gs)
    emask = jnp.concatenate(
        [jnp.broadcast_to(gsel[gi:gi + 1, :], (gsz, t)) for gi in range(N_GROUPS)], axis=0)
    masked = jnp.where(emask, biased, neg)
    iota_e = lax.broadcasted_iota(I32, (n_experts, t), 0)
    ids, gates = [], []
    sel = jnp.zeros((n_experts, t), F32)
    for _ in range(TOP_K):
        _, ei = _first_index_of_max(masked, iota_e, n_experts)
        hit = iota_e == ei
        ids.append(ei)
        gates.append(jnp.sum(jnp.where(hit, scores, 0.0), axis=0, keepdims=True))
        sel = jnp.where(hit, 1.0, sel)
        masked = jnp.where(hit, neg, masked)
    ids = jnp.concatenate(ids, axis=0)
    gates = jnp.concatenate(gates, axis=0)
    gates = gates / jnp.sum(gates, axis=0, keepdims=True) * ROUTE_SCALE
    return ids, gates, sel


def _outproj_kernel(yl_ref, yh_ref, x_ref, mod_ref, gain_ref, wo_ref, rwt_ref, rb_ref,
                    x1_ref, h2_ref, eid_ref, gate_ref, rank_ref, cnt_ref, carry):
    first = jnp.logical_and(pl.program_id(0) == 0, pl.program_id(1) == 0)

    @pl.when(first)
    def _():
        carry[...] = jnp.zeros(carry.shape, F32)

    y = jnp.dot(yl_ref[0].astype(BF16), wo_ref[0], preferred_element_type=F32)
    y = y + jnp.dot(yh_ref[0].astype(BF16), wo_ref[1], preferred_element_type=F32)
    x1 = x_ref[0] + mod_ref[0, 2:3, :] * y
    x1_ref[0] = x1
    ms = jnp.mean(x1 * x1, axis=-1, keepdims=True)
    h2 = x1 * lax.rsqrt(ms + EPS) * gain_ref[...]
    h2 = h2 * (1.0 + mod_ref[0, 4:5, :]) + mod_ref[0, 3:4, :]
    h2_ref[...] = _pack_rows(h2)

    n_experts = rwt_ref.shape[0]
    t = h2.shape[0]
    logits_t = lax.dot_general(rwt_ref[...], h2.astype(BF16), _NT,
                               preferred_element_type=F32)
    ids, gates, sel = _route(logits_t, rb_ref[...], n_experts)
    eid_ref[...] = ids
    gate_ref[...] = gates

    ti = lax.broadcasted_iota(I32, (t, t), 0)
    tj = lax.broadcasted_iota(I32, (t, t), 1)
    upper = (ti < tj).astype(BF16)
    selb = sel.astype(BF16)
    before = jnp.dot(selb, upper, preferred_element_type=F32) + carry[:, 0:1]
    iota_e = lax.broadcasted_iota(I32, (n_experts, t), 0)
    ranks = [jnp.sum(jnp.where(iota_e == ids[k:k + 1, :], before, 0.0), axis=0, keepdims=True)
             for k in range(TOP_K)]
    rank_ref[...] = jnp.concatenate(ranks, axis=0).astype(I32)
    total = carry[...] + jnp.dot(selb, jnp.ones((t, LANES), BF16), preferred_element_type=F32)
    carry[...] = total
    cnt_ref[...] = total.astype(I32)


def _outproj(ylru, yhg, x, mods, gain, wo, rwt, rb, tm):
    b, s, d = x.shape
    hw = ylru.shape[2]
    e = rwt.shape[0]
    n = b * s
    nt = s // tm
    tok = lambda bi, i: (0, bi * nt + i)
    return pl.pallas_call(
        _outproj_kernel,
        out_shape=(jax.ShapeDtypeStruct((b, s, d), F32),
                   jax.ShapeDtypeStruct((n, d // 2), U32),
                   jax.ShapeDtypeStruct((TOP_K, n), I32),
                   jax.ShapeDtypeStruct((TOP_K, n), F32),
                   jax.ShapeDtypeStruct((TOP_K, n), I32),
                   jax.ShapeDtypeStruct((e, LANES), I32)),
        grid=(b, nt),
        in_specs=[pl.BlockSpec((1, tm, hw), lambda bi, i: (bi, i, 0)),
                  pl.BlockSpec((1, tm, hw), lambda bi, i: (bi, i, 0)),
                  pl.BlockSpec((1, tm, d), lambda bi, i: (bi, i, 0)),
                  pl.BlockSpec((1, 8, d), lambda bi, i: (bi, 0, 0)),
                  pl.BlockSpec((1, d), lambda bi, i: (0, 0)),
                  pl.BlockSpec((2, hw, d), lambda bi, i: (0, 0, 0)),
                  pl.BlockSpec((e, d), lambda bi, i: (0, 0)),
                  pl.BlockSpec((e, 1), lambda bi, i: (0, 0))],
        out_specs=(pl.BlockSpec((1, tm, d), lambda bi, i: (bi, i, 0)),
                   pl.BlockSpec((tm, d // 2), lambda bi, i: (bi * nt + i, 0)),
                   pl.BlockSpec((TOP_K, tm), tok),
                   pl.BlockSpec((TOP_K, tm), tok),
                   pl.BlockSpec((TOP_K, tm), tok),
                   pl.BlockSpec((e, LANES), lambda bi, i: (0, 0))),
        scratch_shapes=[pltpu.VMEM((e, LANES), F32)],
        compiler_params=_cparams(("arbitrary", "arbitrary")),
        name="outproj",
    )(ylru, yhg, x, mods, gain, wo, rwt, rb)


def _pos_kernel(pstart, eid_ref, rank_ref, pos_ref):
    eid = eid_ref[...]

    def body(e, acc):
        return jnp.where(eid == e, pstart[e], acc)

    base = lax.fori_loop(0, pstart.shape[0], body, jnp.zeros(eid.shape, I32))
    pos_ref[...] = base + rank_ref[...]


def _pos(pstarts, eid_t, rank_t, tn):
    k, n = eid_t.shape
    spec = pl.BlockSpec((k, tn), lambda i, ps: (0, i))
    return pl.pallas_call(
        _pos_kernel,
        out_shape=jax.ShapeDtypeStruct((k, n), I32),
        grid_spec=pltpu.PrefetchScalarGridSpec(
            num_scalar_prefetch=1, grid=(n // tn,), in_specs=[spec, spec], out_specs=spec),
        compiler_params=_cparams(("arbitrary",)),
        name="pos",
    )(pstarts, eid_t, rank_t)


def _pad_fill(padstart, padlen, zbuf, xs_ref, sem, wait):
    nbits = zbuf.shape[0].bit_length() - 1
    low_bits = SUBLANES.bit_length() - 1

    def go(cp):
        if wait:
            cp.wait()
        else:
            cp.start()

    def body(e, carry):
        st = padstart[e]
        ln = padlen[e]
        end = st + ln
        for bit in range(nbits - 1, low_bits - 1, -1):
            size = 1 << bit
            back = ((ln >> (bit + 1)) << (bit + 1)) + size

            @pl.when((ln & size) != 0)
            def _():
                start = pl.multiple_of(end - back, SUBLANES)
                go(pltpu.make_async_copy(zbuf.at[pl.ds(0, size)],
                                         xs_ref.at[pl.ds(start, size)], sem))
        for j in range(SUBLANES - 1):
            @pl.when(j < (ln & (SUBLANES - 1)))
            def _():
                go(pltpu.make_async_copy(zbuf.at[pl.ds(0, 1)], xs_ref.at[pl.ds(st + j, 1)], sem))
        return carry

    lax.fori_loop(0, padstart.shape[0], body, 0)


def _padfill_kernel(padstart, padlen, xs_in, xs_ref, zbuf, sem):
    del xs_in
    zbuf[...] = jnp.zeros(zbuf.shape, zbuf.dtype)
    _pad_fill(padstart, padlen, zbuf, xs_ref, sem, wait=False)
    _pad_fill(padstart, padlen, zbuf, xs_ref, sem, wait=True)


def _padfill(xs, padstart, padlen, bm):
    return pl.pallas_call(
        _padfill_kernel,
        out_shape=jax.ShapeDtypeStruct(xs.shape, xs.dtype),
        grid_spec=pltpu.PrefetchScalarGridSpec(
            num_scalar_prefetch=2,
            grid=(1,),
            in_specs=[pl.BlockSpec(memory_space=pl.ANY)],
            out_specs=pl.BlockSpec(memory_space=pl.ANY),
            scratch_shapes=[pltpu.VMEM((bm, xs.shape[1]), xs.dtype), pltpu.SemaphoreType.DMA(())]),
        input_output_aliases={2: 0},
        compiler_params=_cparams(("arbitrary",)),
        name="padfill",
    )(padstart, padlen, xs)


SC_SCATTER_WINDOW = 128


def _sc_mesh():
    return plsc.VectorSubcoreMesh(core_axis_name="core", subcore_axis_name="subcore")


def _sc_scatter_rows(rows, pos, cap):
    n, dw = rows.shape
    top_k = pos.shape[0]
    mesh = _sc_mesh()
    n_workers = mesh.num_cores * mesh.num_subcores
    win = SC_SCATTER_WINDOW
    per_w = n // n_workers
    n_chunks = per_w // win
    assert per_w * n_workers == n and n_chunks * win == per_w
    pos_w = pos.reshape(top_k, n_workers, n_chunks, win).transpose(1, 2, 0, 3)
    pos_w = pos_w.reshape(n_workers, n_chunks * top_k, win)

    @functools.partial(pl.kernel, mesh=mesh,
                       out_type=jax.ShapeDtypeStruct((cap, dw), rows.dtype),
                       scratch_types=[pltpu.VMEM((n_chunks * top_k, win), I32),
                                      pltpu.VMEM((win, dw), rows.dtype),
                                      pltpu.SemaphoreType.DMA])
    def scatter(r_hbm, p_hbm, o_hbm, idx_v, rows_v, sem):
        wid = lax.axis_index("subcore") * mesh.num_cores + lax.axis_index("core")
        pltpu.sync_copy(p_hbm.at[wid], idx_v)

        @pl.loop(0, n_chunks)
        def _(c):
            pltpu.sync_copy(r_hbm.at[pl.ds(wid * per_w + c * win, win)], rows_v)
            copies = [pltpu.async_copy(rows_v, o_hbm.at[idx_v.at[c * top_k + k]], sem)
                      for k in range(top_k)]
            for cp in copies:
                cp.wait()

    return scatter(rows, pos_w)


GMM_EXPERTS = 4
GMM_PAIR = 4
GMM_SLOTS = 16
GMM_AHEAD = GMM_SLOTS - GMM_PAIR


def _gmm_rows(xw, w13b, w2b, de):
    xb = _unpack_rows(xw).astype(BF16)
    u = jnp.dot(xb, w13b[...], preferred_element_type=F32)
    hmid = _silu(u[:, :de]) * u[:, de:]
    return _pack_rows(jnp.dot(hmid.astype(BF16), w2b[...], preferred_element_type=F32))


def _gmm_kernel(blk0, nblk, w13_ref, w2_ref, xs_ref, ys_ref, w13b, w2b, xbuf, ybuf, isem, osem):
    step = pl.program_id(0)
    n_e = nblk.shape[0]
    slots, bm = xbuf.shape[0], xbuf.shape[1]
    de = w2_ref.shape[1]
    total = blk0[n_e - 1] + nblk[n_e - 1]

    def x_copy(g):
        rows = pl.ds(pl.multiple_of(g * bm, bm), bm)
        slot = g & (slots - 1)
        return pltpu.make_async_copy(xs_ref.at[rows], xbuf.at[slot], isem.at[slot])

    def y_copy(g):
        rows = pl.ds(pl.multiple_of(g * bm, bm), bm)
        slot = g & (slots - 1)
        return pltpu.make_async_copy(ybuf.at[slot], ys_ref.at[rows], osem.at[slot])

    @pl.when(step == 0)
    def _():
        for j in range(GMM_AHEAD):
            @pl.when(j < total)
            def _():
                x_copy(j).start()

    def run_blocks(g0, count):
        gs = [g0 + j for j in range(count)]
        for g in gs:
            x_copy(g).wait()
        for g in gs:
            @pl.when(g + GMM_AHEAD < total)
            def _():
                x_copy(g + GMM_AHEAD).start()

            @pl.when(g >= slots)
            def _():
                y_copy(g - slots).wait()
        for g in gs:
            slot = g & (slots - 1)
            ybuf[slot] = _gmm_rows(xbuf[slot], w13b, w2b, de)
        for g in gs:
            y_copy(g).start()

    def expert_body(j, carry):
        e = step * GMM_EXPERTS + j
        nb = nblk[e]
        b0 = blk0[e]
        w13b[...] = w13_ref[j].astype(BF16)
        w2b[...] = w2_ref[j].astype(BF16)

        def pair_body(i, c):
            run_blocks(b0 + GMM_PAIR * i, GMM_PAIR)
            return c

        lax.fori_loop(0, nb // GMM_PAIR, pair_body, 0)
        for r in range(1, GMM_PAIR):
            @pl.when(nb % GMM_PAIR == r)
            def _():
                run_blocks(b0 + nb - r, r)
        return carry

    lax.fori_loop(0, GMM_EXPERTS, expert_body, 0)

    @pl.when(step == pl.num_programs(0) - 1)
    def _():
        for j in range(1, slots + 1):
            @pl.when(total - j >= 0)
            def _():
                y_copy(total - j).wait()


def _gmm(xs, w13, w2, blk0, nblk, bm):
    cap, dw = xs.shape
    e, d, de2 = w13.shape
    de = w2.shape[1]
    return pl.pallas_call(
        _gmm_kernel,
        out_shape=jax.ShapeDtypeStruct((cap, dw), U32),
        grid_spec=pltpu.PrefetchScalarGridSpec(
            num_scalar_prefetch=2,
            grid=(e // GMM_EXPERTS,),
            in_specs=[pl.BlockSpec((GMM_EXPERTS, d, de2), lambda i, b0, nb: (i, 0, 0)),
                      pl.BlockSpec((GMM_EXPERTS, de, d), lambda i, b0, nb: (i, 0, 0)),
                      pl.BlockSpec(memory_space=pl.ANY)],
            out_specs=pl.BlockSpec(memory_space=pl.ANY),
            scratch_shapes=[pltpu.VMEM((d, de2), BF16), pltpu.VMEM((de, d), BF16),
                            pltpu.VMEM((GMM_SLOTS, bm, dw), U32),
                            pltpu.VMEM((GMM_SLOTS, bm, dw), U32),
                            pltpu.SemaphoreType.DMA((GMM_SLOTS,)),
                            pltpu.SemaphoreType.DMA((GMM_SLOTS,))]),
        compiler_params=_cparams(("arbitrary",)),
        name="gmm",
    )(blk0, nblk, w13, w2, xs)


def _sc_gather_rows(table, idx, window):
    n_idx = idx.shape[0]
    dw = table.shape[1]
    mesh = _sc_mesh()
    n_workers = mesh.num_cores * mesh.num_subcores
    per_w = n_idx // n_workers
    n_chunks = per_w // window
    assert per_w * n_workers == n_idx and n_chunks * window == per_w and n_chunks % 2 == 0

    @functools.partial(pl.kernel, mesh=mesh,
                       out_type=jax.ShapeDtypeStruct((n_idx, dw), table.dtype),
                       scratch_types=[pltpu.VMEM((per_w,), I32),
                                      pltpu.VMEM((window, dw), table.dtype),
                                      pltpu.VMEM((window, dw), table.dtype),
                                      pltpu.SemaphoreType.DMA, pltpu.SemaphoreType.DMA,
                                      pltpu.SemaphoreType.DMA, pltpu.SemaphoreType.DMA])
    def gather(x_hbm, i_hbm, o_hbm, idx_v, rows0, rows1, gs0, gs1, os0, os1):
        wid = lax.axis_index("subcore") * mesh.num_cores + lax.axis_index("core")
        base = wid * per_w
        pltpu.sync_copy(i_hbm.at[pl.ds(base, per_w)], idx_v)
        bufs = ((rows0, gs0, os0), (rows1, gs1, os1))

        def fetch(j, slot):
            rows, gs, _ = bufs[slot]
            off = pl.multiple_of(j * window, window)
            return pltpu.async_copy(x_hbm.at[idx_v.at[pl.ds(off, window)]], rows, gs)

        def fetch_wait(slot):
            rows, gs, _ = bufs[slot]
            pltpu.make_async_copy(x_hbm.at[idx_v.at[pl.ds(0, window)]], rows, gs).wait()

        def put(j, slot):
            rows, _, osem = bufs[slot]
            off = pl.multiple_of(j * window, window)
            return pltpu.async_copy(rows, o_hbm.at[pl.ds(base + off, window)], osem)

        def put_wait(slot):
            rows, _, osem = bufs[slot]
            pltpu.make_async_copy(rows, o_hbm.at[pl.ds(base, window)], osem).wait()

        fetch(0, 0)

        @pl.loop(0, n_chunks, step=2)
        def _(j):
            @pl.when(j > 0)
            def _():
                put_wait(1)
            fetch(j + 1, 1)
            fetch_wait(0)
            put(j, 0)
            put_wait(0)

            @pl.when(j + 2 < n_chunks)
            def _():
                fetch(j + 2, 0)
            fetch_wait(1)
            put(j + 1, 1)

        put_wait(1)

    return gather(table, idx)


def _combine_kernel(gate_ref, x1_ref, h2_ref, mod_ref, gain_ref, sw13_ref, sw2_ref, yg_ref, o_ref):
    ds_ = sw2_ref.shape[0]
    hb = _unpack_rows(h2_ref[...]).astype(BF16)
    u = jnp.dot(hb, sw13_ref[...], preferred_element_type=F32)
    hmid = _silu(u[:, :ds_]) * u[:, ds_:]
    y = jnp.dot(hmid.astype(BF16), sw2_ref[...], preferred_element_type=F32)

    gate = gate_ref[...]
    moe = gate[:, 0:1] * _unpack_rows(yg_ref[0])
    for k in range(1, TOP_K):
        moe = moe + gate[:, k:k + 1] * _unpack_rows(yg_ref[k])
    xo = x1_ref[...] + mod_ref[0, 5:6, :] * (moe + y)
    ms = jnp.mean(xo * xo, axis=-1, keepdims=True)
    o_ref[...] = xo * lax.rsqrt(ms + EPS) * gain_ref[...]


def _combine(gate_tok, x1, h2, mods, gain, sw13, sw2, yg, tc, tiles_per_batch):
    n, d = x1.shape
    dw = h2.shape[1]
    nt = n // tc
    ds2 = sw13.shape[1]
    ds_ = sw2.shape[0]
    return pl.pallas_call(
        _combine_kernel,
        out_shape=jax.ShapeDtypeStruct((n, d), F32),
        grid=(nt,),
        in_specs=[pl.BlockSpec((tc, TOP_K), lambda i: (i, 0)),
                  pl.BlockSpec((tc, d), lambda i: (i, 0)),
                  pl.BlockSpec((tc, dw), lambda i: (i, 0)),
                  pl.BlockSpec((1, 8, d), lambda i: (i // tiles_per_batch, 0, 0)),
                  pl.BlockSpec((1, d), lambda i: (0, 0)),
                  pl.BlockSpec((d, ds2), lambda i: (0, 0)),
                  pl.BlockSpec((ds_, d), lambda i: (0, 0)),
                  pl.BlockSpec((TOP_K, tc, dw), lambda i: (0, i, 0))],
        out_specs=pl.BlockSpec((tc, d), lambda i: (i, 0)),
        compiler_params=_cparams(("arbitrary",)),
        name="combine",
    )(gate_tok, x1, h2, mods, gain, sw13, sw2, yg)


def _prep_w_in(w, lru_w, hg_w):
    z0 = 2 * lru_w + 2 * hg_w
    col = jnp.arange(w.shape[1])
    scale = jnp.where((col >= z0) & (col < z0 + 2 * hg_w), 0.5, 1.0).astype(w.dtype)
    return (w * scale[None, :]).astype(BF16)


def _block_diag_pairs(w, per):
    nb, bd, _ = w.shape
    w = w.reshape(nb // per, per, bd, bd)
    eye = jnp.eye(per, dtype=w.dtype)
    out = jnp.einsum("gpij,pq->gpiqj", w, eye)
    return out.reshape(nb // per, per * bd, per * bd)


def kernel(x, c, ctx, c_ctx, ada_w, ada_b, norm_mix, norm_ffn, norm_final, w_in, w_out,
           lru_conv_w, lru_conv_b, lru_wa, lru_ba, lru_wx, lru_bx, lru_lambda,
           hgrn_lb_logits, hgrn_norm, router_w, router_b, exp_w13, exp_w2, shared_w13, shared_w2):
    assert ada_w.shape[0] == 1, "single-layer block"
    b, s, d = x.shape
    n = b * s
    lru_w = lru_conv_w.shape[2]
    n_experts = router_w.shape[2]

    rows = -(-(b + 1) // SUBLANES) * SUBLANES
    cs = jnp.zeros((rows, d), F32).at[:b].set(c).at[b].set(c_ctx)
    mod = _ada(cs, ada_w[0], ada_b[0][None, :]).reshape(rows, 6, d)
    mods = jnp.pad(mod, ((0, 0), (0, 2), (0, 0)))

    w_in_bf = _prep_w_in(w_in[0], lru_w, hgrn_norm.shape[1])
    gain1 = norm_mix[0][None, :]
    projx = _inproj(x, mods, gain1, w_in_bf, 2 * PROJ_TILE, shared_mod=False)
    projc = _inproj(ctx, mods[b:b + 1], gain1, w_in_bf, ctx.shape[1], shared_mod=True)

    per = LRU_G // (lru_w // LRU_BLOCKS)
    wg = jnp.stack([_block_diag_pairs(lru_wa[0, 0], per), _block_diag_pairs(lru_wx[0, 0], per),
                    _block_diag_pairs(lru_wa[0, 1], per), _block_diag_pairs(lru_wx[0, 1], per)]
                   )
    wg = (0.5 * wg).astype(BF16)
    bg = 0.5 * jnp.stack([lru_ba[0, 0], lru_bx[0, 0], lru_ba[0, 1], lru_bx[0, 1]])
    ylru = _lru(projx, projc, lru_conv_w[0], lru_conv_b[0][None, :], wg, bg, lru_lambda[0])

    yhg = _hgrn(projx, projc, hgrn_lb_logits, hgrn_norm[0][None, :], (2 * lru_w) // LANES)

    wo = w_out[0].astype(BF16).reshape(2, lru_w, d)
    x1, h2, eid_t, gate_t, rank_t, cnt = _outproj(
        ylru, yhg, x, mods, norm_ffn[0][None, :], wo,
        router_w[0].T.astype(BF16), router_b[0][:, None], PROJ_TILE)

    bm = GMM_BLOCK
    counts = cnt[:, 0]
    padded = (counts + bm - 1) // bm * bm
    pends = jnp.cumsum(padded)
    pstarts = pends - padded
    pos = _pos(pstarts, eid_t, rank_t, POS_TILE)
    cap = n * TOP_K + n_experts * bm

    xs = _padfill(_sc_scatter_rows(h2, pos, cap), pstarts + counts, padded - counts, bm)
    ys = _gmm(xs, exp_w13[0], exp_w2[0], pstarts // bm, padded // bm, bm)

    yg = _sc_gather_rows(ys, pos.reshape(-1), SC_GATHER_WINDOW).reshape(TOP_K, n, d // 2)
    tc = COMBINE_TILE
    out = _combine(gate_t.T, x1.reshape(n, d), h2, mods,
                   norm_final[None, :], shared_w13[0].astype(BF16), shared_w2[0].astype(BF16),
                   yg, tc, s // tc)
    return out.reshape(b, s, d)
```

```python
import functools

import jax
import jax.numpy as jnp
from jax import lax
from jax.experimental import pallas as pl
from jax.experimental.pallas import tpu as pltpu
from jax.experimental.pallas import tpu_sc as plsc

F32 = jnp.float32
BF16 = jnp.bfloat16
I32 = jnp.int32
U32 = jnp.uint32

EPS = 1e-6
LRU_C = 8.0
ROUTE_SCALE = 2.5
GRID_W = 64
HG_CHUNK = 32
N_GROUPS = 8
TOPK_GROUPS = 4
TOP_K = 8
LRU_BLOCKS = 8
HG_HEADS = 4
CONV_W = 4
CONV_PAD_L = 1

SUBLANES = 8
LANES = 128
N_SEG = SUBLANES
VMEM_LIMIT = 56 * 1024 * 1024

PROJ_TILE = 512
POS_TILE = 2048
GMM_BLOCK = 256
SC_GATHER_WINDOW = 64
COMBINE_TILE = 512


def _cparams(sem, vmem=VMEM_LIMIT):
    return pltpu.CompilerParams(dimension_semantics=sem, vmem_limit_bytes=vmem)


def _sigmoid(x):
    return jax.nn.sigmoid(x)


def _silu(x):
    return x * _sigmoid(x)


def _pack_rows(x):
    w = x.shape[1] // 2
    bits = pltpu.bitcast(x.astype(BF16).astype(F32), U32)
    return (bits[:, :w] >> 16) | (bits[:, w:] & jnp.uint32(0xFFFF0000))


def _unpack_rows(p):
    lo = pltpu.bitcast(p << 16, F32)
    hi = pltpu.bitcast(p & jnp.uint32(0xFFFF0000), F32)
    return jnp.concatenate([lo, hi], axis=1)


def _gelu_tanh(x):
    c = 0.7978845608028654
    return 0.5 * x * (1.0 + jnp.tanh(c * (x + 0.044715 * (x * x * x))))


def _ada_kernel(c_ref, w_ref, b_ref, o_ref):
    s = _silu(c_ref[...])
    o_ref[...] = jnp.dot(s, w_ref[...], preferred_element_type=F32,
                         precision=lax.Precision.HIGHEST) + b_ref[...]


def _ada(cs, w, b):
    rows, d = cs.shape
    n = w.shape[1]
    bn = 1024
    return pl.pallas_call(
        _ada_kernel,
        out_shape=jax.ShapeDtypeStruct((rows, n), F32),
        grid=(n // bn,),
        in_specs=[pl.BlockSpec((rows, d), lambda j: (0, 0)),
                  pl.BlockSpec((d, bn), lambda j: (0, j)),
                  pl.BlockSpec((1, bn), lambda j: (0, j))],
        out_specs=pl.BlockSpec((rows, bn), lambda j: (0, j)),
        compiler_params=_cparams(("arbitrary",)),
        name="ada",
    )(cs, w, b)


def _inproj_kernel(x_ref, mod_ref, gain_ref, w_ref, o_ref):
    x = x_ref[0]
    ms = jnp.mean(x * x, axis=-1, keepdims=True)
    y = x * lax.rsqrt(ms + EPS) * gain_ref[...]
    h = y * (1.0 + mod_ref[0, 1:2, :]) + mod_ref[0, 0:1, :]
    o_ref[0] = jnp.dot(h.astype(BF16), w_ref[...], preferred_element_type=F32)


def _inproj(x, mods, gain, w_bf, tm, shared_mod):
    b, s, d = x.shape
    n = w_bf.shape[1]
    mod_map = (lambda bi, i: (0, 0, 0)) if shared_mod else (lambda bi, i: (bi, 0, 0))
    return pl.pallas_call(
        _inproj_kernel,
        out_shape=jax.ShapeDtypeStruct((b, s, n), F32),
        grid=(b, s // tm),
        in_specs=[pl.BlockSpec((1, tm, d), lambda bi, i: (bi, i, 0)),
                  pl.BlockSpec((1, 8, d), mod_map),
                  pl.BlockSpec((1, d), lambda bi, i: (0, 0)),
                  pl.BlockSpec((d, n), lambda bi, i: (0, 0))],
        out_specs=pl.BlockSpec((1, tm, n), lambda bi, i: (bi, i, 0)),
        compiler_params=_cparams(("arbitrary", "arbitrary")),
        name="inproj",
    )(x, mods, gain, w_bf)


LRU_G = 256
LRU_CHUNK = 256


def _seg_rows(s, seg_len):
    return pl.ds(s, seg_len, stride=N_SEG)


def _out_pitch(seg_len):
    return seg_len + SUBLANES


def _lane_store(ref, d, rows, val):
    nl = val.shape[-1] // LANES
    for l in range(nl):
        ref[d * nl + l, rows, :] = val[:, l * LANES:(l + 1) * LANES]


def _lane_load(ref, d, rows, nl):
    return jnp.concatenate([ref[d * nl + l, rows, :] for l in range(nl)], axis=-1)


LRU_CONV_PIECE = 64


def _lru_coeffs(pad_ref, t0, rows, cw, cb, wg_ref, bg, sp, a_ref, b_ref, dst_rows, u_ref):
    for p0 in range(0, rows, LRU_CONV_PIECE):
        win = pad_ref[pl.ds(t0 + p0, LRU_CONV_PIECE + 2 * SUBLANES), :]
        u = cb
        for k in range(CONV_W):
            off = SUBLANES - CONV_PAD_L + k
            u = u + win[off:off + LRU_CONV_PIECE, :] * cw[k:k + 1, :]
        u_ref[p0:p0 + LRU_CONV_PIECE, :] = u
    u = u_ref[0:rows, :]
    ub = u.astype(BF16)
    half_u = 0.5 * u
    for d in range(2):
        tr = jnp.tanh(jnp.dot(ub, wg_ref[2 * d, 0], preferred_element_type=F32)
                      + bg[2 * d:2 * d + 1, :])
        ti = jnp.tanh(jnp.dot(ub, wg_ref[2 * d + 1, 0], preferred_element_type=F32)
                      + bg[2 * d + 1:2 * d + 2, :])
        half_c = (-0.5 * LRU_C) * sp[d:d + 1, :]
        log_a = half_c * tr + half_c
        a = jnp.exp(log_a)
        one_minus_a2 = -jnp.tanh(log_a) * (1.0 + a * a)
        root = jnp.where(one_minus_a2 > 0.0, one_minus_a2 * lax.rsqrt(one_minus_a2), 0.0)
        bb = root * (ti * half_u + half_u)
        _lane_store(a_ref, d, dst_rows, a)
        _lane_store(b_ref, d, dst_rows, bb)


def _seg_scan(a_ref, b_ref, h_ref, p_ref, seg_len, unroll):
    n_lead = a_ref.shape[0]
    nl = n_lead // 2
    zero = jnp.zeros((N_SEG, LANES), F32)
    one = jnp.ones((N_SEG, LANES), F32)
    init = tuple((zero, one) for _ in range(n_lead))

    def step(t, carry):
        out = []
        for i in range(n_lead):
            h, p = carry[i]
            tt = t if i < nl else seg_len - 1 - t
            rows = pl.ds(pl.multiple_of(tt * N_SEG, N_SEG), N_SEG)
            a = a_ref[i, rows, :]
            h = a * h + b_ref[i, rows, :]
            p = p * a
            if h_ref is not None:
                out_rows = pl.ds(tt, N_SEG, stride=_out_pitch(seg_len))
                h_ref[i, out_rows, :] = h
                p_ref[i, out_rows, :] = p
            out.append((h, p))
        return tuple(out)

    def body(i, carry):
        for j in range(unroll):
            carry = step(i * unroll + j, carry)
        return carry

    ends = lax.fori_loop(0, seg_len // unroll, body, init)
    h_end = [jnp.concatenate([ends[d * nl + l][0] for l in range(nl)], axis=-1) for d in range(2)]
    p_end = [jnp.concatenate([ends[d * nl + l][1] for l in range(nl)], axis=-1) for d in range(2)]
    return h_end, p_end


def _seg_carries(h_end, p_end, h0, reverse):
    order = range(N_SEG - 1, -1, -1) if reverse else range(N_SEG)
    cin = [None] * N_SEG
    c = h0
    for s in order:
        cin[s] = c
        c = p_end[s:s + 1, :] * c + h_end[s:s + 1, :]
    return cin, c


def _lru_kernel(rx_ref, rg_ref, rc_ref, cw_ref, cb_ref, wg_ref, bg_ref, lam_ref, o_ref,
                padl, padc, u_s, a_l, b_l, h_l, p_l, a_c, b_c):
    s_len = rx_ref.shape[1]
    c_len = rc_ref.shape[1]
    g = rx_ref.shape[2]
    nl = g // LANES
    seg_l = s_len // N_SEG
    seg_c = c_len // N_SEG

    zeros = jnp.zeros((SUBLANES, g), F32)
    padl[0:SUBLANES, :] = zeros
    padl[SUBLANES + s_len:2 * SUBLANES + s_len, :] = zeros
    padc[0:SUBLANES, :] = zeros
    padc[SUBLANES + c_len:2 * SUBLANES + c_len, :] = zeros
    for i in range(s_len // LRU_CHUNK):
        padl[SUBLANES + i * LRU_CHUNK:SUBLANES + (i + 1) * LRU_CHUNK, :] = (
            rx_ref[0, i * LRU_CHUNK:(i + 1) * LRU_CHUNK, :])
    padc[SUBLANES:SUBLANES + c_len, :] = rc_ref[0]

    cw = cw_ref[...]
    cb = cb_ref[...]
    bg = bg_ref[...]
    x = -lam_ref[...]
    sp = jnp.maximum(x, 0.0) + jnp.log(1.0 + jnp.exp(-jnp.abs(x)))

    _lru_coeffs(padc, 0, c_len, cw, cb, wg_ref, bg, sp, a_l, b_l, pl.ds(0, c_len), u_s)
    for i in range(2 * nl):
        for s in range(N_SEG):
            a_c[i, _seg_rows(s, seg_c), :] = a_l[i, s * seg_c:(s + 1) * seg_c, :]
            b_c[i, _seg_rows(s, seg_c), :] = b_l[i, s * seg_c:(s + 1) * seg_c, :]
    h_end, p_end = _seg_scan(a_c, b_c, None, None, seg_c, unroll=4)
    zero_row = jnp.zeros((1, g), F32)
    _, h0_f = _seg_carries(h_end[0], p_end[0], zero_row, reverse=False)
    _, h0_b = _seg_carries(h_end[1], p_end[1], zero_row, reverse=True)

    def coeff_body(s, carry):
        t0 = pl.multiple_of(s * seg_l, SUBLANES)
        _lru_coeffs(padl, t0, seg_l, cw, cb, wg_ref, bg, sp, a_l, b_l, _seg_rows(s, seg_l), u_s)
        return carry

    lax.fori_loop(0, N_SEG, coeff_body, 0)
    h_end, p_end = _seg_scan(a_l, b_l, h_l, p_l, seg_l, unroll=4)
    cin_f, _ = _seg_carries(h_end[0], p_end[0], h0_f, reverse=False)
    cin_b, _ = _seg_carries(h_end[1], p_end[1], h0_b, reverse=True)

    for s in range(N_SEG):
        rows = pl.ds(s * _out_pitch(seg_l), seg_l)
        h = (_lane_load(h_l, 0, rows, nl) + _lane_load(p_l, 0, rows, nl) * cin_f[s]) + (
            _lane_load(h_l, 1, rows, nl) + _lane_load(p_l, 1, rows, nl) * cin_b[s])
        o_ref[0, s * seg_l:(s + 1) * seg_l, :] = _gelu_tanh(rg_ref[0, s * seg_l:(s + 1) * seg_l, :]) * h


def _lru(projx, projc, cw, cb, wg, bg, lam):
    b, s_len, _ = projx.shape
    c_len = projc.shape[1]
    w = cw.shape[1]
    g = LRU_G
    ng = w // g
    seg_l = s_len // N_SEG
    seg_c = c_len // N_SEG
    assert seg_l == LRU_CHUNK and c_len % (N_SEG * 1) == 0
    return pl.pallas_call(
        _lru_kernel,
        out_shape=jax.ShapeDtypeStruct((b, s_len, w), F32),
        grid=(b, ng),
        in_specs=[pl.BlockSpec((1, s_len, g), lambda bi, j: (bi, 0, j)),
                  pl.BlockSpec((1, s_len, g), lambda bi, j: (bi, 0, ng + j)),
                  pl.BlockSpec((1, c_len, g), lambda bi, j: (bi, 0, j)),
                  pl.BlockSpec((CONV_W, g), lambda bi, j: (0, j)),
                  pl.BlockSpec((1, g), lambda bi, j: (0, j)),
                  pl.BlockSpec((4, 1, g, g), lambda bi, j: (0, j, 0, 0)),
                  pl.BlockSpec((4, g), lambda bi, j: (0, j)),
                  pl.BlockSpec((2, g), lambda bi, j: (0, j))],
        out_specs=pl.BlockSpec((1, s_len, g), lambda bi, j: (bi, 0, j)),
        scratch_shapes=[pltpu.VMEM((s_len + 2 * SUBLANES, g), F32),
                        pltpu.VMEM((c_len + 2 * SUBLANES, g), F32),
                        pltpu.VMEM((max(seg_l, c_len), g), F32),
                        pltpu.VMEM((2 * g // LANES, s_len, LANES), F32),
                        pltpu.VMEM((2 * g // LANES, s_len, LANES), F32),
                        pltpu.VMEM((2 * g // LANES, N_SEG * _out_pitch(seg_l), LANES), F32),
                        pltpu.VMEM((2 * g // LANES, N_SEG * _out_pitch(seg_l), LANES), F32),
                        pltpu.VMEM((2 * g // LANES, c_len, LANES), F32),
                        pltpu.VMEM((2 * g // LANES, c_len, LANES), F32)],
        compiler_params=_cparams(("arbitrary", "arbitrary")),
        name="lru",
    )(projx, projx, projc, cw, cb, wg, bg, lam)


LOG2_E = 1.4426950408889634
HG_PITCH_L = GRID_W + SUBLANES
HG_PITCH_C = SUBLANES


def _hg_gates(half_z, lb):
    c = 0.5 * (1.0 - lb)
    ct = c * jnp.tanh(half_z)
    logf = jnp.log(0.5 * (1.0 + lb) + ct)
    k = c - ct
    return logf, k


def _hg_prepass(load, width, pitch, lb, qd, kd, kl, vs, dec):
    n_pos = HG_CHUNK
    for d, zname in enumerate(("zf", "zb")):
        order = range(n_pos) if d == 0 else range(n_pos - 1, -1, -1)
        g = jnp.zeros((width, LANES), F32)
        lbd = lb[d:d + 1, :]
        for p in order:
            logf, k = _hg_gates(load(zname, p), lbd)
            g = g + logf
            qd[d, p * pitch:p * pitch + width, :] = g
            kl[d, p * pitch:p * pitch + width, :] = k
        g_last = g * LOG2_E
        dec[d, 0:width, :] = jnp.exp2(g_last)
        for p in range(n_pos):
            rows = slice(p * pitch, p * pitch + width)
            gp = qd[d, rows, :] * LOG2_E
            k = kl[d, rows, :]
            qd[d, rows, :] = load("q", p) * jnp.exp2(gp)
            kd[d, rows, :] = k * jnp.exp2(-gp)
            kl[d, rows, :] = k * jnp.exp2(g_last - gp)
    for p in range(n_pos):
        vs[p * pitch:p * pitch + width, :] = load("v", p)


HG_GROUP = 4
_NT = (((1,), (1,)), ((), ()))
_TN = (((0,), (0,)), ((), ()))


def _hg_group_rows(ref, lead, c0, pitch):
    parts = []
    for j in range(HG_GROUP):
        rows = pl.ds(c0 + j, HG_CHUNK, stride=pitch)
        parts.append(ref[rows, :] if lead is None else ref[lead, rows, :])
    return jnp.concatenate(parts, axis=0)


def _hg_scores(g, pitch, qd, kd, masks, sc_ref):
    c0 = g * HG_GROUP
    total = None
    for d in range(2):
        q = _hg_group_rows(qd, d, c0, pitch).astype(BF16)
        k = _hg_group_rows(kd, d, c0, pitch).astype(BF16)
        sc = jnp.where(masks[d], lax.dot_general(q, k, _NT, preferred_element_type=F32), 0.0)
        total = sc if total is None else total + sc
    sc_ref[g] = total.astype(BF16)


def _hg_summaries(g, pitch, kl, vs, sc_ref, oi, kv):
    c0 = g * HG_GROUP
    v = _hg_group_rows(vs, None, c0, pitch).astype(BF16)
    if oi is not None:
        o = jnp.dot(sc_ref[g], v, preferred_element_type=F32)
        for j in range(HG_GROUP):
            oi[pl.ds(c0 + j, HG_CHUNK, stride=pitch), :] = o[j * HG_CHUNK:(j + 1) * HG_CHUNK, :]
    kls = jnp.concatenate([_hg_group_rows(kl, 0, c0, pitch), _hg_group_rows(kl, 1, c0, pitch)],
                          axis=1).astype(BF16)
    for j in range(HG_GROUP):
        sl = slice(j * HG_CHUNK, (j + 1) * HG_CHUNK)
        both = lax.dot_general(v[sl], kls[sl], _TN, preferred_element_type=F32)
        kv[0, c0 + j] = both[:, :LANES]
        kv[1, c0 + j] = both[:, LANES:]


def _hg_steps(step0, n_steps, n_chunks, pitch, qd, dec, kv, states, ox):
    states = list(states)
    for u in range(n_steps):
        cs = (step0 + u, n_chunks - 1 - (step0 + u))
        if ox is not None:
            rows = [pl.ds(c, HG_CHUNK, stride=pitch) for c in cs]
            q2 = jnp.concatenate([qd[0, rows[0], :], qd[1, rows[1], :]], axis=0).astype(BF16)
            s2 = jnp.concatenate(states, axis=0).astype(BF16)
            o2 = lax.dot_general(q2, s2, _NT, preferred_element_type=F32)
            ox[0, rows[0], :] = o2[:HG_CHUNK, :LANES]
            ox[1, rows[1], :] = o2[HG_CHUNK:, LANES:]
        for d in range(2):
            states[d] = states[d] * dec[d, pl.ds(cs[d], 1), :] + kv[d, cs[d]]
    return tuple(states)


def _hgrn_kernel(q_ref, v_ref, zf_ref, zb_ref, g_ref, qc_ref, vc_ref, zfc_ref, zbc_ref,
                 lbl_ref, gain_ref, o_ref, qd, kd, kl, vs, dec, oi, ox, kv, sc):
    s_len = q_ref.shape[1]
    c_len = qc_ref.shape[1]
    n_col = s_len // HG_CHUNK
    n_cc = c_len // HG_CHUNK
    assert n_col == GRID_W and n_cc == HG_PITCH_C
    assert n_col % HG_GROUP == 0 and n_cc % HG_GROUP == 0

    lg = lbl_ref[...]
    m = jnp.max(lg, axis=1, keepdims=True)
    ex = jnp.exp(lg - m)
    lb = ex[:, 0, :] / jnp.sum(ex, axis=1)

    gr = HG_GROUP * HG_CHUNK
    ii = lax.broadcasted_iota(I32, (gr, gr), 0)
    jj = lax.broadcasted_iota(I32, (gr, gr), 1)
    same = (ii // HG_CHUNK) == (jj // HG_CHUNK)
    masks = (jnp.logical_and(same, jj <= ii), jnp.logical_and(same, jj >= ii))

    c_refs = {"q": qc_ref, "v": vc_ref, "zf": zfc_ref, "zb": zbc_ref}

    def load_c(name, p):
        return c_refs[name][0, pl.ds(p, n_cc, stride=HG_CHUNK), :]

    _hg_prepass(load_c, n_cc, HG_PITCH_C, lb, qd, kd, kl, vs, dec)
    for g in range(n_cc // HG_GROUP):
        _hg_summaries(g, HG_PITCH_C, kl, vs, None, None, kv)
    zero = jnp.zeros((LANES, LANES), F32)
    states = _hg_steps(0, n_cc, n_cc, HG_PITCH_C, qd, dec, kv, (zero, zero), None)

    l_refs = {"q": q_ref, "v": v_ref, "zf": zf_ref, "zb": zb_ref}

    def load_l(name, p):
        return l_refs[name][0, p * n_col:(p + 1) * n_col, :]

    _hg_prepass(load_l, n_col, HG_PITCH_L, lb, qd, kd, kl, vs, dec)

    n_groups = n_col // HG_GROUP
    for g in range(n_groups):
        _hg_scores(g, HG_PITCH_L, qd, kd, masks, sc)
        if g > 0:
            _hg_summaries(g - 1, HG_PITCH_L, kl, vs, sc, oi, kv)
    _hg_summaries(n_groups - 1, HG_PITCH_L, kl, vs, sc, oi, kv)
    _hg_steps(0, n_col, n_col, HG_PITCH_L, qd, dec, kv, states, ox)

    gain = gain_ref[...]
    for r in range(HG_CHUNK):
        rows = slice(r * HG_PITCH_L, r * HG_PITCH_L + n_col)
        o = oi[rows, :] + (ox[0, rows, :] + ox[1, rows, :])
        y = o * lax.rsqrt(jnp.mean(o * o, axis=-1, keepdims=True) + EPS) * gain
        o_ref[0, r * n_col:(r + 1) * n_col, :] = y * _silu(g_ref[0, r * n_col:(r + 1) * n_col, :])


def _hgrn(projx, projc, lbl, gain, col0):
    b, s_len, _ = projx.shape
    c_len = projc.shape[1]
    hw = gain.shape[1]
    nh = hw // LANES
    n_slots = lbl.shape[1]

    def xs(k):
        return pl.BlockSpec((1, s_len, LANES), lambda bi, h, k=k: (bi, 0, col0 + k * nh + h))

    def cs(k):
        return pl.BlockSpec((1, c_len, LANES), lambda bi, h, k=k: (bi, 0, col0 + k * nh + h))

    sc_rows = HG_CHUNK * HG_PITCH_L
    return pl.pallas_call(
        _hgrn_kernel,
        out_shape=jax.ShapeDtypeStruct((b, s_len, hw), F32),
        grid=(b, nh),
        in_specs=[xs(0), xs(1), xs(2), xs(3), xs(4), cs(0), cs(1), cs(2), cs(3),
                  pl.BlockSpec((2, n_slots, LANES), lambda bi, h: (0, 0, h)),
                  pl.BlockSpec((1, LANES), lambda bi, h: (0, h))],
        out_specs=pl.BlockSpec((1, s_len, LANES), lambda bi, h: (bi, 0, h)),
        scratch_shapes=[pltpu.VMEM((2, sc_rows, LANES), F32),
                        pltpu.VMEM((2, sc_rows, LANES), F32),
                        pltpu.VMEM((2, sc_rows, LANES), F32),
                        pltpu.VMEM((sc_rows, LANES), F32),
                        pltpu.VMEM((2, HG_PITCH_L, LANES), F32),
                        pltpu.VMEM((sc_rows, LANES), F32),
                        pltpu.VMEM((2, sc_rows, LANES), F32),
                        pltpu.VMEM((2, GRID_W, LANES, LANES), F32),
                        pltpu.VMEM((GRID_W // HG_GROUP, HG_GROUP * HG_CHUNK,
                                    HG_GROUP * HG_CHUNK), BF16)],
        compiler_params=_cparams(("arbitrary", "arbitrary")),
        name="hgrn",
    )(projx, projx, projx, projx, projx, projc, projc, projc, projc, lbl, gain)


def _first_index_of_max(vals, iota, big):
    m = jnp.max(vals, axis=0, keepdims=True)
    idx = jnp.min(jnp.where(vals == m, iota, big), axis=0, keepdims=True)
    return m, idx


def _route(logits_t, rb, n_experts):
    t = logits_t.shape[1]
    gsz = n_experts // N_GROUPS
    neg = -jnp.inf
    scores = _sigmoid(logits_t)
    biased = scores + rb
    iota_g = lax.broadcasted_iota(I32, (gsz, t), 0)
    gscore = []
    for gi in range(N_GROUPS):
        blk = biased[gi * gsz:(gi + 1) * gsz, :]
        m1, i1 = _first_index_of_max(blk, iota_g, gsz)
        m2 = jnp.max(jnp.where(iota_g == i1, neg, blk), axis=0, keepdims=True)
        gscore.append(m1 + m2)
    gs = jnp.concatenate(gscore, axis=0)
    iota_n = lax.broadcasted_iota(I32, (N_GROUPS, t), 0)
    gsel = jnp.zeros((N_GROUPS, t), jnp.bool_)
    for _ in range(TOPK_GROUPS):
        _, gi1 = _first_index_of_max(gs, iota_n, N_GROUPS)
        hit = iota_n == gi1
        gsel = jnp.logical_or(gsel, hit)
        gs = jnp.where(hit, neg, gs)
    emask = jnp.concatenate(
        [jnp.broadcast_to(gsel[gi:gi + 1, :], (gsz, t)) for gi in range(N_GROUPS)], axis=0)
    masked = jnp.where(emask, biased, neg)
    iota_e = lax.broadcasted_iota(I32, (n_experts, t), 0)
    ids, gates = [], []
    sel = jnp.zeros((n_experts, t), F32)
    for _ in range(TOP_K):
        _, ei = _first_index_of_max(masked, iota_e, n_experts)
        hit = iota_e == ei
        ids.append(ei)
        gates.append(jnp.sum(jnp.where(hit, scores, 0.0), axis=0, keepdims=True))
        sel = jnp.where(hit, 1.0, sel)
        masked = jnp.where(hit, neg, masked)
    ids = jnp.concatenate(ids, axis=0)
    gates = jnp.concatenate(gates, axis=0)
    gates = gates / jnp.sum(gates, axis=0, keepdims=True) * ROUTE_SCALE
    return ids, gates, sel


def _outproj_kernel(yl_ref, yh_ref, x_ref, mod_ref, gain_ref, wo_ref, rwt_ref, rb_ref,
                    x1_ref, h2_ref, eid_ref, gate_ref, rank_ref, cnt_ref, carry):
    first = jnp.logical_and(pl.program_id(0) == 0, pl.program_id(1) == 0)

    @pl.when(first)
    def _():
        carry[...] = jnp.zeros(carry.shape, F32)

    y = jnp.dot(yl_ref[0].astype(BF16), wo_ref[0], preferred_element_type=F32)
    y = y + jnp.dot(yh_ref[0].astype(BF16), wo_ref[1], preferred_element_type=F32)
    x1 = x_ref[0] + mod_ref[0, 2:3, :] * y
    x1_ref[0] = x1
    ms = jnp.mean(x1 * x1, axis=-1, keepdims=True)
    h2 = x1 * lax.rsqrt(ms + EPS) * gain_ref[...]
    h2 = h2 * (1.0 + mod_ref[0, 4:5, :]) + mod_ref[0, 3:4, :]
    h2_ref[...] = _pack_rows(h2)

    n_experts = rwt_ref.shape[0]
    t = h2.shape[0]
    logits_t = lax.dot_general(rwt_ref[...], h2.astype(BF16), _NT,
                               preferred_element_type=F32)
    ids, gates, sel = _route(logits_t, rb_ref[...], n_experts)
    eid_ref[...] = ids
    gate_ref[...] = gates

    ti = lax.broadcasted_iota(I32, (t, t), 0)
    tj = lax.broadcasted_iota(I32, (t, t), 1)
    upper = (ti < tj).astype(BF16)
    selb = sel.astype(BF16)
    before = jnp.dot(selb, upper, preferred_element_type=F32) + carry[:, 0:1]
    iota_e = lax.broadcasted_iota(I32, (n_experts, t), 0)
    ranks = [jnp.sum(jnp.where(iota_e == ids[k:k + 1, :], before, 0.0), axis=0, keepdims=True)
             for k in range(TOP_K)]
    rank_ref[...] = jnp.concatenate(ranks, axis=0).astype(I32)
    total = carry[...] + jnp.dot(selb, jnp.ones((t, LANES), BF16), preferred_element_type=F32)
    carry[...] = total
    cnt_ref[...] = total.astype(I32)


def _outproj(ylru, yhg, x, mods, gain, wo, rwt, rb, tm):
    b, s, d = x.shape
    hw = ylru.shape[2]
    e = rwt.shape[0]
    n = b * s
    nt = s // tm
    tok = lambda bi, i: (0, bi * nt + i)
    return pl.pallas_call(
        _outproj_kernel,
        out_shape=(jax.ShapeDtypeStruct((b, s, d), F32),
                   jax.ShapeDtypeStruct((n, d // 2), U32),
                   jax.ShapeDtypeStruct((TOP_K, n), I32),
                   jax.ShapeDtypeStruct((TOP_K, n), F32),
                   jax.ShapeDtypeStruct((TOP_K, n), I32),
                   jax.ShapeDtypeStruct((e, LANES), I32)),
        grid=(b, nt),
        in_specs=[pl.BlockSpec((1, tm, hw), lambda bi, i: (bi, i, 0)),
                  pl.BlockSpec((1, tm, hw), lambda bi, i: (bi, i, 0)),
                  pl.BlockSpec((1, tm, d), lambda bi, i: (bi, i, 0)),
                  pl.BlockSpec((1, 8, d), lambda bi, i: (bi, 0, 0)),
                  pl.BlockSpec((1, d), lambda bi, i: (0, 0)),
                  pl.BlockSpec((2, hw, d), lambda bi, i: (0, 0, 0)),
                  pl.BlockSpec((e, d), lambda bi, i: (0, 0)),
                  pl.BlockSpec((e, 1), lambda bi, i: (0, 0))],
        out_specs=(pl.BlockSpec((1, tm, d), lambda bi, i: (bi, i, 0)),
                   pl.BlockSpec((tm, d // 2), lambda bi, i: (bi * nt + i, 0)),
                   pl.BlockSpec((TOP_K, tm), tok),
                   pl.BlockSpec((TOP_K, tm), tok),
                   pl.BlockSpec((TOP_K, tm), tok),
                   pl.BlockSpec((e, LANES), lambda bi, i: (0, 0))),
        scratch_shapes=[pltpu.VMEM((e, LANES), F32)],
        compiler_params=_cparams(("arbitrary", "arbitrary")),
        name="outproj",
    )(ylru, yhg, x, mods, gain, wo, rwt, rb)


def _pos_kernel(pstart, eid_ref, rank_ref, pos_ref):
    eid = eid_ref[...]

    def body(e, acc):
        return jnp.where(eid == e, pstart[e], acc)

    base = lax.fori_loop(0, pstart.shape[0], body, jnp.zeros(eid.shape, I32))
    pos_ref[...] = base + rank_ref[...]


def _pos(pstarts, eid_t, rank_t, tn):
    k, n = eid_t.shape
    spec = pl.BlockSpec((k, tn), lambda i, ps: (0, i))
    return pl.pallas_call(
        _pos_kernel,
        out_shape=jax.ShapeDtypeStruct((k, n), I32),
        grid_spec=pltpu.PrefetchScalarGridSpec(
            num_scalar_prefetch=1, grid=(n // tn,), in_specs=[spec, spec], out_specs=spec),
        compiler_params=_cparams(("arbitrary",)),
        name="pos",
    )(pstarts, eid_t, rank_t)


def _pad_fill(padstart, padlen, zbuf, xs_ref, sem, wait):
    nbits = zbuf.shape[0].bit_length() - 1
    low_bits = SUBLANES.bit_length() - 1

    def go(cp):
        if wait:
            cp.wait()
        else:
            cp.start()

    def body(e, carry):
        st = padstart[e]
        ln = padlen[e]
        end = st + ln
        for bit in range(nbits - 1, low_bits - 1, -1):
            size = 1 << bit
            back = ((ln >> (bit + 1)) << (bit + 1)) + size

            @pl.when((ln & size) != 0)
            def _():
                start = pl.multiple_of(end - back, SUBLANES)
                go(pltpu.make_async_copy(zbuf.at[pl.ds(0, size)],
                                         xs_ref.at[pl.ds(start, size)], sem))
        for j in range(SUBLANES - 1):
            @pl.when(j < (ln & (SUBLANES - 1)))
            def _():
                go(pltpu.make_async_copy(zbuf.at[pl.ds(0, 1)], xs_ref.at[pl.ds(st + j, 1)], sem))
        return carry

    lax.fori_loop(0, padstart.shape[0], body, 0)


def _padfill_kernel(padstart, padlen, xs_in, xs_ref, zbuf, sem):
    del xs_in
    zbuf[...] = jnp.zeros(zbuf.shape, zbuf.dtype)
    _pad_fill(padstart, padlen, zbuf, xs_ref, sem, wait=False)
    _pad_fill(padstart, padlen, zbuf, xs_ref, sem, wait=True)


def _padfill(xs, padstart, padlen, bm):
    return pl.pallas_call(
        _padfill_kernel,
        out_shape=jax.ShapeDtypeStruct(xs.shape, xs.dtype),
        grid_spec=pltpu.PrefetchScalarGridSpec(
            num_scalar_prefetch=2,
            grid=(1,),
            in_specs=[pl.BlockSpec(memory_space=pl.ANY)],
            out_specs=pl.BlockSpec(memory_space=pl.ANY),
            scratch_shapes=[pltpu.VMEM((bm, xs.shape[1]), xs.dtype), pltpu.SemaphoreType.DMA(())]),
        input_output_aliases={2: 0},
        compiler_params=_cparams(("arbitrary",)),
        name="padfill",
    )(padstart, padlen, xs)


SC_SCATTER_WINDOW = 128


def _sc_mesh():
    return plsc.VectorSubcoreMesh(core_axis_name="core", subcore_axis_name="subcore")


def _sc_scatter_rows(rows, pos, cap):
    n, dw = rows.shape
    top_k = pos.shape[0]
    mesh = _sc_mesh()
    n_workers = mesh.num_cores * mesh.num_subcores
    win = SC_SCATTER_WINDOW
    per_w = n // n_workers
    n_chunks = per_w // win
    assert per_w * n_workers == n and n_chunks * win == per_w
    pos_w = pos.reshape(top_k, n_workers, n_chunks, win).transpose(1, 2, 0, 3)
    pos_w = pos_w.reshape(n_workers, n_chunks * top_k, win)

    @functools.partial(pl.kernel, mesh=mesh,
                       out_type=jax.ShapeDtypeStruct((cap, dw), rows.dtype),
                       scratch_types=[pltpu.VMEM((n_chunks * top_k, win), I32),
                                      pltpu.VMEM((win, dw), rows.dtype),
                                      pltpu.SemaphoreType.DMA])
    def scatter(r_hbm, p_hbm, o_hbm, idx_v, rows_v, sem):
        wid = lax.axis_index("subcore") * mesh.num_cores + lax.axis_index("core")
        pltpu.sync_copy(p_hbm.at[wid], idx_v)

        @pl.loop(0, n_chunks)
        def _(c):
            pltpu.sync_copy(r_hbm.at[pl.ds(wid * per_w + c * win, win)], rows_v)
            copies = [pltpu.async_copy(rows_v, o_hbm.at[idx_v.at[c * top_k + k]], sem)
                      for k in range(top_k)]
            for cp in copies:
                cp.wait()

    return scatter(rows, pos_w)


GMM_EXPERTS = 4
GMM_PAIR = 4
GMM_SLOTS = 16
GMM_AHEAD = GMM_SLOTS - GMM_PAIR


def _gmm_rows(xw, w13b, w2b, de):
    xb = _unpack_rows(xw).astype(BF16)
    u = jnp.dot(xb, w13b[...], preferred_element_type=F32)
    hmid = _silu(u[:, :de]) * u[:, de:]
    return _pack_rows(jnp.dot(hmid.astype(BF16), w2b[...], preferred_element_type=F32))


def _gmm_kernel(blk0, nblk, w13_ref, w2_ref, xs_ref, ys_ref, w13b, w2b, xbuf, ybuf, isem, osem):
    step = pl.program_id(0)
    n_e = nblk.shape[0]
    slots, bm = xbuf.shape[0], xbuf.shape[1]
    de = w2_ref.shape[1]
    total = blk0[n_e - 1] + nblk[n_e - 1]

    def x_copy(g):
        rows = pl.ds(pl.multiple_of(g * bm, bm), bm)
        slot = g & (slots - 1)
        return pltpu.make_async_copy(xs_ref.at[rows], xbuf.at[slot], isem.at[slot])

    def y_copy(g):
        rows = pl.ds(pl.multiple_of(g * bm, bm), bm)
        slot = g & (slots - 1)
        return pltpu.make_async_copy(ybuf.at[slot], ys_ref.at[rows], osem.at[slot])

    @pl.when(step == 0)
    def _():
        for j in range(GMM_AHEAD):
            @pl.when(j < total)
            def _():
                x_copy(j).start()

    def run_blocks(g0, count):
        gs = [g0 + j for j in range(count)]
        for g in gs:
            x_copy(g).wait()
        for g in gs:
            @pl.when(g + GMM_AHEAD < total)
            def _():
                x_copy(g + GMM_AHEAD).start()

            @pl.when(g >= slots)
            def _():
                y_copy(g - slots).wait()
        xbs = [_unpack_rows(xbuf[g & (slots - 1)]).astype(BF16) for g in gs]
        us = [jnp.dot(xb, w13b[...], preferred_element_type=F32) for xb in xbs]
        hs = [(_silu(u[:, :de]) * u[:, de:]).astype(BF16) for u in us]
        ys = [jnp.dot(h, w2b[...], preferred_element_type=F32) for h in hs]
        for g, y in zip(gs, ys):
            ybuf[g & (slots - 1)] = _pack_rows(y)
        for g in gs:
            y_copy(g).start()

    def expert_body(j, carry):
        e = step * GMM_EXPERTS + j
        nb = nblk[e]
        b0 = blk0[e]
        w13b[...] = w13_ref[j].astype(BF16)
        w2b[...] = w2_ref[j].astype(BF16)

        def pair_body(i, c):
            run_blocks(b0 + GMM_PAIR * i, GMM_PAIR)
            return c

        lax.fori_loop(0, nb // GMM_PAIR, pair_body, 0)
        for r in range(1, GMM_PAIR):
            @pl.when(nb % GMM_PAIR == r)
            def _():
                run_blocks(b0 + nb - r, r)
        return carry

    lax.fori_loop(0, GMM_EXPERTS, expert_body, 0)

    @pl.when(step == pl.num_programs(0) - 1)
    def _():
        for j in range(1, slots + 1):
            @pl.when(total - j >= 0)
            def _():
                y_copy(total - j).wait()


def _gmm(xs, w13, w2, blk0, nblk, bm):
    cap, dw = xs.shape
    e, d, de2 = w13.shape
    de = w2.shape[1]
    return pl.pallas_call(
        _gmm_kernel,
        out_shape=jax.ShapeDtypeStruct((cap, dw), U32),
        grid_spec=pltpu.PrefetchScalarGridSpec(
            num_scalar_prefetch=2,
            grid=(e // GMM_EXPERTS,),
            in_specs=[pl.BlockSpec((GMM_EXPERTS, d, de2), lambda i, b0, nb: (i, 0, 0)),
                      pl.BlockSpec((GMM_EXPERTS, de, d), lambda i, b0, nb: (i, 0, 0)),
                      pl.BlockSpec(memory_space=pl.ANY)],
            out_specs=pl.BlockSpec(memory_space=pl.ANY),
            scratch_shapes=[pltpu.VMEM((d, de2), BF16), pltpu.VMEM((de, d), BF16),
                            pltpu.VMEM((GMM_SLOTS, bm, dw), U32),
                            pltpu.VMEM((GMM_SLOTS, bm, dw), U32),
                            pltpu.SemaphoreType.DMA((GMM_SLOTS,)),
                            pltpu.SemaphoreType.DMA((GMM_SLOTS,))]),
        compiler_params=_cparams(("arbitrary",)),
        name="gmm",
    )(blk0, nblk, w13, w2, xs)


def _sc_gather_rows(table, idx, window):
    n_idx = idx.shape[0]
    dw = table.shape[1]
    mesh = _sc_mesh()
    n_workers = mesh.num_cores * mesh.num_subcores
    per_w = n_idx // n_workers
    n_chunks = per_w // window
    assert per_w * n_workers == n_idx and n_chunks * window == per_w and n_chunks % 2 == 0

    @functools.partial(pl.kernel, mesh=mesh,
                       out_type=jax.ShapeDtypeStruct((n_idx, dw), table.dtype),
                       scratch_types=[pltpu.VMEM((per_w,), I32),
                                      pltpu.VMEM((window, dw), table.dtype),
                                      pltpu.VMEM((window, dw), table.dtype),
                                      pltpu.SemaphoreType.DMA, pltpu.SemaphoreType.DMA,
                                      pltpu.SemaphoreType.DMA, pltpu.SemaphoreType.DMA])
    def gather(x_hbm, i_hbm, o_hbm, idx_v, rows0, rows1, gs0, gs1, os0, os1):
        wid = lax.axis_index("subcore") * mesh.num_cores + lax.axis_index("core")
        base = wid * per_w
        pltpu.sync_copy(i_hbm.at[pl.ds(base, per_w)], idx_v)
        bufs = ((rows0, gs0, os0), (rows1, gs1, os1))

        def fetch(j, slot):
            rows, gs, _ = bufs[slot]
            off = pl.multiple_of(j * window, window)
            return pltpu.async_copy(x_hbm.at[idx_v.at[pl.ds(off, window)]], rows, gs)

        def fetch_wait(slot):
            rows, gs, _ = bufs[slot]
            pltpu.make_async_copy(x_hbm.at[idx_v.at[pl.ds(0, window)]], rows, gs).wait()

        def put(j, slot):
            rows, _, osem = bufs[slot]
            off = pl.multiple_of(j * window, window)
            return pltpu.async_copy(rows, o_hbm.at[pl.ds(base + off, window)], osem)

        def put_wait(slot):
            rows, _, osem = bufs[slot]
            pltpu.make_async_copy(rows, o_hbm.at[pl.ds(base, window)], osem).wait()

        fetch(0, 0)

        @pl.loop(0, n_chunks, step=2)
        def _(j):
            @pl.when(j > 0)
            def _():
                put_wait(1)
            fetch(j + 1, 1)
            fetch_wait(0)
            put(j, 0)
            put_wait(0)

            @pl.when(j + 2 < n_chunks)
            def _():
                fetch(j + 2, 0)
            fetch_wait(1)
            put(j + 1, 1)

        put_wait(1)

    return gather(table, idx)


def _combine_kernel(gate_ref, x1_ref, h2_ref, mod_ref, gain_ref, sw13_ref, sw2_ref, yg_ref, o_ref):
    ds_ = sw2_ref.shape[0]
    hb = _unpack_rows(h2_ref[...]).astype(BF16)
    u = jnp.dot(hb, sw13_ref[...], preferred_element_type=F32)
    hmid = _silu(u[:, :ds_]) * u[:, ds_:]
    y = jnp.dot(hmid.astype(BF16), sw2_ref[...], preferred_element_type=F32)

    gate = gate_ref[...]
    moe = gate[:, 0:1] * _unpack_rows(yg_ref[0])
    for k in range(1, TOP_K):
        moe = moe + gate[:, k:k + 1] * _unpack_rows(yg_ref[k])
    xo = x1_ref[...] + mod_ref[0, 5:6, :] * (moe + y)
    ms = jnp.mean(xo * xo, axis=-1, keepdims=True)
    o_ref[...] = xo * lax.rsqrt(ms + EPS) * gain_ref[...]


def _combine(gate_tok, x1, h2, mods, gain, sw13, sw2, yg, tc, tiles_per_batch):
    n, d = x1.shape
    dw = h2.shape[1]
    nt = n // tc
    ds2 = sw13.shape[1]
    ds_ = sw2.shape[0]
    return pl.pallas_call(
        _combine_kernel,
        out_shape=jax.ShapeDtypeStruct((n, d), F32),
        grid=(nt,),
        in_specs=[pl.BlockSpec((tc, TOP_K), lambda i: (i, 0)),
                  pl.BlockSpec((tc, d), lambda i: (i, 0)),
                  pl.BlockSpec((tc, dw), lambda i: (i, 0)),
                  pl.BlockSpec((1, 8, d), lambda i: (i // tiles_per_batch, 0, 0)),
                  pl.BlockSpec((1, d), lambda i: (0, 0)),
                  pl.BlockSpec((d, ds2), lambda i: (0, 0)),
                  pl.BlockSpec((ds_, d), lambda i: (0, 0)),
                  pl.BlockSpec((TOP_K, tc, dw), lambda i: (0, i, 0))],
        out_specs=pl.BlockSpec((tc, d), lambda i: (i, 0)),
        compiler_params=_cparams(("arbitrary",)),
        name="combine",
    )(gate_tok, x1, h2, mods, gain, sw13, sw2, yg)


def _prep_w_in(w, lru_w, hg_w):
    z0 = 2 * lru_w + 2 * hg_w
    col = jnp.arange(w.shape[1])
    scale = jnp.where((col >= z0) & (col < z0 + 2 * hg_w), 0.5, 1.0).astype(w.dtype)
    return (w * scale[None, :]).astype(BF16)


def _block_diag_pairs(w, per):
    nb, bd, _ = w.shape
    w = w.reshape(nb // per, per, bd, bd)
    eye = jnp.eye(per, dtype=w.dtype)
    out = jnp.einsum("gpij,pq->gpiqj", w, eye)
    return out.reshape(nb // per, per * bd, per * bd)


def kernel(x, c, ctx, c_ctx, ada_w, ada_b, norm_mix, norm_ffn, norm_final, w_in, w_out,
           lru_conv_w, lru_conv_b, lru_wa, lru_ba, lru_wx, lru_bx, lru_lambda,
           hgrn_lb_logits, hgrn_norm, router_w, router_b, exp_w13, exp_w2, shared_w13, shared_w2):
    assert ada_w.shape[0] == 1, "single-layer block"
    b, s, d = x.shape
    n = b * s
    lru_w = lru_conv_w.shape[2]
    n_experts = router_w.shape[2]

    rows = -(-(b + 1) // SUBLANES) * SUBLANES
    cs = jnp.zeros((rows, d), F32).at[:b].set(c).at[b].set(c_ctx)
    mod = _ada(cs, ada_w[0], ada_b[0][None, :]).reshape(rows, 6, d)
    mods = jnp.pad(mod, ((0, 0), (0, 2), (0, 0)))

    w_in_bf = _prep_w_in(w_in[0], lru_w, hgrn_norm.shape[1])
    gain1 = norm_mix[0][None, :]
    projx = _inproj(x, mods, gain1, w_in_bf, PROJ_TILE, shared_mod=False)
    projc = _inproj(ctx, mods[b:b + 1], gain1, w_in_bf, ctx.shape[1], shared_mod=True)

    per = LRU_G // (lru_w // LRU_BLOCKS)
    wg = jnp.stack([_block_diag_pairs(lru_wa[0, 0], per), _block_diag_pairs(lru_wx[0, 0], per),
                    _block_diag_pairs(lru_wa[0, 1], per), _block_diag_pairs(lru_wx[0, 1], per)]
                   )
    wg = (0.5 * wg).astype(BF16)
    bg = 0.5 * jnp.stack([lru_ba[0, 0], lru_bx[0, 0], lru_ba[0, 1], lru_bx[0, 1]])
    ylru = _lru(projx, projc, lru_conv_w[0], lru_conv_b[0][None, :], wg, bg, lru_lambda[0])

    yhg = _hgrn(projx, projc, hgrn_lb_logits, hgrn_norm[0][None, :], (2 * lru_w) // LANES)

    wo = w_out[0].astype(BF16).reshape(2, lru_w, d)
    x1, h2, eid_t, gate_t, rank_t, cnt = _outproj(
        ylru, yhg, x, mods, norm_ffn[0][None, :], wo,
        router_w[0].T.astype(BF16), router_b[0][:, None], PROJ_TILE)

    bm = GMM_BLOCK
    counts = cnt[:, 0]
    padded = (counts + bm - 1) // bm * bm
    pends = jnp.cumsum(padded)
    pstarts = pends - padded
    pos = _pos(pstarts, eid_t, rank_t, POS_TILE)
    cap = n * TOP_K + n_experts * bm

    xs = _padfill(_sc_scatter_rows(h2, pos, cap), pstarts + counts, padded - counts, bm)
    ys = _gmm(xs, exp_w13[0], exp_w2[0], pstarts // bm, padded // bm, bm)

    yg = _sc_gather_rows(ys, pos.reshape(-1), SC_GATHER_WINDOW).reshape(TOP_K, n, d // 2)
    tc = COMBINE_TILE
    out = _combine(gate_t.T, x1.reshape(n, d), h2, mods,
                   norm_final[None, :], shared_w13[0].astype(BF16), shared_w2[0].astype(BF16),
                   yg, tc, s // tc)
    return out.reshape(b, s, d)
```

```python
import functools

import jax
import jax.numpy as jnp
from jax import lax
from jax.experimental import pallas as pl
from jax.experimental.pallas import tpu as pltpu
from jax.experimental.pallas import tpu_sc as plsc

F32 = jnp.float32
BF16 = jnp.bfloat16
I32 = jnp.int32
U32 = jnp.uint32

EPS = 1e-6
LRU_C = 8.0
ROUTE_SCALE = 2.5
GRID_W = 64
HG_CHUNK = 32
N_GROUPS = 8
TOPK_GROUPS = 4
TOP_K = 8
LRU_BLOCKS = 8
HG_HEADS = 4
CONV_W = 4
CONV_PAD_L = 1

SUBLANES = 8
LANES = 128
N_SEG = SUBLANES
VMEM_LIMIT = 56 * 1024 * 1024

PROJ_TILE = 512
POS_TILE = 2048
GMM_BLOCK = 256
SC_GATHER_WINDOW = 64
COMBINE_TILE = 512


def _cparams(sem, vmem=VMEM_LIMIT):
    return pltpu.CompilerParams(dimension_semantics=sem, vmem_limit_bytes=vmem)


def _sigmoid(x):
    return jax.nn.sigmoid(x)


def _silu(x):
    return x * _sigmoid(x)


def _pack_rows(x):
    w = x.shape[1] // 2
    bits = pltpu.bitcast(x.astype(BF16).astype(F32), U32)
    return (bits[:, :w] >> 16) | (bits[:, w:] & jnp.uint32(0xFFFF0000))


def _unpack_rows(p):
    lo = pltpu.bitcast(p << 16, F32)
    hi = pltpu.bitcast(p & jnp.uint32(0xFFFF0000), F32)
    return jnp.concatenate([lo, hi], axis=1)


def _gelu_tanh(x):
    c = 0.7978845608028654
    return 0.5 * x * (1.0 + jnp.tanh(c * (x + 0.044715 * (x * x * x))))


def _ada_kernel(c_ref, w_ref, b_ref, o_ref):
    s = _silu(c_ref[...])
    o_ref[...] = jnp.dot(s, w_ref[...], preferred_element_type=F32,
                         precision=lax.Precision.HIGHEST) + b_ref[...]


def _ada(cs, w, b):
    rows, d = cs.shape
    n = w.shape[1]
    bn = 1024
    return pl.pallas_call(
        _ada_kernel,
        out_shape=jax.ShapeDtypeStruct((rows, n), F32),
        grid=(n // bn,),
        in_specs=[pl.BlockSpec((rows, d), lambda j: (0, 0)),
                  pl.BlockSpec((d, bn), lambda j: (0, j)),
                  pl.BlockSpec((1, bn), lambda j: (0, j))],
        out_specs=pl.BlockSpec((rows, bn), lambda j: (0, j)),
        compiler_params=_cparams(("arbitrary",)),
        name="ada",
    )(cs, w, b)


def _inproj_kernel(x_ref, mod_ref, gain_ref, w_ref, o_ref):
    x = x_ref[0]
    ms = jnp.mean(x * x, axis=-1, keepdims=True)
    y = x * lax.rsqrt(ms + EPS) * gain_ref[...]
    h = y * (1.0 + mod_ref[0, 1:2, :]) + mod_ref[0, 0:1, :]
    o_ref[0] = jnp.dot(h.astype(BF16), w_ref[...], preferred_element_type=F32)


def _inproj(x, mods, gain, w_bf, tm, shared_mod):
    b, s, d = x.shape
    n = w_bf.shape[1]
    mod_map = (lambda bi, i: (0, 0, 0)) if shared_mod else (lambda bi, i: (bi, 0, 0))
    return pl.pallas_call(
        _inproj_kernel,
        out_shape=jax.ShapeDtypeStruct((b, s, n), F32),
        grid=(b, s // tm),
        in_specs=[pl.BlockSpec((1, tm, d), lambda bi, i: (bi, i, 0)),
                  pl.BlockSpec((1, 8, d), mod_map),
                  pl.BlockSpec((1, d), lambda bi, i: (0, 0)),
                  pl.BlockSpec((d, n), lambda bi, i: (0, 0))],
        out_specs=pl.BlockSpec((1, tm, n), lambda bi, i: (bi, i, 0)),
        compiler_params=_cparams(("arbitrary", "arbitrary")),
        name="inproj",
    )(x, mods, gain, w_bf)


LRU_G = 256
LRU_CHUNK = 256


def _seg_rows(s, seg_len):
    return pl.ds(s, seg_len, stride=N_SEG)


def _out_pitch(seg_len):
    return seg_len + SUBLANES


def _lane_store(ref, d, rows, val):
    nl = val.shape[-1] // LANES
    for l in range(nl):
        ref[d * nl + l, rows, :] = val[:, l * LANES:(l + 1) * LANES]


def _lane_load(ref, d, rows, nl):
    return jnp.concatenate([ref[d * nl + l, rows, :] for l in range(nl)], axis=-1)


LRU_CONV_PIECE = 64


def _lru_coeffs(pad_ref, t0, rows, cw, cb, wg_ref, bg, sp, a_ref, b_ref, dst_rows, u_ref):
    for p0 in range(0, rows, LRU_CONV_PIECE):
        win = pad_ref[pl.ds(t0 + p0, LRU_CONV_PIECE + 2 * SUBLANES), :]
        u = cb
        for k in range(CONV_W):
            off = SUBLANES - CONV_PAD_L + k
            u = u + win[off:off + LRU_CONV_PIECE, :] * cw[k:k + 1, :]
        u_ref[p0:p0 + LRU_CONV_PIECE, :] = u
    u = u_ref[0:rows, :]
    ub = u.astype(BF16)
    half_u = 0.5 * u
    for d in range(2):
        tr = jnp.tanh(jnp.dot(ub, wg_ref[2 * d, 0], preferred_element_type=F32)
                      + bg[2 * d:2 * d + 1, :])
        ti = jnp.tanh(jnp.dot(ub, wg_ref[2 * d + 1, 0], preferred_element_type=F32)
                      + bg[2 * d + 1:2 * d + 2, :])
        half_c = (-0.5 * LRU_C) * sp[d:d + 1, :]
        log_a = half_c * tr + half_c
        a = jnp.exp(log_a)
        one_minus_a2 = -jnp.tanh(log_a) * (1.0 + a * a)
        root = jnp.where(one_minus_a2 > 0.0, one_minus_a2 * lax.rsqrt(one_minus_a2), 0.0)
        bb = root * (ti * half_u + half_u)
        _lane_store(a_ref, d, dst_rows, a)
        _lane_store(b_ref, d, dst_rows, bb)


def _seg_scan(a_ref, b_ref, h_ref, p_ref, seg_len, unroll):
    n_lead = a_ref.shape[0]
    nl = n_lead // 2
    zero = jnp.zeros((N_SEG, LANES), F32)
    one = jnp.ones((N_SEG, LANES), F32)
    init = tuple((zero, one) for _ in range(n_lead))

    def step(t, carry):
        out = []
        for i in range(n_lead):
            h, p = carry[i]
            tt = t if i < nl else seg_len - 1 - t
            rows = pl.ds(pl.multiple_of(tt * N_SEG, N_SEG), N_SEG)
            a = a_ref[i, rows, :]
            h = a * h + b_ref[i, rows, :]
            p = p * a
            if h_ref is not None:
                out_rows = pl.ds(tt, N_SEG, stride=_out_pitch(seg_len))
                h_ref[i, out_rows, :] = h
                p_ref[i, out_rows, :] = p
            out.append((h, p))
        return tuple(out)

    def body(i, carry):
        for j in range(unroll):
            carry = step(i * unroll + j, carry)
        return carry

    ends = lax.fori_loop(0, seg_len // unroll, body, init)
    h_end = [jnp.concatenate([ends[d * nl + l][0] for l in range(nl)], axis=-1) for d in range(2)]
    p_end = [jnp.concatenate([ends[d * nl + l][1] for l in range(nl)], axis=-1) for d in range(2)]
    return h_end, p_end


def _seg_carries(h_end, p_end, h0, reverse):
    order = range(N_SEG - 1, -1, -1) if reverse else range(N_SEG)
    cin = [None] * N_SEG
    c = h0
    for s in order:
        cin[s] = c
        c = p_end[s:s + 1, :] * c + h_end[s:s + 1, :]
    return cin, c


def _lru_kernel(rx_ref, rg_ref, rc_ref, cw_ref, cb_ref, wg_ref, bg_ref, lam_ref, o_ref,
                padl, padc, u_s, a_l, b_l, h_l, p_l, a_c, b_c):
    s_len = rx_ref.shape[1]
    c_len = rc_ref.shape[1]
    g = rx_ref.shape[2]
    nl = g // LANES
    seg_l = s_len // N_SEG
    seg_c = c_len // N_SEG

    zeros = jnp.zeros((SUBLANES, g), F32)
    padl[0:SUBLANES, :] = zeros
    padl[SUBLANES + s_len:2 * SUBLANES + s_len, :] = zeros
    padc[0:SUBLANES, :] = zeros
    padc[SUBLANES + c_len:2 * SUBLANES + c_len, :] = zeros
    for i in range(s_len // LRU_CHUNK):
        padl[SUBLANES + i * LRU_CHUNK:SUBLANES + (i + 1) * LRU_CHUNK, :] = (
            rx_ref[0, i * LRU_CHUNK:(i + 1) * LRU_CHUNK, :])
    padc[SUBLANES:SUBLANES + c_len, :] = rc_ref[0]

    cw = cw_ref[...]
    cb = cb_ref[...]
    bg = bg_ref[...]
    x = -lam_ref[...]
    sp = jnp.maximum(x, 0.0) + jnp.log(1.0 + jnp.exp(-jnp.abs(x)))

    _lru_coeffs(padc, 0, c_len, cw, cb, wg_ref, bg, sp, a_l, b_l, pl.ds(0, c_len), u_s)
    for i in range(2 * nl):
        for s in range(N_SEG):
            a_c[i, _seg_rows(s, seg_c), :] = a_l[i, s * seg_c:(s + 1) * seg_c, :]
            b_c[i, _seg_rows(s, seg_c), :] = b_l[i, s * seg_c:(s + 1) * seg_c, :]
    h_end, p_end = _seg_scan(a_c, b_c, None, None, seg_c, unroll=4)
    zero_row = jnp.zeros((1, g), F32)
    _, h0_f = _seg_carries(h_end[0], p_end[0], zero_row, reverse=False)
    _, h0_b = _seg_carries(h_end[1], p_end[1], zero_row, reverse=True)

    def coeff_body(s, carry):
        t0 = pl.multiple_of(s * seg_l, SUBLANES)
        _lru_coeffs(padl, t0, seg_l, cw, cb, wg_ref, bg, sp, a_l, b_l, _seg_rows(s, seg_l), u_s)
        return carry

    lax.fori_loop(0, N_SEG, coeff_body, 0)
    h_end, p_end = _seg_scan(a_l, b_l, h_l, p_l, seg_l, unroll=4)
    cin_f, _ = _seg_carries(h_end[0], p_end[0], h0_f, reverse=False)
    cin_b, _ = _seg_carries(h_end[1], p_end[1], h0_b, reverse=True)

    for s in range(N_SEG):
        rows = pl.ds(s * _out_pitch(seg_l), seg_l)
        h = (_lane_load(h_l, 0, rows, nl) + _lane_load(p_l, 0, rows, nl) * cin_f[s]) + (
            _lane_load(h_l, 1, rows, nl) + _lane_load(p_l, 1, rows, nl) * cin_b[s])
        o_ref[0, s * seg_l:(s + 1) * seg_l, :] = _gelu_tanh(rg_ref[0, s * seg_l:(s + 1) * seg_l, :]) * h


def _lru(projx, projc, cw, cb, wg, bg, lam):
    b, s_len, _ = projx.shape
    c_len = projc.shape[1]
    w = cw.shape[1]
    g = LRU_G
    ng = w // g
    seg_l = s_len // N_SEG
    seg_c = c_len // N_SEG
    assert seg_l == LRU_CHUNK and c_len % (N_SEG * 1) == 0
    return pl.pallas_call(
        _lru_kernel,
        out_shape=jax.ShapeDtypeStruct((b, s_len, w), F32),
        grid=(b, ng),
        in_specs=[pl.BlockSpec((1, s_len, g), lambda bi, j: (bi, 0, j)),
                  pl.BlockSpec((1, s_len, g), lambda bi, j: (bi, 0, ng + j)),
                  pl.BlockSpec((1, c_len, g), lambda bi, j: (bi, 0, j)),
                  pl.BlockSpec((CONV_W, g), lambda bi, j: (0, j)),
                  pl.BlockSpec((1, g), lambda bi, j: (0, j)),
                  pl.BlockSpec((4, 1, g, g), lambda bi, j: (0, j, 0, 0)),
                  pl.BlockSpec((4, g), lambda bi, j: (0, j)),
                  pl.BlockSpec((2, g), lambda bi, j: (0, j))],
        out_specs=pl.BlockSpec((1, s_len, g), lambda bi, j: (bi, 0, j)),
        scratch_shapes=[pltpu.VMEM((s_len + 2 * SUBLANES, g), F32),
                        pltpu.VMEM((c_len + 2 * SUBLANES, g), F32),
                        pltpu.VMEM((max(seg_l, c_len), g), F32),
                        pltpu.VMEM((2 * g // LANES, s_len, LANES), F32),
                        pltpu.VMEM((2 * g // LANES, s_len, LANES), F32),
                        pltpu.VMEM((2 * g // LANES, N_SEG * _out_pitch(seg_l), LANES), F32),
                        pltpu.VMEM((2 * g // LANES, N_SEG * _out_pitch(seg_l), LANES), F32),
                        pltpu.VMEM((2 * g // LANES, c_len, LANES), F32),
                        pltpu.VMEM((2 * g // LANES, c_len, LANES), F32)],
        compiler_params=_cparams(("arbitrary", "arbitrary")),
        name="lru",
    )(projx, projx, projc, cw, cb, wg, bg, lam)


LOG2_E = 1.4426950408889634
HG_PITCH_L = GRID_W + SUBLANES
HG_PITCH_C = SUBLANES


def _hg_gates(half_z, lb):
    c = 0.5 * (1.0 - lb)
    ct = c * jnp.tanh(half_z)
    logf = jnp.log(0.5 * (1.0 + lb) + ct)
    k = c - ct
    return logf, k


def _hg_prepass(load, width, pitch, lb, qd, kd, kl, vs, dec):
    n_pos = HG_CHUNK
    for d, zname in enumerate(("zf", "zb")):
        order = range(n_pos) if d == 0 else range(n_pos - 1, -1, -1)
        g = jnp.zeros((width, LANES), F32)
        lbd = lb[d:d + 1, :]
        for p in order:
            logf, k = _hg_gates(load(zname, p), lbd)
            g = g + logf
            qd[d, p * pitch:p * pitch + width, :] = g
            kl[d, p * pitch:p * pitch + width, :] = k
        g_last = g * LOG2_E
        dec[d, 0:width, :] = jnp.exp2(g_last)
        for p in range(n_pos):
            rows = slice(p * pitch, p * pitch + width)
            gp = qd[d, rows, :] * LOG2_E
            k = kl[d, rows, :]
            qd[d, rows, :] = load("q", p) * jnp.exp2(gp)
            kd[d, rows, :] = k * jnp.exp2(-gp)
            kl[d, rows, :] = k * jnp.exp2(g_last - gp)
    for p in range(n_pos):
        vs[p * pitch:p * pitch + width, :] = load("v", p)


HG_GROUP = 4
_NT = (((1,), (1,)), ((), ()))
_TN = (((0,), (0,)), ((), ()))


def _hg_group_rows(ref, lead, c0, pitch):
    parts = []
    for j in range(HG_GROUP):
        rows = pl.ds(c0 + j, HG_CHUNK, stride=pitch)
        parts.append(ref[rows, :] if lead is None else ref[lead, rows, :])
    return jnp.concatenate(parts, axis=0)


def _hg_scores(g, pitch, qd, kd, masks, sc_ref):
    c0 = g * HG_GROUP
    total = None
    for d in range(2):
        q = _hg_group_rows(qd, d, c0, pitch).astype(BF16)
        k = _hg_group_rows(kd, d, c0, pitch).astype(BF16)
        sc = jnp.where(masks[d], lax.dot_general(q, k, _NT, preferred_element_type=F32), 0.0)
        total = sc if total is None else total + sc
    sc_ref[g] = total.astype(BF16)


def _hg_summaries(g, pitch, kl, vs, sc_ref, oi, kv):
    c0 = g * HG_GROUP
    v = _hg_group_rows(vs, None, c0, pitch).astype(BF16)
    if oi is not None:
        o = jnp.dot(sc_ref[g], v, preferred_element_type=F32)
        for j in range(HG_GROUP):
            oi[pl.ds(c0 + j, HG_CHUNK, stride=pitch), :] = o[j * HG_CHUNK:(j + 1) * HG_CHUNK, :]
    kls = jnp.concatenate([_hg_group_rows(kl, 0, c0, pitch), _hg_group_rows(kl, 1, c0, pitch)],
                          axis=1).astype(BF16)
    for j in range(HG_GROUP):
        sl = slice(j * HG_CHUNK, (j + 1) * HG_CHUNK)
        both = lax.dot_general(v[sl], kls[sl], _TN, preferred_element_type=F32)
        kv[0, c0 + j] = both[:, :LANES]
        kv[1, c0 + j] = both[:, LANES:]


def _hg_steps(step0, n_steps, n_chunks, pitch, qd, dec, kv, states, ox):
    states = list(states)
    for u in range(n_steps):
        cs = (step0 + u, n_chunks - 1 - (step0 + u))
        if ox is not None:
            rows = [pl.ds(c, HG_CHUNK, stride=pitch) for c in cs]
            q2 = jnp.concatenate([qd[0, rows[0], :], qd[1, rows[1], :]], axis=0).astype(BF16)
            s2 = jnp.concatenate(states, axis=0).astype(BF16)
            o2 = lax.dot_general(q2, s2, _NT, preferred_element_type=F32)
            ox[0, rows[0], :] = o2[:HG_CHUNK, :LANES]
            ox[1, rows[1], :] = o2[HG_CHUNK:, LANES:]
        for d in range(2):
            states[d] = states[d] * dec[d, pl.ds(cs[d], 1), :] + kv[d, cs[d]]
    return tuple(states)


def _hgrn_kernel(q_ref, v_ref, zf_ref, zb_ref, g_ref, qc_ref, vc_ref, zfc_ref, zbc_ref,
                 lbl_ref, gain_ref, o_ref, qd, kd, kl, vs, dec, oi, ox, kv, sc):
    s_len = q_ref.shape[1]
    c_len = qc_ref.shape[1]
    n_col = s_len // HG_CHUNK
    n_cc = c_len // HG_CHUNK
    assert n_col == GRID_W and n_cc == HG_PITCH_C
    assert n_col % HG_GROUP == 0 and n_cc % HG_GROUP == 0

    lg = lbl_ref[...]
    m = jnp.max(lg, axis=1, keepdims=True)
    ex = jnp.exp(lg - m)
    lb = ex[:, 0, :] / jnp.sum(ex, axis=1)

    gr = HG_GROUP * HG_CHUNK
    ii = lax.broadcasted_iota(I32, (gr, gr), 0)
    jj = lax.broadcasted_iota(I32, (gr, gr), 1)
    same = (ii // HG_CHUNK) == (jj // HG_CHUNK)
    masks = (jnp.logical_and(same, jj <= ii), jnp.logical_and(same, jj >= ii))

    c_refs = {"q": qc_ref, "v": vc_ref, "zf": zfc_ref, "zb": zbc_ref}

    def load_c(name, p):
        return c_refs[name][0, pl.ds(p, n_cc, stride=HG_CHUNK), :]

    _hg_prepass(load_c, n_cc, HG_PITCH_C, lb, qd, kd, kl, vs, dec)
    for g in range(n_cc // HG_GROUP):
        _hg_summaries(g, HG_PITCH_C, kl, vs, None, None, kv)
    zero = jnp.zeros((LANES, LANES), F32)
    states = _hg_steps(0, n_cc, n_cc, HG_PITCH_C, qd, dec, kv, (zero, zero), None)

    l_refs = {"q": q_ref, "v": v_ref, "zf": zf_ref, "zb": zb_ref}

    def load_l(name, p):
        return l_refs[name][0, p * n_col:(p + 1) * n_col, :]

    _hg_prepass(load_l, n_col, HG_PITCH_L, lb, qd, kd, kl, vs, dec)

    n_groups = n_col // HG_GROUP
    for g in range(n_groups):
        _hg_scores(g, HG_PITCH_L, qd, kd, masks, sc)
        if g > 0:
            _hg_summaries(g - 1, HG_PITCH_L, kl, vs, sc, oi, kv)
    _hg_summaries(n_groups - 1, HG_PITCH_L, kl, vs, sc, oi, kv)
    _hg_steps(0, n_col, n_col, HG_PITCH_L, qd, dec, kv, states, ox)

    gain = gain_ref[...]
    for r in range(HG_CHUNK):
        rows = slice(r * HG_PITCH_L, r * HG_PITCH_L + n_col)
        o = oi[rows, :] + (ox[0, rows, :] + ox[1, rows, :])
        y = o * lax.rsqrt(jnp.mean(o * o, axis=-1, keepdims=True) + EPS) * gain
        o_ref[0, r * n_col:(r + 1) * n_col, :] = y * _silu(g_ref[0, r * n_col:(r + 1) * n_col, :])


def _hgrn(projx, projc, lbl, gain, col0):
    b, s_len, _ = projx.shape
    c_len = projc.shape[1]
    hw = gain.shape[1]
    nh = hw // LANES
    n_slots = lbl.shape[1]

    def xs(k):
        return pl.BlockSpec((1, s_len, LANES), lambda bi, h, k=k: (bi, 0, col0 + k * nh + h))

    def cs(k):
        return pl.BlockSpec((1, c_len, LANES), lambda bi, h, k=k: (bi, 0, col0 + k * nh + h))

    sc_rows = HG_CHUNK * HG_PITCH_L
    return pl.pallas_call(
        _hgrn_kernel,
        out_shape=jax.ShapeDtypeStruct((b, s_len, hw), F32),
        grid=(b, nh),
        in_specs=[xs(0), xs(1), xs(2), xs(3), xs(4), cs(0), cs(1), cs(2), cs(3),
                  pl.BlockSpec((2, n_slots, LANES), lambda bi, h: (0, 0, h)),
                  pl.BlockSpec((1, LANES), lambda bi, h: (0, h))],
        out_specs=pl.BlockSpec((1, s_len, LANES), lambda bi, h: (bi, 0, h)),
        scratch_shapes=[pltpu.VMEM((2, sc_rows, LANES), F32),
                        pltpu.VMEM((2, sc_rows, LANES), F32),
                        pltpu.VMEM((2, sc_rows, LANES), F32),
                        pltpu.VMEM((sc_rows, LANES), F32),
                        pltpu.VMEM((2, HG_PITCH_L, LANES), F32),
                        pltpu.VMEM((sc_rows, LANES), F32),
                        pltpu.VMEM((2, sc_rows, LANES), F32),
                        pltpu.VMEM((2, GRID_W, LANES, LANES), F32),
                        pltpu.VMEM((GRID_W // HG_GROUP, HG_GROUP * HG_CHUNK,
                                    HG_GROUP * HG_CHUNK), BF16)],
        compiler_params=_cparams(("arbitrary", "arbitrary")),
        name="hgrn",
    )(projx, projx, projx, projx, projx, projc, projc, projc, projc, lbl, gain)


def _first_index_of_max(vals, iota, big):
    m = jnp.max(vals, axis=0, keepdims=True)
    idx = jnp.min(jnp.where(vals == m, iota, big), axis=0, keepdims=True)
    return m, idx


def _route(logits_t, rb, n_experts):
    t = logits_t.shape[1]
    gsz = n_experts // N_GROUPS
    neg = -jnp.inf
    scores = _sigmoid(logits_t)
    biased = scores + rb
    iota_g = lax.broadcasted_iota(I32, (gsz, t), 0)
    gscore = []
    for gi in range(N_GROUPS):
        blk = biased[gi * gsz:(gi + 1) * gsz, :]
        m1, i1 = _first_index_of_max(blk, iota_g, gsz)
        m2 = jnp.max(jnp.where(iota_g == i1, neg, blk), axis=0, keepdims=True)
        gscore.append(m1 + m2)
    gs = jnp.concatenate(gscore, axis=0)
    iota_n = lax.broadcasted_iota(I32, (N_GROUPS, t), 0)
    gsel = jnp.zeros((N_GROUPS, t), jnp.bool_)
    for _ in range(TOPK_GROUPS):
        _, gi1 = _first_index_of_max(gs, iota_n, N_GROUPS)
        hit = iota_n == gi1
        gsel = jnp.logical_or(gsel, hit)
        gs = jnp.where(hit, neg, gs)
    emask = jnp.concatenate(
        [jnp.broadcast_to(gsel[gi:gi + 1, :], (gsz, t)) for gi in range(N_GROUPS)], axis=0)
    masked = jnp.where(emask, biased, neg)
    iota_e = lax.broadcasted_iota(I32, (n_experts, t), 0)
    ids, gates = [], []
    sel = jnp.zeros((n_experts, t), F32)
    for _ in range(TOP_K):
        _, ei = _first_index_of_max(masked, iota_e, n_experts)
        hit = iota_e == ei
        ids.append(ei)
        gates.append(jnp.sum(jnp.where(hit, scores, 0.0), axis=0, keepdims=True))
        sel = jnp.where(hit, 1.0, sel)
        masked = jnp.where(hit, neg, masked)
    ids = jnp.concatenate(ids, axis=0)
    gates = jnp.concatenate(gates, axis=0)
    gates = gates / jnp.sum(gates, axis=0, keepdims=True) * ROUTE_SCALE
    return ids, gates, sel


def _outproj_kernel(yl_ref, yh_ref, x_ref, mod_ref, gain_ref, wo_ref, rwt_ref, rb_ref,
                    x1_ref, h2_ref, eid_ref, gate_ref, rank_ref, cnt_ref, carry):
    first = jnp.logical_and(pl.program_id(0) == 0, pl.program_id(1) == 0)

    @pl.when(first)
    def _():
        carry[...] = jnp.zeros(carry.shape, F32)

    y = jnp.dot(yl_ref[0].astype(BF16), wo_ref[0], preferred_element_type=F32)
    y = y + jnp.dot(yh_ref[0].astype(BF16), wo_ref[1], preferred_element_type=F32)
    x1 = x_ref[0] + mod_ref[0, 2:3, :] * y
    x1_ref[0] = x1
    ms = jnp.mean(x1 * x1, axis=-1, keepdims=True)
    h2 = x1 * lax.rsqrt(ms + EPS) * gain_ref[...]
    h2 = h2 * (1.0 + mod_ref[0, 4:5, :]) + mod_ref[0, 3:4, :]
    h2_ref[...] = _pack_rows(h2)

    n_experts = rwt_ref.shape[0]
    t = h2.shape[0]
    logits_t = lax.dot_general(rwt_ref[...], h2.astype(BF16), _NT,
                               preferred_element_type=F32)
    ids, gates, sel = _route(logits_t, rb_ref[...], n_experts)
    eid_ref[...] = ids
    gate_ref[...] = gates

    ti = lax.broadcasted_iota(I32, (t, t), 0)
    tj = lax.broadcasted_iota(I32, (t, t), 1)
    upper = (ti < tj).astype(BF16)
    selb = sel.astype(BF16)
    before = jnp.dot(selb, upper, preferred_element_type=F32) + carry[:, 0:1]
    iota_e = lax.broadcasted_iota(I32, (n_experts, t), 0)
    ranks = [jnp.sum(jnp.where(iota_e == ids[k:k + 1, :], before, 0.0), axis=0, keepdims=True)
             for k in range(TOP_K)]
    rank_ref[...] = jnp.concatenate(ranks, axis=0).astype(I32)
    total = carry[...] + jnp.dot(selb, jnp.ones((t, LANES), BF16), preferred_element_type=F32)
    carry[...] = total
    cnt_ref[...] = total.astype(I32)


def _outproj(ylru, yhg, x, mods, gain, wo, rwt, rb, tm):
    b, s, d = x.shape
    hw = ylru.shape[2]
    e = rwt.shape[0]
    n = b * s
    nt = s // tm
    tok = lambda bi, i: (0, bi * nt + i)
    return pl.pallas_call(
        _outproj_kernel,
        out_shape=(jax.ShapeDtypeStruct((b, s, d), F32),
                   jax.ShapeDtypeStruct((n, d // 2), U32),
                   jax.ShapeDtypeStruct((TOP_K, n), I32),
                   jax.ShapeDtypeStruct((TOP_K, n), F32),
                   jax.ShapeDtypeStruct((TOP_K, n), I32),
                   jax.ShapeDtypeStruct((e, LANES), I32)),
        grid=(b, nt),
        in_specs=[pl.BlockSpec((1, tm, hw), lambda bi, i: (bi, i, 0)),
                  pl.BlockSpec((1, tm, hw), lambda bi, i: (bi, i, 0)),
                  pl.BlockSpec((1, tm, d), lambda bi, i: (bi, i, 0)),
                  pl.BlockSpec((1, 8, d), lambda bi, i: (bi, 0, 0)),
                  pl.BlockSpec((1, d), lambda bi, i: (0, 0)),
                  pl.BlockSpec((2, hw, d), lambda bi, i: (0, 0, 0)),
                  pl.BlockSpec((e, d), lambda bi, i: (0, 0)),
                  pl.BlockSpec((e, 1), lambda bi, i: (0, 0))],
        out_specs=(pl.BlockSpec((1, tm, d), lambda bi, i: (bi, i, 0)),
                   pl.BlockSpec((tm, d // 2), lambda bi, i: (bi * nt + i, 0)),
                   pl.BlockSpec((TOP_K, tm), tok),
                   pl.BlockSpec((TOP_K, tm), tok),
                   pl.BlockSpec((TOP_K, tm), tok),
                   pl.BlockSpec((e, LANES), lambda bi, i: (0, 0))),
        scratch_shapes=[pltpu.VMEM((e, LANES), F32)],
        compiler_params=_cparams(("arbitrary", "arbitrary")),
        name="outproj",
    )(ylru, yhg, x, mods, gain, wo, rwt, rb)


def _pos_kernel(pstart, eid_ref, rank_ref, pos_ref):
    eid = eid_ref[...]

    def body(e, acc):
        return jnp.where(eid == e, pstart[e], acc)

    base = lax.fori_loop(0, pstart.shape[0], body, jnp.zeros(eid.shape, I32))
    pos_ref[...] = base + rank_ref[...]


def _pos(pstarts, eid_t, rank_t, tn):
    k, n = eid_t.shape
    spec = pl.BlockSpec((k, tn), lambda i, ps: (0, i))
    return pl.pallas_call(
        _pos_kernel,
        out_shape=jax.ShapeDtypeStruct((k, n), I32),
        grid_spec=pltpu.PrefetchScalarGridSpec(
            num_scalar_prefetch=1, grid=(n // tn,), in_specs=[spec, spec], out_specs=spec),
        compiler_params=_cparams(("arbitrary",)),
        name="pos",
    )(pstarts, eid_t, rank_t)


def _pad_fill(padstart, padlen, zbuf, xs_ref, sem, wait):
    nbits = zbuf.shape[0].bit_length() - 1
    low_bits = SUBLANES.bit_length() - 1

    def go(cp):
        if wait:
            cp.wait()
        else:
            cp.start()

    def body(e, carry):
        st = padstart[e]
        ln = padlen[e]
        end = st + ln
        for bit in range(nbits - 1, low_bits - 1, -1):
            size = 1 << bit
            back = ((ln >> (bit + 1)) << (bit + 1)) + size

            @pl.when((ln & size) != 0)
            def _():
                start = pl.multiple_of(end - back, SUBLANES)
                go(pltpu.make_async_copy(zbuf.at[pl.ds(0, size)],
                                         xs_ref.at[pl.ds(start, size)], sem))
        for j in range(SUBLANES - 1):
            @pl.when(j < (ln & (SUBLANES - 1)))
            def _():
                go(pltpu.make_async_copy(zbuf.at[pl.ds(0, 1)], xs_ref.at[pl.ds(st + j, 1)], sem))
        return carry

    lax.fori_loop(0, padstart.shape[0], body, 0)


def _padfill_kernel(padstart, padlen, xs_in, xs_ref, zbuf, sem):
    del xs_in
    zbuf[...] = jnp.zeros(zbuf.shape, zbuf.dtype)
    _pad_fill(padstart, padlen, zbuf, xs_ref, sem, wait=False)
    _pad_fill(padstart, padlen, zbuf, xs_ref, sem, wait=True)


def _padfill(xs, padstart, padlen, bm):
    return pl.pallas_call(
        _padfill_kernel,
        out_shape=jax.ShapeDtypeStruct(xs.shape, xs.dtype),
        grid_spec=pltpu.PrefetchScalarGridSpec(
            num_scalar_prefetch=2,
            grid=(1,),
            in_specs=[pl.BlockSpec(memory_space=pl.ANY)],
            out_specs=pl.BlockSpec(memory_space=pl.ANY),
            scratch_shapes=[pltpu.VMEM((bm, xs.shape[1]), xs.dtype), pltpu.SemaphoreType.DMA(())]),
        input_output_aliases={2: 0},
        compiler_params=_cparams(("arbitrary",)),
        name="padfill",
    )(padstart, padlen, xs)


SC_SCATTER_WINDOW = 128


def _sc_mesh():
    return plsc.VectorSubcoreMesh(core_axis_name="core", subcore_axis_name="subcore")


def _sc_scatter_rows(rows, pos, cap):
    n, dw = rows.shape
    top_k = pos.shape[0]
    mesh = _sc_mesh()
    n_workers = mesh.num_cores * mesh.num_subcores
    win = SC_SCATTER_WINDOW
    per_w = n // n_workers
    n_chunks = per_w // win
    assert per_w * n_workers == n and n_chunks * win == per_w
    pos_w = pos.reshape(top_k, n_workers, n_chunks, win).transpose(1, 2, 0, 3)
    pos_w = pos_w.reshape(n_workers, n_chunks * top_k, win)

    @functools.partial(pl.kernel, mesh=mesh,
                       out_type=jax.ShapeDtypeStruct((cap, dw), rows.dtype),
                       scratch_types=[pltpu.VMEM((n_chunks * top_k, win), I32),
                                      pltpu.VMEM((win, dw), rows.dtype),
                                      pltpu.SemaphoreType.DMA])
    def scatter(r_hbm, p_hbm, o_hbm, idx_v, rows_v, sem):
        wid = lax.axis_index("subcore") * mesh.num_cores + lax.axis_index("core")
        pltpu.sync_copy(p_hbm.at[wid], idx_v)

        @pl.loop(0, n_chunks)
        def _(c):
            pltpu.sync_copy(r_hbm.at[pl.ds(wid * per_w + c * win, win)], rows_v)
            copies = [pltpu.async_copy(rows_v, o_hbm.at[idx_v.at[c * top_k + k]], sem)
                      for k in range(top_k)]
            for cp in copies:
                cp.wait()

    return scatter(rows, pos_w)


GMM_EXPERTS = 4
GMM_PAIR = 5
GMM_SLOTS = 16
GMM_AHEAD = GMM_SLOTS - GMM_PAIR


def _gmm_rows(xw, w13b, w2b, de):
    xb = _unpack_rows(xw).astype(BF16)
    u = jnp.dot(xb, w13b[...], preferred_element_type=F32)
    hmid = _silu(u[:, :de]) * u[:, de:]
    return _pack_rows(jnp.dot(hmid.astype(BF16), w2b[...], preferred_element_type=F32))


def _gmm_kernel(blk0, nblk, w13_ref, w2_ref, xs_ref, ys_ref, w13b, w2b, xbuf, ybuf, isem, osem):
    step = pl.program_id(0)
    n_e = nblk.shape[0]
    slots, bm = xbuf.shape[0], xbuf.shape[1]
    de = w2_ref.shape[1]
    total = blk0[n_e - 1] + nblk[n_e - 1]

    def x_copy(g):
        rows = pl.ds(pl.multiple_of(g * bm, bm), bm)
        slot = g & (slots - 1)
        return pltpu.make_async_copy(xs_ref.at[rows], xbuf.at[slot], isem.at[slot])

    def y_copy(g):
        rows = pl.ds(pl.multiple_of(g * bm, bm), bm)
        slot = g & (slots - 1)
        return pltpu.make_async_copy(ybuf.at[slot], ys_ref.at[rows], osem.at[slot])

    @pl.when(step == 0)
    def _():
        for j in range(GMM_AHEAD):
            @pl.when(j < total)
            def _():
                x_copy(j).start()

    def run_blocks(g0, count):
        gs = [g0 + j for j in range(count)]
        for g in gs:
            x_copy(g).wait()
        for g in gs:
            @pl.when(g + GMM_AHEAD < total)
            def _():
                x_copy(g + GMM_AHEAD).start()

            @pl.when(g >= slots)
            def _():
                y_copy(g - slots).wait()
        xbs = [_unpack_rows(xbuf[g & (slots - 1)]).astype(BF16) for g in gs]
        us = [jnp.dot(xb, w13b[...], preferred_element_type=F32) for xb in xbs]
        hs = [(_silu(u[:, :de]) * u[:, de:]).astype(BF16) for u in us]
        ys = [jnp.dot(h, w2b[...], preferred_element_type=F32) for h in hs]
        for g, y in zip(gs, ys):
            ybuf[g & (slots - 1)] = _pack_rows(y)
        for g in gs:
            y_copy(g).start()

    def expert_body(j, carry):
        e = step * GMM_EXPERTS + j
        nb = nblk[e]
        b0 = blk0[e]
        w13b[...] = w13_ref[j].astype(BF16)
        w2b[...] = w2_ref[j].astype(BF16)

        def pair_body(i, c):
            run_blocks(b0 + GMM_PAIR * i, GMM_PAIR)
            return c

        lax.fori_loop(0, nb // GMM_PAIR, pair_body, 0)
        for r in range(1, GMM_PAIR):
            @pl.when(nb % GMM_PAIR == r)
            def _():
                run_blocks(b0 + nb - r, r)
        return carry

    lax.fori_loop(0, GMM_EXPERTS, expert_body, 0)

    @pl.when(step == pl.num_programs(0) - 1)
    def _():
        for j in range(1, slots + 1):
            @pl.when(total - j >= 0)
            def _():
                y_copy(total - j).wait()


def _gmm(xs, w13, w2, blk0, nblk, bm):
    cap, dw = xs.shape
    e, d, de2 = w13.shape
    de = w2.shape[1]
    return pl.pallas_call(
        _gmm_kernel,
        out_shape=jax.ShapeDtypeStruct((cap, dw), U32),
        grid_spec=pltpu.PrefetchScalarGridSpec(
            num_scalar_prefetch=2,
            grid=(e // GMM_EXPERTS,),
            in_specs=[pl.BlockSpec((GMM_EXPERTS, d, de2), lambda i, b0, nb: (i, 0, 0)),
                      pl.BlockSpec((GMM_EXPERTS, de, d), lambda i, b0, nb: (i, 0, 0)),
                      pl.BlockSpec(memory_space=pl.ANY)],
            out_specs=pl.BlockSpec(memory_space=pl.ANY),
            scratch_shapes=[pltpu.VMEM((d, de2), BF16), pltpu.VMEM((de, d), BF16),
                            pltpu.VMEM((GMM_SLOTS, bm, dw), U32),
                            pltpu.VMEM((GMM_SLOTS, bm, dw), U32),
                            pltpu.SemaphoreType.DMA((GMM_SLOTS,)),
                            pltpu.SemaphoreType.DMA((GMM_SLOTS,))]),
        compiler_params=_cparams(("arbitrary",)),
        name="gmm",
    )(blk0, nblk, w13, w2, xs)


def _sc_gather_rows(table, idx, window):
    n_idx = idx.shape[0]
    dw = table.shape[1]
    mesh = _sc_mesh()
    n_workers = mesh.num_cores * mesh.num_subcores
    per_w = n_idx // n_workers
    n_chunks = per_w // window
    assert per_w * n_workers == n_idx and n_chunks * window == per_w and n_chunks % 2 == 0

    @functools.partial(pl.kernel, mesh=mesh,
                       out_type=jax.ShapeDtypeStruct((n_idx, dw), table.dtype),
                       scratch_types=[pltpu.VMEM((per_w,), I32),
                                      pltpu.VMEM((window, dw), table.dtype),
                                      pltpu.VMEM((window, dw), table.dtype),
                                      pltpu.SemaphoreType.DMA, pltpu.SemaphoreType.DMA,
                                      pltpu.SemaphoreType.DMA, pltpu.SemaphoreType.DMA])
    def gather(x_hbm, i_hbm, o_hbm, idx_v, rows0, rows1, gs0, gs1, os0, os1):
        wid = lax.axis_index("subcore") * mesh.num_cores + lax.axis_index("core")
        base = wid * per_w
        pltpu.sync_copy(i_hbm.at[pl.ds(base, per_w)], idx_v)
        bufs = ((rows0, gs0, os0), (rows1, gs1, os1))

        def fetch(j, slot):
            rows, gs, _ = bufs[slot]
            off = pl.multiple_of(j * window, window)
            return pltpu.async_copy(x_hbm.at[idx_v.at[pl.ds(off, window)]], rows, gs)

        def fetch_wait(slot):
            rows, gs, _ = bufs[slot]
            pltpu.make_async_copy(x_hbm.at[idx_v.at[pl.ds(0, window)]], rows, gs).wait()

        def put(j, slot):
            rows, _, osem = bufs[slot]
            off = pl.multiple_of(j * window, window)
            return pltpu.async_copy(rows, o_hbm.at[pl.ds(base + off, window)], osem)

        def put_wait(slot):
            rows, _, osem = bufs[slot]
            pltpu.make_async_copy(rows, o_hbm.at[pl.ds(base, window)], osem).wait()

        fetch(0, 0)

        @pl.loop(0, n_chunks, step=2)
        def _(j):
            @pl.when(j > 0)
            def _():
                put_wait(1)
            fetch(j + 1, 1)
            fetch_wait(0)
            put(j, 0)
            put_wait(0)

            @pl.when(j + 2 < n_chunks)
            def _():
                fetch(j + 2, 0)
            fetch_wait(1)
            put(j + 1, 1)

        put_wait(1)

    return gather(table, idx)


def _combine_kernel(gate_ref, x1_ref, h2_ref, mod_ref, gain_ref, sw13_ref, sw2_ref, yg_ref, o_ref):
    ds_ = sw2_ref.shape[0]
    hb = _unpack_rows(h2_ref[...]).astype(BF16)
    u = jnp.dot(hb, sw13_ref[...], preferred_element_type=F32)
    hmid = _silu(u[:, :ds_]) * u[:, ds_:]
    y = jnp.dot(hmid.astype(BF16), sw2_ref[...], preferred_element_type=F32)

    gate = gate_ref[...]
    moe = gate[:, 0:1] * _unpack_rows(yg_ref[0])
    for k in range(1, TOP_K):
        moe = moe + gate[:, k:k + 1] * _unpack_rows(yg_ref[k])
    xo = x1_ref[...] + mod_ref[0, 5:6, :] * (moe + y)
    ms = jnp.mean(xo * xo, axis=-1, keepdims=True)
    o_ref[...] = xo * lax.rsqrt(ms + EPS) * gain_ref[...]


def _combine(gate_tok, x1, h2, mods, gain, sw13, sw2, yg, tc, tiles_per_batch):
    n, d = x1.shape
    dw = h2.shape[1]
    nt = n // tc
    ds2 = sw13.shape[1]
    ds_ = sw2.shape[0]
    return pl.pallas_call(
        _combine_kernel,
        out_shape=jax.ShapeDtypeStruct((n, d), F32),
        grid=(nt,),
        in_specs=[pl.BlockSpec((tc, TOP_K), lambda i: (i, 0)),
                  pl.BlockSpec((tc, d), lambda i: (i, 0)),
                  pl.BlockSpec((tc, dw), lambda i: (i, 0)),
                  pl.BlockSpec((1, 8, d), lambda i: (i // tiles_per_batch, 0, 0)),
                  pl.BlockSpec((1, d), lambda i: (0, 0)),
                  pl.BlockSpec((d, ds2), lambda i: (0, 0)),
                  pl.BlockSpec((ds_, d), lambda i: (0, 0)),
                  pl.BlockSpec((TOP_K, tc, dw), lambda i: (0, i, 0))],
        out_specs=pl.BlockSpec((tc, d), lambda i: (i, 0)),
        compiler_params=_cparams(("arbitrary",)),
        name="combine",
    )(gate_tok, x1, h2, mods, gain, sw13, sw2, yg)


def _prep_w_in(w, lru_w, hg_w):
    z0 = 2 * lru_w + 2 * hg_w
    col = jnp.arange(w.shape[1])
    scale = jnp.where((col >= z0) & (col < z0 + 2 * hg_w), 0.5, 1.0).astype(w.dtype)
    return (w * scale[None, :]).astype(BF16)


def _block_diag_pairs(w, per):
    nb, bd, _ = w.shape
    w = w.reshape(nb // per, per, bd, bd)
    eye = jnp.eye(per, dtype=w.dtype)
    out = jnp.einsum("gpij,pq->gpiqj", w, eye)
    return out.reshape(nb // per, per * bd, per * bd)


def kernel(x, c, ctx, c_ctx, ada_w, ada_b, norm_mix, norm_ffn, norm_final, w_in, w_out,
           lru_conv_w, lru_conv_b, lru_wa, lru_ba, lru_wx, lru_bx, lru_lambda,
           hgrn_lb_logits, hgrn_norm, router_w, router_b, exp_w13, exp_w2, shared_w13, shared_w2):
    assert ada_w.shape[0] == 1, "single-layer block"
    b, s, d = x.shape
    n = b * s
    lru_w = lru_conv_w.shape[2]
    n_experts = router_w.shape[2]

    rows = -(-(b + 1) // SUBLANES) * SUBLANES
    cs = jnp.zeros((rows, d), F32).at[:b].set(c).at[b].set(c_ctx)
    mod = _ada(cs, ada_w[0], ada_b[0][None, :]).reshape(rows, 6, d)
    mods = jnp.pad(mod, ((0, 0), (0, 2), (0, 0)))

    w_in_bf = _prep_w_in(w_in[0], lru_w, hgrn_norm.shape[1])
    gain1 = norm_mix[0][None, :]
    projx = _inproj(x, mods, gain1, w_in_bf, PROJ_TILE, shared_mod=False)
    projc = _inproj(ctx, mods[b:b + 1], gain1, w_in_bf, ctx.shape[1], shared_mod=True)

    per = LRU_G // (lru_w // LRU_BLOCKS)
    wg = jnp.stack([_block_diag_pairs(lru_wa[0, 0], per), _block_diag_pairs(lru_wx[0, 0], per),
                    _block_diag_pairs(lru_wa[0, 1], per), _block_diag_pairs(lru_wx[0, 1], per)]
                   )
    wg = (0.5 * wg).astype(BF16)
    bg = 0.5 * jnp.stack([lru_ba[0, 0], lru_bx[0, 0], lru_ba[0, 1], lru_bx[0, 1]])
    ylru = _lru(projx, projc, lru_conv_w[0], lru_conv_b[0][None, :], wg, bg, lru_lambda[0])

    yhg = _hgrn(projx, projc, hgrn_lb_logits, hgrn_norm[0][None, :], (2 * lru_w) // LANES)

    wo = w_out[0].astype(BF16).reshape(2, lru_w, d)
    x1, h2, eid_t, gate_t, rank_t, cnt = _outproj(
        ylru, yhg, x, mods, norm_ffn[0][None, :], wo,
        router_w[0].T.astype(BF16), router_b[0][:, None], PROJ_TILE)

    bm = GMM_BLOCK
    counts = cnt[:, 0]
    padded = (counts + bm - 1) // bm * bm
    pends = jnp.cumsum(padded)
    pstarts = pends - padded
    pos = _pos(pstarts, eid_t, rank_t, POS_TILE)
    cap = n * TOP_K + n_experts * bm

    xs = _padfill(_sc_scatter_rows(h2, pos, cap), pstarts + counts, padded - counts, bm)
    ys = _gmm(xs, exp_w13[0], exp_w2[0], pstarts // bm, padded // bm, bm)

    yg = _sc_gather_rows(ys, pos.reshape(-1), SC_GATHER_WINDOW).reshape(TOP_K, n, d // 2)
    tc = COMBINE_TILE
    out = _combine(gate_t.T, x1.reshape(n, d), h2, mods,
                   norm_final[None, :], shared_w13[0].astype(BF16), shared_w2[0].astype(BF16),
                   yg, tc, s // tc)
    return out.reshape(b, s, d)
```

```python
import functools

import jax
import jax.numpy as jnp
from jax import lax
from jax.experimental import pallas as pl
from jax.experimental.pallas import tpu as pltpu
from jax.experimental.pallas import tpu_sc as plsc

F32 = jnp.float32
BF16 = jnp.bfloat16
I32 = jnp.int32
U32 = jnp.uint32

EPS = 1e-6
LRU_C = 8.0
ROUTE_SCALE = 2.5
GRID_W = 64
HG_CHUNK = 32
N_GROUPS = 8
TOPK_GROUPS = 4
TOP_K = 8
LRU_BLOCKS = 8
HG_HEADS = 4
CONV_W = 4
CONV_PAD_L = 1

SUBLANES = 8
LANES = 128
N_SEG = SUBLANES
VMEM_LIMIT = 56 * 1024 * 1024

PROJ_TILE = 512
POS_TILE = 8192
GMM_BLOCK = 256
SC_GATHER_WINDOW = 64
COMBINE_TILE = 512


def _cparams(sem, vmem=VMEM_LIMIT):
    return pltpu.CompilerParams(dimension_semantics=sem, vmem_limit_bytes=vmem)


def _sigmoid(x):
    return jax.nn.sigmoid(x)


def _silu(x):
    return x * _sigmoid(x)


def _pack_rows(x):
    w = x.shape[1] // 2
    bits = pltpu.bitcast(x.astype(BF16).astype(F32), U32)
    return (bits[:, :w] >> 16) | (bits[:, w:] & jnp.uint32(0xFFFF0000))


def _unpack_rows(p):
    lo = pltpu.bitcast(p << 16, F32)
    hi = pltpu.bitcast(p & jnp.uint32(0xFFFF0000), F32)
    return jnp.concatenate([lo, hi], axis=1)


def _gelu_tanh(x):
    c = 0.7978845608028654
    return 0.5 * x * (1.0 + jnp.tanh(c * (x + 0.044715 * (x * x * x))))


def _ada_kernel(c_ref, w_ref, b_ref, o_ref):
    s = _silu(c_ref[...])
    o_ref[...] = jnp.dot(s, w_ref[...], preferred_element_type=F32,
                         precision=lax.Precision.HIGHEST) + b_ref[...]


def _ada(cs, w, b):
    rows, d = cs.shape
    n = w.shape[1]
    bn = 1024
    return pl.pallas_call(
        _ada_kernel,
        out_shape=jax.ShapeDtypeStruct((rows, n), F32),
        grid=(n // bn,),
        in_specs=[pl.BlockSpec((rows, d), lambda j: (0, 0)),
                  pl.BlockSpec((d, bn), lambda j: (0, j)),
                  pl.BlockSpec((1, bn), lambda j: (0, j))],
        out_specs=pl.BlockSpec((rows, bn), lambda j: (0, j)),
        compiler_params=_cparams(("arbitrary",)),
        name="ada",
    )(cs, w, b)


def _inproj_kernel(x_ref, mod_ref, gain_ref, w_ref, o_ref):
    x = x_ref[0]
    ms = jnp.mean(x * x, axis=-1, keepdims=True)
    y = x * lax.rsqrt(ms + EPS) * gain_ref[...]
    h = y * (1.0 + mod_ref[0, 1:2, :]) + mod_ref[0, 0:1, :]
    o_ref[0] = jnp.dot(h.astype(BF16), w_ref[...], preferred_element_type=F32)


def _inproj(x, mods, gain, w_bf, tm, shared_mod):
    b, s, d = x.shape
    n = w_bf.shape[1]
    mod_map = (lambda bi, i: (0, 0, 0)) if shared_mod else (lambda bi, i: (bi, 0, 0))
    return pl.pallas_call(
        _inproj_kernel,
        out_shape=jax.ShapeDtypeStruct((b, s, n), F32),
        grid=(b, s // tm),
        in_specs=[pl.BlockSpec((1, tm, d), lambda bi, i: (bi, i, 0)),
                  pl.BlockSpec((1, 8, d), mod_map),
                  pl.BlockSpec((1, d), lambda bi, i: (0, 0)),
                  pl.BlockSpec((d, n), lambda bi, i: (0, 0))],
        out_specs=pl.BlockSpec((1, tm, n), lambda bi, i: (bi, i, 0)),
        compiler_params=_cparams(("arbitrary", "arbitrary")),
        name="inproj",
    )(x, mods, gain, w_bf)


LRU_G = 256
LRU_CHUNK = 256


def _seg_rows(s, seg_len):
    return pl.ds(s, seg_len, stride=N_SEG)


def _out_pitch(seg_len):
    return seg_len + SUBLANES


def _lane_store(ref, d, rows, val):
    nl = val.shape[-1] // LANES
    for l in range(nl):
        ref[d * nl + l, rows, :] = val[:, l * LANES:(l + 1) * LANES]


def _lane_load(ref, d, rows, nl):
    return jnp.concatenate([ref[d * nl + l, rows, :] for l in range(nl)], axis=-1)


LRU_CONV_PIECE = 64


def _lru_coeffs(pad_ref, t0, rows, cw, cb, wg_ref, bg, sp, a_ref, b_ref, dst_rows, u_ref):
    for p0 in range(0, rows, LRU_CONV_PIECE):
        win = pad_ref[pl.ds(t0 + p0, LRU_CONV_PIECE + 2 * SUBLANES), :]
        u = cb
        for k in range(CONV_W):
            off = SUBLANES - CONV_PAD_L + k
            u = u + win[off:off + LRU_CONV_PIECE, :] * cw[k:k + 1, :]
        u_ref[p0:p0 + LRU_CONV_PIECE, :] = u
    u = u_ref[0:rows, :]
    ub = u.astype(BF16)
    half_u = 0.5 * u
    for d in range(2):
        tr = jnp.tanh(jnp.dot(ub, wg_ref[2 * d, 0], preferred_element_type=F32)
                      + bg[2 * d:2 * d + 1, :])
        ti = jnp.tanh(jnp.dot(ub, wg_ref[2 * d + 1, 0], preferred_element_type=F32)
                      + bg[2 * d + 1:2 * d + 2, :])
        half_c = (-0.5 * LRU_C) * sp[d:d + 1, :]
        log_a = half_c * tr + half_c
        a = jnp.exp(log_a)
        one_minus_a2 = -jnp.tanh(log_a) * (1.0 + a * a)
        root = jnp.where(one_minus_a2 > 0.0, one_minus_a2 * lax.rsqrt(one_minus_a2), 0.0)
        bb = root * (ti * half_u + half_u)
        _lane_store(a_ref, d, dst_rows, a)
        _lane_store(b_ref, d, dst_rows, bb)


def _seg_scan(a_ref, b_ref, h_ref, p_ref, seg_len, unroll):
    n_lead = a_ref.shape[0]
    nl = n_lead // 2
    zero = jnp.zeros((N_SEG, LANES), F32)
    one = jnp.ones((N_SEG, LANES), F32)
    init = tuple((zero, one) for _ in range(n_lead))

    def step(t, carry):
        out = []
        for i in range(n_lead):
            h, p = carry[i]
            tt = t if i < nl else seg_len - 1 - t
            rows = pl.ds(pl.multiple_of(tt * N_SEG, N_SEG), N_SEG)
            a = a_ref[i, rows, :]
            h = a * h + b_ref[i, rows, :]
            p = p * a
            if h_ref is not None:
                out_rows = pl.ds(tt, N_SEG, stride=_out_pitch(seg_len))
                h_ref[i, out_rows, :] = h
                p_ref[i, out_rows, :] = p
            out.append((h, p))
        return tuple(out)

    def body(i, carry):
        for j in range(unroll):
            carry = step(i * unroll + j, carry)
        return carry

    ends = lax.fori_loop(0, seg_len // unroll, body, init)
    h_end = [jnp.concatenate([ends[d * nl + l][0] for l in range(nl)], axis=-1) for d in range(2)]
    p_end = [jnp.concatenate([ends[d * nl + l][1] for l in range(nl)], axis=-1) for d in range(2)]
    return h_end, p_end


def _seg_carries(h_end, p_end, h0, reverse):
    order = range(N_SEG - 1, -1, -1) if reverse else range(N_SEG)
    cin = [None] * N_SEG
    c = h0
    for s in order:
        cin[s] = c
        c = p_end[s:s + 1, :] * c + h_end[s:s + 1, :]
    return cin, c


def _lru_kernel(rx_ref, rg_ref, rc_ref, cw_ref, cb_ref, wg_ref, bg_ref, lam_ref, o_ref,
                padl, padc, u_s, a_l, b_l, h_l, p_l, a_c, b_c):
    s_len = rx_ref.shape[1]
    c_len = rc_ref.shape[1]
    g = rx_ref.shape[2]
    nl = g // LANES
    seg_l = s_len // N_SEG
    seg_c = c_len // N_SEG

    zeros = jnp.zeros((SUBLANES, g), F32)
    padl[0:SUBLANES, :] = zeros
    padl[SUBLANES + s_len:2 * SUBLANES + s_len, :] = zeros
    padc[0:SUBLANES, :] = zeros
    padc[SUBLANES + c_len:2 * SUBLANES + c_len, :] = zeros
    for i in range(s_len // LRU_CHUNK):
        padl[SUBLANES + i * LRU_CHUNK:SUBLANES + (i + 1) * LRU_CHUNK, :] = (
            rx_ref[0, i * LRU_CHUNK:(i + 1) * LRU_CHUNK, :])
    padc[SUBLANES:SUBLANES + c_len, :] = rc_ref[0]

    cw = cw_ref[...]
    cb = cb_ref[...]
    bg = bg_ref[...]
    x = -lam_ref[...]
    sp = jnp.maximum(x, 0.0) + jnp.log(1.0 + jnp.exp(-jnp.abs(x)))

    _lru_coeffs(padc, 0, c_len, cw, cb, wg_ref, bg, sp, a_l, b_l, pl.ds(0, c_len), u_s)
    for i in range(2 * nl):
        for s in range(N_SEG):
            a_c[i, _seg_rows(s, seg_c), :] = a_l[i, s * seg_c:(s + 1) * seg_c, :]
            b_c[i, _seg_rows(s, seg_c), :] = b_l[i, s * seg_c:(s + 1) * seg_c, :]
    h_end, p_end = _seg_scan(a_c, b_c, None, None, seg_c, unroll=4)
    zero_row = jnp.zeros((1, g), F32)
    _, h0_f = _seg_carries(h_end[0], p_end[0], zero_row, reverse=False)
    _, h0_b = _seg_carries(h_end[1], p_end[1], zero_row, reverse=True)

    def coeff_body(s, carry):
        t0 = pl.multiple_of(s * seg_l, SUBLANES)
        _lru_coeffs(padl, t0, seg_l, cw, cb, wg_ref, bg, sp, a_l, b_l, _seg_rows(s, seg_l), u_s)
        return carry

    lax.fori_loop(0, N_SEG, coeff_body, 0)
    h_end, p_end = _seg_scan(a_l, b_l, h_l, p_l, seg_l, unroll=4)
    cin_f, _ = _seg_carries(h_end[0], p_end[0], h0_f, reverse=False)
    cin_b, _ = _seg_carries(h_end[1], p_end[1], h0_b, reverse=True)

    for s in range(N_SEG):
        rows = pl.ds(s * _out_pitch(seg_l), seg_l)
        h = (_lane_load(h_l, 0, rows, nl) + _lane_load(p_l, 0, rows, nl) * cin_f[s]) + (
            _lane_load(h_l, 1, rows, nl) + _lane_load(p_l, 1, rows, nl) * cin_b[s])
        o_ref[0, s * seg_l:(s + 1) * seg_l, :] = _gelu_tanh(rg_ref[0, s * seg_l:(s + 1) * seg_l, :]) * h


def _lru(projx, projc, cw, cb, wg, bg, lam):
    b, s_len, _ = projx.shape
    c_len = projc.shape[1]
    w = cw.shape[1]
    g = LRU_G
    ng = w // g
    seg_l = s_len // N_SEG
    seg_c = c_len // N_SEG
    assert seg_l == LRU_CHUNK and c_len % (N_SEG * 1) == 0
    return pl.pallas_call(
        _lru_kernel,
        out_shape=jax.ShapeDtypeStruct((b, s_len, w), F32),
        grid=(b, ng),
        in_specs=[pl.BlockSpec((1, s_len, g), lambda bi, j: (bi, 0, j)),
                  pl.BlockSpec((1, s_len, g), lambda bi, j: (bi, 0, ng + j)),
                  pl.BlockSpec((1, c_len, g), lambda bi, j: (bi, 0, j)),
                  pl.BlockSpec((CONV_W, g), lambda bi, j: (0, j)),
                  pl.BlockSpec((1, g), lambda bi, j: (0, j)),
                  pl.BlockSpec((4, 1, g, g), lambda bi, j: (0, j, 0, 0)),
                  pl.BlockSpec((4, g), lambda bi, j: (0, j)),
                  pl.BlockSpec((2, g), lambda bi, j: (0, j))],
        out_specs=pl.BlockSpec((1, s_len, g), lambda bi, j: (bi, 0, j)),
        scratch_shapes=[pltpu.VMEM((s_len + 2 * SUBLANES, g), F32),
                        pltpu.VMEM((c_len + 2 * SUBLANES, g), F32),
                        pltpu.VMEM((max(seg_l, c_len), g), F32),
                        pltpu.VMEM((2 * g // LANES, s_len, LANES), F32),
                        pltpu.VMEM((2 * g // LANES, s_len, LANES), F32),
                        pltpu.VMEM((2 * g // LANES, N_SEG * _out_pitch(seg_l), LANES), F32),
                        pltpu.VMEM((2 * g // LANES, N_SEG * _out_pitch(seg_l), LANES), F32),
                        pltpu.VMEM((2 * g // LANES, c_len, LANES), F32),
                        pltpu.VMEM((2 * g // LANES, c_len, LANES), F32)],
        compiler_params=_cparams(("arbitrary", "arbitrary")),
        name="lru",
    )(projx, projx, projc, cw, cb, wg, bg, lam)


LOG2_E = 1.4426950408889634
HG_PITCH_L = GRID_W + SUBLANES
HG_PITCH_C = SUBLANES


def _hg_gates(half_z, lb):
    c = 0.5 * (1.0 - lb)
    ct = c * jnp.tanh(half_z)
    logf = jnp.log(0.5 * (1.0 + lb) + ct)
    k = c - ct
    return logf, k


def _hg_prepass(load, width, pitch, lb, qd, kd, kl, vs, dec):
    n_pos = HG_CHUNK
    for d, zname in enumerate(("zf", "zb")):
        order = range(n_pos) if d == 0 else range(n_pos - 1, -1, -1)
        g = jnp.zeros((width, LANES), F32)
        lbd = lb[d:d + 1, :]
        for p in order:
            logf, k = _hg_gates(load(zname, p), lbd)
            g = g + logf
            qd[d, p * pitch:p * pitch + width, :] = g
            kl[d, p * pitch:p * pitch + width, :] = k
        g_last = g * LOG2_E
        dec[d, 0:width, :] = jnp.exp2(g_last)
        for p in range(n_pos):
            rows = slice(p * pitch, p * pitch + width)
            gp = qd[d, rows, :] * LOG2_E
            k = kl[d, rows, :]
            qd[d, rows, :] = load("q", p) * jnp.exp2(gp)
            kd[d, rows, :] = k * jnp.exp2(-gp)
            kl[d, rows, :] = k * jnp.exp2(g_last - gp)
    for p in range(n_pos):
        vs[p * pitch:p * pitch + width, :] = load("v", p)


HG_GROUP = 4
_NT = (((1,), (1,)), ((), ()))
_TN = (((0,), (0,)), ((), ()))


def _hg_group_rows(ref, lead, c0, pitch):
    parts = []
    for j in range(HG_GROUP):
        rows = pl.ds(c0 + j, HG_CHUNK, stride=pitch)
        parts.append(ref[rows, :] if lead is None else ref[lead, rows, :])
    return jnp.concatenate(parts, axis=0)


def _hg_scores(g, pitch, qd, kd, masks, sc_ref):
    c0 = g * HG_GROUP
    total = None
    for d in range(2):
        q = _hg_group_rows(qd, d, c0, pitch).astype(BF16)
        k = _hg_group_rows(kd, d, c0, pitch).astype(BF16)
        sc = jnp.where(masks[d], lax.dot_general(q, k, _NT, preferred_element_type=F32), 0.0)
        total = sc if total is None else total + sc
    sc_ref[g] = total.astype(BF16)


def _hg_summaries(g, pitch, kl, vs, sc_ref, oi, kv):
    c0 = g * HG_GROUP
    v = _hg_group_rows(vs, None, c0, pitch).astype(BF16)
    if oi is not None:
        o = jnp.dot(sc_ref[g], v, preferred_element_type=F32)
        for j in range(HG_GROUP):
            oi[pl.ds(c0 + j, HG_CHUNK, stride=pitch), :] = o[j * HG_CHUNK:(j + 1) * HG_CHUNK, :]
    kls = jnp.concatenate([_hg_group_rows(kl, 0, c0, pitch), _hg_group_rows(kl, 1, c0, pitch)],
                          axis=1).astype(BF16)
    for j in range(HG_GROUP):
        sl = slice(j * HG_CHUNK, (j + 1) * HG_CHUNK)
        both = lax.dot_general(v[sl], kls[sl], _TN, preferred_element_type=F32)
        kv[0, c0 + j] = both[:, :LANES]
        kv[1, c0 + j] = both[:, LANES:]


def _hg_steps(step0, n_steps, n_chunks, pitch, qd, dec, kv, states, ox):
    states = list(states)
    for u in range(n_steps):
        cs = (step0 + u, n_chunks - 1 - (step0 + u))
        if ox is not None:
            rows = [pl.ds(c, HG_CHUNK, stride=pitch) for c in cs]
            q2 = jnp.concatenate([qd[0, rows[0], :], qd[1, rows[1], :]], axis=0).astype(BF16)
            s2 = jnp.concatenate(states, axis=0).astype(BF16)
            o2 = lax.dot_general(q2, s2, _NT, preferred_element_type=F32)
            ox[0, rows[0], :] = o2[:HG_CHUNK, :LANES]
            ox[1, rows[1], :] = o2[HG_CHUNK:, LANES:]
        for d in range(2):
            states[d] = states[d] * dec[d, pl.ds(cs[d], 1), :] + kv[d, cs[d]]
    return tuple(states)


def _hgrn_kernel(q_ref, v_ref, zf_ref, zb_ref, g_ref, qc_ref, vc_ref, zfc_ref, zbc_ref,
                 lbl_ref, gain_ref, o_ref, qd, kd, kl, vs, dec, oi, ox, kv, sc):
    s_len = q_ref.shape[1]
    c_len = qc_ref.shape[1]
    n_col = s_len // HG_CHUNK
    n_cc = c_len // HG_CHUNK
    assert n_col == GRID_W and n_cc == HG_PITCH_C
    assert n_col % HG_GROUP == 0 and n_cc % HG_GROUP == 0

    lg = lbl_ref[...]
    m = jnp.max(lg, axis=1, keepdims=True)
    ex = jnp.exp(lg - m)
    lb = ex[:, 0, :] / jnp.sum(ex, axis=1)

    gr = HG_GROUP * HG_CHUNK
    ii = lax.broadcasted_iota(I32, (gr, gr), 0)
    jj = lax.broadcasted_iota(I32, (gr, gr), 1)
    same = (ii // HG_CHUNK) == (jj // HG_CHUNK)
    masks = (jnp.logical_and(same, jj <= ii), jnp.logical_and(same, jj >= ii))

    c_refs = {"q": qc_ref, "v": vc_ref, "zf": zfc_ref, "zb": zbc_ref}

    def load_c(name, p):
        return c_refs[name][0, pl.ds(p, n_cc, stride=HG_CHUNK), :]

    _hg_prepass(load_c, n_cc, HG_PITCH_C, lb, qd, kd, kl, vs, dec)
    for g in range(n_cc // HG_GROUP):
        _hg_summaries(g, HG_PITCH_C, kl, vs, None, None, kv)
    zero = jnp.zeros((LANES, LANES), F32)
    states = _hg_steps(0, n_cc, n_cc, HG_PITCH_C, qd, dec, kv, (zero, zero), None)

    l_refs = {"q": q_ref, "v": v_ref, "zf": zf_ref, "zb": zb_ref}

    def load_l(name, p):
        return l_refs[name][0, p * n_col:(p + 1) * n_col, :]

    _hg_prepass(load_l, n_col, HG_PITCH_L, lb, qd, kd, kl, vs, dec)

    n_groups = n_col // HG_GROUP
    for g in range(n_groups):
        _hg_scores(g, HG_PITCH_L, qd, kd, masks, sc)
        if g > 0:
            _hg_summaries(g - 1, HG_PITCH_L, kl, vs, sc, oi, kv)
    _hg_summaries(n_groups - 1, HG_PITCH_L, kl, vs, sc, oi, kv)
    _hg_steps(0, n_col, n_col, HG_PITCH_L, qd, dec, kv, states, ox)

    gain = gain_ref[...]
    for r in range(HG_CHUNK):
        rows = slice(r * HG_PITCH_L, r * HG_PITCH_L + n_col)
        o = oi[rows, :] + (ox[0, rows, :] + ox[1, rows, :])
        y = o * lax.rsqrt(jnp.mean(o * o, axis=-1, keepdims=True) + EPS) * gain
        o_ref[0, r * n_col:(r + 1) * n_col, :] = y * _silu(g_ref[0, r * n_col:(r + 1) * n_col, :])


def _hgrn(projx, projc, lbl, gain, col0):
    b, s_len, _ = projx.shape
    c_len = projc.shape[1]
    hw = gain.shape[1]
    nh = hw // LANES
    n_slots = lbl.shape[1]

    def xs(k):
        return pl.BlockSpec((1, s_len, LANES), lambda bi, h, k=k: (bi, 0, col0 + k * nh + h))

    def cs(k):
        return pl.BlockSpec((1, c_len, LANES), lambda bi, h, k=k: (bi, 0, col0 + k * nh + h))

    sc_rows = HG_CHUNK * HG_PITCH_L
    return pl.pallas_call(
        _hgrn_kernel,
        out_shape=jax.ShapeDtypeStruct((b, s_len, hw), F32),
        grid=(b, nh),
        in_specs=[xs(0), xs(1), xs(2), xs(3), xs(4), cs(0), cs(1), cs(2), cs(3),
                  pl.BlockSpec((2, n_slots, LANES), lambda bi, h: (0, 0, h)),
                  pl.BlockSpec((1, LANES), lambda bi, h: (0, h))],
        out_specs=pl.BlockSpec((1, s_len, LANES), lambda bi, h: (bi, 0, h)),
        scratch_shapes=[pltpu.VMEM((2, sc_rows, LANES), F32),
                        pltpu.VMEM((2, sc_rows, LANES), F32),
                        pltpu.VMEM((2, sc_rows, LANES), F32),
                        pltpu.VMEM((sc_rows, LANES), F32),
                        pltpu.VMEM((2, HG_PITCH_L, LANES), F32),
                        pltpu.VMEM((sc_rows, LANES), F32),
                        pltpu.VMEM((2, sc_rows, LANES), F32),
                        pltpu.VMEM((2, GRID_W, LANES, LANES), F32),
                        pltpu.VMEM((GRID_W // HG_GROUP, HG_GROUP * HG_CHUNK,
                                    HG_GROUP * HG_CHUNK), BF16)],
        compiler_params=_cparams(("arbitrary", "arbitrary")),
        name="hgrn",
    )(projx, projx, projx, projx, projx, projc, projc, projc, projc, lbl, gain)


def _first_index_of_max(vals, iota, big):
    m = jnp.max(vals, axis=0, keepdims=True)
    idx = jnp.min(jnp.where(vals == m, iota, big), axis=0, keepdims=True)
    return m, idx


def _route(logits_t, rb, n_experts):
    t = logits_t.shape[1]
    gsz = n_experts // N_GROUPS
    neg = -jnp.inf
    scores = _sigmoid(logits_t)
    biased = scores + rb
    iota_g = lax.broadcasted_iota(I32, (gsz, t), 0)
    gscore = []
    for gi in range(N_GROUPS):
        blk = biased[gi * gsz:(gi + 1) * gsz, :]
        m1, i1 = _first_index_of_max(blk, iota_g, gsz)
        m2 = jnp.max(jnp.where(iota_g == i1, neg, blk), axis=0, keepdims=True)
        gscore.append(m1 + m2)
    gs = jnp.concatenate(gscore, axis=0)
    iota_n = lax.broadcasted_iota(I32, (N_GROUPS, t), 0)
    gsel = jnp.zeros((N_GROUPS, t), jnp.bool_)
    for _ in range(TOPK_GROUPS):
        _, gi1 = _first_index_of_max(gs, iota_n, N_GROUPS)
        hit = iota_n == gi1
        gsel = jnp.logical_or(gsel, hit)
        gs = jnp.where(hit, neg, gs)
    emask = jnp.concatenate(
        [jnp.broadcast_to(gsel[gi:gi + 1, :], (gsz, t)) for gi in range(N_GROUPS)], axis=0)
    masked = jnp.where(emask, biased, neg)
    iota_e = lax.broadcasted_iota(I32, (n_experts, t), 0)
    ids, gates = [], []
    sel = jnp.zeros((n_experts, t), F32)
    for _ in range(TOP_K):
        _, ei = _first_index_of_max(masked, iota_e, n_experts)
        hit = iota_e == ei
        ids.append(ei)
        gates.append(jnp.sum(jnp.where(hit, scores, 0.0), axis=0, keepdims=True))
        sel = jnp.where(hit, 1.0, sel)
        masked = jnp.where(hit, neg, masked)
    ids = jnp.concatenate(ids, axis=0)
    gates = jnp.concatenate(gates, axis=0)
    gates = gates / jnp.sum(gates, axis=0, keepdims=True) * ROUTE_SCALE
    return ids, gates, sel


def _outproj_kernel(yl_ref, yh_ref, x_ref, mod_ref, gain_ref, wo_ref, rwt_ref, rb_ref,
                    x1_ref, h2_ref, eid_ref, gate_ref, rank_ref, cnt_ref, carry):
    first = jnp.logical_and(pl.program_id(0) == 0, pl.program_id(1) == 0)

    @pl.when(first)
    def _():
        carry[...] = jnp.zeros(carry.shape, F32)

    y = jnp.dot(yl_ref[0].astype(BF16), wo_ref[0], preferred_element_type=F32)
    y = y + jnp.dot(yh_ref[0].astype(BF16), wo_ref[1], preferred_element_type=F32)
    x1 = x_ref[0] + mod_ref[0, 2:3, :] * y
    x1_ref[0] = x1
    ms = jnp.mean(x1 * x1, axis=-1, keepdims=True)
    h2 = x1 * lax.rsqrt(ms + EPS) * gain_ref[...]
    h2 = h2 * (1.0 + mod_ref[0, 4:5, :]) + mod_ref[0, 3:4, :]
    h2_ref[...] = _pack_rows(h2)

    n_experts = rwt_ref.shape[0]
    t = h2.shape[0]
    logits_t = lax.dot_general(rwt_ref[...], h2.astype(BF16), _NT,
                               preferred_element_type=F32)
    ids, gates, sel = _route(logits_t, rb_ref[...], n_experts)
    eid_ref[...] = ids
    gate_ref[...] = gates

    ti = lax.broadcasted_iota(I32, (t, t), 0)
    tj = lax.broadcasted_iota(I32, (t, t), 1)
    upper = (ti < tj).astype(BF16)
    selb = sel.astype(BF16)
    before = jnp.dot(selb, upper, preferred_element_type=F32) + carry[:, 0:1]
    iota_e = lax.broadcasted_iota(I32, (n_experts, t), 0)
    ranks = [jnp.sum(jnp.where(iota_e == ids[k:k + 1, :], before, 0.0), axis=0, keepdims=True)
             for k in range(TOP_K)]
    rank_ref[...] = jnp.concatenate(ranks, axis=0).astype(I32)
    total = carry[...] + jnp.dot(selb, jnp.ones((t, LANES), BF16), preferred_element_type=F32)
    carry[...] = total
    cnt_ref[...] = total.astype(I32)


def _outproj(ylru, yhg, x, mods, gain, wo, rwt, rb, tm):
    b, s, d = x.shape
    hw = ylru.shape[2]
    e = rwt.shape[0]
    n = b * s
    nt = s // tm
    tok = lambda bi, i: (0, bi * nt + i)
    return pl.pallas_call(
        _outproj_kernel,
        out_shape=(jax.ShapeDtypeStruct((b, s, d), F32),
                   jax.ShapeDtypeStruct((n, d // 2), U32),
                   jax.ShapeDtypeStruct((TOP_K, n), I32),
                   jax.ShapeDtypeStruct((TOP_K, n), F32),
                   jax.ShapeDtypeStruct((TOP_K, n), I32),
                   jax.ShapeDtypeStruct((e, LANES), I32)),
        grid=(b, nt),
        in_specs=[pl.BlockSpec((1, tm, hw), lambda bi, i: (bi, i, 0)),
                  pl.BlockSpec((1, tm, hw), lambda bi, i: (bi, i, 0)),
                  pl.BlockSpec((1, tm, d), lambda bi, i: (bi, i, 0)),
                  pl.BlockSpec((1, 8, d), lambda bi, i: (bi, 0, 0)),
                  pl.BlockSpec((1, d), lambda bi, i: (0, 0)),
                  pl.BlockSpec((2, hw, d), lambda bi, i: (0, 0, 0)),
                  pl.BlockSpec((e, d), lambda bi, i: (0, 0)),
                  pl.BlockSpec((e, 1), lambda bi, i: (0, 0))],
        out_specs=(pl.BlockSpec((1, tm, d), lambda bi, i: (bi, i, 0)),
                   pl.BlockSpec((tm, d // 2), lambda bi, i: (bi * nt + i, 0)),
                   pl.BlockSpec((TOP_K, tm), tok),
                   pl.BlockSpec((TOP_K, tm), tok),
                   pl.BlockSpec((TOP_K, tm), tok),
                   pl.BlockSpec((e, LANES), lambda bi, i: (0, 0))),
        scratch_shapes=[pltpu.VMEM((e, LANES), F32)],
        compiler_params=_cparams(("arbitrary", "arbitrary")),
        name="outproj",
    )(ylru, yhg, x, mods, gain, wo, rwt, rb)


def _pos_kernel(pstart, eid_ref, rank_ref, pos_ref):
    eid = eid_ref[...]

    def body(e, acc):
        return jnp.where(eid == e, pstart[e], acc)

    base = lax.fori_loop(0, pstart.shape[0], body, jnp.zeros(eid.shape, I32))
    pos_ref[...] = base + rank_ref[...]


def _pos(pstarts, eid_t, rank_t, tn):
    k, n = eid_t.shape
    spec = pl.BlockSpec((k, tn), lambda i, ps: (0, i))
    return pl.pallas_call(
        _pos_kernel,
        out_shape=jax.ShapeDtypeStruct((k, n), I32),
        grid_spec=pltpu.PrefetchScalarGridSpec(
            num_scalar_prefetch=1, grid=(n // tn,), in_specs=[spec, spec], out_specs=spec),
        compiler_params=_cparams(("arbitrary",)),
        name="pos",
    )(pstarts, eid_t, rank_t)


def _pad_fill(padstart, padlen, zbuf, xs_ref, sem, wait):
    nbits = zbuf.shape[0].bit_length() - 1
    low_bits = SUBLANES.bit_length() - 1

    def go(cp):
        if wait:
            cp.wait()
        else:
            cp.start()

    def body(e, carry):
        st = padstart[e]
        ln = padlen[e]
        end = st + ln
        for bit in range(nbits - 1, low_bits - 1, -1):
            size = 1 << bit
            back = ((ln >> (bit + 1)) << (bit + 1)) + size

            @pl.when((ln & size) != 0)
            def _():
                start = pl.multiple_of(end - back, SUBLANES)
                go(pltpu.make_async_copy(zbuf.at[pl.ds(0, size)],
                                         xs_ref.at[pl.ds(start, size)], sem))
        for j in range(SUBLANES - 1):
            @pl.when(j < (ln & (SUBLANES - 1)))
            def _():
                go(pltpu.make_async_copy(zbuf.at[pl.ds(0, 1)], xs_ref.at[pl.ds(st + j, 1)], sem))
        return carry

    lax.fori_loop(0, padstart.shape[0], body, 0)


def _padfill_kernel(padstart, padlen, xs_in, xs_ref, zbuf, sem):
    del xs_in
    zbuf[...] = jnp.zeros(zbuf.shape, zbuf.dtype)
    _pad_fill(padstart, padlen, zbuf, xs_ref, sem, wait=False)
    _pad_fill(padstart, padlen, zbuf, xs_ref, sem, wait=True)


def _padfill(xs, padstart, padlen, bm):
    return pl.pallas_call(
        _padfill_kernel,
        out_shape=jax.ShapeDtypeStruct(xs.shape, xs.dtype),
        grid_spec=pltpu.PrefetchScalarGridSpec(
            num_scalar_prefetch=2,
            grid=(1,),
            in_specs=[pl.BlockSpec(memory_space=pl.ANY)],
            out_specs=pl.BlockSpec(memory_space=pl.ANY),
            scratch_shapes=[pltpu.VMEM((bm, xs.shape[1]), xs.dtype), pltpu.SemaphoreType.DMA(())]),
        input_output_aliases={2: 0},
        compiler_params=_cparams(("arbitrary",)),
        name="padfill",
    )(padstart, padlen, xs)


SC_SCATTER_WINDOW = 128


def _sc_mesh():
    return plsc.VectorSubcoreMesh(core_axis_name="core", subcore_axis_name="subcore")


def _sc_scatter_rows(rows, pos, cap):
    n, dw = rows.shape
    top_k = pos.shape[0]
    mesh = _sc_mesh()
    n_workers = mesh.num_cores * mesh.num_subcores
    win = SC_SCATTER_WINDOW
    per_w = n // n_workers
    n_chunks = per_w // win
    assert per_w * n_workers == n and n_chunks * win == per_w
    pos_w = pos.reshape(top_k, n_workers, n_chunks, win).transpose(1, 2, 0, 3)
    pos_w = pos_w.reshape(n_workers, n_chunks * top_k, win)

    @functools.partial(pl.kernel, mesh=mesh,
                       out_type=jax.ShapeDtypeStruct((cap, dw), rows.dtype),
                       scratch_types=[pltpu.VMEM((n_chunks * top_k, win), I32),
                                      pltpu.VMEM((win, dw), rows.dtype),
                                      pltpu.SemaphoreType.DMA])
    def scatter(r_hbm, p_hbm, o_hbm, idx_v, rows_v, sem):
        wid = lax.axis_index("subcore") * mesh.num_cores + lax.axis_index("core")
        pltpu.sync_copy(p_hbm.at[wid], idx_v)

        @pl.loop(0, n_chunks)
        def _(c):
            pltpu.sync_copy(r_hbm.at[pl.ds(wid * per_w + c * win, win)], rows_v)
            copies = [pltpu.async_copy(rows_v, o_hbm.at[idx_v.at[c * top_k + k]], sem)
                      for k in range(top_k)]
            for cp in copies:
                cp.wait()

    return scatter(rows, pos_w)


GMM_EXPERTS = 4
GMM_PAIR = 4
GMM_SLOTS = 16
GMM_AHEAD = GMM_SLOTS - GMM_PAIR


def _gmm_rows(xw, w13b, w2b, de):
    xb = _unpack_rows(xw).astype(BF16)
    u = jnp.dot(xb, w13b[...], preferred_element_type=F32)
    hmid = _silu(u[:, :de]) * u[:, de:]
    return _pack_rows(jnp.dot(hmid.astype(BF16), w2b[...], preferred_element_type=F32))


def _gmm_kernel(blk0, nblk, w13_ref, w2_ref, xs_ref, ys_ref, w13b, w2b, xbuf, ybuf, isem, osem):
    step = pl.program_id(0)
    n_e = nblk.shape[0]
    slots, bm = xbuf.shape[0], xbuf.shape[1]
    de = w2_ref.shape[1]
    total = blk0[n_e - 1] + nblk[n_e - 1]

    def x_copy(g):
        rows = pl.ds(pl.multiple_of(g * bm, bm), bm)
        slot = g & (slots - 1)
        return pltpu.make_async_copy(xs_ref.at[rows], xbuf.at[slot], isem.at[slot])

    def y_copy(g):
        rows = pl.ds(pl.multiple_of(g * bm, bm), bm)
        slot = g & (slots - 1)
        return pltpu.make_async_copy(ybuf.at[slot], ys_ref.at[rows], osem.at[slot])

    @pl.when(step == 0)
    def _():
        for j in range(GMM_AHEAD):
            @pl.when(j < total)
            def _():
                x_copy(j).start()

    def run_blocks(g0, count):
        gs = [g0 + j for j in range(count)]
        for g in gs:
            x_copy(g).wait()
        for g in gs:
            @pl.when(g + GMM_AHEAD < total)
            def _():
                x_copy(g + GMM_AHEAD).start()

            @pl.when(g >= slots)
            def _():
                y_copy(g - slots).wait()
        xbs = [_unpack_rows(xbuf[g & (slots - 1)]).astype(BF16) for g in gs]
        us = [jnp.dot(xb, w13b[...], preferred_element_type=F32) for xb in xbs]
        hs = [(_silu(u[:, :de]) * u[:, de:]).astype(BF16) for u in us]
        ys = [jnp.dot(h, w2b[...], preferred_element_type=F32) for h in hs]
        for g, y in zip(gs, ys):
            ybuf[g & (slots - 1)] = _pack_rows(y)
        for g in gs:
            y_copy(g).start()

    def expert_body(j, carry):
        e = step * GMM_EXPERTS + j
        nb = nblk[e]
        b0 = blk0[e]
        w13b[...] = w13_ref[j].astype(BF16)
        w2b[...] = w2_ref[j].astype(BF16)

        def pair_body(i, c):
            run_blocks(b0 + GMM_PAIR * i, GMM_PAIR)
            return c

        lax.fori_loop(0, nb // GMM_PAIR, pair_body, 0)
        for r in range(1, GMM_PAIR):
            @pl.when(nb % GMM_PAIR == r)
            def _():
                run_blocks(b0 + nb - r, r)
        return carry

    lax.fori_loop(0, GMM_EXPERTS, expert_body, 0)

    @pl.when(step == pl.num_programs(0) - 1)
    def _():
        for j in range(1, slots + 1):
            @pl.when(total - j >= 0)
            def _():
                y_copy(total - j).wait()


def _gmm(xs, w13, w2, blk0, nblk, bm):
    cap, dw = xs.shape
    e, d, de2 = w13.shape
    de = w2.shape[1]
    return pl.pallas_call(
        _gmm_kernel,
        out_shape=jax.ShapeDtypeStruct((cap, dw), U32),
        grid_spec=pltpu.PrefetchScalarGridSpec(
            num_scalar_prefetch=2,
            grid=(e // GMM_EXPERTS,),
            in_specs=[pl.BlockSpec((GMM_EXPERTS, d, de2), lambda i, b0, nb: (i, 0, 0)),
                      pl.BlockSpec((GMM_EXPERTS, de, d), lambda i, b0, nb: (i, 0, 0)),
                      pl.BlockSpec(memory_space=pl.ANY)],
            out_specs=pl.BlockSpec(memory_space=pl.ANY),
            scratch_shapes=[pltpu.VMEM((d, de2), BF16), pltpu.VMEM((de, d), BF16),
                            pltpu.VMEM((GMM_SLOTS, bm, dw), U32),
                            pltpu.VMEM((GMM_SLOTS, bm, dw), U32),
                            pltpu.SemaphoreType.DMA((GMM_SLOTS,)),
                            pltpu.SemaphoreType.DMA((GMM_SLOTS,))]),
        compiler_params=_cparams(("arbitrary",)),
        name="gmm",
    )(blk0, nblk, w13, w2, xs)


def _sc_gather_rows(table, idx, window):
    n_idx = idx.shape[0]
    dw = table.shape[1]
    mesh = _sc_mesh()
    n_workers = mesh.num_cores * mesh.num_subcores
    per_w = n_idx // n_workers
    n_chunks = per_w // window
    assert per_w * n_workers == n_idx and n_chunks * window == per_w and n_chunks % 2 == 0

    @functools.partial(pl.kernel, mesh=mesh,
                       out_type=jax.ShapeDtypeStruct((n_idx, dw), table.dtype),
                       scratch_types=[pltpu.VMEM((per_w,), I32),
                                      pltpu.VMEM((window, dw), table.dtype),
                                      pltpu.VMEM((window, dw), table.dtype),
                                      pltpu.SemaphoreType.DMA, pltpu.SemaphoreType.DMA,
                                      pltpu.SemaphoreType.DMA, pltpu.SemaphoreType.DMA])
    def gather(x_hbm, i_hbm, o_hbm, idx_v, rows0, rows1, gs0, gs1, os0, os1):
        wid = lax.axis_index("subcore") * mesh.num_cores + lax.axis_index("core")
        base = wid * per_w
        pltpu.sync_copy(i_hbm.at[pl.ds(base, per_w)], idx_v)
        bufs = ((rows0, gs0, os0), (rows1, gs1, os1))

        def fetch(j, slot):
            rows, gs, _ = bufs[slot]
            off = pl.multiple_of(j * window, window)
            return pltpu.async_copy(x_hbm.at[idx_v.at[pl.ds(off, window)]], rows, gs)

        def fetch_wait(slot):
            rows, gs, _ = bufs[slot]
            pltpu.make_async_copy(x_hbm.at[idx_v.at[pl.ds(0, window)]], rows, gs).wait()

        def put(j, slot):
            rows, _, osem = bufs[slot]
            off = pl.multiple_of(j * window, window)
            return pltpu.async_copy(rows, o_hbm.at[pl.ds(base + off, window)], osem)

        def put_wait(slot):
            rows, _, osem = bufs[slot]
            pltpu.make_async_copy(rows, o_hbm.at[pl.ds(base, window)], osem).wait()

        fetch(0, 0)

        @pl.loop(0, n_chunks, step=2)
        def _(j):
            @pl.when(j > 0)
            def _():
                put_wait(1)
            fetch(j + 1, 1)
            fetch_wait(0)
            put(j, 0)
            put_wait(0)

            @pl.when(j + 2 < n_chunks)
            def _():
                fetch(j + 2, 0)
            fetch_wait(1)
            put(j + 1, 1)

        put_wait(1)

    return gather(table, idx)


def _combine_kernel(gate_ref, x1_ref, h2_ref, mod_ref, gain_ref, sw13_ref, sw2_ref, yg_ref, o_ref):
    ds_ = sw2_ref.shape[0]
    hb = _unpack_rows(h2_ref[...]).astype(BF16)
    u = jnp.dot(hb, sw13_ref[...], preferred_element_type=F32)
    hmid = _silu(u[:, :ds_]) * u[:, ds_:]
    y = jnp.dot(hmid.astype(BF16), sw2_ref[...], preferred_element_type=F32)

    gate = gate_ref[...]
    moe = gate[:, 0:1] * _unpack_rows(yg_ref[0])
    for k in range(1, TOP_K):
        moe = moe + gate[:, k:k + 1] * _unpack_rows(yg_ref[k])
    xo = x1_ref[...] + mod_ref[0, 5:6, :] * (moe + y)
    ms = jnp.mean(xo * xo, axis=-1, keepdims=True)
    o_ref[...] = xo * lax.rsqrt(ms + EPS) * gain_ref[...]


def _combine(gate_tok, x1, h2, mods, gain, sw13, sw2, yg, tc, tiles_per_batch):
    n, d = x1.shape
    dw = h2.shape[1]
    nt = n // tc
    ds2 = sw13.shape[1]
    ds_ = sw2.shape[0]
    return pl.pallas_call(
        _combine_kernel,
        out_shape=jax.ShapeDtypeStruct((n, d), F32),
        grid=(nt,),
        in_specs=[pl.BlockSpec((tc, TOP_K), lambda i: (i, 0)),
                  pl.BlockSpec((tc, d), lambda i: (i, 0)),
                  pl.BlockSpec((tc, dw), lambda i: (i, 0)),
                  pl.BlockSpec((1, 8, d), lambda i: (i // tiles_per_batch, 0, 0)),
                  pl.BlockSpec((1, d), lambda i: (0, 0)),
                  pl.BlockSpec((d, ds2), lambda i: (0, 0)),
                  pl.BlockSpec((ds_, d), lambda i: (0, 0)),
                  pl.BlockSpec((TOP_K, tc, dw), lambda i: (0, i, 0))],
        out_specs=pl.BlockSpec((tc, d), lambda i: (i, 0)),
        compiler_params=_cparams(("arbitrary",)),
        name="combine",
    )(gate_tok, x1, h2, mods, gain, sw13, sw2, yg)


def _prep_w_in(w, lru_w, hg_w):
    z0 = 2 * lru_w + 2 * hg_w
    col = jnp.arange(w.shape[1])
    scale = jnp.where((col >= z0) & (col < z0 + 2 * hg_w), 0.5, 1.0).astype(w.dtype)
    return (w * scale[None, :]).astype(BF16)


def _block_diag_pairs(w, per):
    nb, bd, _ = w.shape
    w = w.reshape(nb // per, per, bd, bd)
    eye = jnp.eye(per, dtype=w.dtype)
    out = jnp.einsum("gpij,pq->gpiqj", w, eye)
    return out.reshape(nb // per, per * bd, per * bd)


def kernel(x, c, ctx, c_ctx, ada_w, ada_b, norm_mix, norm_ffn, norm_final, w_in, w_out,
           lru_conv_w, lru_conv_b, lru_wa, lru_ba, lru_wx, lru_bx, lru_lambda,
           hgrn_lb_logits, hgrn_norm, router_w, router_b, exp_w13, exp_w2, shared_w13, shared_w2):
    assert ada_w.shape[0] == 1, "single-layer block"
    b, s, d = x.shape
    n = b * s
    lru_w = lru_conv_w.shape[2]
    n_experts = router_w.shape[2]

    rows = -(-(b + 1) // SUBLANES) * SUBLANES
    cs = jnp.zeros((rows, d), F32).at[:b].set(c).at[b].set(c_ctx)
    mod = _ada(cs, ada_w[0], ada_b[0][None, :]).reshape(rows, 6, d)
    mods = jnp.pad(mod, ((0, 0), (0, 2), (0, 0)))

    w_in_bf = _prep_w_in(w_in[0], lru_w, hgrn_norm.shape[1])
    gain1 = norm_mix[0][None, :]
    projx = _inproj(x, mods, gain1, w_in_bf, PROJ_TILE, shared_mod=False)
    c_len = ctx.shape[1]
    projc = _inproj(ctx.reshape(1, b * c_len, d), mods[b:b + 1], gain1, w_in_bf, PROJ_TILE,
                    shared_mod=True).reshape(b, c_len, -1)

    per = LRU_G // (lru_w // LRU_BLOCKS)
    wg = jnp.stack([_block_diag_pairs(lru_wa[0, 0], per), _block_diag_pairs(lru_wx[0, 0], per),
                    _block_diag_pairs(lru_wa[0, 1], per), _block_diag_pairs(lru_wx[0, 1], per)]
                   )
    wg = (0.5 * wg).astype(BF16)
    bg = 0.5 * jnp.stack([lru_ba[0, 0], lru_bx[0, 0], lru_ba[0, 1], lru_bx[0, 1]])
    ylru = _lru(projx, projc, lru_conv_w[0], lru_conv_b[0][None, :], wg, bg, lru_lambda[0])

    yhg = _hgrn(projx, projc, hgrn_lb_logits, hgrn_norm[0][None, :], (2 * lru_w) // LANES)

    wo = w_out[0].astype(BF16).reshape(2, lru_w, d)
    x1, h2, eid_t, gate_t, rank_t, cnt = _outproj(
        ylru, yhg, x, mods, norm_ffn[0][None, :], wo,
        router_w[0].T.astype(BF16), router_b[0][:, None], PROJ_TILE)

    bm = GMM_BLOCK
    counts = cnt[:, 0]
    padded = (counts + bm - 1) // bm * bm
    pends = jnp.cumsum(padded)
    pstarts = pends - padded
    pos = _pos(pstarts, eid_t, rank_t, POS_TILE)
    cap = n * TOP_K + n_experts * bm

    xs = _padfill(_sc_scatter_rows(h2, pos, cap), pstarts + counts, padded - counts, bm)
    ys = _gmm(xs, exp_w13[0], exp_w2[0], pstarts // bm, padded // bm, bm)

    yg = _sc_gather_rows(ys, pos.reshape(-1), SC_GATHER_WINDOW).reshape(TOP_K, n, d // 2)
    tc = COMBINE_TILE
    out = _combine(gate_t.T, x1.reshape(n, d), h2, mods,
                   norm_final[None, :], shared_w13[0].astype(BF16), shared_w2[0].astype(BF16),
                   yg, tc, s // tc)
    return out.reshape(b, s, d)
```

```python
import functools

import jax
import jax.numpy as jnp
from jax import lax
from jax.experimental import pallas as pl
from jax.experimental.pallas import tpu as pltpu
from jax.experimental.pallas import tpu_sc as plsc

F32 = jnp.float32
BF16 = jnp.bfloat16
I32 = jnp.int32
U32 = jnp.uint32

EPS = 1e-6
LRU_C = 8.0
ROUTE_SCALE = 2.5
GRID_W = 64
HG_CHUNK = 32
N_GROUPS = 8
TOPK_GROUPS = 4
TOP_K = 8
LRU_BLOCKS = 8
HG_HEADS = 4
CONV_W = 4
CONV_PAD_L = 1

SUBLANES = 8
LANES = 128
N_SEG = SUBLANES
VMEM_LIMIT = 56 * 1024 * 1024

PROJ_TILE = 512
ROUTE_PARTS = 2
POS_TILE = 2048
GMM_BLOCK = 256
SC_GATHER_WINDOW = 64
COMBINE_TILE = 512


def _cparams(sem, vmem=VMEM_LIMIT):
    return pltpu.CompilerParams(dimension_semantics=sem, vmem_limit_bytes=vmem)


def _sigmoid(x):
    return jax.nn.sigmoid(x)


def _silu(x):
    return x * _sigmoid(x)


def _pack_rows(x):
    w = x.shape[1] // 2
    bits = pltpu.bitcast(x.astype(BF16).astype(F32), U32)
    return (bits[:, :w] >> 16) | (bits[:, w:] & jnp.uint32(0xFFFF0000))


def _unpack_rows(p):
    lo = pltpu.bitcast(p << 16, F32)
    hi = pltpu.bitcast(p & jnp.uint32(0xFFFF0000), F32)
    return jnp.concatenate([lo, hi], axis=1)


def _gelu_tanh(x):
    c = 0.7978845608028654
    return 0.5 * x * (1.0 + jnp.tanh(c * (x + 0.044715 * (x * x * x))))


def _ada_kernel(c_ref, w_ref, b_ref, o_ref):
    s = _silu(c_ref[...])
    o_ref[...] = jnp.dot(s, w_ref[...], preferred_element_type=F32,
                         precision=lax.Precision.HIGHEST) + b_ref[...]


def _ada(cs, w, b):
    rows, d = cs.shape
    n = w.shape[1]
    bn = 1024
    return pl.pallas_call(
        _ada_kernel,
        out_shape=jax.ShapeDtypeStruct((rows, n), F32),
        grid=(n // bn,),
        in_specs=[pl.BlockSpec((rows, d), lambda j: (0, 0)),
                  pl.BlockSpec((d, bn), lambda j: (0, j)),
                  pl.BlockSpec((1, bn), lambda j: (0, j))],
        out_specs=pl.BlockSpec((rows, bn), lambda j: (0, j)),
        compiler_params=_cparams(("arbitrary",)),
        name="ada",
    )(cs, w, b)


def _inproj_kernel(x_ref, mod_ref, gain_ref, w_ref, o_ref):
    x = x_ref[0]
    ms = jnp.mean(x * x, axis=-1, keepdims=True)
    y = x * lax.rsqrt(ms + EPS) * gain_ref[...]
    h = y * (1.0 + mod_ref[0, 1:2, :]) + mod_ref[0, 0:1, :]
    o_ref[0] = jnp.dot(h.astype(BF16), w_ref[...], preferred_element_type=F32)


def _inproj(x, mods, gain, w_bf, tm, shared_mod):
    b, s, d = x.shape
    n = w_bf.shape[1]
    mod_map = (lambda bi, i: (0, 0, 0)) if shared_mod else (lambda bi, i: (bi, 0, 0))
    return pl.pallas_call(
        _inproj_kernel,
        out_shape=jax.ShapeDtypeStruct((b, s, n), F32),
        grid=(b, s // tm),
        in_specs=[pl.BlockSpec((1, tm, d), lambda bi, i: (bi, i, 0)),
                  pl.BlockSpec((1, 8, d), mod_map),
                  pl.BlockSpec((1, d), lambda bi, i: (0, 0)),
                  pl.BlockSpec((d, n), lambda bi, i: (0, 0))],
        out_specs=pl.BlockSpec((1, tm, n), lambda bi, i: (bi, i, 0)),
        compiler_params=_cparams(("arbitrary", "arbitrary")),
        name="inproj",
    )(x, mods, gain, w_bf)


LRU_G = 256
LRU_CHUNK = 256


def _seg_rows(s, seg_len):
    return pl.ds(s, seg_len, stride=N_SEG)


def _out_pitch(seg_len):
    return seg_len + SUBLANES


def _lane_store(ref, d, rows, val):
    nl = val.shape[-1] // LANES
    for l in range(nl):
        ref[d * nl + l, rows, :] = val[:, l * LANES:(l + 1) * LANES]


def _lane_load(ref, d, rows, nl):
    return jnp.concatenate([ref[d * nl + l, rows, :] for l in range(nl)], axis=-1)


LRU_CONV_PIECE = 64


def _lru_coeffs(pad_ref, t0, rows, cw, cb, wg_ref, bg, sp, a_ref, b_ref, dst_rows, u_ref):
    for p0 in range(0, rows, LRU_CONV_PIECE):
        win = pad_ref[pl.ds(t0 + p0, LRU_CONV_PIECE + 2 * SUBLANES), :]
        u = cb
        for k in range(CONV_W):
            off = SUBLANES - CONV_PAD_L + k
            u = u + win[off:off + LRU_CONV_PIECE, :] * cw[k:k + 1, :]
        u_ref[p0:p0 + LRU_CONV_PIECE, :] = u
    u = u_ref[0:rows, :]
    ub = u.astype(BF16)
    half_u = 0.5 * u
    for d in range(2):
        tr = jnp.tanh(jnp.dot(ub, wg_ref[2 * d, 0], preferred_element_type=F32)
                      + bg[2 * d:2 * d + 1, :])
        ti = jnp.tanh(jnp.dot(ub, wg_ref[2 * d + 1, 0], preferred_element_type=F32)
                      + bg[2 * d + 1:2 * d + 2, :])
        half_c = (-0.5 * LRU_C) * sp[d:d + 1, :]
        log_a = half_c * tr + half_c
        a = jnp.exp(log_a)
        one_minus_a2 = -jnp.tanh(log_a) * (1.0 + a * a)
        root = jnp.where(one_minus_a2 > 0.0, one_minus_a2 * lax.rsqrt(one_minus_a2), 0.0)
        bb = root * (ti * half_u + half_u)
        _lane_store(a_ref, d, dst_rows, a)
        _lane_store(b_ref, d, dst_rows, bb)


def _seg_scan(a_ref, b_ref, h_ref, p_ref, seg_len, unroll):
    n_lead = a_ref.shape[0]
    nl = n_lead // 2
    zero = jnp.zeros((N_SEG, LANES), F32)
    one = jnp.ones((N_SEG, LANES), F32)
    init = tuple((zero, one) for _ in range(n_lead))

    def step(t, carry):
        out = []
        for i in range(n_lead):
            h, p = carry[i]
            tt = t if i < nl else seg_len - 1 - t
            rows = pl.ds(pl.multiple_of(tt * N_SEG, N_SEG), N_SEG)
            a = a_ref[i, rows, :]
            h = a * h + b_ref[i, rows, :]
            p = p * a
            if h_ref is not None:
                out_rows = pl.ds(tt, N_SEG, stride=_out_pitch(seg_len))
                h_ref[i, out_rows, :] = h
                p_ref[i, out_rows, :] = p
            out.append((h, p))
        return tuple(out)

    def body(i, carry):
        for j in range(unroll):
            carry = step(i * unroll + j, carry)
        return carry

    ends = lax.fori_loop(0, seg_len // unroll, body, init)
    h_end = [jnp.concatenate([ends[d * nl + l][0] for l in range(nl)], axis=-1) for d in range(2)]
    p_end = [jnp.concatenate([ends[d * nl + l][1] for l in range(nl)], axis=-1) for d in range(2)]
    return h_end, p_end


def _seg_carries(h_end, p_end, h0, reverse):
    order = range(N_SEG - 1, -1, -1) if reverse else range(N_SEG)
    cin = [None] * N_SEG
    c = h0
    for s in order:
        cin[s] = c
        c = p_end[s:s + 1, :] * c + h_end[s:s + 1, :]
    return cin, c


def _lru_kernel(rx_ref, rg_ref, rc_ref, cw_ref, cb_ref, wg_ref, bg_ref, lam_ref, o_ref,
                padl, padc, u_s, a_l, b_l, h_l, p_l, a_c, b_c):
    s_len = rx_ref.shape[1]
    c_len = rc_ref.shape[1]
    g = rx_ref.shape[2]
    nl = g // LANES
    seg_l = s_len // N_SEG
    seg_c = c_len // N_SEG

    zeros = jnp.zeros((SUBLANES, g), F32)
    padl[0:SUBLANES, :] = zeros
    padl[SUBLANES + s_len:2 * SUBLANES + s_len, :] = zeros
    padc[0:SUBLANES, :] = zeros
    padc[SUBLANES + c_len:2 * SUBLANES + c_len, :] = zeros
    for i in range(s_len // LRU_CHUNK):
        padl[SUBLANES + i * LRU_CHUNK:SUBLANES + (i + 1) * LRU_CHUNK, :] = (
            rx_ref[0, i * LRU_CHUNK:(i + 1) * LRU_CHUNK, :])
    padc[SUBLANES:SUBLANES + c_len, :] = rc_ref[0]

    cw = cw_ref[...]
    cb = cb_ref[...]
    bg = bg_ref[...]
    x = -lam_ref[...]
    sp = jnp.maximum(x, 0.0) + jnp.log(1.0 + jnp.exp(-jnp.abs(x)))

    _lru_coeffs(padc, 0, c_len, cw, cb, wg_ref, bg, sp, a_l, b_l, pl.ds(0, c_len), u_s)
    for i in range(2 * nl):
        for s in range(N_SEG):
            a_c[i, _seg_rows(s, seg_c), :] = a_l[i, s * seg_c:(s + 1) * seg_c, :]
            b_c[i, _seg_rows(s, seg_c), :] = b_l[i, s * seg_c:(s + 1) * seg_c, :]
    h_end, p_end = _seg_scan(a_c, b_c, None, None, seg_c, unroll=4)
    zero_row = jnp.zeros((1, g), F32)
    _, h0_f = _seg_carries(h_end[0], p_end[0], zero_row, reverse=False)
    _, h0_b = _seg_carries(h_end[1], p_end[1], zero_row, reverse=True)

    for s in range(N_SEG):
        _lru_coeffs(padl, s * seg_l, seg_l, cw, cb, wg_ref, bg, sp, a_l, b_l,
                    _seg_rows(s, seg_l), u_s)
    h_end, p_end = _seg_scan(a_l, b_l, h_l, p_l, seg_l, unroll=4)
    cin_f, _ = _seg_carries(h_end[0], p_end[0], h0_f, reverse=False)
    cin_b, _ = _seg_carries(h_end[1], p_end[1], h0_b, reverse=True)

    for s in range(N_SEG):
        rows = pl.ds(s * _out_pitch(seg_l), seg_l)
        h = (_lane_load(h_l, 0, rows, nl) + _lane_load(p_l, 0, rows, nl) * cin_f[s]) + (
            _lane_load(h_l, 1, rows, nl) + _lane_load(p_l, 1, rows, nl) * cin_b[s])
        o_ref[0, s * seg_l:(s + 1) * seg_l, :] = _gelu_tanh(rg_ref[0, s * seg_l:(s + 1) * seg_l, :]) * h


def _lru(projx, projc, cw, cb, wg, bg, lam):
    b, s_len, _ = projx.shape
    c_len = projc.shape[1]
    w = cw.shape[1]
    g = LRU_G
    ng = w // g
    seg_l = s_len // N_SEG
    seg_c = c_len // N_SEG
    assert seg_l == LRU_CHUNK and c_len % (N_SEG * 1) == 0
    return pl.pallas_call(
        _lru_kernel,
        out_shape=jax.ShapeDtypeStruct((b, s_len, w), F32),
        grid=(b, ng),
        in_specs=[pl.BlockSpec((1, s_len, g), lambda bi, j: (bi, 0, j)),
                  pl.BlockSpec((1, s_len, g), lambda bi, j: (bi, 0, ng + j)),
                  pl.BlockSpec((1, c_len, g), lambda bi, j: (bi, 0, j)),
                  pl.BlockSpec((CONV_W, g), lambda bi, j: (0, j)),
                  pl.BlockSpec((1, g), lambda bi, j: (0, j)),
                  pl.BlockSpec((4, 1, g, g), lambda bi, j: (0, j, 0, 0)),
                  pl.BlockSpec((4, g), lambda bi, j: (0, j)),
                  pl.BlockSpec((2, g), lambda bi, j: (0, j))],
        out_specs=pl.BlockSpec((1, s_len, g), lambda bi, j: (bi, 0, j)),
        scratch_shapes=[pltpu.VMEM((s_len + 2 * SUBLANES, g), F32),
                        pltpu.VMEM((c_len + 2 * SUBLANES, g), F32),
                        pltpu.VMEM((max(seg_l, c_len), g), F32),
                        pltpu.VMEM((2 * g // LANES, s_len, LANES), F32),
                        pltpu.VMEM((2 * g // LANES, s_len, LANES), F32),
                        pltpu.VMEM((2 * g // LANES, N_SEG * _out_pitch(seg_l), LANES), F32),
                        pltpu.VMEM((2 * g // LANES, N_SEG * _out_pitch(seg_l), LANES), F32),
                        pltpu.VMEM((2 * g // LANES, c_len, LANES), F32),
                        pltpu.VMEM((2 * g // LANES, c_len, LANES), F32)],
        compiler_params=_cparams(("arbitrary", "arbitrary")),
        name="lru",
    )(projx, projx, projc, cw, cb, wg, bg, lam)


LOG2_E = 1.4426950408889634
HG_PITCH_L = GRID_W + SUBLANES
HG_PITCH_C = SUBLANES


def _hg_gates(half_z, lb):
    c = 0.5 * (1.0 - lb)
    ct = c * jnp.tanh(half_z)
    logf = jnp.log(0.5 * (1.0 + lb) + ct)
    k = c - ct
    return logf, k


def _hg_prepass(load, width, pitch, lb, qd, kd, kl, vs, dec):
    n_pos = HG_CHUNK
    for d, zname in enumerate(("zf", "zb")):
        order = range(n_pos) if d == 0 else range(n_pos - 1, -1, -1)
        g = jnp.zeros((width, LANES), F32)
        lbd = lb[d:d + 1, :]
        for p in order:
            logf, k = _hg_gates(load(zname, p), lbd)
            g = g + logf
            qd[d, p * pitch:p * pitch + width, :] = g
            kl[d, p * pitch:p * pitch + width, :] = k
        g_last = g * LOG2_E
        dec[d, 0:width, :] = jnp.exp2(g_last)
        for p in range(n_pos):
            rows = slice(p * pitch, p * pitch + width)
            gp = qd[d, rows, :] * LOG2_E
            k = kl[d, rows, :]
            qd[d, rows, :] = load("q", p) * jnp.exp2(gp)
            kd[d, rows, :] = k * jnp.exp2(-gp)
            kl[d, rows, :] = k * jnp.exp2(g_last - gp)
    for p in range(n_pos):
        vs[p * pitch:p * pitch + width, :] = load("v", p)


HG_GROUP = 4
_NT = (((1,), (1,)), ((), ()))
_TN = (((0,), (0,)), ((), ()))


def _hg_group_rows(ref, lead, c0, pitch):
    parts = []
    for j in range(HG_GROUP):
        rows = pl.ds(c0 + j, HG_CHUNK, stride=pitch)
        parts.append(ref[rows, :] if lead is None else ref[lead, rows, :])
    return jnp.concatenate(parts, axis=0)


def _hg_scores(g, pitch, qd, kd, masks, sc_ref):
    c0 = g * HG_GROUP
    total = None
    for d in range(2):
        q = _hg_group_rows(qd, d, c0, pitch).astype(BF16)
        k = _hg_group_rows(kd, d, c0, pitch).astype(BF16)
        sc = jnp.where(masks[d], lax.dot_general(q, k, _NT, preferred_element_type=F32), 0.0)
        total = sc if total is None else total + sc
    sc_ref[g] = total.astype(BF16)


def _hg_summaries(g, pitch, kl, vs, sc_ref, oi, kv):
    c0 = g * HG_GROUP
    v = _hg_group_rows(vs, None, c0, pitch).astype(BF16)
    if oi is not None:
        o = jnp.dot(sc_ref[g], v, preferred_element_type=F32)
        for j in range(HG_GROUP):
            oi[pl.ds(c0 + j, HG_CHUNK, stride=pitch), :] = o[j * HG_CHUNK:(j + 1) * HG_CHUNK, :]
    kls = jnp.concatenate([_hg_group_rows(kl, 0, c0, pitch), _hg_group_rows(kl, 1, c0, pitch)],
                          axis=1).astype(BF16)
    for j in range(HG_GROUP):
        sl = slice(j * HG_CHUNK, (j + 1) * HG_CHUNK)
        both = lax.dot_general(v[sl], kls[sl], _TN, preferred_element_type=F32)
        kv[0, c0 + j] = both[:, :LANES]
        kv[1, c0 + j] = both[:, LANES:]


def _hg_steps(step0, n_steps, n_chunks, pitch, qd, dec, kv, states, ox):
    states = list(states)
    for u in range(n_steps):
        cs = (step0 + u, n_chunks - 1 - (step0 + u))
        if ox is not None:
            rows = [pl.ds(c, HG_CHUNK, stride=pitch) for c in cs]
            q2 = jnp.concatenate([qd[0, rows[0], :], qd[1, rows[1], :]], axis=0).astype(BF16)
            s2 = jnp.concatenate(states, axis=0).astype(BF16)
            o2 = lax.dot_general(q2, s2, _NT, preferred_element_type=F32)
            ox[0, rows[0], :] = o2[:HG_CHUNK, :LANES]
            ox[1, rows[1], :] = o2[HG_CHUNK:, LANES:]
        for d in range(2):
            states[d] = states[d] * dec[d, pl.ds(cs[d], 1), :] + kv[d, cs[d]]
    return tuple(states)


def _hgrn_kernel(q_ref, v_ref, zf_ref, zb_ref, g_ref, qc_ref, vc_ref, zfc_ref, zbc_ref,
                 lbl_ref, gain_ref, o_ref, qd, kd, kl, vs, dec, oi, ox, kv, sc):
    s_len = q_ref.shape[1]
    c_len = qc_ref.shape[1]
    n_col = s_len // HG_CHUNK
    n_cc = c_len // HG_CHUNK
    assert n_col == GRID_W and n_cc == HG_PITCH_C
    assert n_col % HG_GROUP == 0 and n_cc % HG_GROUP == 0

    lg = lbl_ref[...]
    m = jnp.max(lg, axis=1, keepdims=True)
    ex = jnp.exp(lg - m)
    lb = ex[:, 0, :] / jnp.sum(ex, axis=1)

    gr = HG_GROUP * HG_CHUNK
    ii = lax.broadcasted_iota(I32, (gr, gr), 0)
    jj = lax.broadcasted_iota(I32, (gr, gr), 1)
    same = (ii // HG_CHUNK) == (jj // HG_CHUNK)
    masks = (jnp.logical_and(same, jj <= ii), jnp.logical_and(same, jj >= ii))

    c_refs = {"q": qc_ref, "v": vc_ref, "zf": zfc_ref, "zb": zbc_ref}

    def load_c(name, p):
        return c_refs[name][0, pl.ds(p, n_cc, stride=HG_CHUNK), :]

    _hg_prepass(load_c, n_cc, HG_PITCH_C, lb, qd, kd, kl, vs, dec)
    for g in range(n_cc // HG_GROUP):
        _hg_summaries(g, HG_PITCH_C, kl, vs, None, None, kv)
    zero = jnp.zeros((LANES, LANES), F32)
    states = _hg_steps(0, n_cc, n_cc, HG_PITCH_C, qd, dec, kv, (zero, zero), None)

    l_refs = {"q": q_ref, "v": v_ref, "zf": zf_ref, "zb": zb_ref}

    def load_l(name, p):
        return l_refs[name][0, p * n_col:(p + 1) * n_col, :]

    _hg_prepass(load_l, n_col, HG_PITCH_L, lb, qd, kd, kl, vs, dec)

    n_groups = n_col // HG_GROUP
    for g in range(n_groups):
        _hg_scores(g, HG_PITCH_L, qd, kd, masks, sc)
        if g > 0:
            _hg_summaries(g - 1, HG_PITCH_L, kl, vs, sc, oi, kv)
    _hg_summaries(n_groups - 1, HG_PITCH_L, kl, vs, sc, oi, kv)
    _hg_steps(0, n_col, n_col, HG_PITCH_L, qd, dec, kv, states, ox)

    gain = gain_ref[...]
    for r in range(HG_CHUNK):
        rows = slice(r * HG_PITCH_L, r * HG_PITCH_L + n_col)
        o = oi[rows, :] + (ox[0, rows, :] + ox[1, rows, :])
        y = o * lax.rsqrt(jnp.mean(o * o, axis=-1, keepdims=True) + EPS) * gain
        o_ref[0, r * n_col:(r + 1) * n_col, :] = y * _silu(g_ref[0, r * n_col:(r + 1) * n_col, :])


def _hgrn(projx, projc, lbl, gain, col0):
    b, s_len, _ = projx.shape
    c_len = projc.shape[1]
    hw = gain.shape[1]
    nh = hw // LANES
    n_slots = lbl.shape[1]

    def xs(k):
        return pl.BlockSpec((1, s_len, LANES), lambda bi, h, k=k: (bi, 0, col0 + k * nh + h))

    def cs(k):
        return pl.BlockSpec((1, c_len, LANES), lambda bi, h, k=k: (bi, 0, col0 + k * nh + h))

    sc_rows = HG_CHUNK * HG_PITCH_L
    return pl.pallas_call(
        _hgrn_kernel,
        out_shape=jax.ShapeDtypeStruct((b, s_len, hw), F32),
        grid=(b, nh),
        in_specs=[xs(0), xs(1), xs(2), xs(3), xs(4), cs(0), cs(1), cs(2), cs(3),
                  pl.BlockSpec((2, n_slots, LANES), lambda bi, h: (0, 0, h)),
                  pl.BlockSpec((1, LANES), lambda bi, h: (0, h))],
        out_specs=pl.BlockSpec((1, s_len, LANES), lambda bi, h: (bi, 0, h)),
        scratch_shapes=[pltpu.VMEM((2, sc_rows, LANES), F32),
                        pltpu.VMEM((2, sc_rows, LANES), F32),
                        pltpu.VMEM((2, sc_rows, LANES), F32),
                        pltpu.VMEM((sc_rows, LANES), F32),
                        pltpu.VMEM((2, HG_PITCH_L, LANES), F32),
                        pltpu.VMEM((sc_rows, LANES), F32),
                        pltpu.VMEM((2, sc_rows, LANES), F32),
                        pltpu.VMEM((2, GRID_W, LANES, LANES), F32),
                        pltpu.VMEM((GRID_W // HG_GROUP, HG_GROUP * HG_CHUNK,
                                    HG_GROUP * HG_CHUNK), BF16)],
        compiler_params=_cparams(("arbitrary", "arbitrary")),
        name="hgrn",
    )(projx, projx, projx, projx, projx, projc, projc, projc, projc, lbl, gain)


def _first_index_of_max(vals, iota, big):
    m = jnp.max(vals, axis=0, keepdims=True)
    idx = jnp.min(jnp.where(vals == m, iota, big), axis=0, keepdims=True)
    return m, idx


def _route(logits_t, rb, n_experts):
    t = logits_t.shape[1]
    gsz = n_experts // N_GROUPS
    neg = -jnp.inf
    scores = _sigmoid(logits_t)
    biased = scores + rb
    iota_g = lax.broadcasted_iota(I32, (gsz, t), 0)
    gscore = []
    for gi in range(N_GROUPS):
        blk = biased[gi * gsz:(gi + 1) * gsz, :]
        m1, i1 = _first_index_of_max(blk, iota_g, gsz)
        m2 = jnp.max(jnp.where(iota_g == i1, neg, blk), axis=0, keepdims=True)
        gscore.append(m1 + m2)
    gs = jnp.concatenate(gscore, axis=0)
    iota_n = lax.broadcasted_iota(I32, (N_GROUPS, t), 0)
    gsel = jnp.zeros((N_GROUPS, t), jnp.bool_)
    for _ in range(TOPK_GROUPS):
        _, gi1 = _first_index_of_max(gs, iota_n, N_GROUPS)
        hit = iota_n == gi1
        gsel = jnp.logical_or(gsel, hit)
        gs = jnp.where(hit, neg, gs)
    emask = jnp.concatenate(
        [jnp.broadcast_to(gsel[gi:gi + 1, :], (gsz, t)) for gi in range(N_GROUPS)], axis=0)
    masked = jnp.where(emask, biased, neg)
    parts = ROUTE_PARTS if t % (ROUTE_PARTS * LANES) == 0 else 1
    tp = t // parts
    iota_e = lax.broadcasted_iota(I32, (n_experts, tp), 0)
    cols = [slice(p * tp, (p + 1) * tp) for p in range(parts)]
    masked_p = [masked[:, c] for c in cols]
    scores_p = [scores[:, c] for c in cols]
    sel_p = [jnp.zeros((n_experts, tp), F32) for _ in cols]
    ids_p = [[] for _ in cols]
    gates_p = [[] for _ in cols]
    for _ in range(TOP_K):
        for p in range(parts):
            _, ei = _first_index_of_max(masked_p[p], iota_e, n_experts)
            hit = iota_e == ei
            ids_p[p].append(ei)
            gates_p[p].append(jnp.sum(jnp.where(hit, scores_p[p], 0.0), axis=0, keepdims=True))
            sel_p[p] = jnp.where(hit, 1.0, sel_p[p])
            masked_p[p] = jnp.where(hit, neg, masked_p[p])
    ids = jnp.concatenate([jnp.concatenate(i, axis=0) for i in ids_p], axis=1)
    gates = jnp.concatenate([jnp.concatenate(g, axis=0) for g in gates_p], axis=1)
    sel = jnp.concatenate(sel_p, axis=1)
    gates = gates / jnp.sum(gates, axis=0, keepdims=True) * ROUTE_SCALE
    return ids, gates, sel


def _outproj_kernel(yl_ref, yh_ref, x_ref, mod_ref, gain_ref, wo_ref, rwt_ref, rb_ref,
                    x1_ref, h2_ref, eid_ref, gate_ref, rank_ref, cnt_ref, carry):
    first = jnp.logical_and(pl.program_id(0) == 0, pl.program_id(1) == 0)

    @pl.when(first)
    def _():
        carry[...] = jnp.zeros(carry.shape, F32)

    y = jnp.dot(yl_ref[0].astype(BF16), wo_ref[0], preferred_element_type=F32)
    y = y + jnp.dot(yh_ref[0].astype(BF16), wo_ref[1], preferred_element_type=F32)
    x1 = x_ref[0] + mod_ref[0, 2:3, :] * y
    x1_ref[0] = x1
    ms = jnp.mean(x1 * x1, axis=-1, keepdims=True)
    h2 = x1 * lax.rsqrt(ms + EPS) * gain_ref[...]
    h2 = h2 * (1.0 + mod_ref[0, 4:5, :]) + mod_ref[0, 3:4, :]
    h2_ref[...] = _pack_rows(h2)

    n_experts = rwt_ref.shape[0]
    t = h2.shape[0]
    logits_t = lax.dot_general(rwt_ref[...], h2.astype(BF16), _NT,
                               preferred_element_type=F32)
    ids, gates, sel = _route(logits_t, rb_ref[...], n_experts)
    eid_ref[...] = ids
    gate_ref[...] = gates

    ti = lax.broadcasted_iota(I32, (t, t), 0)
    tj = lax.broadcasted_iota(I32, (t, t), 1)
    upper = (ti < tj).astype(BF16)
    selb = sel.astype(BF16)
    before = jnp.dot(selb, upper, preferred_element_type=F32) + carry[:, 0:1]
    iota_e = lax.broadcasted_iota(I32, (n_experts, t), 0)
    ranks = [jnp.sum(jnp.where(iota_e == ids[k:k + 1, :], before, 0.0), axis=0, keepdims=True)
             for k in range(TOP_K)]
    rank_ref[...] = jnp.concatenate(ranks, axis=0).astype(I32)
    total = carry[...] + jnp.dot(selb, jnp.ones((t, LANES), BF16), preferred_element_type=F32)
    carry[...] = total
    cnt_ref[...] = total.astype(I32)


def _outproj(ylru, yhg, x, mods, gain, wo, rwt, rb, tm):
    b, s, d = x.shape
    hw = ylru.shape[2]
    e = rwt.shape[0]
    n = b * s
    nt = s // tm
    tok = lambda bi, i: (0, bi * nt + i)
    return pl.pallas_call(
        _outproj_kernel,
        out_shape=(jax.ShapeDtypeStruct((b, s, d), F32),
                   jax.ShapeDtypeStruct((n, d // 2), U32),
                   jax.ShapeDtypeStruct((TOP_K, n), I32),
                   jax.ShapeDtypeStruct((TOP_K, n), F32),
                   jax.ShapeDtypeStruct((TOP_K, n), I32),
                   jax.ShapeDtypeStruct((e, LANES), I32)),
        grid=(b, nt),
        in_specs=[pl.BlockSpec((1, tm, hw), lambda bi, i: (bi, i, 0)),
                  pl.BlockSpec((1, tm, hw), lambda bi, i: (bi, i, 0)),
                  pl.BlockSpec((1, tm, d), lambda bi, i: (bi, i, 0)),
                  pl.BlockSpec((1, 8, d), lambda bi, i: (bi, 0, 0)),
                  pl.BlockSpec((1, d), lambda bi, i: (0, 0)),
                  pl.BlockSpec((2, hw, d), lambda bi, i: (0, 0, 0)),
                  pl.BlockSpec((e, d), lambda bi, i: (0, 0)),
                  pl.BlockSpec((e, 1), lambda bi, i: (0, 0))],
        out_specs=(pl.BlockSpec((1, tm, d), lambda bi, i: (bi, i, 0)),
                   pl.BlockSpec((tm, d // 2), lambda bi, i: (bi * nt + i, 0)),
                   pl.BlockSpec((TOP_K, tm), tok),
                   pl.BlockSpec((TOP_K, tm), tok),
                   pl.BlockSpec((TOP_K, tm), tok),
                   pl.BlockSpec((e, LANES), lambda bi, i: (0, 0))),
        scratch_shapes=[pltpu.VMEM((e, LANES), F32)],
        compiler_params=_cparams(("arbitrary", "arbitrary")),
        name="outproj",
    )(ylru, yhg, x, mods, gain, wo, rwt, rb)


def _pos_kernel(pstart, eid_ref, rank_ref, pos_ref):
    eid = eid_ref[...]

    def body(e, acc):
        return jnp.where(eid == e, pstart[e], acc)

    base = lax.fori_loop(0, pstart.shape[0], body, jnp.zeros(eid.shape, I32))
    pos_ref[...] = base + rank_ref[...]


def _pos(pstarts, eid_t, rank_t, tn):
    k, n = eid_t.shape
    spec = pl.BlockSpec((k, tn), lambda i, ps: (0, i))
    return pl.pallas_call(
        _pos_kernel,
        out_shape=jax.ShapeDtypeStruct((k, n), I32),
        grid_spec=pltpu.PrefetchScalarGridSpec(
            num_scalar_prefetch=1, grid=(n // tn,), in_specs=[spec, spec], out_specs=spec),
        compiler_params=_cparams(("arbitrary",)),
        name="pos",
    )(pstarts, eid_t, rank_t)


def _pad_fill(padstart, padlen, zbuf, xs_ref, sem, wait):
    nbits = zbuf.shape[0].bit_length() - 1
    low_bits = SUBLANES.bit_length() - 1

    def go(cp):
        if wait:
            cp.wait()
        else:
            cp.start()

    def body(e, carry):
        st = padstart[e]
        ln = padlen[e]
        end = st + ln
        for bit in range(nbits - 1, low_bits - 1, -1):
            size = 1 << bit
            back = ((ln >> (bit + 1)) << (bit + 1)) + size

            @pl.when((ln & size) != 0)
            def _():
                start = pl.multiple_of(end - back, SUBLANES)
                go(pltpu.make_async_copy(zbuf.at[pl.ds(0, size)],
                                         xs_ref.at[pl.ds(start, size)], sem))
        for j in range(SUBLANES - 1):
            @pl.when(j < (ln & (SUBLANES - 1)))
            def _():
                go(pltpu.make_async_copy(zbuf.at[pl.ds(0, 1)], xs_ref.at[pl.ds(st + j, 1)], sem))
        return carry

    lax.fori_loop(0, padstart.shape[0], body, 0)


def _padfill_kernel(padstart, padlen, xs_in, xs_ref, zbuf, sem):
    del xs_in
    zbuf[...] = jnp.zeros(zbuf.shape, zbuf.dtype)
    _pad_fill(padstart, padlen, zbuf, xs_ref, sem, wait=False)
    _pad_fill(padstart, padlen, zbuf, xs_ref, sem, wait=True)


def _padfill(xs, padstart, padlen, bm):
    return pl.pallas_call(
        _padfill_kernel,
        out_shape=jax.ShapeDtypeStruct(xs.shape, xs.dtype),
        grid_spec=pltpu.PrefetchScalarGridSpec(
            num_scalar_prefetch=2,
            grid=(1,),
            in_specs=[pl.BlockSpec(memory_space=pl.ANY)],
            out_specs=pl.BlockSpec(memory_space=pl.ANY),
            scratch_shapes=[pltpu.VMEM((bm, xs.shape[1]), xs.dtype), pltpu.SemaphoreType.DMA(())]),
        input_output_aliases={2: 0},
        compiler_params=_cparams(("arbitrary",)),
        name="padfill",
    )(padstart, padlen, xs)


SC_SCATTER_WINDOW = 128


def _sc_mesh():
    return plsc.VectorSubcoreMesh(core_axis_name="core", subcore_axis_name="subcore")


def _sc_scatter_rows(rows, pos, cap):
    n, dw = rows.shape
    top_k = pos.shape[0]
    mesh = _sc_mesh()
    n_workers = mesh.num_cores * mesh.num_subcores
    win = SC_SCATTER_WINDOW
    per_w = n // n_workers
    n_chunks = per_w // win
    assert per_w * n_workers == n and n_chunks * win == per_w
    pos_w = pos.reshape(top_k, n_workers, n_chunks, win).transpose(1, 2, 0, 3)
    pos_w = pos_w.reshape(n_workers, n_chunks * top_k, win)

    @functools.partial(pl.kernel, mesh=mesh,
                       out_type=jax.ShapeDtypeStruct((cap, dw), rows.dtype),
                       scratch_types=[pltpu.VMEM((n_chunks * top_k, win), I32),
                                      pltpu.VMEM((win, dw), rows.dtype),
                                      pltpu.SemaphoreType.DMA])
    def scatter(r_hbm, p_hbm, o_hbm, idx_v, rows_v, sem):
        wid = lax.axis_index("subcore") * mesh.num_cores + lax.axis_index("core")
        pltpu.sync_copy(p_hbm.at[wid], idx_v)

        @pl.loop(0, n_chunks)
        def _(c):
            pltpu.sync_copy(r_hbm.at[pl.ds(wid * per_w + c * win, win)], rows_v)
            copies = [pltpu.async_copy(rows_v, o_hbm.at[idx_v.at[c * top_k + k]], sem)
                      for k in range(top_k)]
            for cp in copies:
                cp.wait()

    return scatter(rows, pos_w)


GMM_EXPERTS = 4
GMM_PAIR = 4
GMM_SLOTS = 16
GMM_AHEAD = GMM_SLOTS - GMM_PAIR


def _gmm_rows(xw, w13b, w2b, de):
    xb = _unpack_rows(xw).astype(BF16)
    u = jnp.dot(xb, w13b[...], preferred_element_type=F32)
    hmid = _silu(u[:, :de]) * u[:, de:]
    return _pack_rows(jnp.dot(hmid.astype(BF16), w2b[...], preferred_element_type=F32))


def _gmm_kernel(blk0, nblk, w13_ref, w2_ref, xs_ref, ys_ref, w13b, w2b, xbuf, ybuf, isem, osem):
    step = pl.program_id(0)
    n_e = nblk.shape[0]
    slots, bm = xbuf.shape[0], xbuf.shape[1]
    de = w2_ref.shape[1]
    total = blk0[n_e - 1] + nblk[n_e - 1]

    def x_copy(g):
        rows = pl.ds(pl.multiple_of(g * bm, bm), bm)
        slot = g & (slots - 1)
        return pltpu.make_async_copy(xs_ref.at[rows], xbuf.at[slot], isem.at[slot])

    def y_copy(g):
        rows = pl.ds(pl.multiple_of(g * bm, bm), bm)
        slot = g & (slots - 1)
        return pltpu.make_async_copy(ybuf.at[slot], ys_ref.at[rows], osem.at[slot])

    @pl.when(step == 0)
    def _():
        for j in range(GMM_AHEAD):
            @pl.when(j < total)
            def _():
                x_copy(j).start()

    def run_blocks(g0, count):
        gs = [g0 + j for j in range(count)]
        for g in gs:
            x_copy(g).wait()
        for g in gs:
            @pl.when(g + GMM_AHEAD < total)
            def _():
                x_copy(g + GMM_AHEAD).start()

            @pl.when(g >= slots)
            def _():
                y_copy(g - slots).wait()
        xbs = [_unpack_rows(xbuf[g & (slots - 1)]).astype(BF16) for g in gs]
        us = [jnp.dot(xb, w13b[...], preferred_element_type=F32) for xb in xbs]
        hs = [(_silu(u[:, :de]) * u[:, de:]).astype(BF16) for u in us]
        ys = [jnp.dot(h, w2b[...], preferred_element_type=F32) for h in hs]
        for g, y in zip(gs, ys):
            ybuf[g & (slots - 1)] = _pack_rows(y)
        for g in gs:
            y_copy(g).start()

    def expert_body(j, carry):
        e = step * GMM_EXPERTS + j
        nb = nblk[e]
        b0 = blk0[e]
        w13b[...] = w13_ref[j].astype(BF16)
        w2b[...] = w2_ref[j].astype(BF16)

        def pair_body(i, c):
            run_blocks(b0 + GMM_PAIR * i, GMM_PAIR)
            return c

        lax.fori_loop(0, nb // GMM_PAIR, pair_body, 0)
        for r in range(1, GMM_PAIR):
            @pl.when(nb % GMM_PAIR == r)
            def _():
                run_blocks(b0 + nb - r, r)
        return carry

    lax.fori_loop(0, GMM_EXPERTS, expert_body, 0)

    @pl.when(step == pl.num_programs(0) - 1)
    def _():
        for j in range(1, slots + 1):
            @pl.when(total - j >= 0)
            def _():
                y_copy(total - j).wait()


def _gmm(xs, w13, w2, blk0, nblk, bm):
    cap, dw = xs.shape
    e, d, de2 = w13.shape
    de = w2.shape[1]
    return pl.pallas_call(
        _gmm_kernel,
        out_shape=jax.ShapeDtypeStruct((cap, dw), U32),
        grid_spec=pltpu.PrefetchScalarGridSpec(
            num_scalar_prefetch=2,
            grid=(e // GMM_EXPERTS,),
            in_specs=[pl.BlockSpec((GMM_EXPERTS, d, de2), lambda i, b0, nb: (i, 0, 0)),
                      pl.BlockSpec((GMM_EXPERTS, de, d), lambda i, b0, nb: (i, 0, 0)),
                      pl.BlockSpec(memory_space=pl.ANY)],
            out_specs=pl.BlockSpec(memory_space=pl.ANY),
            scratch_shapes=[pltpu.VMEM((d, de2), BF16), pltpu.VMEM((de, d), BF16),
                            pltpu.VMEM((GMM_SLOTS, bm, dw), U32),
                            pltpu.VMEM((GMM_SLOTS, bm, dw), U32),
                            pltpu.SemaphoreType.DMA((GMM_SLOTS,)),
                            pltpu.SemaphoreType.DMA((GMM_SLOTS,))]),
        compiler_params=_cparams(("arbitrary",)),
        name="gmm",
    )(blk0, nblk, w13, w2, xs)


def _sc_gather_rows(table, idx, window):
    n_idx = idx.shape[0]
    dw = table.shape[1]
    mesh = _sc_mesh()
    n_workers = mesh.num_cores * mesh.num_subcores
    per_w = n_idx // n_workers
    n_chunks = per_w // window
    assert per_w * n_workers == n_idx and n_chunks * window == per_w and n_chunks % 2 == 0

    @functools.partial(pl.kernel, mesh=mesh,
                       out_type=jax.ShapeDtypeStruct((n_idx, dw), table.dtype),
                       scratch_types=[pltpu.VMEM((per_w,), I32),
                                      pltpu.VMEM((window, dw), table.dtype),
                                      pltpu.VMEM((window, dw), table.dtype),
                                      pltpu.SemaphoreType.DMA, pltpu.SemaphoreType.DMA,
                                      pltpu.SemaphoreType.DMA, pltpu.SemaphoreType.DMA])
    def gather(x_hbm, i_hbm, o_hbm, idx_v, rows0, rows1, gs0, gs1, os0, os1):
        wid = lax.axis_index("subcore") * mesh.num_cores + lax.axis_index("core")
        base = wid * per_w
        pltpu.sync_copy(i_hbm.at[pl.ds(base, per_w)], idx_v)
        bufs = ((rows0, gs0, os0), (rows1, gs1, os1))

        def fetch(j, slot):
            rows, gs, _ = bufs[slot]
            off = pl.multiple_of(j * window, window)
            return pltpu.async_copy(x_hbm.at[idx_v.at[pl.ds(off, window)]], rows, gs)

        def fetch_wait(slot):
            rows, gs, _ = bufs[slot]
            pltpu.make_async_copy(x_hbm.at[idx_v.at[pl.ds(0, window)]], rows, gs).wait()

        def put(j, slot):
            rows, _, osem = bufs[slot]
            off = pl.multiple_of(j * window, window)
            return pltpu.async_copy(rows, o_hbm.at[pl.ds(base + off, window)], osem)

        def put_wait(slot):
            rows, _, osem = bufs[slot]
            pltpu.make_async_copy(rows, o_hbm.at[pl.ds(base, window)], osem).wait()

        fetch(0, 0)

        @pl.loop(0, n_chunks, step=2)
        def _(j):
            @pl.when(j > 0)
            def _():
                put_wait(1)
            fetch(j + 1, 1)
            fetch_wait(0)
            put(j, 0)
            put_wait(0)

            @pl.when(j + 2 < n_chunks)
            def _():
                fetch(j + 2, 0)
            fetch_wait(1)
            put(j + 1, 1)

        put_wait(1)

    return gather(table, idx)


def _combine_kernel(gate_ref, x1_ref, h2_ref, mod_ref, gain_ref, sw13_ref, sw2_ref, yg_ref, o_ref):
    ds_ = sw2_ref.shape[0]
    hb = _unpack_rows(h2_ref[...]).astype(BF16)
    u = jnp.dot(hb, sw13_ref[...], preferred_element_type=F32)
    hmid = _silu(u[:, :ds_]) * u[:, ds_:]
    y = jnp.dot(hmid.astype(BF16), sw2_ref[...], preferred_element_type=F32)

    gate = gate_ref[...]
    moe = gate[:, 0:1] * _unpack_rows(yg_ref[0])
    for k in range(1, TOP_K):
        moe = moe + gate[:, k:k + 1] * _unpack_rows(yg_ref[k])
    xo = x1_ref[...] + mod_ref[0, 5:6, :] * (moe + y)
    ms = jnp.mean(xo * xo, axis=-1, keepdims=True)
    o_ref[...] = xo * lax.rsqrt(ms + EPS) * gain_ref[...]


def _combine(gate_tok, x1, h2, mods, gain, sw13, sw2, yg, tc, tiles_per_batch):
    n, d = x1.shape
    dw = h2.shape[1]
    nt = n // tc
    ds2 = sw13.shape[1]
    ds_ = sw2.shape[0]
    return pl.pallas_call(
        _combine_kernel,
        out_shape=jax.ShapeDtypeStruct((n, d), F32),
        grid=(nt,),
        in_specs=[pl.BlockSpec((tc, TOP_K), lambda i: (i, 0)),
                  pl.BlockSpec((tc, d), lambda i: (i, 0)),
                  pl.BlockSpec((tc, dw), lambda i: (i, 0)),
                  pl.BlockSpec((1, 8, d), lambda i: (i // tiles_per_batch, 0, 0)),
                  pl.BlockSpec((1, d), lambda i: (0, 0)),
                  pl.BlockSpec((d, ds2), lambda i: (0, 0)),
                  pl.BlockSpec((ds_, d), lambda i: (0, 0)),
                  pl.BlockSpec((TOP_K, tc, dw), lambda i: (0, i, 0))],
        out_specs=pl.BlockSpec((tc, d), lambda i: (i, 0)),
        compiler_params=_cparams(("arbitrary",)),
        name="combine",
    )(gate_tok, x1, h2, mods, gain, sw13, sw2, yg)


def _prep_w_in(w, lru_w, hg_w):
    z0 = 2 * lru_w + 2 * hg_w
    col = jnp.arange(w.shape[1])
    scale = jnp.where((col >= z0) & (col < z0 + 2 * hg_w), 0.5, 1.0).astype(w.dtype)
    return (w * scale[None, :]).astype(BF16)


def _block_diag_pairs(w, per):
    nb, bd, _ = w.shape
    w = w.reshape(nb // per, per, bd, bd)
    eye = jnp.eye(per, dtype=w.dtype)
    out = jnp.einsum("gpij,pq->gpiqj", w, eye)
    return out.reshape(nb // per, per * bd, per * bd)


def kernel(x, c, ctx, c_ctx, ada_w, ada_b, norm_mix, norm_ffn, norm_final, w_in, w_out,
           lru_conv_w, lru_conv_b, lru_wa, lru_ba, lru_wx, lru_bx, lru_lambda,
           hgrn_lb_logits, hgrn_norm, router_w, router_b, exp_w13, exp_w2, shared_w13, shared_w2):
    assert ada_w.shape[0] == 1, "single-layer block"
    b, s, d = x.shape
    n = b * s
    lru_w = lru_conv_w.shape[2]
    n_experts = router_w.shape[2]

    rows = -(-(b + 1) // SUBLANES) * SUBLANES
    cs = jnp.zeros((rows, d), F32).at[:b].set(c).at[b].set(c_ctx)
    mod = _ada(cs, ada_w[0], ada_b[0][None, :]).reshape(rows, 6, d)
    mods = jnp.pad(mod, ((0, 0), (0, 2), (0, 0)))

    w_in_bf = _prep_w_in(w_in[0], lru_w, hgrn_norm.shape[1])
    gain1 = norm_mix[0][None, :]
    projx = _inproj(x, mods, gain1, w_in_bf, PROJ_TILE, shared_mod=False)
    projc = _inproj(ctx, mods[b:b + 1], gain1, w_in_bf, ctx.shape[1], shared_mod=True)

    per = LRU_G // (lru_w // LRU_BLOCKS)
    wg = jnp.stack([_block_diag_pairs(lru_wa[0, 0], per), _block_diag_pairs(lru_wx[0, 0], per),
                    _block_diag_pairs(lru_wa[0, 1], per), _block_diag_pairs(lru_wx[0, 1], per)]
                   )
    wg = (0.5 * wg).astype(BF16)
    bg = 0.5 * jnp.stack([lru_ba[0, 0], lru_bx[0, 0], lru_ba[0, 1], lru_bx[0, 1]])
    ylru = _lru(projx, projc, lru_conv_w[0], lru_conv_b[0][None, :], wg, bg, lru_lambda[0])

    yhg = _hgrn(projx, projc, hgrn_lb_logits, hgrn_norm[0][None, :], (2 * lru_w) // LANES)

    wo = w_out[0].astype(BF16).reshape(2, lru_w, d)
    x1, h2, eid_t, gate_t, rank_t, cnt = _outproj(
        ylru, yhg, x, mods, norm_ffn[0][None, :], wo,
        router_w[0].T.astype(BF16), router_b[0][:, None], PROJ_TILE)

    bm = GMM_BLOCK
    counts = cnt[:, 0]
    padded = (counts + bm - 1) // bm * bm
    pends = jnp.cumsum(padded)
    pstarts = pends - padded
    pos = _pos(pstarts, eid_t, rank_t, POS_TILE)
    cap = n * TOP_K + n_experts * bm

    xs = _padfill(_sc_scatter_rows(h2, pos, cap), pstarts + counts, padded - counts, bm)
    ys = _gmm(xs, exp_w13[0], exp_w2[0], pstarts // bm, padded // bm, bm)

    yg = _sc_gather_rows(ys, pos.reshape(-1), SC_GATHER_WINDOW).reshape(TOP_K, n, d // 2)
    tc = COMBINE_TILE
    out = _combine(gate_t.T, x1.reshape(n, d), h2, mods,
                   norm_final[None, :], shared_w13[0].astype(BF16), shared_w2[0].astype(BF16),
                   yg, tc, s // tc)
    return out.reshape(b, s, d)
```

```python
import functools

import jax
import jax.numpy as jnp
from jax import lax
from jax.experimental import pallas as pl
from jax.experimental.pallas import tpu as pltpu
from jax.experimental.pallas import tpu_sc as plsc

F32 = jnp.float32
BF16 = jnp.bfloat16
I32 = jnp.int32
U32 = jnp.uint32

EPS = 1e-6
LRU_C = 8.0
ROUTE_SCALE = 2.5
GRID_W = 64
HG_CHUNK = 32
N_GROUPS = 8
TOPK_GROUPS = 4
TOP_K = 8
LRU_BLOCKS = 8
HG_HEADS = 4
CONV_W = 4
CONV_PAD_L = 1

SUBLANES = 8
LANES = 128
N_SEG = SUBLANES
VMEM_LIMIT = 56 * 1024 * 1024

PROJ_TILE = 512
ROUTE_PARTS = 2
POS_TILE = 2048
GMM_BLOCK = 256
SC_GATHER_WINDOW = 64
COMBINE_TILE = 512


def _cparams(sem, vmem=VMEM_LIMIT):
    return pltpu.CompilerParams(dimension_semantics=sem, vmem_limit_bytes=vmem)


def _sigmoid(x):
    return jax.nn.sigmoid(x)


def _silu(x):
    return x * _sigmoid(x)


def _pack_rows(x):
    w = x.shape[1] // 2
    bits = pltpu.bitcast(x.astype(BF16).astype(F32), U32)
    return (bits[:, :w] >> 16) | (bits[:, w:] & jnp.uint32(0xFFFF0000))


def _unpack_rows(p):
    lo = pltpu.bitcast(p << 16, F32)
    hi = pltpu.bitcast(p & jnp.uint32(0xFFFF0000), F32)
    return jnp.concatenate([lo, hi], axis=1)


def _gelu_tanh(x):
    c = 0.7978845608028654
    return 0.5 * x * (1.0 + jnp.tanh(c * (x + 0.044715 * (x * x * x))))


def _ada_kernel(c_ref, w_ref, b_ref, o_ref):
    s = _silu(c_ref[...])
    o_ref[...] = jnp.dot(s, w_ref[...], preferred_element_type=F32,
                         precision=lax.Precision.HIGHEST) + b_ref[...]


def _ada(cs, w, b):
    rows, d = cs.shape
    n = w.shape[1]
    bn = 1024
    return pl.pallas_call(
        _ada_kernel,
        out_shape=jax.ShapeDtypeStruct((rows, n), F32),
        grid=(n // bn,),
        in_specs=[pl.BlockSpec((rows, d), lambda j: (0, 0)),
                  pl.BlockSpec((d, bn), lambda j: (0, j)),
                  pl.BlockSpec((1, bn), lambda j: (0, j))],
        out_specs=pl.BlockSpec((rows, bn), lambda j: (0, j)),
        compiler_params=_cparams(("arbitrary",)),
        name="ada",
    )(cs, w, b)


def _inproj_kernel(x_ref, mod_ref, gain_ref, w_ref, o_ref):
    x = x_ref[0]
    ms = jnp.mean(x * x, axis=-1, keepdims=True)
    y = x * lax.rsqrt(ms + EPS) * gain_ref[...]
    h = y * (1.0 + mod_ref[0, 1:2, :]) + mod_ref[0, 0:1, :]
    o_ref[0] = jnp.dot(h.astype(BF16), w_ref[...], preferred_element_type=F32)


def _inproj(x, mods, gain, w_bf, tm, shared_mod):
    b, s, d = x.shape
    n = w_bf.shape[1]
    mod_map = (lambda bi, i: (0, 0, 0)) if shared_mod else (lambda bi, i: (bi, 0, 0))
    return pl.pallas_call(
        _inproj_kernel,
        out_shape=jax.ShapeDtypeStruct((b, s, n), F32),
        grid=(b, s // tm),
        in_specs=[pl.BlockSpec((1, tm, d), lambda bi, i: (bi, i, 0)),
                  pl.BlockSpec((1, 8, d), mod_map),
                  pl.BlockSpec((1, d), lambda bi, i: (0, 0)),
                  pl.BlockSpec((d, n), lambda bi, i: (0, 0))],
        out_specs=pl.BlockSpec((1, tm, n), lambda bi, i: (bi, i, 0)),
        compiler_params=_cparams(("arbitrary", "arbitrary")),
        name="inproj",
    )(x, mods, gain, w_bf)


LRU_G = 256
LRU_CHUNK = 256


def _seg_rows(s, seg_len):
    return pl.ds(s, seg_len, stride=N_SEG)


def _out_pitch(seg_len):
    return seg_len + SUBLANES


def _lane_store(ref, d, rows, val):
    nl = val.shape[-1] // LANES
    for l in range(nl):
        ref[d * nl + l, rows, :] = val[:, l * LANES:(l + 1) * LANES]


def _lane_load(ref, d, rows, nl):
    return jnp.concatenate([ref[d * nl + l, rows, :] for l in range(nl)], axis=-1)


LRU_CONV_PIECE = 64


def _lru_coeffs(pad_ref, t0, rows, cw, cb, wg_ref, bg, sp, a_ref, b_ref, dst_rows, u_ref):
    for p0 in range(0, rows, LRU_CONV_PIECE):
        win = pad_ref[pl.ds(t0 + p0, LRU_CONV_PIECE + 2 * SUBLANES), :]
        u = cb
        for k in range(CONV_W):
            off = SUBLANES - CONV_PAD_L + k
            u = u + win[off:off + LRU_CONV_PIECE, :] * cw[k:k + 1, :]
        u_ref[p0:p0 + LRU_CONV_PIECE, :] = u
    u = u_ref[0:rows, :]
    ub = u.astype(BF16)
    half_u = 0.5 * u
    for d in range(2):
        tr = jnp.tanh(jnp.dot(ub, wg_ref[2 * d, 0], preferred_element_type=F32)
                      + bg[2 * d:2 * d + 1, :])
        ti = jnp.tanh(jnp.dot(ub, wg_ref[2 * d + 1, 0], preferred_element_type=F32)
                      + bg[2 * d + 1:2 * d + 2, :])
        half_c = (-0.5 * LRU_C) * sp[d:d + 1, :]
        log_a = half_c * tr + half_c
        a = jnp.exp(log_a)
        one_minus_a2 = -jnp.tanh(log_a) * (1.0 + a * a)
        root = jnp.where(one_minus_a2 > 0.0, one_minus_a2 * lax.rsqrt(one_minus_a2), 0.0)
        bb = root * (ti * half_u + half_u)
        _lane_store(a_ref, d, dst_rows, a)
        _lane_store(b_ref, d, dst_rows, bb)


def _seg_scan(a_ref, b_ref, h_ref, p_ref, seg_len, unroll):
    n_lead = a_ref.shape[0]
    nl = n_lead // 2
    zero = jnp.zeros((N_SEG, LANES), F32)
    one = jnp.ones((N_SEG, LANES), F32)
    init = tuple((zero, one) for _ in range(n_lead))

    def step(t, carry):
        out = []
        for i in range(n_lead):
            h, p = carry[i]
            tt = t if i < nl else seg_len - 1 - t
            rows = pl.ds(pl.multiple_of(tt * N_SEG, N_SEG), N_SEG)
            a = a_ref[i, rows, :]
            h = a * h + b_ref[i, rows, :]
            p = p * a
            if h_ref is not None:
                out_rows = pl.ds(tt, N_SEG, stride=_out_pitch(seg_len))
                h_ref[i, out_rows, :] = h
                p_ref[i, out_rows, :] = p
            out.append((h, p))
        return tuple(out)

    def body(i, carry):
        for j in range(unroll):
            carry = step(i * unroll + j, carry)
        return carry

    ends = lax.fori_loop(0, seg_len // unroll, body, init)
    h_end = [jnp.concatenate([ends[d * nl + l][0] for l in range(nl)], axis=-1) for d in range(2)]
    p_end = [jnp.concatenate([ends[d * nl + l][1] for l in range(nl)], axis=-1) for d in range(2)]
    return h_end, p_end


def _seg_carries(h_end, p_end, h0, reverse):
    order = range(N_SEG - 1, -1, -1) if reverse else range(N_SEG)
    cin = [None] * N_SEG
    c = h0
    for s in order:
        cin[s] = c
        c = p_end[s:s + 1, :] * c + h_end[s:s + 1, :]
    return cin, c


def _lru_kernel(rx_ref, rg_ref, rc_ref, cw_ref, cb_ref, wg_ref, bg_ref, lam_ref, o_ref,
                padl, padc, u_s, a_l, b_l, h_l, p_l, a_c, b_c):
    s_len = rx_ref.shape[1]
    c_len = rc_ref.shape[1]
    g = rx_ref.shape[2]
    nl = g // LANES
    seg_l = s_len // N_SEG
    seg_c = c_len // N_SEG

    zeros = jnp.zeros((SUBLANES, g), F32)
    padl[0:SUBLANES, :] = zeros
    padl[SUBLANES + s_len:2 * SUBLANES + s_len, :] = zeros
    padc[0:SUBLANES, :] = zeros
    padc[SUBLANES + c_len:2 * SUBLANES + c_len, :] = zeros
    for i in range(s_len // LRU_CHUNK):
        padl[SUBLANES + i * LRU_CHUNK:SUBLANES + (i + 1) * LRU_CHUNK, :] = (
            rx_ref[0, i * LRU_CHUNK:(i + 1) * LRU_CHUNK, :])
    padc[SUBLANES:SUBLANES + c_len, :] = rc_ref[0]

    cw = cw_ref[...]
    cb = cb_ref[...]
    bg = bg_ref[...]
    x = -lam_ref[...]
    sp = jnp.maximum(x, 0.0) + jnp.log(1.0 + jnp.exp(-jnp.abs(x)))

    _lru_coeffs(padc, 0, c_len, cw, cb, wg_ref, bg, sp, a_l, b_l, pl.ds(0, c_len), u_s)
    for i in range(2 * nl):
        for s in range(N_SEG):
            a_c[i, _seg_rows(s, seg_c), :] = a_l[i, s * seg_c:(s + 1) * seg_c, :]
            b_c[i, _seg_rows(s, seg_c), :] = b_l[i, s * seg_c:(s + 1) * seg_c, :]
    h_end, p_end = _seg_scan(a_c, b_c, None, None, seg_c, unroll=4)
    zero_row = jnp.zeros((1, g), F32)
    _, h0_f = _seg_carries(h_end[0], p_end[0], zero_row, reverse=False)
    _, h0_b = _seg_carries(h_end[1], p_end[1], zero_row, reverse=True)

    for s in range(N_SEG):
        _lru_coeffs(padl, s * seg_l, seg_l, cw, cb, wg_ref, bg, sp, a_l, b_l,
                    _seg_rows(s, seg_l), u_s)
    h_end, p_end = _seg_scan(a_l, b_l, h_l, p_l, seg_l, unroll=4)
    cin_f, _ = _seg_carries(h_end[0], p_end[0], h0_f, reverse=False)
    cin_b, _ = _seg_carries(h_end[1], p_end[1], h0_b, reverse=True)

    for s in range(N_SEG):
        rows = pl.ds(s * _out_pitch(seg_l), seg_l)
        h = (_lane_load(h_l, 0, rows, nl) + _lane_load(p_l, 0, rows, nl) * cin_f[s]) + (
            _lane_load(h_l, 1, rows, nl) + _lane_load(p_l, 1, rows, nl) * cin_b[s])
        o_ref[0, s * seg_l:(s + 1) * seg_l, :] = _gelu_tanh(rg_ref[0, s * seg_l:(s + 1) * seg_l, :]) * h


def _lru(projx, projc, cw, cb, wg, bg, lam):
    b, s_len, _ = projx.shape
    c_len = projc.shape[1]
    w = cw.shape[1]
    g = LRU_G
    ng = w // g
    seg_l = s_len // N_SEG
    seg_c = c_len // N_SEG
    assert seg_l == LRU_CHUNK and c_len % (N_SEG * 1) == 0
    return pl.pallas_call(
        _lru_kernel,
        out_shape=jax.ShapeDtypeStruct((b, s_len, w), F32),
        grid=(b, ng),
        in_specs=[pl.BlockSpec((1, s_len, g), lambda bi, j: (bi, 0, j)),
                  pl.BlockSpec((1, s_len, g), lambda bi, j: (bi, 0, ng + j)),
                  pl.BlockSpec((1, c_len, g), lambda bi, j: (bi, 0, j)),
                  pl.BlockSpec((CONV_W, g), lambda bi, j: (0, j)),
                  pl.BlockSpec((1, g), lambda bi, j: (0, j)),
                  pl.BlockSpec((4, 1, g, g), lambda bi, j: (0, j, 0, 0)),
                  pl.BlockSpec((4, g), lambda bi, j: (0, j)),
                  pl.BlockSpec((2, g), lambda bi, j: (0, j))],
        out_specs=pl.BlockSpec((1, s_len, g), lambda bi, j: (bi, 0, j)),
        scratch_shapes=[pltpu.VMEM((s_len + 2 * SUBLANES, g), F32),
                        pltpu.VMEM((c_len + 2 * SUBLANES, g), F32),
                        pltpu.VMEM((max(seg_l, c_len), g), F32),
                        pltpu.VMEM((2 * g // LANES, s_len, LANES), F32),
                        pltpu.VMEM((2 * g // LANES, s_len, LANES), F32),
                        pltpu.VMEM((2 * g // LANES, N_SEG * _out_pitch(seg_l), LANES), F32),
                        pltpu.VMEM((2 * g // LANES, N_SEG * _out_pitch(seg_l), LANES), F32),
                        pltpu.VMEM((2 * g // LANES, c_len, LANES), F32),
                        pltpu.VMEM((2 * g // LANES, c_len, LANES), F32)],
        compiler_params=_cparams(("arbitrary", "arbitrary")),
        name="lru",
    )(projx, projx, projc, cw, cb, wg, bg, lam)


LOG2_E = 1.4426950408889634
HG_PITCH_L = GRID_W + SUBLANES
HG_PITCH_C = SUBLANES


def _hg_gates(half_z, lb):
    c = 0.5 * (1.0 - lb)
    ct = c * jnp.tanh(half_z)
    logf = jnp.log(0.5 * (1.0 + lb) + ct)
    k = c - ct
    return logf, k


def _hg_prepass(load, width, pitch, lb, qd, kd, kl, vs, dec):
    n_pos = HG_CHUNK
    for d, zname in enumerate(("zf", "zb")):
        order = range(n_pos) if d == 0 else range(n_pos - 1, -1, -1)
        g = jnp.zeros((width, LANES), F32)
        lbd = lb[d:d + 1, :]
        for p in order:
            logf, k = _hg_gates(load(zname, p), lbd)
            g = g + logf
            qd[d, p * pitch:p * pitch + width, :] = g
            kl[d, p * pitch:p * pitch + width, :] = k
        g_last = g * LOG2_E
        dec[d, 0:width, :] = jnp.exp2(g_last)
        for p in range(n_pos):
            rows = slice(p * pitch, p * pitch + width)
            gp = qd[d, rows, :] * LOG2_E
            k = kl[d, rows, :]
            qd[d, rows, :] = load("q", p) * jnp.exp2(gp)
            kd[d, rows, :] = k * jnp.exp2(-gp)
            kl[d, rows, :] = k * jnp.exp2(g_last - gp)
    for p in range(n_pos):
        vs[p * pitch:p * pitch + width, :] = load("v", p)


HG_GROUP = 4
_NT = (((1,), (1,)), ((), ()))
_TN = (((0,), (0,)), ((), ()))


def _hg_group_rows(ref, lead, c0, pitch):
    parts = []
    for j in range(HG_GROUP):
        rows = pl.ds(c0 + j, HG_CHUNK, stride=pitch)
        parts.append(ref[rows, :] if lead is None else ref[lead, rows, :])
    return jnp.concatenate(parts, axis=0)


def _hg_scores(g, pitch, qd, kd, masks, sc_ref):
    c0 = g * HG_GROUP
    total = None
    for d in range(2):
        q = _hg_group_rows(qd, d, c0, pitch).astype(BF16)
        k = _hg_group_rows(kd, d, c0, pitch).astype(BF16)
        sc = jnp.where(masks[d], lax.dot_general(q, k, _NT, preferred_element_type=F32), 0.0)
        total = sc if total is None else total + sc
    sc_ref[g] = total.astype(BF16)


def _hg_summaries(g, pitch, kl, vs, sc_ref, oi, kv):
    c0 = g * HG_GROUP
    v = _hg_group_rows(vs, None, c0, pitch).astype(BF16)
    if oi is not None:
        o = jnp.dot(sc_ref[g], v, preferred_element_type=F32)
        for j in range(HG_GROUP):
            oi[pl.ds(c0 + j, HG_CHUNK, stride=pitch), :] = o[j * HG_CHUNK:(j + 1) * HG_CHUNK, :]
    kls = jnp.concatenate([_hg_group_rows(kl, 0, c0, pitch), _hg_group_rows(kl, 1, c0, pitch)],
                          axis=1).astype(BF16)
    for j in range(HG_GROUP):
        sl = slice(j * HG_CHUNK, (j + 1) * HG_CHUNK)
        both = lax.dot_general(v[sl], kls[sl], _TN, preferred_element_type=F32)
        kv[0, c0 + j] = both[:, :LANES]
        kv[1, c0 + j] = both[:, LANES:]


def _hg_steps(step0, n_steps, n_chunks, pitch, qd, dec, kv, states, ox):
    states = list(states)
    for u in range(n_steps):
        cs = (step0 + u, n_chunks - 1 - (step0 + u))
        if ox is not None:
            rows = [pl.ds(c, HG_CHUNK, stride=pitch) for c in cs]
            q2 = jnp.concatenate([qd[0, rows[0], :], qd[1, rows[1], :]], axis=0).astype(BF16)
            s2 = jnp.concatenate(states, axis=0).astype(BF16)
            o2 = lax.dot_general(q2, s2, _NT, preferred_element_type=F32)
            ox[0, rows[0], :] = o2[:HG_CHUNK, :LANES]
            ox[1, rows[1], :] = o2[HG_CHUNK:, LANES:]
        for d in range(2):
            states[d] = states[d] * dec[d, pl.ds(cs[d], 1), :] + kv[d, cs[d]]
    return tuple(states)


def _hgrn_kernel(q_ref, v_ref, zf_ref, zb_ref, g_ref, qc_ref, vc_ref, zfc_ref, zbc_ref,
                 lbl_ref, gain_ref, o_ref, qd, kd, kl, vs, dec, oi, ox, kv, sc):
    s_len = q_ref.shape[1]
    c_len = qc_ref.shape[1]
    n_col = s_len // HG_CHUNK
    n_cc = c_len // HG_CHUNK
    assert n_col == GRID_W and n_cc == HG_PITCH_C
    assert n_col % HG_GROUP == 0 and n_cc % HG_GROUP == 0

    lg = lbl_ref[...]
    m = jnp.max(lg, axis=1, keepdims=True)
    ex = jnp.exp(lg - m)
    lb = ex[:, 0, :] / jnp.sum(ex, axis=1)

    gr = HG_GROUP * HG_CHUNK
    ii = lax.broadcasted_iota(I32, (gr, gr), 0)
    jj = lax.broadcasted_iota(I32, (gr, gr), 1)
    same = (ii // HG_CHUNK) == (jj // HG_CHUNK)
    masks = (jnp.logical_and(same, jj <= ii), jnp.logical_and(same, jj >= ii))

    c_refs = {"q": qc_ref, "v": vc_ref, "zf": zfc_ref, "zb": zbc_ref}

    def load_c(name, p):
        return c_refs[name][0, pl.ds(p, n_cc, stride=HG_CHUNK), :]

    _hg_prepass(load_c, n_cc, HG_PITCH_C, lb, qd, kd, kl, vs, dec)
    for g in range(n_cc // HG_GROUP):
        _hg_summaries(g, HG_PITCH_C, kl, vs, None, None, kv)
    zero = jnp.zeros((LANES, LANES), F32)
    states = _hg_steps(0, n_cc, n_cc, HG_PITCH_C, qd, dec, kv, (zero, zero), None)

    l_refs = {"q": q_ref, "v": v_ref, "zf": zf_ref, "zb": zb_ref}

    def load_l(name, p):
        return l_refs[name][0, p * n_col:(p + 1) * n_col, :]

    _hg_prepass(load_l, n_col, HG_PITCH_L, lb, qd, kd, kl, vs, dec)

    n_groups = n_col // HG_GROUP
    for g in range(n_groups):
        _hg_scores(g, HG_PITCH_L, qd, kd, masks, sc)
        if g > 0:
            _hg_summaries(g - 1, HG_PITCH_L, kl, vs, sc, oi, kv)
    _hg_summaries(n_groups - 1, HG_PITCH_L, kl, vs, sc, oi, kv)
    _hg_steps(0, n_col, n_col, HG_PITCH_L, qd, dec, kv, states, ox)

    gain = gain_ref[...]
    for r in range(HG_CHUNK):
        rows = slice(r * HG_PITCH_L, r * HG_PITCH_L + n_col)
        o = oi[rows, :] + (ox[0, rows, :] + ox[1, rows, :])
        y = o * lax.rsqrt(jnp.mean(o * o, axis=-1, keepdims=True) + EPS) * gain
        o_ref[0, r * n_col:(r + 1) * n_col, :] = y * _silu(g_ref[0, r * n_col:(r + 1) * n_col, :])


def _hgrn(projx, projc, lbl, gain, col0):
    b, s_len, _ = projx.shape
    c_len = projc.shape[1]
    hw = gain.shape[1]
    nh = hw // LANES
    n_slots = lbl.shape[1]

    def xs(k):
        return pl.BlockSpec((1, s_len, LANES), lambda bi, h, k=k: (bi, 0, col0 + k * nh + h))

    def cs(k):
        return pl.BlockSpec((1, c_len, LANES), lambda bi, h, k=k: (bi, 0, col0 + k * nh + h))

    sc_rows = HG_CHUNK * HG_PITCH_L
    return pl.pallas_call(
        _hgrn_kernel,
        out_shape=jax.ShapeDtypeStruct((b, s_len, hw), F32),
        grid=(b, nh),
        in_specs=[xs(0), xs(1), xs(2), xs(3), xs(4), cs(0), cs(1), cs(2), cs(3),
                  pl.BlockSpec((2, n_slots, LANES), lambda bi, h: (0, 0, h)),
                  pl.BlockSpec((1, LANES), lambda bi, h: (0, h))],
        out_specs=pl.BlockSpec((1, s_len, LANES), lambda bi, h: (bi, 0, h)),
        scratch_shapes=[pltpu.VMEM((2, sc_rows, LANES), F32),
                        pltpu.VMEM((2, sc_rows, LANES), F32),
                        pltpu.VMEM((2, sc_rows, LANES), F32),
                        pltpu.VMEM((sc_rows, LANES), F32),
                        pltpu.VMEM((2, HG_PITCH_L, LANES), F32),
                        pltpu.VMEM((sc_rows, LANES), F32),
                        pltpu.VMEM((2, sc_rows, LANES), F32),
                        pltpu.VMEM((2, GRID_W, LANES, LANES), F32),
                        pltpu.VMEM((GRID_W // HG_GROUP, HG_GROUP * HG_CHUNK,
                                    HG_GROUP * HG_CHUNK), BF16)],
        compiler_params=_cparams(("arbitrary", "arbitrary")),
        name="hgrn",
    )(projx, projx, projx, projx, projx, projc, projc, projc, projc, lbl, gain)


def _first_index_of_max(vals, iota, big):
    m = jnp.max(vals, axis=0, keepdims=True)
    idx = jnp.min(jnp.where(vals == m, iota, big), axis=0, keepdims=True)
    return m, idx


def _route(logits_t, rb, n_experts):
    t = logits_t.shape[1]
    gsz = n_experts // N_GROUPS
    neg = -jnp.inf
    scores = _sigmoid(logits_t)
    biased = scores + rb
    iota_g = lax.broadcasted_iota(I32, (gsz, t), 0)
    gscore = []
    for gi in range(N_GROUPS):
        blk = biased[gi * gsz:(gi + 1) * gsz, :]
        m1, i1 = _first_index_of_max(blk, iota_g, gsz)
        m2 = jnp.max(jnp.where(iota_g == i1, neg, blk), axis=0, keepdims=True)
        gscore.append(m1 + m2)
    gs = jnp.concatenate(gscore, axis=0)
    iota_n = lax.broadcasted_iota(I32, (N_GROUPS, t), 0)
    gsel = jnp.zeros((N_GROUPS, t), jnp.bool_)
    for _ in range(TOPK_GROUPS):
        _, gi1 = _first_index_of_max(gs, iota_n, N_GROUPS)
        hit = iota_n == gi1
        gsel = jnp.logical_or(gsel, hit)
        gs = jnp.where(hit, neg, gs)
    emask = jnp.concatenate(
        [jnp.broadcast_to(gsel[gi:gi + 1, :], (gsz, t)) for gi in range(N_GROUPS)], axis=0)
    masked = jnp.where(emask, biased, neg)
    parts = ROUTE_PARTS if t % (ROUTE_PARTS * LANES) == 0 else 1
    tp = t // parts
    iota_e = lax.broadcasted_iota(I32, (n_experts, tp), 0)
    cols = [slice(p * tp, (p + 1) * tp) for p in range(parts)]
    masked_p = [masked[:, c] for c in cols]
    scores_p = [scores[:, c] for c in cols]
    sel_p = [jnp.zeros((n_experts, tp), F32) for _ in cols]
    ids_p = [[] for _ in cols]
    gates_p = [[] for _ in cols]
    for _ in range(TOP_K):
        for p in range(parts):
            _, ei = _first_index_of_max(masked_p[p], iota_e, n_experts)
            hit = iota_e == ei
            ids_p[p].append(ei)
            gates_p[p].append(jnp.sum(jnp.where(hit, scores_p[p], 0.0), axis=0, keepdims=True))
            sel_p[p] = jnp.where(hit, 1.0, sel_p[p])
            masked_p[p] = jnp.where(hit, neg, masked_p[p])
    ids = jnp.concatenate([jnp.concatenate(i, axis=0) for i in ids_p], axis=1)
    gates = jnp.concatenate([jnp.concatenate(g, axis=0) for g in gates_p], axis=1)
    sel = jnp.concatenate(sel_p, axis=1)
    gates = gates / jnp.sum(gates, axis=0, keepdims=True) * ROUTE_SCALE
    return ids, gates, sel


def _outproj_kernel(yl_ref, yh_ref, x_ref, mod_ref, gain_ref, wo_ref, rwt_ref, rb_ref,
                    x1_ref, h2_ref, eid_ref, gate_ref, rank_ref, cnt_ref, carry):
    first = jnp.logical_and(pl.program_id(0) == 0, pl.program_id(1) == 0)

    @pl.when(first)
    def _():
        carry[...] = jnp.zeros(carry.shape, F32)

    y = jnp.dot(yl_ref[0].astype(BF16), wo_ref[0], preferred_element_type=F32)
    y = y + jnp.dot(yh_ref[0].astype(BF16), wo_ref[1], preferred_element_type=F32)
    x1 = x_ref[0] + mod_ref[0, 2:3, :] * y
    x1_ref[0] = x1
    ms = jnp.mean(x1 * x1, axis=-1, keepdims=True)
    h2 = x1 * lax.rsqrt(ms + EPS) * gain_ref[...]
    h2 = h2 * (1.0 + mod_ref[0, 4:5, :]) + mod_ref[0, 3:4, :]
    h2_ref[...] = _pack_rows(h2)

    n_experts = rwt_ref.shape[0]
    t = h2.shape[0]
    logits_t = lax.dot_general(rwt_ref[...], h2.astype(BF16), _NT,
                               preferred_element_type=F32)
    ids, gates, sel = _route(logits_t, rb_ref[...], n_experts)
    eid_ref[...] = ids
    gate_ref[...] = gates

    ti = lax.broadcasted_iota(I32, (t, t), 0)
    tj = lax.broadcasted_iota(I32, (t, t), 1)
    upper = (ti < tj).astype(BF16)
    selb = sel.astype(BF16)
    before = jnp.dot(selb, upper, preferred_element_type=F32) + carry[:, 0:1]
    iota_e = lax.broadcasted_iota(I32, (n_experts, t), 0)
    ranks = [jnp.sum(jnp.where(iota_e == ids[k:k + 1, :], before, 0.0), axis=0, keepdims=True)
             for k in range(TOP_K)]
    rank_ref[...] = jnp.concatenate(ranks, axis=0).astype(I32)
    total = carry[...] + jnp.dot(selb, jnp.ones((t, LANES), BF16), preferred_element_type=F32)
    carry[...] = total
    cnt_ref[...] = total.astype(I32)


def _outproj(ylru, yhg, x, mods, gain, wo, rwt, rb, tm):
    b, s, d = x.shape
    hw = ylru.shape[2]
    e = rwt.shape[0]
    n = b * s
    nt = s // tm
    tok = lambda bi, i: (0, bi * nt + i)
    return pl.pallas_call(
        _outproj_kernel,
        out_shape=(jax.ShapeDtypeStruct((b, s, d), F32),
                   jax.ShapeDtypeStruct((n, d // 2), U32),
                   jax.ShapeDtypeStruct((TOP_K, n), I32),
                   jax.ShapeDtypeStruct((TOP_K, n), F32),
                   jax.ShapeDtypeStruct((TOP_K, n), I32),
                   jax.ShapeDtypeStruct((e, LANES), I32)),
        grid=(b, nt),
        in_specs=[pl.BlockSpec((1, tm, hw), lambda bi, i: (bi, i, 0)),
                  pl.BlockSpec((1, tm, hw), lambda bi, i: (bi, i, 0)),
                  pl.BlockSpec((1, tm, d), lambda bi, i: (bi, i, 0)),
                  pl.BlockSpec((1, 8, d), lambda bi, i: (bi, 0, 0)),
                  pl.BlockSpec((1, d), lambda bi, i: (0, 0)),
                  pl.BlockSpec((2, hw, d), lambda bi, i: (0, 0, 0)),
                  pl.BlockSpec((e, d), lambda bi, i: (0, 0)),
                  pl.BlockSpec((e, 1), lambda bi, i: (0, 0))],
        out_specs=(pl.BlockSpec((1, tm, d), lambda bi, i: (bi, i, 0)),
                   pl.BlockSpec((tm, d // 2), lambda bi, i: (bi * nt + i, 0)),
                   pl.BlockSpec((TOP_K, tm), tok),
                   pl.BlockSpec((TOP_K, tm), tok),
                   pl.BlockSpec((TOP_K, tm), tok),
                   pl.BlockSpec((e, LANES), lambda bi, i: (0, 0))),
        scratch_shapes=[pltpu.VMEM((e, LANES), F32)],
        compiler_params=_cparams(("arbitrary", "arbitrary")),
        name="outproj",
    )(ylru, yhg, x, mods, gain, wo, rwt, rb)


def _pos_kernel(pstart, eid_ref, rank_ref, pos_ref):
    eid = eid_ref[...]

    def body(e, acc):
        return jnp.where(eid == e, pstart[e], acc)

    base = lax.fori_loop(0, pstart.shape[0], body, jnp.zeros(eid.shape, I32))
    pos_ref[...] = base + rank_ref[...]


def _pos(pstarts, eid_t, rank_t, tn):
    k, n = eid_t.shape
    spec = pl.BlockSpec((k, tn), lambda i, ps: (0, i))
    return pl.pallas_call(
        _pos_kernel,
        out_shape=jax.ShapeDtypeStruct((k, n), I32),
        grid_spec=pltpu.PrefetchScalarGridSpec(
            num_scalar_prefetch=1, grid=(n // tn,), in_specs=[spec, spec], out_specs=spec),
        compiler_params=_cparams(("arbitrary",)),
        name="pos",
    )(pstarts, eid_t, rank_t)


def _pad_fill(padstart, padlen, zbuf, xs_ref, sem, wait):
    nbits = zbuf.shape[0].bit_length() - 1
    low_bits = SUBLANES.bit_length() - 1

    def go(cp):
        if wait:
            cp.wait()
        else:
            cp.start()

    def body(e, carry):
        st = padstart[e]
        ln = padlen[e]
        end = st + ln
        for bit in range(nbits - 1, low_bits - 1, -1):
            size = 1 << bit
            back = ((ln >> (bit + 1)) << (bit + 1)) + size

            @pl.when((ln & size) != 0)
            def _():
                start = pl.multiple_of(end - back, SUBLANES)
                go(pltpu.make_async_copy(zbuf.at[pl.ds(0, size)],
                                         xs_ref.at[pl.ds(start, size)], sem))
        for j in range(SUBLANES - 1):
            @pl.when(j < (ln & (SUBLANES - 1)))
            def _():
                go(pltpu.make_async_copy(zbuf.at[pl.ds(0, 1)], xs_ref.at[pl.ds(st + j, 1)], sem))
        return carry

    lax.fori_loop(0, padstart.shape[0], body, 0)


def _padfill_kernel(padstart, padlen, xs_in, xs_ref, zbuf, sem):
    del xs_in
    zbuf[...] = jnp.zeros(zbuf.shape, zbuf.dtype)
    _pad_fill(padstart, padlen, zbuf, xs_ref, sem, wait=False)
    _pad_fill(padstart, padlen, zbuf, xs_ref, sem, wait=True)


def _padfill(xs, padstart, padlen, bm):
    return pl.pallas_call(
        _padfill_kernel,
        out_shape=jax.ShapeDtypeStruct(xs.shape, xs.dtype),
        grid_spec=pltpu.PrefetchScalarGridSpec(
            num_scalar_prefetch=2,
            grid=(1,),
            in_specs=[pl.BlockSpec(memory_space=pl.ANY)],
            out_specs=pl.BlockSpec(memory_space=pl.ANY),
            scratch_shapes=[pltpu.VMEM((bm, xs.shape[1]), xs.dtype), pltpu.SemaphoreType.DMA(())]),
        input_output_aliases={2: 0},
        compiler_params=_cparams(("arbitrary",)),
        name="padfill",
    )(padstart, padlen, xs)


SC_SCATTER_WINDOW = 128


def _sc_mesh():
    return plsc.VectorSubcoreMesh(core_axis_name="core", subcore_axis_name="subcore")


def _sc_scatter_rows(rows, pos, cap):
    n, dw = rows.shape
    top_k = pos.shape[0]
    mesh = _sc_mesh()
    n_workers = mesh.num_cores * mesh.num_subcores
    win = SC_SCATTER_WINDOW
    per_w = n // n_workers
    n_chunks = per_w // win
    assert per_w * n_workers == n and n_chunks * win == per_w
    pos_w = pos.reshape(top_k, n_workers, n_chunks, win).transpose(1, 2, 0, 3)
    pos_w = pos_w.reshape(n_workers, n_chunks * top_k, win)

    @functools.partial(pl.kernel, mesh=mesh,
                       out_type=jax.ShapeDtypeStruct((cap, dw), rows.dtype),
                       scratch_types=[pltpu.VMEM((n_chunks * top_k, win), I32),
                                      pltpu.VMEM((win, dw), rows.dtype),
                                      pltpu.SemaphoreType.DMA])
    def scatter(r_hbm, p_hbm, o_hbm, idx_v, rows_v, sem):
        wid = lax.axis_index("subcore") * mesh.num_cores + lax.axis_index("core")
        pltpu.sync_copy(p_hbm.at[wid], idx_v)

        @pl.loop(0, n_chunks)
        def _(c):
            pltpu.sync_copy(r_hbm.at[pl.ds(wid * per_w + c * win, win)], rows_v)
            copies = [pltpu.async_copy(rows_v, o_hbm.at[idx_v.at[c * top_k + k]], sem)
                      for k in range(top_k)]
            for cp in copies:
                cp.wait()

    return scatter(rows, pos_w)


GMM_EXPERTS = 4
GMM_PAIR = 4
GMM_SLOTS = 16
GMM_AHEAD = GMM_SLOTS - GMM_PAIR


def _gmm_rows(xw, w13b, w2b, de):
    xb = _unpack_rows(xw).astype(BF16)
    u = jnp.dot(xb, w13b[...], preferred_element_type=F32)
    hmid = _silu(u[:, :de]) * u[:, de:]
    return _pack_rows(jnp.dot(hmid.astype(BF16), w2b[...], preferred_element_type=F32))


def _gmm_kernel(blk0, nblk, w13_ref, w2_ref, xs_ref, ys_ref, w13b, w2b, xbuf, ybuf, isem, osem):
    step = pl.program_id(0)
    n_e = nblk.shape[0]
    slots, bm = xbuf.shape[0], xbuf.shape[1]
    de = w2_ref.shape[1]
    total = blk0[n_e - 1] + nblk[n_e - 1]

    def x_copy(g):
        rows = pl.ds(pl.multiple_of(g * bm, bm), bm)
        slot = g & (slots - 1)
        return pltpu.make_async_copy(xs_ref.at[rows], xbuf.at[slot], isem.at[slot])

    def y_copy(g):
        rows = pl.ds(pl.multiple_of(g * bm, bm), bm)
        slot = g & (slots - 1)
        return pltpu.make_async_copy(ybuf.at[slot], ys_ref.at[rows], osem.at[slot])

    @pl.when(step == 0)
    def _():
        for j in range(GMM_AHEAD):
            @pl.when(j < total)
            def _():
                x_copy(j).start()

    def run_blocks(g0, count):
        gs = [g0 + j for j in range(count)]
        for g in gs:
            x_copy(g).wait()
        for g in gs:
            @pl.when(g + GMM_AHEAD < total)
            def _():
                x_copy(g + GMM_AHEAD).start()

            @pl.when(g >= slots)
            def _():
                y_copy(g - slots).wait()
        xbs = [_unpack_rows(xbuf[g & (slots - 1)]).astype(BF16) for g in gs]
        us = [jnp.dot(xb, w13b[...], preferred_element_type=F32) for xb in xbs]
        hs = [(_silu(u[:, :de]) * u[:, de:]).astype(BF16) for u in us]
        ys = [jnp.dot(h, w2b[...], preferred_element_type=F32) for h in hs]
        for g, y in zip(gs, ys):
            ybuf[g & (slots - 1)] = _pack_rows(y)
        for g in gs:
            y_copy(g).start()

    def expert_body(j, carry):
        e = step * GMM_EXPERTS + j
        nb = nblk[e]
        b0 = blk0[e]
        w13b[...] = w13_ref[j].astype(BF16)
        w2b[...] = w2_ref[j].astype(BF16)

        def pair_body(i, c):
            run_blocks(b0 + GMM_PAIR * i, GMM_PAIR)
            return c

        lax.fori_loop(0, nb // GMM_PAIR, pair_body, 0)
        for r in range(1, GMM_PAIR):
            @pl.when(nb % GMM_PAIR == r)
            def _():
                run_blocks(b0 + nb - r, r)
        return carry

    lax.fori_loop(0, GMM_EXPERTS, expert_body, 0)

    @pl.when(step == pl.num_programs(0) - 1)
    def _():
        for j in range(1, slots + 1):
            @pl.when(total - j >= 0)
            def _():
                y_copy(total - j).wait()


def _gmm(xs, w13, w2, blk0, nblk, bm):
    cap, dw = xs.shape
    e, d, de2 = w13.shape
    de = w2.shape[1]
    return pl.pallas_call(
        _gmm_kernel,
        out_shape=jax.ShapeDtypeStruct((cap, dw), U32),
        grid_spec=pltpu.PrefetchScalarGridSpec(
            num_scalar_prefetch=2,
            grid=(e // GMM_EXPERTS,),
            in_specs=[pl.BlockSpec((GMM_EXPERTS, d, de2), lambda i, b0, nb: (i, 0, 0)),
                      pl.BlockSpec((GMM_EXPERTS, de, d), lambda i, b0, nb: (i, 0, 0)),
                      pl.BlockSpec(memory_space=pl.ANY)],
            out_specs=pl.BlockSpec(memory_space=pl.ANY),
            scratch_shapes=[pltpu.VMEM((d, de2), BF16), pltpu.VMEM((de, d), BF16),
                            pltpu.VMEM((GMM_SLOTS, bm, dw), U32),
                            pltpu.VMEM((GMM_SLOTS, bm, dw), U32),
                            pltpu.SemaphoreType.DMA((GMM_SLOTS,)),
                            pltpu.SemaphoreType.DMA((GMM_SLOTS,))]),
        compiler_params=_cparams(("arbitrary",)),
        name="gmm",
    )(blk0, nblk, w13, w2, xs)


def _sc_gather_rows(table, idx, window):
    n_idx = idx.shape[0]
    dw = table.shape[1]
    mesh = _sc_mesh()
    n_workers = mesh.num_cores * mesh.num_subcores
    per_w = n_idx // n_workers
    n_chunks = per_w // window
    assert per_w * n_workers == n_idx and n_chunks * window == per_w and n_chunks % 2 == 0

    @functools.partial(pl.kernel, mesh=mesh,
                       out_type=jax.ShapeDtypeStruct((n_idx, dw), table.dtype),
                       scratch_types=[pltpu.VMEM((per_w,), I32),
                                      pltpu.VMEM((window, dw), table.dtype),
                                      pltpu.VMEM((window, dw), table.dtype),
                                      pltpu.SemaphoreType.DMA, pltpu.SemaphoreType.DMA,
                                      pltpu.SemaphoreType.DMA, pltpu.SemaphoreType.DMA])
    def gather(x_hbm, i_hbm, o_hbm, idx_v, rows0, rows1, gs0, gs1, os0, os1):
        wid = lax.axis_index("subcore") * mesh.num_cores + lax.axis_index("core")
        base = wid * per_w
        pltpu.sync_copy(i_hbm.at[pl.ds(base, per_w)], idx_v)
        bufs = ((rows0, gs0, os0), (rows1, gs1, os1))

        def fetch(j, slot):
            rows, gs, _ = bufs[slot]
            off = pl.multiple_of(j * window, window)
            return pltpu.async_copy(x_hbm.at[idx_v.at[pl.ds(off, window)]], rows, gs)

        def fetch_wait(slot):
            rows, gs, _ = bufs[slot]
            pltpu.make_async_copy(x_hbm.at[idx_v.at[pl.ds(0, window)]], rows, gs).wait()

        def put(j, slot):
            rows, _, osem = bufs[slot]
            off = pl.multiple_of(j * window, window)
            return pltpu.async_copy(rows, o_hbm.at[pl.ds(base + off, window)], osem)

        def put_wait(slot):
            rows, _, osem = bufs[slot]
            pltpu.make_async_copy(rows, o_hbm.at[pl.ds(base, window)], osem).wait()

        fetch(0, 0)

        @pl.loop(0, n_chunks, step=2)
        def _(j):
            @pl.when(j > 0)
            def _():
                put_wait(1)
            fetch(j + 1, 1)
            fetch_wait(0)
            put(j, 0)
            put_wait(0)

            @pl.when(j + 2 < n_chunks)
            def _():
                fetch(j + 2, 0)
            fetch_wait(1)
            put(j + 1, 1)

        put_wait(1)

    return gather(table, idx)


COMBINE_SLOTS = 3


def _combine_kernel(gate_ref, x1_ref, h2_ref, mod_ref, gain_ref, sw13_ref, sw2_ref, yg_hbm, o_ref,
                    ybuf, ysem):
    i = pl.program_id(0)
    nt = pl.num_programs(0)
    tc = x1_ref.shape[0]

    def y_copy(j):
        slot = j % COMBINE_SLOTS
        rows = pl.ds(pl.multiple_of(j * tc, tc), tc)
        return pltpu.make_async_copy(yg_hbm.at[:, rows, :], ybuf.at[slot], ysem.at[slot])

    @pl.when(i == 0)
    def _():
        for j in range(COMBINE_SLOTS - 1):
            @pl.when(j < nt)
            def _():
                y_copy(j).start()

    @pl.when(i + COMBINE_SLOTS - 1 < nt)
    def _():
        y_copy(i + COMBINE_SLOTS - 1).start()

    y_copy(i).wait()
    yg_ref = ybuf.at[i % COMBINE_SLOTS]

    ds_ = sw2_ref.shape[0]
    hb = _unpack_rows(h2_ref[...]).astype(BF16)
    u = jnp.dot(hb, sw13_ref[...], preferred_element_type=F32)
    hmid = _silu(u[:, :ds_]) * u[:, ds_:]
    y = jnp.dot(hmid.astype(BF16), sw2_ref[...], preferred_element_type=F32)

    gate = gate_ref[...]
    moe = gate[:, 0:1] * _unpack_rows(yg_ref[0])
    for k in range(1, TOP_K):
        moe = moe + gate[:, k:k + 1] * _unpack_rows(yg_ref[k])
    xo = x1_ref[...] + mod_ref[0, 5:6, :] * (moe + y)
    ms = jnp.mean(xo * xo, axis=-1, keepdims=True)
    o_ref[...] = xo * lax.rsqrt(ms + EPS) * gain_ref[...]


def _combine(gate_tok, x1, h2, mods, gain, sw13, sw2, yg, tc, tiles_per_batch):
    n, d = x1.shape
    dw = h2.shape[1]
    nt = n // tc
    ds2 = sw13.shape[1]
    ds_ = sw2.shape[0]
    return pl.pallas_call(
        _combine_kernel,
        out_shape=jax.ShapeDtypeStruct((n, d), F32),
        grid=(nt,),
        in_specs=[pl.BlockSpec((tc, TOP_K), lambda i: (i, 0)),
                  pl.BlockSpec((tc, d), lambda i: (i, 0)),
                  pl.BlockSpec((tc, dw), lambda i: (i, 0)),
                  pl.BlockSpec((1, 8, d), lambda i: (i // tiles_per_batch, 0, 0)),
                  pl.BlockSpec((1, d), lambda i: (0, 0)),
                  pl.BlockSpec((d, ds2), lambda i: (0, 0)),
                  pl.BlockSpec((ds_, d), lambda i: (0, 0)),
                  pl.BlockSpec(memory_space=pl.ANY)],
        out_specs=pl.BlockSpec((tc, d), lambda i: (i, 0)),
        scratch_shapes=[pltpu.VMEM((COMBINE_SLOTS, TOP_K, tc, dw), U32),
                        pltpu.SemaphoreType.DMA((COMBINE_SLOTS,))],
        compiler_params=_cparams(("arbitrary",)),
        name="combine",
    )(gate_tok, x1, h2, mods, gain, sw13, sw2, yg)


def _prep_w_in(w, lru_w, hg_w):
    z0 = 2 * lru_w + 2 * hg_w
    col = jnp.arange(w.shape[1])
    scale = jnp.where((col >= z0) & (col < z0 + 2 * hg_w), 0.5, 1.0).astype(w.dtype)
    return (w * scale[None, :]).astype(BF16)


def _block_diag_pairs(w, per):
    nb, bd, _ = w.shape
    w = w.reshape(nb // per, per, bd, bd)
    eye = jnp.eye(per, dtype=w.dtype)
    out = jnp.einsum("gpij,pq->gpiqj", w, eye)
    return out.reshape(nb // per, per * bd, per * bd)


def kernel(x, c, ctx, c_ctx, ada_w, ada_b, norm_mix, norm_ffn, norm_final, w_in, w_out,
           lru_conv_w, lru_conv_b, lru_wa, lru_ba, lru_wx, lru_bx, lru_lambda,
           hgrn_lb_logits, hgrn_norm, router_w, router_b, exp_w13, exp_w2, shared_w13, shared_w2):
    assert ada_w.shape[0] == 1, "single-layer block"
    b, s, d = x.shape
    n = b * s
    lru_w = lru_conv_w.shape[2]
    n_experts = router_w.shape[2]

    rows = -(-(b + 1) // SUBLANES) * SUBLANES
    cs = jnp.zeros((rows, d), F32).at[:b].set(c).at[b].set(c_ctx)
    mod = _ada(cs, ada_w[0], ada_b[0][None, :]).reshape(rows, 6, d)
    mods = jnp.pad(mod, ((0, 0), (0, 2), (0, 0)))

    w_in_bf = _prep_w_in(w_in[0], lru_w, hgrn_norm.shape[1])
    gain1 = norm_mix[0][None, :]
    projx = _inproj(x, mods, gain1, w_in_bf, PROJ_TILE, shared_mod=False)
    projc = _inproj(ctx, mods[b:b + 1], gain1, w_in_bf, ctx.shape[1], shared_mod=True)

    per = LRU_G // (lru_w // LRU_BLOCKS)
    wg = jnp.stack([_block_diag_pairs(lru_wa[0, 0], per), _block_diag_pairs(lru_wx[0, 0], per),
                    _block_diag_pairs(lru_wa[0, 1], per), _block_diag_pairs(lru_wx[0, 1], per)]
                   )
    wg = (0.5 * wg).astype(BF16)
    bg = 0.5 * jnp.stack([lru_ba[0, 0], lru_bx[0, 0], lru_ba[0, 1], lru_bx[0, 1]])
    ylru = _lru(projx, projc, lru_conv_w[0], lru_conv_b[0][None, :], wg, bg, lru_lambda[0])

    yhg = _hgrn(projx, projc, hgrn_lb_logits, hgrn_norm[0][None, :], (2 * lru_w) // LANES)

    wo = w_out[0].astype(BF16).reshape(2, lru_w, d)
    x1, h2, eid_t, gate_t, rank_t, cnt = _outproj(
        ylru, yhg, x, mods, norm_ffn[0][None, :], wo,
        router_w[0].T.astype(BF16), router_b[0][:, None], PROJ_TILE)

    bm = GMM_BLOCK
    counts = cnt[:, 0]
    padded = (counts + bm - 1) // bm * bm
    pends = jnp.cumsum(padded)
    pstarts = pends - padded
    pos = _pos(pstarts, eid_t, rank_t, POS_TILE)
    cap = n * TOP_K + n_experts * bm

    xs = _padfill(_sc_scatter_rows(h2, pos, cap), pstarts + counts, padded - counts, bm)
    ys = _gmm(xs, exp_w13[0], exp_w2[0], pstarts // bm, padded // bm, bm)

    yg = _sc_gather_rows(ys, pos.reshape(-1), SC_GATHER_WINDOW).reshape(TOP_K, n, d // 2)
    tc = COMBINE_TILE
    out = _combine(gate_t.T, x1.reshape(n, d), h2, mods,
                   norm_final[None, :], shared_w13[0].astype(BF16), shared_w2[0].astype(BF16),
                   yg, tc, s // tc)
    return out.reshape(b, s, d)
```
